```python
import math
import jax, jax.numpy as jnp
from jax import lax
import numpy as np

D_MODEL = 2048
BATCH = 8
SEQ = 8192
DEPTH = 1

HEAD_DIM = 128
FOX_HEADS = D_MODEL // (2 * HEAD_DIM)
GDN_HEADS = D_MODEL // (2 * HEAD_DIM)
FOX_W = FOX_HEADS * HEAD_DIM
GDN_W = GDN_HEADS * HEAD_DIM
MIX_W = FOX_W + GDN_W
CONV_W = 4
GDN_CHUNK = 64
Q_BLOCK = 128
D_FF = 256 * ((8 * D_MODEL // 3 + 255) // 256)
N_MOD = 9
MACARON_W = 0.5
EPS = 1e-6

_SIZES = [FOX_W, FOX_W, FOX_W, FOX_HEADS,
          3 * GDN_W, GDN_HEADS, GDN_HEADS, GDN_W]
IN_W = sum(_SIZES)
SPLIT_IDX = tuple(int(v) for v in np.cumsum(_SIZES)[:-1])

kernel_name = "hybrid_fox_gdn_macaron_adaln"


def rmsnorm(x, g):
    xf = x.astype(jnp.float32)
    y = xf * lax.rsqrt(jnp.mean(xf * xf, axis=-1, keepdims=True) + EPS)
    return y * g.astype(jnp.float32)


def ada_in(x, g, shift, scale):
    y = rmsnorm(x, g) * (1.0 + scale[:, None, :].astype(jnp.float32)) + shift[:, None, :].astype(jnp.float32)
    return y.astype(x.dtype)


def swiglu(h, wg, wu, wd):
    return (jax.nn.silu(h @ wg) * (h @ wu)) @ wd


def l2norm(t):
    return t * lax.rsqrt(jnp.sum(t * t, axis=-1, keepdims=True) + EPS)


def causal_dwconv(x, w):
    k = w.shape[0]
    return lax.conv_general_dilated(
        x, w[:, None, :].astype(x.dtype), window_strides=(1,), padding=[(k - 1, 0)],
        dimension_numbers=("NWC", "WIO", "NWC"), feature_group_count=x.shape[-1])


def forgetting_attention(q, k, v, logf):
    b, h, s, d = q.shape
    nb = s // Q_BLOCK
    scale = 1.0 / math.sqrt(d)
    cum = jnp.cumsum(logf, axis=-1)
    qb = jnp.moveaxis(q.reshape(b, h, nb, Q_BLOCK, d), 2, 0)
    cb = jnp.moveaxis(cum.reshape(b, h, nb, Q_BLOCK), 2, 0)
    kpos = jnp.arange(s)

    def one_block(args):
        i, q_i, c_i = args
        logits = jnp.einsum("bhqd,bhkd->bhqk", q_i, k) * scale + c_i[..., None] - cum[:, :, None, :]
        qpos = i * Q_BLOCK + jnp.arange(Q_BLOCK)
        mask = kpos[None, :] <= qpos[:, None]
        p = jax.nn.softmax(jnp.where(mask, logits, -jnp.inf), axis=-1)
        return jnp.einsum("bhqk,bhkd->bhqd", p, v)

    o = lax.map(one_block, (jnp.arange(nb), qb, cb))
    return jnp.moveaxis(o, 0, 2).reshape(b, h, s, d)


def gated_delta_rule(q, k, v, g, beta):
    b, h, s, dk = q.shape
    dv = v.shape[-1]
    c = GDN_CHUNK
    n = s // c
    q = q * (dk ** -0.5)
    kb = k * beta[..., None]
    vb = v * beta[..., None]
    resh = lambda t: t.reshape(b, h, n, c, *t.shape[3:])
    q, k, kb, vb, g = resh(q), resh(k), resh(kb), resh(vb), resh(g)
    g = jnp.cumsum(g, axis=-1)
    incl = jnp.tril(jnp.ones((c, c), dtype=bool))
    strict = jnp.tril(jnp.ones((c, c), dtype=bool), -1)
    decay = jnp.exp(jnp.where(incl, g[..., :, None] - g[..., None, :], -jnp.inf))
    lower = jnp.where(strict, jnp.einsum("bhnid,bhnjd->bhnij", kb, k) * decay, 0.0)
    eye = jnp.eye(c, dtype=q.dtype)
    t_inv = lax.linalg.triangular_solve(eye + lower, jnp.broadcast_to(eye, lower.shape),
                                        left_side=True, lower=True, unit_diagonal=True)
    u = t_inv @ vb
    w = t_inv @ (kb * jnp.exp(g)[..., None])
    a_intra = jnp.where(incl, jnp.einsum("bhnid,bhnjd->bhnij", q, k) * decay, 0.0)

    def step(state, inp):
        q_i, k_i, u_i, w_i, g_i, a_i = inp
        v_new = u_i - w_i @ state
        o = (q_i * jnp.exp(g_i)[..., None]) @ state + a_i @ v_new
        g_last = g_i[..., -1]
        state = state * jnp.exp(g_last)[..., None, None] + jnp.einsum(
            "bhcd,bhce->bhde", k_i * jnp.exp(g_last[..., None] - g_i)[..., None], v_new)
        return state, o

    mv = lambda t: jnp.moveaxis(t, 2, 0)
    state0 = jnp.zeros((b, h, dk, dv), dtype=q.dtype)
    _, o = lax.scan(step, state0, (mv(q), mv(k), mv(u), mv(w), mv(g), mv(a_intra)))
    return jnp.moveaxis(o, 0, 2).reshape(b, h, s, dv)


def hybrid_mixer(h, w_in, w_out, fox_f_bias, fox_out_norm, gdn_conv, gdn_A_log, gdn_dt_bias, gdn_out_norm):
    bsz, s, _ = h.shape
    proj = (h @ w_in).astype(jnp.float32)
    q_f, k_f, v_f, f_f, qkv_g, a_g, b_g, z_g = jnp.split(proj, SPLIT_IDX, axis=-1)
    heads = lambda t, nh: t.reshape(bsz, s, nh, HEAD_DIM).transpose(0, 2, 1, 3)

    logf = jax.nn.log_sigmoid(f_f + fox_f_bias.astype(jnp.float32)).transpose(0, 2, 1)
    o_f = forgetting_attention(heads(q_f, FOX_HEADS), heads(k_f, FOX_HEADS), heads(v_f, FOX_HEADS), logf)
    o_f = rmsnorm(o_f.transpose(0, 2, 1, 3), fox_out_norm).reshape(bsz, s, FOX_W)

    qkv_g = jax.nn.silu(causal_dwconv(qkv_g, gdn_conv.astype(jnp.float32)))
    q_g, k_g, v_g = jnp.split(qkv_g, 3, axis=-1)
    q_g = l2norm(heads(q_g, GDN_HEADS))
    k_g = l2norm(heads(k_g, GDN_HEADS))
    v_g = heads(v_g, GDN_HEADS)
    g_log = (-jnp.exp(gdn_A_log.astype(jnp.float32))
             * jax.nn.softplus(a_g + gdn_dt_bias.astype(jnp.float32))).transpose(0, 2, 1)
    beta = jax.nn.sigmoid(b_g).transpose(0, 2, 1)
    o_g = gated_delta_rule(q_g, k_g, v_g, g_log, beta).transpose(0, 2, 1, 3)
    z = z_g.reshape(bsz, s, GDN_HEADS, HEAD_DIM)
    o_g = (rmsnorm(o_g, gdn_out_norm) * jax.nn.silu(z)).reshape(bsz, s, GDN_W)

    o = jnp.concatenate([o_f, o_g], axis=-1).astype(h.dtype)
    return o @ w_out


def _fwd_setup_inputs(seed: int = 0) -> dict:
    key = jax.random.key(seed)
    ks = jax.random.split(key, 20)
    nrm = lambda k, shape, s: jax.random.normal(k, shape, jnp.float32) * s
    x = nrm(ks[0], (BATCH, SEQ, D_MODEL), 1.0)
    c = nrm(ks[1], (BATCH, D_MODEL), 1.0)
    ada_w = nrm(ks[2], (DEPTH, D_MODEL, N_MOD * D_MODEL), 0.5 * D_MODEL ** -0.5)
    ada_b = nrm(ks[3], (DEPTH, N_MOD * D_MODEL), 0.1)
    norm_g = 1.0 + nrm(ks[4], (DEPTH, 3, D_MODEL), 0.1)
    ffn_w_gate = nrm(ks[5], (DEPTH, 2, D_MODEL, D_FF), D_MODEL ** -0.5)
    ffn_w_up = nrm(ks[6], (DEPTH, 2, D_MODEL, D_FF), D_MODEL ** -0.5)
    ffn_w_down = nrm(ks[7], (DEPTH, 2, D_FF, D_MODEL), D_FF ** -0.5)
    w_in = nrm(ks[8], (DEPTH, D_MODEL, IN_W), D_MODEL ** -0.5)
    w_out = nrm(ks[9], (DEPTH, MIX_W, D_MODEL), MIX_W ** -0.5)
    fox_f_bias = 3.0 + nrm(ks[10], (DEPTH, FOX_HEADS), 0.5)
    fox_out_norm = 1.0 + nrm(ks[11], (DEPTH, HEAD_DIM), 0.1)
    gdn_conv = nrm(ks[12], (DEPTH, CONV_W, 3 * GDN_W), CONV_W ** -0.5)
    gdn_A_log = jnp.log(jax.random.uniform(ks[13], (DEPTH, GDN_HEADS), jnp.float32, 1.0, 16.0))
    dt = jnp.exp(jax.random.uniform(ks[14], (DEPTH, GDN_HEADS), jnp.float32,
                                    math.log(1e-3), math.log(1e-1)))
    gdn_dt_bias = dt + jnp.log(-jnp.expm1(-dt))
    gdn_out_norm = 1.0 + nrm(ks[15], (DEPTH, HEAD_DIM), 0.1)
    final_norm = 1.0 + nrm(ks[16], (D_MODEL,), 0.1)
    return {"x": x, "c": c, "ada_w": ada_w, "ada_b": ada_b, "norm_g": norm_g,
            "ffn_w_gate": ffn_w_gate, "ffn_w_up": ffn_w_up, "ffn_w_down": ffn_w_down,
            "w_in": w_in, "w_out": w_out, "fox_f_bias": fox_f_bias, "fox_out_norm": fox_out_norm,
            "gdn_conv": gdn_conv, "gdn_A_log": gdn_A_log, "gdn_dt_bias": gdn_dt_bias,
            "gdn_out_norm": gdn_out_norm, "final_norm": final_norm}


def _fwd_reference(x, c, ada_w, ada_b, norm_g, ffn_w_gate, ffn_w_up, ffn_w_down, w_in, w_out,
              fox_f_bias, fox_out_norm, gdn_conv, gdn_A_log, gdn_dt_bias, gdn_out_norm, final_norm):
    cond = jax.nn.silu(c)
    for l in range(DEPTH):
        mod = cond @ ada_w[l] + ada_b[l]
        sh1, sc1, gt1, sh2, sc2, gt2, sh3, sc3, gt3 = jnp.split(mod, N_MOD, axis=-1)
        h = ada_in(x, norm_g[l, 0], sh1, sc1)
        x = x + MACARON_W * gt1[:, None, :] * swiglu(h, ffn_w_gate[l, 0], ffn_w_up[l, 0], ffn_w_down[l, 0])
        h = ada_in(x, norm_g[l, 1], sh2, sc2)
        x = x + gt2[:, None, :] * hybrid_mixer(h, w_in[l], w_out[l], fox_f_bias[l], fox_out_norm[l],
                                               gdn_conv[l], gdn_A_log[l], gdn_dt_bias[l], gdn_out_norm[l])
        h = ada_in(x, norm_g[l, 2], sh3, sc3)
        x = x + MACARON_W * gt3[:, None, :] * swiglu(h, ffn_w_gate[l, 1], ffn_w_up[l, 1], ffn_w_down[l, 1])
    return rmsnorm(x, final_norm).astype(x.dtype)


import jax as _jax
import jax.numpy as _jnp

TWIN_FORMAT = 'train_step'
FWD_PARAMS = ['x', 'c', 'ada_w', 'ada_b', 'norm_g', 'ffn_w_gate', 'ffn_w_up', 'ffn_w_down', 'w_in', 'w_out', 'fox_f_bias', 'fox_out_norm', 'gdn_conv', 'gdn_A_log', 'gdn_dt_bias', 'gdn_out_norm', 'final_norm']
TWIN_WEIGHTS = ['ada_w', 'ada_b', 'norm_g', 'ffn_w_gate', 'ffn_w_up', 'ffn_w_down', 'w_in', 'w_out', 'fox_f_bias', 'fox_out_norm', 'gdn_conv', 'gdn_A_log', 'gdn_dt_bias', 'gdn_out_norm', 'final_norm']
TWIN_DIFF_INPUT = 'x'
TWIN_INPUTS = ['x', 'c', 'ada_w', 'ada_b', 'norm_g', 'ffn_w_gate', 'ffn_w_up', 'ffn_w_down', 'w_in', 'w_out', 'fox_f_bias', 'fox_out_norm', 'gdn_conv', 'gdn_A_log', 'gdn_dt_bias', 'gdn_out_norm', 'final_norm', 'loss_target', 'm_ada_w', 'm_ada_b', 'm_norm_g', 'm_ffn_w_gate', 'm_ffn_w_up', 'm_ffn_w_down', 'm_w_in', 'm_w_out', 'm_fox_f_bias', 'm_fox_out_norm', 'm_gdn_conv', 'm_gdn_A_log', 'm_gdn_dt_bias', 'm_gdn_out_norm', 'm_final_norm', 'v_ada_w', 'v_ada_b', 'v_norm_g', 'v_ffn_w_gate', 'v_ffn_w_up', 'v_ffn_w_down', 'v_w_in', 'v_w_out', 'v_fox_f_bias', 'v_fox_out_norm', 'v_gdn_conv', 'v_gdn_A_log', 'v_gdn_dt_bias', 'v_gdn_out_norm', 'v_final_norm']
TWIN_OUTPUTS = ['loss', 'grad_x', 'grad_ada_w', 'grad_ada_b', 'grad_norm_g', 'grad_ffn_w_gate', 'grad_ffn_w_up', 'grad_ffn_w_down', 'grad_w_in', 'grad_w_out', 'grad_fox_f_bias', 'grad_fox_out_norm', 'grad_gdn_conv', 'grad_gdn_A_log', 'grad_gdn_dt_bias', 'grad_gdn_out_norm', 'grad_final_norm', 'delta_ada_w', 'delta_ada_b', 'delta_norm_g', 'delta_ffn_w_gate', 'delta_ffn_w_up', 'delta_ffn_w_down', 'delta_w_in', 'delta_w_out', 'delta_fox_f_bias', 'delta_fox_out_norm', 'delta_gdn_conv', 'delta_gdn_A_log', 'delta_gdn_dt_bias', 'delta_gdn_out_norm', 'delta_final_norm', 'new_m_ada_w', 'new_m_ada_b', 'new_m_norm_g', 'new_m_ffn_w_gate', 'new_m_ffn_w_up', 'new_m_ffn_w_down', 'new_m_w_in', 'new_m_w_out', 'new_m_fox_f_bias', 'new_m_fox_out_norm', 'new_m_gdn_conv', 'new_m_gdn_A_log', 'new_m_gdn_dt_bias', 'new_m_gdn_out_norm', 'new_m_final_norm', 'new_v_ada_w', 'new_v_ada_b', 'new_v_norm_g', 'new_v_ffn_w_gate', 'new_v_ffn_w_up', 'new_v_ffn_w_down', 'new_v_w_in', 'new_v_w_out', 'new_v_fox_f_bias', 'new_v_fox_out_norm', 'new_v_gdn_conv', 'new_v_gdn_A_log', 'new_v_gdn_dt_bias', 'new_v_gdn_out_norm', 'new_v_final_norm']
TWIN_LEAF_KINDS = {'loss': 'loss', 'grad_x': 'grad_x', 'grad_ada_w': 'grad_w', 'grad_ada_b': 'grad_w', 'grad_norm_g': 'grad_w', 'grad_ffn_w_gate': 'grad_w', 'grad_ffn_w_up': 'grad_w', 'grad_ffn_w_down': 'grad_w', 'grad_w_in': 'grad_w', 'grad_w_out': 'grad_w', 'grad_fox_f_bias': 'grad_w', 'grad_fox_out_norm': 'grad_w', 'grad_gdn_conv': 'grad_w', 'grad_gdn_A_log': 'grad_w', 'grad_gdn_dt_bias': 'grad_w', 'grad_gdn_out_norm': 'grad_w', 'grad_final_norm': 'grad_w', 'delta_ada_w': 'delta_w', 'delta_ada_b': 'delta_w', 'delta_norm_g': 'delta_w', 'delta_ffn_w_gate': 'delta_w', 'delta_ffn_w_up': 'delta_w', 'delta_ffn_w_down': 'delta_w', 'delta_w_in': 'delta_w', 'delta_w_out': 'delta_w', 'delta_fox_f_bias': 'delta_w', 'delta_fox_out_norm': 'delta_w', 'delta_gdn_conv': 'delta_w', 'delta_gdn_A_log': 'delta_w', 'delta_gdn_dt_bias': 'delta_w', 'delta_gdn_out_norm': 'delta_w', 'delta_final_norm': 'delta_w', 'new_m_ada_w': 'new_m', 'new_m_ada_b': 'new_m', 'new_m_norm_g': 'new_m', 'new_m_ffn_w_gate': 'new_m', 'new_m_ffn_w_up': 'new_m', 'new_m_ffn_w_down': 'new_m', 'new_m_w_in': 'new_m', 'new_m_w_out': 'new_m', 'new_m_fox_f_bias': 'new_m', 'new_m_fox_out_norm': 'new_m', 'new_m_gdn_conv': 'new_m', 'new_m_gdn_A_log': 'new_m', 'new_m_gdn_dt_bias': 'new_m', 'new_m_gdn_out_norm': 'new_m', 'new_m_final_norm': 'new_m', 'new_v_ada_w': 'new_v', 'new_v_ada_b': 'new_v', 'new_v_norm_g': 'new_v', 'new_v_ffn_w_gate': 'new_v', 'new_v_ffn_w_up': 'new_v', 'new_v_ffn_w_down': 'new_v', 'new_v_w_in': 'new_v', 'new_v_w_out': 'new_v', 'new_v_fox_f_bias': 'new_v', 'new_v_fox_out_norm': 'new_v', 'new_v_gdn_conv': 'new_v', 'new_v_gdn_A_log': 'new_v', 'new_v_gdn_dt_bias': 'new_v', 'new_v_gdn_out_norm': 'new_v', 'new_v_final_norm': 'new_v'}


def _forward(args):
    return _fwd_reference(*[args[k] for k in FWD_PARAMS])


def _output_shape():
    def fwd():
        inp = _fwd_setup_inputs(0)
        return _fwd_reference(*[inp[k] for k in FWD_PARAMS])
    out = _jax.eval_shape(fwd)
    return out.shape, out.dtype

N_MICROBATCH = 1
ADAM_LR = 0.001
ADAM_B1 = 0.9
ADAM_B2 = 0.999
ADAM_EPS = 1e-08
ADAM_WD = 0.01
ADAM_STEP = 10
PER_EXAMPLE_BATCH_AXIS = {'x': 0, 'c': 0, 'loss_target': 0}
SHARED_INPUTS = []
_WEIGHT_DTYPES = {'ada_w': _jnp.float32, 'ada_b': _jnp.float32, 'norm_g': _jnp.float32, 'ffn_w_gate': _jnp.float32, 'ffn_w_up': _jnp.float32, 'ffn_w_down': _jnp.float32, 'w_in': _jnp.float32, 'w_out': _jnp.float32, 'fox_f_bias': _jnp.float32, 'fox_out_norm': _jnp.float32, 'gdn_conv': _jnp.float32, 'gdn_A_log': _jnp.float32, 'gdn_dt_bias': _jnp.float32, 'gdn_out_norm': _jnp.float32, 'final_norm': _jnp.float32}
MOMENT_SCALE = {'ada_w': 9.838970e-02, 'ada_b': 2.280943e-01, 'norm_g': 3.058498e-02, 'ffn_w_gate': 9.295050e-03, 'ffn_w_up': 9.203250e-03, 'ffn_w_down': 1.528719e-02, 'w_in': 4.134547e-02, 'w_out': 7.610859e-02, 'fox_f_bias': 2.167285e-01, 'fox_out_norm': 3.057534e-01, 'gdn_conv': 2.347673e-02, 'gdn_A_log': 1.446096e-01, 'gdn_dt_bias': 1.383370e-01, 'gdn_out_norm': 1.263280e-01, 'final_norm': 3.220356e+01}


def _to_microbatches(a, axis):
    t = _jnp.moveaxis(a, axis, 0)
    t = t.reshape((N_MICROBATCH, t.shape[0] // N_MICROBATCH) + t.shape[1:])
    return _jnp.moveaxis(t, 1, axis + 1)


def setup_inputs(seed: int = 0) -> dict:
    inp = _fwd_setup_inputs(seed)
    key = _jax.random.fold_in(_jax.random.key(seed), 7919)
    shape, _ = _output_shape()
    out = dict(inp)
    out["loss_target"] = _jax.random.normal(_jax.random.fold_in(key, 0), shape, _jnp.float32)
    for i, name in enumerate(TWIN_WEIGHTS):
        w = inp[name].astype(_jnp.float32)
        if MOMENT_SCALE is None:
            s = _jnp.sqrt(_jnp.mean(_jnp.square(w)) + 1e-30)
        else:
            s = MOMENT_SCALE[name]
        km, kv = _jax.random.split(_jax.random.fold_in(key, i + 1))
        out[name] = w
        out["m_" + name] = s * _jax.random.normal(km, w.shape, _jnp.float32)
        out["v_" + name] = (s * s) * _jax.random.uniform(kv, w.shape, _jnp.float32, 0.5, 1.5)
    if N_MICROBATCH > 1:
        for name, axis in PER_EXAMPLE_BATCH_AXIS.items():
            out[name] = _to_microbatches(out[name], axis)
    return {'x': out['x'], 'c': out['c'], 'ada_w': out['ada_w'], 'ada_b': out['ada_b'], 'norm_g': out['norm_g'], 'ffn_w_gate': out['ffn_w_gate'], 'ffn_w_up': out['ffn_w_up'], 'ffn_w_down': out['ffn_w_down'], 'w_in': out['w_in'], 'w_out': out['w_out'], 'fox_f_bias': out['fox_f_bias'], 'fox_out_norm': out['fox_out_norm'], 'gdn_conv': out['gdn_conv'], 'gdn_A_log': out['gdn_A_log'], 'gdn_dt_bias': out['gdn_dt_bias'], 'gdn_out_norm': out['gdn_out_norm'], 'final_norm': out['final_norm'], 'loss_target': out['loss_target'], 'm_ada_w': out['m_ada_w'], 'm_ada_b': out['m_ada_b'], 'm_norm_g': out['m_norm_g'], 'm_ffn_w_gate': out['m_ffn_w_gate'], 'm_ffn_w_up': out['m_ffn_w_up'], 'm_ffn_w_down': out['m_ffn_w_down'], 'm_w_in': out['m_w_in'], 'm_w_out': out['m_w_out'], 'm_fox_f_bias': out['m_fox_f_bias'], 'm_fox_out_norm': out['m_fox_out_norm'], 'm_gdn_conv': out['m_gdn_conv'], 'm_gdn_A_log': out['m_gdn_A_log'], 'm_gdn_dt_bias': out['m_gdn_dt_bias'], 'm_gdn_out_norm': out['m_gdn_out_norm'], 'm_final_norm': out['m_final_norm'], 'v_ada_w': out['v_ada_w'], 'v_ada_b': out['v_ada_b'], 'v_norm_g': out['v_norm_g'], 'v_ffn_w_gate': out['v_ffn_w_gate'], 'v_ffn_w_up': out['v_ffn_w_up'], 'v_ffn_w_down': out['v_ffn_w_down'], 'v_w_in': out['v_w_in'], 'v_w_out': out['v_w_out'], 'v_fox_f_bias': out['v_fox_f_bias'], 'v_fox_out_norm': out['v_fox_out_norm'], 'v_gdn_conv': out['v_gdn_conv'], 'v_gdn_A_log': out['v_gdn_A_log'], 'v_gdn_dt_bias': out['v_gdn_dt_bias'], 'v_gdn_out_norm': out['v_gdn_out_norm'], 'v_final_norm': out['v_final_norm']}


def _loss(weights, diff, rest, loss_target):
    with _jax.named_scope("forward"):
        args = {**rest, TWIN_DIFF_INPUT: diff, **{k: w.astype(_WEIGHT_DTYPES[k]) for k, w in weights.items()}}
        y = _forward(args)
    with _jax.named_scope("loss_head"):
        err = _jnp.square(y.astype(_jnp.float32) - loss_target)
        return 0.5 * _jnp.sum(_jnp.mean(err, axis=-1)) if err.ndim else 0.5 * err


def _adamw(w, g, m, v):
    m = ADAM_B1 * m + (1.0 - ADAM_B1) * g
    v = ADAM_B2 * v + (1.0 - ADAM_B2) * _jnp.square(g)
    m_hat = m / (1.0 - ADAM_B1 ** ADAM_STEP)
    v_hat = v / (1.0 - ADAM_B2 ** ADAM_STEP)
    delta = -ADAM_LR * (m_hat / (_jnp.sqrt(v_hat) + ADAM_EPS) + ADAM_WD * w)
    return delta, m, v


def reference(x, c, ada_w, ada_b, norm_g, ffn_w_gate, ffn_w_up, ffn_w_down, w_in, w_out, fox_f_bias, fox_out_norm, gdn_conv, gdn_A_log, gdn_dt_bias, gdn_out_norm, final_norm, loss_target, m_ada_w, m_ada_b, m_norm_g, m_ffn_w_gate, m_ffn_w_up, m_ffn_w_down, m_w_in, m_w_out, m_fox_f_bias, m_fox_out_norm, m_gdn_conv, m_gdn_A_log, m_gdn_dt_bias, m_gdn_out_norm, m_final_norm, v_ada_w, v_ada_b, v_norm_g, v_ffn_w_gate, v_ffn_w_up, v_ffn_w_down, v_w_in, v_w_out, v_fox_f_bias, v_fox_out_norm, v_gdn_conv, v_gdn_A_log, v_gdn_dt_bias, v_gdn_out_norm, v_final_norm):
    given = dict(x=x, c=c, ada_w=ada_w, ada_b=ada_b, norm_g=norm_g, ffn_w_gate=ffn_w_gate, ffn_w_up=ffn_w_up, ffn_w_down=ffn_w_down, w_in=w_in, w_out=w_out, fox_f_bias=fox_f_bias, fox_out_norm=fox_out_norm, gdn_conv=gdn_conv, gdn_A_log=gdn_A_log, gdn_dt_bias=gdn_dt_bias, gdn_out_norm=gdn_out_norm, final_norm=final_norm, loss_target=loss_target, m_ada_w=m_ada_w, m_ada_b=m_ada_b, m_norm_g=m_norm_g, m_ffn_w_gate=m_ffn_w_gate, m_ffn_w_up=m_ffn_w_up, m_ffn_w_down=m_ffn_w_down, m_w_in=m_w_in, m_w_out=m_w_out, m_fox_f_bias=m_fox_f_bias, m_fox_out_norm=m_fox_out_norm, m_gdn_conv=m_gdn_conv, m_gdn_A_log=m_gdn_A_log, m_gdn_dt_bias=m_gdn_dt_bias, m_gdn_out_norm=m_gdn_out_norm, m_final_norm=m_final_norm, v_ada_w=v_ada_w, v_ada_b=v_ada_b, v_norm_g=v_norm_g, v_ffn_w_gate=v_ffn_w_gate, v_ffn_w_up=v_ffn_w_up, v_ffn_w_down=v_ffn_w_down, v_w_in=v_w_in, v_w_out=v_w_out, v_fox_f_bias=v_fox_f_bias, v_fox_out_norm=v_fox_out_norm, v_gdn_conv=v_gdn_conv, v_gdn_A_log=v_gdn_A_log, v_gdn_dt_bias=v_gdn_dt_bias, v_gdn_out_norm=v_gdn_out_norm, v_final_norm=v_final_norm)
    weights = {n: given[n] for n in TWIN_WEIGHTS}
    shared = {n: given[n] for n in SHARED_INPUTS}
    per_example = {n: given[n] for n in ['x', 'c']}
    grad_fn = _jax.value_and_grad(_loss, argnums=(0, 1))

    def one_microbatch(ex, loss_target):
        ex = dict(ex)
        diff = ex.pop(TWIN_DIFF_INPUT)
        return grad_fn(weights, diff, {**shared, **ex}, loss_target)

    if N_MICROBATCH == 1:
        loss, (grad_w, grad_x) = one_microbatch(per_example, given["loss_target"])
    else:
        def body(carry, xs):
            loss_sum, grad_sum = carry
            l_k, (gw_k, gx_k) = one_microbatch(xs[0], xs[1])
            with _jax.named_scope("update"):
                return (loss_sum + l_k, _jax.tree.map(_jnp.add, grad_sum, gw_k)), gx_k

        init = (_jnp.zeros((), _jnp.float32), _jax.tree.map(_jnp.zeros_like, weights))
        (loss, grad_w), grad_x = _jax.lax.scan(body, init, (per_example, given["loss_target"]))
    with _jax.named_scope("update"):
        delta_w, new_m, new_v = {}, {}, {}
        for n in TWIN_WEIGHTS:
            delta_w[n], new_m[n], new_v[n] = _adamw(weights[n], grad_w[n], given["m_" + n], given["v_" + n])
    return (loss, grad_x, *[grad_w[n] for n in TWIN_WEIGHTS], *[delta_w[n] for n in TWIN_WEIGHTS],
            *[new_m[n] for n in TWIN_WEIGHTS], *[new_v[n] for n in TWIN_WEIGHTS])
```

```python
import functools
import math

import jax
import jax.numpy as jnp
from jax import lax
from jax.experimental import pallas as pl
from jax.experimental.pallas import tpu as pltpu

f32 = jnp.float32
bf16 = jnp.bfloat16
HI = lax.Precision.HIGHEST
MESH = pl.DeviceIdType.MESH

EPS = 1e-6
HEAD_DIM = 128
LANES = 128
GDN_CHUNK = 64
CONV_W = 4
MACARON_W = 0.5
ADAM_LR, ADAM_B1, ADAM_B2, ADAM_EPS, ADAM_WD, ADAM_STEP = 0.001, 0.9, 0.999, 1e-08, 0.01, 10
VMEM_LIMIT_V7X = 56 * 1024 * 1024
NEG = -1e30

NN = (((1,), (0,)), ((), ()))
NT = (((1,), (1,)), ((), ()))
TN = (((0,), (0,)), ((), ()))


def _cp(*sem):
    return pltpu.CompilerParams(dimension_semantics=sem, vmem_limit_bytes=VMEM_LIMIT_V7X)


def _dotb(a, b, dn=NN):
    return lax.dot_general(a.astype(bf16), b.astype(bf16), dn, preferred_element_type=f32)


def _doth(a, b, dn=NN):
    return lax.dot_general(a.astype(f32), b.astype(f32), dn, precision=HI, preferred_element_type=f32)


def _sigmoid(x):
    return 1.0 / (1.0 + jnp.exp(-x))


def _softplus(x):
    return jnp.maximum(x, 0.0) + jnp.log(1.0 + jnp.exp(-jnp.abs(x)))


def _lane_col(blk, lane_idx):
    lane = lax.broadcasted_iota(jnp.int32, blk.shape, 1)
    return jnp.sum(jnp.where(lane == lane_idx, blk, 0.0), axis=1, keepdims=True)


def _tile(n, pref, mult=LANES):
    if n <= pref:
        return n
    t = (pref // mult) * mult
    while t >= mult:
        if n % t == 0:
            return t
        t -= mult
    return n


def _mm(name, groups, mode, tiles, epilogue, out_dtypes, extras=()):
    a0, b0 = groups[0][0]
    if mode == "nn":
        (m, k), n = a0.shape, b0.shape[1]
    elif mode == "nt":
        (m, k), n = a0.shape, b0.shape[0]
    else:
        (k, m), n = a0.shape, b0.shape[1]
    tm, tn, tk = _tile(m, tiles[0]), _tile(n, tiles[1]), _tile(k, tiles[2])
    nk = k // tk
    assert m % tm == 0 and n % tn == 0 and k % tk == 0, (name, m, n, k, tm, tn, tk)
    if mode == "nn":
        a_spec = pl.BlockSpec((tm, tk), lambda i, j, kk: (i, kk))
        b_spec = pl.BlockSpec((tk, tn), lambda i, j, kk: (kk, j))
        dn = NN
    elif mode == "nt":
        a_spec = pl.BlockSpec((tm, tk), lambda i, j, kk: (i, kk))
        b_spec = pl.BlockSpec((tn, tk), lambda i, j, kk: (j, kk))
        dn = NT
    else:
        a_spec = pl.BlockSpec((tk, tm), lambda i, j, kk: (kk, i))
        b_spec = pl.BlockSpec((tk, tn), lambda i, j, kk: (kk, j))
        dn = TN
    npairs = sum(len(g) for g in groups)
    nacc, nex, nout = len(groups), len(extras), len(out_dtypes)
    in_specs, args = [], []
    for g in groups:
        for a, b in g:
            in_specs += [a_spec, b_spec]
            args += [a, b]
    for arr, kind in extras:
        if kind == "mn":
            in_specs.append(pl.BlockSpec((tm, tn), lambda i, j, kk: (i, j)))
        else:
            in_specs.append(pl.BlockSpec((1, tn), lambda i, j, kk: (0, j)))
        args.append(arr)

    def body(*refs):
        ins = refs[: 2 * npairs]
        ex = refs[2 * npairs: 2 * npairs + nex]
        outs = refs[2 * npairs + nex: 2 * npairs + nex + nout]
        accs = refs[2 * npairs + nex + nout:]
        kk = pl.program_id(2)

        @pl.when(kk == 0)
        def _():
            for acc in accs:
                acc[...] = jnp.zeros_like(acc)

        p = 0
        for gi, g in enumerate(groups):
            t = None
            for _ in g:
                d = _dotb(ins[2 * p][...], ins[2 * p + 1][...], dn)
                t = d if t is None else t + d
                p += 1
            accs[gi][...] += t

        @pl.when(kk == nk - 1)
        def _():
            res = epilogue([acc[...] for acc in accs], [e[...] for e in ex])
            for o, r in zip(outs, res):
                o[...] = r.astype(o.dtype)

    return pl.pallas_call(
        body, name=name, grid=(m // tm, n // tn, nk),
        in_specs=in_specs,
        out_specs=[pl.BlockSpec((tm, tn), lambda i, j, kk: (i, j)) for _ in out_dtypes],
        out_shape=[jax.ShapeDtypeStruct((m, n), dt) for dt in out_dtypes],
        scratch_shapes=[pltpu.VMEM((tm, tn), f32) for _ in range(nacc)],
        compiler_params=_cp("parallel", "parallel", "arbitrary"),
    )(*args)


def _ep_plain(accs, ex):
    return (accs[0],)


def _ep_swiglu(accs, ex):
    gate, up = accs
    act = gate * _sigmoid(gate) * up
    return gate, up, act


def _ep_residual(accs, ex):
    x, gs = ex
    y = accs[0]
    return x + gs * y, y


def _ep_swiglu_bwd(accs, ex):
    gate, up = ex[0].astype(f32), ex[1].astype(f32)
    dact = accs[0]
    sg = _sigmoid(gate)
    silu = gate * sg
    act = silu * up
    dup = dact * silu
    dgate = dact * up * sg * (1.0 + gate * (1.0 - sg))
    return act, dgate, dup


def _row_tile(s):
    return _tile(s, 256, 8)


def _ada_in(name, x, g, shift, scale):
    s, d = x.shape
    tm = _row_tile(s)

    def body(x_ref, g_ref, sh_ref, sc_ref, h_ref):
        xv = x_ref[...]
        r = lax.rsqrt(jnp.mean(xv * xv, axis=-1, keepdims=True) + EPS)
        h_ref[...] = (xv * r * g_ref[...] * (1.0 + sc_ref[...]) + sh_ref[...]).astype(h_ref.dtype)

    row = pl.BlockSpec((1, d), lambda i: (0, 0))
    blk = pl.BlockSpec((tm, d), lambda i: (i, 0))
    return pl.pallas_call(body, name=name, grid=(s // tm,), in_specs=[blk, row, row, row], out_specs=blk,
                          out_shape=jax.ShapeDtypeStruct((s, d), bf16), compiler_params=_cp("parallel"))(x, g, shift, scale)


def _ada_bwd(name, x, g, scale, dh, dres):
    s, d = x.shape
    tm = _row_tile(s)

    def body(x_ref, g_ref, sc_ref, dh_ref, dres_ref, dx_ref, dsh_ref, a_ref):
        i = pl.program_id(0)

        @pl.when(i == 0)
        def _():
            dsh_ref[...] = jnp.zeros_like(dsh_ref)
            a_ref[...] = jnp.zeros_like(a_ref)

        xv = x_ref[...]
        dhv = dh_ref[...].astype(f32)
        r = lax.rsqrt(jnp.mean(xv * xv, axis=-1, keepdims=True) + EPS)
        n = xv * r
        dn = dhv * (g_ref[...] * (1.0 + sc_ref[...]))
        dx_ref[...] = dres_ref[...] + r * (dn - n * jnp.mean(dn * n, axis=-1, keepdims=True))
        dsh_ref[...] += jnp.sum(dhv, axis=0, keepdims=True)
        a_ref[...] += jnp.sum(dhv * n, axis=0, keepdims=True)

    row = pl.BlockSpec((1, d), lambda i: (0, 0))
    blk = pl.BlockSpec((tm, d), lambda i: (i, 0))
    return pl.pallas_call(
        body, name=name, grid=(s // tm,), in_specs=[blk, row, row, blk, blk], out_specs=[blk, row, row],
        out_shape=[jax.ShapeDtypeStruct((s, d), f32), jax.ShapeDtypeStruct((1, d), f32), jax.ShapeDtypeStruct((1, d), f32)],
        compiler_params=_cp("arbitrary"))(x, g, scale, dh, dres)


def _gate_bwd(name, dx, y, gs):
    s, d = dx.shape
    tm = _row_tile(s)

    def body(dx_ref, y_ref, gs_ref, dy_ref, dgs_ref):
        i = pl.program_id(0)

        @pl.when(i == 0)
        def _():
            dgs_ref[...] = jnp.zeros_like(dgs_ref)

        dxv = dx_ref[...]
        dy_ref[...] = (dxv * gs_ref[...]).astype(dy_ref.dtype)
        dgs_ref[...] += jnp.sum(dxv * y_ref[...].astype(f32), axis=0, keepdims=True)

    row = pl.BlockSpec((1, d), lambda i: (0, 0))
    blk = pl.BlockSpec((tm, d), lambda i: (i, 0))
    return pl.pallas_call(
        body, name=name, grid=(s // tm,), in_specs=[blk, blk, row], out_specs=[blk, row],
        out_shape=[jax.ShapeDtypeStruct((s, d), bf16), jax.ShapeDtypeStruct((1, d), f32)],
        compiler_params=_cp("arbitrary"))(dx, y, gs)


def _final_loss(x, fg, target):
    s, d = x.shape
    tm = _row_tile(s)

    def body(x_ref, g_ref, t_ref, loss_ref, dx_ref, dg_ref):
        i = pl.program_id(0)

        @pl.when(i == 0)
        def _():
            loss_ref[...] = jnp.zeros_like(loss_ref)
            dg_ref[...] = jnp.zeros_like(dg_ref)

        xv = x_ref[...]
        gv = g_ref[...]
        r = lax.rsqrt(jnp.mean(xv * xv, axis=-1, keepdims=True) + EPS)
        n = xv * r
        e = n * gv - t_ref[...]
        per_tok = jnp.mean(e * e, axis=-1, keepdims=True)
        loss_ref[...] += 0.5 * jnp.sum(per_tok, axis=0, keepdims=True) * jnp.ones((1, LANES), f32)
        dy = e * (1.0 / d)
        dg_ref[...] += jnp.sum(dy * n, axis=0, keepdims=True)
        dn = dy * gv
        dx_ref[...] = r * (dn - n * jnp.mean(dn * n, axis=-1, keepdims=True))

    row = pl.BlockSpec((1, d), lambda i: (0, 0))
    blk = pl.BlockSpec((tm, d), lambda i: (i, 0))
    return pl.pallas_call(
        body, name="final_loss", grid=(s // tm,), in_specs=[blk, row, blk],
        out_specs=[pl.BlockSpec((1, LANES), lambda i: (0, 0)), blk, row],
        out_shape=[jax.ShapeDtypeStruct((1, LANES), f32), jax.ShapeDtypeStruct((s, d), f32), jax.ShapeDtypeStruct((1, d), f32)],
        compiler_params=_cp("arbitrary"))(x, fg, target)


def _small_fwd(ps, prm, nh):
    s = ps.shape[0]
    tb = LANES

    def body(ps_ref, prm_ref, sm_ref, cum_ref, cumt_ref, carry):
        i = pl.program_id(0)

        @pl.when(i == 0)
        def _():
            carry[...] = jnp.zeros_like(carry)

        x = ps_ref[...]
        lane = lax.broadcasted_iota(jnp.int32, x.shape, 1)
        fb, dtb, alog = prm_ref[0:1, :], prm_ref[1:2, :], prm_ref[2:3, :]
        logf = -_softplus(-(x + fb))
        glog = -jnp.exp(alog) * _softplus(x + dtb)
        beta = _sigmoid(x)
        sm = jnp.where(lane < nh, logf, jnp.where(lane < 2 * nh, glog, jnp.where(lane < 3 * nh, beta, 0.0)))
        sm_ref[...] = sm
        r = lax.broadcasted_iota(jnp.int32, (tb, tb), 0)
        c = lax.broadcasted_iota(jnp.int32, (tb, tb), 1)
        tril = (c <= r).astype(f32)
        cs = _doth(tril, sm) + carry[...]
        cum_ref[...] = cs
        cumt_ref[...] = cs.T[:8, :]
        carry[...] = cs[tb - 1:tb, :]

    blk = pl.BlockSpec((tb, LANES), lambda i: (i, 0))
    return pl.pallas_call(
        body, name="small_fwd", grid=(s // tb,),
        in_specs=[blk, pl.BlockSpec((8, LANES), lambda i: (0, 0))],
        out_specs=[blk, blk, pl.BlockSpec((8, tb), lambda i: (0, i))],
        out_shape=[jax.ShapeDtypeStruct((s, LANES), f32), jax.ShapeDtypeStruct((s, LANES), f32), jax.ShapeDtypeStruct((8, s), f32)],
        scratch_shapes=[pltpu.VMEM((1, LANES), f32)],
        compiler_params=_cp("arbitrary"))(ps, prm)


def _small_bwd(ps, prm, dsm, dcum_t, nh):
    s = ps.shape[0]
    tb = LANES
    nb = s // tb

    def body(ps_ref, prm_ref, dsm_ref, dct_ref, dps_ref, pg_ref, carry):
        i = pl.program_id(0)

        @pl.when(i == 0)
        def _():
            carry[...] = jnp.zeros_like(carry)
            pg_ref[...] = jnp.zeros_like(pg_ref)

        x = ps_ref[...]
        dsm = dsm_ref[...]
        lane = lax.broadcasted_iota(jnp.int32, x.shape, 1)
        fb, dtb, alog = prm_ref[0:1, :], prm_ref[1:2, :], prm_ref[2:3, :]
        sel = (lax.broadcasted_iota(jnp.int32, (8, LANES), 0) == lax.broadcasted_iota(jnp.int32, (8, LANES), 1)).astype(f32)
        dcum = _doth(dct_ref[...], sel, TN)
        r = lax.broadcasted_iota(jnp.int32, (tb, tb), 0)
        c = lax.broadcasted_iota(jnp.int32, (tb, tb), 1)
        triu = (c >= r).astype(f32)
        dlogf = _doth(triu, dcum) + carry[...]
        carry[...] = dlogf[0:1, :]
        d_f = dlogf * _sigmoid(-(x + fb))
        nega = -jnp.exp(alog)
        xa = x + dtb
        glog = nega * _softplus(xa)
        d_a = dsm * nega * _sigmoid(xa)
        beta = _sigmoid(x)
        d_b = dsm * beta * (1.0 - beta)
        dps = jnp.where(lane < nh, d_f, jnp.where(lane < 2 * nh, d_a, jnp.where(lane < 3 * nh, d_b, 0.0)))
        dps_ref[...] = dps.astype(dps_ref.dtype)
        row0 = jnp.sum(dps, axis=0, keepdims=True)
        row1 = jnp.sum(jnp.where((lane >= nh) & (lane < 2 * nh), dsm * glog, 0.0), axis=0, keepdims=True)
        sub = lax.broadcasted_iota(jnp.int32, (8, LANES), 0)
        pg_ref[...] += jnp.where(sub == 0, row0, jnp.where(sub == 1, row1, 0.0))

    rev = pl.BlockSpec((tb, LANES), lambda i: (nb - 1 - i, 0))
    fix = pl.BlockSpec((8, LANES), lambda i: (0, 0))
    return pl.pallas_call(
        body, name="small_bwd", grid=(nb,),
        in_specs=[rev, fix, rev, pl.BlockSpec((8, tb), lambda i: (0, nb - 1 - i))],
        out_specs=[rev, fix],
        out_shape=[jax.ShapeDtypeStruct((s, LANES), bf16), jax.ShapeDtypeStruct((8, LANES), f32)],
        scratch_shapes=[pltpu.VMEM((1, LANES), f32)],
        compiler_params=_cp("arbitrary"))(ps, prm, dsm, dcum_t)


def _fox_fwd(qkv, cum, cum_t, wn, nh, tq):
    s = qkv.shape[0]
    fw = nh * HEAD_DIM
    scale = 1.0 / math.sqrt(HEAD_DIM)

    def body(q_ref, k_ref, v_ref, cum_ref, cumt_ref, wn_ref, o_ref, on_ref, lse_ref):
        h = pl.program_id(0)
        i = pl.program_id(1)
        q = q_ref[...]
        ci = _lane_col(cum_ref[...], h)
        rows = i * tq + lax.broadcasted_iota(jnp.int32, (tq, tq), 0)
        col0 = lax.broadcasted_iota(jnp.int32, (tq, tq), 1)

        def step(j, carry):
            m, l, acc = carry
            off = pl.multiple_of(j * tq, tq)
            kj = k_ref[pl.ds(off, tq), :]
            vj = v_ref[pl.ds(off, tq), :]
            cj = cumt_ref[pl.ds(h, 1), pl.ds(off, tq)]
            sc = _dotb(q, kj, NT) * scale + (ci - cj)
            sc = jnp.where(col0 + j * tq <= rows, sc, NEG)
            mn = jnp.maximum(m, jnp.max(sc, axis=1, keepdims=True))
            p = jnp.exp(sc - mn)
            alpha = jnp.exp(m - mn)
            return mn, alpha * l + jnp.sum(p, axis=1, keepdims=True), alpha * acc + _dotb(p, vj)

        m, l, acc = lax.fori_loop(
            0, i + 1, step, (jnp.full((tq, 1), NEG, f32), jnp.zeros((tq, 1), f32), jnp.zeros((tq, HEAD_DIM), f32)))
        o = acc / l
        o_ref[...] = o
        lse_ref[0] = m + jnp.log(l)
        r = lax.rsqrt(jnp.mean(o * o, axis=-1, keepdims=True) + EPS)
        on_ref[...] = (o * r * wn_ref[...]).astype(on_ref.dtype)

    return pl.pallas_call(
        body, name="fox_fwd", grid=(nh, s // tq),
        in_specs=[pl.BlockSpec((tq, HEAD_DIM), lambda h, i: (i, h)),
                  pl.BlockSpec((s, HEAD_DIM), lambda h, i: (0, nh + h)),
                  pl.BlockSpec((s, HEAD_DIM), lambda h, i: (0, 2 * nh + h)),
                  pl.BlockSpec((tq, LANES), lambda h, i: (i, 0)),
                  pl.BlockSpec((8, s), lambda h, i: (0, 0)),
                  pl.BlockSpec((1, HEAD_DIM), lambda h, i: (0, 0))],
        out_specs=[pl.BlockSpec((tq, HEAD_DIM), lambda h, i: (i, h)),
                   pl.BlockSpec((tq, HEAD_DIM), lambda h, i: (i, h)),
                   pl.BlockSpec((1, tq, 1), lambda h, i: (h, i, 0))],
        out_shape=[jax.ShapeDtypeStruct((s, fw), f32), jax.ShapeDtypeStruct((s, fw), bf16),
                   jax.ShapeDtypeStruct((nh, s, 1), f32)],
        compiler_params=_cp("parallel", "parallel"))(qkv, qkv, qkv, cum, cum_t, wn)


def _fox_post_bwd(don, o, lse, cum, wn, nh):
    s, fw = o.shape
    tm = _row_tile(s)

    def body(don_ref, o_ref, lse_ref, cum_ref, wn_ref, do_ref, st_ref, dwn_ref):
        i = pl.program_id(0)
        h = pl.program_id(1)

        @pl.when((i == 0) & (h == 0))
        def _():
            dwn_ref[...] = jnp.zeros_like(dwn_ref)

        o = o_ref[...]
        don = don_ref[...].astype(f32)
        r = lax.rsqrt(jnp.mean(o * o, axis=-1, keepdims=True) + EPS)
        n = o * r
        dwn_ref[...] += jnp.sum(don * n, axis=0, keepdims=True)
        dn = don * wn_ref[...]
        do = r * (dn - n * jnp.mean(dn * n, axis=-1, keepdims=True))
        do_ref[...] = do.astype(do_ref.dtype)
        delta = jnp.sum(do * o, axis=-1, keepdims=True)
        ai = _lane_col(cum_ref[...], h) - lse_ref[0]
        lane = lax.broadcasted_iota(jnp.int32, (tm, LANES), 1)
        st_ref[0] = jnp.where(lane == 0, ai, jnp.where(lane == 1, delta, 0.0))

    hb = pl.BlockSpec((tm, HEAD_DIM), lambda i, h: (i, h))
    return pl.pallas_call(
        body, name="fox_post_bwd", grid=(s // tm, nh),
        in_specs=[hb, hb, pl.BlockSpec((1, tm, 1), lambda i, h: (h, i, 0)), pl.BlockSpec((tm, LANES), lambda i, h: (i, 0)),
                  pl.BlockSpec((1, HEAD_DIM), lambda i, h: (0, 0))],
        out_specs=[hb, pl.BlockSpec((1, tm, LANES), lambda i, h: (h, i, 0)), pl.BlockSpec((1, HEAD_DIM), lambda i, h: (0, 0))],
        out_shape=[jax.ShapeDtypeStruct((s, fw), bf16), jax.ShapeDtypeStruct((nh, s, LANES), f32),
                   jax.ShapeDtypeStruct((1, HEAD_DIM), f32)],
        compiler_params=_cp("arbitrary", "arbitrary"))(don, o, lse, cum, wn)


def _fox_bwd(qkv, do, stats, cum_t, nh, tq):
    s = qkv.shape[0]
    fw = nh * HEAD_DIM
    nq = s // tq
    scale = 1.0 / math.sqrt(HEAD_DIM)

    def body(q_ref, k_ref, v_ref, do_ref, st_ref, cumt_ref, dq_ref, dk_ref, dv_ref, cs_ref, rs_ref):
        h = pl.program_id(0)
        j = pl.program_id(1)

        @pl.when(j == 0)
        def _():
            dq_ref[...] = jnp.zeros_like(dq_ref)
            rs_ref[...] = jnp.zeros_like(rs_ref)

        kj = k_ref[...]
        vj = v_ref[...]
        cj = cumt_ref[pl.ds(h, 1), pl.ds(pl.multiple_of(j * tq, tq), tq)]
        cols = j * tq + lax.broadcasted_iota(jnp.int32, (tq, tq), 1)
        row0 = lax.broadcasted_iota(jnp.int32, (tq, tq), 0)

        def step(i, carry):
            dk, dv, cs = carry
            off = pl.multiple_of(i * tq, tq)
            qi = q_ref[pl.ds(off, tq), :]
            doi = do_ref[pl.ds(off, tq), :]
            st = st_ref[0, pl.ds(off, tq), :]
            ai = st[:, 0:1]
            delta = st[:, 1:2]
            sc = _dotb(qi, kj, NT) * scale + (ai - cj)
            p = jnp.where(cols <= row0 + i * tq, jnp.exp(sc), 0.0)
            dv = dv + _dotb(p, doi, TN)
            dp = _dotb(doi, vj, NT)
            ds = p * (dp - delta)
            cs = cs + jnp.sum(ds, axis=0, keepdims=True)
            rs_ref[0, pl.ds(off, tq), :] += jnp.sum(ds, axis=1, keepdims=True)
            dk = dk + _dotb(ds, qi, TN) * scale
            dq_ref[pl.ds(off, tq), :] += _dotb(ds, kj) * scale
            return dk, dv, cs

        dk, dv, cs = lax.fori_loop(
            j, nq, step, (jnp.zeros((tq, HEAD_DIM), f32), jnp.zeros((tq, HEAD_DIM), f32), jnp.zeros((1, tq), f32)))
        dk_ref[...] = dk.astype(dk_ref.dtype)
        dv_ref[...] = dv.astype(dv_ref.dtype)
        cs_ref[0] = cs

    return pl.pallas_call(
        body, name="fox_bwd", grid=(nh, nq),
        in_specs=[pl.BlockSpec((s, HEAD_DIM), lambda h, j: (0, h)),
                  pl.BlockSpec((tq, HEAD_DIM), lambda h, j: (j, nh + h)),
                  pl.BlockSpec((tq, HEAD_DIM), lambda h, j: (j, 2 * nh + h)),
                  pl.BlockSpec((s, HEAD_DIM), lambda h, j: (0, h)),
                  pl.BlockSpec((1, s, LANES), lambda h, j: (h, 0, 0)),
                  pl.BlockSpec((8, s), lambda h, j: (0, 0))],
        out_specs=[pl.BlockSpec((s, HEAD_DIM), lambda h, j: (0, h)),
                   pl.BlockSpec((tq, HEAD_DIM), lambda h, j: (j, h)),
                   pl.BlockSpec((tq, HEAD_DIM), lambda h, j: (j, h)),
                   pl.BlockSpec((1, 1, tq), lambda h, j: (h, 0, j)),
                   pl.BlockSpec((1, s, 1), lambda h, j: (h, 0, 0))],
        out_shape=[jax.ShapeDtypeStruct((s, fw), f32), jax.ShapeDtypeStruct((s, fw), bf16),
                   jax.ShapeDtypeStruct((s, fw), bf16), jax.ShapeDtypeStruct((nh, 1, s), f32),
                   jax.ShapeDtypeStruct((nh, s, 1), f32)],
        compiler_params=_cp("parallel", "arbitrary"))(qkv, qkv, qkv, do, stats, cum_t)


def _conv_pre(xx, w, tm):
    pre = None
    for k in range(CONV_W):
        sh = CONV_W - 1 - k
        t = (pltpu.roll(xx, sh, 0) if sh else xx)[8:, :] * w[k:k + 1, :]
        pre = t if pre is None else pre + t
    return pre


def _gdn_pre(x, w, nh):
    s, cw = x.shape
    tm = _row_tile(s)
    nb = cw // HEAD_DIM

    def body(x_ref, prev_ref, w_ref, y_ref):
        i = pl.program_id(0)
        j = pl.program_id(1)
        prev = jnp.where(i == 0, 0.0, prev_ref[...])
        pre = _conv_pre(jnp.concatenate([prev, x_ref[...]], axis=0), w_ref[...], tm)
        y = pre * _sigmoid(pre)
        yn = y * lax.rsqrt(jnp.sum(y * y, axis=-1, keepdims=True) + EPS)
        y_ref[...] = jnp.where(j < 2 * nh, yn, y)

    return pl.pallas_call(
        body, name="gdn_pre", grid=(s // tm, nb),
        in_specs=[pl.BlockSpec((tm, HEAD_DIM), lambda i, j: (i, j)),
                  pl.BlockSpec((8, HEAD_DIM), lambda i, j: (jnp.maximum(i * (tm // 8) - 1, 0), j)),
                  pl.BlockSpec((CONV_W, HEAD_DIM), lambda i, j: (0, j))],
        out_specs=pl.BlockSpec((tm, HEAD_DIM), lambda i, j: (i, j)),
        out_shape=jax.ShapeDtypeStruct((s, cw), f32),
        compiler_params=_cp("parallel", "parallel"))(x, x, w)


def _gdn_pre_bwd(x, w, dyn, nh):
    s, cw = x.shape
    tm = _row_tile(s)
    nb = cw // HEAD_DIM

    def body(x_ref, prev_ref, w_ref, dyn_ref, dpre_ref):
        i = pl.program_id(0)
        j = pl.program_id(1)
        prev = jnp.where(i == 0, 0.0, prev_ref[...])
        pre = _conv_pre(jnp.concatenate([prev, x_ref[...]], axis=0), w_ref[...], tm)
        sg = _sigmoid(pre)
        y = pre * sg
        dyn = dyn_ref[...]
        r = lax.rsqrt(jnp.sum(y * y, axis=-1, keepdims=True) + EPS)
        yn = y * r
        dy_n = r * (dyn - yn * jnp.sum(dyn * yn, axis=-1, keepdims=True))
        dy = jnp.where(j < 2 * nh, dy_n, dyn)
        dpre_ref[...] = dy * sg * (1.0 + pre * (1.0 - sg))

    hb = pl.BlockSpec((tm, HEAD_DIM), lambda i, j: (i, j))
    return pl.pallas_call(
        body, name="gdn_pre_bwd", grid=(s // tm, nb),
        in_specs=[hb, pl.BlockSpec((8, HEAD_DIM), lambda i, j: (jnp.maximum(i * (tm // 8) - 1, 0), j)),
                  pl.BlockSpec((CONV_W, HEAD_DIM), lambda i, j: (0, j)), hb],
        out_specs=hb, out_shape=jax.ShapeDtypeStruct((s, cw), f32),
        compiler_params=_cp("parallel", "parallel"))(x, x, w, dyn)


def _conv_bwd(x, w, dpre):
    s, cw = x.shape
    tm = _row_tile(s)
    nb = cw // HEAD_DIM
    ni = s // tm

    def body(x_ref, prev_ref, w_ref, dp_ref, nxt_ref, dx_ref, dw_ref):
        j = pl.program_id(0)
        i = pl.program_id(1)

        @pl.when(i == 0)
        def _():
            dw_ref[...] = jnp.zeros_like(dw_ref)

        wv = w_ref[...]
        dp = dp_ref[...]
        nxt = jnp.where(i == ni - 1, 0.0, nxt_ref[...])
        dd = jnp.concatenate([dp, nxt], axis=0)
        prev = jnp.where(i == 0, 0.0, prev_ref[...])
        xx = jnp.concatenate([prev, x_ref[...]], axis=0)
        dx = None
        rows = []
        for k in range(CONV_W):
            sh = CONV_W - 1 - k
            t = (pltpu.roll(dd, tm + 8 - sh, 0) if sh else dd)[:tm, :] * wv[k:k + 1, :]
            dx = t if dx is None else dx + t
            xs = (pltpu.roll(xx, sh, 0) if sh else xx)[8:, :]
            rows.append(jnp.sum(dp * xs, axis=0, keepdims=True))
        dx_ref[...] = dx.astype(dx_ref.dtype)
        dw_ref[...] += jnp.concatenate(rows, axis=0)

    hb = pl.BlockSpec((tm, HEAD_DIM), lambda j, i: (i, j))
    wb = pl.BlockSpec((CONV_W, HEAD_DIM), lambda j, i: (0, j))
    return pl.pallas_call(
        body, name="conv_bwd", grid=(nb, ni),
        in_specs=[hb, pl.BlockSpec((8, HEAD_DIM), lambda j, i: (jnp.maximum(i * (tm // 8) - 1, 0), j)), wb, hb,
                  pl.BlockSpec((8, HEAD_DIM), lambda j, i: (jnp.minimum((i + 1) * (tm // 8), s // 8 - 1), j))],
        out_specs=[hb, wb],
        out_shape=[jax.ShapeDtypeStruct((s, cw), bf16), jax.ShapeDtypeStruct((CONV_W, cw), f32)],
        compiler_params=_cp("parallel", "arbitrary"))(x, x, w, dpre, dpre)


def _chunk_consts():
    c = GDN_CHUNK
    r = lax.broadcasted_iota(jnp.int32, (c, c), 0)
    q = lax.broadcasted_iota(jnp.int32, (c, c), 1)
    return r >= q, r > q, (r == q).astype(f32)


def _chunk_head(qkvn, sm, gcs, gcs_t, h, nh):
    fw = nh * HEAD_DIM
    q = qkvn[:, h * HEAD_DIM:(h + 1) * HEAD_DIM] * (HEAD_DIM ** -0.5)
    k = qkvn[:, fw + h * HEAD_DIM: fw + (h + 1) * HEAD_DIM]
    v = qkvn[:, 2 * fw + h * HEAD_DIM: 2 * fw + (h + 1) * HEAD_DIM]
    beta = _lane_col(sm, 2 * nh + h)
    gc = _lane_col(gcs, nh + h)
    gc_row = gcs_t[nh + h: nh + h + 1, :]
    incl, strict, _ = _chunk_consts()
    decay = jnp.where(incl, jnp.exp(jnp.minimum(gc - gc_row, 0.0)), 0.0)
    eg = jnp.exp(gc)
    g_last = gc[GDN_CHUNK - 1:GDN_CHUNK, :]
    egl = jnp.exp(g_last)
    ekd = jnp.exp(g_last - gc)
    kb = k * beta
    vb = v * beta
    kk = _dotb(kb, k, NT)
    qk = _dotb(q, k, NT)
    return dict(q=q, k=k, v=v, beta=beta, gc=gc, decay=decay, eg=eg, egl=egl, ekd=ekd, kb=kb, vb=vb, kk=kk, qk=qk,
                incl=incl, strict=strict)


def _unit_lower_inverse(low, eye):
    p = -low
    t = eye + p
    for _ in range(5):
        p = _doth(p, p)
        t = t + _doth(t, p)
    return t


def _gdn_fwd(qkvn, sm, z, wn, nh):
    s = qkvn.shape[0]
    c = GDN_CHUNK
    nc = s // c
    fw = nh * HEAD_DIM

    def body(qkvn_ref, sm_ref, z_ref, wn_ref, on_ref, o_ref, st_ref, ti_ref, state):
        ci = pl.program_id(0)

        @pl.when(ci == 0)
        def _():
            state[...] = jnp.zeros_like(state)

        qkvn_v = qkvn_ref[...]
        sm_v = sm_ref[...]
        incl, strict, eye = _chunk_consts()
        gcs = _doth(incl.astype(f32), sm_v)
        gcs_t = gcs.T
        for h in range(nh):
            e = _chunk_head(qkvn_v, sm_v, gcs, gcs_t, h, nh)
            low = jnp.where(strict, e["kk"] * e["decay"], 0.0)
            tinv = _unit_lower_inverse(low, eye)
            u = _doth(tinv, e["vb"])
            w = _doth(tinv, e["kb"] * e["eg"])
            a = jnp.where(incl, e["qk"] * e["decay"], 0.0)
            st = state[h]
            v_new = u - _dotb(w, st)
            o = _dotb(e["q"] * e["eg"], st) + _dotb(a, v_new)
            st_ref[0, h] = st
            ti_ref[0, h] = tinv
            state[h] = st * e["egl"] + _dotb(e["k"] * e["ekd"], v_new, TN)
            sl = slice(h * HEAD_DIM, (h + 1) * HEAD_DIM)
            o_ref[:, sl] = o
            zz = z_ref[:, sl]
            r = lax.rsqrt(jnp.mean(o * o, axis=-1, keepdims=True) + EPS)
            on_ref[:, sl] = (o * r * wn_ref[...] * (zz * _sigmoid(zz))).astype(on_ref.dtype)

    return pl.pallas_call(
        body, name="gdn_fwd", grid=(nc,),
        in_specs=[pl.BlockSpec((c, 3 * fw), lambda i: (i, 0)), pl.BlockSpec((c, LANES), lambda i: (i, 0)),
                  pl.BlockSpec((c, fw), lambda i: (i, 0)), pl.BlockSpec((1, HEAD_DIM), lambda i: (0, 0))],
        out_specs=[pl.BlockSpec((c, fw), lambda i: (i, 0)), pl.BlockSpec((c, fw), lambda i: (i, 0)),
                   pl.BlockSpec((1, nh, HEAD_DIM, HEAD_DIM), lambda i: (i, 0, 0, 0)),
                   pl.BlockSpec((1, nh, c, c), lambda i: (i, 0, 0, 0))],
        out_shape=[jax.ShapeDtypeStruct((s, fw), bf16), jax.ShapeDtypeStruct((s, fw), f32),
                   jax.ShapeDtypeStruct((nc, nh, HEAD_DIM, HEAD_DIM), f32), jax.ShapeDtypeStruct((nc, nh, c, c), f32)],
        scratch_shapes=[pltpu.VMEM((nh, HEAD_DIM, HEAD_DIM), f32)],
        compiler_params=_cp("arbitrary"))(qkvn, sm, z, wn)


def _gdn_post_bwd(don, o, z, wn, nh):
    s, fw = o.shape
    tm = _row_tile(s)

    def body(don_ref, o_ref, z_ref, wn_ref, do_ref, dz_ref, dwn_ref):
        i = pl.program_id(0)
        h = pl.program_id(1)

        @pl.when((i == 0) & (h == 0))
        def _():
            dwn_ref[...] = jnp.zeros_like(dwn_ref)

        o = o_ref[...]
        zz = z_ref[...]
        don = don_ref[...].astype(f32)
        wv = wn_ref[...]
        r = lax.rsqrt(jnp.mean(o * o, axis=-1, keepdims=True) + EPS)
        n = o * r
        sg = _sigmoid(zz)
        silu = zz * sg
        dz_ref[...] = (don * n * wv * sg * (1.0 + zz * (1.0 - sg))).astype(dz_ref.dtype)
        dnw = don * silu
        dwn_ref[...] += jnp.sum(dnw * n, axis=0, keepdims=True)
        dn = dnw * wv
        do_ref[...] = r * (dn - n * jnp.mean(dn * n, axis=-1, keepdims=True))

    hb = pl.BlockSpec((tm, HEAD_DIM), lambda i, h: (i, h))
    wb = pl.BlockSpec((1, HEAD_DIM), lambda i, h: (0, 0))
    return pl.pallas_call(
        body, name="gdn_post_bwd", grid=(s // tm, nh), in_specs=[hb, hb, hb, wb], out_specs=[hb, hb, wb],
        out_shape=[jax.ShapeDtypeStruct((s, fw), f32), jax.ShapeDtypeStruct((s, fw), bf16),
                   jax.ShapeDtypeStruct((1, HEAD_DIM), f32)],
        compiler_params=_cp("arbitrary", "arbitrary"))(don, o, z, wn)


def _gdn_bwd(qkvn, sm, do, states, tinvs, nh):
    s = qkvn.shape[0]
    c = GDN_CHUNK
    nc = s // c
    fw = nh * HEAD_DIM

    def body(qkvn_ref, sm_ref, do_ref, st_ref, ti_ref, dqkvn_ref, dsm_ref, dstate):
        ci = pl.program_id(0)

        @pl.when(ci == 0)
        def _():
            dstate[...] = jnp.zeros_like(dstate)

        qkvn_v = qkvn_ref[...]
        sm_v = sm_ref[...]
        incl, strict, eye = _chunk_consts()
        inclf = incl.astype(f32)
        gcs = _doth(inclf, sm_v)
        gcs_t = gcs.T
        lane = lax.broadcasted_iota(jnp.int32, (c, LANES), 1)
        last_row = lax.broadcasted_iota(jnp.int32, (c, 1), 0) == c - 1
        ones_cl = jnp.ones((c, LANES), f32)
        dsm = jnp.zeros((c, LANES), f32)
        for h in range(nh):
            e = _chunk_head(qkvn_v, sm_v, gcs, gcs_t, h, nh)
            q, k, v, beta, decay, eg, egl, ekd, kb, vb = (e[n] for n in ("q", "k", "v", "beta", "decay", "eg", "egl", "ekd", "kb", "vb"))
            tinv = ti_ref[0, h]
            st = st_ref[0, h]
            dst_out = dstate[h]
            sl = slice(h * HEAD_DIM, (h + 1) * HEAD_DIM)
            do_h = do_ref[:, sl]
            kg = kb * eg
            qg = q * eg
            kd = k * ekd
            u = _doth(tinv, vb)
            w = _doth(tinv, kg)
            a = jnp.where(incl, e["qk"] * decay, 0.0)
            v_new = u - _dotb(w, st)
            dv_new = _dotb(a, do_h, TN) + _dotb(kd, dst_out)
            da = jnp.where(incl, _dotb(do_h, v_new, NT), 0.0)
            dqg = _dotb(do_h, st, NT)
            dkd = _dotb(v_new, dst_out, NT)
            dglast = egl * jnp.sum(jnp.sum(dst_out * st, axis=1, keepdims=True), axis=0, keepdims=True)
            dw = -_dotb(dv_new, st, NT)
            dstate[h] = _dotb(qg, do_h, TN) + egl * dst_out - _dotb(w, dv_new, TN)
            dtinv = _doth(dv_new, vb, NT) + _doth(dw, kg, NT)
            dvb = _doth(tinv, dv_new, TN)
            dkg = _doth(tinv, dw, TN)
            dlow = -_doth(_doth(tinv, dtinv, TN), tinv, NT)
            dkk = jnp.where(strict, dlow * decay, 0.0)
            dqk = da * decay
            darg = (jnp.where(strict, dlow * e["kk"], 0.0) + da * e["qk"]) * decay
            dgc = jnp.sum(darg, axis=1, keepdims=True) - _doth(darg, ones_cl, TN)[:, 0:1]
            dkb = _dotb(dkk, k) + dkg * eg
            dk = _dotb(dkk, kb, TN) + _dotb(dqk, q, TN) + dkd * ekd + dkb * beta
            dq = (_dotb(dqk, k) + dqg * eg) * (HEAD_DIM ** -0.5)
            dbeta = jnp.sum(dkb * k + dvb * v, axis=1, keepdims=True)
            dv = dvb * beta
            s_kd = jnp.sum(dkd * kd, axis=1, keepdims=True)
            dgc = dgc + jnp.sum(dkg * kg + dqg * qg, axis=1, keepdims=True) - s_kd
            dgc = dgc + jnp.where(last_row, jnp.sum(s_kd, axis=0, keepdims=True) + dglast, 0.0)
            dg = _doth(inclf, dgc * ones_cl, TN)[:, 0:1]
            dqkvn_ref[:, sl] = dq
            dqkvn_ref[:, fw + h * HEAD_DIM: fw + (h + 1) * HEAD_DIM] = dk
            dqkvn_ref[:, 2 * fw + h * HEAD_DIM: 2 * fw + (h + 1) * HEAD_DIM] = dv
            dsm = dsm + jnp.where(lane == nh + h, dg, 0.0) + jnp.where(lane == 2 * nh + h, dbeta, 0.0)
        dsm_ref[...] = dsm

    rev = lambda i: (nc - 1 - i, 0)
    rev4 = lambda i: (nc - 1 - i, 0, 0, 0)
    return pl.pallas_call(
        body, name="gdn_bwd", grid=(nc,),
        in_specs=[pl.BlockSpec((c, 3 * fw), rev), pl.BlockSpec((c, LANES), rev), pl.BlockSpec((c, fw), rev),
                  pl.BlockSpec((1, nh, HEAD_DIM, HEAD_DIM), rev4), pl.BlockSpec((1, nh, c, c), rev4)],
        out_specs=[pl.BlockSpec((c, 3 * fw), rev), pl.BlockSpec((c, LANES), rev)],
        out_shape=[jax.ShapeDtypeStruct((s, 3 * fw), f32), jax.ShapeDtypeStruct((s, LANES), f32)],
        scratch_shapes=[pltpu.VMEM((nh, HEAD_DIM, HEAD_DIM), f32)],
        compiler_params=_cp("arbitrary"))(qkvn, sm, do, states, tinvs)


MM_TILES = (1024, 512, 2048)
MM_TILES_TN = (512, 1024, 1024)


def _ffn_fwd(tag, x, g, mod3, wg_t, wu_t, wd):
    sh, sc, gt = mod3
    h = _ada_in(tag + "_ada", x, g, sh, sc)
    gate, up, act = _mm(tag + "_up", [[(h, wg_t)], [(h, wu_t)]], "nt", MM_TILES, _ep_swiglu, (bf16, bf16, bf16))
    xn, y = _mm(tag + "_down", [[(act, wd)]], "nn", MM_TILES, _ep_residual, (f32, bf16),
                extras=((x, "mn"), (MACARON_W * gt, "n")))
    return xn, dict(x=x, h=h, gate=gate, up=up, y=y)


def _ffn_bwd(tag, dxn, res, g, mod3, wg_t, wu_t, wd):
    sh, sc, gt = mod3
    dy, dgs = _gate_bwd(tag + "_gate_bwd", dxn, res["y"], MACARON_W * gt)
    act, dgate, dup = _mm(tag + "_dact", [[(dy, wd)]], "nt", MM_TILES, _ep_swiglu_bwd, (bf16, bf16, bf16),
                          extras=((res["gate"], "mn"), (res["up"], "mn")))
    (dwd,) = _mm(tag + "_dwd", [[(act, dy)]], "tn", MM_TILES_TN, _ep_plain, (bf16,))
    (dwg_t,) = _mm(tag + "_dwg", [[(dgate, res["h"])]], "tn", MM_TILES_TN, _ep_plain, (bf16,))
    (dwu_t,) = _mm(tag + "_dwu", [[(dup, res["h"])]], "tn", MM_TILES_TN, _ep_plain, (bf16,))
    (dh,) = _mm(tag + "_dh", [[(dgate, wg_t), (dup, wu_t)]], "nn", MM_TILES, _ep_plain, (bf16,))
    dx, dsh, a = _ada_bwd(tag + "_ada_bwd", res["x"], g, sc, dh, dxn)
    return dx, (dwg_t, dwu_t, dwd), (dsh, a * g, MACARON_W * dgs), a * (1.0 + sc)


def _local_step(x, target, mods, norm_g, final_norm, ffn_w, w_cat_t, w_out, prm, fox_wn, gdn_wn, conv_w, nh):
    s, d = x.shape
    fw = nh * HEAD_DIM
    tq = _tile(s, 256)
    g_rows = [norm_g[i:i + 1] for i in range(3)]
    m1, m2, m3 = mods[0:3], mods[3:6], mods[6:9]

    x1, r1 = _ffn_fwd("ffn1", x, g_rows[0], m1, *ffn_w[0])
    h2 = _ada_in("mix_ada", x1, g_rows[1], m2[0], m2[1])
    w_fox, w_gdn, w_z, w_s = w_cat_t[:3 * fw], w_cat_t[3 * fw:6 * fw], w_cat_t[6 * fw:7 * fw], w_cat_t[7 * fw:]
    (qkv_f,) = _mm("proj_fox", [[(h2, w_fox)]], "nt", MM_TILES, _ep_plain, (bf16,))
    (qkv_g,) = _mm("proj_gdn", [[(h2, w_gdn)]], "nt", MM_TILES, _ep_plain, (f32,))
    (z,) = _mm("proj_z", [[(h2, w_z)]], "nt", MM_TILES, _ep_plain, (f32,))
    (ps,) = _mm("proj_s", [[(h2, w_s)]], "nt", MM_TILES, _ep_plain, (f32,))
    sm, cum, cum_t = _small_fwd(ps, prm, nh)
    o_f, on_f, lse = _fox_fwd(qkv_f, cum, cum_t, fox_wn, nh, tq)
    qkvn = _gdn_pre(qkv_g, conv_w, nh)
    on_g, o_g, states, tinvs = _gdn_fwd(qkvn, sm, z, gdn_wn, nh)
    w_top, w_bot = w_out[:fw], w_out[fw:]
    x2, mix = _mm("mix_out", [[(on_f, w_top), (on_g, w_bot)]], "nn", MM_TILES, _ep_residual, (f32, bf16),
                  extras=((x1, "mn"), (m2[2], "n")))
    x3, r3 = _ffn_fwd("ffn2", x2, g_rows[2], m3, *ffn_w[1])
    loss, dx3, dfinal = _final_loss(x3, final_norm, target)

    dx2, dffn2, dmod3, dg3 = _ffn_bwd("ffn2", dx3, r3, g_rows[2], m3, *ffn_w[1])
    dmix, dgt2 = _gate_bwd("mix_gate_bwd", dx2, mix, m2[2])
    (don_f,) = _mm("mix_dof", [[(dmix, w_top)]], "nt", MM_TILES, _ep_plain, (f32,))
    (don_g,) = _mm("mix_dog", [[(dmix, w_bot)]], "nt", MM_TILES, _ep_plain, (f32,))
    (dw_top,) = _mm("mix_dwtop", [[(on_f, dmix)]], "tn", MM_TILES_TN, _ep_plain, (bf16,))
    (dw_bot,) = _mm("mix_dwbot", [[(on_g, dmix)]], "tn", MM_TILES_TN, _ep_plain, (bf16,))
    do_f, stats, dfox_wn = _fox_post_bwd(don_f, o_f, lse, cum, fox_wn, nh)
    dq_f, dk_f, dv_f, colsum, rowsum = _fox_bwd(qkv_f, do_f, stats, cum_t, nh, tq)
    do_g, dz, dgdn_wn = _gdn_post_bwd(don_g, o_g, z, gdn_wn, nh)
    dqkvn, dsm = _gdn_bwd(qkvn, sm, do_g, states, tinvs, nh)
    dpre = _gdn_pre_bwd(qkv_g, conv_w, dqkvn, nh)
    dqkv_g, dconv = _conv_bwd(qkv_g, conv_w, dpre)
    dcum_t = rowsum.reshape(nh, s) - colsum.reshape(nh, s)
    if nh < 8:
        dcum_t = jnp.concatenate([dcum_t, jnp.zeros((8 - nh, s), f32)], axis=0)
    dps, pg = _small_bwd(ps, prm, dsm, dcum_t, nh)
    dproj = jnp.concatenate([dq_f.astype(bf16), dk_f, dv_f, dqkv_g, dz, dps], axis=1)
    (dw_cat_t,) = _mm("proj_dw", [[(dproj, h2)]], "tn", MM_TILES_TN, _ep_plain, (bf16,))
    (dh2,) = _mm("proj_dh", [[(dproj, w_cat_t)]], "nn", MM_TILES, _ep_plain, (bf16,))
    dx1, dsh2, a2 = _ada_bwd("mix_ada_bwd", x1, g_rows[1], m2[1], dh2, dx2)
    dmod2 = (dsh2, a2 * g_rows[1], dgt2)
    dg2 = a2 * (1.0 + m2[1])
    dx0, dffn1, dmod1, dg1 = _ffn_bwd("ffn1", dx1, r1, g_rows[0], m1, *ffn_w[0])

    big = dict(ffn=(dffn1, dffn2), w_cat_t=dw_cat_t, w_out=jnp.concatenate([dw_top, dw_bot], axis=0))
    small = dict(loss=loss, norm_g=jnp.concatenate([dg1, dg2, dg3], axis=0), final_norm=dfinal, fox_wn=dfox_wn,
                 gdn_wn=dgdn_wn, pg=pg, conv=dconv, mod=jnp.concatenate(list(dmod1) + list(dmod2) + list(dmod3), axis=1))
    return dx0, big, small


def _w_in_row_groups(nh):
    fw = nh * HEAD_DIM
    sizes = [3 * fw, nh, 3 * fw, nh, nh, fw]
    offs = [0]
    for sz in sizes:
        offs.append(offs[-1] + sz)
    return [(offs[i], offs[i + 1]) for i in range(len(sizes))]


def _build_w_cat_t(w_in_t, nh):
    gq, gf, gg, ga, gb, gz = _w_in_row_groups(nh)
    d = w_in_t.shape[1]
    rows = lambda r: w_in_t[r[0]:r[1]]
    pad = jnp.zeros((LANES - 3 * nh, d), w_in_t.dtype)
    return jnp.concatenate([rows(gq), rows(gg), rows(gz), rows(gf), rows(ga), rows(gb), pad], axis=0)


def _split_dw_cat_t(dw_cat_t, nh):
    fw = nh * HEAD_DIM
    o = 7 * fw
    return jnp.concatenate([dw_cat_t[:3 * fw], dw_cat_t[o:o + nh], dw_cat_t[3 * fw:6 * fw], dw_cat_t[o + nh:o + 2 * nh],
                            dw_cat_t[o + 2 * nh:o + 3 * nh], dw_cat_t[6 * fw:7 * fw]], axis=0)


def _head_params(fox_f_bias, gdn_dt_bias, gdn_a_log, nh):
    z = jnp.zeros((8, LANES), f32)
    z = z.at[0, 0:nh].set(fox_f_bias.reshape(nh))
    z = z.at[1, nh:2 * nh].set(gdn_dt_bias.reshape(nh))
    z = z.at[2, nh:2 * nh].set(gdn_a_log.reshape(nh))
    return z


ANY = pl.BlockSpec(memory_space=pl.ANY)
IN_VMEM = pl.BlockSpec(memory_space=pltpu.VMEM)
N_OTHER_CHIPS = 3


def _place():
    x, y, c = lax.axis_index("x"), lax.axis_index("y"), lax.axis_index("c")
    chips = [(1 - x, y), (x, 1 - y), (1 - x, 1 - y)]
    return x, y, c, chips


def _allgather8(name, v):
    r, n = v.shape

    def body(v_ref, out_ref, send_sems, recv_sems, local_sem):
        x, y, c, _ = _place()
        me = 4 * x + 2 * y + c
        mine = pltpu.make_async_copy(v_ref, out_ref.at[me], local_sem)
        mine.start()
        copies = []
        for k in range(1, 8):
            fx, fy, fc = (k >> 2) & 1, (k >> 1) & 1, k & 1
            peer = (x + fx - 2 * x * fx, y + fy - 2 * y * fy, c + fc - 2 * c * fc)
            cp = pltpu.make_async_remote_copy(src_ref=v_ref, dst_ref=out_ref.at[me], send_sem=send_sems.at[k - 1],
                                              recv_sem=recv_sems.at[k - 1], device_id=peer, device_id_type=MESH)
            cp.start()
            copies.append(cp)
        for cp in copies:
            cp.wait()
        mine.wait()

    return pl.pallas_call(
        body, name=name, in_specs=[IN_VMEM], out_specs=IN_VMEM, out_shape=jax.ShapeDtypeStruct((8, r, n), v.dtype),
        scratch_shapes=[pltpu.SemaphoreType.DMA((7,)), pltpu.SemaphoreType.DMA((7,)), pltpu.SemaphoreType.DMA],
        compiler_params=pltpu.CompilerParams(vmem_limit_bytes=VMEM_LIMIT_V7X))(v)


def _gather_weights(halves):
    nw = len(halves)

    def body(*refs):
        ins, outs = refs[:nw], refs[nw:2 * nw]
        local_sems, ici_send, ici_recv, d2d_send, d2d_recv = refs[2 * nw:]
        x, y, c, chips = _place()
        s = 2 * x + y
        sib = (x, y, 1 - c)
        locals_ = [pltpu.make_async_copy(ins[w], outs[w].at[:, s], local_sems.at[w]) for w in range(nw)]
        for cp in locals_:
            cp.start()
        sends = []
        for w in range(nw):
            for j, (cx, cy) in enumerate(chips):
                cp = pltpu.make_async_remote_copy(
                    src_ref=ins[w].at[c], dst_ref=outs[w].at[c, s], send_sem=ici_send.at[w * 3 + j],
                    recv_sem=ici_recv.at[w * 3 + j], device_id=(cx, cy, c), device_id_type=MESH)
                cp.start()
                sends.append(cp)
        fwds = []
        for w in range(nw):
            for j, (cx, cy) in enumerate(chips):
                sj = 2 * cx + cy
                landed = outs[w].at[c, sj]
                pltpu.make_async_remote_copy(src_ref=ins[w].at[c], dst_ref=landed, send_sem=ici_send.at[w * 3 + j],
                                             recv_sem=ici_recv.at[w * 3 + j], device_id=(cx, cy, c),
                                             device_id_type=MESH).wait_recv()
                cp = pltpu.make_async_remote_copy(src_ref=landed, dst_ref=landed, send_sem=d2d_send.at[w * 3 + j],
                                                  recv_sem=d2d_recv.at[w * 3 + j], device_id=sib, device_id_type=MESH)
                cp.start()
                fwds.append(cp)
        for w in range(nw):
            for j, (cx, cy) in enumerate(chips):
                theirs = outs[w].at[1 - c, 2 * cx + cy]
                pltpu.make_async_remote_copy(src_ref=theirs, dst_ref=theirs, send_sem=d2d_send.at[w * 3 + j],
                                             recv_sem=d2d_recv.at[w * 3 + j], device_id=sib, device_id_type=MESH).wait_recv()
        for cp in sends + fwds:
            cp.wait_send()
        for cp in locals_:
            cp.wait()

    n3 = nw * N_OTHER_CHIPS
    return pl.pallas_call(
        body, name="gather_weights", in_specs=[ANY] * nw, out_specs=[ANY] * nw,
        out_shape=[jax.ShapeDtypeStruct((2, 4) + h.shape[1:], h.dtype) for h in halves],
        scratch_shapes=[pltpu.SemaphoreType.DMA((nw,))] + [pltpu.SemaphoreType.DMA((n3,)) for _ in range(4)],
    )(*halves)


def _send_to_sibling(name, srcs, pick):
    nw = len(srcs)

    def body(*refs):
        ins, outs = refs[:nw], refs[nw:2 * nw]
        send_sems, recv_sems = refs[2 * nw:]
        x, y, c, _ = _place()
        cps = []
        for w in range(nw):
            cp = pltpu.make_async_remote_copy(src_ref=ins[w].at[pick(c)], dst_ref=outs[w], send_sem=send_sems.at[w],
                                              recv_sem=recv_sems.at[w], device_id=(x, y, 1 - c), device_id_type=MESH)
            cp.start()
            cps.append(cp)
        for cp in cps:
            cp.wait()

    return pl.pallas_call(
        body, name=name, in_specs=[ANY] * nw, out_specs=[ANY] * nw,
        out_shape=[jax.ShapeDtypeStruct(a.shape[1:], a.dtype) for a in srcs],
        scratch_shapes=[pltpu.SemaphoreType.DMA((nw,)), pltpu.SemaphoreType.DMA((nw,))],
    )(*srcs)


def _send_to_chips(partials):
    nw = len(partials)

    def body(*refs):
        ins, outs = refs[:nw], refs[nw:2 * nw]
        send_sems, recv_sems = refs[2 * nw:]
        x, y, c, chips = _place()
        cps = []
        for w in range(nw):
            for j, (cx, cy) in enumerate(chips):
                cp = pltpu.make_async_remote_copy(src_ref=ins[w].at[2 * cx + cy], dst_ref=outs[w].at[j],
                                                  send_sem=send_sems.at[w * 3 + j], recv_sem=recv_sems.at[w * 3 + j],
                                                  device_id=(cx, cy, c), device_id_type=MESH)
                cp.start()
                cps.append(cp)
        for cp in cps:
            cp.wait()

    n3 = nw * N_OTHER_CHIPS
    return pl.pallas_call(
        body, name="rs_to_chips", in_specs=[ANY] * nw, out_specs=[ANY] * nw,
        out_shape=[jax.ShapeDtypeStruct((3,) + a.shape[1:], a.dtype) for a in partials],
        scratch_shapes=[pltpu.SemaphoreType.DMA((n3,)), pltpu.SemaphoreType.DMA((n3,))],
    )(*partials)


def _exchange_halves(reduced):
    nw = len(reduced)

    def body(*refs):
        ins, outs = refs[:nw], refs[nw:2 * nw]
        local_sems, send_sems, recv_sems = refs[2 * nw:]
        x, y, c, _ = _place()
        cps, loc = [], []
        for w in range(nw):
            lc = pltpu.make_async_copy(ins[w], outs[w].at[c], local_sems.at[w])
            lc.start()
            loc.append(lc)
            cp = pltpu.make_async_remote_copy(src_ref=ins[w], dst_ref=outs[w].at[c], send_sem=send_sems.at[w],
                                              recv_sem=recv_sems.at[w], device_id=(x, y, 1 - c), device_id_type=MESH)
            cp.start()
            cps.append(cp)
        for cp in cps:
            cp.wait()
        for lc in loc:
            lc.wait()

    return pl.pallas_call(
        body, name="rs_exchange_halves", in_specs=[ANY] * nw, out_specs=[ANY] * nw,
        out_shape=[jax.ShapeDtypeStruct((2,) + a.shape, a.dtype) for a in reduced],
        scratch_shapes=[pltpu.SemaphoreType.DMA((nw,)) for _ in range(3)],
    )(*reduced)


def _add_pair(name, g, recv, c):
    _, nchip, r, d = g.shape
    tr = _tile(r, 512, 16)

    def body(c_ref, g_ref, r_ref, o_ref):
        o_ref[...] = (g_ref[...].astype(f32) + r_ref[...].astype(f32)).astype(o_ref.dtype)

    gs = pltpu.PrefetchScalarGridSpec(
        num_scalar_prefetch=1, grid=(nchip, r // tr),
        in_specs=[pl.BlockSpec((None, None, tr, d), lambda t, i, cr: (cr[0], t, i, 0)),
                  pl.BlockSpec((None, tr, d), lambda t, i, cr: (t, i, 0))],
        out_specs=pl.BlockSpec((None, tr, d), lambda t, i, cr: (t, i, 0)))
    return pl.pallas_call(body, name=name, grid_spec=gs, out_shape=jax.ShapeDtypeStruct((nchip, r, d), bf16),
                          compiler_params=_cp("parallel", "parallel"))(c.reshape(1).astype(jnp.int32), g, recv)


def _add_chips(name, p, recv, s_chip):
    _, r, d = p.shape
    tr = _tile(r, 512, 16)

    def body(s_ref, p_ref, r_ref, o_ref):
        o_ref[...] = ((p_ref[...].astype(f32) + r_ref[0].astype(f32)) + r_ref[1].astype(f32)) + r_ref[2].astype(f32)

    gs = pltpu.PrefetchScalarGridSpec(
        num_scalar_prefetch=1, grid=(r // tr,),
        in_specs=[pl.BlockSpec((None, tr, d), lambda i, sr: (sr[0], i, 0)),
                  pl.BlockSpec((3, tr, d), lambda i, sr: (0, i, 0))],
        out_specs=pl.BlockSpec((tr, d), lambda i, sr: (i, 0)))
    return pl.pallas_call(body, name=name, grid_spec=gs, out_shape=jax.ShapeDtypeStruct((r, d), f32),
                          compiler_params=_cp("parallel"))(s_chip.reshape(1).astype(jnp.int32), p, recv)


def _reduce_scatter(grads, c, s_chip):
    names = [str(i) for i in range(len(grads))]
    from_sib = _send_to_sibling("rs_to_sibling", grads, lambda cc: 1 - cc)
    partial = [_add_pair("rs_add_pair" + n, g, r, c) for n, g, r in zip(names, grads, from_sib)]
    from_chips = _send_to_chips(partial)
    reduced = [_add_chips("rs_add_chips" + n, p, r, s_chip) for n, p, r in zip(names, partial, from_chips)]
    return _exchange_halves(reduced)


def _sum_devices(v):
    n = v.shape[2]

    def body(v_ref, o_ref):
        t = v_ref[0]
        for k in range(1, 8):
            t = t + v_ref[k]
        o_ref[...] = t

    return pl.pallas_call(body, name="sum_devices", out_shape=jax.ShapeDtypeStruct((1, n), f32))(v)


def _silu_rows(v):
    def body(v_ref, o_ref):
        t = v_ref[...]
        o_ref[...] = t * _sigmoid(t)

    return pl.pallas_call(body, name="silu_cond", out_shape=jax.ShapeDtypeStruct(v.shape, f32))(v)


ADAMW_BLOCK_ELEMS = 600 * 1024


def _adamw(name, w, g, m, v):
    r, cdim = w.shape
    tr = _tile(r, max(8, min(256, (ADAMW_BLOCK_ELEMS // cdim) // 8 * 8)), 8)
    c1 = 1.0 - ADAM_B1 ** ADAM_STEP
    c2 = 1.0 - ADAM_B2 ** ADAM_STEP

    def body(w_ref, g_ref, m_ref, v_ref, d_ref, mo_ref, vo_ref):
        gv = g_ref[...]
        mn = ADAM_B1 * m_ref[...] + (1.0 - ADAM_B1) * gv
        vn = ADAM_B2 * v_ref[...] + (1.0 - ADAM_B2) * (gv * gv)
        d_ref[...] = -ADAM_LR * ((mn / c1) / (jnp.sqrt(vn / c2) + ADAM_EPS) + ADAM_WD * w_ref[...])
        mo_ref[...] = mn
        vo_ref[...] = vn

    blk = pl.BlockSpec((tr, cdim), lambda i: (i, 0))
    return pl.pallas_call(body, name=name, grid=(r // tr,), in_specs=[blk] * 4, out_specs=[blk] * 3,
                          out_shape=[jax.ShapeDtypeStruct((r, cdim), f32)] * 3, compiler_params=_cp("parallel"))(w, g, m, v)


def _ep_bias(accs, ex):
    return (accs[0] + ex[0],)


def kernel(x, c, ada_w, ada_b, norm_g, ffn_w_gate, ffn_w_up, ffn_w_down, w_in, w_out, fox_f_bias, fox_out_norm, gdn_conv, gdn_A_log, gdn_dt_bias, gdn_out_norm, final_norm, loss_target, m_ada_w, m_ada_b, m_norm_g, m_ffn_w_gate, m_ffn_w_up, m_ffn_w_down, m_w_in, m_w_out, m_fox_f_bias, m_fox_out_norm, m_gdn_conv, m_gdn_A_log, m_gdn_dt_bias, m_gdn_out_norm, m_final_norm, v_ada_w, v_ada_b, v_norm_g, v_ffn_w_gate, v_ffn_w_up, v_ffn_w_down, v_w_in, v_w_out, v_fox_f_bias, v_fox_out_norm, v_gdn_conv, v_gdn_A_log, v_gdn_dt_bias, v_gdn_out_norm, v_final_norm):
    ix, iy, ic = lax.axis_index("x"), lax.axis_index("y"), lax.axis_index("c")
    s_chip = 2 * ix + iy
    me = 4 * ix + 2 * iy + ic
    _, s, d = x.shape
    nh = d // (2 * HEAD_DIM)
    fw = nh * HEAD_DIM
    ncol = ada_w.shape[2]
    dg_sh = norm_g.shape[2]
    cv_sh = gdn_conv.shape[2]
    ff_sh = ffn_w_gate.shape[3]
    in_sh = w_in.shape[2]
    in_pad = -(-in_sh // 32) * 32
    out_sh = w_out.shape[1]
    per_chip = lambda a, t: a[2 * t]

    pack0 = jnp.concatenate([_silu_rows(c), norm_g[0].reshape(1, 3 * dg_sh), gdn_conv[0].reshape(1, CONV_W * cv_sh)], axis=1)
    got0 = _allgather8("gather_cond", pack0)
    cond_all = got0[:, 0, :d]
    norm_g_full = jnp.concatenate([per_chip(got0, t)[0, d:d + 3 * dg_sh].reshape(3, dg_sh) for t in range(4)], axis=1)
    conv_full = jnp.concatenate([per_chip(got0, t)[0, d + 3 * dg_sh:].reshape(CONV_W, cv_sh) for t in range(4)], axis=1)

    ada_b_sh = lax.dynamic_slice_in_dim(ada_b, s_chip * ncol, ncol, axis=1)
    (mod_sh,) = _mm("ada_mod", [[(cond_all, ada_w[0])]], "nn", (8, 512, 2048), _ep_bias, (f32,), extras=((ada_b_sh, "n"),))
    mod_all = _allgather8("gather_mod", mod_sh)
    mod = jnp.concatenate([lax.dynamic_index_in_dim(per_chip(mod_all, t), me, axis=0, keepdims=True) for t in range(4)], axis=1)
    mods = [mod[:, i * d:(i + 1) * d] for i in range(9)]

    halves = [jnp.swapaxes(ffn_w_gate[0], 1, 2).astype(bf16), jnp.swapaxes(ffn_w_up[0], 1, 2).astype(bf16),
              ffn_w_down[0].astype(bf16),
              jnp.pad(w_in[0].T.astype(bf16), ((0, in_pad - in_sh), (0, 0))).reshape(2, in_pad // 2, d),
              w_out[0].astype(bf16).reshape(2, out_sh // 2, d)]
    g_wg, g_wu, g_wd, g_win, g_wo = _gather_weights(halves)
    ffn_w = [(g_wg[j].reshape(4 * ff_sh, d), g_wu[j].reshape(4 * ff_sh, d), g_wd[j].reshape(4 * ff_sh, d)) for j in range(2)]
    w_cat_t = _build_w_cat_t(jnp.swapaxes(g_win, 0, 1).reshape(4, in_pad, d)[:, :in_sh].reshape(4 * in_sh, d), nh)
    w_out_full = jnp.swapaxes(g_wo, 0, 1).reshape(4 * out_sh, d)
    prm = _head_params(fox_f_bias, gdn_dt_bias, gdn_A_log, nh)

    dx0, big, small = _local_step(x[0], loss_target[0], mods, norm_g_full, final_norm.reshape(1, d), ffn_w, w_cat_t,
                                  w_out_full, prm, fox_out_norm, gdn_out_norm, conv_full, nh)

    pack1 = jnp.concatenate([small["loss"], small["norm_g"].reshape(1, 3 * d), small["final_norm"], small["fox_wn"],
                             small["gdn_wn"], small["pg"][0:1], small["pg"][1:2], small["conv"].reshape(1, CONV_W * 3 * fw),
                             small["mod"]], axis=1)
    got1 = _allgather8("gather_small_grads", pack1)
    tot = _sum_devices(got1)
    o = [0]

    def take(n):
        o[0] += n
        return tot[:, o[0] - n:o[0]]

    loss = take(LANES)[0, 0]
    g_norm_g = lax.dynamic_slice_in_dim(take(3 * d).reshape(3, d), s_chip * dg_sh, dg_sh, axis=1)[None]
    g_final = take(d).reshape(d)
    g_fox_wn = take(HEAD_DIM)
    g_gdn_wn = take(HEAD_DIM)
    pg0, pg1 = take(LANES), take(LANES)
    g_fbias, g_dtb, g_alog = pg0[:, 0:nh], pg0[:, nh:2 * nh], pg1[:, nh:2 * nh]
    g_conv = lax.dynamic_slice_in_dim(take(CONV_W * 3 * fw).reshape(CONV_W, 3 * fw), s_chip * cv_sh, cv_sh, axis=1)[None]
    g_ada_b = take(9 * d)
    dmod_all = got1[:, 0, o[0] - 9 * d:o[0]]
    dmod_sh = lax.dynamic_slice_in_dim(dmod_all, s_chip * ncol, ncol, axis=1)
    (g_ada_w,) = _mm("ada_dw", [[(cond_all, dmod_sh)]], "tn", (2048, 512, 8), _ep_plain, (f32,))

    stack = lambda k: jnp.stack([big["ffn"][0][k], big["ffn"][1][k]]).reshape(2, 4, ff_sh, d)
    dw_in_t = jnp.pad(_split_dw_cat_t(big["w_cat_t"], nh).reshape(4, in_sh, d), ((0, 0), (0, in_pad - in_sh), (0, 0)))
    grads = [stack(0), stack(1), stack(2), jnp.swapaxes(dw_in_t.reshape(4, 2, in_pad // 2, d), 0, 1),
             jnp.swapaxes(big["w_out"].reshape(4, 2, out_sh // 2, d), 0, 1)]
    r_wg, r_wu, r_wd, r_win, r_wo = _reduce_scatter(grads, ic, s_chip)
    g_ffn_gate = jnp.swapaxes(r_wg, 1, 2)[None]
    g_ffn_up = jnp.swapaxes(r_wu, 1, 2)[None]
    g_ffn_down = r_wd[None]
    g_w_in = r_win.reshape(in_pad, d)[:in_sh].T[None]
    g_w_out = r_wo.reshape(out_sh, d)[None]

    def upd(name, w, g, m, v):
        shp = w.shape
        two = lambda a: a.reshape(-1, shp[-1])
        return tuple(t.reshape(shp) for t in _adamw(name, two(w), two(g), two(m), two(v)))

    big_upd = [upd("adamw_ada_w", ada_w, g_ada_w[None], m_ada_w, v_ada_w),
               upd("adamw_ffn_gate", ffn_w_gate, g_ffn_gate, m_ffn_w_gate, v_ffn_w_gate),
               upd("adamw_ffn_up", ffn_w_up, g_ffn_up, m_ffn_w_up, v_ffn_w_up),
               upd("adamw_ffn_down", ffn_w_down, g_ffn_down, m_ffn_w_down, v_ffn_w_down),
               upd("adamw_w_in", w_in, g_w_in, m_w_in, v_w_in),
               upd("adamw_w_out", w_out, g_w_out, m_w_out, v_w_out)]
    small_w = [ada_b, norm_g, fox_f_bias, fox_out_norm, gdn_conv, gdn_A_log, gdn_dt_bias, gdn_out_norm, final_norm]
    small_g = [g_ada_b, g_norm_g, g_fbias, g_fox_wn, g_conv, g_alog, g_dtb, g_gdn_wn, g_final]
    small_m = [m_ada_b, m_norm_g, m_fox_f_bias, m_fox_out_norm, m_gdn_conv, m_gdn_A_log, m_gdn_dt_bias, m_gdn_out_norm, m_final_norm]
    small_v = [v_ada_b, v_norm_g, v_fox_f_bias, v_fox_out_norm, v_gdn_conv, v_gdn_A_log, v_gdn_dt_bias, v_gdn_out_norm, v_final_norm]
    sizes = [a.size for a in small_w]
    npad = -sum(sizes) % LANES
    flat = lambda arrs, fill: jnp.concatenate([a.reshape(1, -1) for a in arrs] + [jnp.full((1, npad), fill, f32)], axis=1)
    sd, sm_, sv = _adamw("adamw_small", flat(small_w, 0.0), flat(small_g, 0.0), flat(small_m, 0.0), flat(small_v, 1.0))

    def unflat(t):
        out, off = [], 0
        for a, n in zip(small_w, sizes):
            out.append(t[0, off:off + n].reshape(a.shape))
            off += n
        return out

    small_g = [g.reshape(a.shape) for g, a in zip(small_g, small_w)]
    s_d, s_m, s_v = unflat(sd), unflat(sm_), unflat(sv)
    def order(bigs, smalls):
        return [bigs[0], smalls[0], smalls[1], bigs[1], bigs[2], bigs[3], bigs[4], bigs[5]] + list(smalls[2:])

    grads_out = order([g_ada_w[None], g_ffn_gate, g_ffn_up, g_ffn_down, g_w_in, g_w_out], small_g)
    deltas = order([u[0] for u in big_upd], s_d)
    new_m = order([u[1] for u in big_upd], s_m)
    new_v = order([u[2] for u in big_upd], s_v)
    return (loss, dx0[None], *grads_out, *deltas, *new_m, *new_v)
```

```python
import functools
import math

import jax
import jax.numpy as jnp
from jax import lax
from jax.experimental import pallas as pl
from jax.experimental.pallas import tpu as pltpu

f32 = jnp.float32
bf16 = jnp.bfloat16
HI = lax.Precision.HIGHEST
MESH = pl.DeviceIdType.MESH

EPS = 1e-6
HEAD_DIM = 128
LANES = 128
GDN_CHUNK = 64
CONV_W = 4
MACARON_W = 0.5
ADAM_LR, ADAM_B1, ADAM_B2, ADAM_EPS, ADAM_WD, ADAM_STEP = 0.001, 0.9, 0.999, 1e-08, 0.01, 10
VMEM_LIMIT_V7X = 56 * 1024 * 1024
NEG = -1e30

NN = (((1,), (0,)), ((), ()))
NT = (((1,), (1,)), ((), ()))
TN = (((0,), (0,)), ((), ()))


def _cp(*sem):
    return pltpu.CompilerParams(dimension_semantics=sem, vmem_limit_bytes=VMEM_LIMIT_V7X)


def _dotb(a, b, dn=NN):
    return lax.dot_general(a.astype(bf16), b.astype(bf16), dn, preferred_element_type=f32)


def _doth(a, b, dn=NN):
    return lax.dot_general(a.astype(f32), b.astype(f32), dn, precision=HI, preferred_element_type=f32)


def _sigmoid(x):
    return 1.0 / (1.0 + jnp.exp(-x))


def _softplus(x):
    return jnp.maximum(x, 0.0) + jnp.log(1.0 + jnp.exp(-jnp.abs(x)))


def _lane_col(blk, lane_idx):
    lane = lax.broadcasted_iota(jnp.int32, blk.shape, 1)
    return jnp.sum(jnp.where(lane == lane_idx, blk, 0.0), axis=1, keepdims=True)


def _tile(n, pref, mult=LANES):
    if n <= pref:
        return n
    t = (pref // mult) * mult
    while t >= mult:
        if n % t == 0:
            return t
        t -= mult
    return n


def _mm(name, groups, mode, tiles, epilogue, out_dtypes, extras=()):
    a0, b0 = groups[0][0]
    if mode == "nn":
        (m, k), n = a0.shape, b0.shape[1]
    elif mode == "nt":
        (m, k), n = a0.shape, b0.shape[0]
    else:
        (k, m), n = a0.shape, b0.shape[1]
    tm, tn, tk = _tile(m, tiles[0]), _tile(n, tiles[1]), _tile(k, tiles[2])
    nk = k // tk
    assert m % tm == 0 and n % tn == 0 and k % tk == 0, (name, m, n, k, tm, tn, tk)
    if mode == "nn":
        a_spec = pl.BlockSpec((tm, tk), lambda i, j, kk: (i, kk))
        b_spec = pl.BlockSpec((tk, tn), lambda i, j, kk: (kk, j))
        dn = NN
    elif mode == "nt":
        a_spec = pl.BlockSpec((tm, tk), lambda i, j, kk: (i, kk))
        b_spec = pl.BlockSpec((tn, tk), lambda i, j, kk: (j, kk))
        dn = NT
    else:
        a_spec = pl.BlockSpec((tk, tm), lambda i, j, kk: (kk, i))
        b_spec = pl.BlockSpec((tk, tn), lambda i, j, kk: (kk, j))
        dn = TN
    npairs = sum(len(g) for g in groups)
    nacc, nex, nout = len(groups), len(extras), len(out_dtypes)
    in_specs, args = [], []
    for g in groups:
        for a, b in g:
            in_specs += [a_spec, b_spec]
            args += [a, b]
    for arr, kind in extras:
        if kind == "mn":
            in_specs.append(pl.BlockSpec((tm, tn), lambda i, j, kk: (i, j)))
        else:
            in_specs.append(pl.BlockSpec((1, tn), lambda i, j, kk: (0, j)))
        args.append(arr)

    def body(*refs):
        ins = refs[: 2 * npairs]
        ex = refs[2 * npairs: 2 * npairs + nex]
        outs = refs[2 * npairs + nex: 2 * npairs + nex + nout]
        accs = refs[2 * npairs + nex + nout:]
        kk = pl.program_id(2)

        @pl.when(kk == 0)
        def _():
            for acc in accs:
                acc[...] = jnp.zeros_like(acc)

        p = 0
        for gi, g in enumerate(groups):
            t = None
            for _ in g:
                d = _dotb(ins[2 * p][...], ins[2 * p + 1][...], dn)
                t = d if t is None else t + d
                p += 1
            accs[gi][...] += t

        @pl.when(kk == nk - 1)
        def _():
            res = epilogue([acc[...] for acc in accs], [e[...] for e in ex])
            for o, r in zip(outs, res):
                o[...] = r.astype(o.dtype)

    return pl.pallas_call(
        body, name=name, grid=(m // tm, n // tn, nk),
        in_specs=in_specs,
        out_specs=[pl.BlockSpec((tm, tn), lambda i, j, kk: (i, j)) for _ in out_dtypes],
        out_shape=[jax.ShapeDtypeStruct((m, n), dt) for dt in out_dtypes],
        scratch_shapes=[pltpu.VMEM((tm, tn), f32) for _ in range(nacc)],
        compiler_params=_cp("parallel", "parallel", "arbitrary"),
    )(*args)


def _ep_plain(accs, ex):
    return (accs[0],)


def _ep_colscale(accs, ex):
    return (accs[0] * ex[0],)


def _ep_swiglu(accs, ex):
    gate, up = accs
    act = gate * _sigmoid(gate) * up
    return gate, up, act


def _ep_residual(accs, ex):
    x, gs = ex
    y = accs[0]
    return x + gs * y, y


def _ep_swiglu_bwd(accs, ex):
    gate, up = ex[0].astype(f32), ex[1].astype(f32)
    dact = accs[0]
    sg = _sigmoid(gate)
    silu = gate * sg
    act = silu * up
    dup = dact * silu
    dgate = dact * up * sg * (1.0 + gate * (1.0 - sg))
    return act, dgate, dup


def _row_tile(s):
    return _tile(s, 256, 8)


def _ada_in(name, x, g, shift, scale):
    s, d = x.shape
    tm = _row_tile(s)

    def body(x_ref, g_ref, sh_ref, sc_ref, h_ref):
        xv = x_ref[...]
        r = lax.rsqrt(jnp.mean(xv * xv, axis=-1, keepdims=True) + EPS)
        h_ref[...] = (xv * r * g_ref[...] * (1.0 + sc_ref[...]) + sh_ref[...]).astype(h_ref.dtype)

    row = pl.BlockSpec((1, d), lambda i: (0, 0))
    blk = pl.BlockSpec((tm, d), lambda i: (i, 0))
    return pl.pallas_call(body, name=name, grid=(s // tm,), in_specs=[blk, row, row, row], out_specs=blk,
                          out_shape=jax.ShapeDtypeStruct((s, d), bf16), compiler_params=_cp("parallel"))(x, g, shift, scale)


def _ada_bwd(name, x, g, scale, dh, dres):
    s, d = x.shape
    tm = _row_tile(s)

    def body(x_ref, g_ref, sc_ref, dh_ref, dres_ref, dx_ref, dsh_ref, a_ref):
        i = pl.program_id(0)

        @pl.when(i == 0)
        def _():
            dsh_ref[...] = jnp.zeros_like(dsh_ref)
            a_ref[...] = jnp.zeros_like(a_ref)

        xv = x_ref[...]
        dhv = dh_ref[...].astype(f32)
        r = lax.rsqrt(jnp.mean(xv * xv, axis=-1, keepdims=True) + EPS)
        n = xv * r
        dn = dhv * (g_ref[...] * (1.0 + sc_ref[...]))
        dx_ref[...] = dres_ref[...] + r * (dn - n * jnp.mean(dn * n, axis=-1, keepdims=True))
        dsh_ref[...] += jnp.sum(dhv, axis=0, keepdims=True)
        a_ref[...] += jnp.sum(dhv * n, axis=0, keepdims=True)

    row = pl.BlockSpec((1, d), lambda i: (0, 0))
    blk = pl.BlockSpec((tm, d), lambda i: (i, 0))
    return pl.pallas_call(
        body, name=name, grid=(s // tm,), in_specs=[blk, row, row, blk, blk], out_specs=[blk, row, row],
        out_shape=[jax.ShapeDtypeStruct((s, d), f32), jax.ShapeDtypeStruct((1, d), f32), jax.ShapeDtypeStruct((1, d), f32)],
        compiler_params=_cp("arbitrary"))(x, g, scale, dh, dres)


def _gate_bwd(name, dx, y, gs):
    s, d = dx.shape
    tm = _row_tile(s)

    def body(dx_ref, y_ref, gs_ref, dy_ref, dgs_ref):
        i = pl.program_id(0)

        @pl.when(i == 0)
        def _():
            dgs_ref[...] = jnp.zeros_like(dgs_ref)

        dxv = dx_ref[...]
        dy_ref[...] = (dxv * gs_ref[...]).astype(dy_ref.dtype)
        dgs_ref[...] += jnp.sum(dxv * y_ref[...].astype(f32), axis=0, keepdims=True)

    row = pl.BlockSpec((1, d), lambda i: (0, 0))
    blk = pl.BlockSpec((tm, d), lambda i: (i, 0))
    return pl.pallas_call(
        body, name=name, grid=(s // tm,), in_specs=[blk, blk, row], out_specs=[blk, row],
        out_shape=[jax.ShapeDtypeStruct((s, d), bf16), jax.ShapeDtypeStruct((1, d), f32)],
        compiler_params=_cp("arbitrary"))(dx, y, gs)


def _final_loss(x, fg, target):
    s, d = x.shape
    tm = _row_tile(s)

    def body(x_ref, g_ref, t_ref, loss_ref, dx_ref, dg_ref):
        i = pl.program_id(0)

        @pl.when(i == 0)
        def _():
            loss_ref[...] = jnp.zeros_like(loss_ref)
            dg_ref[...] = jnp.zeros_like(dg_ref)

        xv = x_ref[...]
        gv = g_ref[...]
        r = lax.rsqrt(jnp.mean(xv * xv, axis=-1, keepdims=True) + EPS)
        n = xv * r
        e = n * gv - t_ref[...]
        per_tok = jnp.mean(e * e, axis=-1, keepdims=True)
        loss_ref[...] += 0.5 * jnp.sum(per_tok, axis=0, keepdims=True) * jnp.ones((1, LANES), f32)
        dy = e * (1.0 / d)
        dg_ref[...] += jnp.sum(dy * n, axis=0, keepdims=True)
        dn = dy * gv
        dx_ref[...] = r * (dn - n * jnp.mean(dn * n, axis=-1, keepdims=True))

    row = pl.BlockSpec((1, d), lambda i: (0, 0))
    blk = pl.BlockSpec((tm, d), lambda i: (i, 0))
    return pl.pallas_call(
        body, name="final_loss", grid=(s // tm,), in_specs=[blk, row, blk],
        out_specs=[pl.BlockSpec((1, LANES), lambda i: (0, 0)), blk, row],
        out_shape=[jax.ShapeDtypeStruct((1, LANES), f32), jax.ShapeDtypeStruct((s, d), f32), jax.ShapeDtypeStruct((1, d), f32)],
        compiler_params=_cp("arbitrary"))(x, fg, target)


def _small_fwd(ps, prm, nh):
    s = ps.shape[0]
    tb = LANES

    def body(ps_ref, prm_ref, sm_ref, cum_ref, carry):
        i = pl.program_id(0)

        @pl.when(i == 0)
        def _():
            carry[...] = jnp.zeros_like(carry)

        x = ps_ref[...]
        lane = lax.broadcasted_iota(jnp.int32, x.shape, 1)
        fb, dtb, alog = prm_ref[0:1, :], prm_ref[1:2, :], prm_ref[2:3, :]
        logf = -_softplus(-(x + fb))
        glog = -jnp.exp(alog) * _softplus(x + dtb)
        beta = _sigmoid(x)
        sm = jnp.where(lane < nh, logf, jnp.where(lane < 2 * nh, glog, jnp.where(lane < 3 * nh, beta, 0.0)))
        sm_ref[...] = sm
        r = lax.broadcasted_iota(jnp.int32, (tb, tb), 0)
        c = lax.broadcasted_iota(jnp.int32, (tb, tb), 1)
        tril = (c <= r).astype(f32)
        cs = _doth(tril, sm) + carry[...]
        cum_ref[...] = cs
        carry[...] = cs[tb - 1:tb, :]

    blk = pl.BlockSpec((tb, LANES), lambda i: (i, 0))
    return pl.pallas_call(
        body, name="small_fwd", grid=(s // tb,),
        in_specs=[blk, pl.BlockSpec((8, LANES), lambda i: (0, 0))],
        out_specs=[blk, blk],
        out_shape=[jax.ShapeDtypeStruct((s, LANES), f32), jax.ShapeDtypeStruct((s, LANES), f32)],
        scratch_shapes=[pltpu.VMEM((1, LANES), f32)],
        compiler_params=_cp("arbitrary"))(ps, prm)


def _small_bwd(ps, prm, dsm, dcum, nh):
    s = ps.shape[0]
    tb = LANES
    nb = s // tb

    def body(ps_ref, prm_ref, dsm_ref, dct_ref, dps_ref, pg_ref, carry):
        i = pl.program_id(0)

        @pl.when(i == 0)
        def _():
            carry[...] = jnp.zeros_like(carry)
            pg_ref[...] = jnp.zeros_like(pg_ref)

        x = ps_ref[...]
        dsm = dsm_ref[...]
        lane = lax.broadcasted_iota(jnp.int32, x.shape, 1)
        fb, dtb, alog = prm_ref[0:1, :], prm_ref[1:2, :], prm_ref[2:3, :]
        r = lax.broadcasted_iota(jnp.int32, (tb, tb), 0)
        c = lax.broadcasted_iota(jnp.int32, (tb, tb), 1)
        triu = (c >= r).astype(f32)
        dlogf = _doth(triu, dct_ref[...]) + carry[...]
        carry[...] = dlogf[0:1, :]
        d_f = dlogf * _sigmoid(-(x + fb))
        nega = -jnp.exp(alog)
        xa = x + dtb
        glog = nega * _softplus(xa)
        d_a = dsm * nega * _sigmoid(xa)
        beta = _sigmoid(x)
        d_b = dsm * beta * (1.0 - beta)
        dps = jnp.where(lane < nh, d_f, jnp.where(lane < 2 * nh, d_a, jnp.where(lane < 3 * nh, d_b, 0.0)))
        dps_ref[...] = dps.astype(dps_ref.dtype)
        row0 = jnp.sum(dps, axis=0, keepdims=True)
        row1 = jnp.sum(jnp.where((lane >= nh) & (lane < 2 * nh), dsm * glog, 0.0), axis=0, keepdims=True)
        sub = lax.broadcasted_iota(jnp.int32, (8, LANES), 0)
        pg_ref[...] += jnp.where(sub == 0, row0, jnp.where(sub == 1, row1, 0.0))

    rev = pl.BlockSpec((tb, LANES), lambda i: (nb - 1 - i, 0))
    fix = pl.BlockSpec((8, LANES), lambda i: (0, 0))
    return pl.pallas_call(
        body, name="small_bwd", grid=(nb,),
        in_specs=[rev, fix, rev, rev],
        out_specs=[rev, fix],
        out_shape=[jax.ShapeDtypeStruct((s, LANES), bf16), jax.ShapeDtypeStruct((8, LANES), f32)],
        scratch_shapes=[pltpu.VMEM((1, LANES), f32)],
        compiler_params=_cp("arbitrary"))(ps, prm, dsm, dcum)


LOG2E = 1.4426950408889634
LN2 = 0.6931471805599453
AUG = 2 * HEAD_DIM
FOX_Q_SCALE = LOG2E / math.sqrt(HEAD_DIM)


def _split3(col):
    hi = col.astype(bf16).astype(f32)
    r1 = col - hi
    mid = r1.astype(bf16).astype(f32)
    lo = (r1 - mid).astype(bf16).astype(f32)
    return hi, mid, lo


def _aug_block(rows, terms, terms_at, ones_at=None):
    lane = lax.broadcasted_iota(jnp.int32, (rows, LANES), 1)
    blk = jnp.zeros((rows, LANES), f32) if ones_at is None else jnp.where((lane >= ones_at) & (lane < ones_at + 3), 1.0, 0.0)
    for i, t in enumerate(terms):
        blk = jnp.where(lane == terms_at + i, t, blk)
    return blk


def _fox_aug(qkv, cum, nh):
    s = qkv.shape[0]
    tm = _row_tile(s)

    def body(q_ref, k_ref, v_ref, cum_ref, qa_ref, ka_ref, va_ref):
        h = pl.program_id(1)
        c2 = _lane_col(cum_ref[...], h) * LOG2E
        hi, mid, lo = _split3(c2)
        qa_ref[:, :HEAD_DIM] = q_ref[...]
        qa_ref[:, HEAD_DIM:] = _aug_block(tm, (hi, mid, lo), 0, 3).astype(bf16)
        ka_ref[:, :HEAD_DIM] = k_ref[...]
        ka_ref[:, HEAD_DIM:] = _aug_block(tm, (-hi, -mid, -lo), 3, 0).astype(bf16)
        va_ref[:, :HEAD_DIM] = v_ref[...]
        va_ref[:, HEAD_DIM:] = _aug_block(tm, (), 0, 0).astype(bf16)

    ab = pl.BlockSpec((tm, AUG), lambda i, h: (i, h))
    return pl.pallas_call(
        body, name="fox_aug", grid=(s // tm, nh),
        in_specs=[pl.BlockSpec((tm, HEAD_DIM), lambda i, h: (i, h)), pl.BlockSpec((tm, HEAD_DIM), lambda i, h: (i, nh + h)),
                  pl.BlockSpec((tm, HEAD_DIM), lambda i, h: (i, 2 * nh + h)), pl.BlockSpec((tm, LANES), lambda i, h: (i, 0))],
        out_specs=[ab, ab, ab], out_shape=[jax.ShapeDtypeStruct((s, nh * AUG), bf16)] * 3,
        compiler_params=_cp("parallel", "parallel"))(qkv, qkv, qkv, cum)


def _fox_fwd(qa, ka, qkv, wn, nh, tq):
    s = qa.shape[0]
    fw = nh * HEAD_DIM

    def body(qa_ref, ka_ref, v_ref, wn_ref, o_ref, on_ref, lse_ref):
        i = pl.program_id(1)
        q = qa_ref[...]

        def logits(j):
            return _dotb(q, ka_ref[pl.ds(pl.multiple_of(j * tq, tq), tq), :], NT)

        def softmax_tile(t, j, m, l):
            mn = jnp.maximum(m, jnp.max(t, axis=1, keepdims=True))
            p = jnp.exp2(t - mn)
            alpha = jnp.exp2(m - mn)
            pv = _dotb(p, v_ref[pl.ds(pl.multiple_of(j * tq, tq), tq), :])
            return mn, alpha * l + jnp.sum(p, axis=1, keepdims=True), alpha, pv

        def step(j, carry):
            t, m, l, acc, alpha_prev, pv_prev = carry
            t_next = logits(j + 1)
            acc = alpha_prev * acc + pv_prev
            m, l, alpha, pv = softmax_tile(t, j, m, l)
            return t_next, m, l, acc, alpha, pv

        zero = (lax.broadcasted_iota(jnp.int32, (tq, HEAD_DIM), 0) + lax.broadcasted_iota(jnp.int32, (tq, HEAD_DIM), 1)).astype(f32) * 0.0
        zcol = lax.broadcasted_iota(jnp.int32, (tq, 1), 0).astype(f32) * 0.0
        init = (logits(0), zcol + NEG, zcol, zero, zcol + 1.0, zero)
        t, m, l, acc, alpha_prev, pv_prev = lax.fori_loop(0, i, step, init)
        acc = alpha_prev * acc + pv_prev
        rows = lax.broadcasted_iota(jnp.int32, (tq, tq), 0)
        cols = lax.broadcasted_iota(jnp.int32, (tq, tq), 1)
        m, l, alpha, pv = softmax_tile(jnp.where(cols <= rows, t, NEG), i, m, l)
        o = (alpha * acc + pv) / l
        o_ref[...] = o
        lse_ref[0] = m + jnp.log2(l)
        r = lax.rsqrt(jnp.mean(o * o, axis=-1, keepdims=True) + EPS)
        on_ref[...] = (o * r * wn_ref[...]).astype(on_ref.dtype)

    hb = pl.BlockSpec((tq, HEAD_DIM), lambda h, i: (i, h))
    return pl.pallas_call(
        body, name="fox_fwd", grid=(nh, s // tq),
        in_specs=[pl.BlockSpec((tq, AUG), lambda h, i: (i, h)), pl.BlockSpec((s, AUG), lambda h, i: (0, h)),
                  pl.BlockSpec((s, HEAD_DIM), lambda h, i: (0, 2 * nh + h)), pl.BlockSpec((1, HEAD_DIM), lambda h, i: (0, 0))],
        out_specs=[hb, hb, pl.BlockSpec((1, tq, 1), lambda h, i: (h, i, 0))],
        out_shape=[jax.ShapeDtypeStruct((s, fw), f32), jax.ShapeDtypeStruct((s, fw), bf16), jax.ShapeDtypeStruct((nh, s, 1), f32)],
        compiler_params=_cp("parallel", "parallel"))(qa, ka, qkv, wn)


def _fox_post_bwd(don, o, lse2, cum, qkv, wn, nh):
    s, fw = o.shape
    tm = _row_tile(s)

    def body(don_ref, o_ref, lse_ref, cum_ref, q_ref, wn_ref, qb_ref, doa_ref, dwn_ref):
        i = pl.program_id(0)
        h = pl.program_id(1)

        @pl.when((i == 0) & (h == 0))
        def _():
            dwn_ref[...] = jnp.zeros_like(dwn_ref)

        o = o_ref[...]
        don = don_ref[...].astype(f32)
        r = lax.rsqrt(jnp.mean(o * o, axis=-1, keepdims=True) + EPS)
        n = o * r
        dwn_ref[...] += jnp.sum(don * n, axis=0, keepdims=True)
        dn = don * wn_ref[...]
        do = r * (dn - n * jnp.mean(dn * n, axis=-1, keepdims=True))
        delta = jnp.sum(do * o, axis=-1, keepdims=True)
        a2 = _lane_col(cum_ref[...], h) * LOG2E - lse_ref[0]
        qb_ref[:, :HEAD_DIM] = q_ref[...]
        qb_ref[:, HEAD_DIM:] = _aug_block(tm, _split3(a2), 0, 3).astype(bf16)
        doa_ref[:, :HEAD_DIM] = do.astype(bf16)
        doa_ref[:, HEAD_DIM:] = _aug_block(tm, _split3(-delta), 0).astype(bf16)

    hb = pl.BlockSpec((tm, HEAD_DIM), lambda i, h: (i, h))
    ab = pl.BlockSpec((tm, AUG), lambda i, h: (i, h))
    return pl.pallas_call(
        body, name="fox_post_bwd", grid=(s // tm, nh),
        in_specs=[hb, hb, pl.BlockSpec((1, tm, 1), lambda i, h: (h, i, 0)), pl.BlockSpec((tm, LANES), lambda i, h: (i, 0)),
                  hb, pl.BlockSpec((1, HEAD_DIM), lambda i, h: (0, 0))],
        out_specs=[ab, ab, pl.BlockSpec((1, HEAD_DIM), lambda i, h: (0, 0))],
        out_shape=[jax.ShapeDtypeStruct((s, nh * AUG), bf16), jax.ShapeDtypeStruct((s, nh * AUG), bf16),
                   jax.ShapeDtypeStruct((1, HEAD_DIM), f32)],
        compiler_params=_cp("arbitrary", "arbitrary"))(don, o, lse2, cum, qkv, wn)


def _fox_bwd(qb, doa, ka, va, nh, tq):
    s = qb.shape[0]
    nq = s // tq
    fw = nh * HEAD_DIM

    def body(qb_ref, doa_ref, ka_ref, va_ref, dqx_ref, dkx_ref, dv_ref):
        j = pl.program_id(1)

        @pl.when(j == 0)
        def _():
            dqx_ref[...] = jnp.zeros_like(dqx_ref)

        kj = ka_ref[...]
        vj = va_ref[...]

        def products(i):
            off = pl.multiple_of(jnp.minimum(i, nq - 1) * tq, tq)
            return _dotb(qb_ref[pl.ds(off, tq), :], kj, NT), _dotb(doa_ref[pl.ds(off, tq), :], vj, NT)

        def tile(i, p, dpd, dk, dv):
            off = pl.multiple_of(i * tq, tq)
            ds = (p * dpd).astype(bf16)
            dv = dv + _dotb(p, doa_ref[pl.ds(off, tq), :HEAD_DIM], TN)
            dk = dk + _dotb(ds, qb_ref[pl.ds(off, tq), :], TN)
            dqx_ref[pl.ds(off, tq), :] += _dotb(ds, kj)
            return dk, dv

        t, dpd = products(j)
        t_next, dpd_next = products(j + 1)
        rows = lax.broadcasted_iota(jnp.int32, (tq, tq), 0)
        cols = lax.broadcasted_iota(jnp.int32, (tq, tq), 1)
        dk, dv = tile(j, jnp.where(cols <= rows, jnp.exp2(t), 0.0), dpd, jnp.zeros((tq, AUG), f32), jnp.zeros((tq, HEAD_DIM), f32))

        def step(i, carry):
            t, dpd, dk, dv = carry
            t_next, dpd_next = products(i + 1)
            dk, dv = tile(i, jnp.exp2(t), dpd, dk, dv)
            return t_next, dpd_next, dk, dv

        _, _, dk, dv = lax.fori_loop(j + 1, nq, step, (t_next, dpd_next, dk, dv))
        dkx_ref[...] = dk
        dv_ref[...] = dv.astype(dv_ref.dtype)

    panel = pl.BlockSpec((s, AUG), lambda h, j: (0, h))
    blk = pl.BlockSpec((tq, AUG), lambda h, j: (j, h))
    return pl.pallas_call(
        body, name="fox_bwd", grid=(nh, nq), in_specs=[panel, panel, blk, blk],
        out_specs=[panel, blk, pl.BlockSpec((tq, HEAD_DIM), lambda h, j: (j, h))],
        out_shape=[jax.ShapeDtypeStruct((s, nh * AUG), f32), jax.ShapeDtypeStruct((s, nh * AUG), f32),
                   jax.ShapeDtypeStruct((s, fw), bf16)],
        compiler_params=_cp("parallel", "arbitrary"))(qb, doa, ka, va)


def _fox_unpack(dqx, dkx, nh):
    s = dqx.shape[0]
    fw = nh * HEAD_DIM
    tm = _row_tile(s)

    def body(dqx_ref, dkx_ref, dq_ref, dk_ref, dcum_ref):
        h = pl.program_id(1)

        @pl.when(h == 0)
        def _():
            dcum_ref[...] = jnp.zeros_like(dcum_ref)

        dq_ref[...] = (dqx_ref[:, :HEAD_DIM] * (HEAD_DIM ** -0.5)).astype(dq_ref.dtype)
        dk_ref[...] = (dkx_ref[:, :HEAD_DIM] * LN2).astype(dk_ref.dtype)
        d = _lane_col(dqx_ref[:, HEAD_DIM:], 0) - _lane_col(dkx_ref[:, HEAD_DIM:], 3)
        lane = lax.broadcasted_iota(jnp.int32, (tm, LANES), 1)
        dcum_ref[...] += jnp.where(lane == h, d, 0.0)

    ab = pl.BlockSpec((tm, AUG), lambda i, h: (i, h))
    hb = pl.BlockSpec((tm, HEAD_DIM), lambda i, h: (i, h))
    return pl.pallas_call(
        body, name="fox_unpack", grid=(s // tm, nh), in_specs=[ab, ab],
        out_specs=[hb, hb, pl.BlockSpec((tm, LANES), lambda i, h: (i, 0))],
        out_shape=[jax.ShapeDtypeStruct((s, fw), bf16), jax.ShapeDtypeStruct((s, fw), bf16), jax.ShapeDtypeStruct((s, LANES), f32)],
        compiler_params=_cp("parallel", "arbitrary"))(dqx, dkx)


def _conv_pre(xx, w, tm):
    pre = None
    for k in range(CONV_W):
        sh = CONV_W - 1 - k
        t = (pltpu.roll(xx, sh, 0) if sh else xx)[8:, :] * w[k:k + 1, :]
        pre = t if pre is None else pre + t
    return pre


def _gdn_pre(x, w, nh):
    s, cw = x.shape
    tm = _row_tile(s)
    fw = nh * HEAD_DIM

    def body(x_ref, prev_ref, w_ref, y_ref):
        i = pl.program_id(0)
        j = pl.program_id(1)
        for h in range(nh):
            sl = slice(h * HEAD_DIM, (h + 1) * HEAD_DIM)
            prev = jnp.where(i == 0, 0.0, prev_ref[:, sl])
            pre = _conv_pre(jnp.concatenate([prev, x_ref[:, sl]], axis=0), w_ref[:, sl], tm)
            y = pre * _sigmoid(pre)
            yn = y * lax.rsqrt(jnp.sum(y * y, axis=-1, keepdims=True) + EPS)
            y_ref[:, sl] = jnp.where(j < 2, yn, y)

    return pl.pallas_call(
        body, name="gdn_pre", grid=(s // tm, cw // fw),
        in_specs=[pl.BlockSpec((tm, fw), lambda i, j: (i, j)),
                  pl.BlockSpec((8, fw), lambda i, j: (jnp.maximum(i * (tm // 8) - 1, 0), j)),
                  pl.BlockSpec((CONV_W, fw), lambda i, j: (0, j))],
        out_specs=pl.BlockSpec((tm, fw), lambda i, j: (i, j)),
        out_shape=jax.ShapeDtypeStruct((s, cw), f32),
        compiler_params=_cp("parallel", "parallel"))(x, x, w)


def _gdn_pre_bwd(x, w, dyn, nh):
    s, cw = x.shape
    tm = _row_tile(s)
    fw = nh * HEAD_DIM

    def body(x_ref, prev_ref, w_ref, dyn_ref, dpre_ref):
        i = pl.program_id(0)
        j = pl.program_id(1)
        for h in range(nh):
            sl = slice(h * HEAD_DIM, (h + 1) * HEAD_DIM)
            prev = jnp.where(i == 0, 0.0, prev_ref[:, sl])
            pre = _conv_pre(jnp.concatenate([prev, x_ref[:, sl]], axis=0), w_ref[:, sl], tm)
            sg = _sigmoid(pre)
            y = pre * sg
            dyn = dyn_ref[:, sl]
            r = lax.rsqrt(jnp.sum(y * y, axis=-1, keepdims=True) + EPS)
            yn = y * r
            dy_n = r * (dyn - yn * jnp.sum(dyn * yn, axis=-1, keepdims=True))
            dy = jnp.where(j < 2, dy_n, dyn)
            dpre_ref[:, sl] = dy * sg * (1.0 + pre * (1.0 - sg))

    hb = pl.BlockSpec((tm, fw), lambda i, j: (i, j))
    return pl.pallas_call(
        body, name="gdn_pre_bwd", grid=(s // tm, cw // fw),
        in_specs=[hb, pl.BlockSpec((8, fw), lambda i, j: (jnp.maximum(i * (tm // 8) - 1, 0), j)),
                  pl.BlockSpec((CONV_W, fw), lambda i, j: (0, j)), hb],
        out_specs=hb, out_shape=jax.ShapeDtypeStruct((s, cw), f32),
        compiler_params=_cp("parallel", "parallel"))(x, x, w, dyn)


def _conv_bwd(x, w, dpre, nh):
    s, cw = x.shape
    tm = _row_tile(s)
    fw = nh * HEAD_DIM
    ni = s // tm

    def body(x_ref, prev_ref, w_ref, dp_ref, nxt_ref, dx_ref, dw_ref):
        i = pl.program_id(1)

        @pl.when(i == 0)
        def _():
            dw_ref[...] = jnp.zeros_like(dw_ref)

        for h in range(nh):
            sl = slice(h * HEAD_DIM, (h + 1) * HEAD_DIM)
            wv = w_ref[:, sl]
            dp = dp_ref[:, sl]
            nxt = jnp.where(i == ni - 1, 0.0, nxt_ref[:, sl])
            dd = jnp.concatenate([dp, nxt], axis=0)
            prev = jnp.where(i == 0, 0.0, prev_ref[:, sl])
            xx = jnp.concatenate([prev, x_ref[:, sl]], axis=0)
            dx = None
            rows = []
            for k in range(CONV_W):
                sh = CONV_W - 1 - k
                t = (pltpu.roll(dd, tm + 8 - sh, 0) if sh else dd)[:tm, :] * wv[k:k + 1, :]
                dx = t if dx is None else dx + t
                xs = (pltpu.roll(xx, sh, 0) if sh else xx)[8:, :]
                rows.append(jnp.sum(dp * xs, axis=0, keepdims=True))
            dx_ref[:, sl] = dx.astype(dx_ref.dtype)
            dw_ref[:, sl] += jnp.concatenate(rows, axis=0)

    hb = pl.BlockSpec((tm, fw), lambda j, i: (i, j))
    wb = pl.BlockSpec((CONV_W, fw), lambda j, i: (0, j))
    return pl.pallas_call(
        body, name="conv_bwd", grid=(cw // fw, ni),
        in_specs=[hb, pl.BlockSpec((8, fw), lambda j, i: (jnp.maximum(i * (tm // 8) - 1, 0), j)), wb, hb,
                  pl.BlockSpec((8, fw), lambda j, i: (jnp.minimum((i + 1) * (tm // 8), s // 8 - 1), j))],
        out_specs=[hb, wb],
        out_shape=[jax.ShapeDtypeStruct((s, cw), bf16), jax.ShapeDtypeStruct((CONV_W, cw), f32)],
        compiler_params=_cp("parallel", "arbitrary"))(x, x, w, dpre, dpre)


def _chunk_consts():
    c = GDN_CHUNK
    r = lax.broadcasted_iota(jnp.int32, (c, c), 0)
    q = lax.broadcasted_iota(jnp.int32, (c, c), 1)
    return r >= q, r > q, (r == q).astype(f32)


def _chunk_head(qkvn, sm, gcs, gcs_t, h, nh):
    fw = nh * HEAD_DIM
    q = qkvn[:, h * HEAD_DIM:(h + 1) * HEAD_DIM] * (HEAD_DIM ** -0.5)
    k = qkvn[:, fw + h * HEAD_DIM: fw + (h + 1) * HEAD_DIM]
    v = qkvn[:, 2 * fw + h * HEAD_DIM: 2 * fw + (h + 1) * HEAD_DIM]
    beta = _lane_col(sm, 2 * nh + h)
    gc = _lane_col(gcs, nh + h)
    gc_row = gcs_t[nh + h: nh + h + 1, :]
    incl, strict, _ = _chunk_consts()
    decay = jnp.where(incl, jnp.exp(jnp.minimum(gc - gc_row, 0.0)), 0.0)
    eg = jnp.exp(gc)
    g_last = gc[GDN_CHUNK - 1:GDN_CHUNK, :]
    egl = jnp.exp(g_last)
    ekd = jnp.exp(g_last - gc)
    kb = k * beta
    vb = v * beta
    kk = _dotb(kb, k, NT)
    qk = _dotb(q, k, NT)
    return dict(q=q, k=k, v=v, beta=beta, gc=gc, decay=decay, eg=eg, egl=egl, ekd=ekd, kb=kb, vb=vb, kk=kk, qk=qk,
                incl=incl, strict=strict)


def _unit_lower_inverse(low, eye):
    p = -low
    t = eye + p
    for _ in range(5):
        p = _doth(p, p)
        t = t + _doth(t, p)
    return t


def _gdn_fwd(qkvn, sm, z, wn, nh):
    s = qkvn.shape[0]
    c = GDN_CHUNK
    nc = s // c
    fw = nh * HEAD_DIM

    def body(qkvn_ref, sm_ref, z_ref, wn_ref, on_ref, o_ref, st_ref, ti_ref, state):
        ci = pl.program_id(0)

        @pl.when(ci == 0)
        def _():
            state[...] = jnp.zeros_like(state)

        qkvn_v = qkvn_ref[...]
        sm_v = sm_ref[...]
        incl, strict, eye = _chunk_consts()
        gcs = _doth(incl.astype(f32), sm_v)
        gcs_t = gcs.T
        for h in range(nh):
            e = _chunk_head(qkvn_v, sm_v, gcs, gcs_t, h, nh)
            low = jnp.where(strict, e["kk"] * e["decay"], 0.0)
            tinv = _unit_lower_inverse(low, eye)
            u = _doth(tinv, e["vb"])
            w = _doth(tinv, e["kb"] * e["eg"])
            a = jnp.where(incl, e["qk"] * e["decay"], 0.0)
            st = state[h]
            v_new = u - _dotb(w, st)
            o = _dotb(e["q"] * e["eg"], st) + _dotb(a, v_new)
            st_ref[0, h] = st
            ti_ref[0, h] = tinv
            state[h] = st * e["egl"] + _dotb(e["k"] * e["ekd"], v_new, TN)
            sl = slice(h * HEAD_DIM, (h + 1) * HEAD_DIM)
            o_ref[:, sl] = o
            zz = z_ref[:, sl]
            r = lax.rsqrt(jnp.mean(o * o, axis=-1, keepdims=True) + EPS)
            on_ref[:, sl] = (o * r * wn_ref[...] * (zz * _sigmoid(zz))).astype(on_ref.dtype)

    return pl.pallas_call(
        body, name="gdn_fwd", grid=(nc,),
        in_specs=[pl.BlockSpec((c, 3 * fw), lambda i: (i, 0)), pl.BlockSpec((c, LANES), lambda i: (i, 0)),
                  pl.BlockSpec((c, fw), lambda i: (i, 0)), pl.BlockSpec((1, HEAD_DIM), lambda i: (0, 0))],
        out_specs=[pl.BlockSpec((c, fw), lambda i: (i, 0)), pl.BlockSpec((c, fw), lambda i: (i, 0)),
                   pl.BlockSpec((1, nh, HEAD_DIM, HEAD_DIM), lambda i: (i, 0, 0, 0)),
                   pl.BlockSpec((1, nh, c, c), lambda i: (i, 0, 0, 0))],
        out_shape=[jax.ShapeDtypeStruct((s, fw), bf16), jax.ShapeDtypeStruct((s, fw), f32),
                   jax.ShapeDtypeStruct((nc, nh, HEAD_DIM, HEAD_DIM), f32), jax.ShapeDtypeStruct((nc, nh, c, c), f32)],
        scratch_shapes=[pltpu.VMEM((nh, HEAD_DIM, HEAD_DIM), f32)],
        compiler_params=_cp("arbitrary"))(qkvn, sm, z, wn)


def _gdn_post_bwd(don, o, z, wn, nh):
    s, fw = o.shape
    tm = _row_tile(s)

    def body(don_ref, o_ref, z_ref, wn_ref, do_ref, dz_ref, dwn_ref):
        i = pl.program_id(0)
        h = pl.program_id(1)

        @pl.when((i == 0) & (h == 0))
        def _():
            dwn_ref[...] = jnp.zeros_like(dwn_ref)

        o = o_ref[...]
        zz = z_ref[...]
        don = don_ref[...].astype(f32)
        wv = wn_ref[...]
        r = lax.rsqrt(jnp.mean(o * o, axis=-1, keepdims=True) + EPS)
        n = o * r
        sg = _sigmoid(zz)
        silu = zz * sg
        dz_ref[...] = (don * n * wv * sg * (1.0 + zz * (1.0 - sg))).astype(dz_ref.dtype)
        dnw = don * silu
        dwn_ref[...] += jnp.sum(dnw * n, axis=0, keepdims=True)
        dn = dnw * wv
        do_ref[...] = r * (dn - n * jnp.mean(dn * n, axis=-1, keepdims=True))

    hb = pl.BlockSpec((tm, HEAD_DIM), lambda i, h: (i, h))
    wb = pl.BlockSpec((1, HEAD_DIM), lambda i, h: (0, 0))
    return pl.pallas_call(
        body, name="gdn_post_bwd", grid=(s // tm, nh), in_specs=[hb, hb, hb, wb], out_specs=[hb, hb, wb],
        out_shape=[jax.ShapeDtypeStruct((s, fw), f32), jax.ShapeDtypeStruct((s, fw), bf16),
                   jax.ShapeDtypeStruct((1, HEAD_DIM), f32)],
        compiler_params=_cp("arbitrary", "arbitrary"))(don, o, z, wn)


def _gdn_bwd(qkvn, sm, do, states, tinvs, nh):
    s = qkvn.shape[0]
    c = GDN_CHUNK
    nc = s // c
    fw = nh * HEAD_DIM

    def body(qkvn_ref, sm_ref, do_ref, st_ref, ti_ref, dqkvn_ref, dsm_ref, dstate):
        ci = pl.program_id(0)

        @pl.when(ci == 0)
        def _():
            dstate[...] = jnp.zeros_like(dstate)

        qkvn_v = qkvn_ref[...]
        sm_v = sm_ref[...]
        incl, strict, eye = _chunk_consts()
        inclf = incl.astype(f32)
        gcs = _doth(inclf, sm_v)
        gcs_t = gcs.T
        lane = lax.broadcasted_iota(jnp.int32, (c, LANES), 1)
        last_row = lax.broadcasted_iota(jnp.int32, (c, 1), 0) == c - 1
        ones_cl = jnp.ones((c, LANES), f32)
        dsm = jnp.zeros((c, LANES), f32)
        for h in range(nh):
            e = _chunk_head(qkvn_v, sm_v, gcs, gcs_t, h, nh)
            q, k, v, beta, decay, eg, egl, ekd, kb, vb = (e[n] for n in ("q", "k", "v", "beta", "decay", "eg", "egl", "ekd", "kb", "vb"))
            tinv = ti_ref[0, h]
            st = st_ref[0, h]
            dst_out = dstate[h]
            sl = slice(h * HEAD_DIM, (h + 1) * HEAD_DIM)
            do_h = do_ref[:, sl]
            kg = kb * eg
            qg = q * eg
            kd = k * ekd
            u = _doth(tinv, vb)
            w = _doth(tinv, kg)
            a = jnp.where(incl, e["qk"] * decay, 0.0)
            v_new = u - _dotb(w, st)
            dv_new = _dotb(a, do_h, TN) + _dotb(kd, dst_out)
            da = jnp.where(incl, _dotb(do_h, v_new, NT), 0.0)
            dqg = _dotb(do_h, st, NT)
            dkd = _dotb(v_new, dst_out, NT)
            dglast = egl * jnp.sum(jnp.sum(dst_out * st, axis=1, keepdims=True), axis=0, keepdims=True)
            dw = -_dotb(dv_new, st, NT)
            dstate[h] = _dotb(qg, do_h, TN) + egl * dst_out - _dotb(w, dv_new, TN)
            dtinv = _doth(dv_new, vb, NT) + _doth(dw, kg, NT)
            dvb = _doth(tinv, dv_new, TN)
            dkg = _doth(tinv, dw, TN)
            dlow = -_doth(_doth(tinv, dtinv, TN), tinv, NT)
            dkk = jnp.where(strict, dlow * decay, 0.0)
            dqk = da * decay
            darg = (jnp.where(strict, dlow * e["kk"], 0.0) + da * e["qk"]) * decay
            dgc = jnp.sum(darg, axis=1, keepdims=True) - _doth(darg, ones_cl, TN)[:, 0:1]
            dkb = _dotb(dkk, k) + dkg * eg
            dk = _dotb(dkk, kb, TN) + _dotb(dqk, q, TN) + dkd * ekd + dkb * beta
            dq = (_dotb(dqk, k) + dqg * eg) * (HEAD_DIM ** -0.5)
            dbeta = jnp.sum(dkb * k + dvb * v, axis=1, keepdims=True)
            dv = dvb * beta
            s_kd = jnp.sum(dkd * kd, axis=1, keepdims=True)
            dgc = dgc + jnp.sum(dkg * kg + dqg * qg, axis=1, keepdims=True) - s_kd
            dgc = dgc + jnp.where(last_row, jnp.sum(s_kd, axis=0, keepdims=True) + dglast, 0.0)
            dg = _doth(inclf, dgc * ones_cl, TN)[:, 0:1]
            dqkvn_ref[:, sl] = dq
            dqkvn_ref[:, fw + h * HEAD_DIM: fw + (h + 1) * HEAD_DIM] = dk
            dqkvn_ref[:, 2 * fw + h * HEAD_DIM: 2 * fw + (h + 1) * HEAD_DIM] = dv
            dsm = dsm + jnp.where(lane == nh + h, dg, 0.0) + jnp.where(lane == 2 * nh + h, dbeta, 0.0)
        dsm_ref[...] = dsm

    rev = lambda i: (nc - 1 - i, 0)
    rev4 = lambda i: (nc - 1 - i, 0, 0, 0)
    return pl.pallas_call(
        body, name="gdn_bwd", grid=(nc,),
        in_specs=[pl.BlockSpec((c, 3 * fw), rev), pl.BlockSpec((c, LANES), rev), pl.BlockSpec((c, fw), rev),
                  pl.BlockSpec((1, nh, HEAD_DIM, HEAD_DIM), rev4), pl.BlockSpec((1, nh, c, c), rev4)],
        out_specs=[pl.BlockSpec((c, 3 * fw), rev), pl.BlockSpec((c, LANES), rev)],
        out_shape=[jax.ShapeDtypeStruct((s, 3 * fw), f32), jax.ShapeDtypeStruct((s, LANES), f32)],
        scratch_shapes=[pltpu.VMEM((nh, HEAD_DIM, HEAD_DIM), f32)],
        compiler_params=_cp("arbitrary"))(qkvn, sm, do, states, tinvs)


MM_TILES = (1024, 512, 2048)
MM_TILES_TN = (512, 1024, 1024)
MM_TILES_LONG_K = (1024, 512, 2560)


def _ffn_fwd(tag, x, g, mod3, wg_t, wu_t, wd):
    sh, sc, gt = mod3
    h = _ada_in(tag + "_ada", x, g, sh, sc)
    gate, up, act = _mm(tag + "_up", [[(h, wg_t)], [(h, wu_t)]], "nt", MM_TILES, _ep_swiglu, (bf16, bf16, bf16))
    xn, y = _mm(tag + "_down", [[(act, wd)]], "nn", MM_TILES, _ep_residual, (f32, bf16),
                extras=((x, "mn"), (MACARON_W * gt, "n")))
    return xn, dict(x=x, h=h, gate=gate, up=up, y=y)


def _ffn_bwd(tag, dxn, res, g, mod3, wg_t, wu_t, wd):
    sh, sc, gt = mod3
    dy, dgs = _gate_bwd(tag + "_gate_bwd", dxn, res["y"], MACARON_W * gt)
    act, dgate, dup = _mm(tag + "_dact", [[(dy, wd)]], "nt", MM_TILES, _ep_swiglu_bwd, (bf16, bf16, bf16),
                          extras=((res["gate"], "mn"), (res["up"], "mn")))
    (dwd,) = _mm(tag + "_dwd", [[(act, dy)]], "tn", MM_TILES_TN, _ep_plain, (bf16,))
    (dwg_t,) = _mm(tag + "_dwg", [[(dgate, res["h"])]], "tn", MM_TILES_TN, _ep_plain, (bf16,))
    (dwu_t,) = _mm(tag + "_dwu", [[(dup, res["h"])]], "tn", MM_TILES_TN, _ep_plain, (bf16,))
    (dh,) = _mm(tag + "_dh", [[(dgate, wg_t), (dup, wu_t)]], "nn", MM_TILES, _ep_plain, (bf16,))
    dx, dsh, a = _ada_bwd(tag + "_ada_bwd", res["x"], g, sc, dh, dxn)
    return dx, (dwg_t, dwu_t, dwd), (dsh, a * g, MACARON_W * dgs), a * (1.0 + sc)


def _local_step(x, target, mods, norm_g, final_norm, ffn_w, w_cat_t, w_out, prm, fox_wn, gdn_wn, conv_w, nh):
    s, d = x.shape
    fw = nh * HEAD_DIM
    tq = _tile(s, 256)
    g_rows = [norm_g[i:i + 1] for i in range(3)]
    m1, m2, m3 = mods[0:3], mods[3:6], mods[6:9]

    x1, r1 = _ffn_fwd("ffn1", x, g_rows[0], m1, *ffn_w[0])
    h2 = _ada_in("mix_ada", x1, g_rows[1], m2[0], m2[1])
    w_fox, w_gdn, w_z, w_s = w_cat_t[:3 * fw], w_cat_t[3 * fw:6 * fw], w_cat_t[6 * fw:7 * fw], w_cat_t[7 * fw:]
    colscale = jnp.concatenate([jnp.full((1, fw), FOX_Q_SCALE, f32), jnp.ones((1, 2 * fw), f32)], axis=1)
    (qkv_f,) = _mm("proj_fox", [[(h2, w_fox)]], "nt", MM_TILES, _ep_colscale, (bf16,), extras=((colscale, "n"),))
    (qkv_g,) = _mm("proj_gdn", [[(h2, w_gdn)]], "nt", MM_TILES, _ep_plain, (f32,))
    (z,) = _mm("proj_z", [[(h2, w_z)]], "nt", MM_TILES, _ep_plain, (f32,))
    (ps,) = _mm("proj_s", [[(h2, w_s)]], "nt", MM_TILES, _ep_plain, (f32,))
    sm, cum = _small_fwd(ps, prm, nh)
    qa, ka, va = _fox_aug(qkv_f, cum, nh)
    o_f, on_f, lse2 = _fox_fwd(qa, ka, qkv_f, fox_wn, nh, tq)
    qkvn = _gdn_pre(qkv_g, conv_w, nh)
    on_g, o_g, states, tinvs = _gdn_fwd(qkvn, sm, z, gdn_wn, nh)
    w_top, w_bot = w_out[:fw], w_out[fw:]
    x2, mix = _mm("mix_out", [[(on_f, w_top), (on_g, w_bot)]], "nn", MM_TILES, _ep_residual, (f32, bf16),
                  extras=((x1, "mn"), (m2[2], "n")))
    x3, r3 = _ffn_fwd("ffn2", x2, g_rows[2], m3, *ffn_w[1])
    loss, dx3, dfinal = _final_loss(x3, final_norm, target)

    dx2, dffn2, dmod3, dg3 = _ffn_bwd("ffn2", dx3, r3, g_rows[2], m3, *ffn_w[1])
    dmix, dgt2 = _gate_bwd("mix_gate_bwd", dx2, mix, m2[2])
    (don_f,) = _mm("mix_dof", [[(dmix, w_top)]], "nt", MM_TILES, _ep_plain, (f32,))
    (don_g,) = _mm("mix_dog", [[(dmix, w_bot)]], "nt", MM_TILES, _ep_plain, (f32,))
    (dw_top,) = _mm("mix_dwtop", [[(on_f, dmix)]], "tn", MM_TILES_TN, _ep_plain, (bf16,))
    (dw_bot,) = _mm("mix_dwbot", [[(on_g, dmix)]], "tn", MM_TILES_TN, _ep_plain, (bf16,))
    qb, doa, dfox_wn = _fox_post_bwd(don_f, o_f, lse2, cum, qkv_f, fox_wn, nh)
    dqx, dkx, dv_f = _fox_bwd(qb, doa, ka, va, nh, tq)
    dq_f, dk_f, dcum = _fox_unpack(dqx, dkx, nh)
    do_g, dz, dgdn_wn = _gdn_post_bwd(don_g, o_g, z, gdn_wn, nh)
    dqkvn, dsm = _gdn_bwd(qkvn, sm, do_g, states, tinvs, nh)
    dpre = _gdn_pre_bwd(qkv_g, conv_w, dqkvn, nh)
    dqkv_g, dconv = _conv_bwd(qkv_g, conv_w, dpre, nh)
    dps, pg = _small_bwd(ps, prm, dsm, dcum, nh)
    dproj = jnp.concatenate([dq_f, dk_f, dv_f, dqkv_g, dz, dps], axis=1)
    (dw_cat_t,) = _mm("proj_dw", [[(dproj, h2)]], "tn", MM_TILES_TN, _ep_plain, (bf16,))
    (dh2,) = _mm("proj_dh", [[(dproj, w_cat_t)]], "nn", MM_TILES_LONG_K, _ep_plain, (bf16,))
    dx1, dsh2, a2 = _ada_bwd("mix_ada_bwd", x1, g_rows[1], m2[1], dh2, dx2)
    dmod2 = (dsh2, a2 * g_rows[1], dgt2)
    dg2 = a2 * (1.0 + m2[1])
    dx0, dffn1, dmod1, dg1 = _ffn_bwd("ffn1", dx1, r1, g_rows[0], m1, *ffn_w[0])

    big = dict(ffn=(dffn1, dffn2), w_cat_t=dw_cat_t, w_out=jnp.concatenate([dw_top, dw_bot], axis=0))
    small = dict(loss=loss, norm_g=jnp.concatenate([dg1, dg2, dg3], axis=0), final_norm=dfinal, fox_wn=dfox_wn,
                 gdn_wn=dgdn_wn, pg=pg, conv=dconv, mod=jnp.concatenate(list(dmod1) + list(dmod2) + list(dmod3), axis=1))
    return dx0, big, small


def _w_in_row_groups(nh):
    fw = nh * HEAD_DIM
    sizes = [3 * fw, nh, 3 * fw, nh, nh, fw]
    offs = [0]
    for sz in sizes:
        offs.append(offs[-1] + sz)
    return [(offs[i], offs[i + 1]) for i in range(len(sizes))]


def _build_w_cat_t(w_in_t, nh):
    gq, gf, gg, ga, gb, gz = _w_in_row_groups(nh)
    d = w_in_t.shape[1]
    rows = lambda r: w_in_t[r[0]:r[1]]
    pad = jnp.zeros((LANES - 3 * nh, d), w_in_t.dtype)
    return jnp.concatenate([rows(gq), rows(gg), rows(gz), rows(gf), rows(ga), rows(gb), pad], axis=0)


def _split_dw_cat_t(dw_cat_t, nh):
    fw = nh * HEAD_DIM
    o = 7 * fw
    return jnp.concatenate([dw_cat_t[:3 * fw], dw_cat_t[o:o + nh], dw_cat_t[3 * fw:6 * fw], dw_cat_t[o + nh:o + 2 * nh],
                            dw_cat_t[o + 2 * nh:o + 3 * nh], dw_cat_t[6 * fw:7 * fw]], axis=0)


def _head_params(fox_f_bias, gdn_dt_bias, gdn_a_log, nh):
    z = jnp.zeros((8, LANES), f32)
    z = z.at[0, 0:nh].set(fox_f_bias.reshape(nh))
    z = z.at[1, nh:2 * nh].set(gdn_dt_bias.reshape(nh))
    z = z.at[2, nh:2 * nh].set(gdn_a_log.reshape(nh))
    return z


ANY = pl.BlockSpec(memory_space=pl.ANY)
IN_VMEM = pl.BlockSpec(memory_space=pltpu.VMEM)
N_OTHER_CHIPS = 3


def _place():
    x, y, c = lax.axis_index("x"), lax.axis_index("y"), lax.axis_index("c")
    chips = [(1 - x, y), (x, 1 - y), (1 - x, 1 - y)]
    return x, y, c, chips


def _allgather8(name, v):
    r, n = v.shape

    def body(v_ref, out_ref, send_sems, recv_sems, local_sem):
        x, y, c, _ = _place()
        me = 4 * x + 2 * y + c
        mine = pltpu.make_async_copy(v_ref, out_ref.at[me], local_sem)
        mine.start()
        copies = []
        for k in range(1, 8):
            fx, fy, fc = (k >> 2) & 1, (k >> 1) & 1, k & 1
            peer = (x + fx - 2 * x * fx, y + fy - 2 * y * fy, c + fc - 2 * c * fc)
            cp = pltpu.make_async_remote_copy(src_ref=v_ref, dst_ref=out_ref.at[me], send_sem=send_sems.at[k - 1],
                                              recv_sem=recv_sems.at[k - 1], device_id=peer, device_id_type=MESH)
            cp.start()
            copies.append(cp)
        for cp in copies:
            cp.wait()
        mine.wait()

    return pl.pallas_call(
        body, name=name, in_specs=[IN_VMEM], out_specs=IN_VMEM, out_shape=jax.ShapeDtypeStruct((8, r, n), v.dtype),
        scratch_shapes=[pltpu.SemaphoreType.DMA((7,)), pltpu.SemaphoreType.DMA((7,)), pltpu.SemaphoreType.DMA],
        compiler_params=pltpu.CompilerParams(vmem_limit_bytes=VMEM_LIMIT_V7X))(v)


def _gather_weights(halves):
    nw = len(halves)

    def body(*refs):
        ins, outs = refs[:nw], refs[nw:2 * nw]
        ici_send, ici_recv, d2d_send, d2d_recv = refs[2 * nw:]
        x, y, c, chips = _place()
        s = 2 * x + y
        sib = (x, y, 1 - c)
        sends = []
        for w in range(nw):
            for j, (cx, cy) in enumerate(chips):
                cp = pltpu.make_async_remote_copy(
                    src_ref=ins[w].at[c], dst_ref=outs[w].at[c, s], send_sem=ici_send.at[w * 3 + j],
                    recv_sem=ici_recv.at[w * 3 + j], device_id=(cx, cy, c), device_id_type=MESH)
                cp.start()
                sends.append(cp)
        fwds = []
        for w in range(nw):
            for j, (cx, cy) in enumerate(chips):
                sj = 2 * cx + cy
                landed = outs[w].at[c, sj]
                pltpu.make_async_remote_copy(src_ref=ins[w].at[c], dst_ref=landed, send_sem=ici_send.at[w * 3 + j],
                                             recv_sem=ici_recv.at[w * 3 + j], device_id=(cx, cy, c),
                                             device_id_type=MESH).wait_recv()
                cp = pltpu.make_async_remote_copy(src_ref=landed, dst_ref=landed, send_sem=d2d_send.at[w * 3 + j],
                                                  recv_sem=d2d_recv.at[w * 3 + j], device_id=sib, device_id_type=MESH)
                cp.start()
                fwds.append(cp)
        for w in range(nw):
            for j, (cx, cy) in enumerate(chips):
                theirs = outs[w].at[1 - c, 2 * cx + cy]
                pltpu.make_async_remote_copy(src_ref=theirs, dst_ref=theirs, send_sem=d2d_send.at[w * 3 + j],
                                             recv_sem=d2d_recv.at[w * 3 + j], device_id=sib, device_id_type=MESH).wait_recv()
        for cp in sends + fwds:
            cp.wait_send()

    n3 = nw * N_OTHER_CHIPS
    return pl.pallas_call(
        body, name="gather_weights", in_specs=[ANY] * nw, out_specs=[ANY] * nw,
        out_shape=[jax.ShapeDtypeStruct((2, 4) + h.shape[1:], h.dtype) for h in halves],
        scratch_shapes=[pltpu.SemaphoreType.DMA((n3,)) for _ in range(4)],
    )(*halves)


def _send_to_sibling(name, srcs, other_half):
    nw = len(srcs)

    def body(*refs):
        ins, outs = refs[:nw], refs[nw:2 * nw]
        send_sems, recv_sems = refs[2 * nw:]
        x, y, c, _ = _place()
        cps = []
        for w in range(nw):
            cp = pltpu.make_async_remote_copy(src_ref=ins[w].at[1 - c] if other_half else ins[w], dst_ref=outs[w],
                                              send_sem=send_sems.at[w], recv_sem=recv_sems.at[w],
                                              device_id=(x, y, 1 - c), device_id_type=MESH)
            cp.start()
            cps.append(cp)
        for cp in cps:
            cp.wait()

    return pl.pallas_call(
        body, name=name, in_specs=[ANY] * nw, out_specs=[ANY] * nw,
        out_shape=[jax.ShapeDtypeStruct(a.shape[1:] if other_half else a.shape, a.dtype) for a in srcs],
        scratch_shapes=[pltpu.SemaphoreType.DMA((nw,)), pltpu.SemaphoreType.DMA((nw,))],
    )(*srcs)


def _send_to_chips(partials):
    nw = len(partials)

    def body(*refs):
        ins, outs = refs[:nw], refs[nw:2 * nw]
        send_sems, recv_sems = refs[2 * nw:]
        x, y, c, chips = _place()
        cps = []
        for w in range(nw):
            for j, (cx, cy) in enumerate(chips):
                cp = pltpu.make_async_remote_copy(src_ref=ins[w].at[2 * cx + cy], dst_ref=outs[w].at[j],
                                                  send_sem=send_sems.at[w * 3 + j], recv_sem=recv_sems.at[w * 3 + j],
                                                  device_id=(cx, cy, c), device_id_type=MESH)
                cp.start()
                cps.append(cp)
        for cp in cps:
            cp.wait()

    n3 = nw * N_OTHER_CHIPS
    return pl.pallas_call(
        body, name="rs_to_chips", in_specs=[ANY] * nw, out_specs=[ANY] * nw,
        out_shape=[jax.ShapeDtypeStruct((3,) + a.shape[1:], a.dtype) for a in partials],
        scratch_shapes=[pltpu.SemaphoreType.DMA((n3,)), pltpu.SemaphoreType.DMA((n3,))],
    )(*partials)


def _add_pair(name, g, recv, c):
    _, nchip, r, d = g.shape
    tr = _tile(r, 512, 16)

    def body(c_ref, g_ref, r_ref, o_ref):
        o_ref[...] = (g_ref[...].astype(f32) + r_ref[...].astype(f32)).astype(o_ref.dtype)

    gs = pltpu.PrefetchScalarGridSpec(
        num_scalar_prefetch=1, grid=(nchip, r // tr),
        in_specs=[pl.BlockSpec((None, None, tr, d), lambda t, i, cr: (cr[0], t, i, 0)),
                  pl.BlockSpec((None, tr, d), lambda t, i, cr: (t, i, 0))],
        out_specs=pl.BlockSpec((None, tr, d), lambda t, i, cr: (t, i, 0)))
    return pl.pallas_call(body, name=name, grid_spec=gs, out_shape=jax.ShapeDtypeStruct((nchip, r, d), bf16),
                          compiler_params=_cp("parallel", "parallel"))(c.reshape(1).astype(jnp.int32), g, recv)


def _add_chips(name, p, recv, s_chip):
    _, r, d = p.shape
    tr = _tile(r, 512, 16)

    def body(s_ref, p_ref, r_ref, o_ref):
        o_ref[...] = ((p_ref[...].astype(f32) + r_ref[0].astype(f32)) + r_ref[1].astype(f32)) + r_ref[2].astype(f32)

    gs = pltpu.PrefetchScalarGridSpec(
        num_scalar_prefetch=1, grid=(r // tr,),
        in_specs=[pl.BlockSpec((None, tr, d), lambda i, sr: (sr[0], i, 0)),
                  pl.BlockSpec((3, tr, d), lambda i, sr: (0, i, 0))],
        out_specs=pl.BlockSpec((tr, d), lambda i, sr: (i, 0)))
    return pl.pallas_call(body, name=name, grid_spec=gs, out_shape=jax.ShapeDtypeStruct((r, d), f32),
                          compiler_params=_cp("parallel"))(s_chip.reshape(1).astype(jnp.int32), p, recv)


def _reduce_scatter(grads, c, s_chip):
    names = [str(i) for i in range(len(grads))]
    from_sib = _send_to_sibling("rs_to_sibling", grads, True)
    partial = [_add_pair("rs_add_pair" + n, g, r, c) for n, g, r in zip(names, grads, from_sib)]
    from_chips = _send_to_chips(partial)
    mine = [_add_chips("rs_add_chips" + n, p, r, s_chip) for n, p, r in zip(names, partial, from_chips)]
    theirs = _send_to_sibling("rs_exchange_halves", mine, False)
    return [jnp.where(c == 0, jnp.stack([a, b]), jnp.stack([b, a])) for a, b in zip(mine, theirs)]


def _sum_devices(v):
    n = v.shape[2]

    def body(v_ref, o_ref):
        t = v_ref[0]
        for k in range(1, 8):
            t = t + v_ref[k]
        o_ref[...] = t

    return pl.pallas_call(body, name="sum_devices", out_shape=jax.ShapeDtypeStruct((1, n), f32))(v)


def _silu_rows(v):
    def body(v_ref, o_ref):
        t = v_ref[...]
        o_ref[...] = t * _sigmoid(t)

    return pl.pallas_call(body, name="silu_cond", out_shape=jax.ShapeDtypeStruct(v.shape, f32))(v)


ADAMW_BLOCK_ELEMS = 600 * 1024


def _adamw(name, w, g, m, v):
    r, cdim = w.shape
    tr = _tile(r, max(8, min(256, (ADAMW_BLOCK_ELEMS // cdim) // 8 * 8)), 8)
    c1 = 1.0 - ADAM_B1 ** ADAM_STEP
    c2 = 1.0 - ADAM_B2 ** ADAM_STEP

    def body(w_ref, g_ref, m_ref, v_ref, d_ref, mo_ref, vo_ref):
        gv = g_ref[...]
        mn = ADAM_B1 * m_ref[...] + (1.0 - ADAM_B1) * gv
        vn = ADAM_B2 * v_ref[...] + (1.0 - ADAM_B2) * (gv * gv)
        d_ref[...] = -ADAM_LR * ((mn / c1) / (jnp.sqrt(vn / c2) + ADAM_EPS) + ADAM_WD * w_ref[...])
        mo_ref[...] = mn
        vo_ref[...] = vn

    blk = pl.BlockSpec((tr, cdim), lambda i: (i, 0))
    return pl.pallas_call(body, name=name, grid=(r // tr,), in_specs=[blk] * 4, out_specs=[blk] * 3,
                          out_shape=[jax.ShapeDtypeStruct((r, cdim), f32)] * 3, compiler_params=_cp("parallel"))(w, g, m, v)


def _ep_bias(accs, ex):
    return (accs[0] + ex[0],)


def kernel(x, c, ada_w, ada_b, norm_g, ffn_w_gate, ffn_w_up, ffn_w_down, w_in, w_out, fox_f_bias, fox_out_norm, gdn_conv, gdn_A_log, gdn_dt_bias, gdn_out_norm, final_norm, loss_target, m_ada_w, m_ada_b, m_norm_g, m_ffn_w_gate, m_ffn_w_up, m_ffn_w_down, m_w_in, m_w_out, m_fox_f_bias, m_fox_out_norm, m_gdn_conv, m_gdn_A_log, m_gdn_dt_bias, m_gdn_out_norm, m_final_norm, v_ada_w, v_ada_b, v_norm_g, v_ffn_w_gate, v_ffn_w_up, v_ffn_w_down, v_w_in, v_w_out, v_fox_f_bias, v_fox_out_norm, v_gdn_conv, v_gdn_A_log, v_gdn_dt_bias, v_gdn_out_norm, v_final_norm):
    ix, iy, ic = lax.axis_index("x"), lax.axis_index("y"), lax.axis_index("c")
    s_chip = 2 * ix + iy
    me = 4 * ix + 2 * iy + ic
    _, s, d = x.shape
    nh = d // (2 * HEAD_DIM)
    fw = nh * HEAD_DIM
    ncol = ada_w.shape[2]
    dg_sh = norm_g.shape[2]
    cv_sh = gdn_conv.shape[2]
    ff_sh = ffn_w_gate.shape[3]
    in_sh = w_in.shape[2]
    in_pad = -(-in_sh // 32) * 32
    out_sh = w_out.shape[1]
    per_chip = lambda a, t: a[2 * t]

    pack0 = jnp.concatenate([_silu_rows(c), norm_g[0].reshape(1, 3 * dg_sh), gdn_conv[0].reshape(1, CONV_W * cv_sh)], axis=1)
    got0 = _allgather8("gather_cond", pack0)
    cond_all = got0[:, 0, :d]
    norm_g_full = jnp.concatenate([per_chip(got0, t)[0, d:d + 3 * dg_sh].reshape(3, dg_sh) for t in range(4)], axis=1)
    conv_full = jnp.concatenate([per_chip(got0, t)[0, d + 3 * dg_sh:].reshape(CONV_W, cv_sh) for t in range(4)], axis=1)

    ada_b_sh = lax.dynamic_slice_in_dim(ada_b, s_chip * ncol, ncol, axis=1)
    (mod_sh,) = _mm("ada_mod", [[(cond_all, ada_w[0])]], "nn", (8, 512, 2048), _ep_bias, (f32,), extras=((ada_b_sh, "n"),))
    mod_all = _allgather8("gather_mod", mod_sh)
    mod = jnp.concatenate([lax.dynamic_index_in_dim(per_chip(mod_all, t), me, axis=0, keepdims=True) for t in range(4)], axis=1)
    mods = [mod[:, i * d:(i + 1) * d] for i in range(9)]

    halves = [jnp.swapaxes(ffn_w_gate[0], 1, 2).astype(bf16), jnp.swapaxes(ffn_w_up[0], 1, 2).astype(bf16),
              ffn_w_down[0].astype(bf16),
              jnp.pad(w_in[0].T.astype(bf16), ((0, in_pad - in_sh), (0, 0))).reshape(2, in_pad // 2, d),
              w_out[0].astype(bf16).reshape(2, out_sh // 2, d)]
    g_wg, g_wu, g_wd, g_win, g_wo = [lax.dynamic_update_slice(g, h[:, None], (0, s_chip, 0, 0))
                                     for g, h in zip(_gather_weights(halves), halves)]
    ffn_w = [(g_wg[j].reshape(4 * ff_sh, d), g_wu[j].reshape(4 * ff_sh, d), g_wd[j].reshape(4 * ff_sh, d)) for j in range(2)]
    w_cat_t = _build_w_cat_t(jnp.swapaxes(g_win, 0, 1).reshape(4, in_pad, d)[:, :in_sh].reshape(4 * in_sh, d), nh)
    w_out_full = jnp.swapaxes(g_wo, 0, 1).reshape(4 * out_sh, d)
    prm = _head_params(fox_f_bias, gdn_dt_bias, gdn_A_log, nh)

    dx0, big, small = _local_step(x[0], loss_target[0], mods, norm_g_full, final_norm.reshape(1, d), ffn_w, w_cat_t,
                                  w_out_full, prm, fox_out_norm, gdn_out_norm, conv_full, nh)

    pack1 = jnp.concatenate([small["loss"], small["norm_g"].reshape(1, 3 * d), small["final_norm"], small["fox_wn"],
                             small["gdn_wn"], small["pg"][0:1], small["pg"][1:2], small["conv"].reshape(1, CONV_W * 3 * fw),
                             small["mod"]], axis=1)
    got1 = _allgather8("gather_small_grads", pack1)
    tot = _sum_devices(got1)
    o = [0]

    def take(n):
        o[0] += n
        return tot[:, o[0] - n:o[0]]

    loss = take(LANES)[0, 0]
    g_norm_g = lax.dynamic_slice_in_dim(take(3 * d).reshape(3, d), s_chip * dg_sh, dg_sh, axis=1)[None]
    g_final = take(d).reshape(d)
    g_fox_wn = take(HEAD_DIM)
    g_gdn_wn = take(HEAD_DIM)
    pg0, pg1 = take(LANES), take(LANES)
    g_fbias, g_dtb, g_alog = pg0[:, 0:nh], pg0[:, nh:2 * nh], pg1[:, nh:2 * nh]
    g_conv = lax.dynamic_slice_in_dim(take(CONV_W * 3 * fw).reshape(CONV_W, 3 * fw), s_chip * cv_sh, cv_sh, axis=1)[None]
    g_ada_b = take(9 * d)
    dmod_all = got1[:, 0, o[0] - 9 * d:o[0]]
    dmod_sh = lax.dynamic_slice_in_dim(dmod_all, s_chip * ncol, ncol, axis=1)
    (g_ada_w,) = _mm("ada_dw", [[(cond_all, dmod_sh)]], "tn", (2048, 512, 8), _ep_plain, (f32,))

    stack = lambda k: jnp.stack([big["ffn"][0][k], big["ffn"][1][k]]).reshape(2, 4, ff_sh, d)
    dw_in_t = jnp.pad(_split_dw_cat_t(big["w_cat_t"], nh).reshape(4, in_sh, d), ((0, 0), (0, in_pad - in_sh), (0, 0)))
    grads = [stack(0), stack(1), stack(2), jnp.swapaxes(dw_in_t.reshape(4, 2, in_pad // 2, d), 0, 1),
             jnp.swapaxes(big["w_out"].reshape(4, 2, out_sh // 2, d), 0, 1)]
    r_wg, r_wu, r_wd, r_win, r_wo = _reduce_scatter(grads, ic, s_chip)
    g_ffn_gate = jnp.swapaxes(r_wg, 1, 2)[None]
    g_ffn_up = jnp.swapaxes(r_wu, 1, 2)[None]
    g_ffn_down = r_wd[None]
    g_w_in = r_win.reshape(in_pad, d)[:in_sh].T[None]
    g_w_out = r_wo.reshape(out_sh, d)[None]

    def upd(name, w, g, m, v):
        shp = w.shape
        two = lambda a: a.reshape(-1, shp[-1])
        return tuple(t.reshape(shp) for t in _adamw(name, two(w), two(g), two(m), two(v)))

    big_upd = [upd("adamw_ada_w", ada_w, g_ada_w[None], m_ada_w, v_ada_w),
               upd("adamw_ffn_gate", ffn_w_gate, g_ffn_gate, m_ffn_w_gate, v_ffn_w_gate),
               upd("adamw_ffn_up", ffn_w_up, g_ffn_up, m_ffn_w_up, v_ffn_w_up),
               upd("adamw_ffn_down", ffn_w_down, g_ffn_down, m_ffn_w_down, v_ffn_w_down),
               upd("adamw_w_in", w_in, g_w_in, m_w_in, v_w_in),
               upd("adamw_w_out", w_out, g_w_out, m_w_out, v_w_out)]
    small_w = [ada_b, norm_g, fox_f_bias, fox_out_norm, gdn_conv, gdn_A_log, gdn_dt_bias, gdn_out_norm, final_norm]
    small_g = [g_ada_b, g_norm_g, g_fbias, g_fox_wn, g_conv, g_alog, g_dtb, g_gdn_wn, g_final]
    small_m = [m_ada_b, m_norm_g, m_fox_f_bias, m_fox_out_norm, m_gdn_conv, m_gdn_A_log, m_gdn_dt_bias, m_gdn_out_norm, m_final_norm]
    small_v = [v_ada_b, v_norm_g, v_fox_f_bias, v_fox_out_norm, v_gdn_conv, v_gdn_A_log, v_gdn_dt_bias, v_gdn_out_norm, v_final_norm]
    sizes = [a.size for a in small_w]
    npad = -sum(sizes) % LANES
    flat = lambda arrs, fill: jnp.concatenate([a.reshape(1, -1) for a in arrs] + [jnp.full((1, npad), fill, f32)], axis=1)
    sd, sm_, sv = _adamw("adamw_small", flat(small_w, 0.0), flat(small_g, 0.0), flat(small_m, 0.0), flat(small_v, 1.0))

    def unflat(t):
        out, off = [], 0
        for a, n in zip(small_w, sizes):
            out.append(t[0, off:off + n].reshape(a.shape))
            off += n
        return out

    small_g = [g.reshape(a.shape) for g, a in zip(small_g, small_w)]
    s_d, s_m, s_v = unflat(sd), unflat(sm_), unflat(sv)
    def order(bigs, smalls):
        return [bigs[0], smalls[0], smalls[1], bigs[1], bigs[2], bigs[3], bigs[4], bigs[5]] + list(smalls[2:])

    grads_out = order([g_ada_w[None], g_ffn_gate, g_ffn_up, g_ffn_down, g_w_in, g_w_out], small_g)
    deltas = order([u[0] for u in big_upd], s_d)
    new_m = order([u[1] for u in big_upd], s_m)
    new_v = order([u[2] for u in big_upd], s_v)
    return (loss, dx0[None], *grads_out, *deltas, *new_m, *new_v)
```

```python
import functools
import math

import jax
import jax.numpy as jnp
from jax import lax
from jax.experimental import pallas as pl
from jax.experimental.pallas import tpu as pltpu

f32 = jnp.float32
bf16 = jnp.bfloat16
HI = lax.Precision.HIGHEST
MESH = pl.DeviceIdType.MESH

EPS = 1e-6
HEAD_DIM = 128
LANES = 128
GDN_CHUNK = 64
CONV_W = 4
MACARON_W = 0.5
ADAM_LR, ADAM_B1, ADAM_B2, ADAM_EPS, ADAM_WD, ADAM_STEP = 0.001, 0.9, 0.999, 1e-08, 0.01, 10
VMEM_LIMIT_V7X = 56 * 1024 * 1024
NEG = -1e30

NN = (((1,), (0,)), ((), ()))
NT = (((1,), (1,)), ((), ()))
TN = (((0,), (0,)), ((), ()))


def _cp(*sem):
    return pltpu.CompilerParams(dimension_semantics=sem, vmem_limit_bytes=VMEM_LIMIT_V7X)


def _dotb(a, b, dn=NN):
    return lax.dot_general(a.astype(bf16), b.astype(bf16), dn, preferred_element_type=f32)


def _doth(a, b, dn=NN):
    return lax.dot_general(a.astype(f32), b.astype(f32), dn, precision=HI, preferred_element_type=f32)


def _sigmoid(x):
    return 1.0 / (1.0 + jnp.exp(-x))


def _softplus(x):
    return jnp.maximum(x, 0.0) + jnp.log(1.0 + jnp.exp(-jnp.abs(x)))


def _lane_col(blk, lane_idx):
    lane = lax.broadcasted_iota(jnp.int32, blk.shape, 1)
    return jnp.sum(jnp.where(lane == lane_idx, blk, 0.0), axis=1, keepdims=True)


def _tile(n, pref, mult=LANES):
    if n <= pref:
        return n
    t = (pref // mult) * mult
    while t >= mult:
        if n % t == 0:
            return t
        t -= mult
    return n


def _mm(name, groups, mode, tiles, epilogue, out_dtypes, extras=()):
    a0, b0 = groups[0][0]
    if mode == "nn":
        (m, k), n = a0.shape, b0.shape[1]
    elif mode == "nt":
        (m, k), n = a0.shape, b0.shape[0]
    else:
        (k, m), n = a0.shape, b0.shape[1]
    tm, tn, tk = _tile(m, tiles[0]), _tile(n, tiles[1]), _tile(k, tiles[2])
    nk = k // tk
    assert m % tm == 0 and n % tn == 0 and k % tk == 0, (name, m, n, k, tm, tn, tk)
    if mode == "nn":
        a_spec = pl.BlockSpec((tm, tk), lambda i, j, kk: (i, kk))
        b_spec = pl.BlockSpec((tk, tn), lambda i, j, kk: (kk, j))
        dn = NN
    elif mode == "nt":
        a_spec = pl.BlockSpec((tm, tk), lambda i, j, kk: (i, kk))
        b_spec = pl.BlockSpec((tn, tk), lambda i, j, kk: (j, kk))
        dn = NT
    else:
        a_spec = pl.BlockSpec((tk, tm), lambda i, j, kk: (kk, i))
        b_spec = pl.BlockSpec((tk, tn), lambda i, j, kk: (kk, j))
        dn = TN
    npairs = sum(len(g) for g in groups)
    nacc, nex, nout = len(groups), len(extras), len(out_dtypes)
    in_specs, args = [], []
    for g in groups:
        for a, b in g:
            in_specs += [a_spec, b_spec]
            args += [a, b]
    for arr, kind in extras:
        if kind == "mn":
            in_specs.append(pl.BlockSpec((tm, tn), lambda i, j, kk: (i, j)))
        else:
            in_specs.append(pl.BlockSpec((1, tn), lambda i, j, kk: (0, j)))
        args.append(arr)

    def body(*refs):
        ins = refs[: 2 * npairs]
        ex = refs[2 * npairs: 2 * npairs + nex]
        outs = refs[2 * npairs + nex: 2 * npairs + nex + nout]
        accs = refs[2 * npairs + nex + nout:]
        kk = pl.program_id(2)

        @pl.when(kk == 0)
        def _():
            for acc in accs:
                acc[...] = jnp.zeros_like(acc)

        p = 0
        for gi, g in enumerate(groups):
            t = None
            for _ in g:
                d = _dotb(ins[2 * p][...], ins[2 * p + 1][...], dn)
                t = d if t is None else t + d
                p += 1
            accs[gi][...] += t

        @pl.when(kk == nk - 1)
        def _():
            res = epilogue([acc[...] for acc in accs], [e[...] for e in ex])
            for o, r in zip(outs, res):
                o[...] = r.astype(o.dtype)

    return pl.pallas_call(
        body, name=name, grid=(m // tm, n // tn, nk),
        in_specs=in_specs,
        out_specs=[pl.BlockSpec((tm, tn), lambda i, j, kk: (i, j)) for _ in out_dtypes],
        out_shape=[jax.ShapeDtypeStruct((m, n), dt) for dt in out_dtypes],
        scratch_shapes=[pltpu.VMEM((tm, tn), f32) for _ in range(nacc)],
        compiler_params=_cp("parallel", "parallel", "arbitrary"),
    )(*args)


def _ep_plain(accs, ex):
    return (accs[0],)


def _ep_colscale(accs, ex):
    return (accs[0] * ex[0],)


def _ep_swiglu(accs, ex):
    gate, up = accs
    act = gate * _sigmoid(gate) * up
    return gate, up, act


def _ep_residual(accs, ex):
    x, gs = ex
    y = accs[0]
    return x + gs * y, y


def _ep_swiglu_bwd(accs, ex):
    gate, up = ex[0].astype(f32), ex[1].astype(f32)
    dact = accs[0]
    sg = _sigmoid(gate)
    silu = gate * sg
    act = silu * up
    dup = dact * silu
    dgate = dact * up * sg * (1.0 + gate * (1.0 - sg))
    return act, dgate, dup


def _row_tile(s):
    return _tile(s, 256, 8)


def _ada_in(name, x, g, shift, scale):
    s, d = x.shape
    tm = _row_tile(s)

    def body(x_ref, g_ref, sh_ref, sc_ref, h_ref):
        xv = x_ref[...]
        r = lax.rsqrt(jnp.mean(xv * xv, axis=-1, keepdims=True) + EPS)
        h_ref[...] = (xv * r * g_ref[...] * (1.0 + sc_ref[...]) + sh_ref[...]).astype(h_ref.dtype)

    row = pl.BlockSpec((1, d), lambda i: (0, 0))
    blk = pl.BlockSpec((tm, d), lambda i: (i, 0))
    return pl.pallas_call(body, name=name, grid=(s // tm,), in_specs=[blk, row, row, row], out_specs=blk,
                          out_shape=jax.ShapeDtypeStruct((s, d), bf16), compiler_params=_cp("parallel"))(x, g, shift, scale)


def _ada_bwd(name, x, g, scale, dh, dres):
    s, d = x.shape
    tm = _row_tile(s)

    def body(x_ref, g_ref, sc_ref, dh_ref, dres_ref, dx_ref, dsh_ref, a_ref):
        i = pl.program_id(0)

        @pl.when(i == 0)
        def _():
            dsh_ref[...] = jnp.zeros_like(dsh_ref)
            a_ref[...] = jnp.zeros_like(a_ref)

        xv = x_ref[...]
        dhv = dh_ref[...].astype(f32)
        r = lax.rsqrt(jnp.mean(xv * xv, axis=-1, keepdims=True) + EPS)
        n = xv * r
        dn = dhv * (g_ref[...] * (1.0 + sc_ref[...]))
        dx_ref[...] = dres_ref[...] + r * (dn - n * jnp.mean(dn * n, axis=-1, keepdims=True))
        dsh_ref[...] += jnp.sum(dhv, axis=0, keepdims=True)
        a_ref[...] += jnp.sum(dhv * n, axis=0, keepdims=True)

    row = pl.BlockSpec((1, d), lambda i: (0, 0))
    blk = pl.BlockSpec((tm, d), lambda i: (i, 0))
    return pl.pallas_call(
        body, name=name, grid=(s // tm,), in_specs=[blk, row, row, blk, blk], out_specs=[blk, row, row],
        out_shape=[jax.ShapeDtypeStruct((s, d), f32), jax.ShapeDtypeStruct((1, d), f32), jax.ShapeDtypeStruct((1, d), f32)],
        compiler_params=_cp("arbitrary"))(x, g, scale, dh, dres)


def _gate_bwd(name, dx, y, gs):
    s, d = dx.shape
    tm = _row_tile(s)

    def body(dx_ref, y_ref, gs_ref, dy_ref, dgs_ref):
        i = pl.program_id(0)

        @pl.when(i == 0)
        def _():
            dgs_ref[...] = jnp.zeros_like(dgs_ref)

        dxv = dx_ref[...]
        dy_ref[...] = (dxv * gs_ref[...]).astype(dy_ref.dtype)
        dgs_ref[...] += jnp.sum(dxv * y_ref[...].astype(f32), axis=0, keepdims=True)

    row = pl.BlockSpec((1, d), lambda i: (0, 0))
    blk = pl.BlockSpec((tm, d), lambda i: (i, 0))
    return pl.pallas_call(
        body, name=name, grid=(s // tm,), in_specs=[blk, blk, row], out_specs=[blk, row],
        out_shape=[jax.ShapeDtypeStruct((s, d), bf16), jax.ShapeDtypeStruct((1, d), f32)],
        compiler_params=_cp("arbitrary"))(dx, y, gs)


def _final_loss(x, fg, target):
    s, d = x.shape
    tm = _row_tile(s)

    def body(x_ref, g_ref, t_ref, loss_ref, dx_ref, dg_ref):
        i = pl.program_id(0)

        @pl.when(i == 0)
        def _():
            loss_ref[...] = jnp.zeros_like(loss_ref)
            dg_ref[...] = jnp.zeros_like(dg_ref)

        xv = x_ref[...]
        gv = g_ref[...]
        r = lax.rsqrt(jnp.mean(xv * xv, axis=-1, keepdims=True) + EPS)
        n = xv * r
        e = n * gv - t_ref[...]
        per_tok = jnp.mean(e * e, axis=-1, keepdims=True)
        loss_ref[...] += 0.5 * jnp.sum(per_tok, axis=0, keepdims=True) * jnp.ones((1, LANES), f32)
        dy = e * (1.0 / d)
        dg_ref[...] += jnp.sum(dy * n, axis=0, keepdims=True)
        dn = dy * gv
        dx_ref[...] = r * (dn - n * jnp.mean(dn * n, axis=-1, keepdims=True))

    row = pl.BlockSpec((1, d), lambda i: (0, 0))
    blk = pl.BlockSpec((tm, d), lambda i: (i, 0))
    return pl.pallas_call(
        body, name="final_loss", grid=(s // tm,), in_specs=[blk, row, blk],
        out_specs=[pl.BlockSpec((1, LANES), lambda i: (0, 0)), blk, row],
        out_shape=[jax.ShapeDtypeStruct((1, LANES), f32), jax.ShapeDtypeStruct((s, d), f32), jax.ShapeDtypeStruct((1, d), f32)],
        compiler_params=_cp("arbitrary"))(x, fg, target)


def _small_fwd(ps, prm, nh):
    s = ps.shape[0]
    tb = LANES

    def body(ps_ref, prm_ref, sm_ref, cum_ref, carry):
        i = pl.program_id(0)

        @pl.when(i == 0)
        def _():
            carry[...] = jnp.zeros_like(carry)

        x = ps_ref[...]
        lane = lax.broadcasted_iota(jnp.int32, x.shape, 1)
        fb, dtb, alog = prm_ref[0:1, :], prm_ref[1:2, :], prm_ref[2:3, :]
        logf = -_softplus(-(x + fb))
        glog = -jnp.exp(alog) * _softplus(x + dtb)
        beta = _sigmoid(x)
        sm = jnp.where(lane < nh, logf, jnp.where(lane < 2 * nh, glog, jnp.where(lane < 3 * nh, beta, 0.0)))
        sm_ref[...] = sm
        r = lax.broadcasted_iota(jnp.int32, (tb, tb), 0)
        c = lax.broadcasted_iota(jnp.int32, (tb, tb), 1)
        tril = (c <= r).astype(f32)
        cs = _doth(tril, sm) + carry[...]
        cum_ref[...] = cs
        carry[...] = cs[tb - 1:tb, :]

    blk = pl.BlockSpec((tb, LANES), lambda i: (i, 0))
    return pl.pallas_call(
        body, name="small_fwd", grid=(s // tb,),
        in_specs=[blk, pl.BlockSpec((8, LANES), lambda i: (0, 0))],
        out_specs=[blk, blk],
        out_shape=[jax.ShapeDtypeStruct((s, LANES), f32), jax.ShapeDtypeStruct((s, LANES), f32)],
        scratch_shapes=[pltpu.VMEM((1, LANES), f32)],
        compiler_params=_cp("arbitrary"))(ps, prm)


def _small_bwd(ps, prm, dsm, dcum, nh):
    s = ps.shape[0]
    tb = LANES
    nb = s // tb

    def body(ps_ref, prm_ref, dsm_ref, dct_ref, dps_ref, pg_ref, carry):
        i = pl.program_id(0)

        @pl.when(i == 0)
        def _():
            carry[...] = jnp.zeros_like(carry)
            pg_ref[...] = jnp.zeros_like(pg_ref)

        x = ps_ref[...]
        dsm = dsm_ref[...]
        lane = lax.broadcasted_iota(jnp.int32, x.shape, 1)
        fb, dtb, alog = prm_ref[0:1, :], prm_ref[1:2, :], prm_ref[2:3, :]
        r = lax.broadcasted_iota(jnp.int32, (tb, tb), 0)
        c = lax.broadcasted_iota(jnp.int32, (tb, tb), 1)
        triu = (c >= r).astype(f32)
        dlogf = _doth(triu, dct_ref[...]) + carry[...]
        carry[...] = dlogf[0:1, :]
        d_f = dlogf * _sigmoid(-(x + fb))
        nega = -jnp.exp(alog)
        xa = x + dtb
        glog = nega * _softplus(xa)
        d_a = dsm * nega * _sigmoid(xa)
        beta = _sigmoid(x)
        d_b = dsm * beta * (1.0 - beta)
        dps = jnp.where(lane < nh, d_f, jnp.where(lane < 2 * nh, d_a, jnp.where(lane < 3 * nh, d_b, 0.0)))
        dps_ref[...] = dps.astype(dps_ref.dtype)
        row0 = jnp.sum(dps, axis=0, keepdims=True)
        row1 = jnp.sum(jnp.where((lane >= nh) & (lane < 2 * nh), dsm * glog, 0.0), axis=0, keepdims=True)
        sub = lax.broadcasted_iota(jnp.int32, (8, LANES), 0)
        pg_ref[...] += jnp.where(sub == 0, row0, jnp.where(sub == 1, row1, 0.0))

    rev = pl.BlockSpec((tb, LANES), lambda i: (nb - 1 - i, 0))
    fix = pl.BlockSpec((8, LANES), lambda i: (0, 0))
    return pl.pallas_call(
        body, name="small_bwd", grid=(nb,),
        in_specs=[rev, fix, rev, rev],
        out_specs=[rev, fix],
        out_shape=[jax.ShapeDtypeStruct((s, LANES), bf16), jax.ShapeDtypeStruct((8, LANES), f32)],
        scratch_shapes=[pltpu.VMEM((1, LANES), f32)],
        compiler_params=_cp("arbitrary"))(ps, prm, dsm, dcum)


LOG2E = 1.4426950408889634
LN2 = 0.6931471805599453
AUG = 2 * HEAD_DIM
FOX_Q_SCALE = LOG2E / math.sqrt(HEAD_DIM)


def _split3(col):
    hi = col.astype(bf16).astype(f32)
    r1 = col - hi
    mid = r1.astype(bf16).astype(f32)
    lo = (r1 - mid).astype(bf16).astype(f32)
    return hi, mid, lo


def _aug_block(rows, terms, terms_at, ones_at=None):
    lane = lax.broadcasted_iota(jnp.int32, (rows, LANES), 1)
    blk = jnp.zeros((rows, LANES), f32) if ones_at is None else jnp.where((lane >= ones_at) & (lane < ones_at + 3), 1.0, 0.0)
    for i, t in enumerate(terms):
        blk = jnp.where(lane == terms_at + i, t, blk)
    return blk


def _fox_aug(qkv, cum, nh):
    s = qkv.shape[0]
    tm = _row_tile(s)

    def body(q_ref, k_ref, v_ref, cum_ref, qa_ref, ka_ref, va_ref):
        h = pl.program_id(1)
        c2 = _lane_col(cum_ref[...], h) * LOG2E
        hi, mid, lo = _split3(c2)
        qa_ref[:, :HEAD_DIM] = q_ref[...]
        qa_ref[:, HEAD_DIM:] = _aug_block(tm, (hi, mid, lo), 0, 3).astype(bf16)
        ka_ref[:, :HEAD_DIM] = k_ref[...]
        ka_ref[:, HEAD_DIM:] = _aug_block(tm, (-hi, -mid, -lo), 3, 0).astype(bf16)
        va_ref[:, :HEAD_DIM] = v_ref[...]
        va_ref[:, HEAD_DIM:] = _aug_block(tm, (), 0, 0).astype(bf16)

    ab = pl.BlockSpec((tm, AUG), lambda i, h: (i, h))
    return pl.pallas_call(
        body, name="fox_aug", grid=(s // tm, nh),
        in_specs=[pl.BlockSpec((tm, HEAD_DIM), lambda i, h: (i, h)), pl.BlockSpec((tm, HEAD_DIM), lambda i, h: (i, nh + h)),
                  pl.BlockSpec((tm, HEAD_DIM), lambda i, h: (i, 2 * nh + h)), pl.BlockSpec((tm, LANES), lambda i, h: (i, 0))],
        out_specs=[ab, ab, ab], out_shape=[jax.ShapeDtypeStruct((s, nh * AUG), bf16)] * 3,
        compiler_params=_cp("parallel", "parallel"))(qkv, qkv, qkv, cum)


def _fox_fwd(qa, ka, qkv, wn, nh, tq):
    s = qa.shape[0]
    fw = nh * HEAD_DIM

    def body(qa_ref, ka_ref, v_ref, wn_ref, o_ref, on_ref, lse_ref):
        i = pl.program_id(1)
        q = qa_ref[...]

        def logits(j):
            return _dotb(q, ka_ref[pl.ds(pl.multiple_of(j * tq, tq), tq), :], NT)

        def softmax_tile(t, j, m, l):
            mn = jnp.maximum(m, jnp.max(t, axis=1, keepdims=True))
            p = jnp.exp2(t - mn)
            alpha = jnp.exp2(m - mn)
            pv = _dotb(p, v_ref[pl.ds(pl.multiple_of(j * tq, tq), tq), :])
            return mn, alpha * l + jnp.sum(p, axis=1, keepdims=True), alpha, pv

        def step(j, carry):
            t, m, l, acc, alpha_prev, pv_prev = carry
            t_next = logits(j + 1)
            acc = alpha_prev * acc + pv_prev
            m, l, alpha, pv = softmax_tile(t, j, m, l)
            return t_next, m, l, acc, alpha, pv

        zero = (lax.broadcasted_iota(jnp.int32, (tq, HEAD_DIM), 0) + lax.broadcasted_iota(jnp.int32, (tq, HEAD_DIM), 1)).astype(f32) * 0.0
        zcol = lax.broadcasted_iota(jnp.int32, (tq, 1), 0).astype(f32) * 0.0
        init = (logits(0), zcol + NEG, zcol, zero, zcol + 1.0, zero)
        t, m, l, acc, alpha_prev, pv_prev = lax.fori_loop(0, i, step, init)
        acc = alpha_prev * acc + pv_prev
        rows = lax.broadcasted_iota(jnp.int32, (tq, tq), 0)
        cols = lax.broadcasted_iota(jnp.int32, (tq, tq), 1)
        m, l, alpha, pv = softmax_tile(jnp.where(cols <= rows, t, NEG), i, m, l)
        o = (alpha * acc + pv) / l
        o_ref[...] = o
        lse_ref[0] = m + jnp.log2(l)
        r = lax.rsqrt(jnp.mean(o * o, axis=-1, keepdims=True) + EPS)
        on_ref[...] = (o * r * wn_ref[...]).astype(on_ref.dtype)

    hb = pl.BlockSpec((tq, HEAD_DIM), lambda h, i: (i, h))
    return pl.pallas_call(
        body, name="fox_fwd", grid=(nh, s // tq),
        in_specs=[pl.BlockSpec((tq, AUG), lambda h, i: (i, h)), pl.BlockSpec((s, AUG), lambda h, i: (0, h)),
                  pl.BlockSpec((s, HEAD_DIM), lambda h, i: (0, 2 * nh + h)), pl.BlockSpec((1, HEAD_DIM), lambda h, i: (0, 0))],
        out_specs=[hb, hb, pl.BlockSpec((1, tq, 1), lambda h, i: (h, i, 0))],
        out_shape=[jax.ShapeDtypeStruct((s, fw), f32), jax.ShapeDtypeStruct((s, fw), bf16), jax.ShapeDtypeStruct((nh, s, 1), f32)],
        compiler_params=_cp("parallel", "parallel"))(qa, ka, qkv, wn)


def _fox_post_bwd(don, o, lse2, cum, qkv, wn, nh):
    s, fw = o.shape
    tm = _row_tile(s)

    def body(don_ref, o_ref, lse_ref, cum_ref, q_ref, wn_ref, qb_ref, doa_ref, dwn_ref):
        i = pl.program_id(0)
        h = pl.program_id(1)

        @pl.when((i == 0) & (h == 0))
        def _():
            dwn_ref[...] = jnp.zeros_like(dwn_ref)

        o = o_ref[...]
        don = don_ref[...].astype(f32)
        r = lax.rsqrt(jnp.mean(o * o, axis=-1, keepdims=True) + EPS)
        n = o * r
        dwn_ref[...] += jnp.sum(don * n, axis=0, keepdims=True)
        dn = don * wn_ref[...]
        do = r * (dn - n * jnp.mean(dn * n, axis=-1, keepdims=True))
        delta = jnp.sum(do * o, axis=-1, keepdims=True)
        a2 = _lane_col(cum_ref[...], h) * LOG2E - lse_ref[0]
        qb_ref[:, :HEAD_DIM] = q_ref[...]
        qb_ref[:, HEAD_DIM:] = _aug_block(tm, _split3(a2), 0, 3).astype(bf16)
        doa_ref[:, :HEAD_DIM] = do.astype(bf16)
        doa_ref[:, HEAD_DIM:] = _aug_block(tm, _split3(-delta), 0).astype(bf16)

    hb = pl.BlockSpec((tm, HEAD_DIM), lambda i, h: (i, h))
    ab = pl.BlockSpec((tm, AUG), lambda i, h: (i, h))
    return pl.pallas_call(
        body, name="fox_post_bwd", grid=(s // tm, nh),
        in_specs=[hb, hb, pl.BlockSpec((1, tm, 1), lambda i, h: (h, i, 0)), pl.BlockSpec((tm, LANES), lambda i, h: (i, 0)),
                  hb, pl.BlockSpec((1, HEAD_DIM), lambda i, h: (0, 0))],
        out_specs=[ab, ab, pl.BlockSpec((1, HEAD_DIM), lambda i, h: (0, 0))],
        out_shape=[jax.ShapeDtypeStruct((s, nh * AUG), bf16), jax.ShapeDtypeStruct((s, nh * AUG), bf16),
                   jax.ShapeDtypeStruct((1, HEAD_DIM), f32)],
        compiler_params=_cp("arbitrary", "arbitrary"))(don, o, lse2, cum, qkv, wn)


def _fox_bwd(qb, doa, ka, va, nh, tq):
    s = qb.shape[0]
    nq = s // tq
    fw = nh * HEAD_DIM

    def body(qb_ref, doa_ref, ka_ref, va_ref, dqx_ref, dkx_ref, dv_ref):
        j = pl.program_id(1)

        @pl.when(j == 0)
        def _():
            dqx_ref[...] = jnp.zeros_like(dqx_ref)

        kj = ka_ref[...]
        vj = va_ref[...]

        def products(i):
            off = pl.multiple_of(jnp.minimum(i, nq - 1) * tq, tq)
            return _dotb(qb_ref[pl.ds(off, tq), :], kj, NT), _dotb(doa_ref[pl.ds(off, tq), :], vj, NT)

        def tile(i, p, dpd, dk, dv):
            off = pl.multiple_of(i * tq, tq)
            ds = (p * dpd).astype(bf16)
            dv = dv + _dotb(p, doa_ref[pl.ds(off, tq), :HEAD_DIM], TN)
            dk = dk + _dotb(ds, qb_ref[pl.ds(off, tq), :], TN)
            dqx_ref[pl.ds(off, tq), :] += _dotb(ds, kj)
            return dk, dv

        t, dpd = products(j)
        t_next, dpd_next = products(j + 1)
        rows = lax.broadcasted_iota(jnp.int32, (tq, tq), 0)
        cols = lax.broadcasted_iota(jnp.int32, (tq, tq), 1)
        dk, dv = tile(j, jnp.where(cols <= rows, jnp.exp2(t), 0.0), dpd, jnp.zeros((tq, AUG), f32), jnp.zeros((tq, HEAD_DIM), f32))

        def step(i, carry):
            t, dpd, dk, dv = carry
            t_next, dpd_next = products(i + 1)
            dk, dv = tile(i, jnp.exp2(t), dpd, dk, dv)
            return t_next, dpd_next, dk, dv

        _, _, dk, dv = lax.fori_loop(j + 1, nq, step, (t_next, dpd_next, dk, dv))
        dkx_ref[...] = dk
        dv_ref[...] = dv.astype(dv_ref.dtype)

    panel = pl.BlockSpec((s, AUG), lambda h, j: (0, h))
    blk = pl.BlockSpec((tq, AUG), lambda h, j: (j, h))
    return pl.pallas_call(
        body, name="fox_bwd", grid=(nh, nq), in_specs=[panel, panel, blk, blk],
        out_specs=[panel, blk, pl.BlockSpec((tq, HEAD_DIM), lambda h, j: (j, h))],
        out_shape=[jax.ShapeDtypeStruct((s, nh * AUG), f32), jax.ShapeDtypeStruct((s, nh * AUG), f32),
                   jax.ShapeDtypeStruct((s, fw), bf16)],
        compiler_params=_cp("parallel", "arbitrary"))(qb, doa, ka, va)


def _fox_unpack(dqx, dkx, nh):
    s = dqx.shape[0]
    fw = nh * HEAD_DIM
    tm = _row_tile(s)

    def body(dqx_ref, dkx_ref, dq_ref, dk_ref, dcum_ref):
        h = pl.program_id(1)

        @pl.when(h == 0)
        def _():
            dcum_ref[...] = jnp.zeros_like(dcum_ref)

        dq_ref[...] = (dqx_ref[:, :HEAD_DIM] * (HEAD_DIM ** -0.5)).astype(dq_ref.dtype)
        dk_ref[...] = (dkx_ref[:, :HEAD_DIM] * LN2).astype(dk_ref.dtype)
        d = _lane_col(dqx_ref[:, HEAD_DIM:], 0) - _lane_col(dkx_ref[:, HEAD_DIM:], 3)
        lane = lax.broadcasted_iota(jnp.int32, (tm, LANES), 1)
        dcum_ref[...] += jnp.where(lane == h, d, 0.0)

    ab = pl.BlockSpec((tm, AUG), lambda i, h: (i, h))
    hb = pl.BlockSpec((tm, HEAD_DIM), lambda i, h: (i, h))
    return pl.pallas_call(
        body, name="fox_unpack", grid=(s // tm, nh), in_specs=[ab, ab],
        out_specs=[hb, hb, pl.BlockSpec((tm, LANES), lambda i, h: (i, 0))],
        out_shape=[jax.ShapeDtypeStruct((s, fw), bf16), jax.ShapeDtypeStruct((s, fw), bf16), jax.ShapeDtypeStruct((s, LANES), f32)],
        compiler_params=_cp("parallel", "arbitrary"))(dqx, dkx)


def _conv_pre(xx, w, tm):
    pre = None
    for k in range(CONV_W):
        sh = CONV_W - 1 - k
        t = (pltpu.roll(xx, sh, 0) if sh else xx)[8:, :] * w[k:k + 1, :]
        pre = t if pre is None else pre + t
    return pre


def _gdn_pre(x, w, nh):
    s, cw = x.shape
    tm = _row_tile(s)
    fw = nh * HEAD_DIM

    def body(x_ref, prev_ref, w_ref, y_ref):
        i = pl.program_id(0)
        j = pl.program_id(1)
        for h in range(nh):
            sl = slice(h * HEAD_DIM, (h + 1) * HEAD_DIM)
            prev = jnp.where(i == 0, 0.0, prev_ref[:, sl])
            pre = _conv_pre(jnp.concatenate([prev, x_ref[:, sl]], axis=0), w_ref[:, sl], tm)
            y = pre * _sigmoid(pre)
            yn = y * lax.rsqrt(jnp.sum(y * y, axis=-1, keepdims=True) + EPS)
            y_ref[:, sl] = jnp.where(j < 2, yn, y)

    return pl.pallas_call(
        body, name="gdn_pre", grid=(s // tm, cw // fw),
        in_specs=[pl.BlockSpec((tm, fw), lambda i, j: (i, j)),
                  pl.BlockSpec((8, fw), lambda i, j: (jnp.maximum(i * (tm // 8) - 1, 0), j)),
                  pl.BlockSpec((CONV_W, fw), lambda i, j: (0, j))],
        out_specs=pl.BlockSpec((tm, fw), lambda i, j: (i, j)),
        out_shape=jax.ShapeDtypeStruct((s, cw), f32),
        compiler_params=_cp("parallel", "parallel"))(x, x, w)


def _gdn_pre_bwd(x, w, dyn, nh):
    s, cw = x.shape
    tm = _row_tile(s)
    fw = nh * HEAD_DIM

    def body(x_ref, prev_ref, w_ref, dyn_ref, dpre_ref):
        i = pl.program_id(0)
        j = pl.program_id(1)
        for h in range(nh):
            sl = slice(h * HEAD_DIM, (h + 1) * HEAD_DIM)
            prev = jnp.where(i == 0, 0.0, prev_ref[:, sl])
            pre = _conv_pre(jnp.concatenate([prev, x_ref[:, sl]], axis=0), w_ref[:, sl], tm)
            sg = _sigmoid(pre)
            y = pre * sg
            dyn = dyn_ref[:, sl]
            r = lax.rsqrt(jnp.sum(y * y, axis=-1, keepdims=True) + EPS)
            yn = y * r
            dy_n = r * (dyn - yn * jnp.sum(dyn * yn, axis=-1, keepdims=True))
            dy = jnp.where(j < 2, dy_n, dyn)
            dpre_ref[:, sl] = dy * sg * (1.0 + pre * (1.0 - sg))

    hb = pl.BlockSpec((tm, fw), lambda i, j: (i, j))
    return pl.pallas_call(
        body, name="gdn_pre_bwd", grid=(s // tm, cw // fw),
        in_specs=[hb, pl.BlockSpec((8, fw), lambda i, j: (jnp.maximum(i * (tm // 8) - 1, 0), j)),
                  pl.BlockSpec((CONV_W, fw), lambda i, j: (0, j)), hb],
        out_specs=hb, out_shape=jax.ShapeDtypeStruct((s, cw), f32),
        compiler_params=_cp("parallel", "parallel"))(x, x, w, dyn)


def _conv_bwd(x, w, dpre, nh):
    s, cw = x.shape
    tm = _row_tile(s)
    fw = nh * HEAD_DIM
    ni = s // tm

    def body(x_ref, prev_ref, w_ref, dp_ref, nxt_ref, dx_ref, dw_ref):
        i = pl.program_id(1)

        @pl.when(i == 0)
        def _():
            dw_ref[...] = jnp.zeros_like(dw_ref)

        for h in range(nh):
            sl = slice(h * HEAD_DIM, (h + 1) * HEAD_DIM)
            wv = w_ref[:, sl]
            dp = dp_ref[:, sl]
            nxt = jnp.where(i == ni - 1, 0.0, nxt_ref[:, sl])
            dd = jnp.concatenate([dp, nxt], axis=0)
            prev = jnp.where(i == 0, 0.0, prev_ref[:, sl])
            xx = jnp.concatenate([prev, x_ref[:, sl]], axis=0)
            dx = None
            rows = []
            for k in range(CONV_W):
                sh = CONV_W - 1 - k
                t = (pltpu.roll(dd, tm + 8 - sh, 0) if sh else dd)[:tm, :] * wv[k:k + 1, :]
                dx = t if dx is None else dx + t
                xs = (pltpu.roll(xx, sh, 0) if sh else xx)[8:, :]
                rows.append(jnp.sum(dp * xs, axis=0, keepdims=True))
            dx_ref[:, sl] = dx.astype(dx_ref.dtype)
            dw_ref[:, sl] += jnp.concatenate(rows, axis=0)

    hb = pl.BlockSpec((tm, fw), lambda j, i: (i, j))
    wb = pl.BlockSpec((CONV_W, fw), lambda j, i: (0, j))
    return pl.pallas_call(
        body, name="conv_bwd", grid=(cw // fw, ni),
        in_specs=[hb, pl.BlockSpec((8, fw), lambda j, i: (jnp.maximum(i * (tm // 8) - 1, 0), j)), wb, hb,
                  pl.BlockSpec((8, fw), lambda j, i: (jnp.minimum((i + 1) * (tm // 8), s // 8 - 1), j))],
        out_specs=[hb, wb],
        out_shape=[jax.ShapeDtypeStruct((s, cw), bf16), jax.ShapeDtypeStruct((CONV_W, cw), f32)],
        compiler_params=_cp("parallel", "arbitrary"))(x, x, w, dpre, dpre)


def _chunk_consts():
    c = GDN_CHUNK
    r = lax.broadcasted_iota(jnp.int32, (c, c), 0)
    q = lax.broadcasted_iota(jnp.int32, (c, c), 1)
    return r >= q, r > q, (r == q).astype(f32)


def _chunk_head(qkvn, sm, gcs, gcs_t, h, nh):
    fw = nh * HEAD_DIM
    q = qkvn[:, h * HEAD_DIM:(h + 1) * HEAD_DIM] * (HEAD_DIM ** -0.5)
    k = qkvn[:, fw + h * HEAD_DIM: fw + (h + 1) * HEAD_DIM]
    v = qkvn[:, 2 * fw + h * HEAD_DIM: 2 * fw + (h + 1) * HEAD_DIM]
    beta = _lane_col(sm, 2 * nh + h)
    gc = _lane_col(gcs, nh + h)
    gc_row = gcs_t[nh + h: nh + h + 1, :]
    incl, strict, _ = _chunk_consts()
    decay = jnp.where(incl, jnp.exp(jnp.minimum(gc - gc_row, 0.0)), 0.0)
    eg = jnp.exp(gc)
    g_last = gc[GDN_CHUNK - 1:GDN_CHUNK, :]
    egl = jnp.exp(g_last)
    ekd = jnp.exp(g_last - gc)
    kb = k * beta
    vb = v * beta
    kk = _dotb(kb, k, NT)
    qk = _dotb(q, k, NT)
    return dict(q=q, k=k, v=v, beta=beta, gc=gc, decay=decay, eg=eg, egl=egl, ekd=ekd, kb=kb, vb=vb, kk=kk, qk=qk,
                incl=incl, strict=strict)


def _unit_lower_inverses(lows, eye):
    ps = [-low for low in lows]
    ts = [eye + p for p in ps]
    for _ in range(5):
        ps = [_doth(p, p) for p in ps]
        ts = [t + _doth(t, p) for t, p in zip(ts, ps)]
    return ts


def _gdn_fwd(qkvn, sm, z, wn, nh):
    s = qkvn.shape[0]
    c = GDN_CHUNK
    nc = s // c
    fw = nh * HEAD_DIM

    def body(qkvn_ref, sm_ref, z_ref, wn_ref, on_ref, o_ref, st_ref, ti_ref, state):
        ci = pl.program_id(0)

        @pl.when(ci == 0)
        def _():
            state[...] = jnp.zeros_like(state)

        qkvn_v = qkvn_ref[...]
        sm_v = sm_ref[...]
        incl, strict, eye = _chunk_consts()
        gcs = _doth(incl.astype(f32), sm_v)
        gcs_t = gcs.T
        heads = range(nh)
        es = [_chunk_head(qkvn_v, sm_v, gcs, gcs_t, h, nh) for h in heads]
        tinvs = _unit_lower_inverses([jnp.where(strict, e["kk"] * e["decay"], 0.0) for e in es], eye)
        us = [_doth(t, e["vb"]) for t, e in zip(tinvs, es)]
        ws = [_doth(t, e["kb"] * e["eg"]) for t, e in zip(tinvs, es)]
        sts = [state[h] for h in heads]
        v_news = [u - _dotb(w, st) for u, w, st in zip(us, ws, sts)]
        qss = [_dotb(e["q"] * e["eg"], st) for e, st in zip(es, sts)]
        os_ = [qs + _dotb(jnp.where(incl, e["qk"] * e["decay"], 0.0), vn) for qs, e, vn in zip(qss, es, v_news)]
        upd = [_dotb(e["k"] * e["ekd"], vn, TN) for e, vn in zip(es, v_news)]
        for h in heads:
            st_ref[0, h] = sts[h]
            ti_ref[0, h] = tinvs[h]
            state[h] = sts[h] * es[h]["egl"] + upd[h]
            sl = slice(h * HEAD_DIM, (h + 1) * HEAD_DIM)
            o = os_[h]
            o_ref[:, sl] = o
            zz = z_ref[:, sl]
            r = lax.rsqrt(jnp.mean(o * o, axis=-1, keepdims=True) + EPS)
            on_ref[:, sl] = (o * r * wn_ref[...] * (zz * _sigmoid(zz))).astype(on_ref.dtype)

    return pl.pallas_call(
        body, name="gdn_fwd", grid=(nc,),
        in_specs=[pl.BlockSpec((c, 3 * fw), lambda i: (i, 0)), pl.BlockSpec((c, LANES), lambda i: (i, 0)),
                  pl.BlockSpec((c, fw), lambda i: (i, 0)), pl.BlockSpec((1, HEAD_DIM), lambda i: (0, 0))],
        out_specs=[pl.BlockSpec((c, fw), lambda i: (i, 0)), pl.BlockSpec((c, fw), lambda i: (i, 0)),
                   pl.BlockSpec((1, nh, HEAD_DIM, HEAD_DIM), lambda i: (i, 0, 0, 0)),
                   pl.BlockSpec((1, nh, c, c), lambda i: (i, 0, 0, 0))],
        out_shape=[jax.ShapeDtypeStruct((s, fw), bf16), jax.ShapeDtypeStruct((s, fw), f32),
                   jax.ShapeDtypeStruct((nc, nh, HEAD_DIM, HEAD_DIM), f32), jax.ShapeDtypeStruct((nc, nh, c, c), f32)],
        scratch_shapes=[pltpu.VMEM((nh, HEAD_DIM, HEAD_DIM), f32)],
        compiler_params=_cp("arbitrary"))(qkvn, sm, z, wn)


def _gdn_post_bwd(don, o, z, wn, nh):
    s, fw = o.shape
    tm = _row_tile(s)

    def body(don_ref, o_ref, z_ref, wn_ref, do_ref, dz_ref, dwn_ref):
        i = pl.program_id(0)
        h = pl.program_id(1)

        @pl.when((i == 0) & (h == 0))
        def _():
            dwn_ref[...] = jnp.zeros_like(dwn_ref)

        o = o_ref[...]
        zz = z_ref[...]
        don = don_ref[...].astype(f32)
        wv = wn_ref[...]
        r = lax.rsqrt(jnp.mean(o * o, axis=-1, keepdims=True) + EPS)
        n = o * r
        sg = _sigmoid(zz)
        silu = zz * sg
        dz_ref[...] = (don * n * wv * sg * (1.0 + zz * (1.0 - sg))).astype(dz_ref.dtype)
        dnw = don * silu
        dwn_ref[...] += jnp.sum(dnw * n, axis=0, keepdims=True)
        dn = dnw * wv
        do_ref[...] = r * (dn - n * jnp.mean(dn * n, axis=-1, keepdims=True))

    hb = pl.BlockSpec((tm, HEAD_DIM), lambda i, h: (i, h))
    wb = pl.BlockSpec((1, HEAD_DIM), lambda i, h: (0, 0))
    return pl.pallas_call(
        body, name="gdn_post_bwd", grid=(s // tm, nh), in_specs=[hb, hb, hb, wb], out_specs=[hb, hb, wb],
        out_shape=[jax.ShapeDtypeStruct((s, fw), f32), jax.ShapeDtypeStruct((s, fw), bf16),
                   jax.ShapeDtypeStruct((1, HEAD_DIM), f32)],
        compiler_params=_cp("arbitrary", "arbitrary"))(don, o, z, wn)


def _gdn_bwd(qkvn, sm, do, states, tinvs, nh):
    s = qkvn.shape[0]
    c = GDN_CHUNK
    nc = s // c
    fw = nh * HEAD_DIM

    def body(qkvn_ref, sm_ref, do_ref, st_ref, ti_ref, dqkvn_ref, dsm_ref, dstate):
        ci = pl.program_id(0)

        @pl.when(ci == 0)
        def _():
            dstate[...] = jnp.zeros_like(dstate)

        qkvn_v = qkvn_ref[...]
        sm_v = sm_ref[...]
        incl, strict, eye = _chunk_consts()
        inclf = incl.astype(f32)
        gcs = _doth(inclf, sm_v)
        gcs_t = gcs.T
        lane = lax.broadcasted_iota(jnp.int32, (c, LANES), 1)
        last_row = lax.broadcasted_iota(jnp.int32, (c, 1), 0) == c - 1
        ones_cl = jnp.ones((c, LANES), f32)
        each = lambda f: [f(h) for h in range(nh)]
        es = each(lambda h: _chunk_head(qkvn_v, sm_v, gcs, gcs_t, h, nh))
        tinv = each(lambda h: ti_ref[0, h])
        st = each(lambda h: st_ref[0, h])
        dst = each(lambda h: dstate[h])
        do = each(lambda h: do_ref[:, h * HEAD_DIM:(h + 1) * HEAD_DIM])
        kg = each(lambda h: es[h]["kb"] * es[h]["eg"])
        qg = each(lambda h: es[h]["q"] * es[h]["eg"])
        kd = each(lambda h: es[h]["k"] * es[h]["ekd"])
        u = each(lambda h: _doth(tinv[h], es[h]["vb"]))
        w = each(lambda h: _doth(tinv[h], kg[h]))
        a = each(lambda h: jnp.where(incl, es[h]["qk"] * es[h]["decay"], 0.0))
        v_new = each(lambda h: u[h] - _dotb(w[h], st[h]))
        dv_new = each(lambda h: _dotb(a[h], do[h], TN) + _dotb(kd[h], dst[h]))
        da = each(lambda h: jnp.where(incl, _dotb(do[h], v_new[h], NT), 0.0))
        dqg = each(lambda h: _dotb(do[h], st[h], NT))
        dkd = each(lambda h: _dotb(v_new[h], dst[h], NT))
        dglast = each(lambda h: es[h]["egl"] * jnp.sum(jnp.sum(dst[h] * st[h], axis=1, keepdims=True), axis=0, keepdims=True))
        dw = each(lambda h: -_dotb(dv_new[h], st[h], NT))
        new_dst = each(lambda h: _dotb(qg[h], do[h], TN) + es[h]["egl"] * dst[h] - _dotb(w[h], dv_new[h], TN))
        dtinv = each(lambda h: _doth(dv_new[h], es[h]["vb"], NT) + _doth(dw[h], kg[h], NT))
        dvb = each(lambda h: _doth(tinv[h], dv_new[h], TN))
        dkg = each(lambda h: _doth(tinv[h], dw[h], TN))
        tdt = each(lambda h: _doth(tinv[h], dtinv[h], TN))
        dlow = each(lambda h: -_doth(tdt[h], tinv[h], NT))
        dkk = each(lambda h: jnp.where(strict, dlow[h] * es[h]["decay"], 0.0))
        dqk = each(lambda h: da[h] * es[h]["decay"])
        darg = each(lambda h: (jnp.where(strict, dlow[h] * es[h]["kk"], 0.0) + da[h] * es[h]["qk"]) * es[h]["decay"])
        dgc = each(lambda h: jnp.sum(darg[h], axis=1, keepdims=True) - _doth(darg[h], ones_cl, TN)[:, 0:1])
        dkb = each(lambda h: _dotb(dkk[h], es[h]["k"]) + dkg[h] * es[h]["eg"])
        dk = each(lambda h: _dotb(dkk[h], es[h]["kb"], TN) + _dotb(dqk[h], es[h]["q"], TN) + dkd[h] * es[h]["ekd"]
                  + dkb[h] * es[h]["beta"])
        dq = each(lambda h: (_dotb(dqk[h], es[h]["k"]) + dqg[h] * es[h]["eg"]) * (HEAD_DIM ** -0.5))
        s_kd = each(lambda h: jnp.sum(dkd[h] * kd[h], axis=1, keepdims=True))
        dgc = each(lambda h: dgc[h] + jnp.sum(dkg[h] * kg[h] + dqg[h] * qg[h], axis=1, keepdims=True) - s_kd[h]
                   + jnp.where(last_row, jnp.sum(s_kd[h], axis=0, keepdims=True) + dglast[h], 0.0))
        dg = each(lambda h: _doth(inclf, dgc[h] * ones_cl, TN)[:, 0:1])
        dsm = jnp.zeros((c, LANES), f32)
        for h in range(nh):
            dstate[h] = new_dst[h]
            dbeta = jnp.sum(dkb[h] * es[h]["k"] + dvb[h] * es[h]["v"], axis=1, keepdims=True)
            dqkvn_ref[:, h * HEAD_DIM:(h + 1) * HEAD_DIM] = dq[h]
            dqkvn_ref[:, fw + h * HEAD_DIM: fw + (h + 1) * HEAD_DIM] = dk[h]
            dqkvn_ref[:, 2 * fw + h * HEAD_DIM: 2 * fw + (h + 1) * HEAD_DIM] = dvb[h] * es[h]["beta"]
            dsm = dsm + jnp.where(lane == nh + h, dg[h], 0.0) + jnp.where(lane == 2 * nh + h, dbeta, 0.0)
        dsm_ref[...] = dsm

    rev = lambda i: (nc - 1 - i, 0)
    rev4 = lambda i: (nc - 1 - i, 0, 0, 0)
    return pl.pallas_call(
        body, name="gdn_bwd", grid=(nc,),
        in_specs=[pl.BlockSpec((c, 3 * fw), rev), pl.BlockSpec((c, LANES), rev), pl.BlockSpec((c, fw), rev),
                  pl.BlockSpec((1, nh, HEAD_DIM, HEAD_DIM), rev4), pl.BlockSpec((1, nh, c, c), rev4)],
        out_specs=[pl.BlockSpec((c, 3 * fw), rev), pl.BlockSpec((c, LANES), rev)],
        out_shape=[jax.ShapeDtypeStruct((s, 3 * fw), f32), jax.ShapeDtypeStruct((s, LANES), f32)],
        scratch_shapes=[pltpu.VMEM((nh, HEAD_DIM, HEAD_DIM), f32)],
        compiler_params=_cp("arbitrary"))(qkvn, sm, do, states, tinvs)


MM_TILES = (1024, 512, 2048)
MM_TILES_TN = (512, 1024, 1024)
MM_TILES_LONG_K = (1024, 512, 2560)


def _ffn_fwd(tag, x, g, mod3, wg_t, wu_t, wd):
    sh, sc, gt = mod3
    h = _ada_in(tag + "_ada", x, g, sh, sc)
    gate, up, act = _mm(tag + "_up", [[(h, wg_t)], [(h, wu_t)]], "nt", MM_TILES, _ep_swiglu, (bf16, bf16, bf16))
    xn, y = _mm(tag + "_down", [[(act, wd)]], "nn", MM_TILES, _ep_residual, (f32, bf16),
                extras=((x, "mn"), (MACARON_W * gt, "n")))
    return xn, dict(x=x, h=h, gate=gate, up=up, y=y)


def _ffn_bwd(tag, dxn, res, g, mod3, wg_t, wu_t, wd):
    sh, sc, gt = mod3
    dy, dgs = _gate_bwd(tag + "_gate_bwd", dxn, res["y"], MACARON_W * gt)
    act, dgate, dup = _mm(tag + "_dact", [[(dy, wd)]], "nt", MM_TILES, _ep_swiglu_bwd, (bf16, bf16, bf16),
                          extras=((res["gate"], "mn"), (res["up"], "mn")))
    (dwd,) = _mm(tag + "_dwd", [[(act, dy)]], "tn", MM_TILES_TN, _ep_plain, (bf16,))
    (dwg_t,) = _mm(tag + "_dwg", [[(dgate, res["h"])]], "tn", MM_TILES_TN, _ep_plain, (bf16,))
    (dwu_t,) = _mm(tag + "_dwu", [[(dup, res["h"])]], "tn", MM_TILES_TN, _ep_plain, (bf16,))
    (dh,) = _mm(tag + "_dh", [[(dgate, wg_t), (dup, wu_t)]], "nn", MM_TILES, _ep_plain, (bf16,))
    dx, dsh, a = _ada_bwd(tag + "_ada_bwd", res["x"], g, sc, dh, dxn)
    return dx, (dwg_t, dwu_t, dwd), (dsh, a * g, MACARON_W * dgs), a * (1.0 + sc)


def _local_step(x, target, mods, norm_g, final_norm, ffn_w, w_cat_t, w_out, prm, fox_wn, gdn_wn, conv_w, nh):
    s, d = x.shape
    fw = nh * HEAD_DIM
    tq = _tile(s, 256)
    g_rows = [norm_g[i:i + 1] for i in range(3)]
    m1, m2, m3 = mods[0:3], mods[3:6], mods[6:9]

    x1, r1 = _ffn_fwd("ffn1", x, g_rows[0], m1, *ffn_w[0])
    h2 = _ada_in("mix_ada", x1, g_rows[1], m2[0], m2[1])
    w_fox, w_gdn, w_z, w_s = w_cat_t[:3 * fw], w_cat_t[3 * fw:6 * fw], w_cat_t[6 * fw:7 * fw], w_cat_t[7 * fw:]
    colscale = jnp.concatenate([jnp.full((1, fw), FOX_Q_SCALE, f32), jnp.ones((1, 2 * fw), f32)], axis=1)
    (qkv_f,) = _mm("proj_fox", [[(h2, w_fox)]], "nt", MM_TILES, _ep_colscale, (bf16,), extras=((colscale, "n"),))
    (qkv_g,) = _mm("proj_gdn", [[(h2, w_gdn)]], "nt", MM_TILES, _ep_plain, (f32,))
    (z,) = _mm("proj_z", [[(h2, w_z)]], "nt", MM_TILES, _ep_plain, (f32,))
    (ps,) = _mm("proj_s", [[(h2, w_s)]], "nt", MM_TILES, _ep_plain, (f32,))
    sm, cum = _small_fwd(ps, prm, nh)
    qa, ka, va = _fox_aug(qkv_f, cum, nh)
    o_f, on_f, lse2 = _fox_fwd(qa, ka, qkv_f, fox_wn, nh, tq)
    qkvn = _gdn_pre(qkv_g, conv_w, nh)
    on_g, o_g, states, tinvs = _gdn_fwd(qkvn, sm, z, gdn_wn, nh)
    w_top, w_bot = w_out[:fw], w_out[fw:]
    x2, mix = _mm("mix_out", [[(on_f, w_top), (on_g, w_bot)]], "nn", MM_TILES, _ep_residual, (f32, bf16),
                  extras=((x1, "mn"), (m2[2], "n")))
    x3, r3 = _ffn_fwd("ffn2", x2, g_rows[2], m3, *ffn_w[1])
    loss, dx3, dfinal = _final_loss(x3, final_norm, target)

    dx2, dffn2, dmod3, dg3 = _ffn_bwd("ffn2", dx3, r3, g_rows[2], m3, *ffn_w[1])
    dmix, dgt2 = _gate_bwd("mix_gate_bwd", dx2, mix, m2[2])
    (don_f,) = _mm("mix_dof", [[(dmix, w_top)]], "nt", MM_TILES, _ep_plain, (f32,))
    (don_g,) = _mm("mix_dog", [[(dmix, w_bot)]], "nt", MM_TILES, _ep_plain, (f32,))
    (dw_top,) = _mm("mix_dwtop", [[(on_f, dmix)]], "tn", MM_TILES_TN, _ep_plain, (bf16,))
    (dw_bot,) = _mm("mix_dwbot", [[(on_g, dmix)]], "tn", MM_TILES_TN, _ep_plain, (bf16,))
    qb, doa, dfox_wn = _fox_post_bwd(don_f, o_f, lse2, cum, qkv_f, fox_wn, nh)
    dqx, dkx, dv_f = _fox_bwd(qb, doa, ka, va, nh, tq)
    dq_f, dk_f, dcum = _fox_unpack(dqx, dkx, nh)
    do_g, dz, dgdn_wn = _gdn_post_bwd(don_g, o_g, z, gdn_wn, nh)
    dqkvn, dsm = _gdn_bwd(qkvn, sm, do_g, states, tinvs, nh)
    dpre = _gdn_pre_bwd(qkv_g, conv_w, dqkvn, nh)
    dqkv_g, dconv = _conv_bwd(qkv_g, conv_w, dpre, nh)
    dps, pg = _small_bwd(ps, prm, dsm, dcum, nh)
    dproj = jnp.concatenate([dq_f, dk_f, dv_f, dqkv_g, dz, dps], axis=1)
    (dw_cat_t,) = _mm("proj_dw", [[(dproj, h2)]], "tn", MM_TILES_TN, _ep_plain, (bf16,))
    (dh2,) = _mm("proj_dh", [[(dproj, w_cat_t)]], "nn", MM_TILES_LONG_K, _ep_plain, (bf16,))
    dx1, dsh2, a2 = _ada_bwd("mix_ada_bwd", x1, g_rows[1], m2[1], dh2, dx2)
    dmod2 = (dsh2, a2 * g_rows[1], dgt2)
    dg2 = a2 * (1.0 + m2[1])
    dx0, dffn1, dmod1, dg1 = _ffn_bwd("ffn1", dx1, r1, g_rows[0], m1, *ffn_w[0])

    big = dict(ffn=(dffn1, dffn2), w_cat_t=dw_cat_t, w_out=jnp.concatenate([dw_top, dw_bot], axis=0))
    small = dict(loss=loss, norm_g=jnp.concatenate([dg1, dg2, dg3], axis=0), final_norm=dfinal, fox_wn=dfox_wn,
                 gdn_wn=dgdn_wn, pg=pg, conv=dconv, mod=jnp.concatenate(list(dmod1) + list(dmod2) + list(dmod3), axis=1))
    return dx0, big, small


def _w_in_row_groups(nh):
    fw = nh * HEAD_DIM
    sizes = [3 * fw, nh, 3 * fw, nh, nh, fw]
    offs = [0]
    for sz in sizes:
        offs.append(offs[-1] + sz)
    return [(offs[i], offs[i + 1]) for i in range(len(sizes))]


def _build_w_cat_t(w_in_t, nh):
    gq, gf, gg, ga, gb, gz = _w_in_row_groups(nh)
    d = w_in_t.shape[1]
    rows = lambda r: w_in_t[r[0]:r[1]]
    pad = jnp.zeros((LANES - 3 * nh, d), w_in_t.dtype)
    return jnp.concatenate([rows(gq), rows(gg), rows(gz), rows(gf), rows(ga), rows(gb), pad], axis=0)


def _split_dw_cat_t(dw_cat_t, nh):
    fw = nh * HEAD_DIM
    o = 7 * fw
    return jnp.concatenate([dw_cat_t[:3 * fw], dw_cat_t[o:o + nh], dw_cat_t[3 * fw:6 * fw], dw_cat_t[o + nh:o + 2 * nh],
                            dw_cat_t[o + 2 * nh:o + 3 * nh], dw_cat_t[6 * fw:7 * fw]], axis=0)


def _head_params(fox_f_bias, gdn_dt_bias, gdn_a_log, nh):
    z = jnp.zeros((8, LANES), f32)
    z = z.at[0, 0:nh].set(fox_f_bias.reshape(nh))
    z = z.at[1, nh:2 * nh].set(gdn_dt_bias.reshape(nh))
    z = z.at[2, nh:2 * nh].set(gdn_a_log.reshape(nh))
    return z


ANY = pl.BlockSpec(memory_space=pl.ANY)
IN_VMEM = pl.BlockSpec(memory_space=pltpu.VMEM)
N_OTHER_CHIPS = 3


def _place():
    x, y, c = lax.axis_index("x"), lax.axis_index("y"), lax.axis_index("c")
    chips = [(1 - x, y), (x, 1 - y), (1 - x, 1 - y)]
    return x, y, c, chips


def _allgather8(name, v):
    r, n = v.shape

    def body(v_ref, out_ref, send_sems, recv_sems, local_sem):
        x, y, c, _ = _place()
        me = 4 * x + 2 * y + c
        mine = pltpu.make_async_copy(v_ref, out_ref.at[me], local_sem)
        mine.start()
        copies = []
        for k in range(1, 8):
            fx, fy, fc = (k >> 2) & 1, (k >> 1) & 1, k & 1
            peer = (x + fx - 2 * x * fx, y + fy - 2 * y * fy, c + fc - 2 * c * fc)
            cp = pltpu.make_async_remote_copy(src_ref=v_ref, dst_ref=out_ref.at[me], send_sem=send_sems.at[k - 1],
                                              recv_sem=recv_sems.at[k - 1], device_id=peer, device_id_type=MESH)
            cp.start()
            copies.append(cp)
        for cp in copies:
            cp.wait()
        mine.wait()

    return pl.pallas_call(
        body, name=name, in_specs=[IN_VMEM], out_specs=IN_VMEM, out_shape=jax.ShapeDtypeStruct((8, r, n), v.dtype),
        scratch_shapes=[pltpu.SemaphoreType.DMA((7,)), pltpu.SemaphoreType.DMA((7,)), pltpu.SemaphoreType.DMA],
        compiler_params=pltpu.CompilerParams(vmem_limit_bytes=VMEM_LIMIT_V7X))(v)


def _gather_weights(halves):
    nw = len(halves)

    def body(*refs):
        ins, outs = refs[:nw], refs[nw:2 * nw]
        ici_send, ici_recv, d2d_send, d2d_recv = refs[2 * nw:]
        x, y, c, chips = _place()
        s = 2 * x + y
        sib = (x, y, 1 - c)
        sends = []
        for w in range(nw):
            for j, (cx, cy) in enumerate(chips):
                cp = pltpu.make_async_remote_copy(
                    src_ref=ins[w].at[c], dst_ref=outs[w].at[c, s], send_sem=ici_send.at[w * 3 + j],
                    recv_sem=ici_recv.at[w * 3 + j], device_id=(cx, cy, c), device_id_type=MESH)
                cp.start()
                sends.append(cp)
        fwds = []
        for w in range(nw):
            for j, (cx, cy) in enumerate(chips):
                sj = 2 * cx + cy
                landed = outs[w].at[c, sj]
                pltpu.make_async_remote_copy(src_ref=ins[w].at[c], dst_ref=landed, send_sem=ici_send.at[w * 3 + j],
                                             recv_sem=ici_recv.at[w * 3 + j], device_id=(cx, cy, c),
                                             device_id_type=MESH).wait_recv()
                cp = pltpu.make_async_remote_copy(src_ref=landed, dst_ref=landed, send_sem=d2d_send.at[w * 3 + j],
                                                  recv_sem=d2d_recv.at[w * 3 + j], device_id=sib, device_id_type=MESH)
                cp.start()
                fwds.append(cp)
        for w in range(nw):
            for j, (cx, cy) in enumerate(chips):
                theirs = outs[w].at[1 - c, 2 * cx + cy]
                pltpu.make_async_remote_copy(src_ref=theirs, dst_ref=theirs, send_sem=d2d_send.at[w * 3 + j],
                                             recv_sem=d2d_recv.at[w * 3 + j], device_id=sib, device_id_type=MESH).wait_recv()
        for cp in sends + fwds:
            cp.wait_send()

    n3 = nw * N_OTHER_CHIPS
    return pl.pallas_call(
        body, name="gather_weights", in_specs=[ANY] * nw, out_specs=[ANY] * nw,
        out_shape=[jax.ShapeDtypeStruct((2, 4) + h.shape[1:], h.dtype) for h in halves],
        scratch_shapes=[pltpu.SemaphoreType.DMA((n3,)) for _ in range(4)],
    )(*halves)


def _send_to_sibling(name, srcs, other_half):
    nw = len(srcs)

    def body(*refs):
        ins, outs = refs[:nw], refs[nw:2 * nw]
        send_sems, recv_sems = refs[2 * nw:]
        x, y, c, _ = _place()
        cps = []
        for w in range(nw):
            cp = pltpu.make_async_remote_copy(src_ref=ins[w].at[1 - c] if other_half else ins[w], dst_ref=outs[w],
                                              send_sem=send_sems.at[w], recv_sem=recv_sems.at[w],
                                              device_id=(x, y, 1 - c), device_id_type=MESH)
            cp.start()
            cps.append(cp)
        for cp in cps:
            cp.wait()

    return pl.pallas_call(
        body, name=name, in_specs=[ANY] * nw, out_specs=[ANY] * nw,
        out_shape=[jax.ShapeDtypeStruct(a.shape[1:] if other_half else a.shape, a.dtype) for a in srcs],
        scratch_shapes=[pltpu.SemaphoreType.DMA((nw,)), pltpu.SemaphoreType.DMA((nw,))],
    )(*srcs)


def _send_to_chips(partials):
    nw = len(partials)

    def body(*refs):
        ins, outs = refs[:nw], refs[nw:2 * nw]
        send_sems, recv_sems = refs[2 * nw:]
        x, y, c, chips = _place()
        cps = []
        for w in range(nw):
            for j, (cx, cy) in enumerate(chips):
                cp = pltpu.make_async_remote_copy(src_ref=ins[w].at[2 * cx + cy], dst_ref=outs[w].at[j],
                                                  send_sem=send_sems.at[w * 3 + j], recv_sem=recv_sems.at[w * 3 + j],
                                                  device_id=(cx, cy, c), device_id_type=MESH)
                cp.start()
                cps.append(cp)
        for cp in cps:
            cp.wait()

    n3 = nw * N_OTHER_CHIPS
    return pl.pallas_call(
        body, name="rs_to_chips", in_specs=[ANY] * nw, out_specs=[ANY] * nw,
        out_shape=[jax.ShapeDtypeStruct((3,) + a.shape[1:], a.dtype) for a in partials],
        scratch_shapes=[pltpu.SemaphoreType.DMA((n3,)), pltpu.SemaphoreType.DMA((n3,))],
    )(*partials)


def _add_pair(name, g, recv, c):
    _, nchip, r, d = g.shape
    tr = _tile(r, 512, 16)

    def body(c_ref, g_ref, r_ref, o_ref):
        o_ref[...] = (g_ref[...].astype(f32) + r_ref[...].astype(f32)).astype(o_ref.dtype)

    gs = pltpu.PrefetchScalarGridSpec(
        num_scalar_prefetch=1, grid=(nchip, r // tr),
        in_specs=[pl.BlockSpec((None, None, tr, d), lambda t, i, cr: (cr[0], t, i, 0)),
                  pl.BlockSpec((None, tr, d), lambda t, i, cr: (t, i, 0))],
        out_specs=pl.BlockSpec((None, tr, d), lambda t, i, cr: (t, i, 0)))
    return pl.pallas_call(body, name=name, grid_spec=gs, out_shape=jax.ShapeDtypeStruct((nchip, r, d), bf16),
                          compiler_params=_cp("parallel", "parallel"))(c.reshape(1).astype(jnp.int32), g, recv)


def _add_chips(name, p, recv, s_chip):
    _, r, d = p.shape
    tr = _tile(r, 512, 16)

    def body(s_ref, p_ref, r_ref, o_ref):
        o_ref[...] = ((p_ref[...].astype(f32) + r_ref[0].astype(f32)) + r_ref[1].astype(f32)) + r_ref[2].astype(f32)

    gs = pltpu.PrefetchScalarGridSpec(
        num_scalar_prefetch=1, grid=(r // tr,),
        in_specs=[pl.BlockSpec((None, tr, d), lambda i, sr: (sr[0], i, 0)),
                  pl.BlockSpec((3, tr, d), lambda i, sr: (0, i, 0))],
        out_specs=pl.BlockSpec((tr, d), lambda i, sr: (i, 0)))
    return pl.pallas_call(body, name=name, grid_spec=gs, out_shape=jax.ShapeDtypeStruct((r, d), f32),
                          compiler_params=_cp("parallel"))(s_chip.reshape(1).astype(jnp.int32), p, recv)


def _reduce_scatter(grads, c, s_chip):
    names = [str(i) for i in range(len(grads))]
    from_sib = _send_to_sibling("rs_to_sibling", grads, True)
    partial = [_add_pair("rs_add_pair" + n, g, r, c) for n, g, r in zip(names, grads, from_sib)]
    from_chips = _send_to_chips(partial)
    mine = [_add_chips("rs_add_chips" + n, p, r, s_chip) for n, p, r in zip(names, partial, from_chips)]
    theirs = _send_to_sibling("rs_exchange_halves", mine, False)
    return [jnp.where(c == 0, jnp.stack([a, b]), jnp.stack([b, a])) for a, b in zip(mine, theirs)]


def _sum_devices(v):
    n = v.shape[2]

    def body(v_ref, o_ref):
        t = v_ref[0]
        for k in range(1, 8):
            t = t + v_ref[k]
        o_ref[...] = t

    return pl.pallas_call(body, name="sum_devices", out_shape=jax.ShapeDtypeStruct((1, n), f32))(v)


def _silu_rows(v):
    def body(v_ref, o_ref):
        t = v_ref[...]
        o_ref[...] = t * _sigmoid(t)

    return pl.pallas_call(body, name="silu_cond", out_shape=jax.ShapeDtypeStruct(v.shape, f32))(v)


ADAMW_BLOCK_ELEMS = 600 * 1024


def _adamw(name, w, g, m, v):
    r, cdim = w.shape
    tr = _tile(r, max(8, min(256, (ADAMW_BLOCK_ELEMS // cdim) // 8 * 8)), 8)
    c1 = 1.0 - ADAM_B1 ** ADAM_STEP
    c2 = 1.0 - ADAM_B2 ** ADAM_STEP

    def body(w_ref, g_ref, m_ref, v_ref, d_ref, mo_ref, vo_ref):
        gv = g_ref[...]
        mn = ADAM_B1 * m_ref[...] + (1.0 - ADAM_B1) * gv
        vn = ADAM_B2 * v_ref[...] + (1.0 - ADAM_B2) * (gv * gv)
        d_ref[...] = -ADAM_LR * ((mn / c1) / (jnp.sqrt(vn / c2) + ADAM_EPS) + ADAM_WD * w_ref[...])
        mo_ref[...] = mn
        vo_ref[...] = vn

    blk = pl.BlockSpec((tr, cdim), lambda i: (i, 0))
    return pl.pallas_call(body, name=name, grid=(r // tr,), in_specs=[blk] * 4, out_specs=[blk] * 3,
                          out_shape=[jax.ShapeDtypeStruct((r, cdim), f32)] * 3, compiler_params=_cp("parallel"))(w, g, m, v)


def _ep_bias(accs, ex):
    return (accs[0] + ex[0],)


def kernel(x, c, ada_w, ada_b, norm_g, ffn_w_gate, ffn_w_up, ffn_w_down, w_in, w_out, fox_f_bias, fox_out_norm, gdn_conv, gdn_A_log, gdn_dt_bias, gdn_out_norm, final_norm, loss_target, m_ada_w, m_ada_b, m_norm_g, m_ffn_w_gate, m_ffn_w_up, m_ffn_w_down, m_w_in, m_w_out, m_fox_f_bias, m_fox_out_norm, m_gdn_conv, m_gdn_A_log, m_gdn_dt_bias, m_gdn_out_norm, m_final_norm, v_ada_w, v_ada_b, v_norm_g, v_ffn_w_gate, v_ffn_w_up, v_ffn_w_down, v_w_in, v_w_out, v_fox_f_bias, v_fox_out_norm, v_gdn_conv, v_gdn_A_log, v_gdn_dt_bias, v_gdn_out_norm, v_final_norm):
    ix, iy, ic = lax.axis_index("x"), lax.axis_index("y"), lax.axis_index("c")
    s_chip = 2 * ix + iy
    me = 4 * ix + 2 * iy + ic
    _, s, d = x.shape
    nh = d // (2 * HEAD_DIM)
    fw = nh * HEAD_DIM
    ncol = ada_w.shape[2]
    dg_sh = norm_g.shape[2]
    cv_sh = gdn_conv.shape[2]
    ff_sh = ffn_w_gate.shape[3]
    in_sh = w_in.shape[2]
    in_pad = -(-in_sh // 32) * 32
    out_sh = w_out.shape[1]
    per_chip = lambda a, t: a[2 * t]

    pack0 = jnp.concatenate([_silu_rows(c), norm_g[0].reshape(1, 3 * dg_sh), gdn_conv[0].reshape(1, CONV_W * cv_sh)], axis=1)
    got0 = _allgather8("gather_cond", pack0)
    cond_all = got0[:, 0, :d]
    norm_g_full = jnp.concatenate([per_chip(got0, t)[0, d:d + 3 * dg_sh].reshape(3, dg_sh) for t in range(4)], axis=1)
    conv_full = jnp.concatenate([per_chip(got0, t)[0, d + 3 * dg_sh:].reshape(CONV_W, cv_sh) for t in range(4)], axis=1)

    ada_b_sh = lax.dynamic_slice_in_dim(ada_b, s_chip * ncol, ncol, axis=1)
    (mod_sh,) = _mm("ada_mod", [[(cond_all, ada_w[0])]], "nn", (8, 512, 2048), _ep_bias, (f32,), extras=((ada_b_sh, "n"),))
    mod_all = _allgather8("gather_mod", mod_sh)
    mod = jnp.concatenate([lax.dynamic_index_in_dim(per_chip(mod_all, t), me, axis=0, keepdims=True) for t in range(4)], axis=1)
    mods = [mod[:, i * d:(i + 1) * d] for i in range(9)]

    halves = [jnp.swapaxes(ffn_w_gate[0], 1, 2).astype(bf16), jnp.swapaxes(ffn_w_up[0], 1, 2).astype(bf16),
              ffn_w_down[0].astype(bf16),
              jnp.pad(w_in[0].T.astype(bf16), ((0, in_pad - in_sh), (0, 0))).reshape(2, in_pad // 2, d),
              w_out[0].astype(bf16).reshape(2, out_sh // 2, d)]
    g_wg, g_wu, g_wd, g_win, g_wo = [lax.dynamic_update_slice(g, h[:, None], (0, s_chip, 0, 0))
                                     for g, h in zip(_gather_weights(halves), halves)]
    ffn_w = [(g_wg[j].reshape(4 * ff_sh, d), g_wu[j].reshape(4 * ff_sh, d), g_wd[j].reshape(4 * ff_sh, d)) for j in range(2)]
    w_cat_t = _build_w_cat_t(jnp.swapaxes(g_win, 0, 1).reshape(4, in_pad, d)[:, :in_sh].reshape(4 * in_sh, d), nh)
    w_out_full = jnp.swapaxes(g_wo, 0, 1).reshape(4 * out_sh, d)
    prm = _head_params(fox_f_bias, gdn_dt_bias, gdn_A_log, nh)

    dx0, big, small = _local_step(x[0], loss_target[0], mods, norm_g_full, final_norm.reshape(1, d), ffn_w, w_cat_t,
                                  w_out_full, prm, fox_out_norm, gdn_out_norm, conv_full, nh)

    pack1 = jnp.concatenate([small["loss"], small["norm_g"].reshape(1, 3 * d), small["final_norm"], small["fox_wn"],
                             small["gdn_wn"], small["pg"][0:1], small["pg"][1:2], small["conv"].reshape(1, CONV_W * 3 * fw),
                             small["mod"]], axis=1)
    got1 = _allgather8("gather_small_grads", pack1)
    tot = _sum_devices(got1)
    o = [0]

    def take(n):
        o[0] += n
        return tot[:, o[0] - n:o[0]]

    loss = take(LANES)[0, 0]
    g_norm_g = lax.dynamic_slice_in_dim(take(3 * d).reshape(3, d), s_chip * dg_sh, dg_sh, axis=1)[None]
    g_final = take(d).reshape(d)
    g_fox_wn = take(HEAD_DIM)
    g_gdn_wn = take(HEAD_DIM)
    pg0, pg1 = take(LANES), take(LANES)
    g_fbias, g_dtb, g_alog = pg0[:, 0:nh], pg0[:, nh:2 * nh], pg1[:, nh:2 * nh]
    g_conv = lax.dynamic_slice_in_dim(take(CONV_W * 3 * fw).reshape(CONV_W, 3 * fw), s_chip * cv_sh, cv_sh, axis=1)[None]
    g_ada_b = take(9 * d)
    dmod_all = got1[:, 0, o[0] - 9 * d:o[0]]
    dmod_sh = lax.dynamic_slice_in_dim(dmod_all, s_chip * ncol, ncol, axis=1)
    (g_ada_w,) = _mm("ada_dw", [[(cond_all, dmod_sh)]], "tn", (2048, 512, 8), _ep_plain, (f32,))

    stack = lambda k: jnp.stack([big["ffn"][0][k], big["ffn"][1][k]]).reshape(2, 4, ff_sh, d)
    dw_in_t = jnp.pad(_split_dw_cat_t(big["w_cat_t"], nh).reshape(4, in_sh, d), ((0, 0), (0, in_pad - in_sh), (0, 0)))
    grads = [stack(0), stack(1), stack(2), jnp.swapaxes(dw_in_t.reshape(4, 2, in_pad // 2, d), 0, 1),
             jnp.swapaxes(big["w_out"].reshape(4, 2, out_sh // 2, d), 0, 1)]
    r_wg, r_wu, r_wd, r_win, r_wo = _reduce_scatter(grads, ic, s_chip)
    g_ffn_gate = jnp.swapaxes(r_wg, 1, 2)[None]
    g_ffn_up = jnp.swapaxes(r_wu, 1, 2)[None]
    g_ffn_down = r_wd[None]
    g_w_in = r_win.reshape(in_pad, d)[:in_sh].T[None]
    g_w_out = r_wo.reshape(out_sh, d)[None]

    def upd(name, w, g, m, v):
        shp = w.shape
        two = lambda a: a.reshape(-1, shp[-1])
        return tuple(t.reshape(shp) for t in _adamw(name, two(w), two(g), two(m), two(v)))

    big_upd = [upd("adamw_ada_w", ada_w, g_ada_w[None], m_ada_w, v_ada_w),
               upd("adamw_ffn_gate", ffn_w_gate, g_ffn_gate, m_ffn_w_gate, v_ffn_w_gate),
               upd("adamw_ffn_up", ffn_w_up, g_ffn_up, m_ffn_w_up, v_ffn_w_up),
               upd("adamw_ffn_down", ffn_w_down, g_ffn_down, m_ffn_w_down, v_ffn_w_down),
               upd("adamw_w_in", w_in, g_w_in, m_w_in, v_w_in),
               upd("adamw_w_out", w_out, g_w_out, m_w_out, v_w_out)]
    small_w = [ada_b, norm_g, fox_f_bias, fox_out_norm, gdn_conv, gdn_A_log, gdn_dt_bias, gdn_out_norm, final_norm]
    small_g = [g_ada_b, g_norm_g, g_fbias, g_fox_wn, g_conv, g_alog, g_dtb, g_gdn_wn, g_final]
    small_m = [m_ada_b, m_norm_g, m_fox_f_bias, m_fox_out_norm, m_gdn_conv, m_gdn_A_log, m_gdn_dt_bias, m_gdn_out_norm, m_final_norm]
    small_v = [v_ada_b, v_norm_g, v_fox_f_bias, v_fox_out_norm, v_gdn_conv, v_gdn_A_log, v_gdn_dt_bias, v_gdn_out_norm, v_final_norm]
    sizes = [a.size for a in small_w]
    npad = -sum(sizes) % LANES
    flat = lambda arrs, fill: jnp.concatenate([a.reshape(1, -1) for a in arrs] + [jnp.full((1, npad), fill, f32)], axis=1)
    sd, sm_, sv = _adamw("adamw_small", flat(small_w, 0.0), flat(small_g, 0.0), flat(small_m, 0.0), flat(small_v, 1.0))

    def unflat(t):
        out, off = [], 0
        for a, n in zip(small_w, sizes):
            out.append(t[0, off:off + n].reshape(a.shape))
            off += n
        return out

    small_g = [g.reshape(a.shape) for g, a in zip(small_g, small_w)]
    s_d, s_m, s_v = unflat(sd), unflat(sm_), unflat(sv)
    def order(bigs, smalls):
        return [bigs[0], smalls[0], smalls[1], bigs[1], bigs[2], bigs[3], bigs[4], bigs[5]] + list(smalls[2:])

    grads_out = order([g_ada_w[None], g_ffn_gate, g_ffn_up, g_ffn_down, g_w_in, g_w_out], small_g)
    deltas = order([u[0] for u in big_upd], s_d)
    new_m = order([u[1] for u in big_upd], s_m)
    new_v = order([u[2] for u in big_upd], s_v)
    return (loss, dx0[None], *grads_out, *deltas, *new_m, *new_v)
```

```python
import functools
import math

import jax
import jax.numpy as jnp
from jax import lax
from jax.experimental import pallas as pl
from jax.experimental.pallas import tpu as pltpu

f32 = jnp.float32
bf16 = jnp.bfloat16
HI = lax.Precision.HIGHEST
MESH = pl.DeviceIdType.MESH

EPS = 1e-6
HEAD_DIM = 128
LANES = 128
GDN_CHUNK = 64
CONV_W = 4
MACARON_W = 0.5
ADAM_LR, ADAM_B1, ADAM_B2, ADAM_EPS, ADAM_WD, ADAM_STEP = 0.001, 0.9, 0.999, 1e-08, 0.01, 10
VMEM_LIMIT_V7X = 56 * 1024 * 1024
NEG = -1e30

NN = (((1,), (0,)), ((), ()))
NT = (((1,), (1,)), ((), ()))
TN = (((0,), (0,)), ((), ()))


def _cp(*sem):
    return pltpu.CompilerParams(dimension_semantics=sem, vmem_limit_bytes=VMEM_LIMIT_V7X)


def _dotb(a, b, dn=NN):
    return lax.dot_general(a.astype(bf16), b.astype(bf16), dn, preferred_element_type=f32)


def _doth(a, b, dn=NN):
    return lax.dot_general(a.astype(f32), b.astype(f32), dn, precision=HI, preferred_element_type=f32)


def _sigmoid(x):
    return 1.0 / (1.0 + jnp.exp(-x))


def _softplus(x):
    return jnp.maximum(x, 0.0) + jnp.log(1.0 + jnp.exp(-jnp.abs(x)))


def _lane_col(blk, lane_idx):
    lane = lax.broadcasted_iota(jnp.int32, blk.shape, 1)
    return jnp.sum(jnp.where(lane == lane_idx, blk, 0.0), axis=1, keepdims=True)


def _tile(n, pref, mult=LANES):
    if n <= pref:
        return n
    t = (pref // mult) * mult
    while t >= mult:
        if n % t == 0:
            return t
        t -= mult
    return n


def _mm(name, groups, mode, tiles, epilogue, out_dtypes, extras=(), comm=None):
    a0, b0 = groups[0][0]
    if mode == "nn":
        (m, k), n = a0.shape, b0.shape[1]
    elif mode == "nt":
        (m, k), n = a0.shape, b0.shape[0]
    else:
        (k, m), n = a0.shape, b0.shape[1]
    tm, tn, tk = _tile(m, tiles[0]), _tile(n, tiles[1]), _tile(k, tiles[2])
    nk = k // tk
    assert m % tm == 0 and n % tn == 0 and k % tk == 0, (name, m, n, k, tm, tn, tk)
    if mode == "nn":
        a_spec = pl.BlockSpec((tm, tk), lambda i, j, kk: (i, kk))
        b_spec = pl.BlockSpec((tk, tn), lambda i, j, kk: (kk, j))
        dn = NN
    elif mode == "nt":
        a_spec = pl.BlockSpec((tm, tk), lambda i, j, kk: (i, kk))
        b_spec = pl.BlockSpec((tn, tk), lambda i, j, kk: (j, kk))
        dn = NT
    else:
        a_spec = pl.BlockSpec((tk, tm), lambda i, j, kk: (kk, i))
        b_spec = pl.BlockSpec((tk, tn), lambda i, j, kk: (kk, j))
        dn = TN
    npairs = sum(len(g) for g in groups)
    nacc, nex, nout = len(groups), len(extras), len(out_dtypes)
    in_specs, args = [], []
    for g in groups:
        for a, b in g:
            in_specs += [a_spec, b_spec]
            args += [a, b]
    for arr, kind in extras:
        if kind == "mn":
            in_specs.append(pl.BlockSpec((tm, tn), lambda i, j, kk: (i, j)))
        else:
            in_specs.append(pl.BlockSpec((1, tn), lambda i, j, kk: (0, j)))
        args.append(arr)
    nci = len(comm.ins) if comm else 0
    nco = len(comm.out_shapes) if comm else 0
    grid = (m // tm, n // tn, nk)

    def body(*refs):
        ins = refs[: 2 * npairs]
        ex = refs[2 * npairs: 2 * npairs + nex]
        c_ins = refs[2 * npairs + nex: 2 * npairs + nex + nci]
        o0 = 2 * npairs + nex + nci
        outs = refs[o0: o0 + nout]
        c_outs = refs[o0 + nout: o0 + nout + nco]
        accs = refs[o0 + nout + nco: o0 + nout + nco + nacc]
        c_sems = refs[o0 + nout + nco + nacc:]
        kk = pl.program_id(2)
        if comm:
            step = (pl.program_id(0) * grid[1] + pl.program_id(1)) * nk + kk

            @pl.when(step == 0)
            def _():
                comm.run("start", c_ins, c_outs, c_sems)

        @pl.when(kk == 0)
        def _():
            for acc in accs:
                acc[...] = jnp.zeros_like(acc)

        p = 0
        for gi, g in enumerate(groups):
            t = None
            for _ in g:
                d = _dotb(ins[2 * p][...], ins[2 * p + 1][...], dn)
                t = d if t is None else t + d
                p += 1
            accs[gi][...] += t

        @pl.when(kk == nk - 1)
        def _():
            res = epilogue([acc[...] for acc in accs], [e[...] for e in ex])
            for o, r in zip(outs, res):
                o[...] = r.astype(o.dtype)

        if comm:
            @pl.when(step == grid[0] * grid[1] * nk - 1)
            def _():
                comm.run("finish", c_ins, c_outs, c_sems)

    any_spec = pl.BlockSpec(memory_space=pl.ANY)
    res = pl.pallas_call(
        body, name=name, grid=grid,
        in_specs=in_specs + [any_spec] * nci,
        out_specs=[pl.BlockSpec((tm, tn), lambda i, j, kk: (i, j)) for _ in out_dtypes] + [any_spec] * nco,
        out_shape=[jax.ShapeDtypeStruct((m, n), dt) for dt in out_dtypes] + (list(comm.out_shapes) if comm else []),
        scratch_shapes=[pltpu.VMEM((tm, tn), f32) for _ in range(nacc)] + (comm.sems() if comm else []),
        compiler_params=_cp(*(("arbitrary",) * 3 if comm else ("parallel", "parallel", "arbitrary"))),
    )(*args, *(comm.ins if comm else []))
    return (res[:nout], res[nout:]) if comm else res


def _ep_plain(accs, ex):
    return (accs[0],)


def _ep_colscale(accs, ex):
    return (accs[0] * ex[0],)


def _ep_swiglu(accs, ex):
    gate, up = accs
    act = gate * _sigmoid(gate) * up
    return gate, up, act


def _ep_residual(accs, ex):
    x, gs = ex
    y = accs[0]
    return x + gs * y, y


def _ep_swiglu_bwd(accs, ex):
    gate, up = ex[0].astype(f32), ex[1].astype(f32)
    dact = accs[0]
    sg = _sigmoid(gate)
    silu = gate * sg
    act = silu * up
    dup = dact * silu
    dgate = dact * up * sg * (1.0 + gate * (1.0 - sg))
    return act, dgate, dup


def _row_tile(s):
    return _tile(s, 256, 8)


def _ada_in(name, x, g, shift, scale):
    s, d = x.shape
    tm = _row_tile(s)

    def body(x_ref, g_ref, sh_ref, sc_ref, h_ref):
        xv = x_ref[...]
        r = lax.rsqrt(jnp.mean(xv * xv, axis=-1, keepdims=True) + EPS)
        h_ref[...] = (xv * r * g_ref[...] * (1.0 + sc_ref[...]) + sh_ref[...]).astype(h_ref.dtype)

    row = pl.BlockSpec((1, d), lambda i: (0, 0))
    blk = pl.BlockSpec((tm, d), lambda i: (i, 0))
    return pl.pallas_call(body, name=name, grid=(s // tm,), in_specs=[blk, row, row, row], out_specs=blk,
                          out_shape=jax.ShapeDtypeStruct((s, d), bf16), compiler_params=_cp("parallel"))(x, g, shift, scale)


def _ada_bwd(name, x, g, scale, dh, dres):
    s, d = x.shape
    tm = _row_tile(s)

    def body(x_ref, g_ref, sc_ref, dh_ref, dres_ref, dx_ref, dsh_ref, a_ref):
        i = pl.program_id(0)

        @pl.when(i == 0)
        def _():
            dsh_ref[...] = jnp.zeros_like(dsh_ref)
            a_ref[...] = jnp.zeros_like(a_ref)

        xv = x_ref[...]
        dhv = dh_ref[...].astype(f32)
        r = lax.rsqrt(jnp.mean(xv * xv, axis=-1, keepdims=True) + EPS)
        n = xv * r
        dn = dhv * (g_ref[...] * (1.0 + sc_ref[...]))
        dx_ref[...] = dres_ref[...] + r * (dn - n * jnp.mean(dn * n, axis=-1, keepdims=True))
        dsh_ref[...] += jnp.sum(dhv, axis=0, keepdims=True)
        a_ref[...] += jnp.sum(dhv * n, axis=0, keepdims=True)

    row = pl.BlockSpec((1, d), lambda i: (0, 0))
    blk = pl.BlockSpec((tm, d), lambda i: (i, 0))
    return pl.pallas_call(
        body, name=name, grid=(s // tm,), in_specs=[blk, row, row, blk, blk], out_specs=[blk, row, row],
        out_shape=[jax.ShapeDtypeStruct((s, d), f32), jax.ShapeDtypeStruct((1, d), f32), jax.ShapeDtypeStruct((1, d), f32)],
        compiler_params=_cp("arbitrary"))(x, g, scale, dh, dres)


def _gate_bwd(name, dx, y, gs):
    s, d = dx.shape
    tm = _row_tile(s)

    def body(dx_ref, y_ref, gs_ref, dy_ref, dgs_ref):
        i = pl.program_id(0)

        @pl.when(i == 0)
        def _():
            dgs_ref[...] = jnp.zeros_like(dgs_ref)

        dxv = dx_ref[...]
        dy_ref[...] = (dxv * gs_ref[...]).astype(dy_ref.dtype)
        dgs_ref[...] += jnp.sum(dxv * y_ref[...].astype(f32), axis=0, keepdims=True)

    row = pl.BlockSpec((1, d), lambda i: (0, 0))
    blk = pl.BlockSpec((tm, d), lambda i: (i, 0))
    return pl.pallas_call(
        body, name=name, grid=(s // tm,), in_specs=[blk, blk, row], out_specs=[blk, row],
        out_shape=[jax.ShapeDtypeStruct((s, d), bf16), jax.ShapeDtypeStruct((1, d), f32)],
        compiler_params=_cp("arbitrary"))(dx, y, gs)


def _final_loss(x, fg, target):
    s, d = x.shape
    tm = _row_tile(s)

    def body(x_ref, g_ref, t_ref, loss_ref, dx_ref, dg_ref):
        i = pl.program_id(0)

        @pl.when(i == 0)
        def _():
            loss_ref[...] = jnp.zeros_like(loss_ref)
            dg_ref[...] = jnp.zeros_like(dg_ref)

        xv = x_ref[...]
        gv = g_ref[...]
        r = lax.rsqrt(jnp.mean(xv * xv, axis=-1, keepdims=True) + EPS)
        n = xv * r
        e = n * gv - t_ref[...]
        per_tok = jnp.mean(e * e, axis=-1, keepdims=True)
        loss_ref[...] += 0.5 * jnp.sum(per_tok, axis=0, keepdims=True) * jnp.ones((1, LANES), f32)
        dy = e * (1.0 / d)
        dg_ref[...] += jnp.sum(dy * n, axis=0, keepdims=True)
        dn = dy * gv
        dx_ref[...] = r * (dn - n * jnp.mean(dn * n, axis=-1, keepdims=True))

    row = pl.BlockSpec((1, d), lambda i: (0, 0))
    blk = pl.BlockSpec((tm, d), lambda i: (i, 0))
    return pl.pallas_call(
        body, name="final_loss", grid=(s // tm,), in_specs=[blk, row, blk],
        out_specs=[pl.BlockSpec((1, LANES), lambda i: (0, 0)), blk, row],
        out_shape=[jax.ShapeDtypeStruct((1, LANES), f32), jax.ShapeDtypeStruct((s, d), f32), jax.ShapeDtypeStruct((1, d), f32)],
        compiler_params=_cp("arbitrary"))(x, fg, target)


def _small_fwd(ps, prm, nh):
    s = ps.shape[0]
    tb = LANES

    def body(ps_ref, prm_ref, sm_ref, cum_ref, carry):
        i = pl.program_id(0)

        @pl.when(i == 0)
        def _():
            carry[...] = jnp.zeros_like(carry)

        x = ps_ref[...]
        lane = lax.broadcasted_iota(jnp.int32, x.shape, 1)
        fb, dtb, alog = prm_ref[0:1, :], prm_ref[1:2, :], prm_ref[2:3, :]
        logf = -_softplus(-(x + fb))
        glog = -jnp.exp(alog) * _softplus(x + dtb)
        beta = _sigmoid(x)
        sm = jnp.where(lane < nh, logf, jnp.where(lane < 2 * nh, glog, jnp.where(lane < 3 * nh, beta, 0.0)))
        sm_ref[...] = sm
        r = lax.broadcasted_iota(jnp.int32, (tb, tb), 0)
        c = lax.broadcasted_iota(jnp.int32, (tb, tb), 1)
        tril = (c <= r).astype(f32)
        cs = _doth(tril, sm) + carry[...]
        cum_ref[...] = cs
        carry[...] = cs[tb - 1:tb, :]

    blk = pl.BlockSpec((tb, LANES), lambda i: (i, 0))
    return pl.pallas_call(
        body, name="small_fwd", grid=(s // tb,),
        in_specs=[blk, pl.BlockSpec((8, LANES), lambda i: (0, 0))],
        out_specs=[blk, blk],
        out_shape=[jax.ShapeDtypeStruct((s, LANES), f32), jax.ShapeDtypeStruct((s, LANES), f32)],
        scratch_shapes=[pltpu.VMEM((1, LANES), f32)],
        compiler_params=_cp("arbitrary"))(ps, prm)


def _small_bwd(ps, prm, dsm, dcum, nh):
    s = ps.shape[0]
    tb = LANES
    nb = s // tb

    def body(ps_ref, prm_ref, dsm_ref, dct_ref, dps_ref, pg_ref, carry):
        i = pl.program_id(0)

        @pl.when(i == 0)
        def _():
            carry[...] = jnp.zeros_like(carry)
            pg_ref[...] = jnp.zeros_like(pg_ref)

        x = ps_ref[...]
        dsm = dsm_ref[...]
        lane = lax.broadcasted_iota(jnp.int32, x.shape, 1)
        fb, dtb, alog = prm_ref[0:1, :], prm_ref[1:2, :], prm_ref[2:3, :]
        r = lax.broadcasted_iota(jnp.int32, (tb, tb), 0)
        c = lax.broadcasted_iota(jnp.int32, (tb, tb), 1)
        triu = (c >= r).astype(f32)
        dlogf = _doth(triu, dct_ref[...]) + carry[...]
        carry[...] = dlogf[0:1, :]
        d_f = dlogf * _sigmoid(-(x + fb))
        nega = -jnp.exp(alog)
        xa = x + dtb
        glog = nega * _softplus(xa)
        d_a = dsm * nega * _sigmoid(xa)
        beta = _sigmoid(x)
        d_b = dsm * beta * (1.0 - beta)
        dps = jnp.where(lane < nh, d_f, jnp.where(lane < 2 * nh, d_a, jnp.where(lane < 3 * nh, d_b, 0.0)))
        dps_ref[...] = dps.astype(dps_ref.dtype)
        row0 = jnp.sum(dps, axis=0, keepdims=True)
        row1 = jnp.sum(jnp.where((lane >= nh) & (lane < 2 * nh), dsm * glog, 0.0), axis=0, keepdims=True)
        sub = lax.broadcasted_iota(jnp.int32, (8, LANES), 0)
        pg_ref[...] += jnp.where(sub == 0, row0, jnp.where(sub == 1, row1, 0.0))

    rev = pl.BlockSpec((tb, LANES), lambda i: (nb - 1 - i, 0))
    fix = pl.BlockSpec((8, LANES), lambda i: (0, 0))
    return pl.pallas_call(
        body, name="small_bwd", grid=(nb,),
        in_specs=[rev, fix, rev, rev],
        out_specs=[rev, fix],
        out_shape=[jax.ShapeDtypeStruct((s, LANES), bf16), jax.ShapeDtypeStruct((8, LANES), f32)],
        scratch_shapes=[pltpu.VMEM((1, LANES), f32)],
        compiler_params=_cp("arbitrary"))(ps, prm, dsm, dcum)


LOG2E = 1.4426950408889634
LN2 = 0.6931471805599453
AUG = 2 * HEAD_DIM
FOX_Q_SCALE = LOG2E / math.sqrt(HEAD_DIM)
FOX_KEY_GROUP = 4


def _split3(col):
    hi = col.astype(bf16).astype(f32)
    r1 = col - hi
    mid = r1.astype(bf16).astype(f32)
    lo = (r1 - mid).astype(bf16).astype(f32)
    return hi, mid, lo


def _aug_block(rows, terms, terms_at, ones_at=None):
    lane = lax.broadcasted_iota(jnp.int32, (rows, LANES), 1)
    blk = jnp.zeros((rows, LANES), f32) if ones_at is None else jnp.where((lane >= ones_at) & (lane < ones_at + 3), 1.0, 0.0)
    for i, t in enumerate(terms):
        blk = jnp.where(lane == terms_at + i, t, blk)
    return blk


def _fox_aug(qkv, cum, nh):
    s = qkv.shape[0]
    tm = _row_tile(s)

    def body(q_ref, k_ref, v_ref, cum_ref, qa_ref, ka_ref, va_ref):
        h = pl.program_id(1)
        c2 = _lane_col(cum_ref[...], h) * LOG2E
        hi, mid, lo = _split3(c2)
        qa_ref[:, :HEAD_DIM] = q_ref[...]
        qa_ref[:, HEAD_DIM:] = _aug_block(tm, (hi, mid, lo), 0, 3).astype(bf16)
        ka_ref[:, :HEAD_DIM] = k_ref[...]
        ka_ref[:, HEAD_DIM:] = _aug_block(tm, (-hi, -mid, -lo), 3, 0).astype(bf16)
        va_ref[:, :HEAD_DIM] = v_ref[...]
        va_ref[:, HEAD_DIM:] = _aug_block(tm, (), 0, 0).astype(bf16)

    ab = pl.BlockSpec((tm, AUG), lambda i, h: (i, h))
    return pl.pallas_call(
        body, name="fox_aug", grid=(s // tm, nh),
        in_specs=[pl.BlockSpec((tm, HEAD_DIM), lambda i, h: (i, h)), pl.BlockSpec((tm, HEAD_DIM), lambda i, h: (i, nh + h)),
                  pl.BlockSpec((tm, HEAD_DIM), lambda i, h: (i, 2 * nh + h)), pl.BlockSpec((tm, LANES), lambda i, h: (i, 0))],
        out_specs=[ab, ab, ab], out_shape=[jax.ShapeDtypeStruct((s, nh * AUG), bf16)] * 3,
        compiler_params=_cp("parallel", "parallel"))(qkv, qkv, qkv, cum)


def _fox_fwd(qa, ka, qkv, wn, nh, tq):
    s = qa.shape[0]
    fw = nh * HEAD_DIM
    group = FOX_KEY_GROUP if s // tq >= 2 * FOX_KEY_GROUP else 2

    def body(qa_ref, ka_ref, v_ref, wn_ref, o_ref, on_ref, lse_ref):
        i = pl.program_id(1)
        q = qa_ref[...]

        def logits_t(j, rows):
            return _dotb(ka_ref[pl.ds(pl.multiple_of(j * tq, tq), rows), :], q, NT)

        def pv_t(j, p_t):
            return _dotb(v_ref[pl.ds(pl.multiple_of(j * tq, tq), p_t.shape[0]), :], p_t, TN)

        key = lax.broadcasted_iota(jnp.int32, (tq, tq), 0)
        qry = lax.broadcasted_iota(jnp.int32, (tq, tq), 1)
        js = [i] + [jnp.maximum(i - n, 0) for n in range(1, group)]
        ts = [jnp.where(key <= qry, logits_t(i, tq), NEG)]
        ts += [jnp.where(lax.rem(i, group) >= n, logits_t(js[n], tq), NEG) for n in range(1, group)]
        m = functools.reduce(jnp.maximum, [jnp.max(t, axis=0, keepdims=True) for t in ts])
        ps = [jnp.exp2(t - m) for t in ts]
        l = functools.reduce(jnp.add, [jnp.sum(p, axis=0, keepdims=True) for p in ps])
        acc = functools.reduce(jnp.add, [pv_t(j, p) for j, p in zip(js, ps)])

        def step(jj, carry):
            m, l, acc = carry
            t = logits_t(group * jj, group * tq)
            mn = jnp.maximum(m, jnp.max(t, axis=0, keepdims=True))
            p_t = jnp.exp2(t - mn)
            alpha = jnp.exp2(m - mn)
            return mn, alpha * l + jnp.sum(p_t, axis=0, keepdims=True), alpha * acc + pv_t(group * jj, p_t)

        m, l, acc = lax.fori_loop(0, i // group, step, (m, l, acc))
        o = (acc / l).T
        o_ref[...] = o
        sub = lax.broadcasted_iota(jnp.int32, (LANES, tq), 0)
        lse_ref[0] = jnp.where(sub == 0, m + jnp.log2(l), 0.0).T
        r = lax.rsqrt(jnp.mean(o * o, axis=-1, keepdims=True) + EPS)
        on_ref[...] = (o * r * wn_ref[...]).astype(on_ref.dtype)

    hb = pl.BlockSpec((tq, HEAD_DIM), lambda h, i: (i, h))
    return pl.pallas_call(
        body, name="fox_fwd", grid=(nh, s // tq),
        in_specs=[pl.BlockSpec((tq, AUG), lambda h, i: (i, h)), pl.BlockSpec((s, AUG), lambda h, i: (0, h)),
                  pl.BlockSpec((s, HEAD_DIM), lambda h, i: (0, 2 * nh + h)), pl.BlockSpec((1, HEAD_DIM), lambda h, i: (0, 0))],
        out_specs=[hb, hb, pl.BlockSpec((1, tq, LANES), lambda h, i: (h, i, 0))],
        out_shape=[jax.ShapeDtypeStruct((s, fw), f32), jax.ShapeDtypeStruct((s, fw), bf16), jax.ShapeDtypeStruct((nh, s, LANES), f32)],
        compiler_params=_cp("parallel", "parallel"))(qa, ka, qkv, wn)


def _fox_post_bwd(don, o, lse2, cum, qkv, wn, nh):
    s, fw = o.shape
    tm = _row_tile(s)

    def body(don_ref, o_ref, lse_ref, cum_ref, q_ref, wn_ref, qb_ref, doa_ref, dwn_ref):
        i = pl.program_id(0)
        h = pl.program_id(1)

        @pl.when((i == 0) & (h == 0))
        def _():
            dwn_ref[...] = jnp.zeros_like(dwn_ref)

        o = o_ref[...]
        don = don_ref[...].astype(f32)
        r = lax.rsqrt(jnp.mean(o * o, axis=-1, keepdims=True) + EPS)
        n = o * r
        dwn_ref[...] += jnp.sum(don * n, axis=0, keepdims=True)
        dn = don * wn_ref[...]
        do = r * (dn - n * jnp.mean(dn * n, axis=-1, keepdims=True))
        delta = jnp.sum(do * o, axis=-1, keepdims=True)
        a2 = _lane_col(cum_ref[...], h) * LOG2E - _lane_col(lse_ref[0], 0)
        qb_ref[:, :HEAD_DIM] = q_ref[...]
        qb_ref[:, HEAD_DIM:] = _aug_block(tm, _split3(a2), 0, 3).astype(bf16)
        doa_ref[:, :HEAD_DIM] = do.astype(bf16)
        doa_ref[:, HEAD_DIM:] = _aug_block(tm, _split3(-delta), 0).astype(bf16)

    hb = pl.BlockSpec((tm, HEAD_DIM), lambda i, h: (i, h))
    ab = pl.BlockSpec((tm, AUG), lambda i, h: (i, h))
    return pl.pallas_call(
        body, name="fox_post_bwd", grid=(s // tm, nh),
        in_specs=[hb, hb, pl.BlockSpec((1, tm, LANES), lambda i, h: (h, i, 0)), pl.BlockSpec((tm, LANES), lambda i, h: (i, 0)),
                  hb, pl.BlockSpec((1, HEAD_DIM), lambda i, h: (0, 0))],
        out_specs=[ab, ab, pl.BlockSpec((1, HEAD_DIM), lambda i, h: (0, 0))],
        out_shape=[jax.ShapeDtypeStruct((s, nh * AUG), bf16), jax.ShapeDtypeStruct((s, nh * AUG), bf16),
                   jax.ShapeDtypeStruct((1, HEAD_DIM), f32)],
        compiler_params=_cp("arbitrary", "arbitrary"))(don, o, lse2, cum, qkv, wn)


def _fox_bwd(qb, doa, ka, va, nh, tq):
    s = qb.shape[0]
    nq = s // tq
    fw = nh * HEAD_DIM

    def body(qb_ref, doa_ref, ka_ref, va_ref, dqx_ref, dkx_ref, dv_ref, dv_acc):
        j = pl.program_id(1)

        @pl.when(j == 0)
        def _():
            dqx_ref[...] = jnp.zeros_like(dqx_ref)

        kj = ka_ref[...]
        vj = va_ref[...]

        def tile(i, first=False, keep=None):
            off = pl.multiple_of(i * tq, tq)
            qi = qb_ref[pl.ds(off, tq), :]
            doi = doa_ref[pl.ds(off, tq), :]
            p = jnp.exp2(_dotb(qi, kj, NT))
            if keep is not None:
                p = jnp.where(keep, p, 0.0)
            ds = (p * _dotb(doi, vj, NT)).astype(bf16)
            dv = _dotb(p, doi[:, :HEAD_DIM], TN)
            dk = _dotb(ds, qi, TN)
            if first:
                dv_acc[...] = dv
                dkx_ref[...] = dk
            else:
                dv_acc[...] += dv
                dkx_ref[...] += dk
            dqx_ref[pl.ds(off, tq), :] += _dotb(ds, kj)

        rows = lax.broadcasted_iota(jnp.int32, (tq, tq), 0)
        cols = lax.broadcasted_iota(jnp.int32, (tq, tq), 1)
        tile(j, first=True, keep=cols <= rows)
        n = nq - 1 - j
        tile(jnp.minimum(j + 1, nq - 1), keep=lax.rem(n, 2) == 1)

        def pair(ii, carry):
            i0 = j + 1 + lax.rem(n, 2) + 2 * ii
            tile(i0)
            tile(i0 + 1)
            return carry

        lax.fori_loop(0, n // 2, pair, 0)
        dv_ref[...] = dv_acc[...].astype(dv_ref.dtype)

    panel = pl.BlockSpec((s, AUG), lambda h, j: (0, h))
    blk = pl.BlockSpec((tq, AUG), lambda h, j: (j, h))
    return pl.pallas_call(
        body, name="fox_bwd", grid=(nh, nq), in_specs=[panel, panel, blk, blk],
        out_specs=[panel, blk, pl.BlockSpec((tq, HEAD_DIM), lambda h, j: (j, h))],
        out_shape=[jax.ShapeDtypeStruct((s, nh * AUG), f32), jax.ShapeDtypeStruct((s, nh * AUG), f32),
                   jax.ShapeDtypeStruct((s, fw), bf16)],
        scratch_shapes=[pltpu.VMEM((tq, HEAD_DIM), f32)],
        compiler_params=_cp("parallel", "arbitrary"))(qb, doa, ka, va)


def _fox_unpack(dqx, dkx, nh):
    s = dqx.shape[0]
    fw = nh * HEAD_DIM
    tm = _row_tile(s)

    def body(dqx_ref, dkx_ref, dq_ref, dk_ref, dcum_ref):
        h = pl.program_id(1)

        @pl.when(h == 0)
        def _():
            dcum_ref[...] = jnp.zeros_like(dcum_ref)

        dq_ref[...] = (dqx_ref[:, :HEAD_DIM] * (HEAD_DIM ** -0.5)).astype(dq_ref.dtype)
        dk_ref[...] = (dkx_ref[:, :HEAD_DIM] * LN2).astype(dk_ref.dtype)
        d = _lane_col(dqx_ref[:, HEAD_DIM:], 0) - _lane_col(dkx_ref[:, HEAD_DIM:], 3)
        lane = lax.broadcasted_iota(jnp.int32, (tm, LANES), 1)
        dcum_ref[...] += jnp.where(lane == h, d, 0.0)

    ab = pl.BlockSpec((tm, AUG), lambda i, h: (i, h))
    hb = pl.BlockSpec((tm, HEAD_DIM), lambda i, h: (i, h))
    return pl.pallas_call(
        body, name="fox_unpack", grid=(s // tm, nh), in_specs=[ab, ab],
        out_specs=[hb, hb, pl.BlockSpec((tm, LANES), lambda i, h: (i, 0))],
        out_shape=[jax.ShapeDtypeStruct((s, fw), bf16), jax.ShapeDtypeStruct((s, fw), bf16), jax.ShapeDtypeStruct((s, LANES), f32)],
        compiler_params=_cp("parallel", "arbitrary"))(dqx, dkx)


def _conv_pre(xx, w, tm):
    pre = None
    for k in range(CONV_W):
        sh = CONV_W - 1 - k
        t = (pltpu.roll(xx, sh, 0) if sh else xx)[8:, :] * w[k:k + 1, :]
        pre = t if pre is None else pre + t
    return pre


def _gdn_pre(x, w, nh):
    s, cw = x.shape
    tm = _row_tile(s)
    fw = nh * HEAD_DIM

    def body(x_ref, prev_ref, w_ref, y_ref):
        i = pl.program_id(0)
        j = pl.program_id(1)
        for h in range(nh):
            sl = slice(h * HEAD_DIM, (h + 1) * HEAD_DIM)
            prev = jnp.where(i == 0, 0.0, prev_ref[:, sl])
            pre = _conv_pre(jnp.concatenate([prev, x_ref[:, sl]], axis=0), w_ref[:, sl], tm)
            y = pre * _sigmoid(pre)
            yn = y * lax.rsqrt(jnp.sum(y * y, axis=-1, keepdims=True) + EPS)
            y_ref[:, sl] = jnp.where(j < 2, yn, y)

    return pl.pallas_call(
        body, name="gdn_pre", grid=(s // tm, cw // fw),
        in_specs=[pl.BlockSpec((tm, fw), lambda i, j: (i, j)),
                  pl.BlockSpec((8, fw), lambda i, j: (jnp.maximum(i * (tm // 8) - 1, 0), j)),
                  pl.BlockSpec((CONV_W, fw), lambda i, j: (0, j))],
        out_specs=pl.BlockSpec((tm, fw), lambda i, j: (i, j)),
        out_shape=jax.ShapeDtypeStruct((s, cw), f32),
        compiler_params=_cp("parallel", "parallel"))(x, x, w)


def _gdn_pre_bwd(x, w, dyn, nh):
    s, cw = x.shape
    tm = _row_tile(s)
    fw = nh * HEAD_DIM

    def body(x_ref, prev_ref, w_ref, dyn_ref, dpre_ref):
        i = pl.program_id(0)
        j = pl.program_id(1)
        for h in range(nh):
            sl = slice(h * HEAD_DIM, (h + 1) * HEAD_DIM)
            prev = jnp.where(i == 0, 0.0, prev_ref[:, sl])
            pre = _conv_pre(jnp.concatenate([prev, x_ref[:, sl]], axis=0), w_ref[:, sl], tm)
            sg = _sigmoid(pre)
            y = pre * sg
            dyn = dyn_ref[:, sl]
            r = lax.rsqrt(jnp.sum(y * y, axis=-1, keepdims=True) + EPS)
            yn = y * r
            dy_n = r * (dyn - yn * jnp.sum(dyn * yn, axis=-1, keepdims=True))
            dy = jnp.where(j < 2, dy_n, dyn)
            dpre_ref[:, sl] = dy * sg * (1.0 + pre * (1.0 - sg))

    hb = pl.BlockSpec((tm, fw), lambda i, j: (i, j))
    return pl.pallas_call(
        body, name="gdn_pre_bwd", grid=(s // tm, cw // fw),
        in_specs=[hb, pl.BlockSpec((8, fw), lambda i, j: (jnp.maximum(i * (tm // 8) - 1, 0), j)),
                  pl.BlockSpec((CONV_W, fw), lambda i, j: (0, j)), hb],
        out_specs=hb, out_shape=jax.ShapeDtypeStruct((s, cw), f32),
        compiler_params=_cp("parallel", "parallel"))(x, x, w, dyn)


def _conv_bwd(x, w, dpre, nh):
    s, cw = x.shape
    tm = _row_tile(s)
    fw = nh * HEAD_DIM
    ni = s // tm

    def body(x_ref, prev_ref, w_ref, dp_ref, nxt_ref, dx_ref, dw_ref):
        i = pl.program_id(1)

        @pl.when(i == 0)
        def _():
            dw_ref[...] = jnp.zeros_like(dw_ref)

        for h in range(nh):
            sl = slice(h * HEAD_DIM, (h + 1) * HEAD_DIM)
            wv = w_ref[:, sl]
            dp = dp_ref[:, sl]
            nxt = jnp.where(i == ni - 1, 0.0, nxt_ref[:, sl])
            dd = jnp.concatenate([dp, nxt], axis=0)
            prev = jnp.where(i == 0, 0.0, prev_ref[:, sl])
            xx = jnp.concatenate([prev, x_ref[:, sl]], axis=0)
            dx = None
            rows = []
            for k in range(CONV_W):
                sh = CONV_W - 1 - k
                t = (pltpu.roll(dd, tm + 8 - sh, 0) if sh else dd)[:tm, :] * wv[k:k + 1, :]
                dx = t if dx is None else dx + t
                xs = (pltpu.roll(xx, sh, 0) if sh else xx)[8:, :]
                rows.append(jnp.sum(dp * xs, axis=0, keepdims=True))
            dx_ref[:, sl] = dx.astype(dx_ref.dtype)
            dw_ref[:, sl] += jnp.concatenate(rows, axis=0)

    hb = pl.BlockSpec((tm, fw), lambda j, i: (i, j))
    wb = pl.BlockSpec((CONV_W, fw), lambda j, i: (0, j))
    return pl.pallas_call(
        body, name="conv_bwd", grid=(cw // fw, ni),
        in_specs=[hb, pl.BlockSpec((8, fw), lambda j, i: (jnp.maximum(i * (tm // 8) - 1, 0), j)), wb, hb,
                  pl.BlockSpec((8, fw), lambda j, i: (jnp.minimum((i + 1) * (tm // 8), s // 8 - 1), j))],
        out_specs=[hb, wb],
        out_shape=[jax.ShapeDtypeStruct((s, cw), bf16), jax.ShapeDtypeStruct((CONV_W, cw), f32)],
        compiler_params=_cp("parallel", "arbitrary"))(x, x, w, dpre, dpre)


def _chunk_consts():
    c = GDN_CHUNK
    r = lax.broadcasted_iota(jnp.int32, (c, c), 0)
    q = lax.broadcasted_iota(jnp.int32, (c, c), 1)
    return r >= q, r > q, (r == q).astype(f32)


def _chunk_head(qkvn, sm, gcs, gcs_t, h, nh):
    fw = nh * HEAD_DIM
    q = qkvn[:, h * HEAD_DIM:(h + 1) * HEAD_DIM] * (HEAD_DIM ** -0.5)
    k = qkvn[:, fw + h * HEAD_DIM: fw + (h + 1) * HEAD_DIM]
    v = qkvn[:, 2 * fw + h * HEAD_DIM: 2 * fw + (h + 1) * HEAD_DIM]
    beta = _lane_col(sm, 2 * nh + h)
    gc = _lane_col(gcs, nh + h)
    gc_row = gcs_t[nh + h: nh + h + 1, :]
    incl, strict, _ = _chunk_consts()
    decay = jnp.where(incl, jnp.exp(jnp.minimum(gc - gc_row, 0.0)), 0.0)
    eg = jnp.exp(gc)
    g_last = gc[GDN_CHUNK - 1:GDN_CHUNK, :]
    egl = jnp.exp(g_last)
    ekd = jnp.exp(g_last - gc)
    kb = k * beta
    vb = v * beta
    kk = _dotb(kb, k, NT)
    qk = _dotb(q, k, NT)
    return dict(q=q, k=k, v=v, beta=beta, gc=gc, decay=decay, eg=eg, egl=egl, ekd=ekd, kb=kb, vb=vb, kk=kk, qk=qk,
                incl=incl, strict=strict)


def _unit_lower_inverses(lows, eye):
    ps = [-low for low in lows]
    ts = [eye + p for p in ps]
    for _ in range(5):
        ps = [_doth(p, p) for p in ps]
        ts = [t + _doth(t, p) for t, p in zip(ts, ps)]
    return ts


def _gdn_fwd(qkvn, sm, z, wn, nh):
    s = qkvn.shape[0]
    c = GDN_CHUNK
    nc = s // c
    fw = nh * HEAD_DIM

    def body(qkvn_ref, sm_ref, z_ref, wn_ref, on_ref, o_ref, st_ref, ti_ref, state):
        ci = pl.program_id(0)

        @pl.when(ci == 0)
        def _():
            state[...] = jnp.zeros_like(state)

        qkvn_v = qkvn_ref[...]
        sm_v = sm_ref[...]
        incl, strict, eye = _chunk_consts()
        gcs = _doth(incl.astype(f32), sm_v)
        gcs_t = gcs.T
        heads = range(nh)
        es = [_chunk_head(qkvn_v, sm_v, gcs, gcs_t, h, nh) for h in heads]
        tinvs = _unit_lower_inverses([jnp.where(strict, e["kk"] * e["decay"], 0.0) for e in es], eye)
        us = [_doth(t, e["vb"]) for t, e in zip(tinvs, es)]
        ws = [_doth(t, e["kb"] * e["eg"]) for t, e in zip(tinvs, es)]
        sts = [state[h] for h in heads]
        v_news = [u - _dotb(w, st) for u, w, st in zip(us, ws, sts)]
        qss = [_dotb(e["q"] * e["eg"], st) for e, st in zip(es, sts)]
        os_ = [qs + _dotb(jnp.where(incl, e["qk"] * e["decay"], 0.0), vn) for qs, e, vn in zip(qss, es, v_news)]
        upd = [_dotb(e["k"] * e["ekd"], vn, TN) for e, vn in zip(es, v_news)]
        for h in heads:
            st_ref[0, h] = sts[h]
            ti_ref[0, h] = tinvs[h]
            state[h] = sts[h] * es[h]["egl"] + upd[h]
            sl = slice(h * HEAD_DIM, (h + 1) * HEAD_DIM)
            o = os_[h]
            o_ref[:, sl] = o
            zz = z_ref[:, sl]
            r = lax.rsqrt(jnp.mean(o * o, axis=-1, keepdims=True) + EPS)
            on_ref[:, sl] = (o * r * wn_ref[...] * (zz * _sigmoid(zz))).astype(on_ref.dtype)

    return pl.pallas_call(
        body, name="gdn_fwd", grid=(nc,),
        in_specs=[pl.BlockSpec((c, 3 * fw), lambda i: (i, 0)), pl.BlockSpec((c, LANES), lambda i: (i, 0)),
                  pl.BlockSpec((c, fw), lambda i: (i, 0)), pl.BlockSpec((1, HEAD_DIM), lambda i: (0, 0))],
        out_specs=[pl.BlockSpec((c, fw), lambda i: (i, 0)), pl.BlockSpec((c, fw), lambda i: (i, 0)),
                   pl.BlockSpec((1, nh, HEAD_DIM, HEAD_DIM), lambda i: (i, 0, 0, 0)),
                   pl.BlockSpec((1, nh, c, c), lambda i: (i, 0, 0, 0))],
        out_shape=[jax.ShapeDtypeStruct((s, fw), bf16), jax.ShapeDtypeStruct((s, fw), f32),
                   jax.ShapeDtypeStruct((nc, nh, HEAD_DIM, HEAD_DIM), f32), jax.ShapeDtypeStruct((nc, nh, c, c), f32)],
        scratch_shapes=[pltpu.VMEM((nh, HEAD_DIM, HEAD_DIM), f32)],
        compiler_params=_cp("arbitrary"))(qkvn, sm, z, wn)


def _gdn_post_bwd(don, o, z, wn, nh):
    s, fw = o.shape
    tm = _row_tile(s)

    def body(don_ref, o_ref, z_ref, wn_ref, do_ref, dz_ref, dwn_ref):
        i = pl.program_id(0)
        h = pl.program_id(1)

        @pl.when((i == 0) & (h == 0))
        def _():
            dwn_ref[...] = jnp.zeros_like(dwn_ref)

        o = o_ref[...]
        zz = z_ref[...]
        don = don_ref[...].astype(f32)
        wv = wn_ref[...]
        r = lax.rsqrt(jnp.mean(o * o, axis=-1, keepdims=True) + EPS)
        n = o * r
        sg = _sigmoid(zz)
        silu = zz * sg
        dz_ref[...] = (don * n * wv * sg * (1.0 + zz * (1.0 - sg))).astype(dz_ref.dtype)
        dnw = don * silu
        dwn_ref[...] += jnp.sum(dnw * n, axis=0, keepdims=True)
        dn = dnw * wv
        do_ref[...] = r * (dn - n * jnp.mean(dn * n, axis=-1, keepdims=True))

    hb = pl.BlockSpec((tm, HEAD_DIM), lambda i, h: (i, h))
    wb = pl.BlockSpec((1, HEAD_DIM), lambda i, h: (0, 0))
    return pl.pallas_call(
        body, name="gdn_post_bwd", grid=(s // tm, nh), in_specs=[hb, hb, hb, wb], out_specs=[hb, hb, wb],
        out_shape=[jax.ShapeDtypeStruct((s, fw), f32), jax.ShapeDtypeStruct((s, fw), bf16),
                   jax.ShapeDtypeStruct((1, HEAD_DIM), f32)],
        compiler_params=_cp("arbitrary", "arbitrary"))(don, o, z, wn)


def _gdn_bwd(qkvn, sm, do, states, tinvs, nh):
    s = qkvn.shape[0]
    c = GDN_CHUNK
    nc = s // c
    fw = nh * HEAD_DIM

    def body(qkvn_ref, sm_ref, do_ref, st_ref, ti_ref, dqkvn_ref, dsm_ref, dstate):
        ci = pl.program_id(0)

        @pl.when(ci == 0)
        def _():
            dstate[...] = jnp.zeros_like(dstate)

        qkvn_v = qkvn_ref[...]
        sm_v = sm_ref[...]
        incl, strict, eye = _chunk_consts()
        inclf = incl.astype(f32)
        gcs = _doth(inclf, sm_v)
        gcs_t = gcs.T
        lane = lax.broadcasted_iota(jnp.int32, (c, LANES), 1)
        last_row = lax.broadcasted_iota(jnp.int32, (c, 1), 0) == c - 1
        ones_cl = jnp.ones((c, LANES), f32)
        each = lambda f: [f(h) for h in range(nh)]
        es = each(lambda h: _chunk_head(qkvn_v, sm_v, gcs, gcs_t, h, nh))
        tinv = each(lambda h: ti_ref[0, h])
        st = each(lambda h: st_ref[0, h])
        dst = each(lambda h: dstate[h])
        do = each(lambda h: do_ref[:, h * HEAD_DIM:(h + 1) * HEAD_DIM])
        kg = each(lambda h: es[h]["kb"] * es[h]["eg"])
        qg = each(lambda h: es[h]["q"] * es[h]["eg"])
        kd = each(lambda h: es[h]["k"] * es[h]["ekd"])
        u = each(lambda h: _doth(tinv[h], es[h]["vb"]))
        w = each(lambda h: _doth(tinv[h], kg[h]))
        a = each(lambda h: jnp.where(incl, es[h]["qk"] * es[h]["decay"], 0.0))
        v_new = each(lambda h: u[h] - _dotb(w[h], st[h]))
        dv_new = each(lambda h: _dotb(a[h], do[h], TN) + _dotb(kd[h], dst[h]))
        da = each(lambda h: jnp.where(incl, _dotb(do[h], v_new[h], NT), 0.0))
        dqg = each(lambda h: _dotb(do[h], st[h], NT))
        dkd = each(lambda h: _dotb(v_new[h], dst[h], NT))
        dglast = each(lambda h: es[h]["egl"] * jnp.sum(jnp.sum(dst[h] * st[h], axis=1, keepdims=True), axis=0, keepdims=True))
        dw = each(lambda h: -_dotb(dv_new[h], st[h], NT))
        new_dst = each(lambda h: _dotb(qg[h], do[h], TN) + es[h]["egl"] * dst[h] - _dotb(w[h], dv_new[h], TN))
        dtinv = each(lambda h: _doth(dv_new[h], es[h]["vb"], NT) + _doth(dw[h], kg[h], NT))
        dvb = each(lambda h: _doth(tinv[h], dv_new[h], TN))
        dkg = each(lambda h: _doth(tinv[h], dw[h], TN))
        tdt = each(lambda h: _doth(tinv[h], dtinv[h], TN))
        dlow = each(lambda h: -_doth(tdt[h], tinv[h], NT))
        dkk = each(lambda h: jnp.where(strict, dlow[h] * es[h]["decay"], 0.0))
        dqk = each(lambda h: da[h] * es[h]["decay"])
        darg = each(lambda h: (jnp.where(strict, dlow[h] * es[h]["kk"], 0.0) + da[h] * es[h]["qk"]) * es[h]["decay"])
        dgc = each(lambda h: jnp.sum(darg[h], axis=1, keepdims=True) - _doth(darg[h], ones_cl, TN)[:, 0:1])
        dkb = each(lambda h: _dotb(dkk[h], es[h]["k"]) + dkg[h] * es[h]["eg"])
        dk = each(lambda h: _dotb(dkk[h], es[h]["kb"], TN) + _dotb(dqk[h], es[h]["q"], TN) + dkd[h] * es[h]["ekd"]
                  + dkb[h] * es[h]["beta"])
        dq = each(lambda h: (_dotb(dqk[h], es[h]["k"]) + dqg[h] * es[h]["eg"]) * (HEAD_DIM ** -0.5))
        s_kd = each(lambda h: jnp.sum(dkd[h] * kd[h], axis=1, keepdims=True))
        dgc = each(lambda h: dgc[h] + jnp.sum(dkg[h] * kg[h] + dqg[h] * qg[h], axis=1, keepdims=True) - s_kd[h]
                   + jnp.where(last_row, jnp.sum(s_kd[h], axis=0, keepdims=True) + dglast[h], 0.0))
        dg = each(lambda h: _doth(inclf, dgc[h] * ones_cl, TN)[:, 0:1])
        dsm = jnp.zeros((c, LANES), f32)
        for h in range(nh):
            dstate[h] = new_dst[h]
            dbeta = jnp.sum(dkb[h] * es[h]["k"] + dvb[h] * es[h]["v"], axis=1, keepdims=True)
            dqkvn_ref[:, h * HEAD_DIM:(h + 1) * HEAD_DIM] = dq[h]
            dqkvn_ref[:, fw + h * HEAD_DIM: fw + (h + 1) * HEAD_DIM] = dk[h]
            dqkvn_ref[:, 2 * fw + h * HEAD_DIM: 2 * fw + (h + 1) * HEAD_DIM] = dvb[h] * es[h]["beta"]
            dsm = dsm + jnp.where(lane == nh + h, dg[h], 0.0) + jnp.where(lane == 2 * nh + h, dbeta, 0.0)
        dsm_ref[...] = dsm

    rev = lambda i: (nc - 1 - i, 0)
    rev4 = lambda i: (nc - 1 - i, 0, 0, 0)
    return pl.pallas_call(
        body, name="gdn_bwd", grid=(nc,),
        in_specs=[pl.BlockSpec((c, 3 * fw), rev), pl.BlockSpec((c, LANES), rev), pl.BlockSpec((c, fw), rev),
                  pl.BlockSpec((1, nh, HEAD_DIM, HEAD_DIM), rev4), pl.BlockSpec((1, nh, c, c), rev4)],
        out_specs=[pl.BlockSpec((c, 3 * fw), rev), pl.BlockSpec((c, LANES), rev)],
        out_shape=[jax.ShapeDtypeStruct((s, 3 * fw), f32), jax.ShapeDtypeStruct((s, LANES), f32)],
        scratch_shapes=[pltpu.VMEM((nh, HEAD_DIM, HEAD_DIM), f32)],
        compiler_params=_cp("arbitrary"))(qkvn, sm, do, states, tinvs)


MM_TILES = (1024, 512, 2048)
MM_TILES_TN = (512, 1024, 1024)
MM_TILES_LONG_K = (1024, 512, 2560)


def _hosted(res, comm):
    return res if comm else (res, None)


def _ffn_fwd(tag, x, g, mod3, w, comm_up=None, comm_down=None):
    wg_t, wu_t, wd = w
    sh, sc, gt = mod3
    h = _ada_in(tag + "_ada", x, g, sh, sc)
    (gate, up, act), got_up = _hosted(_mm(tag + "_up", [[(h, wg_t)], [(h, wu_t)]], "nt", MM_TILES, _ep_swiglu,
                                          (bf16, bf16, bf16), comm=comm_up), comm_up)
    (xn, y), got_down = _hosted(_mm(tag + "_down", [[(act, wd)]], "nn", MM_TILES, _ep_residual, (f32, bf16),
                                    extras=((x, "mn"), (MACARON_W * gt, "n")), comm=comm_down), comm_down)
    return xn, dict(x=x, h=h, gate=gate, up=up, y=y), got_up, got_down


def _ffn_bwd(tag, dxn, res, g, mod3, w, comm_dact=None):
    wg_t, wu_t, wd = w
    sh, sc, gt = mod3
    dy, dgs = _gate_bwd(tag + "_gate_bwd", dxn, res["y"], MACARON_W * gt)
    (act, dgate, dup), got = _hosted(_mm(tag + "_dact", [[(dy, wd)]], "nt", MM_TILES, _ep_swiglu_bwd, (bf16, bf16, bf16),
                                         extras=((res["gate"], "mn"), (res["up"], "mn")), comm=comm_dact), comm_dact)
    (dwd,) = _mm(tag + "_dwd", [[(act, dy)]], "tn", MM_TILES_TN, _ep_plain, (bf16,))
    (dwg_t,) = _mm(tag + "_dwg", [[(dgate, res["h"])]], "tn", MM_TILES_TN, _ep_plain, (bf16,))
    (dwu_t,) = _mm(tag + "_dwu", [[(dup, res["h"])]], "tn", MM_TILES_TN, _ep_plain, (bf16,))
    (dh,) = _mm(tag + "_dh", [[(dgate, wg_t), (dup, wu_t)]], "nn", MM_TILES, _ep_plain, (bf16,))
    dx, dsh, a = _ada_bwd(tag + "_ada_bwd", res["x"], g, sc, dh, dxn)
    return dx, (dwg_t, dwu_t, dwd), (dsh, a * g, MACARON_W * dgs), a * (1.0 + sc), got


def _local_step(x, target, mods, norm_g, final_norm, ffn1_w, later_w, prm, fox_wn, gdn_wn, conv_w, nh, hooks=None):
    s, d = x.shape
    fw = nh * HEAD_DIM
    tq = _tile(s, 256)
    g_rows = [norm_g[i:i + 1] for i in range(3)]
    m1, m2, m3 = mods[0:3], mods[3:6], mods[6:9]

    x1, r1, got_up, got_down = _ffn_fwd("ffn1", x, g_rows[0], m1, ffn1_w, hooks and hooks.gather_mix_spec(),
                                        hooks and hooks.gather_ffn2_spec())
    w_cat_t, w_out, ffn2_w = hooks.gathered(got_up, got_down) if hooks else later_w
    h2 = _ada_in("mix_ada", x1, g_rows[1], m2[0], m2[1])
    w_fox, w_gdn, w_z, w_s = w_cat_t[:3 * fw], w_cat_t[3 * fw:6 * fw], w_cat_t[6 * fw:7 * fw], w_cat_t[7 * fw:]
    colscale = jnp.concatenate([jnp.full((1, fw), FOX_Q_SCALE, f32), jnp.ones((1, 2 * fw), f32)], axis=1)
    (qkv_f,) = _mm("proj_fox", [[(h2, w_fox)]], "nt", MM_TILES, _ep_colscale, (bf16,), extras=((colscale, "n"),))
    (qkv_g,) = _mm("proj_gdn", [[(h2, w_gdn)]], "nt", MM_TILES, _ep_plain, (f32,))
    (z,) = _mm("proj_z", [[(h2, w_z)]], "nt", MM_TILES, _ep_plain, (f32,))
    (ps,) = _mm("proj_s", [[(h2, w_s)]], "nt", MM_TILES, _ep_plain, (f32,))
    sm, cum = _small_fwd(ps, prm, nh)
    qa, ka, va = _fox_aug(qkv_f, cum, nh)
    o_f, on_f, lse2 = _fox_fwd(qa, ka, qkv_f, fox_wn, nh, tq)
    qkvn = _gdn_pre(qkv_g, conv_w, nh)
    on_g, o_g, states, tinvs = _gdn_fwd(qkvn, sm, z, gdn_wn, nh)
    w_top, w_bot = w_out[:fw], w_out[fw:]
    x2, mix = _mm("mix_out", [[(on_f, w_top), (on_g, w_bot)]], "nn", MM_TILES, _ep_residual, (f32, bf16),
                  extras=((x1, "mn"), (m2[2], "n")))
    x3, r3, _, _ = _ffn_fwd("ffn2", x2, g_rows[2], m3, ffn2_w)
    loss, dx3, dfinal = _final_loss(x3, final_norm, target)

    dx2, dffn2, dmod3, dg3, _ = _ffn_bwd("ffn2", dx3, r3, g_rows[2], m3, ffn2_w)
    rs_ffn2 = hooks and hooks.rs_ffn2_spec(dffn2)
    dmix, dgt2 = _gate_bwd("mix_gate_bwd", dx2, mix, m2[2])
    (don_f,) = _mm("mix_dof", [[(dmix, w_top)]], "nt", MM_TILES, _ep_plain, (f32,))
    (don_g,) = _mm("mix_dog", [[(dmix, w_bot)]], "nt", MM_TILES, _ep_plain, (f32,))
    (dw_top,) = _mm("mix_dwtop", [[(on_f, dmix)]], "tn", MM_TILES_TN, _ep_plain, (bf16,))
    (dw_bot,) = _mm("mix_dwbot", [[(on_g, dmix)]], "tn", MM_TILES_TN, _ep_plain, (bf16,))
    qb, doa, dfox_wn = _fox_post_bwd(don_f, o_f, lse2, cum, qkv_f, fox_wn, nh)
    dqx, dkx, dv_f = _fox_bwd(qb, doa, ka, va, nh, tq)
    dq_f, dk_f, dcum = _fox_unpack(dqx, dkx, nh)
    do_g, dz, dgdn_wn = _gdn_post_bwd(don_g, o_g, z, gdn_wn, nh)
    dqkvn, dsm = _gdn_bwd(qkvn, sm, do_g, states, tinvs, nh)
    dpre = _gdn_pre_bwd(qkv_g, conv_w, dqkvn, nh)
    dqkv_g, dconv = _conv_bwd(qkv_g, conv_w, dpre, nh)
    dps, pg = _small_bwd(ps, prm, dsm, dcum, nh)
    dproj = jnp.concatenate([dq_f, dk_f, dv_f, dqkv_g, dz, dps], axis=1)
    ((dw_cat_t,), got_ffn2) = _hosted(_mm("proj_dw", [[(dproj, h2)]], "tn", MM_TILES_TN, _ep_plain, (bf16,), comm=rs_ffn2), rs_ffn2)
    dw_out = jnp.concatenate([dw_top, dw_bot], axis=0)
    rs_mix = hooks and hooks.rs_mix_spec(dw_cat_t, dw_out)
    (dh2,) = _mm("proj_dh", [[(dproj, w_cat_t)]], "nn", MM_TILES_LONG_K, _ep_plain, (bf16,))
    dx1, dsh2, a2 = _ada_bwd("mix_ada_bwd", x1, g_rows[1], m2[1], dh2, dx2)
    dmod2 = (dsh2, a2 * g_rows[1], dgt2)
    dg2 = a2 * (1.0 + m2[1])
    dx0, dffn1, dmod1, dg1, got_mix = _ffn_bwd("ffn1", dx1, r1, g_rows[0], m1, ffn1_w, rs_mix)

    big = dict(ffn=(dffn1, dffn2), w_cat_t=dw_cat_t, w_out=dw_out, got_ffn2=got_ffn2, got_mix=got_mix)
    small = dict(loss=loss, norm_g=jnp.concatenate([dg1, dg2, dg3], axis=0), final_norm=dfinal, fox_wn=dfox_wn,
                 gdn_wn=dgdn_wn, pg=pg, conv=dconv, mod=jnp.concatenate(list(dmod1) + list(dmod2) + list(dmod3), axis=1))
    return dx0, big, small


def _w_in_row_groups(nh):
    fw = nh * HEAD_DIM
    sizes = [3 * fw, nh, 3 * fw, nh, nh, fw]
    offs = [0]
    for sz in sizes:
        offs.append(offs[-1] + sz)
    return [(offs[i], offs[i + 1]) for i in range(len(sizes))]


def _build_w_cat_t(w_in_t, nh):
    gq, gf, gg, ga, gb, gz = _w_in_row_groups(nh)
    d = w_in_t.shape[1]
    rows = lambda r: w_in_t[r[0]:r[1]]
    pad = jnp.zeros((LANES - 3 * nh, d), w_in_t.dtype)
    return jnp.concatenate([rows(gq), rows(gg), rows(gz), rows(gf), rows(ga), rows(gb), pad], axis=0)


def _split_dw_cat_t(dw_cat_t, nh):
    fw = nh * HEAD_DIM
    o = 7 * fw
    return jnp.concatenate([dw_cat_t[:3 * fw], dw_cat_t[o:o + nh], dw_cat_t[3 * fw:6 * fw], dw_cat_t[o + nh:o + 2 * nh],
                            dw_cat_t[o + 2 * nh:o + 3 * nh], dw_cat_t[6 * fw:7 * fw]], axis=0)


def _head_params(fox_f_bias, gdn_dt_bias, gdn_a_log, nh):
    z = jnp.zeros((8, LANES), f32)
    z = z.at[0, 0:nh].set(fox_f_bias.reshape(nh))
    z = z.at[1, nh:2 * nh].set(gdn_dt_bias.reshape(nh))
    z = z.at[2, nh:2 * nh].set(gdn_a_log.reshape(nh))
    return z


ANY = pl.BlockSpec(memory_space=pl.ANY)
IN_VMEM = pl.BlockSpec(memory_space=pltpu.VMEM)
N_OTHER_CHIPS = 3


def _place():
    x, y, c = lax.axis_index("x"), lax.axis_index("y"), lax.axis_index("c")
    chips = [(1 - x, y), (x, 1 - y), (1 - x, 1 - y)]
    return x, y, c, chips


def _allgather8(name, v):
    r, n = v.shape

    def body(v_ref, out_ref, send_sems, recv_sems, local_sem):
        x, y, c, _ = _place()
        me = 4 * x + 2 * y + c
        mine = pltpu.make_async_copy(v_ref, out_ref.at[me], local_sem)
        mine.start()
        copies = []
        for k in range(1, 8):
            fx, fy, fc = (k >> 2) & 1, (k >> 1) & 1, k & 1
            peer = (x + fx - 2 * x * fx, y + fy - 2 * y * fy, c + fc - 2 * c * fc)
            cp = pltpu.make_async_remote_copy(src_ref=v_ref, dst_ref=out_ref.at[me], send_sem=send_sems.at[k - 1],
                                              recv_sem=recv_sems.at[k - 1], device_id=peer, device_id_type=MESH)
            cp.start()
            copies.append(cp)
        for cp in copies:
            cp.wait()
        mine.wait()

    return pl.pallas_call(
        body, name=name, in_specs=[IN_VMEM], out_specs=IN_VMEM, out_shape=jax.ShapeDtypeStruct((8, r, n), v.dtype),
        scratch_shapes=[pltpu.SemaphoreType.DMA((7,)), pltpu.SemaphoreType.DMA((7,)), pltpu.SemaphoreType.DMA],
        compiler_params=pltpu.CompilerParams(vmem_limit_bytes=VMEM_LIMIT_V7X))(v)


class _CommSpec:
    def __init__(self, ins, out_shapes, sem_counts, run):
        self.ins, self.out_shapes, self.sem_counts, self.run = list(ins), list(out_shapes), sem_counts, run

    def sems(self):
        return [pltpu.SemaphoreType.DMA((n,)) for n in self.sem_counts]


def _run_comm(name, spec):
    ni, no = len(spec.ins), len(spec.out_shapes)

    def body(*refs):
        ins, outs, sems = refs[:ni], refs[ni:ni + no], refs[ni + no:]
        spec.run("start", ins, outs, sems)
        spec.run("finish", ins, outs, sems)

    return pl.pallas_call(body, name=name, in_specs=[ANY] * ni, out_specs=[ANY] * no, out_shape=spec.out_shapes,
                          scratch_shapes=spec.sems())(*spec.ins)


def _gather_spec(halves):
    nw = len(halves)

    def run(phase, ins, outs, sems):
        ici_send, ici_recv, d2d_send, d2d_recv = sems
        x, y, c, chips = _place()
        s = 2 * x + y
        sib = (x, y, 1 - c)

        def over_ici(w, j, dst):
            cx, cy = chips[j]
            return pltpu.make_async_remote_copy(src_ref=ins[w].at[c], dst_ref=dst, send_sem=ici_send.at[w * 3 + j],
                                                recv_sem=ici_recv.at[w * 3 + j], device_id=(cx, cy, c), device_id_type=MESH)

        def to_sibling(w, j, blk):
            return pltpu.make_async_remote_copy(src_ref=blk, dst_ref=blk, send_sem=d2d_send.at[w * 3 + j],
                                                recv_sem=d2d_recv.at[w * 3 + j], device_id=sib, device_id_type=MESH)

        pairs = [(w, j) for w in range(nw) for j in range(N_OTHER_CHIPS)]
        chip_of = lambda j: 2 * chips[j][0] + chips[j][1]
        if phase == "start":
            for w, j in pairs:
                over_ici(w, j, outs[w].at[c, s]).start()
            return
        for w, j in pairs:
            landed = outs[w].at[c, chip_of(j)]
            over_ici(w, j, landed).wait_recv()
            to_sibling(w, j, landed).start()
        for w, j in pairs:
            to_sibling(w, j, outs[w].at[1 - c, chip_of(j)]).wait_recv()
        for w, j in pairs:
            over_ici(w, j, outs[w].at[c, s]).wait_send()
            to_sibling(w, j, outs[w].at[c, chip_of(j)]).wait_send()

    n3 = nw * N_OTHER_CHIPS
    return _CommSpec(halves, [jax.ShapeDtypeStruct((2, 4) + h.shape[1:], h.dtype) for h in halves], [n3] * 4, run)


def _to_chips_spec(partials):
    nw = len(partials)

    def run(phase, ins, outs, sems):
        send_sems, recv_sems = sems
        x, y, c, chips = _place()
        for w in range(nw):
            for j, (cx, cy) in enumerate(chips):
                cp = pltpu.make_async_remote_copy(src_ref=ins[w].at[2 * cx + cy], dst_ref=outs[w].at[j],
                                                  send_sem=send_sems.at[w * 3 + j], recv_sem=recv_sems.at[w * 3 + j],
                                                  device_id=(cx, cy, c), device_id_type=MESH)
                if phase == "start":
                    cp.start()
                else:
                    cp.wait()

    n3 = nw * N_OTHER_CHIPS
    return _CommSpec(partials, [jax.ShapeDtypeStruct((3,) + a.shape[1:], a.dtype) for a in partials], [n3, n3], run)


def _send_to_sibling(name, srcs, other_half):
    nw = len(srcs)

    def body(*refs):
        ins, outs = refs[:nw], refs[nw:2 * nw]
        send_sems, recv_sems = refs[2 * nw:]
        x, y, c, _ = _place()
        cps = []
        for w in range(nw):
            cp = pltpu.make_async_remote_copy(src_ref=ins[w].at[1 - c] if other_half else ins[w], dst_ref=outs[w],
                                              send_sem=send_sems.at[w], recv_sem=recv_sems.at[w],
                                              device_id=(x, y, 1 - c), device_id_type=MESH)
            cp.start()
            cps.append(cp)
        for cp in cps:
            cp.wait()

    return pl.pallas_call(
        body, name=name, in_specs=[ANY] * nw, out_specs=[ANY] * nw,
        out_shape=[jax.ShapeDtypeStruct(a.shape[1:] if other_half else a.shape, a.dtype) for a in srcs],
        scratch_shapes=[pltpu.SemaphoreType.DMA((nw,)), pltpu.SemaphoreType.DMA((nw,))],
    )(*srcs)


def _add_pair(name, g, recv, c):
    _, nchip, r, d = g.shape
    tr = _tile(r, 512, 16)

    def body(c_ref, g_ref, r_ref, o_ref):
        o_ref[...] = (g_ref[...].astype(f32) + r_ref[...].astype(f32)).astype(o_ref.dtype)

    gs = pltpu.PrefetchScalarGridSpec(
        num_scalar_prefetch=1, grid=(nchip, r // tr),
        in_specs=[pl.BlockSpec((None, None, tr, d), lambda t, i, cr: (cr[0], t, i, 0)),
                  pl.BlockSpec((None, tr, d), lambda t, i, cr: (t, i, 0))],
        out_specs=pl.BlockSpec((None, tr, d), lambda t, i, cr: (t, i, 0)))
    return pl.pallas_call(body, name=name, grid_spec=gs, out_shape=jax.ShapeDtypeStruct((nchip, r, d), bf16),
                          compiler_params=_cp("parallel", "parallel"))(c.reshape(1).astype(jnp.int32), g, recv)


def _add_chips(name, p, recv, s_chip):
    _, r, d = p.shape
    tr = _tile(r, 512, 16)

    def body(s_ref, p_ref, r_ref, o_ref):
        o_ref[...] = ((p_ref[...].astype(f32) + r_ref[0].astype(f32)) + r_ref[1].astype(f32)) + r_ref[2].astype(f32)

    gs = pltpu.PrefetchScalarGridSpec(
        num_scalar_prefetch=1, grid=(r // tr,),
        in_specs=[pl.BlockSpec((None, tr, d), lambda i, sr: (sr[0], i, 0)),
                  pl.BlockSpec((3, tr, d), lambda i, sr: (0, i, 0))],
        out_specs=pl.BlockSpec((tr, d), lambda i, sr: (i, 0)))
    return pl.pallas_call(body, name=name, grid_spec=gs, out_shape=jax.ShapeDtypeStruct((r, d), f32),
                          compiler_params=_cp("parallel"))(s_chip.reshape(1).astype(jnp.int32), p, recv)


def _rs_pair_sums(tag, grads, c):
    from_sib = _send_to_sibling("rs_to_sibling_" + tag, grads, True)
    return [_add_pair("rs_add_pair_%s%d" % (tag, n), g, r, c) for n, (g, r) in enumerate(zip(grads, from_sib))]


def _rs_chip_sums(tag, partial, from_chips, s_chip):
    return [_add_chips("rs_add_chips_%s%d" % (tag, n), p, r, s_chip) for n, (p, r) in enumerate(zip(partial, from_chips))]


def _rs_both_halves(mine, c):
    theirs = _send_to_sibling("rs_exchange_halves", mine, False)
    return [jnp.where(c == 0, jnp.stack([a, b]), jnp.stack([b, a])) for a, b in zip(mine, theirs)]


def _sum_devices(v):
    n = v.shape[2]

    def body(v_ref, o_ref):
        t = v_ref[0]
        for k in range(1, 8):
            t = t + v_ref[k]
        o_ref[...] = t

    return pl.pallas_call(body, name="sum_devices", out_shape=jax.ShapeDtypeStruct((1, n), f32))(v)


def _silu_rows(v):
    def body(v_ref, o_ref):
        t = v_ref[...]
        o_ref[...] = t * _sigmoid(t)

    return pl.pallas_call(body, name="silu_cond", out_shape=jax.ShapeDtypeStruct(v.shape, f32))(v)


ADAMW_BLOCK_ELEMS = 600 * 1024


def _adamw(name, w, g, m, v):
    r, cdim = w.shape
    tr = _tile(r, max(8, min(256, (ADAMW_BLOCK_ELEMS // cdim) // 8 * 8)), 8)
    c1 = 1.0 - ADAM_B1 ** ADAM_STEP
    c2 = 1.0 - ADAM_B2 ** ADAM_STEP

    def body(w_ref, g_ref, m_ref, v_ref, d_ref, mo_ref, vo_ref):
        gv = g_ref[...]
        mn = ADAM_B1 * m_ref[...] + (1.0 - ADAM_B1) * gv
        vn = ADAM_B2 * v_ref[...] + (1.0 - ADAM_B2) * (gv * gv)
        d_ref[...] = -ADAM_LR * ((mn / c1) / (jnp.sqrt(vn / c2) + ADAM_EPS) + ADAM_WD * w_ref[...])
        mo_ref[...] = mn
        vo_ref[...] = vn

    blk = pl.BlockSpec((tr, cdim), lambda i: (i, 0))
    return pl.pallas_call(body, name=name, grid=(r // tr,), in_specs=[blk] * 4, out_specs=[blk] * 3,
                          out_shape=[jax.ShapeDtypeStruct((r, cdim), f32)] * 3, compiler_params=_cp("parallel"))(w, g, m, v)


def _ep_bias(accs, ex):
    return (accs[0] + ex[0],)


def kernel(x, c, ada_w, ada_b, norm_g, ffn_w_gate, ffn_w_up, ffn_w_down, w_in, w_out, fox_f_bias, fox_out_norm, gdn_conv, gdn_A_log, gdn_dt_bias, gdn_out_norm, final_norm, loss_target, m_ada_w, m_ada_b, m_norm_g, m_ffn_w_gate, m_ffn_w_up, m_ffn_w_down, m_w_in, m_w_out, m_fox_f_bias, m_fox_out_norm, m_gdn_conv, m_gdn_A_log, m_gdn_dt_bias, m_gdn_out_norm, m_final_norm, v_ada_w, v_ada_b, v_norm_g, v_ffn_w_gate, v_ffn_w_up, v_ffn_w_down, v_w_in, v_w_out, v_fox_f_bias, v_fox_out_norm, v_gdn_conv, v_gdn_A_log, v_gdn_dt_bias, v_gdn_out_norm, v_final_norm):
    ix, iy, ic = lax.axis_index("x"), lax.axis_index("y"), lax.axis_index("c")
    s_chip = 2 * ix + iy
    me = 4 * ix + 2 * iy + ic
    _, s, d = x.shape
    nh = d // (2 * HEAD_DIM)
    fw = nh * HEAD_DIM
    ncol = ada_w.shape[2]
    dg_sh = norm_g.shape[2]
    cv_sh = gdn_conv.shape[2]
    ff_sh = ffn_w_gate.shape[3]
    in_sh = w_in.shape[2]
    in_pad = -(-in_sh // 32) * 32
    out_sh = w_out.shape[1]
    per_chip = lambda a, t: a[2 * t]

    pack0 = jnp.concatenate([_silu_rows(c), norm_g[0].reshape(1, 3 * dg_sh), gdn_conv[0].reshape(1, CONV_W * cv_sh)], axis=1)
    got0 = _allgather8("gather_cond", pack0)
    cond_all = got0[:, 0, :d]
    norm_g_full = jnp.concatenate([per_chip(got0, t)[0, d:d + 3 * dg_sh].reshape(3, dg_sh) for t in range(4)], axis=1)
    conv_full = jnp.concatenate([per_chip(got0, t)[0, d + 3 * dg_sh:].reshape(CONV_W, cv_sh) for t in range(4)], axis=1)

    ada_b_sh = lax.dynamic_slice_in_dim(ada_b, s_chip * ncol, ncol, axis=1)
    (mod_sh,) = _mm("ada_mod", [[(cond_all, ada_w[0])]], "nn", (8, 512, 2048), _ep_bias, (f32,), extras=((ada_b_sh, "n"),))
    mod_all = _allgather8("gather_mod", mod_sh)
    mod = jnp.concatenate([lax.dynamic_index_in_dim(per_chip(mod_all, t), me, axis=0, keepdims=True) for t in range(4)], axis=1)
    mods = [mod[:, i * d:(i + 1) * d] for i in range(9)]

    halved = lambda a: a.reshape(2, a.shape[0] // 2, d)
    ffn_halves = [[halved(ffn_w_gate[0, j].T.astype(bf16)), halved(ffn_w_up[0, j].T.astype(bf16)),
                   halved(ffn_w_down[0, j].astype(bf16))] for j in range(2)]
    mix_halves = [halved(jnp.pad(w_in[0].T.astype(bf16), ((0, in_pad - in_sh), (0, 0)))), halved(w_out[0].astype(bf16))]
    with_own = lambda got, hs: [lax.dynamic_update_slice(g, h[:, None], (0, s_chip, 0, 0)) for g, h in zip(got, hs)]
    ffn_full = lambda got, hs: tuple(g.reshape(4 * ff_sh, d) for g in with_own(got, hs))
    ffn_blocks = lambda grads: [g.reshape(2, 4, ff_sh // 2, d) for g in grads]
    ffn1_w = ffn_full(_run_comm("gather_ffn1", _gather_spec(ffn_halves[0])), ffn_halves[0])
    prm = _head_params(fox_f_bias, gdn_dt_bias, gdn_A_log, nh)

    class Hooks:
        def gather_mix_spec(self):
            return _gather_spec(mix_halves)

        def gather_ffn2_spec(self):
            return _gather_spec(ffn_halves[1])

        def gathered(self, got_mix, got_ffn2):
            g_win, g_wo = with_own(got_mix, mix_halves)
            w_in_t = jnp.swapaxes(g_win, 0, 1).reshape(4, in_pad, d)[:, :in_sh].reshape(4 * in_sh, d)
            return _build_w_cat_t(w_in_t, nh), jnp.swapaxes(g_wo, 0, 1).reshape(4 * out_sh, d), ffn_full(got_ffn2, ffn_halves[1])

        def rs_ffn2_spec(self, dffn2):
            self.ffn2_pairs = _rs_pair_sums("ffn2", ffn_blocks(dffn2), ic)
            return _to_chips_spec(self.ffn2_pairs)

        def rs_mix_spec(self, dw_cat_t, dw_out):
            dw_in_t = jnp.pad(_split_dw_cat_t(dw_cat_t, nh).reshape(4, in_sh, d), ((0, 0), (0, in_pad - in_sh), (0, 0)))
            grads = [jnp.swapaxes(dw_in_t.reshape(4, 2, in_pad // 2, d), 0, 1),
                     jnp.swapaxes(dw_out.reshape(4, 2, out_sh // 2, d), 0, 1)]
            self.mix_pairs = _rs_pair_sums("mix", grads, ic)
            return _to_chips_spec(self.mix_pairs)

    hooks = Hooks()

    dx0, big, small = _local_step(x[0], loss_target[0], mods, norm_g_full, final_norm.reshape(1, d), ffn1_w, None, prm,
                                  fox_out_norm, gdn_out_norm, conv_full, nh, hooks)

    pack1 = jnp.concatenate([small["loss"], small["norm_g"].reshape(1, 3 * d), small["final_norm"], small["fox_wn"],
                             small["gdn_wn"], small["pg"][0:1], small["pg"][1:2], small["conv"].reshape(1, CONV_W * 3 * fw),
                             small["mod"]], axis=1)
    got1 = _allgather8("gather_small_grads", pack1)
    tot = _sum_devices(got1)
    o = [0]

    def take(n):
        o[0] += n
        return tot[:, o[0] - n:o[0]]

    loss = take(LANES)[0, 0]
    g_norm_g = lax.dynamic_slice_in_dim(take(3 * d).reshape(3, d), s_chip * dg_sh, dg_sh, axis=1)[None]
    g_final = take(d).reshape(d)
    g_fox_wn = take(HEAD_DIM)
    g_gdn_wn = take(HEAD_DIM)
    pg0, pg1 = take(LANES), take(LANES)
    g_fbias, g_dtb, g_alog = pg0[:, 0:nh], pg0[:, nh:2 * nh], pg1[:, nh:2 * nh]
    g_conv = lax.dynamic_slice_in_dim(take(CONV_W * 3 * fw).reshape(CONV_W, 3 * fw), s_chip * cv_sh, cv_sh, axis=1)[None]
    g_ada_b = take(9 * d)
    dmod_all = got1[:, 0, o[0] - 9 * d:o[0]]
    dmod_sh = lax.dynamic_slice_in_dim(dmod_all, s_chip * ncol, ncol, axis=1)
    (g_ada_w,) = _mm("ada_dw", [[(cond_all, dmod_sh)]], "tn", (2048, 512, 8), _ep_plain, (f32,))

    ffn1_pairs = _rs_pair_sums("ffn1", ffn_blocks(big["ffn"][0]), ic)
    ffn1_mine = _rs_chip_sums("ffn1", ffn1_pairs, _run_comm("rs_to_chips_ffn1", _to_chips_spec(ffn1_pairs)), s_chip)
    ffn2_mine = _rs_chip_sums("ffn2", hooks.ffn2_pairs, big["got_ffn2"], s_chip)
    mix_mine = _rs_chip_sums("mix", hooks.mix_pairs, big["got_mix"], s_chip)
    r1g, r1u, r1d, r2g, r2u, r2d, r_win, r_wo = _rs_both_halves(ffn1_mine + ffn2_mine + mix_mine, ic)
    rows = lambda r: r.reshape(-1, d)
    g_ffn_gate = jnp.stack([rows(r1g).T, rows(r2g).T])[None]
    g_ffn_up = jnp.stack([rows(r1u).T, rows(r2u).T])[None]
    g_ffn_down = jnp.stack([rows(r1d), rows(r2d)])[None]
    g_w_in = rows(r_win)[:in_sh].T[None]
    g_w_out = rows(r_wo)[None]

    def upd(name, w, g, m, v):
        shp = w.shape
        two = lambda a: a.reshape(-1, shp[-1])
        return tuple(t.reshape(shp) for t in _adamw(name, two(w), two(g), two(m), two(v)))

    big_upd = [upd("adamw_ada_w", ada_w, g_ada_w[None], m_ada_w, v_ada_w),
               upd("adamw_ffn_gate", ffn_w_gate, g_ffn_gate, m_ffn_w_gate, v_ffn_w_gate),
               upd("adamw_ffn_up", ffn_w_up, g_ffn_up, m_ffn_w_up, v_ffn_w_up),
               upd("adamw_ffn_down", ffn_w_down, g_ffn_down, m_ffn_w_down, v_ffn_w_down),
               upd("adamw_w_in", w_in, g_w_in, m_w_in, v_w_in),
               upd("adamw_w_out", w_out, g_w_out, m_w_out, v_w_out)]
    small_w = [ada_b, norm_g, fox_f_bias, fox_out_norm, gdn_conv, gdn_A_log, gdn_dt_bias, gdn_out_norm, final_norm]
    small_g = [g_ada_b, g_norm_g, g_fbias, g_fox_wn, g_conv, g_alog, g_dtb, g_gdn_wn, g_final]
    small_m = [m_ada_b, m_norm_g, m_fox_f_bias, m_fox_out_norm, m_gdn_conv, m_gdn_A_log, m_gdn_dt_bias, m_gdn_out_norm, m_final_norm]
    small_v = [v_ada_b, v_norm_g, v_fox_f_bias, v_fox_out_norm, v_gdn_conv, v_gdn_A_log, v_gdn_dt_bias, v_gdn_out_norm, v_final_norm]
    sizes = [a.size for a in small_w]
    npad = -sum(sizes) % LANES
    flat = lambda arrs, fill: jnp.concatenate([a.reshape(1, -1) for a in arrs] + [jnp.full((1, npad), fill, f32)], axis=1)
    sd, sm_, sv = _adamw("adamw_small", flat(small_w, 0.0), flat(small_g, 0.0), flat(small_m, 0.0), flat(small_v, 1.0))

    def unflat(t):
        out, off = [], 0
        for a, n in zip(small_w, sizes):
            out.append(t[0, off:off + n].reshape(a.shape))
            off += n
        return out

    small_g = [g.reshape(a.shape) for g, a in zip(small_g, small_w)]
    s_d, s_m, s_v = unflat(sd), unflat(sm_), unflat(sv)
    def order(bigs, smalls):
        return [bigs[0], smalls[0], smalls[1], bigs[1], bigs[2], bigs[3], bigs[4], bigs[5]] + list(smalls[2:])

    grads_out = order([g_ada_w[None], g_ffn_gate, g_ffn_up, g_ffn_down, g_w_in, g_w_out], small_g)
    deltas = order([u[0] for u in big_upd], s_d)
    new_m = order([u[1] for u in big_upd], s_m)
    new_v = order([u[2] for u in big_upd], s_v)
    return (loss, dx0[None], *grads_out, *deltas, *new_m, *new_v)
```

```python
import functools
import math

import jax
import jax.numpy as jnp
from jax import lax
from jax.experimental import pallas as pl
from jax.experimental.pallas import tpu as pltpu

f32 = jnp.float32
bf16 = jnp.bfloat16
HI = lax.Precision.HIGHEST
MESH = pl.DeviceIdType.MESH

EPS = 1e-6
HEAD_DIM = 128
LANES = 128
GDN_CHUNK = 64
CONV_W = 4
MACARON_W = 0.5
ADAM_LR, ADAM_B1, ADAM_B2, ADAM_EPS, ADAM_WD, ADAM_STEP = 0.001, 0.9, 0.999, 1e-08, 0.01, 10
VMEM_LIMIT_V7X = 56 * 1024 * 1024
NEG = -1e30

NN = (((1,), (0,)), ((), ()))
NT = (((1,), (1,)), ((), ()))
TN = (((0,), (0,)), ((), ()))


def _cp(*sem):
    return pltpu.CompilerParams(dimension_semantics=sem, vmem_limit_bytes=VMEM_LIMIT_V7X)


def _dotb(a, b, dn=NN):
    return lax.dot_general(a.astype(bf16), b.astype(bf16), dn, preferred_element_type=f32)


def _doth(a, b, dn=NN):
    return lax.dot_general(a.astype(f32), b.astype(f32), dn, precision=HI, preferred_element_type=f32)


def _sigmoid(x):
    return 1.0 / (1.0 + jnp.exp(-x))


def _softplus(x):
    return jnp.maximum(x, 0.0) + jnp.log(1.0 + jnp.exp(-jnp.abs(x)))


def _lane_col(blk, lane_idx):
    lane = lax.broadcasted_iota(jnp.int32, blk.shape, 1)
    return jnp.sum(jnp.where(lane == lane_idx, blk, 0.0), axis=1, keepdims=True)


def _tile(n, pref, mult=LANES):
    if n <= pref:
        return n
    t = (pref // mult) * mult
    while t >= mult:
        if n % t == 0:
            return t
        t -= mult
    return n


def _mm(name, groups, mode, tiles, epilogue, out_dtypes, extras=(), comm=None):
    a0, b0 = groups[0][0]
    if mode == "nn":
        (m, k), n = a0.shape, b0.shape[1]
    elif mode == "nt":
        (m, k), n = a0.shape, b0.shape[0]
    else:
        (k, m), n = a0.shape, b0.shape[1]
    tm, tn, tk = _tile(m, tiles[0]), _tile(n, tiles[1]), _tile(k, tiles[2])
    nk = k // tk
    assert m % tm == 0 and n % tn == 0 and k % tk == 0, (name, m, n, k, tm, tn, tk)
    if mode == "nn":
        a_spec = pl.BlockSpec((tm, tk), lambda i, j, kk: (i, kk))
        b_spec = pl.BlockSpec((tk, tn), lambda i, j, kk: (kk, j))
        dn = NN
    elif mode == "nt":
        a_spec = pl.BlockSpec((tm, tk), lambda i, j, kk: (i, kk))
        b_spec = pl.BlockSpec((tn, tk), lambda i, j, kk: (j, kk))
        dn = NT
    else:
        a_spec = pl.BlockSpec((tk, tm), lambda i, j, kk: (kk, i))
        b_spec = pl.BlockSpec((tk, tn), lambda i, j, kk: (kk, j))
        dn = TN
    npairs = sum(len(g) for g in groups)
    nacc, nex, nout = len(groups), len(extras), len(out_dtypes)
    in_specs, args = [], []
    for g in groups:
        for a, b in g:
            in_specs += [a_spec, b_spec]
            args += [a, b]
    for arr, kind in extras:
        if kind == "mn":
            in_specs.append(pl.BlockSpec((tm, tn), lambda i, j, kk: (i, j)))
        else:
            in_specs.append(pl.BlockSpec((1, tn), lambda i, j, kk: (0, j)))
        args.append(arr)
    nci = len(comm.ins) if comm else 0
    nco = len(comm.out_shapes) if comm else 0
    grid = (m // tm, n // tn, nk)

    def body(*refs):
        ins = refs[: 2 * npairs]
        ex = refs[2 * npairs: 2 * npairs + nex]
        c_ins = refs[2 * npairs + nex: 2 * npairs + nex + nci]
        o0 = 2 * npairs + nex + nci
        outs = refs[o0: o0 + nout]
        c_outs = refs[o0 + nout: o0 + nout + nco]
        accs = refs[o0 + nout + nco: o0 + nout + nco + nacc]
        c_sems = refs[o0 + nout + nco + nacc:]
        kk = pl.program_id(2)
        if comm:
            step = (pl.program_id(0) * grid[1] + pl.program_id(1)) * nk + kk

            @pl.when(step == 0)
            def _():
                comm.run("start", c_ins, c_outs, c_sems)

        @pl.when(kk == 0)
        def _():
            for acc in accs:
                acc[...] = jnp.zeros_like(acc)

        p = 0
        for gi, g in enumerate(groups):
            t = None
            for _ in g:
                d = _dotb(ins[2 * p][...], ins[2 * p + 1][...], dn)
                t = d if t is None else t + d
                p += 1
            accs[gi][...] += t

        @pl.when(kk == nk - 1)
        def _():
            res = epilogue([acc[...] for acc in accs], [e[...] for e in ex])
            for o, r in zip(outs, res):
                o[...] = r.astype(o.dtype)

        if comm:
            @pl.when(step == grid[0] * grid[1] * nk - 1)
            def _():
                comm.run("finish", c_ins, c_outs, c_sems)

    any_spec = pl.BlockSpec(memory_space=pl.ANY)
    res = pl.pallas_call(
        body, name=name, grid=grid,
        in_specs=in_specs + [any_spec] * nci,
        out_specs=[pl.BlockSpec((tm, tn), lambda i, j, kk: (i, j)) for _ in out_dtypes] + [any_spec] * nco,
        out_shape=[jax.ShapeDtypeStruct((m, n), dt) for dt in out_dtypes] + (list(comm.out_shapes) if comm else []),
        scratch_shapes=[pltpu.VMEM((tm, tn), f32) for _ in range(nacc)] + (comm.sems() if comm else []),
        compiler_params=_cp(*(("arbitrary",) * 3 if comm else ("parallel", "parallel", "arbitrary"))),
    )(*args, *(comm.ins if comm else []))
    return (res[:nout], res[nout:]) if comm else res


def _ep_plain(accs, ex):
    return (accs[0],)


def _ep_colscale(accs, ex):
    return (accs[0] * ex[0],)


def _ep_swiglu(accs, ex):
    gate, up = accs
    act = gate * _sigmoid(gate) * up
    return gate, up, act


def _ep_residual(accs, ex):
    x, gs = ex
    y = accs[0]
    return x + gs * y, y


def _ep_swiglu_bwd(accs, ex):
    gate, up = ex[0].astype(f32), ex[1].astype(f32)
    dact = accs[0]
    sg = _sigmoid(gate)
    silu = gate * sg
    act = silu * up
    dup = dact * silu
    dgate = dact * up * sg * (1.0 + gate * (1.0 - sg))
    return act, dgate, dup


def _row_tile(s):
    return _tile(s, 256, 8)


def _ada_in(name, x, g, shift, scale):
    s, d = x.shape
    tm = _row_tile(s)

    def body(x_ref, g_ref, sh_ref, sc_ref, h_ref):
        xv = x_ref[...]
        r = lax.rsqrt(jnp.mean(xv * xv, axis=-1, keepdims=True) + EPS)
        h_ref[...] = (xv * r * g_ref[...] * (1.0 + sc_ref[...]) + sh_ref[...]).astype(h_ref.dtype)

    row = pl.BlockSpec((1, d), lambda i: (0, 0))
    blk = pl.BlockSpec((tm, d), lambda i: (i, 0))
    return pl.pallas_call(body, name=name, grid=(s // tm,), in_specs=[blk, row, row, row], out_specs=blk,
                          out_shape=jax.ShapeDtypeStruct((s, d), bf16), compiler_params=_cp("parallel"))(x, g, shift, scale)


def _ada_bwd(name, x, g, scale, dh, dres):
    s, d = x.shape
    tm = _row_tile(s)

    def body(x_ref, g_ref, sc_ref, dh_ref, dres_ref, dx_ref, dsh_ref, a_ref):
        i = pl.program_id(0)

        @pl.when(i == 0)
        def _():
            dsh_ref[...] = jnp.zeros_like(dsh_ref)
            a_ref[...] = jnp.zeros_like(a_ref)

        xv = x_ref[...]
        dhv = dh_ref[...].astype(f32)
        r = lax.rsqrt(jnp.mean(xv * xv, axis=-1, keepdims=True) + EPS)
        n = xv * r
        dn = dhv * (g_ref[...] * (1.0 + sc_ref[...]))
        dx_ref[...] = dres_ref[...] + r * (dn - n * jnp.mean(dn * n, axis=-1, keepdims=True))
        dsh_ref[...] += jnp.sum(dhv, axis=0, keepdims=True)
        a_ref[...] += jnp.sum(dhv * n, axis=0, keepdims=True)

    row = pl.BlockSpec((1, d), lambda i: (0, 0))
    blk = pl.BlockSpec((tm, d), lambda i: (i, 0))
    return pl.pallas_call(
        body, name=name, grid=(s // tm,), in_specs=[blk, row, row, blk, blk], out_specs=[blk, row, row],
        out_shape=[jax.ShapeDtypeStruct((s, d), f32), jax.ShapeDtypeStruct((1, d), f32), jax.ShapeDtypeStruct((1, d), f32)],
        compiler_params=_cp("arbitrary"))(x, g, scale, dh, dres)


def _gate_bwd(name, dx, y, gs):
    s, d = dx.shape
    tm = _row_tile(s)

    def body(dx_ref, y_ref, gs_ref, dy_ref, dgs_ref):
        i = pl.program_id(0)

        @pl.when(i == 0)
        def _():
            dgs_ref[...] = jnp.zeros_like(dgs_ref)

        dxv = dx_ref[...]
        dy_ref[...] = (dxv * gs_ref[...]).astype(dy_ref.dtype)
        dgs_ref[...] += jnp.sum(dxv * y_ref[...].astype(f32), axis=0, keepdims=True)

    row = pl.BlockSpec((1, d), lambda i: (0, 0))
    blk = pl.BlockSpec((tm, d), lambda i: (i, 0))
    return pl.pallas_call(
        body, name=name, grid=(s // tm,), in_specs=[blk, blk, row], out_specs=[blk, row],
        out_shape=[jax.ShapeDtypeStruct((s, d), bf16), jax.ShapeDtypeStruct((1, d), f32)],
        compiler_params=_cp("arbitrary"))(dx, y, gs)


def _final_loss(x, fg, target):
    s, d = x.shape
    tm = _row_tile(s)

    def body(x_ref, g_ref, t_ref, loss_ref, dx_ref, dg_ref):
        i = pl.program_id(0)

        @pl.when(i == 0)
        def _():
            loss_ref[...] = jnp.zeros_like(loss_ref)
            dg_ref[...] = jnp.zeros_like(dg_ref)

        xv = x_ref[...]
        gv = g_ref[...]
        r = lax.rsqrt(jnp.mean(xv * xv, axis=-1, keepdims=True) + EPS)
        n = xv * r
        e = n * gv - t_ref[...]
        per_tok = jnp.mean(e * e, axis=-1, keepdims=True)
        loss_ref[...] += 0.5 * jnp.sum(per_tok, axis=0, keepdims=True) * jnp.ones((1, LANES), f32)
        dy = e * (1.0 / d)
        dg_ref[...] += jnp.sum(dy * n, axis=0, keepdims=True)
        dn = dy * gv
        dx_ref[...] = r * (dn - n * jnp.mean(dn * n, axis=-1, keepdims=True))

    row = pl.BlockSpec((1, d), lambda i: (0, 0))
    blk = pl.BlockSpec((tm, d), lambda i: (i, 0))
    return pl.pallas_call(
        body, name="final_loss", grid=(s // tm,), in_specs=[blk, row, blk],
        out_specs=[pl.BlockSpec((1, LANES), lambda i: (0, 0)), blk, row],
        out_shape=[jax.ShapeDtypeStruct((1, LANES), f32), jax.ShapeDtypeStruct((s, d), f32), jax.ShapeDtypeStruct((1, d), f32)],
        compiler_params=_cp("arbitrary"))(x, fg, target)


def _small_fwd(ps, prm, nh):
    s = ps.shape[0]
    tb = LANES

    def body(ps_ref, prm_ref, sm_ref, cum_ref, carry):
        i = pl.program_id(0)

        @pl.when(i == 0)
        def _():
            carry[...] = jnp.zeros_like(carry)

        x = ps_ref[...]
        lane = lax.broadcasted_iota(jnp.int32, x.shape, 1)
        fb, dtb, alog = prm_ref[0:1, :], prm_ref[1:2, :], prm_ref[2:3, :]
        logf = -_softplus(-(x + fb))
        glog = -jnp.exp(alog) * _softplus(x + dtb)
        beta = _sigmoid(x)
        sm = jnp.where(lane < nh, logf, jnp.where(lane < 2 * nh, glog, jnp.where(lane < 3 * nh, beta, 0.0)))
        sm_ref[...] = sm
        r = lax.broadcasted_iota(jnp.int32, (tb, tb), 0)
        c = lax.broadcasted_iota(jnp.int32, (tb, tb), 1)
        tril = (c <= r).astype(f32)
        cs = _doth(tril, sm) + carry[...]
        cum_ref[...] = cs
        carry[...] = cs[tb - 1:tb, :]

    blk = pl.BlockSpec((tb, LANES), lambda i: (i, 0))
    return pl.pallas_call(
        body, name="small_fwd", grid=(s // tb,),
        in_specs=[blk, pl.BlockSpec((8, LANES), lambda i: (0, 0))],
        out_specs=[blk, blk],
        out_shape=[jax.ShapeDtypeStruct((s, LANES), f32), jax.ShapeDtypeStruct((s, LANES), f32)],
        scratch_shapes=[pltpu.VMEM((1, LANES), f32)],
        compiler_params=_cp("arbitrary"))(ps, prm)


def _small_bwd(ps, prm, dsm, dcum, nh):
    s = ps.shape[0]
    tb = LANES
    nb = s // tb

    def body(ps_ref, prm_ref, dsm_ref, dct_ref, dps_ref, pg_ref, carry):
        i = pl.program_id(0)

        @pl.when(i == 0)
        def _():
            carry[...] = jnp.zeros_like(carry)
            pg_ref[...] = jnp.zeros_like(pg_ref)

        x = ps_ref[...]
        dsm = dsm_ref[...]
        lane = lax.broadcasted_iota(jnp.int32, x.shape, 1)
        fb, dtb, alog = prm_ref[0:1, :], prm_ref[1:2, :], prm_ref[2:3, :]
        r = lax.broadcasted_iota(jnp.int32, (tb, tb), 0)
        c = lax.broadcasted_iota(jnp.int32, (tb, tb), 1)
        triu = (c >= r).astype(f32)
        dlogf = _doth(triu, dct_ref[...]) + carry[...]
        carry[...] = dlogf[0:1, :]
        d_f = dlogf * _sigmoid(-(x + fb))
        nega = -jnp.exp(alog)
        xa = x + dtb
        glog = nega * _softplus(xa)
        d_a = dsm * nega * _sigmoid(xa)
        beta = _sigmoid(x)
        d_b = dsm * beta * (1.0 - beta)
        dps = jnp.where(lane < nh, d_f, jnp.where(lane < 2 * nh, d_a, jnp.where(lane < 3 * nh, d_b, 0.0)))
        dps_ref[...] = dps.astype(dps_ref.dtype)
        row0 = jnp.sum(dps, axis=0, keepdims=True)
        row1 = jnp.sum(jnp.where((lane >= nh) & (lane < 2 * nh), dsm * glog, 0.0), axis=0, keepdims=True)
        sub = lax.broadcasted_iota(jnp.int32, (8, LANES), 0)
        pg_ref[...] += jnp.where(sub == 0, row0, jnp.where(sub == 1, row1, 0.0))

    rev = pl.BlockSpec((tb, LANES), lambda i: (nb - 1 - i, 0))
    fix = pl.BlockSpec((8, LANES), lambda i: (0, 0))
    return pl.pallas_call(
        body, name="small_bwd", grid=(nb,),
        in_specs=[rev, fix, rev, rev],
        out_specs=[rev, fix],
        out_shape=[jax.ShapeDtypeStruct((s, LANES), bf16), jax.ShapeDtypeStruct((8, LANES), f32)],
        scratch_shapes=[pltpu.VMEM((1, LANES), f32)],
        compiler_params=_cp("arbitrary"))(ps, prm, dsm, dcum)


LOG2E = 1.4426950408889634
LN2 = 0.6931471805599453
AUG = 2 * HEAD_DIM
FOX_Q_SCALE = LOG2E / math.sqrt(HEAD_DIM)
FOX_KEY_GROUP = 4


def _split3(col):
    hi = col.astype(bf16).astype(f32)
    r1 = col - hi
    mid = r1.astype(bf16).astype(f32)
    lo = (r1 - mid).astype(bf16).astype(f32)
    return hi, mid, lo


def _aug_block(rows, terms, terms_at, ones_at=None):
    lane = lax.broadcasted_iota(jnp.int32, (rows, LANES), 1)
    blk = jnp.zeros((rows, LANES), f32) if ones_at is None else jnp.where((lane >= ones_at) & (lane < ones_at + 3), 1.0, 0.0)
    for i, t in enumerate(terms):
        blk = jnp.where(lane == terms_at + i, t, blk)
    return blk


def _fox_aug(qkv, cum, nh):
    s = qkv.shape[0]
    tm = _row_tile(s)

    def body(q_ref, k_ref, v_ref, cum_ref, qa_ref, ka_ref, va_ref):
        h = pl.program_id(1)
        c2 = _lane_col(cum_ref[...], h) * LOG2E
        hi, mid, lo = _split3(c2)
        qa_ref[:, :HEAD_DIM] = q_ref[...]
        qa_ref[:, HEAD_DIM:] = _aug_block(tm, (hi, mid, lo), 0, 3).astype(bf16)
        ka_ref[:, :HEAD_DIM] = k_ref[...]
        ka_ref[:, HEAD_DIM:] = _aug_block(tm, (-hi, -mid, -lo), 3, 0).astype(bf16)
        va_ref[:, :HEAD_DIM] = v_ref[...]
        va_ref[:, HEAD_DIM:] = _aug_block(tm, (), 0, 0).astype(bf16)

    ab = pl.BlockSpec((tm, AUG), lambda i, h: (i, h))
    return pl.pallas_call(
        body, name="fox_aug", grid=(s // tm, nh),
        in_specs=[pl.BlockSpec((tm, HEAD_DIM), lambda i, h: (i, h)), pl.BlockSpec((tm, HEAD_DIM), lambda i, h: (i, nh + h)),
                  pl.BlockSpec((tm, HEAD_DIM), lambda i, h: (i, 2 * nh + h)), pl.BlockSpec((tm, LANES), lambda i, h: (i, 0))],
        out_specs=[ab, ab, ab], out_shape=[jax.ShapeDtypeStruct((s, nh * AUG), bf16)] * 3,
        compiler_params=_cp("parallel", "parallel"))(qkv, qkv, qkv, cum)


def _fox_fwd(qa, ka, qkv, wn, nh, tq):
    s = qa.shape[0]
    fw = nh * HEAD_DIM
    group = FOX_KEY_GROUP if s // tq >= 2 * FOX_KEY_GROUP else 2

    def body(qa_ref, ka_ref, v_ref, wn_ref, o_ref, on_ref, lse_ref):
        i = pl.program_id(1)
        q = qa_ref[...]

        def logits_t(j, rows):
            return _dotb(ka_ref[pl.ds(pl.multiple_of(j * tq, tq), rows), :], q, NT)

        def pv_t(j, p_t):
            return _dotb(v_ref[pl.ds(pl.multiple_of(j * tq, tq), p_t.shape[0]), :], p_t, TN)

        key = lax.broadcasted_iota(jnp.int32, (tq, tq), 0)
        qry = lax.broadcasted_iota(jnp.int32, (tq, tq), 1)
        js = [i] + [jnp.maximum(i - n, 0) for n in range(1, group)]
        ts = [jnp.where(key <= qry, logits_t(i, tq), NEG)]
        ts += [jnp.where(lax.rem(i, group) >= n, logits_t(js[n], tq), NEG) for n in range(1, group)]
        m = functools.reduce(jnp.maximum, [jnp.max(t, axis=0, keepdims=True) for t in ts])
        ps = [jnp.exp2(t - m) for t in ts]
        l = functools.reduce(jnp.add, [jnp.sum(p, axis=0, keepdims=True) for p in ps])
        acc = functools.reduce(jnp.add, [pv_t(j, p) for j, p in zip(js, ps)])

        def step(jj, carry):
            m, l, acc = carry
            t = logits_t(group * jj, group * tq)
            mn = jnp.maximum(m, jnp.max(t, axis=0, keepdims=True))
            p_t = jnp.exp2(t - mn)
            alpha = jnp.exp2(m - mn)
            return mn, alpha * l + jnp.sum(p_t, axis=0, keepdims=True), alpha * acc + pv_t(group * jj, p_t)

        m, l, acc = lax.fori_loop(0, i // group, step, (m, l, acc))
        o = (acc / l).T
        o_ref[...] = o
        sub = lax.broadcasted_iota(jnp.int32, (LANES, tq), 0)
        lse_ref[0] = jnp.where(sub == 0, m + jnp.log2(l), 0.0).T
        r = lax.rsqrt(jnp.mean(o * o, axis=-1, keepdims=True) + EPS)
        on_ref[...] = (o * r * wn_ref[...]).astype(on_ref.dtype)

    hb = pl.BlockSpec((tq, HEAD_DIM), lambda h, i: (i, h))
    return pl.pallas_call(
        body, name="fox_fwd", grid=(nh, s // tq),
        in_specs=[pl.BlockSpec((tq, AUG), lambda h, i: (i, h)), pl.BlockSpec((s, AUG), lambda h, i: (0, h)),
                  pl.BlockSpec((s, HEAD_DIM), lambda h, i: (0, 2 * nh + h)), pl.BlockSpec((1, HEAD_DIM), lambda h, i: (0, 0))],
        out_specs=[hb, hb, pl.BlockSpec((1, tq, LANES), lambda h, i: (h, i, 0))],
        out_shape=[jax.ShapeDtypeStruct((s, fw), f32), jax.ShapeDtypeStruct((s, fw), bf16), jax.ShapeDtypeStruct((nh, s, LANES), f32)],
        compiler_params=_cp("parallel", "parallel"))(qa, ka, qkv, wn)


def _fox_post_bwd(don, o, lse2, cum, qkv, wn, nh):
    s, fw = o.shape
    tm = _row_tile(s)

    def body(don_ref, o_ref, lse_ref, cum_ref, q_ref, wn_ref, qb_ref, doa_ref, dwn_ref):
        i = pl.program_id(0)
        h = pl.program_id(1)

        @pl.when((i == 0) & (h == 0))
        def _():
            dwn_ref[...] = jnp.zeros_like(dwn_ref)

        o = o_ref[...]
        don = don_ref[...].astype(f32)
        r = lax.rsqrt(jnp.mean(o * o, axis=-1, keepdims=True) + EPS)
        n = o * r
        dwn_ref[...] += jnp.sum(don * n, axis=0, keepdims=True)
        dn = don * wn_ref[...]
        do = r * (dn - n * jnp.mean(dn * n, axis=-1, keepdims=True))
        delta = jnp.sum(do * o, axis=-1, keepdims=True)
        a2 = _lane_col(cum_ref[...], h) * LOG2E - _lane_col(lse_ref[0], 0)
        qb_ref[:, :HEAD_DIM] = q_ref[...]
        qb_ref[:, HEAD_DIM:] = _aug_block(tm, _split3(a2), 0, 3).astype(bf16)
        doa_ref[:, :HEAD_DIM] = do.astype(bf16)
        doa_ref[:, HEAD_DIM:] = _aug_block(tm, _split3(-delta), 0).astype(bf16)

    hb = pl.BlockSpec((tm, HEAD_DIM), lambda i, h: (i, h))
    ab = pl.BlockSpec((tm, AUG), lambda i, h: (i, h))
    return pl.pallas_call(
        body, name="fox_post_bwd", grid=(s // tm, nh),
        in_specs=[hb, hb, pl.BlockSpec((1, tm, LANES), lambda i, h: (h, i, 0)), pl.BlockSpec((tm, LANES), lambda i, h: (i, 0)),
                  hb, pl.BlockSpec((1, HEAD_DIM), lambda i, h: (0, 0))],
        out_specs=[ab, ab, pl.BlockSpec((1, HEAD_DIM), lambda i, h: (0, 0))],
        out_shape=[jax.ShapeDtypeStruct((s, nh * AUG), bf16), jax.ShapeDtypeStruct((s, nh * AUG), bf16),
                   jax.ShapeDtypeStruct((1, HEAD_DIM), f32)],
        compiler_params=_cp("arbitrary", "arbitrary"))(don, o, lse2, cum, qkv, wn)


def _fox_bwd(qb, doa, ka, va, nh, tq):
    s = qb.shape[0]
    nq = s // tq
    fw = nh * HEAD_DIM

    def body(qb_ref, doa_ref, ka_ref, va_ref, dqx_ref, dkx_ref, dv_ref, dv_acc):
        j = pl.program_id(1)

        @pl.when(j == 0)
        def _():
            dqx_ref[...] = jnp.zeros_like(dqx_ref)

        kj = ka_ref[...]
        vj = va_ref[...]

        def tile(i, rows=tq, first=False, keep=None):
            off = pl.multiple_of(i * tq, tq)
            qi = qb_ref[pl.ds(off, rows), :]
            doi = doa_ref[pl.ds(off, rows), :]
            p = jnp.exp2(_dotb(qi, kj, NT))
            if keep is not None:
                p = jnp.where(keep, p, 0.0)
            ds = (p * _dotb(doi, vj, NT)).astype(bf16)
            dv = _dotb(p, doi[:, :HEAD_DIM], TN)
            dk = _dotb(ds, qi, TN)
            if first:
                dv_acc[...] = dv
                dkx_ref[...] = dk
            else:
                dv_acc[...] += dv
                dkx_ref[...] += dk
            dqx_ref[pl.ds(off, rows), :] += _dotb(ds, kj)

        n = nq - 1 - j
        b0 = jnp.minimum(j, nq - 2)
        qpos = b0 * tq + lax.broadcasted_iota(jnp.int32, (2 * tq, tq), 0)
        kpos = j * tq + lax.broadcasted_iota(jnp.int32, (2 * tq, tq), 1)
        tile(b0, 2 * tq, first=True, keep=(kpos <= qpos) & ((qpos < (j + 1) * tq) | (lax.rem(n, 2) == 1)))

        def pair(ii, carry):
            tile(j + 1 + lax.rem(n, 2) + 2 * ii, 2 * tq)
            return carry

        lax.fori_loop(0, n // 2, pair, 0)
        dv_ref[...] = dv_acc[...].astype(dv_ref.dtype)

    panel = pl.BlockSpec((s, AUG), lambda h, j: (0, h))
    blk = pl.BlockSpec((tq, AUG), lambda h, j: (j, h))
    return pl.pallas_call(
        body, name="fox_bwd", grid=(nh, nq), in_specs=[panel, panel, blk, blk],
        out_specs=[panel, blk, pl.BlockSpec((tq, HEAD_DIM), lambda h, j: (j, h))],
        out_shape=[jax.ShapeDtypeStruct((s, nh * AUG), f32), jax.ShapeDtypeStruct((s, nh * AUG), f32),
                   jax.ShapeDtypeStruct((s, fw), bf16)],
        scratch_shapes=[pltpu.VMEM((tq, HEAD_DIM), f32)],
        compiler_params=_cp("parallel", "arbitrary"))(qb, doa, ka, va)


def _fox_unpack(dqx, dkx, nh):
    s = dqx.shape[0]
    fw = nh * HEAD_DIM
    tm = _row_tile(s)

    def body(dqx_ref, dkx_ref, dq_ref, dk_ref, dcum_ref):
        h = pl.program_id(1)

        @pl.when(h == 0)
        def _():
            dcum_ref[...] = jnp.zeros_like(dcum_ref)

        dq_ref[...] = (dqx_ref[:, :HEAD_DIM] * (HEAD_DIM ** -0.5)).astype(dq_ref.dtype)
        dk_ref[...] = (dkx_ref[:, :HEAD_DIM] * LN2).astype(dk_ref.dtype)
        d = _lane_col(dqx_ref[:, HEAD_DIM:], 0) - _lane_col(dkx_ref[:, HEAD_DIM:], 3)
        lane = lax.broadcasted_iota(jnp.int32, (tm, LANES), 1)
        dcum_ref[...] += jnp.where(lane == h, d, 0.0)

    ab = pl.BlockSpec((tm, AUG), lambda i, h: (i, h))
    hb = pl.BlockSpec((tm, HEAD_DIM), lambda i, h: (i, h))
    return pl.pallas_call(
        body, name="fox_unpack", grid=(s // tm, nh), in_specs=[ab, ab],
        out_specs=[hb, hb, pl.BlockSpec((tm, LANES), lambda i, h: (i, 0))],
        out_shape=[jax.ShapeDtypeStruct((s, fw), bf16), jax.ShapeDtypeStruct((s, fw), bf16), jax.ShapeDtypeStruct((s, LANES), f32)],
        compiler_params=_cp("parallel", "arbitrary"))(dqx, dkx)


def _conv_pre(xx, w, tm):
    pre = None
    for k in range(CONV_W):
        sh = CONV_W - 1 - k
        t = (pltpu.roll(xx, sh, 0) if sh else xx)[8:, :] * w[k:k + 1, :]
        pre = t if pre is None else pre + t
    return pre


def _gdn_pre(x, w, nh):
    s, cw = x.shape
    tm = _row_tile(s)
    fw = nh * HEAD_DIM

    def body(x_ref, prev_ref, w_ref, y_ref):
        i = pl.program_id(0)
        j = pl.program_id(1)
        for h in range(nh):
            sl = slice(h * HEAD_DIM, (h + 1) * HEAD_DIM)
            prev = jnp.where(i == 0, 0.0, prev_ref[:, sl])
            pre = _conv_pre(jnp.concatenate([prev, x_ref[:, sl]], axis=0), w_ref[:, sl], tm)
            y = pre * _sigmoid(pre)
            yn = y * lax.rsqrt(jnp.sum(y * y, axis=-1, keepdims=True) + EPS)
            y_ref[:, sl] = jnp.where(j < 2, yn, y)

    return pl.pallas_call(
        body, name="gdn_pre", grid=(s // tm, cw // fw),
        in_specs=[pl.BlockSpec((tm, fw), lambda i, j: (i, j)),
                  pl.BlockSpec((8, fw), lambda i, j: (jnp.maximum(i * (tm // 8) - 1, 0), j)),
                  pl.BlockSpec((CONV_W, fw), lambda i, j: (0, j))],
        out_specs=pl.BlockSpec((tm, fw), lambda i, j: (i, j)),
        out_shape=jax.ShapeDtypeStruct((s, cw), f32),
        compiler_params=_cp("parallel", "parallel"))(x, x, w)


def _gdn_pre_bwd(x, w, dyn, nh):
    s, cw = x.shape
    tm = _row_tile(s)
    fw = nh * HEAD_DIM

    def body(x_ref, prev_ref, w_ref, dyn_ref, dpre_ref):
        i = pl.program_id(0)
        j = pl.program_id(1)
        for h in range(nh):
            sl = slice(h * HEAD_DIM, (h + 1) * HEAD_DIM)
            prev = jnp.where(i == 0, 0.0, prev_ref[:, sl])
            pre = _conv_pre(jnp.concatenate([prev, x_ref[:, sl]], axis=0), w_ref[:, sl], tm)
            sg = _sigmoid(pre)
            y = pre * sg
            dyn = dyn_ref[:, sl]
            r = lax.rsqrt(jnp.sum(y * y, axis=-1, keepdims=True) + EPS)
            yn = y * r
            dy_n = r * (dyn - yn * jnp.sum(dyn * yn, axis=-1, keepdims=True))
            dy = jnp.where(j < 2, dy_n, dyn)
            dpre_ref[:, sl] = dy * sg * (1.0 + pre * (1.0 - sg))

    hb = pl.BlockSpec((tm, fw), lambda i, j: (i, j))
    return pl.pallas_call(
        body, name="gdn_pre_bwd", grid=(s // tm, cw // fw),
        in_specs=[hb, pl.BlockSpec((8, fw), lambda i, j: (jnp.maximum(i * (tm // 8) - 1, 0), j)),
                  pl.BlockSpec((CONV_W, fw), lambda i, j: (0, j)), hb],
        out_specs=hb, out_shape=jax.ShapeDtypeStruct((s, cw), f32),
        compiler_params=_cp("parallel", "parallel"))(x, x, w, dyn)


def _conv_bwd(x, w, dpre, nh):
    s, cw = x.shape
    tm = _row_tile(s)
    fw = nh * HEAD_DIM
    ni = s // tm

    def body(x_ref, prev_ref, w_ref, dp_ref, nxt_ref, dx_ref, dw_ref):
        i = pl.program_id(1)

        @pl.when(i == 0)
        def _():
            dw_ref[...] = jnp.zeros_like(dw_ref)

        for h in range(nh):
            sl = slice(h * HEAD_DIM, (h + 1) * HEAD_DIM)
            wv = w_ref[:, sl]
            dp = dp_ref[:, sl]
            nxt = jnp.where(i == ni - 1, 0.0, nxt_ref[:, sl])
            dd = jnp.concatenate([dp, nxt], axis=0)
            prev = jnp.where(i == 0, 0.0, prev_ref[:, sl])
            xx = jnp.concatenate([prev, x_ref[:, sl]], axis=0)
            dx = None
            rows = []
            for k in range(CONV_W):
                sh = CONV_W - 1 - k
                t = (pltpu.roll(dd, tm + 8 - sh, 0) if sh else dd)[:tm, :] * wv[k:k + 1, :]
                dx = t if dx is None else dx + t
                xs = (pltpu.roll(xx, sh, 0) if sh else xx)[8:, :]
                rows.append(jnp.sum(dp * xs, axis=0, keepdims=True))
            dx_ref[:, sl] = dx.astype(dx_ref.dtype)
            dw_ref[:, sl] += jnp.concatenate(rows, axis=0)

    hb = pl.BlockSpec((tm, fw), lambda j, i: (i, j))
    wb = pl.BlockSpec((CONV_W, fw), lambda j, i: (0, j))
    return pl.pallas_call(
        body, name="conv_bwd", grid=(cw // fw, ni),
        in_specs=[hb, pl.BlockSpec((8, fw), lambda j, i: (jnp.maximum(i * (tm // 8) - 1, 0), j)), wb, hb,
                  pl.BlockSpec((8, fw), lambda j, i: (jnp.minimum((i + 1) * (tm // 8), s // 8 - 1), j))],
        out_specs=[hb, wb],
        out_shape=[jax.ShapeDtypeStruct((s, cw), bf16), jax.ShapeDtypeStruct((CONV_W, cw), f32)],
        compiler_params=_cp("parallel", "arbitrary"))(x, x, w, dpre, dpre)


def _chunk_consts():
    c = GDN_CHUNK
    r = lax.broadcasted_iota(jnp.int32, (c, c), 0)
    q = lax.broadcasted_iota(jnp.int32, (c, c), 1)
    return r >= q, r > q, (r == q).astype(f32)


def _chunk_head(qkvn, sm, gcs, gcs_t, h, nh):
    fw = nh * HEAD_DIM
    q = qkvn[:, h * HEAD_DIM:(h + 1) * HEAD_DIM] * (HEAD_DIM ** -0.5)
    k = qkvn[:, fw + h * HEAD_DIM: fw + (h + 1) * HEAD_DIM]
    v = qkvn[:, 2 * fw + h * HEAD_DIM: 2 * fw + (h + 1) * HEAD_DIM]
    beta = _lane_col(sm, 2 * nh + h)
    gc = _lane_col(gcs, nh + h)
    gc_row = gcs_t[nh + h: nh + h + 1, :]
    incl, strict, _ = _chunk_consts()
    decay = jnp.where(incl, jnp.exp(jnp.minimum(gc - gc_row, 0.0)), 0.0)
    eg = jnp.exp(gc)
    g_last = gc[GDN_CHUNK - 1:GDN_CHUNK, :]
    egl = jnp.exp(g_last)
    ekd = jnp.exp(g_last - gc)
    kb = k * beta
    vb = v * beta
    kk = _dotb(kb, k, NT)
    qk = _dotb(q, k, NT)
    return dict(q=q, k=k, v=v, beta=beta, gc=gc, decay=decay, eg=eg, egl=egl, ekd=ekd, kb=kb, vb=vb, kk=kk, qk=qk,
                incl=incl, strict=strict)


def _unit_lower_inverses(lows, eye):
    ps = [-low for low in lows]
    ts = [eye + p for p in ps]
    for _ in range(5):
        ps = [_doth(p, p) for p in ps]
        ts = [t + _doth(t, p) for t, p in zip(ts, ps)]
    return ts


def _gdn_fwd(qkvn, sm, z, wn, nh):
    s = qkvn.shape[0]
    c = GDN_CHUNK
    nc = s // c
    fw = nh * HEAD_DIM

    def body(qkvn_ref, sm_ref, z_ref, wn_ref, on_ref, o_ref, st_ref, ti_ref, state):
        ci = pl.program_id(0)

        @pl.when(ci == 0)
        def _():
            state[...] = jnp.zeros_like(state)

        qkvn_v = qkvn_ref[...]
        sm_v = sm_ref[...]
        incl, strict, eye = _chunk_consts()
        gcs = _doth(incl.astype(f32), sm_v)
        gcs_t = gcs.T
        heads = range(nh)
        es = [_chunk_head(qkvn_v, sm_v, gcs, gcs_t, h, nh) for h in heads]
        tinvs = _unit_lower_inverses([jnp.where(strict, e["kk"] * e["decay"], 0.0) for e in es], eye)
        us = [_doth(t, e["vb"]) for t, e in zip(tinvs, es)]
        ws = [_doth(t, e["kb"] * e["eg"]) for t, e in zip(tinvs, es)]
        sts = [state[h] for h in heads]
        v_news = [u - _dotb(w, st) for u, w, st in zip(us, ws, sts)]
        qss = [_dotb(e["q"] * e["eg"], st) for e, st in zip(es, sts)]
        os_ = [qs + _dotb(jnp.where(incl, e["qk"] * e["decay"], 0.0), vn) for qs, e, vn in zip(qss, es, v_news)]
        upd = [_dotb(e["k"] * e["ekd"], vn, TN) for e, vn in zip(es, v_news)]
        for h in heads:
            st_ref[0, h] = sts[h]
            ti_ref[0, h] = tinvs[h]
            state[h] = sts[h] * es[h]["egl"] + upd[h]
            sl = slice(h * HEAD_DIM, (h + 1) * HEAD_DIM)
            o = os_[h]
            o_ref[:, sl] = o
            zz = z_ref[:, sl]
            r = lax.rsqrt(jnp.mean(o * o, axis=-1, keepdims=True) + EPS)
            on_ref[:, sl] = (o * r * wn_ref[...] * (zz * _sigmoid(zz))).astype(on_ref.dtype)

    return pl.pallas_call(
        body, name="gdn_fwd", grid=(nc,),
        in_specs=[pl.BlockSpec((c, 3 * fw), lambda i: (i, 0)), pl.BlockSpec((c, LANES), lambda i: (i, 0)),
                  pl.BlockSpec((c, fw), lambda i: (i, 0)), pl.BlockSpec((1, HEAD_DIM), lambda i: (0, 0))],
        out_specs=[pl.BlockSpec((c, fw), lambda i: (i, 0)), pl.BlockSpec((c, fw), lambda i: (i, 0)),
                   pl.BlockSpec((1, nh, HEAD_DIM, HEAD_DIM), lambda i: (i, 0, 0, 0)),
                   pl.BlockSpec((1, nh, c, c), lambda i: (i, 0, 0, 0))],
        out_shape=[jax.ShapeDtypeStruct((s, fw), bf16), jax.ShapeDtypeStruct((s, fw), f32),
                   jax.ShapeDtypeStruct((nc, nh, HEAD_DIM, HEAD_DIM), f32), jax.ShapeDtypeStruct((nc, nh, c, c), f32)],
        scratch_shapes=[pltpu.VMEM((nh, HEAD_DIM, HEAD_DIM), f32)],
        compiler_params=_cp("arbitrary"))(qkvn, sm, z, wn)


def _gdn_post_bwd(don, o, z, wn, nh):
    s, fw = o.shape
    tm = _row_tile(s)

    def body(don_ref, o_ref, z_ref, wn_ref, do_ref, dz_ref, dwn_ref):
        i = pl.program_id(0)
        h = pl.program_id(1)

        @pl.when((i == 0) & (h == 0))
        def _():
            dwn_ref[...] = jnp.zeros_like(dwn_ref)

        o = o_ref[...]
        zz = z_ref[...]
        don = don_ref[...].astype(f32)
        wv = wn_ref[...]
        r = lax.rsqrt(jnp.mean(o * o, axis=-1, keepdims=True) + EPS)
        n = o * r
        sg = _sigmoid(zz)
        silu = zz * sg
        dz_ref[...] = (don * n * wv * sg * (1.0 + zz * (1.0 - sg))).astype(dz_ref.dtype)
        dnw = don * silu
        dwn_ref[...] += jnp.sum(dnw * n, axis=0, keepdims=True)
        dn = dnw * wv
        do_ref[...] = r * (dn - n * jnp.mean(dn * n, axis=-1, keepdims=True))

    hb = pl.BlockSpec((tm, HEAD_DIM), lambda i, h: (i, h))
    wb = pl.BlockSpec((1, HEAD_DIM), lambda i, h: (0, 0))
    return pl.pallas_call(
        body, name="gdn_post_bwd", grid=(s // tm, nh), in_specs=[hb, hb, hb, wb], out_specs=[hb, hb, wb],
        out_shape=[jax.ShapeDtypeStruct((s, fw), f32), jax.ShapeDtypeStruct((s, fw), bf16),
                   jax.ShapeDtypeStruct((1, HEAD_DIM), f32)],
        compiler_params=_cp("arbitrary", "arbitrary"))(don, o, z, wn)


def _gdn_bwd(qkvn, sm, do, states, tinvs, nh):
    s = qkvn.shape[0]
    c = GDN_CHUNK
    nc = s // c
    fw = nh * HEAD_DIM

    def body(qkvn_ref, sm_ref, do_ref, st_ref, ti_ref, dqkvn_ref, dsm_ref, dstate):
        ci = pl.program_id(0)

        @pl.when(ci == 0)
        def _():
            dstate[...] = jnp.zeros_like(dstate)

        qkvn_v = qkvn_ref[...]
        sm_v = sm_ref[...]
        incl, strict, eye = _chunk_consts()
        inclf = incl.astype(f32)
        gcs = _doth(inclf, sm_v)
        gcs_t = gcs.T
        lane = lax.broadcasted_iota(jnp.int32, (c, LANES), 1)
        last_row = lax.broadcasted_iota(jnp.int32, (c, 1), 0) == c - 1
        ones_cl = jnp.ones((c, LANES), f32)
        each = lambda f: [f(h) for h in range(nh)]
        es = each(lambda h: _chunk_head(qkvn_v, sm_v, gcs, gcs_t, h, nh))
        tinv = each(lambda h: ti_ref[0, h])
        st = each(lambda h: st_ref[0, h])
        dst = each(lambda h: dstate[h])
        do = each(lambda h: do_ref[:, h * HEAD_DIM:(h + 1) * HEAD_DIM])
        kg = each(lambda h: es[h]["kb"] * es[h]["eg"])
        qg = each(lambda h: es[h]["q"] * es[h]["eg"])
        kd = each(lambda h: es[h]["k"] * es[h]["ekd"])
        u = each(lambda h: _doth(tinv[h], es[h]["vb"]))
        w = each(lambda h: _doth(tinv[h], kg[h]))
        a = each(lambda h: jnp.where(incl, es[h]["qk"] * es[h]["decay"], 0.0))
        v_new = each(lambda h: u[h] - _dotb(w[h], st[h]))
        dv_new = each(lambda h: _dotb(a[h], do[h], TN) + _dotb(kd[h], dst[h]))
        da = each(lambda h: jnp.where(incl, _dotb(do[h], v_new[h], NT), 0.0))
        dqg = each(lambda h: _dotb(do[h], st[h], NT))
        dkd = each(lambda h: _dotb(v_new[h], dst[h], NT))
        dglast = each(lambda h: es[h]["egl"] * jnp.sum(jnp.sum(dst[h] * st[h], axis=1, keepdims=True), axis=0, keepdims=True))
        dw = each(lambda h: -_dotb(dv_new[h], st[h], NT))
        new_dst = each(lambda h: _dotb(qg[h], do[h], TN) + es[h]["egl"] * dst[h] - _dotb(w[h], dv_new[h], TN))
        dtinv = each(lambda h: _doth(dv_new[h], es[h]["vb"], NT) + _doth(dw[h], kg[h], NT))
        dvb = each(lambda h: _doth(tinv[h], dv_new[h], TN))
        dkg = each(lambda h: _doth(tinv[h], dw[h], TN))
        tdt = each(lambda h: _doth(tinv[h], dtinv[h], TN))
        dlow = each(lambda h: -_doth(tdt[h], tinv[h], NT))
        dkk = each(lambda h: jnp.where(strict, dlow[h] * es[h]["decay"], 0.0))
        dqk = each(lambda h: da[h] * es[h]["decay"])
        darg = each(lambda h: (jnp.where(strict, dlow[h] * es[h]["kk"], 0.0) + da[h] * es[h]["qk"]) * es[h]["decay"])
        dgc = each(lambda h: jnp.sum(darg[h], axis=1, keepdims=True) - _doth(darg[h], ones_cl, TN)[:, 0:1])
        dkb = each(lambda h: _dotb(dkk[h], es[h]["k"]) + dkg[h] * es[h]["eg"])
        dk = each(lambda h: _dotb(dkk[h], es[h]["kb"], TN) + _dotb(dqk[h], es[h]["q"], TN) + dkd[h] * es[h]["ekd"]
                  + dkb[h] * es[h]["beta"])
        dq = each(lambda h: (_dotb(dqk[h], es[h]["k"]) + dqg[h] * es[h]["eg"]) * (HEAD_DIM ** -0.5))
        s_kd = each(lambda h: jnp.sum(dkd[h] * kd[h], axis=1, keepdims=True))
        dgc = each(lambda h: dgc[h] + jnp.sum(dkg[h] * kg[h] + dqg[h] * qg[h], axis=1, keepdims=True) - s_kd[h]
                   + jnp.where(last_row, jnp.sum(s_kd[h], axis=0, keepdims=True) + dglast[h], 0.0))
        dg = each(lambda h: _doth(inclf, dgc[h] * ones_cl, TN)[:, 0:1])
        dsm = jnp.zeros((c, LANES), f32)
        for h in range(nh):
            dstate[h] = new_dst[h]
            dbeta = jnp.sum(dkb[h] * es[h]["k"] + dvb[h] * es[h]["v"], axis=1, keepdims=True)
            dqkvn_ref[:, h * HEAD_DIM:(h + 1) * HEAD_DIM] = dq[h]
            dqkvn_ref[:, fw + h * HEAD_DIM: fw + (h + 1) * HEAD_DIM] = dk[h]
            dqkvn_ref[:, 2 * fw + h * HEAD_DIM: 2 * fw + (h + 1) * HEAD_DIM] = dvb[h] * es[h]["beta"]
            dsm = dsm + jnp.where(lane == nh + h, dg[h], 0.0) + jnp.where(lane == 2 * nh + h, dbeta, 0.0)
        dsm_ref[...] = dsm

    rev = lambda i: (nc - 1 - i, 0)
    rev4 = lambda i: (nc - 1 - i, 0, 0, 0)
    return pl.pallas_call(
        body, name="gdn_bwd", grid=(nc,),
        in_specs=[pl.BlockSpec((c, 3 * fw), rev), pl.BlockSpec((c, LANES), rev), pl.BlockSpec((c, fw), rev),
                  pl.BlockSpec((1, nh, HEAD_DIM, HEAD_DIM), rev4), pl.BlockSpec((1, nh, c, c), rev4)],
        out_specs=[pl.BlockSpec((c, 3 * fw), rev), pl.BlockSpec((c, LANES), rev)],
        out_shape=[jax.ShapeDtypeStruct((s, 3 * fw), f32), jax.ShapeDtypeStruct((s, LANES), f32)],
        scratch_shapes=[pltpu.VMEM((nh, HEAD_DIM, HEAD_DIM), f32)],
        compiler_params=_cp("arbitrary"))(qkvn, sm, do, states, tinvs)


MM_TILES = (1024, 512, 2048)
MM_TILES_TN = (512, 1024, 4096)
MM_TILES_F_DEEP = (1024, 512, 2816)
MM_TILES_LONG_K = (1024, 512, 2560)


def _hosted(res, comm):
    return res if comm else (res, None)


def _ffn_fwd(tag, x, g, mod3, w, comm_up=None, comm_down=None, wd_of=None):
    wg_t, wu_t, wd = w
    sh, sc, gt = mod3
    h = _ada_in(tag + "_ada", x, g, sh, sc)
    (gate, up, act), got_up = _hosted(_mm(tag + "_up", [[(h, wg_t)], [(h, wu_t)]], "nt", MM_TILES, _ep_swiglu,
                                          (bf16, bf16, bf16), comm=comm_up), comm_up)
    if wd_of:
        wd = wd_of(got_up)
    (xn, y), got_down = _hosted(_mm(tag + "_down", [[(act, wd)]], "nn", MM_TILES_F_DEEP, _ep_residual, (f32, bf16),
                                    extras=((x, "mn"), (MACARON_W * gt, "n")), comm=comm_down), comm_down)
    return xn, dict(x=x, h=h, gate=gate, up=up, y=y), got_up, got_down


def _ffn_bwd(tag, dxn, res, g, mod3, w, comm_dact=None, comm_dh_of=None):
    wg_t, wu_t, wd = w
    sh, sc, gt = mod3
    dy, dgs = _gate_bwd(tag + "_gate_bwd", dxn, res["y"], MACARON_W * gt)
    (act, dgate, dup), got = _hosted(_mm(tag + "_dact", [[(dy, wd)]], "nt", MM_TILES, _ep_swiglu_bwd, (bf16, bf16, bf16),
                                         extras=((res["gate"], "mn"), (res["up"], "mn")), comm=comm_dact), comm_dact)
    (dwd,) = _mm(tag + "_dwd", [[(act, dy)]], "tn", MM_TILES_TN, _ep_plain, (bf16,))
    (dwg_t,) = _mm(tag + "_dwg", [[(dgate, res["h"])]], "tn", MM_TILES_TN, _ep_plain, (bf16,))
    (dwu_t,) = _mm(tag + "_dwu", [[(dup, res["h"])]], "tn", MM_TILES_TN, _ep_plain, (bf16,))
    comm_dh = comm_dh_of and comm_dh_of((dwg_t, dwu_t, dwd))
    (dh,), got_dh = _hosted(_mm(tag + "_dh", [[(dgate, wg_t), (dup, wu_t)]], "nn", MM_TILES_F_DEEP, _ep_plain, (bf16,), comm=comm_dh),
                            comm_dh)
    dx, dsh, a = _ada_bwd(tag + "_ada_bwd", res["x"], g, sc, dh, dxn)
    return dx, (dwg_t, dwu_t, dwd), (dsh, a * g, MACARON_W * dgs), a * (1.0 + sc), got, got_dh


def _local_step(x, target, mods, norm_g, final_norm, ffn1_w, later_w, prm, fox_wn, gdn_wn, conv_w, nh, hooks=None):
    s, d = x.shape
    fw = nh * HEAD_DIM
    tq = _tile(s, min(256, s // 2))
    g_rows = [norm_g[i:i + 1] for i in range(3)]
    m1, m2, m3 = mods[0:3], mods[3:6], mods[6:9]

    x1, r1, got_up, got_down = _ffn_fwd("ffn1", x, g_rows[0], m1, ffn1_w, hooks and hooks.gather_mix_spec(),
                                        hooks and hooks.gather_ffn2_spec(), hooks and hooks.ffn1_wd)
    if hooks:
        ffn1_w = ffn1_w[:2] + (hooks.ffn1_wd(got_up),)
    w_cat_t, w_out, ffn2_w = hooks.gathered(got_up, got_down) if hooks else later_w
    h2 = _ada_in("mix_ada", x1, g_rows[1], m2[0], m2[1])
    w_fox, w_gdn, w_z, w_s = w_cat_t[:3 * fw], w_cat_t[3 * fw:6 * fw], w_cat_t[6 * fw:7 * fw], w_cat_t[7 * fw:]
    colscale = jnp.concatenate([jnp.full((1, fw), FOX_Q_SCALE, f32), jnp.ones((1, 2 * fw), f32)], axis=1)
    (qkv_f,) = _mm("proj_fox", [[(h2, w_fox)]], "nt", MM_TILES, _ep_colscale, (bf16,), extras=((colscale, "n"),))
    (qkv_g,) = _mm("proj_gdn", [[(h2, w_gdn)]], "nt", MM_TILES, _ep_plain, (f32,))
    (z,) = _mm("proj_z", [[(h2, w_z)]], "nt", MM_TILES, _ep_plain, (f32,))
    (ps,) = _mm("proj_s", [[(h2, w_s)]], "nt", MM_TILES, _ep_plain, (f32,))
    sm, cum = _small_fwd(ps, prm, nh)
    qa, ka, va = _fox_aug(qkv_f, cum, nh)
    o_f, on_f, lse2 = _fox_fwd(qa, ka, qkv_f, fox_wn, nh, tq)
    qkvn = _gdn_pre(qkv_g, conv_w, nh)
    on_g, o_g, states, tinvs = _gdn_fwd(qkvn, sm, z, gdn_wn, nh)
    w_top, w_bot = w_out[:fw], w_out[fw:]
    x2, mix = _mm("mix_out", [[(on_f, w_top), (on_g, w_bot)]], "nn", MM_TILES, _ep_residual, (f32, bf16),
                  extras=((x1, "mn"), (m2[2], "n")))
    x3, r3, _, _ = _ffn_fwd("ffn2", x2, g_rows[2], m3, ffn2_w)
    loss, dx3, dfinal = _final_loss(x3, final_norm, target)

    dx2, dffn2, dmod3, dg3, _, _ = _ffn_bwd("ffn2", dx3, r3, g_rows[2], m3, ffn2_w)
    rs_ffn2 = hooks and hooks.rs_ffn2_spec(dffn2)
    dmix, dgt2 = _gate_bwd("mix_gate_bwd", dx2, mix, m2[2])
    (don_f,) = _mm("mix_dof", [[(dmix, w_top)]], "nt", MM_TILES, _ep_plain, (f32,))
    (don_g,) = _mm("mix_dog", [[(dmix, w_bot)]], "nt", MM_TILES, _ep_plain, (f32,))
    (dw_top,) = _mm("mix_dwtop", [[(on_f, dmix)]], "tn", MM_TILES_TN, _ep_plain, (bf16,))
    (dw_bot,) = _mm("mix_dwbot", [[(on_g, dmix)]], "tn", MM_TILES_TN, _ep_plain, (bf16,))
    qb, doa, dfox_wn = _fox_post_bwd(don_f, o_f, lse2, cum, qkv_f, fox_wn, nh)
    dqx, dkx, dv_f = _fox_bwd(qb, doa, ka, va, nh, tq)
    dq_f, dk_f, dcum = _fox_unpack(dqx, dkx, nh)
    do_g, dz, dgdn_wn = _gdn_post_bwd(don_g, o_g, z, gdn_wn, nh)
    dqkvn, dsm = _gdn_bwd(qkvn, sm, do_g, states, tinvs, nh)
    dpre = _gdn_pre_bwd(qkv_g, conv_w, dqkvn, nh)
    dqkv_g, dconv = _conv_bwd(qkv_g, conv_w, dpre, nh)
    dps, pg = _small_bwd(ps, prm, dsm, dcum, nh)
    dproj = jnp.concatenate([dq_f, dk_f, dv_f, dqkv_g, dz, dps], axis=1)
    ((dw_cat_t,), got_ffn2) = _hosted(_mm("proj_dw", [[(dproj, h2)]], "tn", MM_TILES_TN, _ep_plain, (bf16,), comm=rs_ffn2), rs_ffn2)
    dw_out = jnp.concatenate([dw_top, dw_bot], axis=0)
    rs_mix = hooks and hooks.rs_mix_spec(dw_cat_t, dw_out)
    (dh2,) = _mm("proj_dh", [[(dproj, w_cat_t)]], "nn", MM_TILES_LONG_K, _ep_plain, (bf16,))
    dx1, dsh2, a2 = _ada_bwd("mix_ada_bwd", x1, g_rows[1], m2[1], dh2, dx2)
    dmod2 = (dsh2, a2 * g_rows[1], dgt2)
    dg2 = a2 * (1.0 + m2[1])
    dx0, dffn1, dmod1, dg1, got_mix, got_ffn1 = _ffn_bwd("ffn1", dx1, r1, g_rows[0], m1, ffn1_w, rs_mix,
                                                          hooks and hooks.rs_ffn1_spec)

    big = dict(ffn=(dffn1, dffn2), w_cat_t=dw_cat_t, w_out=dw_out, got_ffn2=got_ffn2, got_mix=got_mix, got_ffn1=got_ffn1)
    small = dict(loss=loss, norm_g=jnp.concatenate([dg1, dg2, dg3], axis=0), final_norm=dfinal, fox_wn=dfox_wn,
                 gdn_wn=dgdn_wn, pg=pg, conv=dconv, mod=jnp.concatenate(list(dmod1) + list(dmod2) + list(dmod3), axis=1))
    return dx0, big, small


def _w_in_row_groups(nh):
    fw = nh * HEAD_DIM
    sizes = [3 * fw, nh, 3 * fw, nh, nh, fw]
    offs = [0]
    for sz in sizes:
        offs.append(offs[-1] + sz)
    return [(offs[i], offs[i + 1]) for i in range(len(sizes))]


def _build_w_cat_t(w_in_t, nh):
    gq, gf, gg, ga, gb, gz = _w_in_row_groups(nh)
    d = w_in_t.shape[1]
    rows = lambda r: w_in_t[r[0]:r[1]]
    pad = jnp.zeros((LANES - 3 * nh, d), w_in_t.dtype)
    return jnp.concatenate([rows(gq), rows(gg), rows(gz), rows(gf), rows(ga), rows(gb), pad], axis=0)


def _split_dw_cat_t(dw_cat_t, nh):
    fw = nh * HEAD_DIM
    o = 7 * fw
    return jnp.concatenate([dw_cat_t[:3 * fw], dw_cat_t[o:o + nh], dw_cat_t[3 * fw:6 * fw], dw_cat_t[o + nh:o + 2 * nh],
                            dw_cat_t[o + 2 * nh:o + 3 * nh], dw_cat_t[6 * fw:7 * fw]], axis=0)


def _head_params(fox_f_bias, gdn_dt_bias, gdn_a_log, nh):
    z = jnp.zeros((8, LANES), f32)
    z = z.at[0, 0:nh].set(fox_f_bias.reshape(nh))
    z = z.at[1, nh:2 * nh].set(gdn_dt_bias.reshape(nh))
    z = z.at[2, nh:2 * nh].set(gdn_a_log.reshape(nh))
    return z


ANY = pl.BlockSpec(memory_space=pl.ANY)
IN_VMEM = pl.BlockSpec(memory_space=pltpu.VMEM)
N_OTHER_CHIPS = 3


def _place():
    x, y, c = lax.axis_index("x"), lax.axis_index("y"), lax.axis_index("c")
    chips = [(1 - x, y), (x, 1 - y), (1 - x, 1 - y)]
    return x, y, c, chips


def _allgather8(name, v):
    r, n = v.shape

    def body(v_ref, out_ref, send_sems, recv_sems, local_sem):
        x, y, c, _ = _place()
        me = 4 * x + 2 * y + c
        mine = pltpu.make_async_copy(v_ref, out_ref.at[me], local_sem)
        mine.start()
        copies = []
        for k in range(1, 8):
            fx, fy, fc = (k >> 2) & 1, (k >> 1) & 1, k & 1
            peer = (x + fx - 2 * x * fx, y + fy - 2 * y * fy, c + fc - 2 * c * fc)
            cp = pltpu.make_async_remote_copy(src_ref=v_ref, dst_ref=out_ref.at[me], send_sem=send_sems.at[k - 1],
                                              recv_sem=recv_sems.at[k - 1], device_id=peer, device_id_type=MESH)
            cp.start()
            copies.append(cp)
        for cp in copies:
            cp.wait()
        mine.wait()

    return pl.pallas_call(
        body, name=name, in_specs=[IN_VMEM], out_specs=IN_VMEM, out_shape=jax.ShapeDtypeStruct((8, r, n), v.dtype),
        scratch_shapes=[pltpu.SemaphoreType.DMA((7,)), pltpu.SemaphoreType.DMA((7,)), pltpu.SemaphoreType.DMA],
        compiler_params=pltpu.CompilerParams(vmem_limit_bytes=VMEM_LIMIT_V7X))(v)


class _CommSpec:
    def __init__(self, ins, out_shapes, sem_counts, run):
        self.ins, self.out_shapes, self.sem_counts, self.run = list(ins), list(out_shapes), sem_counts, run

    def sems(self):
        return [pltpu.SemaphoreType.DMA((n,)) for n in self.sem_counts]


def _run_comm(name, spec):
    ni, no = len(spec.ins), len(spec.out_shapes)

    def body(*refs):
        ins, outs, sems = refs[:ni], refs[ni:ni + no], refs[ni + no:]
        spec.run("start", ins, outs, sems)
        spec.run("finish", ins, outs, sems)

    return pl.pallas_call(body, name=name, in_specs=[ANY] * ni, out_specs=[ANY] * no, out_shape=spec.out_shapes,
                          scratch_shapes=spec.sems())(*spec.ins)


def _gather_spec(halves):
    nw = len(halves)

    def run(phase, ins, outs, sems):
        ici_send, ici_recv, d2d_send, d2d_recv = sems
        x, y, c, chips = _place()
        s = 2 * x + y
        sib = (x, y, 1 - c)

        def over_ici(w, j, dst):
            cx, cy = chips[j]
            return pltpu.make_async_remote_copy(src_ref=ins[w].at[c], dst_ref=dst, send_sem=ici_send.at[w * 3 + j],
                                                recv_sem=ici_recv.at[w * 3 + j], device_id=(cx, cy, c), device_id_type=MESH)

        def to_sibling(w, j, blk):
            return pltpu.make_async_remote_copy(src_ref=blk, dst_ref=blk, send_sem=d2d_send.at[w * 3 + j],
                                                recv_sem=d2d_recv.at[w * 3 + j], device_id=sib, device_id_type=MESH)

        pairs = [(w, j) for w in range(nw) for j in range(N_OTHER_CHIPS)]
        chip_of = lambda j: 2 * chips[j][0] + chips[j][1]
        if phase == "start":
            for w, j in pairs:
                over_ici(w, j, outs[w].at[c, s]).start()
            return
        for w, j in pairs:
            landed = outs[w].at[c, chip_of(j)]
            over_ici(w, j, landed).wait_recv()
            to_sibling(w, j, landed).start()
        for w, j in pairs:
            to_sibling(w, j, outs[w].at[1 - c, chip_of(j)]).wait_recv()
        for w, j in pairs:
            over_ici(w, j, outs[w].at[c, s]).wait_send()
            to_sibling(w, j, outs[w].at[c, chip_of(j)]).wait_send()

    n3 = nw * N_OTHER_CHIPS
    return _CommSpec(halves, [jax.ShapeDtypeStruct((2, 4) + h.shape[1:], h.dtype) for h in halves], [n3] * 4, run)


def _to_chips_spec(partials):
    nw = len(partials)

    def run(phase, ins, outs, sems):
        send_sems, recv_sems = sems
        x, y, c, chips = _place()
        for w in range(nw):
            for j, (cx, cy) in enumerate(chips):
                cp = pltpu.make_async_remote_copy(src_ref=ins[w].at[2 * cx + cy], dst_ref=outs[w].at[j],
                                                  send_sem=send_sems.at[w * 3 + j], recv_sem=recv_sems.at[w * 3 + j],
                                                  device_id=(cx, cy, c), device_id_type=MESH)
                if phase == "start":
                    cp.start()
                else:
                    cp.wait()

    n3 = nw * N_OTHER_CHIPS
    return _CommSpec(partials, [jax.ShapeDtypeStruct((3,) + a.shape[1:], a.dtype) for a in partials], [n3, n3], run)


def _send_to_sibling(name, srcs, other_half):
    nw = len(srcs)

    def body(*refs):
        ins, outs = refs[:nw], refs[nw:2 * nw]
        send_sems, recv_sems = refs[2 * nw:]
        x, y, c, _ = _place()
        cps = []
        for w in range(nw):
            cp = pltpu.make_async_remote_copy(src_ref=ins[w].at[1 - c] if other_half else ins[w], dst_ref=outs[w],
                                              send_sem=send_sems.at[w], recv_sem=recv_sems.at[w],
                                              device_id=(x, y, 1 - c), device_id_type=MESH)
            cp.start()
            cps.append(cp)
        for cp in cps:
            cp.wait()

    return pl.pallas_call(
        body, name=name, in_specs=[ANY] * nw, out_specs=[ANY] * nw,
        out_shape=[jax.ShapeDtypeStruct(a.shape[1:] if other_half else a.shape, a.dtype) for a in srcs],
        scratch_shapes=[pltpu.SemaphoreType.DMA((nw,)), pltpu.SemaphoreType.DMA((nw,))],
    )(*srcs)


def _add_pair(name, g, recv, c):
    _, nchip, r, d = g.shape
    tr = _tile(r, 512, 16)

    def body(c_ref, g_ref, r_ref, o_ref):
        o_ref[...] = (g_ref[...].astype(f32) + r_ref[...].astype(f32)).astype(o_ref.dtype)

    gs = pltpu.PrefetchScalarGridSpec(
        num_scalar_prefetch=1, grid=(nchip, r // tr),
        in_specs=[pl.BlockSpec((None, None, tr, d), lambda t, i, cr: (cr[0], t, i, 0)),
                  pl.BlockSpec((None, tr, d), lambda t, i, cr: (t, i, 0))],
        out_specs=pl.BlockSpec((None, tr, d), lambda t, i, cr: (t, i, 0)))
    return pl.pallas_call(body, name=name, grid_spec=gs, out_shape=jax.ShapeDtypeStruct((nchip, r, d), bf16),
                          compiler_params=_cp("parallel", "parallel"))(c.reshape(1).astype(jnp.int32), g, recv)


def _add_chips(name, p, recv, s_chip):
    _, r, d = p.shape
    tr = _tile(r, 512, 16)

    def body(s_ref, p_ref, r_ref, o_ref):
        o_ref[...] = ((p_ref[...].astype(f32) + r_ref[0].astype(f32)) + r_ref[1].astype(f32)) + r_ref[2].astype(f32)

    gs = pltpu.PrefetchScalarGridSpec(
        num_scalar_prefetch=1, grid=(r // tr,),
        in_specs=[pl.BlockSpec((None, tr, d), lambda i, sr: (sr[0], i, 0)),
                  pl.BlockSpec((3, tr, d), lambda i, sr: (0, i, 0))],
        out_specs=pl.BlockSpec((tr, d), lambda i, sr: (i, 0)))
    return pl.pallas_call(body, name=name, grid_spec=gs, out_shape=jax.ShapeDtypeStruct((r, d), f32),
                          compiler_params=_cp("parallel"))(s_chip.reshape(1).astype(jnp.int32), p, recv)


def _rs_pair_sums(tag, grads, c):
    from_sib = _send_to_sibling("rs_to_sibling_" + tag, grads, True)
    return [_add_pair("rs_add_pair_%s%d" % (tag, n), g, r, c) for n, (g, r) in enumerate(zip(grads, from_sib))]


def _rs_chip_sums(tag, partial, from_chips, s_chip):
    return [_add_chips("rs_add_chips_%s%d" % (tag, n), p, r, s_chip) for n, (p, r) in enumerate(zip(partial, from_chips))]


def _rs_both_halves(mine, c):
    theirs = _send_to_sibling("rs_exchange_halves", mine, False)
    return [jnp.where(c == 0, jnp.stack([a, b]), jnp.stack([b, a])) for a, b in zip(mine, theirs)]


def _sum_devices(v):
    n = v.shape[2]

    def body(v_ref, o_ref):
        t = v_ref[0]
        for k in range(1, 8):
            t = t + v_ref[k]
        o_ref[...] = t

    return pl.pallas_call(body, name="sum_devices", out_shape=jax.ShapeDtypeStruct((1, n), f32))(v)


def _silu_rows(v):
    def body(v_ref, o_ref):
        t = v_ref[...]
        o_ref[...] = t * _sigmoid(t)

    return pl.pallas_call(body, name="silu_cond", out_shape=jax.ShapeDtypeStruct(v.shape, f32))(v)


ADAMW_BLOCK_ELEMS = 600 * 1024


def _adamw(name, w, g, m, v):
    r, cdim = w.shape
    tr = _tile(r, max(8, min(256, (ADAMW_BLOCK_ELEMS // cdim) // 8 * 8)), 8)
    c1 = 1.0 - ADAM_B1 ** ADAM_STEP
    c2 = 1.0 - ADAM_B2 ** ADAM_STEP

    def body(w_ref, g_ref, m_ref, v_ref, d_ref, mo_ref, vo_ref):
        gv = g_ref[...]
        mn = ADAM_B1 * m_ref[...] + (1.0 - ADAM_B1) * gv
        vn = ADAM_B2 * v_ref[...] + (1.0 - ADAM_B2) * (gv * gv)
        d_ref[...] = -ADAM_LR * ((mn / c1) / (jnp.sqrt(vn / c2) + ADAM_EPS) + ADAM_WD * w_ref[...])
        mo_ref[...] = mn
        vo_ref[...] = vn

    blk = pl.BlockSpec((tr, cdim), lambda i: (i, 0))
    return pl.pallas_call(body, name=name, grid=(r // tr,), in_specs=[blk] * 4, out_specs=[blk] * 3,
                          out_shape=[jax.ShapeDtypeStruct((r, cdim), f32)] * 3, compiler_params=_cp("parallel"))(w, g, m, v)


def _ep_bias(accs, ex):
    return (accs[0] + ex[0],)


def kernel(x, c, ada_w, ada_b, norm_g, ffn_w_gate, ffn_w_up, ffn_w_down, w_in, w_out, fox_f_bias, fox_out_norm, gdn_conv, gdn_A_log, gdn_dt_bias, gdn_out_norm, final_norm, loss_target, m_ada_w, m_ada_b, m_norm_g, m_ffn_w_gate, m_ffn_w_up, m_ffn_w_down, m_w_in, m_w_out, m_fox_f_bias, m_fox_out_norm, m_gdn_conv, m_gdn_A_log, m_gdn_dt_bias, m_gdn_out_norm, m_final_norm, v_ada_w, v_ada_b, v_norm_g, v_ffn_w_gate, v_ffn_w_up, v_ffn_w_down, v_w_in, v_w_out, v_fox_f_bias, v_fox_out_norm, v_gdn_conv, v_gdn_A_log, v_gdn_dt_bias, v_gdn_out_norm, v_final_norm):
    ix, iy, ic = lax.axis_index("x"), lax.axis_index("y"), lax.axis_index("c")
    s_chip = 2 * ix + iy
    me = 4 * ix + 2 * iy + ic
    _, s, d = x.shape
    nh = d // (2 * HEAD_DIM)
    fw = nh * HEAD_DIM
    ncol = ada_w.shape[2]
    dg_sh = norm_g.shape[2]
    cv_sh = gdn_conv.shape[2]
    ff_sh = ffn_w_gate.shape[3]
    in_sh = w_in.shape[2]
    in_pad = -(-in_sh // 32) * 32
    out_sh = w_out.shape[1]
    per_chip = lambda a, t: a[2 * t]

    pack0 = jnp.concatenate([_silu_rows(c), norm_g[0].reshape(1, 3 * dg_sh), gdn_conv[0].reshape(1, CONV_W * cv_sh)], axis=1)
    got0 = _allgather8("gather_cond", pack0)
    cond_all = got0[:, 0, :d]
    norm_g_full = jnp.concatenate([per_chip(got0, t)[0, d:d + 3 * dg_sh].reshape(3, dg_sh) for t in range(4)], axis=1)
    conv_full = jnp.concatenate([per_chip(got0, t)[0, d + 3 * dg_sh:].reshape(CONV_W, cv_sh) for t in range(4)], axis=1)

    ada_b_sh = lax.dynamic_slice_in_dim(ada_b, s_chip * ncol, ncol, axis=1)
    (mod_sh,) = _mm("ada_mod", [[(cond_all, ada_w[0])]], "nn", (8, 512, 2048), _ep_bias, (f32,), extras=((ada_b_sh, "n"),))
    mod_all = _allgather8("gather_mod", mod_sh)
    mod = jnp.concatenate([lax.dynamic_index_in_dim(per_chip(mod_all, t), me, axis=0, keepdims=True) for t in range(4)], axis=1)
    mods = [mod[:, i * d:(i + 1) * d] for i in range(9)]

    halved = lambda a: a.reshape(2, a.shape[0] // 2, d)
    ffn_halves = [[halved(ffn_w_gate[0, j].T.astype(bf16)), halved(ffn_w_up[0, j].T.astype(bf16)),
                   halved(ffn_w_down[0, j].astype(bf16))] for j in range(2)]
    mix_halves = [halved(jnp.pad(w_in[0].T.astype(bf16), ((0, in_pad - in_sh), (0, 0)))), halved(w_out[0].astype(bf16))]
    with_own = lambda got, hs: [lax.dynamic_update_slice(g, h[:, None], (0, s_chip, 0, 0)) for g, h in zip(got, hs)]
    ffn_full = lambda got, hs: tuple(g.reshape(4 * ff_sh, d) for g in with_own(got, hs))
    ffn_blocks = lambda grads: [g.reshape(2, 4, ff_sh // 2, d) for g in grads]
    ffn1_w = ffn_full(_run_comm("gather_ffn1", _gather_spec(ffn_halves[0][:2])), ffn_halves[0][:2]) + (None,)
    prm = _head_params(fox_f_bias, gdn_dt_bias, gdn_A_log, nh)

    class Hooks:
        def gather_mix_spec(self):
            return _gather_spec(mix_halves + ffn_halves[0][2:])

        def ffn1_wd(self, got):
            return ffn_full(got[2:], ffn_halves[0][2:])[0]

        def gather_ffn2_spec(self):
            return _gather_spec(ffn_halves[1])

        def gathered(self, got_mix, got_ffn2):
            g_win, g_wo = with_own(got_mix[:2], mix_halves)
            w_in_t = jnp.swapaxes(g_win, 0, 1).reshape(4, in_pad, d)[:, :in_sh].reshape(4 * in_sh, d)
            return _build_w_cat_t(w_in_t, nh), jnp.swapaxes(g_wo, 0, 1).reshape(4 * out_sh, d), ffn_full(got_ffn2, ffn_halves[1])

        def rs_ffn2_spec(self, dffn2):
            self.ffn2_pairs = _rs_pair_sums("ffn2", ffn_blocks(dffn2), ic)
            return _to_chips_spec(self.ffn2_pairs)

        def rs_ffn1_spec(self, dffn1):
            self.ffn1_pairs = _rs_pair_sums("ffn1", ffn_blocks(dffn1), ic)
            return _to_chips_spec(self.ffn1_pairs)

        def rs_mix_spec(self, dw_cat_t, dw_out):
            dw_in_t = jnp.pad(_split_dw_cat_t(dw_cat_t, nh).reshape(4, in_sh, d), ((0, 0), (0, in_pad - in_sh), (0, 0)))
            grads = [jnp.swapaxes(dw_in_t.reshape(4, 2, in_pad // 2, d), 0, 1),
                     jnp.swapaxes(dw_out.reshape(4, 2, out_sh // 2, d), 0, 1)]
            self.mix_pairs = _rs_pair_sums("mix", grads, ic)
            return _to_chips_spec(self.mix_pairs)

    hooks = Hooks()

    dx0, big, small = _local_step(x[0], loss_target[0], mods, norm_g_full, final_norm.reshape(1, d), ffn1_w, None, prm,
                                  fox_out_norm, gdn_out_norm, conv_full, nh, hooks)

    pack1 = jnp.concatenate([small["loss"], small["norm_g"].reshape(1, 3 * d), small["final_norm"], small["fox_wn"],
                             small["gdn_wn"], small["pg"][0:1], small["pg"][1:2], small["conv"].reshape(1, CONV_W * 3 * fw),
                             small["mod"]], axis=1)
    got1 = _allgather8("gather_small_grads", pack1)
    tot = _sum_devices(got1)
    o = [0]

    def take(n):
        o[0] += n
        return tot[:, o[0] - n:o[0]]

    loss = take(LANES)[0, 0]
    g_norm_g = lax.dynamic_slice_in_dim(take(3 * d).reshape(3, d), s_chip * dg_sh, dg_sh, axis=1)[None]
    g_final = take(d).reshape(d)
    g_fox_wn = take(HEAD_DIM)
    g_gdn_wn = take(HEAD_DIM)
    pg0, pg1 = take(LANES), take(LANES)
    g_fbias, g_dtb, g_alog = pg0[:, 0:nh], pg0[:, nh:2 * nh], pg1[:, nh:2 * nh]
    g_conv = lax.dynamic_slice_in_dim(take(CONV_W * 3 * fw).reshape(CONV_W, 3 * fw), s_chip * cv_sh, cv_sh, axis=1)[None]
    g_ada_b = take(9 * d)
    dmod_all = got1[:, 0, o[0] - 9 * d:o[0]]
    dmod_sh = lax.dynamic_slice_in_dim(dmod_all, s_chip * ncol, ncol, axis=1)
    (g_ada_w,) = _mm("ada_dw", [[(cond_all, dmod_sh)]], "tn", (2048, 512, 8), _ep_plain, (f32,))

    ffn1_mine = _rs_chip_sums("ffn1", hooks.ffn1_pairs, big["got_ffn1"], s_chip)
    ffn2_mine = _rs_chip_sums("ffn2", hooks.ffn2_pairs, big["got_ffn2"], s_chip)
    mix_mine = _rs_chip_sums("mix", hooks.mix_pairs, big["got_mix"], s_chip)
    r1g, r1u, r1d, r2g, r2u, r2d, r_win, r_wo = _rs_both_halves(ffn1_mine + ffn2_mine + mix_mine, ic)
    rows = lambda r: r.reshape(-1, d)
    g_ffn_gate = jnp.stack([rows(r1g).T, rows(r2g).T])[None]
    g_ffn_up = jnp.stack([rows(r1u).T, rows(r2u).T])[None]
    g_ffn_down = jnp.stack([rows(r1d), rows(r2d)])[None]
    g_w_in = rows(r_win)[:in_sh].T[None]
    g_w_out = rows(r_wo)[None]

    def upd(name, w, g, m, v):
        shp = w.shape
        two = lambda a: a.reshape(-1, shp[-1])
        return tuple(t.reshape(shp) for t in _adamw(name, two(w), two(g), two(m), two(v)))

    big_upd = [upd("adamw_ada_w", ada_w, g_ada_w[None], m_ada_w, v_ada_w),
               upd("adamw_ffn_gate", ffn_w_gate, g_ffn_gate, m_ffn_w_gate, v_ffn_w_gate),
               upd("adamw_ffn_up", ffn_w_up, g_ffn_up, m_ffn_w_up, v_ffn_w_up),
               upd("adamw_ffn_down", ffn_w_down, g_ffn_down, m_ffn_w_down, v_ffn_w_down),
               upd("adamw_w_in", w_in, g_w_in, m_w_in, v_w_in),
               upd("adamw_w_out", w_out, g_w_out, m_w_out, v_w_out)]
    small_w = [ada_b, norm_g, fox_f_bias, fox_out_norm, gdn_conv, gdn_A_log, gdn_dt_bias, gdn_out_norm, final_norm]
    small_g = [g_ada_b, g_norm_g, g_fbias, g_fox_wn, g_conv, g_alog, g_dtb, g_gdn_wn, g_final]
    small_m = [m_ada_b, m_norm_g, m_fox_f_bias, m_fox_out_norm, m_gdn_conv, m_gdn_A_log, m_gdn_dt_bias, m_gdn_out_norm, m_final_norm]
    small_v = [v_ada_b, v_norm_g, v_fox_f_bias, v_fox_out_norm, v_gdn_conv, v_gdn_A_log, v_gdn_dt_bias, v_gdn_out_norm, v_final_norm]
    sizes = [a.size for a in small_w]
    npad = -sum(sizes) % LANES
    flat = lambda arrs, fill: jnp.concatenate([a.reshape(1, -1) for a in arrs] + [jnp.full((1, npad), fill, f32)], axis=1)
    sd, sm_, sv = _adamw("adamw_small", flat(small_w, 0.0), flat(small_g, 0.0), flat(small_m, 0.0), flat(small_v, 1.0))

    def unflat(t):
        out, off = [], 0
        for a, n in zip(small_w, sizes):
            out.append(t[0, off:off + n].reshape(a.shape))
            off += n
        return out

    small_g = [g.reshape(a.shape) for g, a in zip(small_g, small_w)]
    s_d, s_m, s_v = unflat(sd), unflat(sm_), unflat(sv)
    def order(bigs, smalls):
        return [bigs[0], smalls[0], smalls[1], bigs[1], bigs[2], bigs[3], bigs[4], bigs[5]] + list(smalls[2:])

    grads_out = order([g_ada_w[None], g_ffn_gate, g_ffn_up, g_ffn_down, g_w_in, g_w_out], small_g)
    deltas = order([u[0] for u in big_upd], s_d)
    new_m = order([u[1] for u in big_upd], s_m)
    new_v = order([u[2] for u in big_upd], s_v)
    return (loss, dx0[None], *grads_out, *deltas, *new_m, *new_v)
```

```python
import functools
import math

import jax
import jax.numpy as jnp
from jax import lax
from jax.experimental import pallas as pl
from jax.experimental.pallas import tpu as pltpu

f32 = jnp.float32
bf16 = jnp.bfloat16
HI = lax.Precision.HIGHEST
MESH = pl.DeviceIdType.MESH

EPS = 1e-6
HEAD_DIM = 128
LANES = 128
GDN_CHUNK = 64
CONV_W = 4
MACARON_W = 0.5
ADAM_LR, ADAM_B1, ADAM_B2, ADAM_EPS, ADAM_WD, ADAM_STEP = 0.001, 0.9, 0.999, 1e-08, 0.01, 10
VMEM_LIMIT_V7X = 56 * 1024 * 1024
NEG = -1e30

NN = (((1,), (0,)), ((), ()))
NT = (((1,), (1,)), ((), ()))
TN = (((0,), (0,)), ((), ()))


def _cp(*sem):
    return pltpu.CompilerParams(dimension_semantics=sem, vmem_limit_bytes=VMEM_LIMIT_V7X)


def _dotb(a, b, dn=NN):
    return lax.dot_general(a.astype(bf16), b.astype(bf16), dn, preferred_element_type=f32)


def _doth(a, b, dn=NN):
    return lax.dot_general(a.astype(f32), b.astype(f32), dn, precision=HI, preferred_element_type=f32)


def _dotm(a, b, dn=NN):
    return lax.dot_general(a.astype(f32), b.astype(f32), dn, precision=lax.Precision.HIGH, preferred_element_type=f32)


def _sigmoid(x):
    return 1.0 / (1.0 + jnp.exp(-x))


def _softplus(x):
    return jnp.maximum(x, 0.0) + jnp.log(1.0 + jnp.exp(-jnp.abs(x)))


def _lane_col(blk, lane_idx):
    lane = lax.broadcasted_iota(jnp.int32, blk.shape, 1)
    return jnp.sum(jnp.where(lane == lane_idx, blk, 0.0), axis=1, keepdims=True)


def _tile(n, pref, mult=LANES):
    if n <= pref:
        return n
    t = (pref // mult) * mult
    while t >= mult:
        if n % t == 0:
            return t
        t -= mult
    return n


MM_EPILOGUE_CHUNKS = 2


def _mm(name, groups, mode, tiles, epilogue, out_dtypes, extras=(), comm=None):
    a0, b0 = groups[0][0]
    if mode == "nn":
        (m, k), n = a0.shape, b0.shape[1]
    elif mode == "nt":
        (m, k), n = a0.shape, b0.shape[0]
    else:
        (k, m), n = a0.shape, b0.shape[1]
    tm, tn, tk = _tile(m, tiles[0]), _tile(n, tiles[1]), _tile(k, tiles[2])
    nk = k // tk
    assert m % tm == 0 and n % tn == 0 and k % tk == 0, (name, m, n, k, tm, tn, tk)
    if mode == "nn":
        a_spec = pl.BlockSpec((tm, tk), lambda i, j, kk: (i, kk))
        b_spec = pl.BlockSpec((tk, tn), lambda i, j, kk: (kk, j))
        dn = NN
    elif mode == "nt":
        a_spec = pl.BlockSpec((tm, tk), lambda i, j, kk: (i, kk))
        b_spec = pl.BlockSpec((tn, tk), lambda i, j, kk: (j, kk))
        dn = NT
    else:
        a_spec = pl.BlockSpec((tk, tm), lambda i, j, kk: (kk, i))
        b_spec = pl.BlockSpec((tk, tn), lambda i, j, kk: (kk, j))
        dn = TN
    npairs = sum(len(g) for g in groups)
    nacc, nex, nout = len(groups), len(extras), len(out_dtypes)
    in_specs, args = [], []
    for g in groups:
        for a, b in g:
            in_specs += [a_spec, b_spec]
            args += [a, b]
    for arr, kind in extras:
        if kind == "mn":
            in_specs.append(pl.BlockSpec((tm, tn), lambda i, j, kk: (i, j)))
        else:
            in_specs.append(pl.BlockSpec((1, tn), lambda i, j, kk: (0, j)))
        args.append(arr)
    nci = len(comm.ins) if comm else 0
    nco = len(comm.out_shapes) if comm else 0
    grid = (m // tm, n // tn, nk)

    def body(*refs):
        ins = refs[: 2 * npairs]
        ex = refs[2 * npairs: 2 * npairs + nex]
        c_ins = refs[2 * npairs + nex: 2 * npairs + nex + nci]
        o0 = 2 * npairs + nex + nci
        outs = refs[o0: o0 + nout]
        c_outs = refs[o0 + nout: o0 + nout + nco]
        accs = refs[o0 + nout + nco: o0 + nout + nco + nacc]
        c_sems = refs[o0 + nout + nco + nacc:]
        kk = pl.program_id(2)
        if comm:
            step = (pl.program_id(0) * grid[1] + pl.program_id(1)) * nk + kk

            @pl.when(step == 0)
            def _():
                comm.run("start", c_ins, c_outs, c_sems)

        if nk == 1 and mode != "tn":
            nsub = MM_EPILOGUE_CHUNKS if tn % (MM_EPILOGUE_CHUNKS * LANES) == 0 else 1
            w = tn // nsub
            for cidx in range(nsub):
                lo = cidx * w
                sums, p = [], 0
                for g in groups:
                    t = None
                    for _ in g:
                        b = ins[2 * p + 1][lo:lo + w, :] if mode == "nt" else ins[2 * p + 1][:, lo:lo + w]
                        d = _dotb(ins[2 * p][...], b, dn)
                        t = d if t is None else t + d
                        p += 1
                    sums.append(t)
                res = epilogue(sums, [e[:, lo:lo + w] for e in ex])
                for o, r in zip(outs, res):
                    o[:, lo:lo + w] = r.astype(o.dtype)
        else:
            @pl.when(kk == 0)
            def _():
                for acc in accs:
                    acc[...] = jnp.zeros_like(acc)

            p = 0
            for gi, g in enumerate(groups):
                t = None
                for _ in g:
                    d = _dotb(ins[2 * p][...], ins[2 * p + 1][...], dn)
                    t = d if t is None else t + d
                    p += 1
                accs[gi][...] += t

            @pl.when(kk == nk - 1)
            def _():
                res = epilogue([acc[...] for acc in accs], [e[...] for e in ex])
                for o, r in zip(outs, res):
                    o[...] = r.astype(o.dtype)

        if comm:
            @pl.when(step == grid[0] * grid[1] * nk - 1)
            def _():
                comm.run("finish", c_ins, c_outs, c_sems)

    any_spec = pl.BlockSpec(memory_space=pl.ANY)
    res = pl.pallas_call(
        body, name=name, grid=grid,
        in_specs=in_specs + [any_spec] * nci,
        out_specs=[pl.BlockSpec((tm, tn), lambda i, j, kk: (i, j)) for _ in out_dtypes] + [any_spec] * nco,
        out_shape=[jax.ShapeDtypeStruct((m, n), dt) for dt in out_dtypes] + (list(comm.out_shapes) if comm else []),
        scratch_shapes=[pltpu.VMEM((tm, tn), f32) for _ in range(nacc)] + (comm.sems() if comm else []),
        compiler_params=_cp(*(("arbitrary",) * 3 if comm else ("parallel", "parallel", "arbitrary"))),
    )(*args, *(comm.ins if comm else []))
    return (res[:nout], res[nout:]) if comm else res


def _ep_plain(accs, ex):
    return (accs[0],)


def _ep_colscale(accs, ex):
    return (accs[0] * ex[0],)


def _ep_swiglu(accs, ex):
    gate, up = accs
    act = gate * _sigmoid(gate) * up
    return gate, up, act


def _ep_residual(accs, ex):
    x, gs = ex
    y = accs[0]
    return x + gs * y, y


def _ep_swiglu_bwd(accs, ex):
    gate, up = ex[0].astype(f32), ex[1].astype(f32)
    dact = accs[0]
    sg = _sigmoid(gate)
    silu = gate * sg
    act = silu * up
    dup = dact * silu
    dgate = dact * up * sg * (1.0 + gate * (1.0 - sg))
    return act, dgate, dup


def _row_tile(s):
    return _tile(s, 256, 8)


def _head_row_tile(s):
    return _tile(s, 1024, 8)


def _ada_in(name, x, g, shift, scale):
    s, d = x.shape
    tm = _row_tile(s)

    def body(x_ref, g_ref, sh_ref, sc_ref, h_ref):
        xv = x_ref[...]
        r = lax.rsqrt(jnp.mean(xv * xv, axis=-1, keepdims=True) + EPS)
        h_ref[...] = (xv * r * g_ref[...] * (1.0 + sc_ref[...]) + sh_ref[...]).astype(h_ref.dtype)

    row = pl.BlockSpec((1, d), lambda i: (0, 0))
    blk = pl.BlockSpec((tm, d), lambda i: (i, 0))
    return pl.pallas_call(body, name=name, grid=(s // tm,), in_specs=[blk, row, row, row], out_specs=blk,
                          out_shape=jax.ShapeDtypeStruct((s, d), bf16), compiler_params=_cp("parallel"))(x, g, shift, scale)


def _ada_bwd(name, x, g, scale, dh, dres):
    s, d = x.shape
    tm = _row_tile(s)

    def body(x_ref, g_ref, sc_ref, dh_ref, dres_ref, dx_ref, dsh_ref, a_ref):
        i = pl.program_id(0)

        @pl.when(i == 0)
        def _():
            dsh_ref[...] = jnp.zeros_like(dsh_ref)
            a_ref[...] = jnp.zeros_like(a_ref)

        xv = x_ref[...]
        dhv = dh_ref[...].astype(f32)
        r = lax.rsqrt(jnp.mean(xv * xv, axis=-1, keepdims=True) + EPS)
        n = xv * r
        dn = dhv * (g_ref[...] * (1.0 + sc_ref[...]))
        dx_ref[...] = dres_ref[...] + r * (dn - n * jnp.mean(dn * n, axis=-1, keepdims=True))
        dsh_ref[...] += jnp.sum(dhv, axis=0, keepdims=True)
        a_ref[...] += jnp.sum(dhv * n, axis=0, keepdims=True)

    row = pl.BlockSpec((1, d), lambda i: (0, 0))
    blk = pl.BlockSpec((tm, d), lambda i: (i, 0))
    return pl.pallas_call(
        body, name=name, grid=(s // tm,), in_specs=[blk, row, row, blk, blk], out_specs=[blk, row, row],
        out_shape=[jax.ShapeDtypeStruct((s, d), f32), jax.ShapeDtypeStruct((1, d), f32), jax.ShapeDtypeStruct((1, d), f32)],
        compiler_params=_cp("arbitrary"))(x, g, scale, dh, dres)


def _gate_bwd(name, dx, y, gs):
    s, d = dx.shape
    tm = _row_tile(s)

    def body(dx_ref, y_ref, gs_ref, dy_ref, dgs_ref):
        i = pl.program_id(0)

        @pl.when(i == 0)
        def _():
            dgs_ref[...] = jnp.zeros_like(dgs_ref)

        dxv = dx_ref[...]
        dy_ref[...] = (dxv * gs_ref[...]).astype(dy_ref.dtype)
        dgs_ref[...] += jnp.sum(dxv * y_ref[...].astype(f32), axis=0, keepdims=True)

    row = pl.BlockSpec((1, d), lambda i: (0, 0))
    blk = pl.BlockSpec((tm, d), lambda i: (i, 0))
    return pl.pallas_call(
        body, name=name, grid=(s // tm,), in_specs=[blk, blk, row], out_specs=[blk, row],
        out_shape=[jax.ShapeDtypeStruct((s, d), bf16), jax.ShapeDtypeStruct((1, d), f32)],
        compiler_params=_cp("arbitrary"))(dx, y, gs)


def _final_loss(x, fg, target):
    s, d = x.shape
    tm = _row_tile(s)

    def body(x_ref, g_ref, t_ref, loss_ref, dx_ref, dg_ref):
        i = pl.program_id(0)

        @pl.when(i == 0)
        def _():
            loss_ref[...] = jnp.zeros_like(loss_ref)
            dg_ref[...] = jnp.zeros_like(dg_ref)

        xv = x_ref[...]
        gv = g_ref[...]
        r = lax.rsqrt(jnp.mean(xv * xv, axis=-1, keepdims=True) + EPS)
        n = xv * r
        e = n * gv - t_ref[...]
        per_tok = jnp.mean(e * e, axis=-1, keepdims=True)
        loss_ref[...] += 0.5 * jnp.sum(per_tok, axis=0, keepdims=True) * jnp.ones((1, LANES), f32)
        dy = e * (1.0 / d)
        dg_ref[...] += jnp.sum(dy * n, axis=0, keepdims=True)
        dn = dy * gv
        dx_ref[...] = r * (dn - n * jnp.mean(dn * n, axis=-1, keepdims=True))

    row = pl.BlockSpec((1, d), lambda i: (0, 0))
    blk = pl.BlockSpec((tm, d), lambda i: (i, 0))
    return pl.pallas_call(
        body, name="final_loss", grid=(s // tm,), in_specs=[blk, row, blk],
        out_specs=[pl.BlockSpec((1, LANES), lambda i: (0, 0)), blk, row],
        out_shape=[jax.ShapeDtypeStruct((1, LANES), f32), jax.ShapeDtypeStruct((s, d), f32), jax.ShapeDtypeStruct((1, d), f32)],
        compiler_params=_cp("arbitrary"))(x, fg, target)


def _small_fwd(ps, prm, nh):
    s = ps.shape[0]
    tb = LANES

    def body(ps_ref, prm_ref, sm_ref, cum_ref, carry):
        i = pl.program_id(0)

        @pl.when(i == 0)
        def _():
            carry[...] = jnp.zeros_like(carry)

        x = ps_ref[...]
        lane = lax.broadcasted_iota(jnp.int32, x.shape, 1)
        fb, dtb, alog = prm_ref[0:1, :], prm_ref[1:2, :], prm_ref[2:3, :]
        logf = -_softplus(-(x + fb))
        glog = -jnp.exp(alog) * _softplus(x + dtb)
        beta = _sigmoid(x)
        sm = jnp.where(lane < nh, logf, jnp.where(lane < 2 * nh, glog, jnp.where(lane < 3 * nh, beta, 0.0)))
        sm_ref[...] = sm
        r = lax.broadcasted_iota(jnp.int32, (tb, tb), 0)
        c = lax.broadcasted_iota(jnp.int32, (tb, tb), 1)
        tril = (c <= r).astype(f32)
        cs = _doth(tril, sm) + carry[...]
        cum_ref[...] = cs
        carry[...] = cs[tb - 1:tb, :]

    blk = pl.BlockSpec((tb, LANES), lambda i: (i, 0))
    return pl.pallas_call(
        body, name="small_fwd", grid=(s // tb,),
        in_specs=[blk, pl.BlockSpec((8, LANES), lambda i: (0, 0))],
        out_specs=[blk, blk],
        out_shape=[jax.ShapeDtypeStruct((s, LANES), f32), jax.ShapeDtypeStruct((s, LANES), f32)],
        scratch_shapes=[pltpu.VMEM((1, LANES), f32)],
        compiler_params=_cp("arbitrary"))(ps, prm)


def _small_bwd(ps, prm, dsm, dcum, nh):
    s = ps.shape[0]
    tb = LANES
    nb = s // tb

    def body(ps_ref, prm_ref, dsm_ref, dct_ref, dps_ref, pg_ref, carry):
        i = pl.program_id(0)

        @pl.when(i == 0)
        def _():
            carry[...] = jnp.zeros_like(carry)
            pg_ref[...] = jnp.zeros_like(pg_ref)

        x = ps_ref[...]
        dsm = dsm_ref[...]
        lane = lax.broadcasted_iota(jnp.int32, x.shape, 1)
        fb, dtb, alog = prm_ref[0:1, :], prm_ref[1:2, :], prm_ref[2:3, :]
        r = lax.broadcasted_iota(jnp.int32, (tb, tb), 0)
        c = lax.broadcasted_iota(jnp.int32, (tb, tb), 1)
        triu = (c >= r).astype(f32)
        dlogf = _doth(triu, dct_ref[...]) + carry[...]
        carry[...] = dlogf[0:1, :]
        d_f = dlogf * _sigmoid(-(x + fb))
        nega = -jnp.exp(alog)
        xa = x + dtb
        glog = nega * _softplus(xa)
        d_a = dsm * nega * _sigmoid(xa)
        beta = _sigmoid(x)
        d_b = dsm * beta * (1.0 - beta)
        dps = jnp.where(lane < nh, d_f, jnp.where(lane < 2 * nh, d_a, jnp.where(lane < 3 * nh, d_b, 0.0)))
        dps_ref[...] = dps.astype(dps_ref.dtype)
        row0 = jnp.sum(dps, axis=0, keepdims=True)
        row1 = jnp.sum(jnp.where((lane >= nh) & (lane < 2 * nh), dsm * glog, 0.0), axis=0, keepdims=True)
        sub = lax.broadcasted_iota(jnp.int32, (8, LANES), 0)
        pg_ref[...] += jnp.where(sub == 0, row0, jnp.where(sub == 1, row1, 0.0))

    rev = pl.BlockSpec((tb, LANES), lambda i: (nb - 1 - i, 0))
    fix = pl.BlockSpec((8, LANES), lambda i: (0, 0))
    return pl.pallas_call(
        body, name="small_bwd", grid=(nb,),
        in_specs=[rev, fix, rev, rev],
        out_specs=[rev, fix],
        out_shape=[jax.ShapeDtypeStruct((s, LANES), bf16), jax.ShapeDtypeStruct((8, LANES), f32)],
        scratch_shapes=[pltpu.VMEM((1, LANES), f32)],
        compiler_params=_cp("arbitrary"))(ps, prm, dsm, dcum)


LOG2E = 1.4426950408889634
LN2 = 0.6931471805599453
AUG = 2 * HEAD_DIM
FOX_Q_SCALE = LOG2E / math.sqrt(HEAD_DIM)
FOX_KEY_GROUP = 4


def _split3(col):
    hi = col.astype(bf16).astype(f32)
    r1 = col - hi
    mid = r1.astype(bf16).astype(f32)
    lo = (r1 - mid).astype(bf16).astype(f32)
    return hi, mid, lo


def _aug_block(rows, terms, terms_at, ones_at=None):
    lane = lax.broadcasted_iota(jnp.int32, (rows, LANES), 1)
    blk = jnp.zeros((rows, LANES), f32) if ones_at is None else jnp.where((lane >= ones_at) & (lane < ones_at + 3), 1.0, 0.0)
    for i, t in enumerate(terms):
        blk = jnp.where(lane == terms_at + i, t, blk)
    return blk


def _fox_aug(qkv, cum, nh):
    s = qkv.shape[0]
    tm = _head_row_tile(s)

    def body(q_ref, k_ref, v_ref, cum_ref, qa_ref, ka_ref, va_ref):
        h = pl.program_id(1)
        c2 = _lane_col(cum_ref[...], h) * LOG2E
        hi, mid, lo = _split3(c2)
        qa_ref[:, :HEAD_DIM] = q_ref[...]
        qa_ref[:, HEAD_DIM:] = _aug_block(tm, (hi, mid, lo), 0, 3).astype(bf16)
        ka_ref[:, :HEAD_DIM] = k_ref[...]
        ka_ref[:, HEAD_DIM:] = _aug_block(tm, (-hi, -mid, -lo), 3, 0).astype(bf16)
        va_ref[:, :HEAD_DIM] = v_ref[...]
        va_ref[:, HEAD_DIM:] = _aug_block(tm, (), 0, 0).astype(bf16)

    ab = pl.BlockSpec((tm, AUG), lambda i, h: (i, h))
    return pl.pallas_call(
        body, name="fox_aug", grid=(s // tm, nh),
        in_specs=[pl.BlockSpec((tm, HEAD_DIM), lambda i, h: (i, h)), pl.BlockSpec((tm, HEAD_DIM), lambda i, h: (i, nh + h)),
                  pl.BlockSpec((tm, HEAD_DIM), lambda i, h: (i, 2 * nh + h)), pl.BlockSpec((tm, LANES), lambda i, h: (i, 0))],
        out_specs=[ab, ab, ab], out_shape=[jax.ShapeDtypeStruct((s, nh * AUG), bf16)] * 3,
        compiler_params=_cp("parallel", "parallel"))(qkv, qkv, qkv, cum)


def _fox_fwd(qa, ka, qkv, wn, nh, tq):
    s = qa.shape[0]
    fw = nh * HEAD_DIM
    group = FOX_KEY_GROUP if s // tq >= 2 * FOX_KEY_GROUP else 2

    def body(qa_ref, ka_ref, v_ref, wn_ref, o_ref, on_ref, lse_ref):
        i = pl.program_id(1)
        q = qa_ref[...]

        def logits_t(j, rows):
            return _dotb(ka_ref[pl.ds(pl.multiple_of(j * tq, tq), rows), :], q, NT)

        def pv_t(j, p_t):
            return _dotb(v_ref[pl.ds(pl.multiple_of(j * tq, tq), p_t.shape[0]), :], p_t, TN)

        key = lax.broadcasted_iota(jnp.int32, (tq, tq), 0)
        qry = lax.broadcasted_iota(jnp.int32, (tq, tq), 1)
        js = [i] + [jnp.maximum(i - n, 0) for n in range(1, group)]
        ts = [jnp.where(key <= qry, logits_t(i, tq), NEG)]
        ts += [jnp.where(lax.rem(i, group) >= n, logits_t(js[n], tq), NEG) for n in range(1, group)]
        m = functools.reduce(jnp.maximum, [jnp.max(t, axis=0, keepdims=True) for t in ts])
        ps = [jnp.exp2(t - m) for t in ts]
        l = functools.reduce(jnp.add, [jnp.sum(p, axis=0, keepdims=True) for p in ps])
        acc = functools.reduce(jnp.add, [pv_t(j, p) for j, p in zip(js, ps)])

        def step(jj, carry):
            m, l, acc = carry
            t = logits_t(group * jj, group * tq)
            mn = jnp.maximum(m, jnp.max(t, axis=0, keepdims=True))
            p_t = jnp.exp2(t - mn)
            alpha = jnp.exp2(m - mn)
            return mn, alpha * l + jnp.sum(p_t, axis=0, keepdims=True), alpha * acc + pv_t(group * jj, p_t)

        m, l, acc = lax.fori_loop(0, i // group, step, (m, l, acc))
        o = (acc / l).T
        o_ref[...] = o
        sub = lax.broadcasted_iota(jnp.int32, (LANES, tq), 0)
        lse_ref[0] = jnp.where(sub == 0, m + jnp.log2(l), 0.0).T
        r = lax.rsqrt(jnp.mean(o * o, axis=-1, keepdims=True) + EPS)
        on_ref[...] = (o * r * wn_ref[...]).astype(on_ref.dtype)

    hb = pl.BlockSpec((tq, HEAD_DIM), lambda h, i: (i, h))
    return pl.pallas_call(
        body, name="fox_fwd", grid=(nh, s // tq),
        in_specs=[pl.BlockSpec((tq, AUG), lambda h, i: (i, h)), pl.BlockSpec((s, AUG), lambda h, i: (0, h)),
                  pl.BlockSpec((s, HEAD_DIM), lambda h, i: (0, 2 * nh + h)), pl.BlockSpec((1, HEAD_DIM), lambda h, i: (0, 0))],
        out_specs=[hb, hb, pl.BlockSpec((1, tq, LANES), lambda h, i: (h, i, 0))],
        out_shape=[jax.ShapeDtypeStruct((s, fw), f32), jax.ShapeDtypeStruct((s, fw), bf16), jax.ShapeDtypeStruct((nh, s, LANES), f32)],
        compiler_params=_cp("parallel", "parallel"))(qa, ka, qkv, wn)


def _fox_post_bwd(don, o, lse2, cum, qkv, wn, nh):
    s, fw = o.shape
    tm = _head_row_tile(s)

    def body(don_ref, o_ref, lse_ref, cum_ref, q_ref, wn_ref, qb_ref, doa_ref, dwn_ref):
        i = pl.program_id(0)
        h = pl.program_id(1)

        @pl.when((i == 0) & (h == 0))
        def _():
            dwn_ref[...] = jnp.zeros_like(dwn_ref)

        o = o_ref[...]
        don = don_ref[...].astype(f32)
        r = lax.rsqrt(jnp.mean(o * o, axis=-1, keepdims=True) + EPS)
        n = o * r
        dwn_ref[...] += jnp.sum(don * n, axis=0, keepdims=True)
        dn = don * wn_ref[...]
        do = r * (dn - n * jnp.mean(dn * n, axis=-1, keepdims=True))
        delta = jnp.sum(do * o, axis=-1, keepdims=True)
        a2 = _lane_col(cum_ref[...], h) * LOG2E - _lane_col(lse_ref[0], 0)
        qb_ref[:, :HEAD_DIM] = q_ref[...]
        qb_ref[:, HEAD_DIM:] = _aug_block(tm, _split3(a2), 0, 3).astype(bf16)
        doa_ref[:, :HEAD_DIM] = do.astype(bf16)
        doa_ref[:, HEAD_DIM:] = _aug_block(tm, _split3(-delta), 0).astype(bf16)

    hb = pl.BlockSpec((tm, HEAD_DIM), lambda i, h: (i, h))
    ab = pl.BlockSpec((tm, AUG), lambda i, h: (i, h))
    return pl.pallas_call(
        body, name="fox_post_bwd", grid=(s // tm, nh),
        in_specs=[hb, hb, pl.BlockSpec((1, tm, LANES), lambda i, h: (h, i, 0)), pl.BlockSpec((tm, LANES), lambda i, h: (i, 0)),
                  hb, pl.BlockSpec((1, HEAD_DIM), lambda i, h: (0, 0))],
        out_specs=[ab, ab, pl.BlockSpec((1, HEAD_DIM), lambda i, h: (0, 0))],
        out_shape=[jax.ShapeDtypeStruct((s, nh * AUG), bf16), jax.ShapeDtypeStruct((s, nh * AUG), bf16),
                   jax.ShapeDtypeStruct((1, HEAD_DIM), f32)],
        compiler_params=_cp("arbitrary", "arbitrary"))(don, o, lse2, cum, qkv, wn)


def _fox_bwd(qb, doa, ka, va, nh, tq):
    s = qb.shape[0]
    nq = s // tq
    fw = nh * HEAD_DIM

    def body(qb_ref, doa_ref, ka_ref, va_ref, dqx_ref, dkx_ref, dv_ref, dv_acc):
        j = pl.program_id(1)

        @pl.when(j == 0)
        def _():
            dqx_ref[...] = jnp.zeros_like(dqx_ref)

        kj = ka_ref[...]
        vj = va_ref[...]

        def tile(i, rows=tq, first=False, keep=None):
            off = pl.multiple_of(i * tq, tq)
            qi = qb_ref[pl.ds(off, rows), :]
            doi = doa_ref[pl.ds(off, rows), :]
            p = jnp.exp2(_dotb(qi, kj, NT))
            if keep is not None:
                p = jnp.where(keep, p, 0.0)
            ds = (p * _dotb(doi, vj, NT)).astype(bf16)
            dv = _dotb(p, doi[:, :HEAD_DIM], TN)
            dk = _dotb(ds, qi, TN)
            if first:
                dv_acc[...] = dv
                dkx_ref[...] = dk
            else:
                dv_acc[...] += dv
                dkx_ref[...] += dk
            dqx_ref[pl.ds(off, rows), :] += _dotb(ds, kj)

        n = nq - 1 - j
        b0 = jnp.minimum(j, nq - 2)
        qpos = b0 * tq + lax.broadcasted_iota(jnp.int32, (2 * tq, tq), 0)
        kpos = j * tq + lax.broadcasted_iota(jnp.int32, (2 * tq, tq), 1)
        tile(b0, 2 * tq, first=True, keep=(kpos <= qpos) & ((qpos < (j + 1) * tq) | (lax.rem(n, 2) == 1)))

        def pair(ii, carry):
            tile(j + 1 + lax.rem(n, 2) + 2 * ii, 2 * tq)
            return carry

        lax.fori_loop(0, n // 2, pair, 0)
        dv_ref[...] = dv_acc[...].astype(dv_ref.dtype)

    panel = pl.BlockSpec((s, AUG), lambda h, j: (0, h))
    blk = pl.BlockSpec((tq, AUG), lambda h, j: (j, h))
    return pl.pallas_call(
        body, name="fox_bwd", grid=(nh, nq), in_specs=[panel, panel, blk, blk],
        out_specs=[panel, blk, pl.BlockSpec((tq, HEAD_DIM), lambda h, j: (j, h))],
        out_shape=[jax.ShapeDtypeStruct((s, nh * AUG), f32), jax.ShapeDtypeStruct((s, nh * AUG), f32),
                   jax.ShapeDtypeStruct((s, fw), bf16)],
        scratch_shapes=[pltpu.VMEM((tq, HEAD_DIM), f32)],
        compiler_params=_cp("parallel", "arbitrary"))(qb, doa, ka, va)


def _fox_unpack(dqx, dkx, nh):
    s = dqx.shape[0]
    fw = nh * HEAD_DIM
    tm = _head_row_tile(s)

    def body(dqx_ref, dkx_ref, dq_ref, dk_ref, dcum_ref):
        h = pl.program_id(1)

        @pl.when(h == 0)
        def _():
            dcum_ref[...] = jnp.zeros_like(dcum_ref)

        dq_ref[...] = (dqx_ref[:, :HEAD_DIM] * (HEAD_DIM ** -0.5)).astype(dq_ref.dtype)
        dk_ref[...] = (dkx_ref[:, :HEAD_DIM] * LN2).astype(dk_ref.dtype)
        d = _lane_col(dqx_ref[:, HEAD_DIM:], 0) - _lane_col(dkx_ref[:, HEAD_DIM:], 3)
        lane = lax.broadcasted_iota(jnp.int32, (tm, LANES), 1)
        dcum_ref[...] += jnp.where(lane == h, d, 0.0)

    ab = pl.BlockSpec((tm, AUG), lambda i, h: (i, h))
    hb = pl.BlockSpec((tm, HEAD_DIM), lambda i, h: (i, h))
    return pl.pallas_call(
        body, name="fox_unpack", grid=(s // tm, nh), in_specs=[ab, ab],
        out_specs=[hb, hb, pl.BlockSpec((tm, LANES), lambda i, h: (i, 0))],
        out_shape=[jax.ShapeDtypeStruct((s, fw), bf16), jax.ShapeDtypeStruct((s, fw), bf16), jax.ShapeDtypeStruct((s, LANES), f32)],
        compiler_params=_cp("parallel", "arbitrary"))(dqx, dkx)


def _conv_pre(xx, w, tm):
    pre = None
    for k in range(CONV_W):
        sh = CONV_W - 1 - k
        t = (pltpu.roll(xx, sh, 0) if sh else xx)[8:, :] * w[k:k + 1, :]
        pre = t if pre is None else pre + t
    return pre


def _gdn_pre(x, w, nh):
    s, cw = x.shape
    tm = _row_tile(s)
    fw = nh * HEAD_DIM

    def body(x_ref, prev_ref, w_ref, y_ref):
        i = pl.program_id(0)
        j = pl.program_id(1)
        for h in range(nh):
            sl = slice(h * HEAD_DIM, (h + 1) * HEAD_DIM)
            prev = jnp.where(i == 0, 0.0, prev_ref[:, sl])
            pre = _conv_pre(jnp.concatenate([prev, x_ref[:, sl]], axis=0), w_ref[:, sl], tm)
            y = pre * _sigmoid(pre)
            yn = y * lax.rsqrt(jnp.sum(y * y, axis=-1, keepdims=True) + EPS)
            y_ref[:, sl] = jnp.where(j < 2, yn, y)

    return pl.pallas_call(
        body, name="gdn_pre", grid=(s // tm, cw // fw),
        in_specs=[pl.BlockSpec((tm, fw), lambda i, j: (i, j)),
                  pl.BlockSpec((8, fw), lambda i, j: (jnp.maximum(i * (tm // 8) - 1, 0), j)),
                  pl.BlockSpec((CONV_W, fw), lambda i, j: (0, j))],
        out_specs=pl.BlockSpec((tm, fw), lambda i, j: (i, j)),
        out_shape=jax.ShapeDtypeStruct((s, cw), f32),
        compiler_params=_cp("parallel", "parallel"))(x, x, w)


def _gdn_pre_bwd(x, w, dyn, nh):
    s, cw = x.shape
    tm = _row_tile(s)
    fw = nh * HEAD_DIM

    def body(x_ref, prev_ref, w_ref, dyn_ref, dpre_ref):
        i = pl.program_id(0)
        j = pl.program_id(1)
        for h in range(nh):
            sl = slice(h * HEAD_DIM, (h + 1) * HEAD_DIM)
            prev = jnp.where(i == 0, 0.0, prev_ref[:, sl])
            pre = _conv_pre(jnp.concatenate([prev, x_ref[:, sl]], axis=0), w_ref[:, sl], tm)
            sg = _sigmoid(pre)
            y = pre * sg
            dyn = dyn_ref[:, sl]
            r = lax.rsqrt(jnp.sum(y * y, axis=-1, keepdims=True) + EPS)
            yn = y * r
            dy_n = r * (dyn - yn * jnp.sum(dyn * yn, axis=-1, keepdims=True))
            dy = jnp.where(j < 2, dy_n, dyn)
            dpre_ref[:, sl] = dy * sg * (1.0 + pre * (1.0 - sg))

    hb = pl.BlockSpec((tm, fw), lambda i, j: (i, j))
    return pl.pallas_call(
        body, name="gdn_pre_bwd", grid=(s // tm, cw // fw),
        in_specs=[hb, pl.BlockSpec((8, fw), lambda i, j: (jnp.maximum(i * (tm // 8) - 1, 0), j)),
                  pl.BlockSpec((CONV_W, fw), lambda i, j: (0, j)), hb],
        out_specs=hb, out_shape=jax.ShapeDtypeStruct((s, cw), f32),
        compiler_params=_cp("parallel", "parallel"))(x, x, w, dyn)


def _conv_bwd(x, w, dpre, nh):
    s, cw = x.shape
    tm = _row_tile(s)
    fw = nh * HEAD_DIM
    ni = s // tm

    def body(x_ref, prev_ref, w_ref, dp_ref, nxt_ref, dx_ref, dw_ref):
        i = pl.program_id(1)

        @pl.when(i == 0)
        def _():
            dw_ref[...] = jnp.zeros_like(dw_ref)

        for h in range(nh):
            sl = slice(h * HEAD_DIM, (h + 1) * HEAD_DIM)
            wv = w_ref[:, sl]
            dp = dp_ref[:, sl]
            nxt = jnp.where(i == ni - 1, 0.0, nxt_ref[:, sl])
            dd = jnp.concatenate([dp, nxt], axis=0)
            prev = jnp.where(i == 0, 0.0, prev_ref[:, sl])
            xx = jnp.concatenate([prev, x_ref[:, sl]], axis=0)
            dx = None
            rows = []
            for k in range(CONV_W):
                sh = CONV_W - 1 - k
                t = (pltpu.roll(dd, tm + 8 - sh, 0) if sh else dd)[:tm, :] * wv[k:k + 1, :]
                dx = t if dx is None else dx + t
                xs = (pltpu.roll(xx, sh, 0) if sh else xx)[8:, :]
                rows.append(jnp.sum(dp * xs, axis=0, keepdims=True))
            dx_ref[:, sl] = dx.astype(dx_ref.dtype)
            dw_ref[:, sl] += jnp.concatenate(rows, axis=0)

    hb = pl.BlockSpec((tm, fw), lambda j, i: (i, j))
    wb = pl.BlockSpec((CONV_W, fw), lambda j, i: (0, j))
    return pl.pallas_call(
        body, name="conv_bwd", grid=(cw // fw, ni),
        in_specs=[hb, pl.BlockSpec((8, fw), lambda j, i: (jnp.maximum(i * (tm // 8) - 1, 0), j)), wb, hb,
                  pl.BlockSpec((8, fw), lambda j, i: (jnp.minimum((i + 1) * (tm // 8), s // 8 - 1), j))],
        out_specs=[hb, wb],
        out_shape=[jax.ShapeDtypeStruct((s, cw), bf16), jax.ShapeDtypeStruct((CONV_W, cw), f32)],
        compiler_params=_cp("parallel", "arbitrary"))(x, x, w, dpre, dpre)


def _chunk_consts():
    c = GDN_CHUNK
    r = lax.broadcasted_iota(jnp.int32, (c, c), 0)
    q = lax.broadcasted_iota(jnp.int32, (c, c), 1)
    return r >= q, r > q, (r == q).astype(f32)


def _chunk_head(qkvn, sm, gcs, gcs_t, h, nh):
    fw = nh * HEAD_DIM
    q = qkvn[:, h * HEAD_DIM:(h + 1) * HEAD_DIM] * (HEAD_DIM ** -0.5)
    k = qkvn[:, fw + h * HEAD_DIM: fw + (h + 1) * HEAD_DIM]
    v = qkvn[:, 2 * fw + h * HEAD_DIM: 2 * fw + (h + 1) * HEAD_DIM]
    beta = _lane_col(sm, 2 * nh + h)
    gc = _lane_col(gcs, nh + h)
    gc_row = gcs_t[nh + h: nh + h + 1, :]
    incl, strict, _ = _chunk_consts()
    decay = jnp.where(incl, jnp.exp(jnp.minimum(gc - gc_row, 0.0)), 0.0)
    eg = jnp.exp(gc)
    g_last = gc[GDN_CHUNK - 1:GDN_CHUNK, :]
    egl = jnp.exp(g_last)
    ekd = jnp.exp(g_last - gc)
    kb = k * beta
    vb = v * beta
    kk = _dotb(kb, k, NT)
    qk = _dotb(q, k, NT)
    return dict(q=q, k=k, v=v, beta=beta, gc=gc, decay=decay, eg=eg, egl=egl, ekd=ekd, kb=kb, vb=vb, kk=kk, qk=qk,
                incl=incl, strict=strict)


def _unit_lower_inverses(lows, eye):
    ps = [-low for low in lows]
    ts = [eye + p for p in ps]
    for _ in range(5):
        ps = [_dotm(p, p) for p in ps]
        ts = [t + _dotm(t, p) for t, p in zip(ts, ps)]
    return ts


def _gdn_fwd(qkvn, sm, z, wn, nh):
    s = qkvn.shape[0]
    c = GDN_CHUNK
    nc = s // c
    fw = nh * HEAD_DIM

    def body(qkvn_ref, sm_ref, z_ref, wn_ref, on_ref, o_ref, st_ref, ti_ref, state):
        ci = pl.program_id(0)

        @pl.when(ci == 0)
        def _():
            state[...] = jnp.zeros_like(state)

        qkvn_v = qkvn_ref[...]
        sm_v = sm_ref[...]
        incl, strict, eye = _chunk_consts()
        gcs = _doth(incl.astype(f32), sm_v)
        gcs_t = gcs.T
        heads = range(nh)
        es = [_chunk_head(qkvn_v, sm_v, gcs, gcs_t, h, nh) for h in heads]
        tinvs = _unit_lower_inverses([jnp.where(strict, e["kk"] * e["decay"], 0.0) for e in es], eye)
        us = [_dotm(t, e["vb"]) for t, e in zip(tinvs, es)]
        ws = [_dotm(t, e["kb"] * e["eg"]) for t, e in zip(tinvs, es)]
        sts = [state[h] for h in heads]
        v_news = [u - _dotb(w, st) for u, w, st in zip(us, ws, sts)]
        qss = [_dotb(e["q"] * e["eg"], st) for e, st in zip(es, sts)]
        os_ = [qs + _dotb(jnp.where(incl, e["qk"] * e["decay"], 0.0), vn) for qs, e, vn in zip(qss, es, v_news)]
        upd = [_dotb(e["k"] * e["ekd"], vn, TN) for e, vn in zip(es, v_news)]
        for h in heads:
            st_ref[0, h] = sts[h]
            ti_ref[0, h] = tinvs[h]
            state[h] = sts[h] * es[h]["egl"] + upd[h]
            sl = slice(h * HEAD_DIM, (h + 1) * HEAD_DIM)
            o = os_[h]
            o_ref[:, sl] = o
            zz = z_ref[:, sl]
            r = lax.rsqrt(jnp.mean(o * o, axis=-1, keepdims=True) + EPS)
            on_ref[:, sl] = (o * r * wn_ref[...] * (zz * _sigmoid(zz))).astype(on_ref.dtype)

    return pl.pallas_call(
        body, name="gdn_fwd", grid=(nc,),
        in_specs=[pl.BlockSpec((c, 3 * fw), lambda i: (i, 0)), pl.BlockSpec((c, LANES), lambda i: (i, 0)),
                  pl.BlockSpec((c, fw), lambda i: (i, 0)), pl.BlockSpec((1, HEAD_DIM), lambda i: (0, 0))],
        out_specs=[pl.BlockSpec((c, fw), lambda i: (i, 0)), pl.BlockSpec((c, fw), lambda i: (i, 0)),
                   pl.BlockSpec((1, nh, HEAD_DIM, HEAD_DIM), lambda i: (i, 0, 0, 0)),
                   pl.BlockSpec((1, nh, c, c), lambda i: (i, 0, 0, 0))],
        out_shape=[jax.ShapeDtypeStruct((s, fw), bf16), jax.ShapeDtypeStruct((s, fw), f32),
                   jax.ShapeDtypeStruct((nc, nh, HEAD_DIM, HEAD_DIM), f32), jax.ShapeDtypeStruct((nc, nh, c, c), f32)],
        scratch_shapes=[pltpu.VMEM((nh, HEAD_DIM, HEAD_DIM), f32)],
        compiler_params=_cp("arbitrary"))(qkvn, sm, z, wn)


def _gdn_post_bwd(don, o, z, wn, nh):
    s, fw = o.shape
    tm = _head_row_tile(s)

    def body(don_ref, o_ref, z_ref, wn_ref, do_ref, dz_ref, dwn_ref):
        i = pl.program_id(0)
        h = pl.program_id(1)

        @pl.when((i == 0) & (h == 0))
        def _():
            dwn_ref[...] = jnp.zeros_like(dwn_ref)

        o = o_ref[...]
        zz = z_ref[...]
        don = don_ref[...].astype(f32)
        wv = wn_ref[...]
        r = lax.rsqrt(jnp.mean(o * o, axis=-1, keepdims=True) + EPS)
        n = o * r
        sg = _sigmoid(zz)
        silu = zz * sg
        dz_ref[...] = (don * n * wv * sg * (1.0 + zz * (1.0 - sg))).astype(dz_ref.dtype)
        dnw = don * silu
        dwn_ref[...] += jnp.sum(dnw * n, axis=0, keepdims=True)
        dn = dnw * wv
        do_ref[...] = r * (dn - n * jnp.mean(dn * n, axis=-1, keepdims=True))

    hb = pl.BlockSpec((tm, HEAD_DIM), lambda i, h: (i, h))
    wb = pl.BlockSpec((1, HEAD_DIM), lambda i, h: (0, 0))
    return pl.pallas_call(
        body, name="gdn_post_bwd", grid=(s // tm, nh), in_specs=[hb, hb, hb, wb], out_specs=[hb, hb, wb],
        out_shape=[jax.ShapeDtypeStruct((s, fw), f32), jax.ShapeDtypeStruct((s, fw), bf16),
                   jax.ShapeDtypeStruct((1, HEAD_DIM), f32)],
        compiler_params=_cp("arbitrary", "arbitrary"))(don, o, z, wn)


def _gdn_bwd(qkvn, sm, do, states, tinvs, nh):
    s = qkvn.shape[0]
    c = GDN_CHUNK
    nc = s // c
    fw = nh * HEAD_DIM

    def body(qkvn_ref, sm_ref, do_ref, st_ref, ti_ref, dqkvn_ref, dsm_ref, dstate):
        ci = pl.program_id(0)

        @pl.when(ci == 0)
        def _():
            dstate[...] = jnp.zeros_like(dstate)

        qkvn_v = qkvn_ref[...]
        sm_v = sm_ref[...]
        incl, strict, eye = _chunk_consts()
        inclf = incl.astype(f32)
        gcs = _doth(inclf, sm_v)
        gcs_t = gcs.T
        lane = lax.broadcasted_iota(jnp.int32, (c, LANES), 1)
        last_row = lax.broadcasted_iota(jnp.int32, (c, 1), 0) == c - 1
        ones_cl = jnp.ones((c, LANES), f32)
        each = lambda f: [f(h) for h in range(nh)]
        es = each(lambda h: _chunk_head(qkvn_v, sm_v, gcs, gcs_t, h, nh))
        tinv = each(lambda h: ti_ref[0, h])
        st = each(lambda h: st_ref[0, h])
        dst = each(lambda h: dstate[h])
        do = each(lambda h: do_ref[:, h * HEAD_DIM:(h + 1) * HEAD_DIM])
        kg = each(lambda h: es[h]["kb"] * es[h]["eg"])
        qg = each(lambda h: es[h]["q"] * es[h]["eg"])
        kd = each(lambda h: es[h]["k"] * es[h]["ekd"])
        u = each(lambda h: _dotm(tinv[h], es[h]["vb"]))
        w = each(lambda h: _dotm(tinv[h], kg[h]))
        a = each(lambda h: jnp.where(incl, es[h]["qk"] * es[h]["decay"], 0.0))
        v_new = each(lambda h: u[h] - _dotb(w[h], st[h]))
        dv_new = each(lambda h: _dotb(a[h], do[h], TN) + _dotb(kd[h], dst[h]))
        da = each(lambda h: jnp.where(incl, _dotb(do[h], v_new[h], NT), 0.0))
        dqg = each(lambda h: _dotb(do[h], st[h], NT))
        dkd = each(lambda h: _dotb(v_new[h], dst[h], NT))
        dglast = each(lambda h: es[h]["egl"] * jnp.sum(jnp.sum(dst[h] * st[h], axis=1, keepdims=True), axis=0, keepdims=True))
        dw = each(lambda h: -_dotb(dv_new[h], st[h], NT))
        new_dst = each(lambda h: _dotb(qg[h], do[h], TN) + es[h]["egl"] * dst[h] - _dotb(w[h], dv_new[h], TN))
        dtinv = each(lambda h: _dotm(dv_new[h], es[h]["vb"], NT) + _dotm(dw[h], kg[h], NT))
        dvb = each(lambda h: _dotm(tinv[h], dv_new[h], TN))
        dkg = each(lambda h: _dotm(tinv[h], dw[h], TN))
        tdt = each(lambda h: _dotm(tinv[h], dtinv[h], TN))
        dlow = each(lambda h: -_dotm(tdt[h], tinv[h], NT))
        dkk = each(lambda h: jnp.where(strict, dlow[h] * es[h]["decay"], 0.0))
        dqk = each(lambda h: da[h] * es[h]["decay"])
        darg = each(lambda h: (jnp.where(strict, dlow[h] * es[h]["kk"], 0.0) + da[h] * es[h]["qk"]) * es[h]["decay"])
        dgc = each(lambda h: jnp.sum(darg[h], axis=1, keepdims=True) - _doth(darg[h], ones_cl, TN)[:, 0:1])
        dkb = each(lambda h: _dotb(dkk[h], es[h]["k"]) + dkg[h] * es[h]["eg"])
        dk = each(lambda h: _dotb(dkk[h], es[h]["kb"], TN) + _dotb(dqk[h], es[h]["q"], TN) + dkd[h] * es[h]["ekd"]
                  + dkb[h] * es[h]["beta"])
        dq = each(lambda h: (_dotb(dqk[h], es[h]["k"]) + dqg[h] * es[h]["eg"]) * (HEAD_DIM ** -0.5))
        s_kd = each(lambda h: jnp.sum(dkd[h] * kd[h], axis=1, keepdims=True))
        dgc = each(lambda h: dgc[h] + jnp.sum(dkg[h] * kg[h] + dqg[h] * qg[h], axis=1, keepdims=True) - s_kd[h]
                   + jnp.where(last_row, jnp.sum(s_kd[h], axis=0, keepdims=True) + dglast[h], 0.0))
        dg = each(lambda h: _doth(inclf, dgc[h] * ones_cl, TN)[:, 0:1])
        dsm = jnp.zeros((c, LANES), f32)
        for h in range(nh):
            dstate[h] = new_dst[h]
            dbeta = jnp.sum(dkb[h] * es[h]["k"] + dvb[h] * es[h]["v"], axis=1, keepdims=True)
            dqkvn_ref[:, h * HEAD_DIM:(h + 1) * HEAD_DIM] = dq[h]
            dqkvn_ref[:, fw + h * HEAD_DIM: fw + (h + 1) * HEAD_DIM] = dk[h]
            dqkvn_ref[:, 2 * fw + h * HEAD_DIM: 2 * fw + (h + 1) * HEAD_DIM] = dvb[h] * es[h]["beta"]
            dsm = dsm + jnp.where(lane == nh + h, dg[h], 0.0) + jnp.where(lane == 2 * nh + h, dbeta, 0.0)
        dsm_ref[...] = dsm

    rev = lambda i: (nc - 1 - i, 0)
    rev4 = lambda i: (nc - 1 - i, 0, 0, 0)
    return pl.pallas_call(
        body, name="gdn_bwd", grid=(nc,),
        in_specs=[pl.BlockSpec((c, 3 * fw), rev), pl.BlockSpec((c, LANES), rev), pl.BlockSpec((c, fw), rev),
                  pl.BlockSpec((1, nh, HEAD_DIM, HEAD_DIM), rev4), pl.BlockSpec((1, nh, c, c), rev4)],
        out_specs=[pl.BlockSpec((c, 3 * fw), rev), pl.BlockSpec((c, LANES), rev)],
        out_shape=[jax.ShapeDtypeStruct((s, 3 * fw), f32), jax.ShapeDtypeStruct((s, LANES), f32)],
        scratch_shapes=[pltpu.VMEM((nh, HEAD_DIM, HEAD_DIM), f32)],
        compiler_params=_cp("arbitrary"))(qkvn, sm, do, states, tinvs)


MM_TILES = (1024, 512, 2048)
MM_TILES_TN = (512, 1024, 4096)
MM_TILES_F_DEEP = (1024, 512, 2816)
MM_TILES_LONG_K = (1024, 512, 2560)


def _hosted(res, comm):
    return res if comm else (res, None)


def _ffn_fwd(tag, x, g, mod3, w, comm_up=None, comm_down=None, wd_of=None):
    wg_t, wu_t, wd = w
    sh, sc, gt = mod3
    h = _ada_in(tag + "_ada", x, g, sh, sc)
    (gate, up, act), got_up = _hosted(_mm(tag + "_up", [[(h, wg_t)], [(h, wu_t)]], "nt", MM_TILES, _ep_swiglu,
                                          (bf16, bf16, bf16), comm=comm_up), comm_up)
    if wd_of:
        wd = wd_of(got_up)
    (xn, y), got_down = _hosted(_mm(tag + "_down", [[(act, wd)]], "nn", MM_TILES_F_DEEP, _ep_residual, (f32, bf16),
                                    extras=((x, "mn"), (MACARON_W * gt, "n")), comm=comm_down), comm_down)
    return xn, dict(x=x, h=h, gate=gate, up=up, y=y), got_up, got_down


def _ffn_bwd(tag, dxn, res, g, mod3, w, comm_dact=None, comm_dh_of=None):
    wg_t, wu_t, wd = w
    sh, sc, gt = mod3
    dy, dgs = _gate_bwd(tag + "_gate_bwd", dxn, res["y"], MACARON_W * gt)
    (act, dgate, dup), got = _hosted(_mm(tag + "_dact", [[(dy, wd)]], "nt", MM_TILES, _ep_swiglu_bwd, (bf16, bf16, bf16),
                                         extras=((res["gate"], "mn"), (res["up"], "mn")), comm=comm_dact), comm_dact)
    (dwd,) = _mm(tag + "_dwd", [[(act, dy)]], "tn", MM_TILES_TN, _ep_plain, (bf16,))
    (dwg_t,) = _mm(tag + "_dwg", [[(dgate, res["h"])]], "tn", MM_TILES_TN, _ep_plain, (bf16,))
    (dwu_t,) = _mm(tag + "_dwu", [[(dup, res["h"])]], "tn", MM_TILES_TN, _ep_plain, (bf16,))
    comm_dh = comm_dh_of and comm_dh_of((dwg_t, dwu_t, dwd))
    (dh,), got_dh = _hosted(_mm(tag + "_dh", [[(dgate, wg_t), (dup, wu_t)]], "nn", MM_TILES_F_DEEP, _ep_plain, (bf16,), comm=comm_dh),
                            comm_dh)
    dx, dsh, a = _ada_bwd(tag + "_ada_bwd", res["x"], g, sc, dh, dxn)
    return dx, (dwg_t, dwu_t, dwd), (dsh, a * g, MACARON_W * dgs), a * (1.0 + sc), got, got_dh


def _local_step(x, target, mods, norm_g, final_norm, ffn1_w, later_w, prm, fox_wn, gdn_wn, conv_w, nh, hooks=None):
    s, d = x.shape
    fw = nh * HEAD_DIM
    tq = _tile(s, min(256, s // 2))
    g_rows = [norm_g[i:i + 1] for i in range(3)]
    m1, m2, m3 = mods[0:3], mods[3:6], mods[6:9]

    x1, r1, got_up, got_down = _ffn_fwd("ffn1", x, g_rows[0], m1, ffn1_w, hooks and hooks.gather_mix_spec(),
                                        hooks and hooks.gather_ffn2_spec(), hooks and hooks.ffn1_wd)
    if hooks:
        ffn1_w = ffn1_w[:2] + (hooks.ffn1_wd(got_up),)
    w_cat_t, w_out, ffn2_w = hooks.gathered(got_up, got_down) if hooks else later_w
    h2 = _ada_in("mix_ada", x1, g_rows[1], m2[0], m2[1])
    w_fox, w_gdn, w_z, w_s = w_cat_t[:3 * fw], w_cat_t[3 * fw:6 * fw], w_cat_t[6 * fw:7 * fw], w_cat_t[7 * fw:]
    colscale = jnp.concatenate([jnp.full((1, fw), FOX_Q_SCALE, f32), jnp.ones((1, 2 * fw), f32)], axis=1)
    (qkv_f,) = _mm("proj_fox", [[(h2, w_fox)]], "nt", MM_TILES, _ep_colscale, (bf16,), extras=((colscale, "n"),))
    (qkv_g,) = _mm("proj_gdn", [[(h2, w_gdn)]], "nt", MM_TILES, _ep_plain, (f32,))
    (z,) = _mm("proj_z", [[(h2, w_z)]], "nt", MM_TILES, _ep_plain, (f32,))
    (ps,) = _mm("proj_s", [[(h2, w_s)]], "nt", MM_TILES, _ep_plain, (f32,))
    sm, cum = _small_fwd(ps, prm, nh)
    qa, ka, va = _fox_aug(qkv_f, cum, nh)
    o_f, on_f, lse2 = _fox_fwd(qa, ka, qkv_f, fox_wn, nh, tq)
    qkvn = _gdn_pre(qkv_g, conv_w, nh)
    on_g, o_g, states, tinvs = _gdn_fwd(qkvn, sm, z, gdn_wn, nh)
    w_top, w_bot = w_out[:fw], w_out[fw:]
    x2, mix = _mm("mix_out", [[(on_f, w_top), (on_g, w_bot)]], "nn", MM_TILES, _ep_residual, (f32, bf16),
                  extras=((x1, "mn"), (m2[2], "n")))
    x3, r3, _, _ = _ffn_fwd("ffn2", x2, g_rows[2], m3, ffn2_w)
    loss, dx3, dfinal = _final_loss(x3, final_norm, target)

    dx2, dffn2, dmod3, dg3, _, _ = _ffn_bwd("ffn2", dx3, r3, g_rows[2], m3, ffn2_w)
    rs_ffn2 = hooks and hooks.rs_ffn2_spec(dffn2)
    dmix, dgt2 = _gate_bwd("mix_gate_bwd", dx2, mix, m2[2])
    (don_f,) = _mm("mix_dof", [[(dmix, w_top)]], "nt", MM_TILES, _ep_plain, (f32,))
    (don_g,) = _mm("mix_dog", [[(dmix, w_bot)]], "nt", MM_TILES, _ep_plain, (f32,))
    (dw_top,) = _mm("mix_dwtop", [[(on_f, dmix)]], "tn", MM_TILES_TN, _ep_plain, (bf16,))
    (dw_bot,) = _mm("mix_dwbot", [[(on_g, dmix)]], "tn", MM_TILES_TN, _ep_plain, (bf16,))
    qb, doa, dfox_wn = _fox_post_bwd(don_f, o_f, lse2, cum, qkv_f, fox_wn, nh)
    dqx, dkx, dv_f = _fox_bwd(qb, doa, ka, va, nh, tq)
    dq_f, dk_f, dcum = _fox_unpack(dqx, dkx, nh)
    do_g, dz, dgdn_wn = _gdn_post_bwd(don_g, o_g, z, gdn_wn, nh)
    dqkvn, dsm = _gdn_bwd(qkvn, sm, do_g, states, tinvs, nh)
    dpre = _gdn_pre_bwd(qkv_g, conv_w, dqkvn, nh)
    dqkv_g, dconv = _conv_bwd(qkv_g, conv_w, dpre, nh)
    dps, pg = _small_bwd(ps, prm, dsm, dcum, nh)
    dproj = jnp.concatenate([dq_f, dk_f, dv_f, dqkv_g, dz, dps], axis=1)
    ((dw_cat_t,), got_ffn2) = _hosted(_mm("proj_dw", [[(dproj, h2)]], "tn", MM_TILES_TN, _ep_plain, (bf16,), comm=rs_ffn2), rs_ffn2)
    dw_out = jnp.concatenate([dw_top, dw_bot], axis=0)
    rs_mix = hooks and hooks.rs_mix_spec(dw_cat_t, dw_out)
    (dh2,) = _mm("proj_dh", [[(dproj, w_cat_t)]], "nn", MM_TILES_LONG_K, _ep_plain, (bf16,))
    dx1, dsh2, a2 = _ada_bwd("mix_ada_bwd", x1, g_rows[1], m2[1], dh2, dx2)
    dmod2 = (dsh2, a2 * g_rows[1], dgt2)
    dg2 = a2 * (1.0 + m2[1])
    dx0, dffn1, dmod1, dg1, got_mix, got_ffn1 = _ffn_bwd("ffn1", dx1, r1, g_rows[0], m1, ffn1_w, rs_mix,
                                                          hooks and hooks.rs_ffn1_spec)

    big = dict(ffn=(dffn1, dffn2), w_cat_t=dw_cat_t, w_out=dw_out, got_ffn2=got_ffn2, got_mix=got_mix, got_ffn1=got_ffn1)
    small = dict(loss=loss, norm_g=jnp.concatenate([dg1, dg2, dg3], axis=0), final_norm=dfinal, fox_wn=dfox_wn,
                 gdn_wn=dgdn_wn, pg=pg, conv=dconv, mod=jnp.concatenate(list(dmod1) + list(dmod2) + list(dmod3), axis=1))
    return dx0, big, small


def _w_in_row_groups(nh):
    fw = nh * HEAD_DIM
    sizes = [3 * fw, nh, 3 * fw, nh, nh, fw]
    offs = [0]
    for sz in sizes:
        offs.append(offs[-1] + sz)
    return [(offs[i], offs[i + 1]) for i in range(len(sizes))]


def _build_w_cat_t(w_in_t, nh):
    gq, gf, gg, ga, gb, gz = _w_in_row_groups(nh)
    d = w_in_t.shape[1]
    rows = lambda r: w_in_t[r[0]:r[1]]
    pad = jnp.zeros((LANES - 3 * nh, d), w_in_t.dtype)
    return jnp.concatenate([rows(gq), rows(gg), rows(gz), rows(gf), rows(ga), rows(gb), pad], axis=0)


def _split_dw_cat_t(dw_cat_t, nh):
    fw = nh * HEAD_DIM
    o = 7 * fw
    return jnp.concatenate([dw_cat_t[:3 * fw], dw_cat_t[o:o + nh], dw_cat_t[3 * fw:6 * fw], dw_cat_t[o + nh:o + 2 * nh],
                            dw_cat_t[o + 2 * nh:o + 3 * nh], dw_cat_t[6 * fw:7 * fw]], axis=0)


def _head_params(fox_f_bias, gdn_dt_bias, gdn_a_log, nh):
    z = jnp.zeros((8, LANES), f32)
    z = z.at[0, 0:nh].set(fox_f_bias.reshape(nh))
    z = z.at[1, nh:2 * nh].set(gdn_dt_bias.reshape(nh))
    z = z.at[2, nh:2 * nh].set(gdn_a_log.reshape(nh))
    return z


ANY = pl.BlockSpec(memory_space=pl.ANY)
IN_VMEM = pl.BlockSpec(memory_space=pltpu.VMEM)
N_OTHER_CHIPS = 3


def _place():
    x, y, c = lax.axis_index("x"), lax.axis_index("y"), lax.axis_index("c")
    chips = [(1 - x, y), (x, 1 - y), (1 - x, 1 - y)]
    return x, y, c, chips


def _allgather8(name, v):
    r, n = v.shape

    def body(v_ref, out_ref, send_sems, recv_sems, local_sem):
        x, y, c, _ = _place()
        me = 4 * x + 2 * y + c
        mine = pltpu.make_async_copy(v_ref, out_ref.at[me], local_sem)
        mine.start()
        copies = []
        for k in range(1, 8):
            fx, fy, fc = (k >> 2) & 1, (k >> 1) & 1, k & 1
            peer = (x + fx - 2 * x * fx, y + fy - 2 * y * fy, c + fc - 2 * c * fc)
            cp = pltpu.make_async_remote_copy(src_ref=v_ref, dst_ref=out_ref.at[me], send_sem=send_sems.at[k - 1],
                                              recv_sem=recv_sems.at[k - 1], device_id=peer, device_id_type=MESH)
            cp.start()
            copies.append(cp)
        for cp in copies:
            cp.wait()
        mine.wait()

    return pl.pallas_call(
        body, name=name, in_specs=[IN_VMEM], out_specs=IN_VMEM, out_shape=jax.ShapeDtypeStruct((8, r, n), v.dtype),
        scratch_shapes=[pltpu.SemaphoreType.DMA((7,)), pltpu.SemaphoreType.DMA((7,)), pltpu.SemaphoreType.DMA],
        compiler_params=pltpu.CompilerParams(vmem_limit_bytes=VMEM_LIMIT_V7X))(v)


class _CommSpec:
    def __init__(self, ins, out_shapes, sem_counts, run):
        self.ins, self.out_shapes, self.sem_counts, self.run = list(ins), list(out_shapes), sem_counts, run

    def sems(self):
        return [pltpu.SemaphoreType.DMA((n,)) for n in self.sem_counts]


def _run_comm(name, spec):
    ni, no = len(spec.ins), len(spec.out_shapes)

    def body(*refs):
        ins, outs, sems = refs[:ni], refs[ni:ni + no], refs[ni + no:]
        spec.run("start", ins, outs, sems)
        spec.run("finish", ins, outs, sems)

    return pl.pallas_call(body, name=name, in_specs=[ANY] * ni, out_specs=[ANY] * no, out_shape=spec.out_shapes,
                          scratch_shapes=spec.sems())(*spec.ins)


def _gather_spec(halves):
    nw = len(halves)

    def run(phase, ins, outs, sems):
        ici_send, ici_recv, d2d_send, d2d_recv = sems
        x, y, c, chips = _place()
        s = 2 * x + y
        sib = (x, y, 1 - c)

        def over_ici(w, j, dst):
            cx, cy = chips[j]
            return pltpu.make_async_remote_copy(src_ref=ins[w].at[c], dst_ref=dst, send_sem=ici_send.at[w * 3 + j],
                                                recv_sem=ici_recv.at[w * 3 + j], device_id=(cx, cy, c), device_id_type=MESH)

        def to_sibling(w, j, blk):
            return pltpu.make_async_remote_copy(src_ref=blk, dst_ref=blk, send_sem=d2d_send.at[w * 3 + j],
                                                recv_sem=d2d_recv.at[w * 3 + j], device_id=sib, device_id_type=MESH)

        pairs = [(w, j) for w in range(nw) for j in range(N_OTHER_CHIPS)]
        chip_of = lambda j: 2 * chips[j][0] + chips[j][1]
        if phase == "start":
            for w, j in pairs:
                over_ici(w, j, outs[w].at[c, s]).start()
            return
        for w, j in pairs:
            landed = outs[w].at[c, chip_of(j)]
            over_ici(w, j, landed).wait_recv()
            to_sibling(w, j, landed).start()
        for w, j in pairs:
            to_sibling(w, j, outs[w].at[1 - c, chip_of(j)]).wait_recv()
        for w, j in pairs:
            over_ici(w, j, outs[w].at[c, s]).wait_send()
            to_sibling(w, j, outs[w].at[c, chip_of(j)]).wait_send()

    n3 = nw * N_OTHER_CHIPS
    return _CommSpec(halves, [jax.ShapeDtypeStruct((2, 4) + h.shape[1:], h.dtype) for h in halves], [n3] * 4, run)


def _to_chips_spec(partials):
    nw = len(partials)

    def run(phase, ins, outs, sems):
        send_sems, recv_sems = sems
        x, y, c, chips = _place()
        for w in range(nw):
            for j, (cx, cy) in enumerate(chips):
                cp = pltpu.make_async_remote_copy(src_ref=ins[w].at[2 * cx + cy], dst_ref=outs[w].at[j],
                                                  send_sem=send_sems.at[w * 3 + j], recv_sem=recv_sems.at[w * 3 + j],
                                                  device_id=(cx, cy, c), device_id_type=MESH)
                if phase == "start":
                    cp.start()
                else:
                    cp.wait()

    n3 = nw * N_OTHER_CHIPS
    return _CommSpec(partials, [jax.ShapeDtypeStruct((3,) + a.shape[1:], a.dtype) for a in partials], [n3, n3], run)


def _send_to_sibling(name, srcs, other_half):
    nw = len(srcs)

    def body(*refs):
        ins, outs = refs[:nw], refs[nw:2 * nw]
        send_sems, recv_sems = refs[2 * nw:]
        x, y, c, _ = _place()
        cps = []
        for w in range(nw):
            cp = pltpu.make_async_remote_copy(src_ref=ins[w].at[1 - c] if other_half else ins[w], dst_ref=outs[w],
                                              send_sem=send_sems.at[w], recv_sem=recv_sems.at[w],
                                              device_id=(x, y, 1 - c), device_id_type=MESH)
            cp.start()
            cps.append(cp)
        for cp in cps:
            cp.wait()

    return pl.pallas_call(
        body, name=name, in_specs=[ANY] * nw, out_specs=[ANY] * nw,
        out_shape=[jax.ShapeDtypeStruct(a.shape[1:] if other_half else a.shape, a.dtype) for a in srcs],
        scratch_shapes=[pltpu.SemaphoreType.DMA((nw,)), pltpu.SemaphoreType.DMA((nw,))],
    )(*srcs)


def _add_pair(name, g, recv, c):
    _, nchip, r, d = g.shape
    tr = _tile(r, 512, 16)

    def body(c_ref, g_ref, r_ref, o_ref):
        o_ref[...] = (g_ref[...].astype(f32) + r_ref[...].astype(f32)).astype(o_ref.dtype)

    gs = pltpu.PrefetchScalarGridSpec(
        num_scalar_prefetch=1, grid=(nchip, r // tr),
        in_specs=[pl.BlockSpec((None, None, tr, d), lambda t, i, cr: (cr[0], t, i, 0)),
                  pl.BlockSpec((None, tr, d), lambda t, i, cr: (t, i, 0))],
        out_specs=pl.BlockSpec((None, tr, d), lambda t, i, cr: (t, i, 0)))
    return pl.pallas_call(body, name=name, grid_spec=gs, out_shape=jax.ShapeDtypeStruct((nchip, r, d), bf16),
                          compiler_params=_cp("parallel", "parallel"))(c.reshape(1).astype(jnp.int32), g, recv)


def _add_chips(name, p, recv, s_chip):
    _, r, d = p.shape
    tr = _tile(r, 512, 16)

    def body(s_ref, p_ref, r_ref, o_ref):
        o_ref[...] = ((p_ref[...].astype(f32) + r_ref[0].astype(f32)) + r_ref[1].astype(f32)) + r_ref[2].astype(f32)

    gs = pltpu.PrefetchScalarGridSpec(
        num_scalar_prefetch=1, grid=(r // tr,),
        in_specs=[pl.BlockSpec((None, tr, d), lambda i, sr: (sr[0], i, 0)),
                  pl.BlockSpec((3, tr, d), lambda i, sr: (0, i, 0))],
        out_specs=pl.BlockSpec((tr, d), lambda i, sr: (i, 0)))
    return pl.pallas_call(body, name=name, grid_spec=gs, out_shape=jax.ShapeDtypeStruct((r, d), f32),
                          compiler_params=_cp("parallel"))(s_chip.reshape(1).astype(jnp.int32), p, recv)


def _rs_pair_sums(tag, grads, c):
    from_sib = _send_to_sibling("rs_to_sibling_" + tag, grads, True)
    return [_add_pair("rs_add_pair_%s%d" % (tag, n), g, r, c) for n, (g, r) in enumerate(zip(grads, from_sib))]


def _rs_chip_sums(tag, partial, from_chips, s_chip):
    return [_add_chips("rs_add_chips_%s%d" % (tag, n), p, r, s_chip) for n, (p, r) in enumerate(zip(partial, from_chips))]


def _rs_both_halves(mine, c):
    theirs = _send_to_sibling("rs_exchange_halves", mine, False)
    return [jnp.where(c == 0, jnp.stack([a, b]), jnp.stack([b, a])) for a, b in zip(mine, theirs)]


def _sum_devices(v):
    n = v.shape[2]

    def body(v_ref, o_ref):
        t = v_ref[0]
        for k in range(1, 8):
            t = t + v_ref[k]
        o_ref[...] = t

    return pl.pallas_call(body, name="sum_devices", out_shape=jax.ShapeDtypeStruct((1, n), f32))(v)


def _silu_rows(v):
    def body(v_ref, o_ref):
        t = v_ref[...]
        o_ref[...] = t * _sigmoid(t)

    return pl.pallas_call(body, name="silu_cond", out_shape=jax.ShapeDtypeStruct(v.shape, f32))(v)


ADAMW_BLOCK_ELEMS = 600 * 1024


def _adamw(name, w, g, m, v):
    r, cdim = w.shape
    tr = _tile(r, max(8, min(256, (ADAMW_BLOCK_ELEMS // cdim) // 8 * 8)), 8)
    c1 = 1.0 - ADAM_B1 ** ADAM_STEP
    c2 = 1.0 - ADAM_B2 ** ADAM_STEP

    def body(w_ref, g_ref, m_ref, v_ref, d_ref, mo_ref, vo_ref):
        gv = g_ref[...]
        mn = ADAM_B1 * m_ref[...] + (1.0 - ADAM_B1) * gv
        vn = ADAM_B2 * v_ref[...] + (1.0 - ADAM_B2) * (gv * gv)
        d_ref[...] = -ADAM_LR * ((mn / c1) / (jnp.sqrt(vn / c2) + ADAM_EPS) + ADAM_WD * w_ref[...])
        mo_ref[...] = mn
        vo_ref[...] = vn

    blk = pl.BlockSpec((tr, cdim), lambda i: (i, 0))
    return pl.pallas_call(body, name=name, grid=(r // tr,), in_specs=[blk] * 4, out_specs=[blk] * 3,
                          out_shape=[jax.ShapeDtypeStruct((r, cdim), f32)] * 3, compiler_params=_cp("parallel"))(w, g, m, v)


def _ep_bias(accs, ex):
    return (accs[0] + ex[0],)


def kernel(x, c, ada_w, ada_b, norm_g, ffn_w_gate, ffn_w_up, ffn_w_down, w_in, w_out, fox_f_bias, fox_out_norm, gdn_conv, gdn_A_log, gdn_dt_bias, gdn_out_norm, final_norm, loss_target, m_ada_w, m_ada_b, m_norm_g, m_ffn_w_gate, m_ffn_w_up, m_ffn_w_down, m_w_in, m_w_out, m_fox_f_bias, m_fox_out_norm, m_gdn_conv, m_gdn_A_log, m_gdn_dt_bias, m_gdn_out_norm, m_final_norm, v_ada_w, v_ada_b, v_norm_g, v_ffn_w_gate, v_ffn_w_up, v_ffn_w_down, v_w_in, v_w_out, v_fox_f_bias, v_fox_out_norm, v_gdn_conv, v_gdn_A_log, v_gdn_dt_bias, v_gdn_out_norm, v_final_norm):
    ix, iy, ic = lax.axis_index("x"), lax.axis_index("y"), lax.axis_index("c")
    s_chip = 2 * ix + iy
    me = 4 * ix + 2 * iy + ic
    _, s, d = x.shape
    nh = d // (2 * HEAD_DIM)
    fw = nh * HEAD_DIM
    ncol = ada_w.shape[2]
    dg_sh = norm_g.shape[2]
    cv_sh = gdn_conv.shape[2]
    ff_sh = ffn_w_gate.shape[3]
    in_sh = w_in.shape[2]
    in_pad = -(-in_sh // 32) * 32
    out_sh = w_out.shape[1]
    per_chip = lambda a, t: a[2 * t]

    pack0 = jnp.concatenate([_silu_rows(c), norm_g[0].reshape(1, 3 * dg_sh), gdn_conv[0].reshape(1, CONV_W * cv_sh)], axis=1)
    got0 = _allgather8("gather_cond", pack0)
    cond_all = got0[:, 0, :d]
    norm_g_full = jnp.concatenate([per_chip(got0, t)[0, d:d + 3 * dg_sh].reshape(3, dg_sh) for t in range(4)], axis=1)
    conv_full = jnp.concatenate([per_chip(got0, t)[0, d + 3 * dg_sh:].reshape(CONV_W, cv_sh) for t in range(4)], axis=1)

    ada_b_sh = lax.dynamic_slice_in_dim(ada_b, s_chip * ncol, ncol, axis=1)
    (mod_sh,) = _mm("ada_mod", [[(cond_all, ada_w[0])]], "nn", (8, 512, 2048), _ep_bias, (f32,), extras=((ada_b_sh, "n"),))
    mod_all = _allgather8("gather_mod", mod_sh)
    mod = jnp.concatenate([lax.dynamic_index_in_dim(per_chip(mod_all, t), me, axis=0, keepdims=True) for t in range(4)], axis=1)
    mods = [mod[:, i * d:(i + 1) * d] for i in range(9)]

    halved = lambda a: a.reshape(2, a.shape[0] // 2, d)
    ffn_halves = [[halved(ffn_w_gate[0, j].T.astype(bf16)), halved(ffn_w_up[0, j].T.astype(bf16)),
                   halved(ffn_w_down[0, j].astype(bf16))] for j in range(2)]
    mix_halves = [halved(jnp.pad(w_in[0].T.astype(bf16), ((0, in_pad - in_sh), (0, 0)))), halved(w_out[0].astype(bf16))]
    with_own = lambda got, hs: [lax.dynamic_update_slice(g, h[:, None], (0, s_chip, 0, 0)) for g, h in zip(got, hs)]
    ffn_full = lambda got, hs: tuple(g.reshape(4 * ff_sh, d) for g in with_own(got, hs))
    ffn_blocks = lambda grads: [g.reshape(2, 4, ff_sh // 2, d) for g in grads]
    ffn1_w = ffn_full(_run_comm("gather_ffn1", _gather_spec(ffn_halves[0][:2])), ffn_halves[0][:2]) + (None,)
    prm = _head_params(fox_f_bias, gdn_dt_bias, gdn_A_log, nh)

    class Hooks:
        def gather_mix_spec(self):
            return _gather_spec(mix_halves + ffn_halves[0][2:])

        def ffn1_wd(self, got):
            return ffn_full(got[2:], ffn_halves[0][2:])[0]

        def gather_ffn2_spec(self):
            return _gather_spec(ffn_halves[1])

        def gathered(self, got_mix, got_ffn2):
            g_win, g_wo = with_own(got_mix[:2], mix_halves)
            w_in_t = jnp.swapaxes(g_win, 0, 1).reshape(4, in_pad, d)[:, :in_sh].reshape(4 * in_sh, d)
            return _build_w_cat_t(w_in_t, nh), jnp.swapaxes(g_wo, 0, 1).reshape(4 * out_sh, d), ffn_full(got_ffn2, ffn_halves[1])

        def rs_ffn2_spec(self, dffn2):
            self.ffn2_pairs = _rs_pair_sums("ffn2", ffn_blocks(dffn2), ic)
            return _to_chips_spec(self.ffn2_pairs)

        def rs_ffn1_spec(self, dffn1):
            self.ffn1_pairs = _rs_pair_sums("ffn1", ffn_blocks(dffn1), ic)
            return _to_chips_spec(self.ffn1_pairs)

        def rs_mix_spec(self, dw_cat_t, dw_out):
            dw_in_t = jnp.pad(_split_dw_cat_t(dw_cat_t, nh).reshape(4, in_sh, d), ((0, 0), (0, in_pad - in_sh), (0, 0)))
            grads = [jnp.swapaxes(dw_in_t.reshape(4, 2, in_pad // 2, d), 0, 1),
                     jnp.swapaxes(dw_out.reshape(4, 2, out_sh // 2, d), 0, 1)]
            self.mix_pairs = _rs_pair_sums("mix", grads, ic)
            return _to_chips_spec(self.mix_pairs)

    hooks = Hooks()

    dx0, big, small = _local_step(x[0], loss_target[0], mods, norm_g_full, final_norm.reshape(1, d), ffn1_w, None, prm,
                                  fox_out_norm, gdn_out_norm, conv_full, nh, hooks)

    pack1 = jnp.concatenate([small["loss"], small["norm_g"].reshape(1, 3 * d), small["final_norm"], small["fox_wn"],
                             small["gdn_wn"], small["pg"][0:1], small["pg"][1:2], small["conv"].reshape(1, CONV_W * 3 * fw),
                             small["mod"]], axis=1)
    got1 = _allgather8("gather_small_grads", pack1)
    tot = _sum_devices(got1)
    o = [0]

    def take(n):
        o[0] += n
        return tot[:, o[0] - n:o[0]]

    loss = take(LANES)[0, 0]
    g_norm_g = lax.dynamic_slice_in_dim(take(3 * d).reshape(3, d), s_chip * dg_sh, dg_sh, axis=1)[None]
    g_final = take(d).reshape(d)
    g_fox_wn = take(HEAD_DIM)
    g_gdn_wn = take(HEAD_DIM)
    pg0, pg1 = take(LANES), take(LANES)
    g_fbias, g_dtb, g_alog = pg0[:, 0:nh], pg0[:, nh:2 * nh], pg1[:, nh:2 * nh]
    g_conv = lax.dynamic_slice_in_dim(take(CONV_W * 3 * fw).reshape(CONV_W, 3 * fw), s_chip * cv_sh, cv_sh, axis=1)[None]
    g_ada_b = take(9 * d)
    dmod_all = got1[:, 0, o[0] - 9 * d:o[0]]
    dmod_sh = lax.dynamic_slice_in_dim(dmod_all, s_chip * ncol, ncol, axis=1)
    (g_ada_w,) = _mm("ada_dw", [[(cond_all, dmod_sh)]], "tn", (2048, 512, 8), _ep_plain, (f32,))

    ffn1_mine = _rs_chip_sums("ffn1", hooks.ffn1_pairs, big["got_ffn1"], s_chip)
    ffn2_mine = _rs_chip_sums("ffn2", hooks.ffn2_pairs, big["got_ffn2"], s_chip)
    mix_mine = _rs_chip_sums("mix", hooks.mix_pairs, big["got_mix"], s_chip)
    r1g, r1u, r1d, r2g, r2u, r2d, r_win, r_wo = _rs_both_halves(ffn1_mine + ffn2_mine + mix_mine, ic)
    rows = lambda r: r.reshape(-1, d)
    g_ffn_gate = jnp.stack([rows(r1g).T, rows(r2g).T])[None]
    g_ffn_up = jnp.stack([rows(r1u).T, rows(r2u).T])[None]
    g_ffn_down = jnp.stack([rows(r1d), rows(r2d)])[None]
    g_w_in = rows(r_win)[:in_sh].T[None]
    g_w_out = rows(r_wo)[None]

    def upd(name, w, g, m, v):
        shp = w.shape
        two = lambda a: a.reshape(-1, shp[-1])
        return tuple(t.reshape(shp) for t in _adamw(name, two(w), two(g), two(m), two(v)))

    big_upd = [upd("adamw_ada_w", ada_w, g_ada_w[None], m_ada_w, v_ada_w),
               upd("adamw_ffn_gate", ffn_w_gate, g_ffn_gate, m_ffn_w_gate, v_ffn_w_gate),
               upd("adamw_ffn_up", ffn_w_up, g_ffn_up, m_ffn_w_up, v_ffn_w_up),
               upd("adamw_ffn_down", ffn_w_down, g_ffn_down, m_ffn_w_down, v_ffn_w_down),
               upd("adamw_w_in", w_in, g_w_in, m_w_in, v_w_in),
               upd("adamw_w_out", w_out, g_w_out, m_w_out, v_w_out)]
    small_w = [ada_b, norm_g, fox_f_bias, fox_out_norm, gdn_conv, gdn_A_log, gdn_dt_bias, gdn_out_norm, final_norm]
    small_g = [g_ada_b, g_norm_g, g_fbias, g_fox_wn, g_conv, g_alog, g_dtb, g_gdn_wn, g_final]
    small_m = [m_ada_b, m_norm_g, m_fox_f_bias, m_fox_out_norm, m_gdn_conv, m_gdn_A_log, m_gdn_dt_bias, m_gdn_out_norm, m_final_norm]
    small_v = [v_ada_b, v_norm_g, v_fox_f_bias, v_fox_out_norm, v_gdn_conv, v_gdn_A_log, v_gdn_dt_bias, v_gdn_out_norm, v_final_norm]
    sizes = [a.size for a in small_w]
    npad = -sum(sizes) % LANES
    flat = lambda arrs, fill: jnp.concatenate([a.reshape(1, -1) for a in arrs] + [jnp.full((1, npad), fill, f32)], axis=1)
    sd, sm_, sv = _adamw("adamw_small", flat(small_w, 0.0), flat(small_g, 0.0), flat(small_m, 0.0), flat(small_v, 1.0))

    def unflat(t):
        out, off = [], 0
        for a, n in zip(small_w, sizes):
            out.append(t[0, off:off + n].reshape(a.shape))
            off += n
        return out

    small_g = [g.reshape(a.shape) for g, a in zip(small_g, small_w)]
    s_d, s_m, s_v = unflat(sd), unflat(sm_), unflat(sv)
    def order(bigs, smalls):
        return [bigs[0], smalls[0], smalls[1], bigs[1], bigs[2], bigs[3], bigs[4], bigs[5]] + list(smalls[2:])

    grads_out = order([g_ada_w[None], g_ffn_gate, g_ffn_up, g_ffn_down, g_w_in, g_w_out], small_g)
    deltas = order([u[0] for u in big_upd], s_d)
    new_m = order([u[1] for u in big_upd], s_m)
    new_v = order([u[2] for u in big_upd], s_v)
    return (loss, dx0[None], *grads_out, *deltas, *new_m, *new_v)
```

```python
import functools
import math

import jax
import jax.numpy as jnp
from jax import lax
from jax.experimental import pallas as pl
from jax.experimental.pallas import tpu as pltpu

f32 = jnp.float32
bf16 = jnp.bfloat16
HI = lax.Precision.HIGHEST
MESH = pl.DeviceIdType.MESH

EPS = 1e-6
HEAD_DIM = 128
LANES = 128
GDN_CHUNK = 64
CONV_W = 4
MACARON_W = 0.5
ADAM_LR, ADAM_B1, ADAM_B2, ADAM_EPS, ADAM_WD, ADAM_STEP = 0.001, 0.9, 0.999, 1e-08, 0.01, 10
VMEM_LIMIT_V7X = 56 * 1024 * 1024
NEG = -1e30

NN = (((1,), (0,)), ((), ()))
NT = (((1,), (1,)), ((), ()))
TN = (((0,), (0,)), ((), ()))


def _cp(*sem):
    return pltpu.CompilerParams(dimension_semantics=sem, vmem_limit_bytes=VMEM_LIMIT_V7X)


def _dotb(a, b, dn=NN):
    return lax.dot_general(a.astype(bf16), b.astype(bf16), dn, preferred_element_type=f32)


def _doth(a, b, dn=NN):
    return lax.dot_general(a.astype(f32), b.astype(f32), dn, precision=HI, preferred_element_type=f32)


def _dotm(a, b, dn=NN):
    return lax.dot_general(a.astype(f32), b.astype(f32), dn, precision=lax.Precision.HIGH, preferred_element_type=f32)


def _sigmoid(x):
    return 1.0 / (1.0 + jnp.exp(-x))


def _softplus(x):
    return jnp.maximum(x, 0.0) + jnp.log(1.0 + jnp.exp(-jnp.abs(x)))


def _lane_col(blk, lane_idx):
    lane = lax.broadcasted_iota(jnp.int32, blk.shape, 1)
    return jnp.sum(jnp.where(lane == lane_idx, blk, 0.0), axis=1, keepdims=True)


def _tile(n, pref, mult=LANES):
    if n <= pref:
        return n
    t = (pref // mult) * mult
    while t >= mult:
        if n % t == 0:
            return t
        t -= mult
    return n


MM_EPILOGUE_CHUNKS = 2


def _mm(name, groups, mode, tiles, epilogue, out_dtypes, extras=(), comm=None):
    a0, b0 = groups[0][0]
    if mode == "nn":
        (m, k), n = a0.shape, b0.shape[1]
    elif mode == "nt":
        (m, k), n = a0.shape, b0.shape[0]
    else:
        (k, m), n = a0.shape, b0.shape[1]
    tm, tn, tk = _tile(m, tiles[0]), _tile(n, tiles[1]), _tile(k, tiles[2])
    nk = k // tk
    assert m % tm == 0 and n % tn == 0 and k % tk == 0, (name, m, n, k, tm, tn, tk)
    if mode == "nn":
        a_spec = pl.BlockSpec((tm, tk), lambda i, j, kk: (i, kk))
        b_spec = pl.BlockSpec((tk, tn), lambda i, j, kk: (kk, j))
        dn = NN
    elif mode == "nt":
        a_spec = pl.BlockSpec((tm, tk), lambda i, j, kk: (i, kk))
        b_spec = pl.BlockSpec((tn, tk), lambda i, j, kk: (j, kk))
        dn = NT
    else:
        a_spec = pl.BlockSpec((tk, tm), lambda i, j, kk: (kk, i))
        b_spec = pl.BlockSpec((tk, tn), lambda i, j, kk: (kk, j))
        dn = TN
    npairs = sum(len(g) for g in groups)
    nacc, nex, nout = len(groups), len(extras), len(out_dtypes)
    in_specs, args = [], []
    for g in groups:
        for a, b in g:
            in_specs += [a_spec, b_spec]
            args += [a, b]
    for arr, kind in extras:
        if kind == "mn":
            in_specs.append(pl.BlockSpec((tm, tn), lambda i, j, kk: (i, j)))
        else:
            in_specs.append(pl.BlockSpec((1, tn), lambda i, j, kk: (0, j)))
        args.append(arr)
    nci = len(comm.ins) if comm else 0
    nco = len(comm.out_shapes) if comm else 0
    grid = (m // tm, n // tn, nk)

    def body(*refs):
        ins = refs[: 2 * npairs]
        ex = refs[2 * npairs: 2 * npairs + nex]
        c_ins = refs[2 * npairs + nex: 2 * npairs + nex + nci]
        o0 = 2 * npairs + nex + nci
        outs = refs[o0: o0 + nout]
        c_outs = refs[o0 + nout: o0 + nout + nco]
        accs = refs[o0 + nout + nco: o0 + nout + nco + nacc]
        c_sems = refs[o0 + nout + nco + nacc:]
        kk = pl.program_id(2)
        if comm:
            step = (pl.program_id(0) * grid[1] + pl.program_id(1)) * nk + kk

            @pl.when(step == 0)
            def _():
                comm.run("start", c_ins, c_outs, c_sems)

        if nk == 1 and mode != "tn":
            nsub = MM_EPILOGUE_CHUNKS if tn % (MM_EPILOGUE_CHUNKS * LANES) == 0 else 1
            w = tn // nsub
            for cidx in range(nsub):
                lo = cidx * w
                sums, p = [], 0
                for g in groups:
                    t = None
                    for _ in g:
                        b = ins[2 * p + 1][lo:lo + w, :] if mode == "nt" else ins[2 * p + 1][:, lo:lo + w]
                        d = _dotb(ins[2 * p][...], b, dn)
                        t = d if t is None else t + d
                        p += 1
                    sums.append(t)
                res = epilogue(sums, [e[:, lo:lo + w] for e in ex])
                for o, r in zip(outs, res):
                    o[:, lo:lo + w] = r.astype(o.dtype)
        else:
            @pl.when(kk == 0)
            def _():
                for acc in accs:
                    acc[...] = jnp.zeros_like(acc)

            p = 0
            for gi, g in enumerate(groups):
                t = None
                for _ in g:
                    d = _dotb(ins[2 * p][...], ins[2 * p + 1][...], dn)
                    t = d if t is None else t + d
                    p += 1
                accs[gi][...] += t

            @pl.when(kk == nk - 1)
            def _():
                res = epilogue([acc[...] for acc in accs], [e[...] for e in ex])
                for o, r in zip(outs, res):
                    o[...] = r.astype(o.dtype)

        if comm:
            @pl.when(step == grid[0] * grid[1] * nk - 1)
            def _():
                comm.run("finish", c_ins, c_outs, c_sems)

    any_spec = pl.BlockSpec(memory_space=pl.ANY)
    res = pl.pallas_call(
        body, name=name, grid=grid,
        in_specs=in_specs + [any_spec] * nci,
        out_specs=[pl.BlockSpec((tm, tn), lambda i, j, kk: (i, j)) for _ in out_dtypes] + [any_spec] * nco,
        out_shape=[jax.ShapeDtypeStruct((m, n), dt) for dt in out_dtypes] + (list(comm.out_shapes) if comm else []),
        scratch_shapes=[pltpu.VMEM((tm, tn), f32) for _ in range(nacc)] + (comm.sems() if comm else []),
        compiler_params=_cp(*(("arbitrary",) * 3 if comm else ("parallel", "parallel", "arbitrary"))),
    )(*args, *(comm.ins if comm else []))
    return (res[:nout], res[nout:]) if comm else res


def _ep_plain(accs, ex):
    return (accs[0],)


def _ep_colscale(accs, ex):
    return (accs[0] * ex[0],)


def _ep_swiglu(accs, ex):
    gate, up = accs
    act = gate * _sigmoid(gate) * up
    return gate, up, act


def _ep_residual(accs, ex):
    x, gs = ex
    y = accs[0]
    return x + gs * y, y


def _ep_swiglu_bwd(accs, ex):
    gate, up = ex[0].astype(f32), ex[1].astype(f32)
    dact = accs[0]
    sg = _sigmoid(gate)
    silu = gate * sg
    act = silu * up
    dup = dact * silu
    dgate = dact * up * sg * (1.0 + gate * (1.0 - sg))
    return act, dgate, dup


def _row_tile(s):
    return _tile(s, 256, 8)


def _head_row_tile(s):
    return _tile(s, 1024, 8)


def _ada_in(name, x, g, shift, scale):
    s, d = x.shape
    tm = _row_tile(s)

    def body(x_ref, g_ref, sh_ref, sc_ref, h_ref):
        xv = x_ref[...]
        r = lax.rsqrt(jnp.mean(xv * xv, axis=-1, keepdims=True) + EPS)
        h_ref[...] = (xv * r * g_ref[...] * (1.0 + sc_ref[...]) + sh_ref[...]).astype(h_ref.dtype)

    row = pl.BlockSpec((1, d), lambda i: (0, 0))
    blk = pl.BlockSpec((tm, d), lambda i: (i, 0))
    return pl.pallas_call(body, name=name, grid=(s // tm,), in_specs=[blk, row, row, row], out_specs=blk,
                          out_shape=jax.ShapeDtypeStruct((s, d), bf16), compiler_params=_cp("parallel"))(x, g, shift, scale)


def _ada_bwd(name, x, g, scale, dh, dres):
    s, d = x.shape
    tm = _row_tile(s)

    def body(x_ref, g_ref, sc_ref, dh_ref, dres_ref, dx_ref, dsh_ref, a_ref):
        i = pl.program_id(0)

        @pl.when(i == 0)
        def _():
            dsh_ref[...] = jnp.zeros_like(dsh_ref)
            a_ref[...] = jnp.zeros_like(a_ref)

        xv = x_ref[...]
        dhv = dh_ref[...].astype(f32)
        r = lax.rsqrt(jnp.mean(xv * xv, axis=-1, keepdims=True) + EPS)
        n = xv * r
        dn = dhv * (g_ref[...] * (1.0 + sc_ref[...]))
        dx_ref[...] = dres_ref[...] + r * (dn - n * jnp.mean(dn * n, axis=-1, keepdims=True))
        dsh_ref[...] += jnp.sum(dhv, axis=0, keepdims=True)
        a_ref[...] += jnp.sum(dhv * n, axis=0, keepdims=True)

    row = pl.BlockSpec((1, d), lambda i: (0, 0))
    blk = pl.BlockSpec((tm, d), lambda i: (i, 0))
    return pl.pallas_call(
        body, name=name, grid=(s // tm,), in_specs=[blk, row, row, blk, blk], out_specs=[blk, row, row],
        out_shape=[jax.ShapeDtypeStruct((s, d), f32), jax.ShapeDtypeStruct((1, d), f32), jax.ShapeDtypeStruct((1, d), f32)],
        compiler_params=_cp("arbitrary"))(x, g, scale, dh, dres)


def _gate_bwd(name, dx, y, gs):
    s, d = dx.shape
    tm = _row_tile(s)

    def body(dx_ref, y_ref, gs_ref, dy_ref, dgs_ref):
        i = pl.program_id(0)

        @pl.when(i == 0)
        def _():
            dgs_ref[...] = jnp.zeros_like(dgs_ref)

        dxv = dx_ref[...]
        dy_ref[...] = (dxv * gs_ref[...]).astype(dy_ref.dtype)
        dgs_ref[...] += jnp.sum(dxv * y_ref[...].astype(f32), axis=0, keepdims=True)

    row = pl.BlockSpec((1, d), lambda i: (0, 0))
    blk = pl.BlockSpec((tm, d), lambda i: (i, 0))
    return pl.pallas_call(
        body, name=name, grid=(s // tm,), in_specs=[blk, blk, row], out_specs=[blk, row],
        out_shape=[jax.ShapeDtypeStruct((s, d), bf16), jax.ShapeDtypeStruct((1, d), f32)],
        compiler_params=_cp("arbitrary"))(dx, y, gs)


def _final_loss(x, fg, target):
    s, d = x.shape
    tm = _row_tile(s)

    def body(x_ref, g_ref, t_ref, loss_ref, dx_ref, dg_ref):
        i = pl.program_id(0)

        @pl.when(i == 0)
        def _():
            loss_ref[...] = jnp.zeros_like(loss_ref)
            dg_ref[...] = jnp.zeros_like(dg_ref)

        xv = x_ref[...]
        gv = g_ref[...]
        r = lax.rsqrt(jnp.mean(xv * xv, axis=-1, keepdims=True) + EPS)
        n = xv * r
        e = n * gv - t_ref[...]
        per_tok = jnp.mean(e * e, axis=-1, keepdims=True)
        loss_ref[...] += 0.5 * jnp.sum(per_tok, axis=0, keepdims=True) * jnp.ones((1, LANES), f32)
        dy = e * (1.0 / d)
        dg_ref[...] += jnp.sum(dy * n, axis=0, keepdims=True)
        dn = dy * gv
        dx_ref[...] = r * (dn - n * jnp.mean(dn * n, axis=-1, keepdims=True))

    row = pl.BlockSpec((1, d), lambda i: (0, 0))
    blk = pl.BlockSpec((tm, d), lambda i: (i, 0))
    return pl.pallas_call(
        body, name="final_loss", grid=(s // tm,), in_specs=[blk, row, blk],
        out_specs=[pl.BlockSpec((1, LANES), lambda i: (0, 0)), blk, row],
        out_shape=[jax.ShapeDtypeStruct((1, LANES), f32), jax.ShapeDtypeStruct((s, d), f32), jax.ShapeDtypeStruct((1, d), f32)],
        compiler_params=_cp("arbitrary"))(x, fg, target)


def _small_fwd(ps, prm, nh):
    s = ps.shape[0]
    tb = LANES

    def body(ps_ref, prm_ref, sm_ref, cum_ref, carry):
        i = pl.program_id(0)

        @pl.when(i == 0)
        def _():
            carry[...] = jnp.zeros_like(carry)

        x = ps_ref[...]
        lane = lax.broadcasted_iota(jnp.int32, x.shape, 1)
        fb, dtb, alog = prm_ref[0:1, :], prm_ref[1:2, :], prm_ref[2:3, :]
        logf = -_softplus(-(x + fb))
        glog = -jnp.exp(alog) * _softplus(x + dtb)
        beta = _sigmoid(x)
        sm = jnp.where(lane < nh, logf, jnp.where(lane < 2 * nh, glog, jnp.where(lane < 3 * nh, beta, 0.0)))
        sm_ref[...] = sm
        r = lax.broadcasted_iota(jnp.int32, (tb, tb), 0)
        c = lax.broadcasted_iota(jnp.int32, (tb, tb), 1)
        tril = (c <= r).astype(f32)
        cs = _doth(tril, sm) + carry[...]
        cum_ref[...] = cs
        carry[...] = cs[tb - 1:tb, :]

    blk = pl.BlockSpec((tb, LANES), lambda i: (i, 0))
    return pl.pallas_call(
        body, name="small_fwd", grid=(s // tb,),
        in_specs=[blk, pl.BlockSpec((8, LANES), lambda i: (0, 0))],
        out_specs=[blk, blk],
        out_shape=[jax.ShapeDtypeStruct((s, LANES), f32), jax.ShapeDtypeStruct((s, LANES), f32)],
        scratch_shapes=[pltpu.VMEM((1, LANES), f32)],
        compiler_params=_cp("arbitrary"))(ps, prm)


def _small_bwd(ps, prm, dsm, dcum, nh):
    s = ps.shape[0]
    tb = LANES
    nb = s // tb

    def body(ps_ref, prm_ref, dsm_ref, dct_ref, dps_ref, pg_ref, carry):
        i = pl.program_id(0)

        @pl.when(i == 0)
        def _():
            carry[...] = jnp.zeros_like(carry)
            pg_ref[...] = jnp.zeros_like(pg_ref)

        x = ps_ref[...]
        dsm = dsm_ref[...]
        lane = lax.broadcasted_iota(jnp.int32, x.shape, 1)
        fb, dtb, alog = prm_ref[0:1, :], prm_ref[1:2, :], prm_ref[2:3, :]
        r = lax.broadcasted_iota(jnp.int32, (tb, tb), 0)
        c = lax.broadcasted_iota(jnp.int32, (tb, tb), 1)
        triu = (c >= r).astype(f32)
        dlogf = _doth(triu, dct_ref[...]) + carry[...]
        carry[...] = dlogf[0:1, :]
        d_f = dlogf * _sigmoid(-(x + fb))
        nega = -jnp.exp(alog)
        xa = x + dtb
        glog = nega * _softplus(xa)
        d_a = dsm * nega * _sigmoid(xa)
        beta = _sigmoid(x)
        d_b = dsm * beta * (1.0 - beta)
        dps = jnp.where(lane < nh, d_f, jnp.where(lane < 2 * nh, d_a, jnp.where(lane < 3 * nh, d_b, 0.0)))
        dps_ref[...] = dps.astype(dps_ref.dtype)
        row0 = jnp.sum(dps, axis=0, keepdims=True)
        row1 = jnp.sum(jnp.where((lane >= nh) & (lane < 2 * nh), dsm * glog, 0.0), axis=0, keepdims=True)
        sub = lax.broadcasted_iota(jnp.int32, (8, LANES), 0)
        pg_ref[...] += jnp.where(sub == 0, row0, jnp.where(sub == 1, row1, 0.0))

    rev = pl.BlockSpec((tb, LANES), lambda i: (nb - 1 - i, 0))
    fix = pl.BlockSpec((8, LANES), lambda i: (0, 0))
    return pl.pallas_call(
        body, name="small_bwd", grid=(nb,),
        in_specs=[rev, fix, rev, rev],
        out_specs=[rev, fix],
        out_shape=[jax.ShapeDtypeStruct((s, LANES), bf16), jax.ShapeDtypeStruct((8, LANES), f32)],
        scratch_shapes=[pltpu.VMEM((1, LANES), f32)],
        compiler_params=_cp("arbitrary"))(ps, prm, dsm, dcum)


LOG2E = 1.4426950408889634
LN2 = 0.6931471805599453
AUG = 2 * HEAD_DIM
FOX_Q_SCALE = LOG2E / math.sqrt(HEAD_DIM)
FOX_KEY_GROUP = 8


def _split3(col):
    hi = col.astype(bf16).astype(f32)
    r1 = col - hi
    mid = r1.astype(bf16).astype(f32)
    lo = (r1 - mid).astype(bf16).astype(f32)
    return hi, mid, lo


def _aug_block(rows, terms, terms_at, ones_at=None):
    lane = lax.broadcasted_iota(jnp.int32, (rows, LANES), 1)
    blk = jnp.zeros((rows, LANES), f32) if ones_at is None else jnp.where((lane >= ones_at) & (lane < ones_at + 3), 1.0, 0.0)
    for i, t in enumerate(terms):
        blk = jnp.where(lane == terms_at + i, t, blk)
    return blk


def _fox_aug(qkv, cum, nh):
    s = qkv.shape[0]
    tm = _head_row_tile(s)

    def body(q_ref, k_ref, v_ref, cum_ref, qa_ref, ka_ref, va_ref):
        h = pl.program_id(1)
        c2 = _lane_col(cum_ref[...], h) * LOG2E
        hi, mid, lo = _split3(c2)
        qa_ref[:, :HEAD_DIM] = q_ref[...]
        qa_ref[:, HEAD_DIM:] = _aug_block(tm, (hi, mid, lo), 0, 3).astype(bf16)
        ka_ref[:, :HEAD_DIM] = k_ref[...]
        ka_ref[:, HEAD_DIM:] = _aug_block(tm, (-hi, -mid, -lo), 3, 0).astype(bf16)
        va_ref[:, :HEAD_DIM] = v_ref[...]
        va_ref[:, HEAD_DIM:] = _aug_block(tm, (), 0, 0).astype(bf16)

    ab = pl.BlockSpec((tm, AUG), lambda i, h: (i, h))
    return pl.pallas_call(
        body, name="fox_aug", grid=(s // tm, nh),
        in_specs=[pl.BlockSpec((tm, HEAD_DIM), lambda i, h: (i, h)), pl.BlockSpec((tm, HEAD_DIM), lambda i, h: (i, nh + h)),
                  pl.BlockSpec((tm, HEAD_DIM), lambda i, h: (i, 2 * nh + h)), pl.BlockSpec((tm, LANES), lambda i, h: (i, 0))],
        out_specs=[ab, ab, ab], out_shape=[jax.ShapeDtypeStruct((s, nh * AUG), bf16)] * 3,
        compiler_params=_cp("parallel", "parallel"))(qkv, qkv, qkv, cum)


def _fox_fwd(qa, ka, qkv, wn, nh, tq):
    s = qa.shape[0]
    fw = nh * HEAD_DIM
    group = FOX_KEY_GROUP
    while group > s // tq:
        group //= 2

    def body(qa_ref, ka_ref, v_ref, wn_ref, o_ref, on_ref, lse_ref):
        i = pl.program_id(1)
        q = qa_ref[...]

        def logits_t(j, rows):
            return _dotb(ka_ref[pl.ds(pl.multiple_of(j * tq, tq), rows), :], q, NT)

        def pv_t(j, p_t):
            return _dotb(v_ref[pl.ds(pl.multiple_of(j * tq, tq), p_t.shape[0]), :], p_t, TN)

        def update(j0, blocks, carry):
            m, l, acc = carry
            t = logits_t(j0, blocks * tq)
            mn = jnp.maximum(m, jnp.max(t, axis=0, keepdims=True))
            p_t = jnp.exp2(t - mn)
            alpha = jnp.exp2(m - mn)
            return mn, alpha * l + jnp.sum(p_t, axis=0, keepdims=True), alpha * acc + pv_t(j0, p_t)

        key = lax.broadcasted_iota(jnp.int32, (tq, tq), 0)
        qry = lax.broadcasted_iota(jnp.int32, (tq, tq), 1)
        t = jnp.where(key <= qry, logits_t(i, tq), NEG)
        m = jnp.max(t, axis=0, keepdims=True)
        p_t = jnp.exp2(t - m)
        carry = (m, jnp.sum(p_t, axis=0, keepdims=True), pv_t(i, p_t))
        carry = lax.fori_loop(0, i // group, lambda jj, c: update(group * jj, group, c), carry)
        start, part = group * (i // group), group // 2
        while part:
            has = lax.rem(i // part, 2)
            carry = lax.fori_loop(0, has, functools.partial(lambda _, c, j0, blocks: update(j0, blocks, c), j0=start, blocks=part), carry)
            start, part = start + part * has, part // 2
        m, l, acc = carry
        o = (acc / l).T
        o_ref[...] = o
        sub = lax.broadcasted_iota(jnp.int32, (LANES, tq), 0)
        lse_ref[0] = jnp.where(sub == 0, m + jnp.log2(l), 0.0).T
        r = lax.rsqrt(jnp.mean(o * o, axis=-1, keepdims=True) + EPS)
        on_ref[...] = (o * r * wn_ref[...]).astype(on_ref.dtype)

    hb = pl.BlockSpec((tq, HEAD_DIM), lambda h, i: (i, h))
    return pl.pallas_call(
        body, name="fox_fwd", grid=(nh, s // tq),
        in_specs=[pl.BlockSpec((tq, AUG), lambda h, i: (i, h)), pl.BlockSpec((s, AUG), lambda h, i: (0, h)),
                  pl.BlockSpec((s, HEAD_DIM), lambda h, i: (0, 2 * nh + h)), pl.BlockSpec((1, HEAD_DIM), lambda h, i: (0, 0))],
        out_specs=[hb, hb, pl.BlockSpec((1, tq, LANES), lambda h, i: (h, i, 0))],
        out_shape=[jax.ShapeDtypeStruct((s, fw), f32), jax.ShapeDtypeStruct((s, fw), bf16), jax.ShapeDtypeStruct((nh, s, LANES), f32)],
        compiler_params=_cp("parallel", "parallel"))(qa, ka, qkv, wn)


def _fox_post_bwd(don, o, lse2, cum, qkv, wn, nh):
    s, fw = o.shape
    tm = _head_row_tile(s)

    def body(don_ref, o_ref, lse_ref, cum_ref, q_ref, wn_ref, qb_ref, doa_ref, qbt_ref, dot_ref, dwn_ref):
        i = pl.program_id(0)
        h = pl.program_id(1)

        @pl.when((i == 0) & (h == 0))
        def _():
            dwn_ref[...] = jnp.zeros_like(dwn_ref)

        o = o_ref[...]
        don = don_ref[...].astype(f32)
        r = lax.rsqrt(jnp.mean(o * o, axis=-1, keepdims=True) + EPS)
        n = o * r
        dwn_ref[...] += jnp.sum(don * n, axis=0, keepdims=True)
        dn = don * wn_ref[...]
        do = r * (dn - n * jnp.mean(dn * n, axis=-1, keepdims=True))
        delta = jnp.sum(do * o, axis=-1, keepdims=True)
        a2 = _lane_col(cum_ref[...], h) * LOG2E - _lane_col(lse_ref[0], 0)
        q_aug = _aug_block(tm, _split3(a2), 0, 3)
        q = q_ref[...]
        qb_ref[:, :HEAD_DIM] = q
        qb_ref[:, HEAD_DIM:] = q_aug.astype(bf16)
        doa_ref[:, :HEAD_DIM] = do.astype(bf16)
        doa_ref[:, HEAD_DIM:] = _aug_block(tm, _split3(-delta), 0).astype(bf16)
        qbt_ref[:HEAD_DIM, :] = q.astype(f32).T.astype(bf16)
        qbt_ref[HEAD_DIM:, :] = q_aug.T.astype(bf16)
        dot_ref[...] = do.T.astype(bf16)

    hb = pl.BlockSpec((tm, HEAD_DIM), lambda i, h: (i, h))
    ab = pl.BlockSpec((tm, AUG), lambda i, h: (i, h))
    return pl.pallas_call(
        body, name="fox_post_bwd", grid=(s // tm, nh),
        in_specs=[hb, hb, pl.BlockSpec((1, tm, LANES), lambda i, h: (h, i, 0)), pl.BlockSpec((tm, LANES), lambda i, h: (i, 0)),
                  hb, pl.BlockSpec((1, HEAD_DIM), lambda i, h: (0, 0))],
        out_specs=[ab, ab, pl.BlockSpec((AUG, tm), lambda i, h: (h, i)), pl.BlockSpec((HEAD_DIM, tm), lambda i, h: (h, i)),
                   pl.BlockSpec((1, HEAD_DIM), lambda i, h: (0, 0))],
        out_shape=[jax.ShapeDtypeStruct((s, nh * AUG), bf16), jax.ShapeDtypeStruct((s, nh * AUG), bf16),
                   jax.ShapeDtypeStruct((nh * AUG, s), bf16), jax.ShapeDtypeStruct((nh * HEAD_DIM, s), bf16),
                   jax.ShapeDtypeStruct((1, HEAD_DIM), f32)],
        compiler_params=_cp("arbitrary", "arbitrary"))(don, o, lse2, cum, qkv, wn)


def _fox_bwd(qb, doa, qb_t, do_t, ka, va, nh, tq):
    s = qb.shape[0]
    nq = s // tq
    fw = nh * HEAD_DIM

    def body(qb_ref, doa_ref, qbt_ref, dot_ref, ka_ref, va_ref, dqx_ref, dkx_ref, dv_ref, dk_acc, dv_acc):
        j = pl.program_id(1)

        @pl.when(j == 0)
        def _():
            dqx_ref[...] = jnp.zeros_like(dqx_ref)

        kj = ka_ref[...]
        vj = va_ref[...]

        def tile(i, rows=tq, first=False, keep=None):
            off = pl.multiple_of(i * tq, tq)
            p = jnp.exp2(_dotb(qb_ref[pl.ds(off, rows), :], kj, NT))
            if keep is not None:
                p = jnp.where(keep, p, 0.0)
            ds = (p * _dotb(doa_ref[pl.ds(off, rows), :], vj, NT)).astype(bf16)
            dv = _dotb(dot_ref[:, pl.ds(off, rows)], p)
            dk = _dotb(qbt_ref[:, pl.ds(off, rows)], ds)
            if first:
                dv_acc[...] = dv
                dk_acc[...] = dk
            else:
                dv_acc[...] += dv
                dk_acc[...] += dk
            dqx_ref[pl.ds(off, rows), :] += _dotb(ds, kj)

        n = nq - 1 - j
        b0 = jnp.minimum(j, nq - 2)
        qpos = b0 * tq + lax.broadcasted_iota(jnp.int32, (2 * tq, tq), 0)
        kpos = j * tq + lax.broadcasted_iota(jnp.int32, (2 * tq, tq), 1)
        tile(b0, 2 * tq, first=True, keep=(kpos <= qpos) & ((qpos < (j + 1) * tq) | (lax.rem(n, 2) == 1)))

        first_pair = j + 1 + lax.rem(n, 2)
        pairs = n // 2

        def pair(ii, carry):
            tile(first_pair, 2 * tq)
            return carry

        def quad(ii, carry):
            tile(first_pair + 2 * lax.rem(pairs, 2) + 4 * ii, 4 * tq)
            return carry

        lax.fori_loop(0, lax.rem(pairs, 2), pair, 0)
        lax.fori_loop(0, pairs // 2, quad, 0)
        dkx_ref[...] = dk_acc[...].T
        dv_ref[...] = dv_acc[...].T.astype(dv_ref.dtype)

    panel = pl.BlockSpec((s, AUG), lambda h, j: (0, h))
    blk = pl.BlockSpec((tq, AUG), lambda h, j: (j, h))
    return pl.pallas_call(
        body, name="fox_bwd", grid=(nh, nq),
        in_specs=[panel, panel, pl.BlockSpec((AUG, s), lambda h, j: (h, 0)), pl.BlockSpec((HEAD_DIM, s), lambda h, j: (h, 0)),
                  blk, blk],
        out_specs=[panel, blk, pl.BlockSpec((tq, HEAD_DIM), lambda h, j: (j, h))],
        out_shape=[jax.ShapeDtypeStruct((s, nh * AUG), f32), jax.ShapeDtypeStruct((s, nh * AUG), f32),
                   jax.ShapeDtypeStruct((s, fw), bf16)],
        scratch_shapes=[pltpu.VMEM((AUG, tq), f32), pltpu.VMEM((HEAD_DIM, tq), f32)],
        compiler_params=_cp("parallel", "arbitrary"))(qb, doa, qb_t, do_t, ka, va)


def _fox_unpack(dqx, dkx, nh):
    s = dqx.shape[0]
    fw = nh * HEAD_DIM
    tm = _head_row_tile(s)

    def body(dqx_ref, dkx_ref, dq_ref, dk_ref, dcum_ref):
        h = pl.program_id(1)

        @pl.when(h == 0)
        def _():
            dcum_ref[...] = jnp.zeros_like(dcum_ref)

        dq_ref[...] = (dqx_ref[:, :HEAD_DIM] * (HEAD_DIM ** -0.5)).astype(dq_ref.dtype)
        dk_ref[...] = (dkx_ref[:, :HEAD_DIM] * LN2).astype(dk_ref.dtype)
        d = _lane_col(dqx_ref[:, HEAD_DIM:], 0) - _lane_col(dkx_ref[:, HEAD_DIM:], 3)
        lane = lax.broadcasted_iota(jnp.int32, (tm, LANES), 1)
        dcum_ref[...] += jnp.where(lane == h, d, 0.0)

    ab = pl.BlockSpec((tm, AUG), lambda i, h: (i, h))
    hb = pl.BlockSpec((tm, HEAD_DIM), lambda i, h: (i, h))
    return pl.pallas_call(
        body, name="fox_unpack", grid=(s // tm, nh), in_specs=[ab, ab],
        out_specs=[hb, hb, pl.BlockSpec((tm, LANES), lambda i, h: (i, 0))],
        out_shape=[jax.ShapeDtypeStruct((s, fw), bf16), jax.ShapeDtypeStruct((s, fw), bf16), jax.ShapeDtypeStruct((s, LANES), f32)],
        compiler_params=_cp("parallel", "arbitrary"))(dqx, dkx)


def _conv_pre(xx, w, tm):
    pre = None
    for k in range(CONV_W):
        sh = CONV_W - 1 - k
        t = (pltpu.roll(xx, sh, 0) if sh else xx)[8:, :] * w[k:k + 1, :]
        pre = t if pre is None else pre + t
    return pre


def _gdn_pre(x, w, nh):
    s, cw = x.shape
    tm = _row_tile(s)
    fw = nh * HEAD_DIM

    def body(x_ref, prev_ref, w_ref, y_ref):
        i = pl.program_id(0)
        j = pl.program_id(1)
        for h in range(nh):
            sl = slice(h * HEAD_DIM, (h + 1) * HEAD_DIM)
            prev = jnp.where(i == 0, 0.0, prev_ref[:, sl])
            pre = _conv_pre(jnp.concatenate([prev, x_ref[:, sl]], axis=0), w_ref[:, sl], tm)
            y = pre * _sigmoid(pre)
            yn = y * lax.rsqrt(jnp.sum(y * y, axis=-1, keepdims=True) + EPS)
            y_ref[:, sl] = jnp.where(j < 2, yn, y)

    return pl.pallas_call(
        body, name="gdn_pre", grid=(s // tm, cw // fw),
        in_specs=[pl.BlockSpec((tm, fw), lambda i, j: (i, j)),
                  pl.BlockSpec((8, fw), lambda i, j: (jnp.maximum(i * (tm // 8) - 1, 0), j)),
                  pl.BlockSpec((CONV_W, fw), lambda i, j: (0, j))],
        out_specs=pl.BlockSpec((tm, fw), lambda i, j: (i, j)),
        out_shape=jax.ShapeDtypeStruct((s, cw), f32),
        compiler_params=_cp("parallel", "parallel"))(x, x, w)


def _gdn_pre_bwd(x, w, dyn, nh):
    s, cw = x.shape
    tm = _row_tile(s)
    fw = nh * HEAD_DIM

    def body(x_ref, prev_ref, w_ref, dyn_ref, dpre_ref):
        i = pl.program_id(0)
        j = pl.program_id(1)
        for h in range(nh):
            sl = slice(h * HEAD_DIM, (h + 1) * HEAD_DIM)
            prev = jnp.where(i == 0, 0.0, prev_ref[:, sl])
            pre = _conv_pre(jnp.concatenate([prev, x_ref[:, sl]], axis=0), w_ref[:, sl], tm)
            sg = _sigmoid(pre)
            y = pre * sg
            dyn = dyn_ref[:, sl]
            r = lax.rsqrt(jnp.sum(y * y, axis=-1, keepdims=True) + EPS)
            yn = y * r
            dy_n = r * (dyn - yn * jnp.sum(dyn * yn, axis=-1, keepdims=True))
            dy = jnp.where(j < 2, dy_n, dyn)
            dpre_ref[:, sl] = dy * sg * (1.0 + pre * (1.0 - sg))

    hb = pl.BlockSpec((tm, fw), lambda i, j: (i, j))
    return pl.pallas_call(
        body, name="gdn_pre_bwd", grid=(s // tm, cw // fw),
        in_specs=[hb, pl.BlockSpec((8, fw), lambda i, j: (jnp.maximum(i * (tm // 8) - 1, 0), j)),
                  pl.BlockSpec((CONV_W, fw), lambda i, j: (0, j)), hb],
        out_specs=hb, out_shape=jax.ShapeDtypeStruct((s, cw), f32),
        compiler_params=_cp("parallel", "parallel"))(x, x, w, dyn)


def _conv_bwd(x, w, dpre, nh):
    s, cw = x.shape
    tm = _row_tile(s)
    fw = nh * HEAD_DIM
    ni = s // tm

    def body(x_ref, prev_ref, w_ref, dp_ref, nxt_ref, dx_ref, dw_ref):
        i = pl.program_id(1)

        @pl.when(i == 0)
        def _():
            dw_ref[...] = jnp.zeros_like(dw_ref)

        for h in range(nh):
            sl = slice(h * HEAD_DIM, (h + 1) * HEAD_DIM)
            wv = w_ref[:, sl]
            dp = dp_ref[:, sl]
            nxt = jnp.where(i == ni - 1, 0.0, nxt_ref[:, sl])
            dd = jnp.concatenate([dp, nxt], axis=0)
            prev = jnp.where(i == 0, 0.0, prev_ref[:, sl])
            xx = jnp.concatenate([prev, x_ref[:, sl]], axis=0)
            dx = None
            rows = []
            for k in range(CONV_W):
                sh = CONV_W - 1 - k
                t = (pltpu.roll(dd, tm + 8 - sh, 0) if sh else dd)[:tm, :] * wv[k:k + 1, :]
                dx = t if dx is None else dx + t
                xs = (pltpu.roll(xx, sh, 0) if sh else xx)[8:, :]
                rows.append(jnp.sum(dp * xs, axis=0, keepdims=True))
            dx_ref[:, sl] = dx.astype(dx_ref.dtype)
            dw_ref[:, sl] += jnp.concatenate(rows, axis=0)

    hb = pl.BlockSpec((tm, fw), lambda j, i: (i, j))
    wb = pl.BlockSpec((CONV_W, fw), lambda j, i: (0, j))
    return pl.pallas_call(
        body, name="conv_bwd", grid=(cw // fw, ni),
        in_specs=[hb, pl.BlockSpec((8, fw), lambda j, i: (jnp.maximum(i * (tm // 8) - 1, 0), j)), wb, hb,
                  pl.BlockSpec((8, fw), lambda j, i: (jnp.minimum((i + 1) * (tm // 8), s // 8 - 1), j))],
        out_specs=[hb, wb],
        out_shape=[jax.ShapeDtypeStruct((s, cw), bf16), jax.ShapeDtypeStruct((CONV_W, cw), f32)],
        compiler_params=_cp("parallel", "arbitrary"))(x, x, w, dpre, dpre)


def _chunk_consts():
    c = GDN_CHUNK
    r = lax.broadcasted_iota(jnp.int32, (c, c), 0)
    q = lax.broadcasted_iota(jnp.int32, (c, c), 1)
    return r >= q, r > q, (r == q).astype(f32)


def _chunk_head(qkvn, sm, gcs, gcs_t, h, nh):
    fw = nh * HEAD_DIM
    q = qkvn[:, h * HEAD_DIM:(h + 1) * HEAD_DIM] * (HEAD_DIM ** -0.5)
    k = qkvn[:, fw + h * HEAD_DIM: fw + (h + 1) * HEAD_DIM]
    v = qkvn[:, 2 * fw + h * HEAD_DIM: 2 * fw + (h + 1) * HEAD_DIM]
    beta = _lane_col(sm, 2 * nh + h)
    gc = _lane_col(gcs, nh + h)
    gc_row = gcs_t[nh + h: nh + h + 1, :]
    incl, strict, _ = _chunk_consts()
    decay = jnp.where(incl, jnp.exp(jnp.minimum(gc - gc_row, 0.0)), 0.0)
    eg = jnp.exp(gc)
    g_last = gc[GDN_CHUNK - 1:GDN_CHUNK, :]
    egl = jnp.exp(g_last)
    ekd = jnp.exp(g_last - gc)
    kb = k * beta
    vb = v * beta
    kk = _dotb(kb, k, NT)
    qk = _dotb(q, k, NT)
    return dict(q=q, k=k, v=v, beta=beta, gc=gc, decay=decay, eg=eg, egl=egl, ekd=ekd, kb=kb, vb=vb, kk=kk, qk=qk,
                incl=incl, strict=strict)


def _unit_lower_inverses(lows, eye):
    ps = [-low for low in lows]
    ts = [eye + p for p in ps]
    for _ in range(5):
        ps = [_dotm(p, p) for p in ps]
        ts = [t + _dotm(t, p) for t, p in zip(ts, ps)]
    return ts


def _gdn_fwd(qkvn, sm, z, wn, nh):
    s = qkvn.shape[0]
    c = GDN_CHUNK
    nc = s // c
    fw = nh * HEAD_DIM

    def body(qkvn_ref, sm_ref, z_ref, wn_ref, on_ref, o_ref, st_ref, ti_ref, state):
        ci = pl.program_id(0)

        @pl.when(ci == 0)
        def _():
            state[...] = jnp.zeros_like(state)

        qkvn_v = qkvn_ref[...]
        sm_v = sm_ref[...]
        incl, strict, eye = _chunk_consts()
        gcs = _doth(incl.astype(f32), sm_v)
        gcs_t = gcs.T
        heads = range(nh)
        es = [_chunk_head(qkvn_v, sm_v, gcs, gcs_t, h, nh) for h in heads]
        tinvs = _unit_lower_inverses([jnp.where(strict, e["kk"] * e["decay"], 0.0) for e in es], eye)
        us = [_dotm(t, e["vb"]) for t, e in zip(tinvs, es)]
        ws = [_dotm(t, e["kb"] * e["eg"]) for t, e in zip(tinvs, es)]
        sts = [state[h] for h in heads]
        v_news = [u - _dotb(w, st) for u, w, st in zip(us, ws, sts)]
        qss = [_dotb(e["q"] * e["eg"], st) for e, st in zip(es, sts)]
        os_ = [qs + _dotb(jnp.where(incl, e["qk"] * e["decay"], 0.0), vn) for qs, e, vn in zip(qss, es, v_news)]
        upd = [_dotb(e["k"] * e["ekd"], vn, TN) for e, vn in zip(es, v_news)]
        for h in heads:
            st_ref[0, h] = sts[h]
            ti_ref[0, h] = tinvs[h]
            state[h] = sts[h] * es[h]["egl"] + upd[h]
            sl = slice(h * HEAD_DIM, (h + 1) * HEAD_DIM)
            o = os_[h]
            o_ref[:, sl] = o
            zz = z_ref[:, sl]
            r = lax.rsqrt(jnp.mean(o * o, axis=-1, keepdims=True) + EPS)
            on_ref[:, sl] = (o * r * wn_ref[...] * (zz * _sigmoid(zz))).astype(on_ref.dtype)

    return pl.pallas_call(
        body, name="gdn_fwd", grid=(nc,),
        in_specs=[pl.BlockSpec((c, 3 * fw), lambda i: (i, 0)), pl.BlockSpec((c, LANES), lambda i: (i, 0)),
                  pl.BlockSpec((c, fw), lambda i: (i, 0)), pl.BlockSpec((1, HEAD_DIM), lambda i: (0, 0))],
        out_specs=[pl.BlockSpec((c, fw), lambda i: (i, 0)), pl.BlockSpec((c, fw), lambda i: (i, 0)),
                   pl.BlockSpec((1, nh, HEAD_DIM, HEAD_DIM), lambda i: (i, 0, 0, 0)),
                   pl.BlockSpec((1, nh, c, c), lambda i: (i, 0, 0, 0))],
        out_shape=[jax.ShapeDtypeStruct((s, fw), bf16), jax.ShapeDtypeStruct((s, fw), f32),
                   jax.ShapeDtypeStruct((nc, nh, HEAD_DIM, HEAD_DIM), f32), jax.ShapeDtypeStruct((nc, nh, c, c), f32)],
        scratch_shapes=[pltpu.VMEM((nh, HEAD_DIM, HEAD_DIM), f32)],
        compiler_params=_cp("arbitrary"))(qkvn, sm, z, wn)


def _gdn_post_bwd(don, o, z, wn, nh):
    s, fw = o.shape
    tm = _head_row_tile(s)

    def body(don_ref, o_ref, z_ref, wn_ref, do_ref, dz_ref, dwn_ref):
        i = pl.program_id(0)
        h = pl.program_id(1)

        @pl.when((i == 0) & (h == 0))
        def _():
            dwn_ref[...] = jnp.zeros_like(dwn_ref)

        o = o_ref[...]
        zz = z_ref[...]
        don = don_ref[...].astype(f32)
        wv = wn_ref[...]
        r = lax.rsqrt(jnp.mean(o * o, axis=-1, keepdims=True) + EPS)
        n = o * r
        sg = _sigmoid(zz)
        silu = zz * sg
        dz_ref[...] = (don * n * wv * sg * (1.0 + zz * (1.0 - sg))).astype(dz_ref.dtype)
        dnw = don * silu
        dwn_ref[...] += jnp.sum(dnw * n, axis=0, keepdims=True)
        dn = dnw * wv
        do_ref[...] = r * (dn - n * jnp.mean(dn * n, axis=-1, keepdims=True))

    hb = pl.BlockSpec((tm, HEAD_DIM), lambda i, h: (i, h))
    wb = pl.BlockSpec((1, HEAD_DIM), lambda i, h: (0, 0))
    return pl.pallas_call(
        body, name="gdn_post_bwd", grid=(s // tm, nh), in_specs=[hb, hb, hb, wb], out_specs=[hb, hb, wb],
        out_shape=[jax.ShapeDtypeStruct((s, fw), f32), jax.ShapeDtypeStruct((s, fw), bf16),
                   jax.ShapeDtypeStruct((1, HEAD_DIM), f32)],
        compiler_params=_cp("arbitrary", "arbitrary"))(don, o, z, wn)


def _gdn_bwd(qkvn, sm, do, states, tinvs, nh):
    s = qkvn.shape[0]
    c = GDN_CHUNK
    nc = s // c
    fw = nh * HEAD_DIM

    def body(qkvn_ref, sm_ref, do_ref, st_ref, ti_ref, dqkvn_ref, dsm_ref, dstate):
        ci = pl.program_id(0)

        @pl.when(ci == 0)
        def _():
            dstate[...] = jnp.zeros_like(dstate)

        qkvn_v = qkvn_ref[...]
        sm_v = sm_ref[...]
        incl, strict, eye = _chunk_consts()
        inclf = incl.astype(f32)
        gcs = _doth(inclf, sm_v)
        gcs_t = gcs.T
        lane = lax.broadcasted_iota(jnp.int32, (c, LANES), 1)
        last_row = lax.broadcasted_iota(jnp.int32, (c, 1), 0) == c - 1
        ones_cl = jnp.ones((c, LANES), f32)
        each = lambda f: [f(h) for h in range(nh)]
        es = each(lambda h: _chunk_head(qkvn_v, sm_v, gcs, gcs_t, h, nh))
        tinv = each(lambda h: ti_ref[0, h])
        st = each(lambda h: st_ref[0, h])
        dst = each(lambda h: dstate[h])
        do = each(lambda h: do_ref[:, h * HEAD_DIM:(h + 1) * HEAD_DIM])
        kg = each(lambda h: es[h]["kb"] * es[h]["eg"])
        qg = each(lambda h: es[h]["q"] * es[h]["eg"])
        kd = each(lambda h: es[h]["k"] * es[h]["ekd"])
        u = each(lambda h: _dotm(tinv[h], es[h]["vb"]))
        w = each(lambda h: _dotm(tinv[h], kg[h]))
        a = each(lambda h: jnp.where(incl, es[h]["qk"] * es[h]["decay"], 0.0))
        v_new = each(lambda h: u[h] - _dotb(w[h], st[h]))
        dv_new = each(lambda h: _dotb(a[h], do[h], TN) + _dotb(kd[h], dst[h]))
        da = each(lambda h: jnp.where(incl, _dotb(do[h], v_new[h], NT), 0.0))
        dqg = each(lambda h: _dotb(do[h], st[h], NT))
        dkd = each(lambda h: _dotb(v_new[h], dst[h], NT))
        dglast = each(lambda h: es[h]["egl"] * jnp.sum(jnp.sum(dst[h] * st[h], axis=1, keepdims=True), axis=0, keepdims=True))
        dw = each(lambda h: -_dotb(dv_new[h], st[h], NT))
        new_dst = each(lambda h: _dotb(qg[h], do[h], TN) + es[h]["egl"] * dst[h] - _dotb(w[h], dv_new[h], TN))
        dtinv = each(lambda h: _dotm(dv_new[h], es[h]["vb"], NT) + _dotm(dw[h], kg[h], NT))
        dvb = each(lambda h: _dotm(tinv[h], dv_new[h], TN))
        dkg = each(lambda h: _dotm(tinv[h], dw[h], TN))
        tdt = each(lambda h: _dotm(tinv[h], dtinv[h], TN))
        dlow = each(lambda h: -_dotm(tdt[h], tinv[h], NT))
        dkk = each(lambda h: jnp.where(strict, dlow[h] * es[h]["decay"], 0.0))
        dqk = each(lambda h: da[h] * es[h]["decay"])
        darg = each(lambda h: (jnp.where(strict, dlow[h] * es[h]["kk"], 0.0) + da[h] * es[h]["qk"]) * es[h]["decay"])
        dgc = each(lambda h: jnp.sum(darg[h], axis=1, keepdims=True) - _doth(darg[h], ones_cl, TN)[:, 0:1])
        dkb = each(lambda h: _dotb(dkk[h], es[h]["k"]) + dkg[h] * es[h]["eg"])
        dk = each(lambda h: _dotb(dkk[h], es[h]["kb"], TN) + _dotb(dqk[h], es[h]["q"], TN) + dkd[h] * es[h]["ekd"]
                  + dkb[h] * es[h]["beta"])
        dq = each(lambda h: (_dotb(dqk[h], es[h]["k"]) + dqg[h] * es[h]["eg"]) * (HEAD_DIM ** -0.5))
        s_kd = each(lambda h: jnp.sum(dkd[h] * kd[h], axis=1, keepdims=True))
        dgc = each(lambda h: dgc[h] + jnp.sum(dkg[h] * kg[h] + dqg[h] * qg[h], axis=1, keepdims=True) - s_kd[h]
                   + jnp.where(last_row, jnp.sum(s_kd[h], axis=0, keepdims=True) + dglast[h], 0.0))
        dg = each(lambda h: _doth(inclf, dgc[h] * ones_cl, TN)[:, 0:1])
        dsm = jnp.zeros((c, LANES), f32)
        for h in range(nh):
            dstate[h] = new_dst[h]
            dbeta = jnp.sum(dkb[h] * es[h]["k"] + dvb[h] * es[h]["v"], axis=1, keepdims=True)
            dqkvn_ref[:, h * HEAD_DIM:(h + 1) * HEAD_DIM] = dq[h]
            dqkvn_ref[:, fw + h * HEAD_DIM: fw + (h + 1) * HEAD_DIM] = dk[h]
            dqkvn_ref[:, 2 * fw + h * HEAD_DIM: 2 * fw + (h + 1) * HEAD_DIM] = dvb[h] * es[h]["beta"]
            dsm = dsm + jnp.where(lane == nh + h, dg[h], 0.0) + jnp.where(lane == 2 * nh + h, dbeta, 0.0)
        dsm_ref[...] = dsm

    rev = lambda i: (nc - 1 - i, 0)
    rev4 = lambda i: (nc - 1 - i, 0, 0, 0)
    return pl.pallas_call(
        body, name="gdn_bwd", grid=(nc,),
        in_specs=[pl.BlockSpec((c, 3 * fw), rev), pl.BlockSpec((c, LANES), rev), pl.BlockSpec((c, fw), rev),
                  pl.BlockSpec((1, nh, HEAD_DIM, HEAD_DIM), rev4), pl.BlockSpec((1, nh, c, c), rev4)],
        out_specs=[pl.BlockSpec((c, 3 * fw), rev), pl.BlockSpec((c, LANES), rev)],
        out_shape=[jax.ShapeDtypeStruct((s, 3 * fw), f32), jax.ShapeDtypeStruct((s, LANES), f32)],
        scratch_shapes=[pltpu.VMEM((nh, HEAD_DIM, HEAD_DIM), f32)],
        compiler_params=_cp("arbitrary"))(qkvn, sm, do, states, tinvs)


MM_TILES = (1024, 512, 2048)
MM_TILES_TN = (512, 1024, 4096)
MM_TILES_F_DEEP = (1024, 512, 2816)
MM_TILES_LONG_K = (1024, 512, 2560)


def _hosted(res, comm):
    return res if comm else (res, None)


def _ffn_fwd(tag, x, g, mod3, w, comm_up=None, comm_down=None, wd_of=None):
    wg_t, wu_t, wd = w
    sh, sc, gt = mod3
    h = _ada_in(tag + "_ada", x, g, sh, sc)
    (gate, up, act), got_up = _hosted(_mm(tag + "_up", [[(h, wg_t)], [(h, wu_t)]], "nt", MM_TILES, _ep_swiglu,
                                          (bf16, bf16, bf16), comm=comm_up), comm_up)
    if wd_of:
        wd = wd_of(got_up)
    (xn, y), got_down = _hosted(_mm(tag + "_down", [[(act, wd)]], "nn", MM_TILES_F_DEEP, _ep_residual, (f32, bf16),
                                    extras=((x, "mn"), (MACARON_W * gt, "n")), comm=comm_down), comm_down)
    return xn, dict(x=x, h=h, gate=gate, up=up, y=y), got_up, got_down


def _ffn_bwd(tag, dxn, res, g, mod3, w, comm_dact=None, comm_dh_of=None):
    wg_t, wu_t, wd = w
    sh, sc, gt = mod3
    dy, dgs = _gate_bwd(tag + "_gate_bwd", dxn, res["y"], MACARON_W * gt)
    (act, dgate, dup), got = _hosted(_mm(tag + "_dact", [[(dy, wd)]], "nt", MM_TILES, _ep_swiglu_bwd, (bf16, bf16, bf16),
                                         extras=((res["gate"], "mn"), (res["up"], "mn")), comm=comm_dact), comm_dact)
    (dwd,) = _mm(tag + "_dwd", [[(act, dy)]], "tn", MM_TILES_TN, _ep_plain, (bf16,))
    (dwg_t,) = _mm(tag + "_dwg", [[(dgate, res["h"])]], "tn", MM_TILES_TN, _ep_plain, (bf16,))
    (dwu_t,) = _mm(tag + "_dwu", [[(dup, res["h"])]], "tn", MM_TILES_TN, _ep_plain, (bf16,))
    comm_dh = comm_dh_of and comm_dh_of((dwg_t, dwu_t, dwd))
    (dh,), got_dh = _hosted(_mm(tag + "_dh", [[(dgate, wg_t), (dup, wu_t)]], "nn", MM_TILES_F_DEEP, _ep_plain, (bf16,), comm=comm_dh),
                            comm_dh)
    dx, dsh, a = _ada_bwd(tag + "_ada_bwd", res["x"], g, sc, dh, dxn)
    return dx, (dwg_t, dwu_t, dwd), (dsh, a * g, MACARON_W * dgs), a * (1.0 + sc), got, got_dh


def _local_step(x, target, mods, norm_g, final_norm, ffn1_w, later_w, prm, fox_wn, gdn_wn, conv_w, nh, hooks=None):
    s, d = x.shape
    fw = nh * HEAD_DIM
    tq = _tile(s, min(256, s // 2))
    g_rows = [norm_g[i:i + 1] for i in range(3)]
    m1, m2, m3 = mods[0:3], mods[3:6], mods[6:9]

    x1, r1, got_up, got_down = _ffn_fwd("ffn1", x, g_rows[0], m1, ffn1_w, hooks and hooks.gather_mix_spec(),
                                        hooks and hooks.gather_ffn2_spec(), hooks and hooks.ffn1_wd)
    if hooks:
        ffn1_w = ffn1_w[:2] + (hooks.ffn1_wd(got_up),)
    w_cat_t, w_out, ffn2_w = hooks.gathered(got_up, got_down) if hooks else later_w
    h2 = _ada_in("mix_ada", x1, g_rows[1], m2[0], m2[1])
    w_fox, w_gdn, w_z, w_s = w_cat_t[:3 * fw], w_cat_t[3 * fw:6 * fw], w_cat_t[6 * fw:7 * fw], w_cat_t[7 * fw:]
    colscale = jnp.concatenate([jnp.full((1, fw), FOX_Q_SCALE, f32), jnp.ones((1, 2 * fw), f32)], axis=1)
    (qkv_f,) = _mm("proj_fox", [[(h2, w_fox)]], "nt", MM_TILES, _ep_colscale, (bf16,), extras=((colscale, "n"),))
    (qkv_g,) = _mm("proj_gdn", [[(h2, w_gdn)]], "nt", MM_TILES, _ep_plain, (f32,))
    (z,) = _mm("proj_z", [[(h2, w_z)]], "nt", MM_TILES, _ep_plain, (f32,))
    (ps,) = _mm("proj_s", [[(h2, w_s)]], "nt", MM_TILES, _ep_plain, (f32,))
    sm, cum = _small_fwd(ps, prm, nh)
    qa, ka, va = _fox_aug(qkv_f, cum, nh)
    o_f, on_f, lse2 = _fox_fwd(qa, ka, qkv_f, fox_wn, nh, tq)
    qkvn = _gdn_pre(qkv_g, conv_w, nh)
    on_g, o_g, states, tinvs = _gdn_fwd(qkvn, sm, z, gdn_wn, nh)
    w_top, w_bot = w_out[:fw], w_out[fw:]
    x2, mix = _mm("mix_out", [[(on_f, w_top), (on_g, w_bot)]], "nn", MM_TILES, _ep_residual, (f32, bf16),
                  extras=((x1, "mn"), (m2[2], "n")))
    x3, r3, _, _ = _ffn_fwd("ffn2", x2, g_rows[2], m3, ffn2_w)
    loss, dx3, dfinal = _final_loss(x3, final_norm, target)

    dx2, dffn2, dmod3, dg3, _, _ = _ffn_bwd("ffn2", dx3, r3, g_rows[2], m3, ffn2_w)
    rs_ffn2 = hooks and hooks.rs_ffn2_spec(dffn2)
    dmix, dgt2 = _gate_bwd("mix_gate_bwd", dx2, mix, m2[2])
    (don_f,) = _mm("mix_dof", [[(dmix, w_top)]], "nt", MM_TILES, _ep_plain, (f32,))
    (don_g,) = _mm("mix_dog", [[(dmix, w_bot)]], "nt", MM_TILES, _ep_plain, (f32,))
    (dw_top,) = _mm("mix_dwtop", [[(on_f, dmix)]], "tn", MM_TILES_TN, _ep_plain, (bf16,))
    (dw_bot,) = _mm("mix_dwbot", [[(on_g, dmix)]], "tn", MM_TILES_TN, _ep_plain, (bf16,))
    qb, doa, qb_t, do_t, dfox_wn = _fox_post_bwd(don_f, o_f, lse2, cum, qkv_f, fox_wn, nh)
    dqx, dkx, dv_f = _fox_bwd(qb, doa, qb_t, do_t, ka, va, nh, tq)
    dq_f, dk_f, dcum = _fox_unpack(dqx, dkx, nh)
    do_g, dz, dgdn_wn = _gdn_post_bwd(don_g, o_g, z, gdn_wn, nh)
    dqkvn, dsm = _gdn_bwd(qkvn, sm, do_g, states, tinvs, nh)
    dpre = _gdn_pre_bwd(qkv_g, conv_w, dqkvn, nh)
    dqkv_g, dconv = _conv_bwd(qkv_g, conv_w, dpre, nh)
    dps, pg = _small_bwd(ps, prm, dsm, dcum, nh)
    dproj = jnp.concatenate([dq_f, dk_f, dv_f, dqkv_g, dz, dps], axis=1)
    ((dw_cat_t,), got_ffn2) = _hosted(_mm("proj_dw", [[(dproj, h2)]], "tn", MM_TILES_TN, _ep_plain, (bf16,), comm=rs_ffn2), rs_ffn2)
    dw_out = jnp.concatenate([dw_top, dw_bot], axis=0)
    rs_mix = hooks and hooks.rs_mix_spec(dw_cat_t, dw_out)
    (dh2,) = _mm("proj_dh", [[(dproj, w_cat_t)]], "nn", MM_TILES_LONG_K, _ep_plain, (bf16,))
    dx1, dsh2, a2 = _ada_bwd("mix_ada_bwd", x1, g_rows[1], m2[1], dh2, dx2)
    dmod2 = (dsh2, a2 * g_rows[1], dgt2)
    dg2 = a2 * (1.0 + m2[1])
    dx0, dffn1, dmod1, dg1, got_mix, got_ffn1 = _ffn_bwd("ffn1", dx1, r1, g_rows[0], m1, ffn1_w, rs_mix,
                                                          hooks and hooks.rs_ffn1_spec)

    big = dict(ffn=(dffn1, dffn2), w_cat_t=dw_cat_t, w_out=dw_out, got_ffn2=got_ffn2, got_mix=got_mix, got_ffn1=got_ffn1)
    small = dict(loss=loss, norm_g=jnp.concatenate([dg1, dg2, dg3], axis=0), final_norm=dfinal, fox_wn=dfox_wn,
                 gdn_wn=dgdn_wn, pg=pg, conv=dconv, mod=jnp.concatenate(list(dmod1) + list(dmod2) + list(dmod3), axis=1))
    return dx0, big, small


def _w_in_row_groups(nh):
    fw = nh * HEAD_DIM
    sizes = [3 * fw, nh, 3 * fw, nh, nh, fw]
    offs = [0]
    for sz in sizes:
        offs.append(offs[-1] + sz)
    return [(offs[i], offs[i + 1]) for i in range(len(sizes))]


def _build_w_cat_t(w_in_t, nh):
    gq, gf, gg, ga, gb, gz = _w_in_row_groups(nh)
    d = w_in_t.shape[1]
    rows = lambda r: w_in_t[r[0]:r[1]]
    pad = jnp.zeros((LANES - 3 * nh, d), w_in_t.dtype)
    return jnp.concatenate([rows(gq), rows(gg), rows(gz), rows(gf), rows(ga), rows(gb), pad], axis=0)


def _split_dw_cat_t(dw_cat_t, nh):
    fw = nh * HEAD_DIM
    o = 7 * fw
    return jnp.concatenate([dw_cat_t[:3 * fw], dw_cat_t[o:o + nh], dw_cat_t[3 * fw:6 * fw], dw_cat_t[o + nh:o + 2 * nh],
                            dw_cat_t[o + 2 * nh:o + 3 * nh], dw_cat_t[6 * fw:7 * fw]], axis=0)


def _head_params(fox_f_bias, gdn_dt_bias, gdn_a_log, nh):
    z = jnp.zeros((8, LANES), f32)
    z = z.at[0, 0:nh].set(fox_f_bias.reshape(nh))
    z = z.at[1, nh:2 * nh].set(gdn_dt_bias.reshape(nh))
    z = z.at[2, nh:2 * nh].set(gdn_a_log.reshape(nh))
    return z


ANY = pl.BlockSpec(memory_space=pl.ANY)
IN_VMEM = pl.BlockSpec(memory_space=pltpu.VMEM)
N_OTHER_CHIPS = 3


def _place():
    x, y, c = lax.axis_index("x"), lax.axis_index("y"), lax.axis_index("c")
    chips = [(1 - x, y), (x, 1 - y), (1 - x, 1 - y)]
    return x, y, c, chips


def _allgather8(name, v):
    r, n = v.shape

    def body(v_ref, out_ref, send_sems, recv_sems, local_sem):
        x, y, c, _ = _place()
        me = 4 * x + 2 * y + c
        mine = pltpu.make_async_copy(v_ref, out_ref.at[me], local_sem)
        mine.start()
        copies = []
        for k in range(1, 8):
            fx, fy, fc = (k >> 2) & 1, (k >> 1) & 1, k & 1
            peer = (x + fx - 2 * x * fx, y + fy - 2 * y * fy, c + fc - 2 * c * fc)
            cp = pltpu.make_async_remote_copy(src_ref=v_ref, dst_ref=out_ref.at[me], send_sem=send_sems.at[k - 1],
                                              recv_sem=recv_sems.at[k - 1], device_id=peer, device_id_type=MESH)
            cp.start()
            copies.append(cp)
        for cp in copies:
            cp.wait()
        mine.wait()

    return pl.pallas_call(
        body, name=name, in_specs=[IN_VMEM], out_specs=IN_VMEM, out_shape=jax.ShapeDtypeStruct((8, r, n), v.dtype),
        scratch_shapes=[pltpu.SemaphoreType.DMA((7,)), pltpu.SemaphoreType.DMA((7,)), pltpu.SemaphoreType.DMA],
        compiler_params=pltpu.CompilerParams(vmem_limit_bytes=VMEM_LIMIT_V7X))(v)


class _CommSpec:
    def __init__(self, ins, out_shapes, sem_counts, run):
        self.ins, self.out_shapes, self.sem_counts, self.run = list(ins), list(out_shapes), sem_counts, run

    def sems(self):
        return [pltpu.SemaphoreType.DMA((n,)) for n in self.sem_counts]


def _run_comm(name, spec):
    ni, no = len(spec.ins), len(spec.out_shapes)

    def body(*refs):
        ins, outs, sems = refs[:ni], refs[ni:ni + no], refs[ni + no:]
        spec.run("start", ins, outs, sems)
        spec.run("finish", ins, outs, sems)

    return pl.pallas_call(body, name=name, in_specs=[ANY] * ni, out_specs=[ANY] * no, out_shape=spec.out_shapes,
                          scratch_shapes=spec.sems())(*spec.ins)


def _gather_spec(halves):
    nw = len(halves)

    def run(phase, ins, outs, sems):
        ici_send, ici_recv, d2d_send, d2d_recv = sems
        x, y, c, chips = _place()
        s = 2 * x + y
        sib = (x, y, 1 - c)

        def over_ici(w, j, dst):
            cx, cy = chips[j]
            return pltpu.make_async_remote_copy(src_ref=ins[w].at[c], dst_ref=dst, send_sem=ici_send.at[w * 3 + j],
                                                recv_sem=ici_recv.at[w * 3 + j], device_id=(cx, cy, c), device_id_type=MESH)

        def to_sibling(w, j, blk):
            return pltpu.make_async_remote_copy(src_ref=blk, dst_ref=blk, send_sem=d2d_send.at[w * 3 + j],
                                                recv_sem=d2d_recv.at[w * 3 + j], device_id=sib, device_id_type=MESH)

        pairs = [(w, j) for w in range(nw) for j in range(N_OTHER_CHIPS)]
        chip_of = lambda j: 2 * chips[j][0] + chips[j][1]
        if phase == "start":
            for w, j in pairs:
                over_ici(w, j, outs[w].at[c, s]).start()
            return
        for w, j in pairs:
            landed = outs[w].at[c, chip_of(j)]
            over_ici(w, j, landed).wait_recv()
            to_sibling(w, j, landed).start()
        for w, j in pairs:
            to_sibling(w, j, outs[w].at[1 - c, chip_of(j)]).wait_recv()
        for w, j in pairs:
            over_ici(w, j, outs[w].at[c, s]).wait_send()
            to_sibling(w, j, outs[w].at[c, chip_of(j)]).wait_send()

    n3 = nw * N_OTHER_CHIPS
    return _CommSpec(halves, [jax.ShapeDtypeStruct((2, 4) + h.shape[1:], h.dtype) for h in halves], [n3] * 4, run)


def _to_chips_spec(partials):
    nw = len(partials)

    def run(phase, ins, outs, sems):
        send_sems, recv_sems = sems
        x, y, c, chips = _place()
        for w in range(nw):
            for j, (cx, cy) in enumerate(chips):
                cp = pltpu.make_async_remote_copy(src_ref=ins[w].at[2 * cx + cy], dst_ref=outs[w].at[j],
                                                  send_sem=send_sems.at[w * 3 + j], recv_sem=recv_sems.at[w * 3 + j],
                                                  device_id=(cx, cy, c), device_id_type=MESH)
                if phase == "start":
                    cp.start()
                else:
                    cp.wait()

    n3 = nw * N_OTHER_CHIPS
    return _CommSpec(partials, [jax.ShapeDtypeStruct((3,) + a.shape[1:], a.dtype) for a in partials], [n3, n3], run)


def _send_to_sibling(name, srcs, other_half):
    nw = len(srcs)

    def body(*refs):
        ins, outs = refs[:nw], refs[nw:2 * nw]
        send_sems, recv_sems = refs[2 * nw:]
        x, y, c, _ = _place()
        cps = []
        for w in range(nw):
            cp = pltpu.make_async_remote_copy(src_ref=ins[w].at[1 - c] if other_half else ins[w], dst_ref=outs[w],
                                              send_sem=send_sems.at[w], recv_sem=recv_sems.at[w],
                                              device_id=(x, y, 1 - c), device_id_type=MESH)
            cp.start()
            cps.append(cp)
        for cp in cps:
            cp.wait()

    return pl.pallas_call(
        body, name=name, in_specs=[ANY] * nw, out_specs=[ANY] * nw,
        out_shape=[jax.ShapeDtypeStruct(a.shape[1:] if other_half else a.shape, a.dtype) for a in srcs],
        scratch_shapes=[pltpu.SemaphoreType.DMA((nw,)), pltpu.SemaphoreType.DMA((nw,))],
    )(*srcs)


def _add_pair(name, g, recv, c):
    _, nchip, r, d = g.shape
    tr = _tile(r, 512, 16)

    def body(c_ref, g_ref, r_ref, o_ref):
        o_ref[...] = (g_ref[...].astype(f32) + r_ref[...].astype(f32)).astype(o_ref.dtype)

    gs = pltpu.PrefetchScalarGridSpec(
        num_scalar_prefetch=1, grid=(nchip, r // tr),
        in_specs=[pl.BlockSpec((None, None, tr, d), lambda t, i, cr: (cr[0], t, i, 0)),
                  pl.BlockSpec((None, tr, d), lambda t, i, cr: (t, i, 0))],
        out_specs=pl.BlockSpec((None, tr, d), lambda t, i, cr: (t, i, 0)))
    return pl.pallas_call(body, name=name, grid_spec=gs, out_shape=jax.ShapeDtypeStruct((nchip, r, d), bf16),
                          compiler_params=_cp("parallel", "parallel"))(c.reshape(1).astype(jnp.int32), g, recv)


def _add_chips(name, p, recv, s_chip):
    _, r, d = p.shape
    tr = _tile(r, 512, 16)

    def body(s_ref, p_ref, r_ref, o_ref):
        o_ref[...] = ((p_ref[...].astype(f32) + r_ref[0].astype(f32)) + r_ref[1].astype(f32)) + r_ref[2].astype(f32)

    gs = pltpu.PrefetchScalarGridSpec(
        num_scalar_prefetch=1, grid=(r // tr,),
        in_specs=[pl.BlockSpec((None, tr, d), lambda i, sr: (sr[0], i, 0)),
                  pl.BlockSpec((3, tr, d), lambda i, sr: (0, i, 0))],
        out_specs=pl.BlockSpec((tr, d), lambda i, sr: (i, 0)))
    return pl.pallas_call(body, name=name, grid_spec=gs, out_shape=jax.ShapeDtypeStruct((r, d), f32),
                          compiler_params=_cp("parallel"))(s_chip.reshape(1).astype(jnp.int32), p, recv)


def _rs_pair_sums(tag, grads, c):
    from_sib = _send_to_sibling("rs_to_sibling_" + tag, grads, True)
    return [_add_pair("rs_add_pair_%s%d" % (tag, n), g, r, c) for n, (g, r) in enumerate(zip(grads, from_sib))]


def _rs_chip_sums(tag, partial, from_chips, s_chip):
    return [_add_chips("rs_add_chips_%s%d" % (tag, n), p, r, s_chip) for n, (p, r) in enumerate(zip(partial, from_chips))]


def _rs_both_halves(mine, c):
    theirs = _send_to_sibling("rs_exchange_halves", mine, False)
    return [jnp.where(c == 0, jnp.stack([a, b]), jnp.stack([b, a])) for a, b in zip(mine, theirs)]


def _sum_devices(v):
    n = v.shape[2]

    def body(v_ref, o_ref):
        t = v_ref[0]
        for k in range(1, 8):
            t = t + v_ref[k]
        o_ref[...] = t

    return pl.pallas_call(body, name="sum_devices", out_shape=jax.ShapeDtypeStruct((1, n), f32))(v)


def _silu_rows(v):
    def body(v_ref, o_ref):
        t = v_ref[...]
        o_ref[...] = t * _sigmoid(t)

    return pl.pallas_call(body, name="silu_cond", out_shape=jax.ShapeDtypeStruct(v.shape, f32))(v)


ADAMW_BLOCK_ELEMS = 600 * 1024


def _adamw(name, w, g, m, v):
    r, cdim = w.shape
    tr = _tile(r, max(8, min(256, (ADAMW_BLOCK_ELEMS // cdim) // 8 * 8)), 8)
    c1 = 1.0 - ADAM_B1 ** ADAM_STEP
    c2 = 1.0 - ADAM_B2 ** ADAM_STEP

    def body(w_ref, g_ref, m_ref, v_ref, d_ref, mo_ref, vo_ref):
        gv = g_ref[...]
        mn = ADAM_B1 * m_ref[...] + (1.0 - ADAM_B1) * gv
        vn = ADAM_B2 * v_ref[...] + (1.0 - ADAM_B2) * (gv * gv)
        d_ref[...] = -ADAM_LR * ((mn / c1) / (jnp.sqrt(vn / c2) + ADAM_EPS) + ADAM_WD * w_ref[...])
        mo_ref[...] = mn
        vo_ref[...] = vn

    blk = pl.BlockSpec((tr, cdim), lambda i: (i, 0))
    return pl.pallas_call(body, name=name, grid=(r // tr,), in_specs=[blk] * 4, out_specs=[blk] * 3,
                          out_shape=[jax.ShapeDtypeStruct((r, cdim), f32)] * 3, compiler_params=_cp("parallel"))(w, g, m, v)


def _ep_bias(accs, ex):
    return (accs[0] + ex[0],)


def kernel(x, c, ada_w, ada_b, norm_g, ffn_w_gate, ffn_w_up, ffn_w_down, w_in, w_out, fox_f_bias, fox_out_norm, gdn_conv, gdn_A_log, gdn_dt_bias, gdn_out_norm, final_norm, loss_target, m_ada_w, m_ada_b, m_norm_g, m_ffn_w_gate, m_ffn_w_up, m_ffn_w_down, m_w_in, m_w_out, m_fox_f_bias, m_fox_out_norm, m_gdn_conv, m_gdn_A_log, m_gdn_dt_bias, m_gdn_out_norm, m_final_norm, v_ada_w, v_ada_b, v_norm_g, v_ffn_w_gate, v_ffn_w_up, v_ffn_w_down, v_w_in, v_w_out, v_fox_f_bias, v_fox_out_norm, v_gdn_conv, v_gdn_A_log, v_gdn_dt_bias, v_gdn_out_norm, v_final_norm):
    ix, iy, ic = lax.axis_index("x"), lax.axis_index("y"), lax.axis_index("c")
    s_chip = 2 * ix + iy
    me = 4 * ix + 2 * iy + ic
    _, s, d = x.shape
    nh = d // (2 * HEAD_DIM)
    fw = nh * HEAD_DIM
    ncol = ada_w.shape[2]
    dg_sh = norm_g.shape[2]
    cv_sh = gdn_conv.shape[2]
    ff_sh = ffn_w_gate.shape[3]
    in_sh = w_in.shape[2]
    in_pad = -(-in_sh // 32) * 32
    out_sh = w_out.shape[1]
    per_chip = lambda a, t: a[2 * t]

    pack0 = jnp.concatenate([_silu_rows(c), norm_g[0].reshape(1, 3 * dg_sh), gdn_conv[0].reshape(1, CONV_W * cv_sh)], axis=1)
    got0 = _allgather8("gather_cond", pack0)
    cond_all = got0[:, 0, :d]
    norm_g_full = jnp.concatenate([per_chip(got0, t)[0, d:d + 3 * dg_sh].reshape(3, dg_sh) for t in range(4)], axis=1)
    conv_full = jnp.concatenate([per_chip(got0, t)[0, d + 3 * dg_sh:].reshape(CONV_W, cv_sh) for t in range(4)], axis=1)

    ada_b_sh = lax.dynamic_slice_in_dim(ada_b, s_chip * ncol, ncol, axis=1)
    (mod_sh,) = _mm("ada_mod", [[(cond_all, ada_w[0])]], "nn", (8, 512, 2048), _ep_bias, (f32,), extras=((ada_b_sh, "n"),))
    mod_all = _allgather8("gather_mod", mod_sh)
    mod = jnp.concatenate([lax.dynamic_index_in_dim(per_chip(mod_all, t), me, axis=0, keepdims=True) for t in range(4)], axis=1)
    mods = [mod[:, i * d:(i + 1) * d] for i in range(9)]

    halved = lambda a: a.reshape(2, a.shape[0] // 2, d)
    ffn_halves = [[halved(ffn_w_gate[0, j].T.astype(bf16)), halved(ffn_w_up[0, j].T.astype(bf16)),
                   halved(ffn_w_down[0, j].astype(bf16))] for j in range(2)]
    mix_halves = [halved(jnp.pad(w_in[0].T.astype(bf16), ((0, in_pad - in_sh), (0, 0)))), halved(w_out[0].astype(bf16))]
    with_own = lambda got, hs: [lax.dynamic_update_slice(g, h[:, None], (0, s_chip, 0, 0)) for g, h in zip(got, hs)]
    ffn_full = lambda got, hs: tuple(g.reshape(4 * ff_sh, d) for g in with_own(got, hs))
    ffn_blocks = lambda grads: [g.reshape(2, 4, ff_sh // 2, d) for g in grads]
    ffn1_w = ffn_full(_run_comm("gather_ffn1", _gather_spec(ffn_halves[0][:2])), ffn_halves[0][:2]) + (None,)
    prm = _head_params(fox_f_bias, gdn_dt_bias, gdn_A_log, nh)

    class Hooks:
        def gather_mix_spec(self):
            return _gather_spec(mix_halves + ffn_halves[0][2:])

        def ffn1_wd(self, got):
            return ffn_full(got[2:], ffn_halves[0][2:])[0]

        def gather_ffn2_spec(self):
            return _gather_spec(ffn_halves[1])

        def gathered(self, got_mix, got_ffn2):
            g_win, g_wo = with_own(got_mix[:2], mix_halves)
            w_in_t = jnp.swapaxes(g_win, 0, 1).reshape(4, in_pad, d)[:, :in_sh].reshape(4 * in_sh, d)
            return _build_w_cat_t(w_in_t, nh), jnp.swapaxes(g_wo, 0, 1).reshape(4 * out_sh, d), ffn_full(got_ffn2, ffn_halves[1])

        def rs_ffn2_spec(self, dffn2):
            self.ffn2_pairs = _rs_pair_sums("ffn2", ffn_blocks(dffn2), ic)
            return _to_chips_spec(self.ffn2_pairs)

        def rs_ffn1_spec(self, dffn1):
            self.ffn1_pairs = _rs_pair_sums("ffn1", ffn_blocks(dffn1), ic)
            return _to_chips_spec(self.ffn1_pairs)

        def rs_mix_spec(self, dw_cat_t, dw_out):
            dw_in_t = jnp.pad(_split_dw_cat_t(dw_cat_t, nh).reshape(4, in_sh, d), ((0, 0), (0, in_pad - in_sh), (0, 0)))
            grads = [jnp.swapaxes(dw_in_t.reshape(4, 2, in_pad // 2, d), 0, 1),
                     jnp.swapaxes(dw_out.reshape(4, 2, out_sh // 2, d), 0, 1)]
            self.mix_pairs = _rs_pair_sums("mix", grads, ic)
            return _to_chips_spec(self.mix_pairs)

    hooks = Hooks()

    dx0, big, small = _local_step(x[0], loss_target[0], mods, norm_g_full, final_norm.reshape(1, d), ffn1_w, None, prm,
                                  fox_out_norm, gdn_out_norm, conv_full, nh, hooks)

    pack1 = jnp.concatenate([small["loss"], small["norm_g"].reshape(1, 3 * d), small["final_norm"], small["fox_wn"],
                             small["gdn_wn"], small["pg"][0:1], small["pg"][1:2], small["conv"].reshape(1, CONV_W * 3 * fw),
                             small["mod"]], axis=1)
    got1 = _allgather8("gather_small_grads", pack1)
    tot = _sum_devices(got1)
    o = [0]

    def take(n):
        o[0] += n
        return tot[:, o[0] - n:o[0]]

    loss = take(LANES)[0, 0]
    g_norm_g = lax.dynamic_slice_in_dim(take(3 * d).reshape(3, d), s_chip * dg_sh, dg_sh, axis=1)[None]
    g_final = take(d).reshape(d)
    g_fox_wn = take(HEAD_DIM)
    g_gdn_wn = take(HEAD_DIM)
    pg0, pg1 = take(LANES), take(LANES)
    g_fbias, g_dtb, g_alog = pg0[:, 0:nh], pg0[:, nh:2 * nh], pg1[:, nh:2 * nh]
    g_conv = lax.dynamic_slice_in_dim(take(CONV_W * 3 * fw).reshape(CONV_W, 3 * fw), s_chip * cv_sh, cv_sh, axis=1)[None]
    g_ada_b = take(9 * d)
    dmod_all = got1[:, 0, o[0] - 9 * d:o[0]]
    dmod_sh = lax.dynamic_slice_in_dim(dmod_all, s_chip * ncol, ncol, axis=1)
    (g_ada_w,) = _mm("ada_dw", [[(cond_all, dmod_sh)]], "tn", (2048, 512, 8), _ep_plain, (f32,))

    ffn1_mine = _rs_chip_sums("ffn1", hooks.ffn1_pairs, big["got_ffn1"], s_chip)
    ffn2_mine = _rs_chip_sums("ffn2", hooks.ffn2_pairs, big["got_ffn2"], s_chip)
    mix_mine = _rs_chip_sums("mix", hooks.mix_pairs, big["got_mix"], s_chip)
    r1g, r1u, r1d, r2g, r2u, r2d, r_win, r_wo = _rs_both_halves(ffn1_mine + ffn2_mine + mix_mine, ic)
    rows = lambda r: r.reshape(-1, d)
    g_ffn_gate = jnp.stack([rows(r1g).T, rows(r2g).T])[None]
    g_ffn_up = jnp.stack([rows(r1u).T, rows(r2u).T])[None]
    g_ffn_down = jnp.stack([rows(r1d), rows(r2d)])[None]
    g_w_in = rows(r_win)[:in_sh].T[None]
    g_w_out = rows(r_wo)[None]

    def upd(name, w, g, m, v):
        shp = w.shape
        two = lambda a: a.reshape(-1, shp[-1])
        return tuple(t.reshape(shp) for t in _adamw(name, two(w), two(g), two(m), two(v)))

    big_upd = [upd("adamw_ada_w", ada_w, g_ada_w[None], m_ada_w, v_ada_w),
               upd("adamw_ffn_gate", ffn_w_gate, g_ffn_gate, m_ffn_w_gate, v_ffn_w_gate),
               upd("adamw_ffn_up", ffn_w_up, g_ffn_up, m_ffn_w_up, v_ffn_w_up),
               upd("adamw_ffn_down", ffn_w_down, g_ffn_down, m_ffn_w_down, v_ffn_w_down),
               upd("adamw_w_in", w_in, g_w_in, m_w_in, v_w_in),
               upd("adamw_w_out", w_out, g_w_out, m_w_out, v_w_out)]
    small_w = [ada_b, norm_g, fox_f_bias, fox_out_norm, gdn_conv, gdn_A_log, gdn_dt_bias, gdn_out_norm, final_norm]
    small_g = [g_ada_b, g_norm_g, g_fbias, g_fox_wn, g_conv, g_alog, g_dtb, g_gdn_wn, g_final]
    small_m = [m_ada_b, m_norm_g, m_fox_f_bias, m_fox_out_norm, m_gdn_conv, m_gdn_A_log, m_gdn_dt_bias, m_gdn_out_norm, m_final_norm]
    small_v = [v_ada_b, v_norm_g, v_fox_f_bias, v_fox_out_norm, v_gdn_conv, v_gdn_A_log, v_gdn_dt_bias, v_gdn_out_norm, v_final_norm]
    sizes = [a.size for a in small_w]
    npad = -sum(sizes) % LANES
    flat = lambda arrs, fill: jnp.concatenate([a.reshape(1, -1) for a in arrs] + [jnp.full((1, npad), fill, f32)], axis=1)
    sd, sm_, sv = _adamw("adamw_small", flat(small_w, 0.0), flat(small_g, 0.0), flat(small_m, 0.0), flat(small_v, 1.0))

    def unflat(t):
        out, off = [], 0
        for a, n in zip(small_w, sizes):
            out.append(t[0, off:off + n].reshape(a.shape))
            off += n
        return out

    small_g = [g.reshape(a.shape) for g, a in zip(small_g, small_w)]
    s_d, s_m, s_v = unflat(sd), unflat(sm_), unflat(sv)
    def order(bigs, smalls):
        return [bigs[0], smalls[0], smalls[1], bigs[1], bigs[2], bigs[3], bigs[4], bigs[5]] + list(smalls[2:])

    grads_out = order([g_ada_w[None], g_ffn_gate, g_ffn_up, g_ffn_down, g_w_in, g_w_out], small_g)
    deltas = order([u[0] for u in big_upd], s_d)
    new_m = order([u[1] for u in big_upd], s_m)
    new_v = order([u[2] for u in big_upd], s_v)
    return (loss, dx0[None], *grads_out, *deltas, *new_m, *new_v)
```

```python
import functools
import math

import jax
import jax.numpy as jnp
from jax import lax
from jax.experimental import pallas as pl
from jax.experimental.pallas import tpu as pltpu

f32 = jnp.float32
bf16 = jnp.bfloat16
HI = lax.Precision.HIGHEST
MESH = pl.DeviceIdType.MESH

EPS = 1e-6
HEAD_DIM = 128
LANES = 128
GDN_CHUNK = 64
CONV_W = 4
MACARON_W = 0.5
ADAM_LR, ADAM_B1, ADAM_B2, ADAM_EPS, ADAM_WD, ADAM_STEP = 0.001, 0.9, 0.999, 1e-08, 0.01, 10
VMEM_LIMIT_V7X = 56 * 1024 * 1024
NEG = -1e30

NN = (((1,), (0,)), ((), ()))
NT = (((1,), (1,)), ((), ()))
TN = (((0,), (0,)), ((), ()))


def _cp(*sem):
    return pltpu.CompilerParams(dimension_semantics=sem, vmem_limit_bytes=VMEM_LIMIT_V7X)


def _dotb(a, b, dn=NN):
    return lax.dot_general(a.astype(bf16), b.astype(bf16), dn, preferred_element_type=f32)


def _doth(a, b, dn=NN):
    return lax.dot_general(a.astype(f32), b.astype(f32), dn, precision=HI, preferred_element_type=f32)


def _dotm(a, b, dn=NN):
    return lax.dot_general(a.astype(f32), b.astype(f32), dn, precision=lax.Precision.HIGH, preferred_element_type=f32)


def _sigmoid(x):
    return 1.0 / (1.0 + jnp.exp(-x))


def _softplus(x):
    return jnp.maximum(x, 0.0) + jnp.log(1.0 + jnp.exp(-jnp.abs(x)))


def _lane_col(blk, lane_idx):
    lane = lax.broadcasted_iota(jnp.int32, blk.shape, 1)
    return jnp.sum(jnp.where(lane == lane_idx, blk, 0.0), axis=1, keepdims=True)


def _tile(n, pref, mult=LANES):
    if n <= pref:
        return n
    t = (pref // mult) * mult
    while t >= mult:
        if n % t == 0:
            return t
        t -= mult
    return n


MM_EPILOGUE_CHUNKS = 2


def _mm(name, groups, mode, tiles, epilogue, out_dtypes, extras=(), comm=None):
    a0, b0 = groups[0][0]
    if mode == "nn":
        (m, k), n = a0.shape, b0.shape[1]
    elif mode == "nt":
        (m, k), n = a0.shape, b0.shape[0]
    else:
        (k, m), n = a0.shape, b0.shape[1]
    tm, tn, tk = _tile(m, tiles[0]), _tile(n, tiles[1]), _tile(k, tiles[2])
    nk = k // tk
    assert m % tm == 0 and n % tn == 0 and k % tk == 0, (name, m, n, k, tm, tn, tk)
    if mode == "nn":
        a_spec = pl.BlockSpec((tm, tk), lambda i, j, kk: (i, kk))
        b_spec = pl.BlockSpec((tk, tn), lambda i, j, kk: (kk, j))
        dn = NN
    elif mode == "nt":
        a_spec = pl.BlockSpec((tm, tk), lambda i, j, kk: (i, kk))
        b_spec = pl.BlockSpec((tn, tk), lambda i, j, kk: (j, kk))
        dn = NT
    else:
        a_spec = pl.BlockSpec((tk, tm), lambda i, j, kk: (kk, i))
        b_spec = pl.BlockSpec((tk, tn), lambda i, j, kk: (kk, j))
        dn = TN
    npairs = sum(len(g) for g in groups)
    nacc, nex, nout = len(groups), len(extras), len(out_dtypes)
    in_specs, args = [], []
    for g in groups:
        for a, b in g:
            in_specs += [a_spec, b_spec]
            args += [a, b]
    for arr, kind in extras:
        if kind == "mn":
            in_specs.append(pl.BlockSpec((tm, tn), lambda i, j, kk: (i, j)))
        else:
            in_specs.append(pl.BlockSpec((1, tn), lambda i, j, kk: (0, j)))
        args.append(arr)
    nci = len(comm.ins) if comm else 0
    nco = len(comm.out_shapes) if comm else 0
    grid = (m // tm, n // tn, nk)

    def body(*refs):
        ins = refs[: 2 * npairs]
        ex = refs[2 * npairs: 2 * npairs + nex]
        c_ins = refs[2 * npairs + nex: 2 * npairs + nex + nci]
        o0 = 2 * npairs + nex + nci
        outs = refs[o0: o0 + nout]
        c_outs = refs[o0 + nout: o0 + nout + nco]
        accs = refs[o0 + nout + nco: o0 + nout + nco + nacc]
        c_sems = refs[o0 + nout + nco + nacc:]
        kk = pl.program_id(2)
        if comm:
            step = (pl.program_id(0) * grid[1] + pl.program_id(1)) * nk + kk

            @pl.when(step == 0)
            def _():
                comm.run("start", c_ins, c_outs, c_sems)

        if nk == 1 and mode != "tn":
            nsub = MM_EPILOGUE_CHUNKS if tn % (MM_EPILOGUE_CHUNKS * LANES) == 0 else 1
            w = tn // nsub
            for cidx in range(nsub):
                lo = cidx * w
                sums, p = [], 0
                for g in groups:
                    t = None
                    for _ in g:
                        b = ins[2 * p + 1][lo:lo + w, :] if mode == "nt" else ins[2 * p + 1][:, lo:lo + w]
                        d = _dotb(ins[2 * p][...], b, dn)
                        t = d if t is None else t + d
                        p += 1
                    sums.append(t)
                res = epilogue(sums, [e[:, lo:lo + w] for e in ex])
                for o, r in zip(outs, res):
                    o[:, lo:lo + w] = r.astype(o.dtype)
        else:
            @pl.when(kk == 0)
            def _():
                for acc in accs:
                    acc[...] = jnp.zeros_like(acc)

            p = 0
            for gi, g in enumerate(groups):
                t = None
                for _ in g:
                    d = _dotb(ins[2 * p][...], ins[2 * p + 1][...], dn)
                    t = d if t is None else t + d
                    p += 1
                accs[gi][...] += t

            @pl.when(kk == nk - 1)
            def _():
                res = epilogue([acc[...] for acc in accs], [e[...] for e in ex])
                for o, r in zip(outs, res):
                    o[...] = r.astype(o.dtype)

        if comm:
            @pl.when(step == grid[0] * grid[1] * nk - 1)
            def _():
                comm.run("finish", c_ins, c_outs, c_sems)

    any_spec = pl.BlockSpec(memory_space=pl.ANY)
    res = pl.pallas_call(
        body, name=name, grid=grid,
        in_specs=in_specs + [any_spec] * nci,
        out_specs=[pl.BlockSpec((tm, tn), lambda i, j, kk: (i, j)) for _ in out_dtypes] + [any_spec] * nco,
        out_shape=[jax.ShapeDtypeStruct((m, n), dt) for dt in out_dtypes] + (list(comm.out_shapes) if comm else []),
        scratch_shapes=[pltpu.VMEM((tm, tn), f32) for _ in range(nacc)] + (comm.sems() if comm else []),
        compiler_params=_cp(*(("arbitrary",) * 3 if comm else ("parallel", "parallel", "arbitrary"))),
    )(*args, *(comm.ins if comm else []))
    return (res[:nout], res[nout:]) if comm else res


def _ep_plain(accs, ex):
    return (accs[0],)


def _ep_colscale(accs, ex):
    return (accs[0] * ex[0],)


def _ep_swiglu(accs, ex):
    gate, up = accs
    act = gate * _sigmoid(gate) * up
    return gate, up, act


def _ep_residual(accs, ex):
    x, gs = ex
    y = accs[0]
    return x + gs * y, y


def _ep_swiglu_bwd(accs, ex):
    gate, up = ex[0].astype(f32), ex[1].astype(f32)
    dact = accs[0]
    sg = _sigmoid(gate)
    silu = gate * sg
    act = silu * up
    dup = dact * silu
    dgate = dact * up * sg * (1.0 + gate * (1.0 - sg))
    return act, dgate, dup


def _row_tile(s):
    return _tile(s, 256, 8)


def _head_row_tile(s):
    return _tile(s, 1024, 8)


def _ada_in(name, x, g, shift, scale):
    s, d = x.shape
    tm = _row_tile(s)

    def body(x_ref, g_ref, sh_ref, sc_ref, h_ref):
        xv = x_ref[...]
        r = lax.rsqrt(jnp.mean(xv * xv, axis=-1, keepdims=True) + EPS)
        h_ref[...] = (xv * r * g_ref[...] * (1.0 + sc_ref[...]) + sh_ref[...]).astype(h_ref.dtype)

    row = pl.BlockSpec((1, d), lambda i: (0, 0))
    blk = pl.BlockSpec((tm, d), lambda i: (i, 0))
    return pl.pallas_call(body, name=name, grid=(s // tm,), in_specs=[blk, row, row, row], out_specs=blk,
                          out_shape=jax.ShapeDtypeStruct((s, d), bf16), compiler_params=_cp("parallel"))(x, g, shift, scale)


def _gate_terms(dxv, y_ref, gs_ref, dy_ref, dgs_ref):
    dy_ref[...] = (dxv * gs_ref[...]).astype(dy_ref.dtype)
    dgs_ref[...] += jnp.sum(dxv * y_ref[...].astype(f32), axis=0, keepdims=True)


def _ada_bwd(name, x, g, scale, dh, dres, y, gs):
    s, d = x.shape
    tm = _row_tile(s)

    def body(x_ref, g_ref, sc_ref, dh_ref, dres_ref, y_ref, gs_ref, dx_ref, dsh_ref, a_ref, dy_ref, dgs_ref):
        i = pl.program_id(0)

        @pl.when(i == 0)
        def _():
            dsh_ref[...] = jnp.zeros_like(dsh_ref)
            a_ref[...] = jnp.zeros_like(a_ref)
            dgs_ref[...] = jnp.zeros_like(dgs_ref)

        xv = x_ref[...]
        dhv = dh_ref[...].astype(f32)
        r = lax.rsqrt(jnp.mean(xv * xv, axis=-1, keepdims=True) + EPS)
        n = xv * r
        dn = dhv * (g_ref[...] * (1.0 + sc_ref[...]))
        dxv = dres_ref[...] + r * (dn - n * jnp.mean(dn * n, axis=-1, keepdims=True))
        dx_ref[...] = dxv
        dsh_ref[...] += jnp.sum(dhv, axis=0, keepdims=True)
        a_ref[...] += jnp.sum(dhv * n, axis=0, keepdims=True)
        _gate_terms(dxv, y_ref, gs_ref, dy_ref, dgs_ref)

    row = pl.BlockSpec((1, d), lambda i: (0, 0))
    blk = pl.BlockSpec((tm, d), lambda i: (i, 0))
    return pl.pallas_call(
        body, name=name, grid=(s // tm,), in_specs=[blk, row, row, blk, blk, blk, row], out_specs=[blk, row, row, blk, row],
        out_shape=[jax.ShapeDtypeStruct((s, d), f32), jax.ShapeDtypeStruct((1, d), f32), jax.ShapeDtypeStruct((1, d), f32),
                   jax.ShapeDtypeStruct((s, d), bf16), jax.ShapeDtypeStruct((1, d), f32)],
        compiler_params=_cp("arbitrary"))(x, g, scale, dh, dres, y, gs)


def _ada_bwd_first(name, x, g, scale, dh, dres):
    s, d = x.shape
    tm = _row_tile(s)

    def body(x_ref, g_ref, sc_ref, dh_ref, dres_ref, dx_ref, dsh_ref, a_ref):
        i = pl.program_id(0)

        @pl.when(i == 0)
        def _():
            dsh_ref[...] = jnp.zeros_like(dsh_ref)
            a_ref[...] = jnp.zeros_like(a_ref)

        xv = x_ref[...]
        dhv = dh_ref[...].astype(f32)
        r = lax.rsqrt(jnp.mean(xv * xv, axis=-1, keepdims=True) + EPS)
        n = xv * r
        dn = dhv * (g_ref[...] * (1.0 + sc_ref[...]))
        dx_ref[...] = dres_ref[...] + r * (dn - n * jnp.mean(dn * n, axis=-1, keepdims=True))
        dsh_ref[...] += jnp.sum(dhv, axis=0, keepdims=True)
        a_ref[...] += jnp.sum(dhv * n, axis=0, keepdims=True)

    row = pl.BlockSpec((1, d), lambda i: (0, 0))
    blk = pl.BlockSpec((tm, d), lambda i: (i, 0))
    return pl.pallas_call(
        body, name=name, grid=(s // tm,), in_specs=[blk, row, row, blk, blk], out_specs=[blk, row, row],
        out_shape=[jax.ShapeDtypeStruct((s, d), f32), jax.ShapeDtypeStruct((1, d), f32), jax.ShapeDtypeStruct((1, d), f32)],
        compiler_params=_cp("arbitrary"))(x, g, scale, dh, dres)


def _final_loss(x, fg, target, y, gs):
    s, d = x.shape
    tm = _row_tile(s)

    def body(x_ref, g_ref, t_ref, y_ref, gs_ref, loss_ref, dx_ref, dg_ref, dy_ref, dgs_ref):
        i = pl.program_id(0)

        @pl.when(i == 0)
        def _():
            loss_ref[...] = jnp.zeros_like(loss_ref)
            dg_ref[...] = jnp.zeros_like(dg_ref)
            dgs_ref[...] = jnp.zeros_like(dgs_ref)

        xv = x_ref[...]
        gv = g_ref[...]
        r = lax.rsqrt(jnp.mean(xv * xv, axis=-1, keepdims=True) + EPS)
        n = xv * r
        e = n * gv - t_ref[...]
        per_tok = jnp.mean(e * e, axis=-1, keepdims=True)
        loss_ref[...] += 0.5 * jnp.sum(per_tok, axis=0, keepdims=True) * jnp.ones((1, LANES), f32)
        dy = e * (1.0 / d)
        dg_ref[...] += jnp.sum(dy * n, axis=0, keepdims=True)
        dn = dy * gv
        dxv = r * (dn - n * jnp.mean(dn * n, axis=-1, keepdims=True))
        dx_ref[...] = dxv
        _gate_terms(dxv, y_ref, gs_ref, dy_ref, dgs_ref)

    row = pl.BlockSpec((1, d), lambda i: (0, 0))
    blk = pl.BlockSpec((tm, d), lambda i: (i, 0))
    return pl.pallas_call(
        body, name="final_loss", grid=(s // tm,), in_specs=[blk, row, blk, blk, row],
        out_specs=[pl.BlockSpec((1, LANES), lambda i: (0, 0)), blk, row, blk, row],
        out_shape=[jax.ShapeDtypeStruct((1, LANES), f32), jax.ShapeDtypeStruct((s, d), f32), jax.ShapeDtypeStruct((1, d), f32),
                   jax.ShapeDtypeStruct((s, d), bf16), jax.ShapeDtypeStruct((1, d), f32)],
        compiler_params=_cp("arbitrary"))(x, fg, target, y, gs)


def _small_fwd(ps, prm, nh):
    s = ps.shape[0]
    tb = LANES

    def body(ps_ref, prm_ref, sm_ref, cum_ref, carry):
        i = pl.program_id(0)

        @pl.when(i == 0)
        def _():
            carry[...] = jnp.zeros_like(carry)

        x = ps_ref[...]
        lane = lax.broadcasted_iota(jnp.int32, x.shape, 1)
        fb, dtb, alog = prm_ref[0:1, :], prm_ref[1:2, :], prm_ref[2:3, :]
        logf = -_softplus(-(x + fb))
        glog = -jnp.exp(alog) * _softplus(x + dtb)
        beta = _sigmoid(x)
        sm = jnp.where(lane < nh, logf, jnp.where(lane < 2 * nh, glog, jnp.where(lane < 3 * nh, beta, 0.0)))
        sm_ref[...] = sm
        r = lax.broadcasted_iota(jnp.int32, (tb, tb), 0)
        c = lax.broadcasted_iota(jnp.int32, (tb, tb), 1)
        tril = (c <= r).astype(f32)
        cs = _doth(tril, sm) + carry[...]
        cum_ref[...] = cs
        carry[...] = cs[tb - 1:tb, :]

    blk = pl.BlockSpec((tb, LANES), lambda i: (i, 0))
    return pl.pallas_call(
        body, name="small_fwd", grid=(s // tb,),
        in_specs=[blk, pl.BlockSpec((8, LANES), lambda i: (0, 0))],
        out_specs=[blk, blk],
        out_shape=[jax.ShapeDtypeStruct((s, LANES), f32), jax.ShapeDtypeStruct((s, LANES), f32)],
        scratch_shapes=[pltpu.VMEM((1, LANES), f32)],
        compiler_params=_cp("arbitrary"))(ps, prm)


def _small_bwd(ps, prm, dsm, dcum, nh):
    s = ps.shape[0]
    tb = LANES
    nb = s // tb

    def body(ps_ref, prm_ref, dsm_ref, dct_ref, dps_ref, pg_ref, carry):
        i = pl.program_id(0)

        @pl.when(i == 0)
        def _():
            carry[...] = jnp.zeros_like(carry)
            pg_ref[...] = jnp.zeros_like(pg_ref)

        x = ps_ref[...]
        dsm = dsm_ref[...]
        lane = lax.broadcasted_iota(jnp.int32, x.shape, 1)
        fb, dtb, alog = prm_ref[0:1, :], prm_ref[1:2, :], prm_ref[2:3, :]
        r = lax.broadcasted_iota(jnp.int32, (tb, tb), 0)
        c = lax.broadcasted_iota(jnp.int32, (tb, tb), 1)
        triu = (c >= r).astype(f32)
        dlogf = _doth(triu, dct_ref[...]) + carry[...]
        carry[...] = dlogf[0:1, :]
        d_f = dlogf * _sigmoid(-(x + fb))
        nega = -jnp.exp(alog)
        xa = x + dtb
        glog = nega * _softplus(xa)
        d_a = dsm * nega * _sigmoid(xa)
        beta = _sigmoid(x)
        d_b = dsm * beta * (1.0 - beta)
        dps = jnp.where(lane < nh, d_f, jnp.where(lane < 2 * nh, d_a, jnp.where(lane < 3 * nh, d_b, 0.0)))
        dps_ref[...] = dps.astype(dps_ref.dtype)
        row0 = jnp.sum(dps, axis=0, keepdims=True)
        row1 = jnp.sum(jnp.where((lane >= nh) & (lane < 2 * nh), dsm * glog, 0.0), axis=0, keepdims=True)
        sub = lax.broadcasted_iota(jnp.int32, (8, LANES), 0)
        pg_ref[...] += jnp.where(sub == 0, row0, jnp.where(sub == 1, row1, 0.0))

    rev = pl.BlockSpec((tb, LANES), lambda i: (nb - 1 - i, 0))
    fix = pl.BlockSpec((8, LANES), lambda i: (0, 0))
    return pl.pallas_call(
        body, name="small_bwd", grid=(nb,),
        in_specs=[rev, fix, rev, rev],
        out_specs=[rev, fix],
        out_shape=[jax.ShapeDtypeStruct((s, LANES), bf16), jax.ShapeDtypeStruct((8, LANES), f32)],
        scratch_shapes=[pltpu.VMEM((1, LANES), f32)],
        compiler_params=_cp("arbitrary"))(ps, prm, dsm, dcum)


LOG2E = 1.4426950408889634
LN2 = 0.6931471805599453
AUG = 2 * HEAD_DIM
FOX_Q_SCALE = LOG2E / math.sqrt(HEAD_DIM)
FOX_KEY_GROUP = 8


def _split3(col):
    hi = col.astype(bf16).astype(f32)
    r1 = col - hi
    mid = r1.astype(bf16).astype(f32)
    lo = (r1 - mid).astype(bf16).astype(f32)
    return hi, mid, lo


def _aug_block(rows, terms, terms_at, ones_at=None):
    lane = lax.broadcasted_iota(jnp.int32, (rows, LANES), 1)
    blk = jnp.zeros((rows, LANES), f32) if ones_at is None else jnp.where((lane >= ones_at) & (lane < ones_at + 3), 1.0, 0.0)
    for i, t in enumerate(terms):
        blk = jnp.where(lane == terms_at + i, t, blk)
    return blk


def _fox_aug(qkv, cum, nh):
    s = qkv.shape[0]
    tm = _head_row_tile(s)

    def body(q_ref, k_ref, v_ref, cum_ref, qa_ref, ka_ref, va_ref):
        h = pl.program_id(1)
        c2 = _lane_col(cum_ref[...], h) * LOG2E
        hi, mid, lo = _split3(c2)
        qa_ref[:, :HEAD_DIM] = q_ref[...]
        qa_ref[:, HEAD_DIM:] = _aug_block(tm, (hi, mid, lo), 0, 3).astype(bf16)
        ka_ref[:, :HEAD_DIM] = k_ref[...]
        ka_ref[:, HEAD_DIM:] = _aug_block(tm, (-hi, -mid, -lo), 3, 0).astype(bf16)
        va_ref[:, :HEAD_DIM] = v_ref[...]
        va_ref[:, HEAD_DIM:] = _aug_block(tm, (), 0, 0).astype(bf16)

    ab = pl.BlockSpec((tm, AUG), lambda i, h: (i, h))
    return pl.pallas_call(
        body, name="fox_aug", grid=(s // tm, nh),
        in_specs=[pl.BlockSpec((tm, HEAD_DIM), lambda i, h: (i, h)), pl.BlockSpec((tm, HEAD_DIM), lambda i, h: (i, nh + h)),
                  pl.BlockSpec((tm, HEAD_DIM), lambda i, h: (i, 2 * nh + h)), pl.BlockSpec((tm, LANES), lambda i, h: (i, 0))],
        out_specs=[ab, ab, ab], out_shape=[jax.ShapeDtypeStruct((s, nh * AUG), bf16)] * 3,
        compiler_params=_cp("parallel", "parallel"))(qkv, qkv, qkv, cum)


def _fox_fwd(qa, ka, qkv, wn, nh, tq):
    s = qa.shape[0]
    fw = nh * HEAD_DIM
    group = FOX_KEY_GROUP
    while group > s // tq:
        group //= 2

    def body(qa_ref, ka_ref, v_ref, wn_ref, o_ref, on_ref, lse_ref):
        i = pl.program_id(1)
        q = qa_ref[...]

        def logits_t(j, rows):
            return _dotb(ka_ref[pl.ds(pl.multiple_of(j * tq, tq), rows), :], q, NT)

        def pv_t(j, p_t):
            return _dotb(v_ref[pl.ds(pl.multiple_of(j * tq, tq), p_t.shape[0]), :], p_t, TN)

        def update(j0, blocks, carry):
            m, l, acc = carry
            parts = [(j0, blocks)] if blocks == 1 else [(j0, blocks // 2), (j0 + blocks // 2, blocks // 2)]
            ts = [logits_t(j, nb * tq) for j, nb in parts]
            mn = functools.reduce(jnp.maximum, [m] + [jnp.max(t, axis=0, keepdims=True) for t in ts])
            ps = [jnp.exp2(t - mn) for t in ts]
            alpha = jnp.exp2(m - mn)
            l = functools.reduce(jnp.add, [alpha * l] + [jnp.sum(p, axis=0, keepdims=True) for p in ps])
            acc = functools.reduce(jnp.add, [alpha * acc] + [pv_t(j, p) for (j, _), p in zip(parts, ps)])
            return mn, l, acc

        key = lax.broadcasted_iota(jnp.int32, (tq, tq), 0)
        qry = lax.broadcasted_iota(jnp.int32, (tq, tq), 1)
        t = jnp.where(key <= qry, logits_t(i, tq), NEG)
        m = jnp.max(t, axis=0, keepdims=True)
        p_t = jnp.exp2(t - m)
        carry = (m, jnp.sum(p_t, axis=0, keepdims=True), pv_t(i, p_t))
        carry = lax.fori_loop(0, i // group, lambda jj, c: update(group * jj, group, c), carry)
        start, part = group * (i // group), group // 2
        while part:
            has = lax.rem(i // part, 2)
            carry = lax.fori_loop(0, has, functools.partial(lambda _, c, j0, blocks: update(j0, blocks, c), j0=start, blocks=part), carry)
            start, part = start + part * has, part // 2
        m, l, acc = carry
        o = (acc / l).T
        o_ref[...] = o
        sub = lax.broadcasted_iota(jnp.int32, (LANES, tq), 0)
        lse_ref[0] = jnp.where(sub == 0, m + jnp.log2(l), 0.0).T
        r = lax.rsqrt(jnp.mean(o * o, axis=-1, keepdims=True) + EPS)
        on_ref[...] = (o * r * wn_ref[...]).astype(on_ref.dtype)

    hb = pl.BlockSpec((tq, HEAD_DIM), lambda h, i: (i, h))
    return pl.pallas_call(
        body, name="fox_fwd", grid=(nh, s // tq),
        in_specs=[pl.BlockSpec((tq, AUG), lambda h, i: (i, h)), pl.BlockSpec((s, AUG), lambda h, i: (0, h)),
                  pl.BlockSpec((s, HEAD_DIM), lambda h, i: (0, 2 * nh + h)), pl.BlockSpec((1, HEAD_DIM), lambda h, i: (0, 0))],
        out_specs=[hb, hb, pl.BlockSpec((1, tq, LANES), lambda h, i: (h, i, 0))],
        out_shape=[jax.ShapeDtypeStruct((s, fw), f32), jax.ShapeDtypeStruct((s, fw), bf16), jax.ShapeDtypeStruct((nh, s, LANES), f32)],
        compiler_params=_cp("parallel", "parallel"))(qa, ka, qkv, wn)


def _fox_post_bwd(don, o, lse2, cum, qkv, wn, nh):
    s, fw = o.shape
    tm = _head_row_tile(s)

    def body(don_ref, o_ref, lse_ref, cum_ref, q_ref, wn_ref, qb_ref, doa_ref, qbt_ref, dot_ref, dwn_ref):
        i = pl.program_id(0)
        h = pl.program_id(1)

        @pl.when((i == 0) & (h == 0))
        def _():
            dwn_ref[...] = jnp.zeros_like(dwn_ref)

        o = o_ref[...]
        don = don_ref[...].astype(f32)
        r = lax.rsqrt(jnp.mean(o * o, axis=-1, keepdims=True) + EPS)
        n = o * r
        dwn_ref[...] += jnp.sum(don * n, axis=0, keepdims=True)
        dn = don * wn_ref[...]
        do = r * (dn - n * jnp.mean(dn * n, axis=-1, keepdims=True))
        delta = jnp.sum(do * o, axis=-1, keepdims=True)
        a2 = _lane_col(cum_ref[...], h) * LOG2E - _lane_col(lse_ref[0], 0)
        q_aug = _aug_block(tm, _split3(a2), 0, 3)
        q = q_ref[...]
        qb_ref[:, :HEAD_DIM] = q
        qb_ref[:, HEAD_DIM:] = q_aug.astype(bf16)
        doa_ref[:, :HEAD_DIM] = do.astype(bf16)
        doa_ref[:, HEAD_DIM:] = _aug_block(tm, _split3(-delta), 0).astype(bf16)
        qbt_ref[:HEAD_DIM, :] = q.astype(f32).T.astype(bf16)
        qbt_ref[HEAD_DIM:, :] = q_aug.T.astype(bf16)
        dot_ref[...] = do.T.astype(bf16)

    hb = pl.BlockSpec((tm, HEAD_DIM), lambda i, h: (i, h))
    ab = pl.BlockSpec((tm, AUG), lambda i, h: (i, h))
    return pl.pallas_call(
        body, name="fox_post_bwd", grid=(s // tm, nh),
        in_specs=[hb, hb, pl.BlockSpec((1, tm, LANES), lambda i, h: (h, i, 0)), pl.BlockSpec((tm, LANES), lambda i, h: (i, 0)),
                  hb, pl.BlockSpec((1, HEAD_DIM), lambda i, h: (0, 0))],
        out_specs=[ab, ab, pl.BlockSpec((AUG, tm), lambda i, h: (h, i)), pl.BlockSpec((HEAD_DIM, tm), lambda i, h: (h, i)),
                   pl.BlockSpec((1, HEAD_DIM), lambda i, h: (0, 0))],
        out_shape=[jax.ShapeDtypeStruct((s, nh * AUG), bf16), jax.ShapeDtypeStruct((s, nh * AUG), bf16),
                   jax.ShapeDtypeStruct((nh * AUG, s), bf16), jax.ShapeDtypeStruct((nh * HEAD_DIM, s), bf16),
                   jax.ShapeDtypeStruct((1, HEAD_DIM), f32)],
        compiler_params=_cp("arbitrary", "arbitrary"))(don, o, lse2, cum, qkv, wn)


def _fox_bwd(qb, doa, qb_t, do_t, ka, va, nh, tq):
    s = qb.shape[0]
    nq = s // tq
    fw = nh * HEAD_DIM

    def body(qb_ref, doa_ref, qbt_ref, dot_ref, ka_ref, va_ref, dqx_ref, dkx_ref, dv_ref, dk_acc, dv_acc):
        j = pl.program_id(1)

        @pl.when(j == 0)
        def _():
            dqx_ref[...] = jnp.zeros_like(dqx_ref)

        kj = ka_ref[...]
        vj = va_ref[...]

        def tile(i, rows=tq, first=False, keep=None):
            off = pl.multiple_of(i * tq, tq)
            p = jnp.exp2(_dotb(qb_ref[pl.ds(off, rows), :], kj, NT))
            if keep is not None:
                p = jnp.where(keep, p, 0.0)
            ds = (p * _dotb(doa_ref[pl.ds(off, rows), :], vj, NT)).astype(bf16)
            dv = _dotb(dot_ref[:, pl.ds(off, rows)], p)
            dk = _dotb(qbt_ref[:, pl.ds(off, rows)], ds)
            if first:
                dv_acc[...] = dv
                dk_acc[...] = dk
            else:
                dv_acc[...] += dv
                dk_acc[...] += dk
            dqx_ref[pl.ds(off, rows), :] += _dotb(ds, kj)

        n = nq - 1 - j
        b0 = jnp.minimum(j, nq - 2)
        qpos = b0 * tq + lax.broadcasted_iota(jnp.int32, (2 * tq, tq), 0)
        kpos = j * tq + lax.broadcasted_iota(jnp.int32, (2 * tq, tq), 1)
        tile(b0, 2 * tq, first=True, keep=(kpos <= qpos) & ((qpos < (j + 1) * tq) | (lax.rem(n, 2) == 1)))

        first_pair = j + 1 + lax.rem(n, 2)
        pairs = n // 2

        def pair(ii, carry):
            tile(first_pair, 2 * tq)
            return carry

        def quad(ii, carry):
            tile(first_pair + 2 * lax.rem(pairs, 2) + 4 * ii, 4 * tq)
            return carry

        lax.fori_loop(0, lax.rem(pairs, 2), pair, 0)
        lax.fori_loop(0, pairs // 2, quad, 0)
        dkx_ref[...] = dk_acc[...].T
        dv_ref[...] = dv_acc[...].T.astype(dv_ref.dtype)

    panel = pl.BlockSpec((s, AUG), lambda h, j: (0, h))
    blk = pl.BlockSpec((tq, AUG), lambda h, j: (j, h))
    return pl.pallas_call(
        body, name="fox_bwd", grid=(nh, nq),
        in_specs=[panel, panel, pl.BlockSpec((AUG, s), lambda h, j: (h, 0)), pl.BlockSpec((HEAD_DIM, s), lambda h, j: (h, 0)),
                  blk, blk],
        out_specs=[panel, blk, pl.BlockSpec((tq, HEAD_DIM), lambda h, j: (j, h))],
        out_shape=[jax.ShapeDtypeStruct((s, nh * AUG), f32), jax.ShapeDtypeStruct((s, nh * AUG), f32),
                   jax.ShapeDtypeStruct((s, fw), bf16)],
        scratch_shapes=[pltpu.VMEM((AUG, tq), f32), pltpu.VMEM((HEAD_DIM, tq), f32)],
        compiler_params=_cp("parallel", "arbitrary"))(qb, doa, qb_t, do_t, ka, va)


def _fox_unpack(dqx, dkx, nh):
    s = dqx.shape[0]
    fw = nh * HEAD_DIM
    tm = _head_row_tile(s)

    def body(dqx_ref, dkx_ref, dq_ref, dk_ref, dcum_ref):
        h = pl.program_id(1)

        @pl.when(h == 0)
        def _():
            dcum_ref[...] = jnp.zeros_like(dcum_ref)

        dq_ref[...] = (dqx_ref[:, :HEAD_DIM] * (HEAD_DIM ** -0.5)).astype(dq_ref.dtype)
        dk_ref[...] = (dkx_ref[:, :HEAD_DIM] * LN2).astype(dk_ref.dtype)
        d = _lane_col(dqx_ref[:, HEAD_DIM:], 0) - _lane_col(dkx_ref[:, HEAD_DIM:], 3)
        lane = lax.broadcasted_iota(jnp.int32, (tm, LANES), 1)
        dcum_ref[...] += jnp.where(lane == h, d, 0.0)

    ab = pl.BlockSpec((tm, AUG), lambda i, h: (i, h))
    hb = pl.BlockSpec((tm, HEAD_DIM), lambda i, h: (i, h))
    return pl.pallas_call(
        body, name="fox_unpack", grid=(s // tm, nh), in_specs=[ab, ab],
        out_specs=[hb, hb, pl.BlockSpec((tm, LANES), lambda i, h: (i, 0))],
        out_shape=[jax.ShapeDtypeStruct((s, fw), bf16), jax.ShapeDtypeStruct((s, fw), bf16), jax.ShapeDtypeStruct((s, LANES), f32)],
        compiler_params=_cp("parallel", "arbitrary"))(dqx, dkx)


def _conv_pre(xx, w, tm):
    pre = None
    for k in range(CONV_W):
        sh = CONV_W - 1 - k
        t = (pltpu.roll(xx, sh, 0) if sh else xx)[8:, :] * w[k:k + 1, :]
        pre = t if pre is None else pre + t
    return pre


def _gdn_pre(x, w, nh):
    s, cw = x.shape
    tm = _row_tile(s)
    fw = nh * HEAD_DIM

    def body(x_ref, prev_ref, w_ref, y_ref):
        i = pl.program_id(0)
        j = pl.program_id(1)
        for h in range(nh):
            sl = slice(h * HEAD_DIM, (h + 1) * HEAD_DIM)
            prev = jnp.where(i == 0, 0.0, prev_ref[:, sl])
            pre = _conv_pre(jnp.concatenate([prev, x_ref[:, sl]], axis=0), w_ref[:, sl], tm)
            y = pre * _sigmoid(pre)
            yn = y * lax.rsqrt(jnp.sum(y * y, axis=-1, keepdims=True) + EPS)
            y_ref[:, sl] = jnp.where(j < 2, yn, y)

    return pl.pallas_call(
        body, name="gdn_pre", grid=(s // tm, cw // fw),
        in_specs=[pl.BlockSpec((tm, fw), lambda i, j: (i, j)),
                  pl.BlockSpec((8, fw), lambda i, j: (jnp.maximum(i * (tm // 8) - 1, 0), j)),
                  pl.BlockSpec((CONV_W, fw), lambda i, j: (0, j))],
        out_specs=pl.BlockSpec((tm, fw), lambda i, j: (i, j)),
        out_shape=jax.ShapeDtypeStruct((s, cw), f32),
        compiler_params=_cp("parallel", "parallel"))(x, x, w)


def _gdn_pre_bwd(x, w, dyn, nh):
    s, cw = x.shape
    tm = _row_tile(s)
    fw = nh * HEAD_DIM

    def body(x_ref, prev_ref, w_ref, dyn_ref, dpre_ref):
        i = pl.program_id(0)
        j = pl.program_id(1)
        for h in range(nh):
            sl = slice(h * HEAD_DIM, (h + 1) * HEAD_DIM)
            prev = jnp.where(i == 0, 0.0, prev_ref[:, sl])
            pre = _conv_pre(jnp.concatenate([prev, x_ref[:, sl]], axis=0), w_ref[:, sl], tm)
            sg = _sigmoid(pre)
            y = pre * sg
            dyn = dyn_ref[:, sl]
            r = lax.rsqrt(jnp.sum(y * y, axis=-1, keepdims=True) + EPS)
            yn = y * r
            dy_n = r * (dyn - yn * jnp.sum(dyn * yn, axis=-1, keepdims=True))
            dy = jnp.where(j < 2, dy_n, dyn)
            dpre_ref[:, sl] = dy * sg * (1.0 + pre * (1.0 - sg))

    hb = pl.BlockSpec((tm, fw), lambda i, j: (i, j))
    return pl.pallas_call(
        body, name="gdn_pre_bwd", grid=(s // tm, cw // fw),
        in_specs=[hb, pl.BlockSpec((8, fw), lambda i, j: (jnp.maximum(i * (tm // 8) - 1, 0), j)),
                  pl.BlockSpec((CONV_W, fw), lambda i, j: (0, j)), hb],
        out_specs=hb, out_shape=jax.ShapeDtypeStruct((s, cw), f32),
        compiler_params=_cp("parallel", "parallel"))(x, x, w, dyn)


def _conv_bwd(x, w, dpre, nh):
    s, cw = x.shape
    tm = _row_tile(s)
    fw = nh * HEAD_DIM
    ni = s // tm

    def body(x_ref, prev_ref, w_ref, dp_ref, nxt_ref, dx_ref, dw_ref):
        i = pl.program_id(1)

        @pl.when(i == 0)
        def _():
            dw_ref[...] = jnp.zeros_like(dw_ref)

        for h in range(nh):
            sl = slice(h * HEAD_DIM, (h + 1) * HEAD_DIM)
            wv = w_ref[:, sl]
            dp = dp_ref[:, sl]
            nxt = jnp.where(i == ni - 1, 0.0, nxt_ref[:, sl])
            dd = jnp.concatenate([dp, nxt], axis=0)
            prev = jnp.where(i == 0, 0.0, prev_ref[:, sl])
            xx = jnp.concatenate([prev, x_ref[:, sl]], axis=0)
            dx = None
            rows = []
            for k in range(CONV_W):
                sh = CONV_W - 1 - k
                t = (pltpu.roll(dd, tm + 8 - sh, 0) if sh else dd)[:tm, :] * wv[k:k + 1, :]
                dx = t if dx is None else dx + t
                xs = (pltpu.roll(xx, sh, 0) if sh else xx)[8:, :]
                rows.append(jnp.sum(dp * xs, axis=0, keepdims=True))
            dx_ref[:, sl] = dx.astype(dx_ref.dtype)
            dw_ref[:, sl] += jnp.concatenate(rows, axis=0)

    hb = pl.BlockSpec((tm, fw), lambda j, i: (i, j))
    wb = pl.BlockSpec((CONV_W, fw), lambda j, i: (0, j))
    return pl.pallas_call(
        body, name="conv_bwd", grid=(cw // fw, ni),
        in_specs=[hb, pl.BlockSpec((8, fw), lambda j, i: (jnp.maximum(i * (tm // 8) - 1, 0), j)), wb, hb,
                  pl.BlockSpec((8, fw), lambda j, i: (jnp.minimum((i + 1) * (tm // 8), s // 8 - 1), j))],
        out_specs=[hb, wb],
        out_shape=[jax.ShapeDtypeStruct((s, cw), bf16), jax.ShapeDtypeStruct((CONV_W, cw), f32)],
        compiler_params=_cp("parallel", "arbitrary"))(x, x, w, dpre, dpre)


def _chunk_consts():
    c = GDN_CHUNK
    r = lax.broadcasted_iota(jnp.int32, (c, c), 0)
    q = lax.broadcasted_iota(jnp.int32, (c, c), 1)
    return r >= q, r > q, (r == q).astype(f32)


def _chunk_head(qkvn, sm, gcs, gcs_t, h, nh):
    fw = nh * HEAD_DIM
    q = qkvn[:, h * HEAD_DIM:(h + 1) * HEAD_DIM] * (HEAD_DIM ** -0.5)
    k = qkvn[:, fw + h * HEAD_DIM: fw + (h + 1) * HEAD_DIM]
    v = qkvn[:, 2 * fw + h * HEAD_DIM: 2 * fw + (h + 1) * HEAD_DIM]
    beta = _lane_col(sm, 2 * nh + h)
    gc = _lane_col(gcs, nh + h)
    gc_row = gcs_t[nh + h: nh + h + 1, :]
    incl, strict, _ = _chunk_consts()
    decay = jnp.where(incl, jnp.exp(jnp.minimum(gc - gc_row, 0.0)), 0.0)
    eg = jnp.exp(gc)
    g_last = gc[GDN_CHUNK - 1:GDN_CHUNK, :]
    egl = jnp.exp(g_last)
    ekd = jnp.exp(g_last - gc)
    kb = k * beta
    vb = v * beta
    kk = _dotb(kb, k, NT)
    qk = _dotb(q, k, NT)
    return dict(q=q, k=k, v=v, beta=beta, gc=gc, decay=decay, eg=eg, egl=egl, ekd=ekd, kb=kb, vb=vb, kk=kk, qk=qk,
                incl=incl, strict=strict)


def _unit_lower_inverses(lows, eye):
    ps = [-low for low in lows]
    ts = [eye + p for p in ps]
    for _ in range(5):
        ps = [_dotm(p, p) for p in ps]
        ts = [t + _dotm(t, p) for t, p in zip(ts, ps)]
    return ts


def _gdn_fwd(qkvn, sm, z, wn, nh):
    s = qkvn.shape[0]
    c = GDN_CHUNK
    nc = s // c
    fw = nh * HEAD_DIM

    def body(qkvn_ref, sm_ref, z_ref, wn_ref, on_ref, o_ref, st_ref, ti_ref, state):
        ci = pl.program_id(0)

        @pl.when(ci == 0)
        def _():
            state[...] = jnp.zeros_like(state)

        qkvn_v = qkvn_ref[...]
        sm_v = sm_ref[...]
        incl, strict, eye = _chunk_consts()
        gcs = _doth(incl.astype(f32), sm_v)
        gcs_t = gcs.T
        heads = range(nh)
        es = [_chunk_head(qkvn_v, sm_v, gcs, gcs_t, h, nh) for h in heads]
        tinvs = _unit_lower_inverses([jnp.where(strict, e["kk"] * e["decay"], 0.0) for e in es], eye)
        us = [_dotm(t, e["vb"]) for t, e in zip(tinvs, es)]
        ws = [_dotm(t, e["kb"] * e["eg"]) for t, e in zip(tinvs, es)]
        sts = [state[h] for h in heads]
        v_news = [u - _dotb(w, st) for u, w, st in zip(us, ws, sts)]
        qss = [_dotb(e["q"] * e["eg"], st) for e, st in zip(es, sts)]
        os_ = [qs + _dotb(jnp.where(incl, e["qk"] * e["decay"], 0.0), vn) for qs, e, vn in zip(qss, es, v_news)]
        upd = [_dotb(e["k"] * e["ekd"], vn, TN) for e, vn in zip(es, v_news)]
        for h in heads:
            st_ref[0, h] = sts[h]
            ti_ref[0, h] = tinvs[h]
            state[h] = sts[h] * es[h]["egl"] + upd[h]
            sl = slice(h * HEAD_DIM, (h + 1) * HEAD_DIM)
            o = os_[h]
            o_ref[:, sl] = o
            zz = z_ref[:, sl]
            r = lax.rsqrt(jnp.mean(o * o, axis=-1, keepdims=True) + EPS)
            on_ref[:, sl] = (o * r * wn_ref[...] * (zz * _sigmoid(zz))).astype(on_ref.dtype)

    return pl.pallas_call(
        body, name="gdn_fwd", grid=(nc,),
        in_specs=[pl.BlockSpec((c, 3 * fw), lambda i: (i, 0)), pl.BlockSpec((c, LANES), lambda i: (i, 0)),
                  pl.BlockSpec((c, fw), lambda i: (i, 0)), pl.BlockSpec((1, HEAD_DIM), lambda i: (0, 0))],
        out_specs=[pl.BlockSpec((c, fw), lambda i: (i, 0)), pl.BlockSpec((c, fw), lambda i: (i, 0)),
                   pl.BlockSpec((1, nh, HEAD_DIM, HEAD_DIM), lambda i: (i, 0, 0, 0)),
                   pl.BlockSpec((1, nh, c, c), lambda i: (i, 0, 0, 0))],
        out_shape=[jax.ShapeDtypeStruct((s, fw), bf16), jax.ShapeDtypeStruct((s, fw), f32),
                   jax.ShapeDtypeStruct((nc, nh, HEAD_DIM, HEAD_DIM), f32), jax.ShapeDtypeStruct((nc, nh, c, c), f32)],
        scratch_shapes=[pltpu.VMEM((nh, HEAD_DIM, HEAD_DIM), f32)],
        compiler_params=_cp("arbitrary"))(qkvn, sm, z, wn)


def _gdn_post_bwd(don, o, z, wn, nh):
    s, fw = o.shape
    tm = _head_row_tile(s)

    def body(don_ref, o_ref, z_ref, wn_ref, do_ref, dz_ref, dwn_ref):
        i = pl.program_id(0)
        h = pl.program_id(1)

        @pl.when((i == 0) & (h == 0))
        def _():
            dwn_ref[...] = jnp.zeros_like(dwn_ref)

        o = o_ref[...]
        zz = z_ref[...]
        don = don_ref[...].astype(f32)
        wv = wn_ref[...]
        r = lax.rsqrt(jnp.mean(o * o, axis=-1, keepdims=True) + EPS)
        n = o * r
        sg = _sigmoid(zz)
        silu = zz * sg
        dz_ref[...] = (don * n * wv * sg * (1.0 + zz * (1.0 - sg))).astype(dz_ref.dtype)
        dnw = don * silu
        dwn_ref[...] += jnp.sum(dnw * n, axis=0, keepdims=True)
        dn = dnw * wv
        do_ref[...] = r * (dn - n * jnp.mean(dn * n, axis=-1, keepdims=True))

    hb = pl.BlockSpec((tm, HEAD_DIM), lambda i, h: (i, h))
    wb = pl.BlockSpec((1, HEAD_DIM), lambda i, h: (0, 0))
    return pl.pallas_call(
        body, name="gdn_post_bwd", grid=(s // tm, nh), in_specs=[hb, hb, hb, wb], out_specs=[hb, hb, wb],
        out_shape=[jax.ShapeDtypeStruct((s, fw), f32), jax.ShapeDtypeStruct((s, fw), bf16),
                   jax.ShapeDtypeStruct((1, HEAD_DIM), f32)],
        compiler_params=_cp("arbitrary", "arbitrary"))(don, o, z, wn)


def _gdn_bwd(qkvn, sm, do, states, tinvs, nh):
    s = qkvn.shape[0]
    c = GDN_CHUNK
    nc = s // c
    fw = nh * HEAD_DIM

    def body(qkvn_ref, sm_ref, do_ref, st_ref, ti_ref, dqkvn_ref, dsm_ref, dstate):
        ci = pl.program_id(0)

        @pl.when(ci == 0)
        def _():
            dstate[...] = jnp.zeros_like(dstate)

        qkvn_v = qkvn_ref[...]
        sm_v = sm_ref[...]
        incl, strict, eye = _chunk_consts()
        inclf = incl.astype(f32)
        gcs = _doth(inclf, sm_v)
        gcs_t = gcs.T
        lane = lax.broadcasted_iota(jnp.int32, (c, LANES), 1)
        last_row = lax.broadcasted_iota(jnp.int32, (c, 1), 0) == c - 1
        ones_cl = jnp.ones((c, LANES), f32)
        each = lambda f: [f(h) for h in range(nh)]
        es = each(lambda h: _chunk_head(qkvn_v, sm_v, gcs, gcs_t, h, nh))
        tinv = each(lambda h: ti_ref[0, h])
        st = each(lambda h: st_ref[0, h])
        dst = each(lambda h: dstate[h])
        do = each(lambda h: do_ref[:, h * HEAD_DIM:(h + 1) * HEAD_DIM])
        kg = each(lambda h: es[h]["kb"] * es[h]["eg"])
        qg = each(lambda h: es[h]["q"] * es[h]["eg"])
        kd = each(lambda h: es[h]["k"] * es[h]["ekd"])
        u = each(lambda h: _dotm(tinv[h], es[h]["vb"]))
        w = each(lambda h: _dotm(tinv[h], kg[h]))
        a = each(lambda h: jnp.where(incl, es[h]["qk"] * es[h]["decay"], 0.0))
        v_new = each(lambda h: u[h] - _dotb(w[h], st[h]))
        dv_new = each(lambda h: _dotb(a[h], do[h], TN) + _dotb(kd[h], dst[h]))
        da = each(lambda h: jnp.where(incl, _dotb(do[h], v_new[h], NT), 0.0))
        dqg = each(lambda h: _dotb(do[h], st[h], NT))
        dkd = each(lambda h: _dotb(v_new[h], dst[h], NT))
        dglast = each(lambda h: es[h]["egl"] * jnp.sum(jnp.sum(dst[h] * st[h], axis=1, keepdims=True), axis=0, keepdims=True))
        dw = each(lambda h: -_dotb(dv_new[h], st[h], NT))
        new_dst = each(lambda h: _dotb(qg[h], do[h], TN) + es[h]["egl"] * dst[h] - _dotb(w[h], dv_new[h], TN))
        dtinv = each(lambda h: _dotm(dv_new[h], es[h]["vb"], NT) + _dotm(dw[h], kg[h], NT))
        dvb = each(lambda h: _dotm(tinv[h], dv_new[h], TN))
        dkg = each(lambda h: _dotm(tinv[h], dw[h], TN))
        tdt = each(lambda h: _dotm(tinv[h], dtinv[h], TN))
        dlow = each(lambda h: -_dotm(tdt[h], tinv[h], NT))
        dkk = each(lambda h: jnp.where(strict, dlow[h] * es[h]["decay"], 0.0))
        dqk = each(lambda h: da[h] * es[h]["decay"])
        darg = each(lambda h: (jnp.where(strict, dlow[h] * es[h]["kk"], 0.0) + da[h] * es[h]["qk"]) * es[h]["decay"])
        dgc = each(lambda h: jnp.sum(darg[h], axis=1, keepdims=True) - _doth(darg[h], ones_cl, TN)[:, 0:1])
        dkb = each(lambda h: _dotb(dkk[h], es[h]["k"]) + dkg[h] * es[h]["eg"])
        dk = each(lambda h: _dotb(dkk[h], es[h]["kb"], TN) + _dotb(dqk[h], es[h]["q"], TN) + dkd[h] * es[h]["ekd"]
                  + dkb[h] * es[h]["beta"])
        dq = each(lambda h: (_dotb(dqk[h], es[h]["k"]) + dqg[h] * es[h]["eg"]) * (HEAD_DIM ** -0.5))
        s_kd = each(lambda h: jnp.sum(dkd[h] * kd[h], axis=1, keepdims=True))
        dgc = each(lambda h: dgc[h] + jnp.sum(dkg[h] * kg[h] + dqg[h] * qg[h], axis=1, keepdims=True) - s_kd[h]
                   + jnp.where(last_row, jnp.sum(s_kd[h], axis=0, keepdims=True) + dglast[h], 0.0))
        dg = each(lambda h: _doth(inclf, dgc[h] * ones_cl, TN)[:, 0:1])
        dsm = jnp.zeros((c, LANES), f32)
        for h in range(nh):
            dstate[h] = new_dst[h]
            dbeta = jnp.sum(dkb[h] * es[h]["k"] + dvb[h] * es[h]["v"], axis=1, keepdims=True)
            dqkvn_ref[:, h * HEAD_DIM:(h + 1) * HEAD_DIM] = dq[h]
            dqkvn_ref[:, fw + h * HEAD_DIM: fw + (h + 1) * HEAD_DIM] = dk[h]
            dqkvn_ref[:, 2 * fw + h * HEAD_DIM: 2 * fw + (h + 1) * HEAD_DIM] = dvb[h] * es[h]["beta"]
            dsm = dsm + jnp.where(lane == nh + h, dg[h], 0.0) + jnp.where(lane == 2 * nh + h, dbeta, 0.0)
        dsm_ref[...] = dsm

    rev = lambda i: (nc - 1 - i, 0)
    rev4 = lambda i: (nc - 1 - i, 0, 0, 0)
    return pl.pallas_call(
        body, name="gdn_bwd", grid=(nc,),
        in_specs=[pl.BlockSpec((c, 3 * fw), rev), pl.BlockSpec((c, LANES), rev), pl.BlockSpec((c, fw), rev),
                  pl.BlockSpec((1, nh, HEAD_DIM, HEAD_DIM), rev4), pl.BlockSpec((1, nh, c, c), rev4)],
        out_specs=[pl.BlockSpec((c, 3 * fw), rev), pl.BlockSpec((c, LANES), rev)],
        out_shape=[jax.ShapeDtypeStruct((s, 3 * fw), f32), jax.ShapeDtypeStruct((s, LANES), f32)],
        scratch_shapes=[pltpu.VMEM((nh, HEAD_DIM, HEAD_DIM), f32)],
        compiler_params=_cp("arbitrary"))(qkvn, sm, do, states, tinvs)


MM_TILES = (1024, 512, 2048)
MM_TILES_TN = (512, 1024, 4096)
MM_TILES_F_DEEP = (1024, 512, 2816)
MM_TILES_LONG_K = (1024, 512, 2560)


def _hosted(res, comm):
    return res if comm else (res, None)


def _ffn_fwd(tag, x, g, mod3, w, comm_up=None, comm_down=None, wd_of=None):
    wg_t, wu_t, wd = w
    sh, sc, gt = mod3
    h = _ada_in(tag + "_ada", x, g, sh, sc)
    (gate, up, act), got_up = _hosted(_mm(tag + "_up", [[(h, wg_t)], [(h, wu_t)]], "nt", MM_TILES, _ep_swiglu,
                                          (bf16, bf16, bf16), comm=comm_up), comm_up)
    if wd_of:
        wd = wd_of(got_up)
    (xn, y), got_down = _hosted(_mm(tag + "_down", [[(act, wd)]], "nn", MM_TILES_F_DEEP, _ep_residual, (f32, bf16),
                                    extras=((x, "mn"), (MACARON_W * gt, "n")), comm=comm_down), comm_down)
    return xn, dict(x=x, h=h, gate=gate, up=up, y=y), got_up, got_down


def _ffn_bwd(tag, dxn, dy, dgs, res, g, mod3, w, below=None, comm_dact=None, comm_dh_of=None):
    wg_t, wu_t, wd = w
    sh, sc, gt = mod3
    (act, dgate, dup), got = _hosted(_mm(tag + "_dact", [[(dy, wd)]], "nt", MM_TILES, _ep_swiglu_bwd, (bf16, bf16, bf16),
                                         extras=((res["gate"], "mn"), (res["up"], "mn")), comm=comm_dact), comm_dact)
    (dwd,) = _mm(tag + "_dwd", [[(act, dy)]], "tn", MM_TILES_TN, _ep_plain, (bf16,))
    (dwg_t,) = _mm(tag + "_dwg", [[(dgate, res["h"])]], "tn", MM_TILES_TN, _ep_plain, (bf16,))
    (dwu_t,) = _mm(tag + "_dwu", [[(dup, res["h"])]], "tn", MM_TILES_TN, _ep_plain, (bf16,))
    comm_dh = comm_dh_of and comm_dh_of((dwg_t, dwu_t, dwd))
    (dh,), got_dh = _hosted(_mm(tag + "_dh", [[(dgate, wg_t), (dup, wu_t)]], "nn", MM_TILES_F_DEEP, _ep_plain, (bf16,), comm=comm_dh),
                            comm_dh)
    if below:
        dx, dsh, a, dy_below, dgs_below = _ada_bwd(tag + "_ada_bwd", res["x"], g, sc, dh, dxn, *below)
    else:
        (dx, dsh, a), dy_below, dgs_below = _ada_bwd_first(tag + "_ada_bwd", res["x"], g, sc, dh, dxn), None, None
    return dx, (dy_below, dgs_below), (dwg_t, dwu_t, dwd), (dsh, a * g, MACARON_W * dgs), a * (1.0 + sc), got, got_dh


def _local_step(x, target, mods, norm_g, final_norm, ffn1_w, later_w, prm, fox_wn, gdn_wn, conv_w, nh, hooks=None):
    s, d = x.shape
    fw = nh * HEAD_DIM
    tq = _tile(s, min(256, s // 2))
    g_rows = [norm_g[i:i + 1] for i in range(3)]
    m1, m2, m3 = mods[0:3], mods[3:6], mods[6:9]

    x1, r1, got_up, got_down = _ffn_fwd("ffn1", x, g_rows[0], m1, ffn1_w, hooks and hooks.gather_mix_spec(),
                                        hooks and hooks.gather_ffn2_spec(), hooks and hooks.ffn1_wd)
    if hooks:
        ffn1_w = ffn1_w[:2] + (hooks.ffn1_wd(got_up),)
    w_cat_t, w_out, ffn2_w = hooks.gathered(got_up, got_down) if hooks else later_w
    h2 = _ada_in("mix_ada", x1, g_rows[1], m2[0], m2[1])
    w_fox, w_gdn, w_z, w_s = w_cat_t[:3 * fw], w_cat_t[3 * fw:6 * fw], w_cat_t[6 * fw:7 * fw], w_cat_t[7 * fw:]
    colscale = jnp.concatenate([jnp.full((1, fw), FOX_Q_SCALE, f32), jnp.ones((1, 2 * fw), f32)], axis=1)
    (qkv_f,) = _mm("proj_fox", [[(h2, w_fox)]], "nt", MM_TILES, _ep_colscale, (bf16,), extras=((colscale, "n"),))
    (qkv_g,) = _mm("proj_gdn", [[(h2, w_gdn)]], "nt", MM_TILES, _ep_plain, (f32,))
    (z,) = _mm("proj_z", [[(h2, w_z)]], "nt", MM_TILES, _ep_plain, (f32,))
    (ps,) = _mm("proj_s", [[(h2, w_s)]], "nt", MM_TILES, _ep_plain, (f32,))
    sm, cum = _small_fwd(ps, prm, nh)
    qa, ka, va = _fox_aug(qkv_f, cum, nh)
    o_f, on_f, lse2 = _fox_fwd(qa, ka, qkv_f, fox_wn, nh, tq)
    qkvn = _gdn_pre(qkv_g, conv_w, nh)
    on_g, o_g, states, tinvs = _gdn_fwd(qkvn, sm, z, gdn_wn, nh)
    w_top, w_bot = w_out[:fw], w_out[fw:]
    x2, mix = _mm("mix_out", [[(on_f, w_top), (on_g, w_bot)]], "nn", MM_TILES, _ep_residual, (f32, bf16),
                  extras=((x1, "mn"), (m2[2], "n")))
    x3, r3, got_up2, _ = _ffn_fwd("ffn2", x2, g_rows[2], m3, ffn2_w, hooks and hooks.gather_ffn2_wd_spec(), None,
                                  hooks and hooks.ffn2_wd)
    if hooks:
        ffn2_w = ffn2_w[:2] + (hooks.ffn2_wd(got_up2),)
    loss, dx3, dfinal, dy3, dgs3 = _final_loss(x3, final_norm, target, r3["y"], MACARON_W * m3[2])

    dx2, (dmix, dgt2), dffn2, dmod3, dg3, _, _ = _ffn_bwd("ffn2", dx3, dy3, dgs3, r3, g_rows[2], m3, ffn2_w, (mix, m2[2]))
    rs_ffn2 = hooks and hooks.rs_ffn2_spec(dffn2)
    (don_f,) = _mm("mix_dof", [[(dmix, w_top)]], "nt", MM_TILES, _ep_plain, (f32,))
    (don_g,) = _mm("mix_dog", [[(dmix, w_bot)]], "nt", MM_TILES, _ep_plain, (f32,))
    (dw_top,) = _mm("mix_dwtop", [[(on_f, dmix)]], "tn", MM_TILES_TN, _ep_plain, (bf16,))
    (dw_bot,) = _mm("mix_dwbot", [[(on_g, dmix)]], "tn", MM_TILES_TN, _ep_plain, (bf16,))
    qb, doa, qb_t, do_t, dfox_wn = _fox_post_bwd(don_f, o_f, lse2, cum, qkv_f, fox_wn, nh)
    dqx, dkx, dv_f = _fox_bwd(qb, doa, qb_t, do_t, ka, va, nh, tq)
    dq_f, dk_f, dcum = _fox_unpack(dqx, dkx, nh)
    do_g, dz, dgdn_wn = _gdn_post_bwd(don_g, o_g, z, gdn_wn, nh)
    dqkvn, dsm = _gdn_bwd(qkvn, sm, do_g, states, tinvs, nh)
    dpre = _gdn_pre_bwd(qkv_g, conv_w, dqkvn, nh)
    dqkv_g, dconv = _conv_bwd(qkv_g, conv_w, dpre, nh)
    dps, pg = _small_bwd(ps, prm, dsm, dcum, nh)
    dproj = jnp.concatenate([dq_f, dk_f, dv_f, dqkv_g, dz, dps], axis=1)
    ((dw_cat_t,), got_ffn2) = _hosted(_mm("proj_dw", [[(dproj, h2)]], "tn", MM_TILES_TN, _ep_plain, (bf16,), comm=rs_ffn2), rs_ffn2)
    dw_out = jnp.concatenate([dw_top, dw_bot], axis=0)
    rs_mix = hooks and hooks.rs_mix_spec(dw_cat_t, dw_out)
    (dh2,) = _mm("proj_dh", [[(dproj, w_cat_t)]], "nn", MM_TILES_LONG_K, _ep_plain, (bf16,))
    dx1, dsh2, a2, dy1, dgs1 = _ada_bwd("mix_ada_bwd", x1, g_rows[1], m2[1], dh2, dx2, r1["y"], MACARON_W * m1[2])
    dmod2 = (dsh2, a2 * g_rows[1], dgt2)
    dg2 = a2 * (1.0 + m2[1])
    dx0, _, dffn1, dmod1, dg1, got_mix, got_ffn1 = _ffn_bwd("ffn1", dx1, dy1, dgs1, r1, g_rows[0], m1, ffn1_w, None, rs_mix,
                                                             hooks and hooks.rs_ffn1_spec)

    big = dict(ffn=(dffn1, dffn2), w_cat_t=dw_cat_t, w_out=dw_out, got_ffn2=got_ffn2, got_mix=got_mix, got_ffn1=got_ffn1)
    small = dict(loss=loss, norm_g=jnp.concatenate([dg1, dg2, dg3], axis=0), final_norm=dfinal, fox_wn=dfox_wn,
                 gdn_wn=dgdn_wn, pg=pg, conv=dconv, mod=jnp.concatenate(list(dmod1) + list(dmod2) + list(dmod3), axis=1))
    return dx0, big, small


def _w_in_row_groups(nh):
    fw = nh * HEAD_DIM
    sizes = [3 * fw, nh, 3 * fw, nh, nh, fw]
    offs = [0]
    for sz in sizes:
        offs.append(offs[-1] + sz)
    return [(offs[i], offs[i + 1]) for i in range(len(sizes))]


def _build_w_cat_t(w_in_t, nh):
    gq, gf, gg, ga, gb, gz = _w_in_row_groups(nh)
    d = w_in_t.shape[1]
    rows = lambda r: w_in_t[r[0]:r[1]]
    pad = jnp.zeros((LANES - 3 * nh, d), w_in_t.dtype)
    return jnp.concatenate([rows(gq), rows(gg), rows(gz), rows(gf), rows(ga), rows(gb), pad], axis=0)


def _split_dw_cat_t(dw_cat_t, nh):
    fw = nh * HEAD_DIM
    o = 7 * fw
    return jnp.concatenate([dw_cat_t[:3 * fw], dw_cat_t[o:o + nh], dw_cat_t[3 * fw:6 * fw], dw_cat_t[o + nh:o + 2 * nh],
                            dw_cat_t[o + 2 * nh:o + 3 * nh], dw_cat_t[6 * fw:7 * fw]], axis=0)


def _head_params(fox_f_bias, gdn_dt_bias, gdn_a_log, nh):
    z = jnp.zeros((8, LANES), f32)
    z = z.at[0, 0:nh].set(fox_f_bias.reshape(nh))
    z = z.at[1, nh:2 * nh].set(gdn_dt_bias.reshape(nh))
    z = z.at[2, nh:2 * nh].set(gdn_a_log.reshape(nh))
    return z


ANY = pl.BlockSpec(memory_space=pl.ANY)
IN_VMEM = pl.BlockSpec(memory_space=pltpu.VMEM)
N_OTHER_CHIPS = 3


def _place():
    x, y, c = lax.axis_index("x"), lax.axis_index("y"), lax.axis_index("c")
    chips = [(1 - x, y), (x, 1 - y), (1 - x, 1 - y)]
    return x, y, c, chips


def _allgather8(name, v):
    r, n = v.shape

    def body(v_ref, out_ref, send_sems, recv_sems, local_sem):
        x, y, c, _ = _place()
        me = 4 * x + 2 * y + c
        mine = pltpu.make_async_copy(v_ref, out_ref.at[me], local_sem)
        mine.start()
        copies = []
        for k in range(1, 8):
            fx, fy, fc = (k >> 2) & 1, (k >> 1) & 1, k & 1
            peer = (x + fx - 2 * x * fx, y + fy - 2 * y * fy, c + fc - 2 * c * fc)
            cp = pltpu.make_async_remote_copy(src_ref=v_ref, dst_ref=out_ref.at[me], send_sem=send_sems.at[k - 1],
                                              recv_sem=recv_sems.at[k - 1], device_id=peer, device_id_type=MESH)
            cp.start()
            copies.append(cp)
        for cp in copies:
            cp.wait()
        mine.wait()

    return pl.pallas_call(
        body, name=name, in_specs=[IN_VMEM], out_specs=IN_VMEM, out_shape=jax.ShapeDtypeStruct((8, r, n), v.dtype),
        scratch_shapes=[pltpu.SemaphoreType.DMA((7,)), pltpu.SemaphoreType.DMA((7,)), pltpu.SemaphoreType.DMA],
        compiler_params=pltpu.CompilerParams(vmem_limit_bytes=VMEM_LIMIT_V7X))(v)


class _CommSpec:
    def __init__(self, ins, out_shapes, sem_counts, run):
        self.ins, self.out_shapes, self.sem_counts, self.run = list(ins), list(out_shapes), sem_counts, run

    def sems(self):
        return [pltpu.SemaphoreType.DMA((n,)) for n in self.sem_counts]


def _run_comm(name, spec):
    ni, no = len(spec.ins), len(spec.out_shapes)

    def body(*refs):
        ins, outs, sems = refs[:ni], refs[ni:ni + no], refs[ni + no:]
        spec.run("start", ins, outs, sems)
        spec.run("finish", ins, outs, sems)

    return pl.pallas_call(body, name=name, in_specs=[ANY] * ni, out_specs=[ANY] * no, out_shape=spec.out_shapes,
                          scratch_shapes=spec.sems())(*spec.ins)


def _gather_spec(halves):
    nw = len(halves)

    def run(phase, ins, outs, sems):
        ici_send, ici_recv, d2d_send, d2d_recv = sems
        x, y, c, chips = _place()
        s = 2 * x + y
        sib = (x, y, 1 - c)

        def over_ici(w, j, dst):
            cx, cy = chips[j]
            return pltpu.make_async_remote_copy(src_ref=ins[w].at[c], dst_ref=dst, send_sem=ici_send.at[w * 3 + j],
                                                recv_sem=ici_recv.at[w * 3 + j], device_id=(cx, cy, c), device_id_type=MESH)

        def to_sibling(w, j, blk):
            return pltpu.make_async_remote_copy(src_ref=blk, dst_ref=blk, send_sem=d2d_send.at[w * 3 + j],
                                                recv_sem=d2d_recv.at[w * 3 + j], device_id=sib, device_id_type=MESH)

        pairs = [(w, j) for w in range(nw) for j in range(N_OTHER_CHIPS)]
        chip_of = lambda j: 2 * chips[j][0] + chips[j][1]
        if phase == "start":
            for w, j in pairs:
                over_ici(w, j, outs[w].at[c, s]).start()
            return
        for w, j in pairs:
            landed = outs[w].at[c, chip_of(j)]
            over_ici(w, j, landed).wait_recv()
            to_sibling(w, j, landed).start()
        for w, j in pairs:
            to_sibling(w, j, outs[w].at[1 - c, chip_of(j)]).wait_recv()
        for w, j in pairs:
            over_ici(w, j, outs[w].at[c, s]).wait_send()
            to_sibling(w, j, outs[w].at[c, chip_of(j)]).wait_send()

    n3 = nw * N_OTHER_CHIPS
    return _CommSpec(halves, [jax.ShapeDtypeStruct((2, 4) + h.shape[1:], h.dtype) for h in halves], [n3] * 4, run)


def _to_chips_spec(partials):
    nw = len(partials)

    def run(phase, ins, outs, sems):
        send_sems, recv_sems = sems
        x, y, c, chips = _place()
        for w in range(nw):
            for j, (cx, cy) in enumerate(chips):
                cp = pltpu.make_async_remote_copy(src_ref=ins[w].at[2 * cx + cy], dst_ref=outs[w].at[j],
                                                  send_sem=send_sems.at[w * 3 + j], recv_sem=recv_sems.at[w * 3 + j],
                                                  device_id=(cx, cy, c), device_id_type=MESH)
                if phase == "start":
                    cp.start()
                else:
                    cp.wait()

    n3 = nw * N_OTHER_CHIPS
    return _CommSpec(partials, [jax.ShapeDtypeStruct((3,) + a.shape[1:], a.dtype) for a in partials], [n3, n3], run)


def _send_to_sibling(name, srcs, other_half):
    nw = len(srcs)

    def body(*refs):
        ins, outs = refs[:nw], refs[nw:2 * nw]
        send_sems, recv_sems = refs[2 * nw:]
        x, y, c, _ = _place()
        cps = []
        for w in range(nw):
            cp = pltpu.make_async_remote_copy(src_ref=ins[w].at[1 - c] if other_half else ins[w], dst_ref=outs[w],
                                              send_sem=send_sems.at[w], recv_sem=recv_sems.at[w],
                                              device_id=(x, y, 1 - c), device_id_type=MESH)
            cp.start()
            cps.append(cp)
        for cp in cps:
            cp.wait()

    return pl.pallas_call(
        body, name=name, in_specs=[ANY] * nw, out_specs=[ANY] * nw,
        out_shape=[jax.ShapeDtypeStruct(a.shape[1:] if other_half else a.shape, a.dtype) for a in srcs],
        scratch_shapes=[pltpu.SemaphoreType.DMA((nw,)), pltpu.SemaphoreType.DMA((nw,))],
    )(*srcs)


def _add_pair(name, g, recv, c):
    _, nchip, r, d = g.shape
    tr = _tile(r, 512, 16)

    def body(c_ref, g_ref, r_ref, o_ref):
        o_ref[...] = (g_ref[...].astype(f32) + r_ref[...].astype(f32)).astype(o_ref.dtype)

    gs = pltpu.PrefetchScalarGridSpec(
        num_scalar_prefetch=1, grid=(nchip, r // tr),
        in_specs=[pl.BlockSpec((None, None, tr, d), lambda t, i, cr: (cr[0], t, i, 0)),
                  pl.BlockSpec((None, tr, d), lambda t, i, cr: (t, i, 0))],
        out_specs=pl.BlockSpec((None, tr, d), lambda t, i, cr: (t, i, 0)))
    return pl.pallas_call(body, name=name, grid_spec=gs, out_shape=jax.ShapeDtypeStruct((nchip, r, d), bf16),
                          compiler_params=_cp("parallel", "parallel"))(c.reshape(1).astype(jnp.int32), g, recv)


def _add_chips(name, p, recv, s_chip):
    _, r, d = p.shape
    tr = _tile(r, 512, 16)

    def body(s_ref, p_ref, r_ref, o_ref):
        o_ref[...] = ((p_ref[...].astype(f32) + r_ref[0].astype(f32)) + r_ref[1].astype(f32)) + r_ref[2].astype(f32)

    gs = pltpu.PrefetchScalarGridSpec(
        num_scalar_prefetch=1, grid=(r // tr,),
        in_specs=[pl.BlockSpec((None, tr, d), lambda i, sr: (sr[0], i, 0)),
                  pl.BlockSpec((3, tr, d), lambda i, sr: (0, i, 0))],
        out_specs=pl.BlockSpec((tr, d), lambda i, sr: (i, 0)))
    return pl.pallas_call(body, name=name, grid_spec=gs, out_shape=jax.ShapeDtypeStruct((r, d), f32),
                          compiler_params=_cp("parallel"))(s_chip.reshape(1).astype(jnp.int32), p, recv)


def _rs_pair_sums(tag, grads, c):
    from_sib = _send_to_sibling("rs_to_sibling_" + tag, grads, True)
    return [_add_pair("rs_add_pair_%s%d" % (tag, n), g, r, c) for n, (g, r) in enumerate(zip(grads, from_sib))]


def _rs_chip_sums(tag, partial, from_chips, s_chip):
    return [_add_chips("rs_add_chips_%s%d" % (tag, n), p, r, s_chip) for n, (p, r) in enumerate(zip(partial, from_chips))]


def _rs_both_halves(mine, c):
    theirs = _send_to_sibling("rs_exchange_halves", mine, False)
    return [jnp.where(c == 0, jnp.stack([a, b]), jnp.stack([b, a])) for a, b in zip(mine, theirs)]


def _sum_devices(v):
    n = v.shape[2]

    def body(v_ref, o_ref):
        t = v_ref[0]
        for k in range(1, 8):
            t = t + v_ref[k]
        o_ref[...] = t

    return pl.pallas_call(body, name="sum_devices", out_shape=jax.ShapeDtypeStruct((1, n), f32))(v)


def _silu_rows(v):
    def body(v_ref, o_ref):
        t = v_ref[...]
        o_ref[...] = t * _sigmoid(t)

    return pl.pallas_call(body, name="silu_cond", out_shape=jax.ShapeDtypeStruct(v.shape, f32))(v)


ADAMW_BLOCK_ELEMS = 600 * 1024


def _adamw(name, w, g, m, v):
    r, cdim = w.shape
    tr = _tile(r, max(8, min(256, (ADAMW_BLOCK_ELEMS // cdim) // 8 * 8)), 8)
    c1 = 1.0 - ADAM_B1 ** ADAM_STEP
    c2 = 1.0 - ADAM_B2 ** ADAM_STEP

    def body(w_ref, g_ref, m_ref, v_ref, d_ref, mo_ref, vo_ref):
        gv = g_ref[...]
        mn = ADAM_B1 * m_ref[...] + (1.0 - ADAM_B1) * gv
        vn = ADAM_B2 * v_ref[...] + (1.0 - ADAM_B2) * (gv * gv)
        d_ref[...] = -ADAM_LR * ((mn / c1) / (jnp.sqrt(vn / c2) + ADAM_EPS) + ADAM_WD * w_ref[...])
        mo_ref[...] = mn
        vo_ref[...] = vn

    blk = pl.BlockSpec((tr, cdim), lambda i: (i, 0))
    return pl.pallas_call(body, name=name, grid=(r // tr,), in_specs=[blk] * 4, out_specs=[blk] * 3,
                          out_shape=[jax.ShapeDtypeStruct((r, cdim), f32)] * 3, compiler_params=_cp("parallel"))(w, g, m, v)


def _ep_bias(accs, ex):
    return (accs[0] + ex[0],)


def kernel(x, c, ada_w, ada_b, norm_g, ffn_w_gate, ffn_w_up, ffn_w_down, w_in, w_out, fox_f_bias, fox_out_norm, gdn_conv, gdn_A_log, gdn_dt_bias, gdn_out_norm, final_norm, loss_target, m_ada_w, m_ada_b, m_norm_g, m_ffn_w_gate, m_ffn_w_up, m_ffn_w_down, m_w_in, m_w_out, m_fox_f_bias, m_fox_out_norm, m_gdn_conv, m_gdn_A_log, m_gdn_dt_bias, m_gdn_out_norm, m_final_norm, v_ada_w, v_ada_b, v_norm_g, v_ffn_w_gate, v_ffn_w_up, v_ffn_w_down, v_w_in, v_w_out, v_fox_f_bias, v_fox_out_norm, v_gdn_conv, v_gdn_A_log, v_gdn_dt_bias, v_gdn_out_norm, v_final_norm):
    ix, iy, ic = lax.axis_index("x"), lax.axis_index("y"), lax.axis_index("c")
    s_chip = 2 * ix + iy
    me = 4 * ix + 2 * iy + ic
    _, s, d = x.shape
    nh = d // (2 * HEAD_DIM)
    fw = nh * HEAD_DIM
    ncol = ada_w.shape[2]
    dg_sh = norm_g.shape[2]
    cv_sh = gdn_conv.shape[2]
    ff_sh = ffn_w_gate.shape[3]
    in_sh = w_in.shape[2]
    in_pad = -(-in_sh // 32) * 32
    out_sh = w_out.shape[1]
    per_chip = lambda a, t: a[2 * t]

    pack0 = jnp.concatenate([_silu_rows(c), norm_g[0].reshape(1, 3 * dg_sh), gdn_conv[0].reshape(1, CONV_W * cv_sh)], axis=1)
    got0 = _allgather8("gather_cond", pack0)
    cond_all = got0[:, 0, :d]
    norm_g_full = jnp.concatenate([per_chip(got0, t)[0, d:d + 3 * dg_sh].reshape(3, dg_sh) for t in range(4)], axis=1)
    conv_full = jnp.concatenate([per_chip(got0, t)[0, d + 3 * dg_sh:].reshape(CONV_W, cv_sh) for t in range(4)], axis=1)

    ada_b_sh = lax.dynamic_slice_in_dim(ada_b, s_chip * ncol, ncol, axis=1)
    (mod_sh,) = _mm("ada_mod", [[(cond_all, ada_w[0])]], "nn", (8, 512, 2048), _ep_bias, (f32,), extras=((ada_b_sh, "n"),))
    mod_all = _allgather8("gather_mod", mod_sh)
    mod = jnp.concatenate([lax.dynamic_index_in_dim(per_chip(mod_all, t), me, axis=0, keepdims=True) for t in range(4)], axis=1)
    mods = [mod[:, i * d:(i + 1) * d] for i in range(9)]

    halved = lambda a: a.reshape(2, a.shape[0] // 2, d)
    ffn_halves = [[halved(ffn_w_gate[0, j].T.astype(bf16)), halved(ffn_w_up[0, j].T.astype(bf16)),
                   halved(ffn_w_down[0, j].astype(bf16))] for j in range(2)]
    mix_halves = [halved(jnp.pad(w_in[0].T.astype(bf16), ((0, in_pad - in_sh), (0, 0)))), halved(w_out[0].astype(bf16))]
    with_own = lambda got, hs: [lax.dynamic_update_slice(g, h[:, None], (0, s_chip, 0, 0)) for g, h in zip(got, hs)]
    ffn_full = lambda got, hs: tuple(g.reshape(4 * ff_sh, d) for g in with_own(got, hs))
    ffn_blocks = lambda grads: [g.reshape(2, 4, ff_sh // 2, d) for g in grads]
    ffn1_w = ffn_full(_run_comm("gather_ffn1", _gather_spec(ffn_halves[0][:2])), ffn_halves[0][:2]) + (None,)
    prm = _head_params(fox_f_bias, gdn_dt_bias, gdn_A_log, nh)

    class Hooks:
        def gather_mix_spec(self):
            return _gather_spec(mix_halves + ffn_halves[0][2:])

        def ffn1_wd(self, got):
            return ffn_full(got[2:], ffn_halves[0][2:])[0]

        def gather_ffn2_spec(self):
            return _gather_spec(ffn_halves[1][:2])

        def gather_ffn2_wd_spec(self):
            return _gather_spec(ffn_halves[1][2:])

        def ffn2_wd(self, got):
            return ffn_full(got, ffn_halves[1][2:])[0]

        def gathered(self, got_mix, got_ffn2):
            g_win, g_wo = with_own(got_mix[:2], mix_halves)
            w_in_t = jnp.swapaxes(g_win, 0, 1).reshape(4, in_pad, d)[:, :in_sh].reshape(4 * in_sh, d)
            return (_build_w_cat_t(w_in_t, nh), jnp.swapaxes(g_wo, 0, 1).reshape(4 * out_sh, d),
                    ffn_full(got_ffn2, ffn_halves[1][:2]) + (None,))

        def rs_ffn2_spec(self, dffn2):
            self.ffn2_pairs = _rs_pair_sums("ffn2", ffn_blocks(dffn2), ic)
            return _to_chips_spec(self.ffn2_pairs)

        def rs_ffn1_spec(self, dffn1):
            self.ffn1_pairs = _rs_pair_sums("ffn1", ffn_blocks(dffn1), ic)
            return _to_chips_spec(self.ffn1_pairs)

        def rs_mix_spec(self, dw_cat_t, dw_out):
            dw_in_t = jnp.pad(_split_dw_cat_t(dw_cat_t, nh).reshape(4, in_sh, d), ((0, 0), (0, in_pad - in_sh), (0, 0)))
            grads = [jnp.swapaxes(dw_in_t.reshape(4, 2, in_pad // 2, d), 0, 1),
                     jnp.swapaxes(dw_out.reshape(4, 2, out_sh // 2, d), 0, 1)]
            self.mix_pairs = _rs_pair_sums("mix", grads, ic)
            return _to_chips_spec(self.mix_pairs)

    hooks = Hooks()

    dx0, big, small = _local_step(x[0], loss_target[0], mods, norm_g_full, final_norm.reshape(1, d), ffn1_w, None, prm,
                                  fox_out_norm, gdn_out_norm, conv_full, nh, hooks)

    pack1 = jnp.concatenate([small["loss"], small["norm_g"].reshape(1, 3 * d), small["final_norm"], small["fox_wn"],
                             small["gdn_wn"], small["pg"][0:1], small["pg"][1:2], small["conv"].reshape(1, CONV_W * 3 * fw),
                             small["mod"]], axis=1)
    got1 = _allgather8("gather_small_grads", pack1)
    tot = _sum_devices(got1)
    o = [0]

    def take(n):
        o[0] += n
        return tot[:, o[0] - n:o[0]]

    loss = take(LANES)[0, 0]
    g_norm_g = lax.dynamic_slice_in_dim(take(3 * d).reshape(3, d), s_chip * dg_sh, dg_sh, axis=1)[None]
    g_final = take(d).reshape(d)
    g_fox_wn = take(HEAD_DIM)
    g_gdn_wn = take(HEAD_DIM)
    pg0, pg1 = take(LANES), take(LANES)
    g_fbias, g_dtb, g_alog = pg0[:, 0:nh], pg0[:, nh:2 * nh], pg1[:, nh:2 * nh]
    g_conv = lax.dynamic_slice_in_dim(take(CONV_W * 3 * fw).reshape(CONV_W, 3 * fw), s_chip * cv_sh, cv_sh, axis=1)[None]
    g_ada_b = take(9 * d)
    dmod_all = got1[:, 0, o[0] - 9 * d:o[0]]
    dmod_sh = lax.dynamic_slice_in_dim(dmod_all, s_chip * ncol, ncol, axis=1)
    (g_ada_w,) = _mm("ada_dw", [[(cond_all, dmod_sh)]], "tn", (2048, 512, 8), _ep_plain, (f32,))

    ffn1_mine = _rs_chip_sums("ffn1", hooks.ffn1_pairs, big["got_ffn1"], s_chip)
    ffn2_mine = _rs_chip_sums("ffn2", hooks.ffn2_pairs, big["got_ffn2"], s_chip)
    mix_mine = _rs_chip_sums("mix", hooks.mix_pairs, big["got_mix"], s_chip)
    r1g, r1u, r1d, r2g, r2u, r2d, r_win, r_wo = _rs_both_halves(ffn1_mine + ffn2_mine + mix_mine, ic)
    rows = lambda r: r.reshape(-1, d)
    g_ffn_gate = jnp.stack([rows(r1g).T, rows(r2g).T])[None]
    g_ffn_up = jnp.stack([rows(r1u).T, rows(r2u).T])[None]
    g_ffn_down = jnp.stack([rows(r1d), rows(r2d)])[None]
    g_w_in = rows(r_win)[:in_sh].T[None]
    g_w_out = rows(r_wo)[None]

    def upd(name, w, g, m, v):
        shp = w.shape
        two = lambda a: a.reshape(-1, shp[-1])
        return tuple(t.reshape(shp) for t in _adamw(name, two(w), two(g), two(m), two(v)))

    big_upd = [upd("adamw_ada_w", ada_w, g_ada_w[None], m_ada_w, v_ada_w),
               upd("adamw_ffn_gate", ffn_w_gate, g_ffn_gate, m_ffn_w_gate, v_ffn_w_gate),
               upd("adamw_ffn_up", ffn_w_up, g_ffn_up, m_ffn_w_up, v_ffn_w_up),
               upd("adamw_ffn_down", ffn_w_down, g_ffn_down, m_ffn_w_down, v_ffn_w_down),
               upd("adamw_w_in", w_in, g_w_in, m_w_in, v_w_in),
               upd("adamw_w_out", w_out, g_w_out, m_w_out, v_w_out)]
    small_w = [ada_b, norm_g, fox_f_bias, fox_out_norm, gdn_conv, gdn_A_log, gdn_dt_bias, gdn_out_norm, final_norm]
    small_g = [g_ada_b, g_norm_g, g_fbias, g_fox_wn, g_conv, g_alog, g_dtb, g_gdn_wn, g_final]
    small_m = [m_ada_b, m_norm_g, m_fox_f_bias, m_fox_out_norm, m_gdn_conv, m_gdn_A_log, m_gdn_dt_bias, m_gdn_out_norm, m_final_norm]
    small_v = [v_ada_b, v_norm_g, v_fox_f_bias, v_fox_out_norm, v_gdn_conv, v_gdn_A_log, v_gdn_dt_bias, v_gdn_out_norm, v_final_norm]
    sizes = [a.size for a in small_w]
    npad = -sum(sizes) % LANES
    flat = lambda arrs, fill: jnp.concatenate([a.reshape(1, -1) for a in arrs] + [jnp.full((1, npad), fill, f32)], axis=1)
    sd, sm_, sv = _adamw("adamw_small", flat(small_w, 0.0), flat(small_g, 0.0), flat(small_m, 0.0), flat(small_v, 1.0))

    def unflat(t):
        out, off = [], 0
        for a, n in zip(small_w, sizes):
            out.append(t[0, off:off + n].reshape(a.shape))
            off += n
        return out

    small_g = [g.reshape(a.shape) for g, a in zip(small_g, small_w)]
    s_d, s_m, s_v = unflat(sd), unflat(sm_), unflat(sv)
    def order(bigs, smalls):
        return [bigs[0], smalls[0], smalls[1], bigs[1], bigs[2], bigs[3], bigs[4], bigs[5]] + list(smalls[2:])

    grads_out = order([g_ada_w[None], g_ffn_gate, g_ffn_up, g_ffn_down, g_w_in, g_w_out], small_g)
    deltas = order([u[0] for u in big_upd], s_d)
    new_m = order([u[1] for u in big_upd], s_m)
    new_v = order([u[2] for u in big_upd], s_v)
    return (loss, dx0[None], *grads_out, *deltas, *new_m, *new_v)
```

```python
import functools
import math

import jax
import jax.numpy as jnp
from jax import lax
from jax.experimental import pallas as pl
from jax.experimental.pallas import tpu as pltpu

f32 = jnp.float32
bf16 = jnp.bfloat16
HI = lax.Precision.HIGHEST
MESH = pl.DeviceIdType.MESH

EPS = 1e-6
HEAD_DIM = 128
LANES = 128
GDN_CHUNK = 64
CONV_W = 4
MACARON_W = 0.5
ADAM_LR, ADAM_B1, ADAM_B2, ADAM_EPS, ADAM_WD, ADAM_STEP = 0.001, 0.9, 0.999, 1e-08, 0.01, 10
VMEM_LIMIT_V7X = 56 * 1024 * 1024
NEG = -1e30

NN = (((1,), (0,)), ((), ()))
NT = (((1,), (1,)), ((), ()))
TN = (((0,), (0,)), ((), ()))


def _cp(*sem):
    return pltpu.CompilerParams(dimension_semantics=sem, vmem_limit_bytes=VMEM_LIMIT_V7X)


def _dotb(a, b, dn=NN):
    return lax.dot_general(a.astype(bf16), b.astype(bf16), dn, preferred_element_type=f32)


def _doth(a, b, dn=NN):
    return lax.dot_general(a.astype(f32), b.astype(f32), dn, precision=HI, preferred_element_type=f32)


def _dotm(a, b, dn=NN):
    return lax.dot_general(a.astype(f32), b.astype(f32), dn, precision=lax.Precision.HIGH, preferred_element_type=f32)


def _sigmoid(x):
    return 1.0 / (1.0 + jnp.exp(-x))


def _softplus(x):
    return jnp.maximum(x, 0.0) + jnp.log(1.0 + jnp.exp(-jnp.abs(x)))


def _lane_col(blk, lane_idx):
    lane = lax.broadcasted_iota(jnp.int32, blk.shape, 1)
    return jnp.sum(jnp.where(lane == lane_idx, blk, 0.0), axis=1, keepdims=True)


def _tile(n, pref, mult=LANES):
    if n <= pref:
        return n
    t = (pref // mult) * mult
    while t >= mult:
        if n % t == 0:
            return t
        t -= mult
    return n


MM_EPILOGUE_CHUNKS = 2


def _mm(name, groups, mode, tiles, epilogue, out_dtypes, extras=(), comm=None):
    a0, b0 = groups[0][0]
    if mode == "nn":
        (m, k), n = a0.shape, b0.shape[1]
    elif mode == "nt":
        (m, k), n = a0.shape, b0.shape[0]
    else:
        (k, m), n = a0.shape, b0.shape[1]
    tm, tn, tk = _tile(m, tiles[0]), _tile(n, tiles[1]), _tile(k, tiles[2])
    nk = k // tk
    assert m % tm == 0 and n % tn == 0 and k % tk == 0, (name, m, n, k, tm, tn, tk)
    if mode == "nn":
        a_spec = pl.BlockSpec((tm, tk), lambda i, j, kk: (i, kk))
        b_spec = pl.BlockSpec((tk, tn), lambda i, j, kk: (kk, j))
        dn = NN
    elif mode == "nt":
        a_spec = pl.BlockSpec((tm, tk), lambda i, j, kk: (i, kk))
        b_spec = pl.BlockSpec((tn, tk), lambda i, j, kk: (j, kk))
        dn = NT
    else:
        a_spec = pl.BlockSpec((tk, tm), lambda i, j, kk: (kk, i))
        b_spec = pl.BlockSpec((tk, tn), lambda i, j, kk: (kk, j))
        dn = TN
    npairs = sum(len(g) for g in groups)
    nacc, nex, nout = len(groups), len(extras), len(out_dtypes)
    in_specs, args = [], []
    for g in groups:
        for a, b in g:
            in_specs += [a_spec, b_spec]
            args += [a, b]
    for arr, kind in extras:
        if kind == "mn":
            in_specs.append(pl.BlockSpec((tm, tn), lambda i, j, kk: (i, j)))
        else:
            in_specs.append(pl.BlockSpec((1, tn), lambda i, j, kk: (0, j)))
        args.append(arr)
    nci = len(comm.ins) if comm else 0
    nco = len(comm.out_shapes) if comm else 0
    grid = (m // tm, n // tn, nk)

    def body(*refs):
        ins = refs[: 2 * npairs]
        ex = refs[2 * npairs: 2 * npairs + nex]
        c_ins = refs[2 * npairs + nex: 2 * npairs + nex + nci]
        o0 = 2 * npairs + nex + nci
        outs = refs[o0: o0 + nout]
        c_outs = refs[o0 + nout: o0 + nout + nco]
        accs = refs[o0 + nout + nco: o0 + nout + nco + nacc]
        c_sems = refs[o0 + nout + nco + nacc:]
        kk = pl.program_id(2)
        if comm:
            step = (pl.program_id(0) * grid[1] + pl.program_id(1)) * nk + kk

            @pl.when(step == 0)
            def _():
                comm.run("start", c_ins, c_outs, c_sems)

        if nk == 1 and mode != "tn":
            nsub = MM_EPILOGUE_CHUNKS if tn % (MM_EPILOGUE_CHUNKS * LANES) == 0 else 1
            w = tn // nsub
            for cidx in range(nsub):
                lo = cidx * w
                sums, p = [], 0
                for g in groups:
                    t = None
                    for _ in g:
                        b = ins[2 * p + 1][lo:lo + w, :] if mode == "nt" else ins[2 * p + 1][:, lo:lo + w]
                        d = _dotb(ins[2 * p][...], b, dn)
                        t = d if t is None else t + d
                        p += 1
                    sums.append(t)
                res = epilogue(sums, [e[:, lo:lo + w] for e in ex])
                for o, r in zip(outs, res):
                    o[:, lo:lo + w] = r.astype(o.dtype)
        else:
            @pl.when(kk == 0)
            def _():
                for acc in accs:
                    acc[...] = jnp.zeros_like(acc)

            p = 0
            for gi, g in enumerate(groups):
                t = None
                for _ in g:
                    d = _dotb(ins[2 * p][...], ins[2 * p + 1][...], dn)
                    t = d if t is None else t + d
                    p += 1
                accs[gi][...] += t

            @pl.when(kk == nk - 1)
            def _():
                res = epilogue([acc[...] for acc in accs], [e[...] for e in ex])
                for o, r in zip(outs, res):
                    o[...] = r.astype(o.dtype)

        if comm:
            @pl.when(step == grid[0] * grid[1] * nk - 1)
            def _():
                comm.run("finish", c_ins, c_outs, c_sems)

    any_spec = pl.BlockSpec(memory_space=pl.ANY)
    res = pl.pallas_call(
        body, name=name, grid=grid,
        in_specs=in_specs + [any_spec] * nci,
        out_specs=[pl.BlockSpec((tm, tn), lambda i, j, kk: (i, j)) for _ in out_dtypes] + [any_spec] * nco,
        out_shape=[jax.ShapeDtypeStruct((m, n), dt) for dt in out_dtypes] + (list(comm.out_shapes) if comm else []),
        scratch_shapes=[pltpu.VMEM((tm, tn), f32) for _ in range(nacc)] + (comm.sems() if comm else []),
        compiler_params=_cp(*(("arbitrary",) * 3 if comm else ("parallel", "parallel", "arbitrary"))),
    )(*args, *(comm.ins if comm else []))
    return (res[:nout], res[nout:]) if comm else res


def _ep_plain(accs, ex):
    return (accs[0],)


def _ep_colscale(accs, ex):
    return (accs[0] * ex[0],)


def _ep_swiglu(accs, ex):
    gate, up = accs
    act = gate * _sigmoid(gate) * up
    return gate, up, act


def _ep_residual(accs, ex):
    x, gs = ex
    y = accs[0]
    return x + gs * y, y


def _ep_swiglu_bwd(accs, ex):
    gate, up = ex[0].astype(f32), ex[1].astype(f32)
    dact = accs[0]
    sg = _sigmoid(gate)
    silu = gate * sg
    act = silu * up
    dup = dact * silu
    dgate = dact * up * sg * (1.0 + gate * (1.0 - sg))
    return act, dgate, dup


def _row_tile(s):
    return _tile(s, 256, 8)


def _head_row_tile(s):
    return _tile(s, 1024, 8)


def _ada_in(name, x, g, shift, scale):
    s, d = x.shape
    tm = _row_tile(s)

    def body(x_ref, g_ref, sh_ref, sc_ref, h_ref):
        xv = x_ref[...]
        r = lax.rsqrt(jnp.mean(xv * xv, axis=-1, keepdims=True) + EPS)
        h_ref[...] = (xv * r * g_ref[...] * (1.0 + sc_ref[...]) + sh_ref[...]).astype(h_ref.dtype)

    row = pl.BlockSpec((1, d), lambda i: (0, 0))
    blk = pl.BlockSpec((tm, d), lambda i: (i, 0))
    return pl.pallas_call(body, name=name, grid=(s // tm,), in_specs=[blk, row, row, row], out_specs=blk,
                          out_shape=jax.ShapeDtypeStruct((s, d), bf16), compiler_params=_cp("parallel"))(x, g, shift, scale)


def _gate_terms(dxv, y_ref, gs_ref, dy_ref, dgs_ref):
    dy_ref[...] = (dxv * gs_ref[...]).astype(dy_ref.dtype)
    dgs_ref[...] += jnp.sum(dxv * y_ref[...].astype(f32), axis=0, keepdims=True)


def _ada_bwd(name, x, g, scale, dh, dres, y, gs):
    s, d = x.shape
    tm = _row_tile(s)

    def body(x_ref, g_ref, sc_ref, dh_ref, dres_ref, y_ref, gs_ref, dx_ref, dsh_ref, a_ref, dy_ref, dgs_ref):
        i = pl.program_id(0)

        @pl.when(i == 0)
        def _():
            dsh_ref[...] = jnp.zeros_like(dsh_ref)
            a_ref[...] = jnp.zeros_like(a_ref)
            dgs_ref[...] = jnp.zeros_like(dgs_ref)

        xv = x_ref[...]
        dhv = dh_ref[...].astype(f32)
        r = lax.rsqrt(jnp.mean(xv * xv, axis=-1, keepdims=True) + EPS)
        n = xv * r
        dn = dhv * (g_ref[...] * (1.0 + sc_ref[...]))
        dxv = dres_ref[...] + r * (dn - n * jnp.mean(dn * n, axis=-1, keepdims=True))
        dx_ref[...] = dxv
        dsh_ref[...] += jnp.sum(dhv, axis=0, keepdims=True)
        a_ref[...] += jnp.sum(dhv * n, axis=0, keepdims=True)
        _gate_terms(dxv, y_ref, gs_ref, dy_ref, dgs_ref)

    row = pl.BlockSpec((1, d), lambda i: (0, 0))
    blk = pl.BlockSpec((tm, d), lambda i: (i, 0))
    return pl.pallas_call(
        body, name=name, grid=(s // tm,), in_specs=[blk, row, row, blk, blk, blk, row], out_specs=[blk, row, row, blk, row],
        out_shape=[jax.ShapeDtypeStruct((s, d), f32), jax.ShapeDtypeStruct((1, d), f32), jax.ShapeDtypeStruct((1, d), f32),
                   jax.ShapeDtypeStruct((s, d), bf16), jax.ShapeDtypeStruct((1, d), f32)],
        compiler_params=_cp("arbitrary"))(x, g, scale, dh, dres, y, gs)


def _ada_bwd_first(name, x, g, scale, dh, dres):
    s, d = x.shape
    tm = _row_tile(s)

    def body(x_ref, g_ref, sc_ref, dh_ref, dres_ref, dx_ref, dsh_ref, a_ref):
        i = pl.program_id(0)

        @pl.when(i == 0)
        def _():
            dsh_ref[...] = jnp.zeros_like(dsh_ref)
            a_ref[...] = jnp.zeros_like(a_ref)

        xv = x_ref[...]
        dhv = dh_ref[...].astype(f32)
        r = lax.rsqrt(jnp.mean(xv * xv, axis=-1, keepdims=True) + EPS)
        n = xv * r
        dn = dhv * (g_ref[...] * (1.0 + sc_ref[...]))
        dx_ref[...] = dres_ref[...] + r * (dn - n * jnp.mean(dn * n, axis=-1, keepdims=True))
        dsh_ref[...] += jnp.sum(dhv, axis=0, keepdims=True)
        a_ref[...] += jnp.sum(dhv * n, axis=0, keepdims=True)

    row = pl.BlockSpec((1, d), lambda i: (0, 0))
    blk = pl.BlockSpec((tm, d), lambda i: (i, 0))
    return pl.pallas_call(
        body, name=name, grid=(s // tm,), in_specs=[blk, row, row, blk, blk], out_specs=[blk, row, row],
        out_shape=[jax.ShapeDtypeStruct((s, d), f32), jax.ShapeDtypeStruct((1, d), f32), jax.ShapeDtypeStruct((1, d), f32)],
        compiler_params=_cp("arbitrary"))(x, g, scale, dh, dres)


def _final_loss(x, fg, target, y, gs):
    s, d = x.shape
    tm = _row_tile(s)

    def body(x_ref, g_ref, t_ref, y_ref, gs_ref, loss_ref, dx_ref, dg_ref, dy_ref, dgs_ref):
        i = pl.program_id(0)

        @pl.when(i == 0)
        def _():
            loss_ref[...] = jnp.zeros_like(loss_ref)
            dg_ref[...] = jnp.zeros_like(dg_ref)
            dgs_ref[...] = jnp.zeros_like(dgs_ref)

        xv = x_ref[...]
        gv = g_ref[...]
        r = lax.rsqrt(jnp.mean(xv * xv, axis=-1, keepdims=True) + EPS)
        n = xv * r
        e = n * gv - t_ref[...]
        per_tok = jnp.mean(e * e, axis=-1, keepdims=True)
        loss_ref[...] += 0.5 * jnp.sum(per_tok, axis=0, keepdims=True) * jnp.ones((1, LANES), f32)
        dy = e * (1.0 / d)
        dg_ref[...] += jnp.sum(dy * n, axis=0, keepdims=True)
        dn = dy * gv
        dxv = r * (dn - n * jnp.mean(dn * n, axis=-1, keepdims=True))
        dx_ref[...] = dxv
        _gate_terms(dxv, y_ref, gs_ref, dy_ref, dgs_ref)

    row = pl.BlockSpec((1, d), lambda i: (0, 0))
    blk = pl.BlockSpec((tm, d), lambda i: (i, 0))
    return pl.pallas_call(
        body, name="final_loss", grid=(s // tm,), in_specs=[blk, row, blk, blk, row],
        out_specs=[pl.BlockSpec((1, LANES), lambda i: (0, 0)), blk, row, blk, row],
        out_shape=[jax.ShapeDtypeStruct((1, LANES), f32), jax.ShapeDtypeStruct((s, d), f32), jax.ShapeDtypeStruct((1, d), f32),
                   jax.ShapeDtypeStruct((s, d), bf16), jax.ShapeDtypeStruct((1, d), f32)],
        compiler_params=_cp("arbitrary"))(x, fg, target, y, gs)


def _small_fwd(ps, prm, nh):
    s = ps.shape[0]
    tb = LANES

    def body(ps_ref, prm_ref, sm_ref, cum_ref, carry):
        i = pl.program_id(0)

        @pl.when(i == 0)
        def _():
            carry[...] = jnp.zeros_like(carry)

        x = ps_ref[...]
        lane = lax.broadcasted_iota(jnp.int32, x.shape, 1)
        fb, dtb, alog = prm_ref[0:1, :], prm_ref[1:2, :], prm_ref[2:3, :]
        logf = -_softplus(-(x + fb))
        glog = -jnp.exp(alog) * _softplus(x + dtb)
        beta = _sigmoid(x)
        sm = jnp.where(lane < nh, logf, jnp.where(lane < 2 * nh, glog, jnp.where(lane < 3 * nh, beta, 0.0)))
        sm_ref[...] = sm
        r = lax.broadcasted_iota(jnp.int32, (tb, tb), 0)
        c = lax.broadcasted_iota(jnp.int32, (tb, tb), 1)
        tril = (c <= r).astype(f32)
        cs = _doth(tril, sm) + carry[...]
        cum_ref[...] = cs
        carry[...] = cs[tb - 1:tb, :]

    blk = pl.BlockSpec((tb, LANES), lambda i: (i, 0))
    return pl.pallas_call(
        body, name="small_fwd", grid=(s // tb,),
        in_specs=[blk, pl.BlockSpec((8, LANES), lambda i: (0, 0))],
        out_specs=[blk, blk],
        out_shape=[jax.ShapeDtypeStruct((s, LANES), f32), jax.ShapeDtypeStruct((s, LANES), f32)],
        scratch_shapes=[pltpu.VMEM((1, LANES), f32)],
        compiler_params=_cp("arbitrary"))(ps, prm)


def _small_bwd(ps, prm, dsm, dcum, nh):
    s = ps.shape[0]
    tb = LANES
    nb = s // tb

    def body(ps_ref, prm_ref, dsm_ref, dct_ref, dps_ref, pg_ref, carry):
        i = pl.program_id(0)

        @pl.when(i == 0)
        def _():
            carry[...] = jnp.zeros_like(carry)
            pg_ref[...] = jnp.zeros_like(pg_ref)

        x = ps_ref[...]
        dsm = dsm_ref[...]
        lane = lax.broadcasted_iota(jnp.int32, x.shape, 1)
        fb, dtb, alog = prm_ref[0:1, :], prm_ref[1:2, :], prm_ref[2:3, :]
        r = lax.broadcasted_iota(jnp.int32, (tb, tb), 0)
        c = lax.broadcasted_iota(jnp.int32, (tb, tb), 1)
        triu = (c >= r).astype(f32)
        dlogf = _doth(triu, dct_ref[...]) + carry[...]
        carry[...] = dlogf[0:1, :]
        d_f = dlogf * _sigmoid(-(x + fb))
        nega = -jnp.exp(alog)
        xa = x + dtb
        glog = nega * _softplus(xa)
        d_a = dsm * nega * _sigmoid(xa)
        beta = _sigmoid(x)
        d_b = dsm * beta * (1.0 - beta)
        dps = jnp.where(lane < nh, d_f, jnp.where(lane < 2 * nh, d_a, jnp.where(lane < 3 * nh, d_b, 0.0)))
        dps_ref[...] = dps.astype(dps_ref.dtype)
        row0 = jnp.sum(dps, axis=0, keepdims=True)
        row1 = jnp.sum(jnp.where((lane >= nh) & (lane < 2 * nh), dsm * glog, 0.0), axis=0, keepdims=True)
        sub = lax.broadcasted_iota(jnp.int32, (8, LANES), 0)
        pg_ref[...] += jnp.where(sub == 0, row0, jnp.where(sub == 1, row1, 0.0))

    rev = pl.BlockSpec((tb, LANES), lambda i: (nb - 1 - i, 0))
    fix = pl.BlockSpec((8, LANES), lambda i: (0, 0))
    return pl.pallas_call(
        body, name="small_bwd", grid=(nb,),
        in_specs=[rev, fix, rev, rev],
        out_specs=[rev, fix],
        out_shape=[jax.ShapeDtypeStruct((s, LANES), bf16), jax.ShapeDtypeStruct((8, LANES), f32)],
        scratch_shapes=[pltpu.VMEM((1, LANES), f32)],
        compiler_params=_cp("arbitrary"))(ps, prm, dsm, dcum)


LOG2E = 1.4426950408889634
LN2 = 0.6931471805599453
AUG = 2 * HEAD_DIM
FOX_Q_SCALE = LOG2E / math.sqrt(HEAD_DIM)
FOX_KEY_GROUP = 8


def _split3(col):
    hi = col.astype(bf16).astype(f32)
    r1 = col - hi
    mid = r1.astype(bf16).astype(f32)
    lo = (r1 - mid).astype(bf16).astype(f32)
    return hi, mid, lo


def _aug_block(rows, terms, terms_at, ones_at=None):
    lane = lax.broadcasted_iota(jnp.int32, (rows, LANES), 1)
    blk = jnp.zeros((rows, LANES), f32) if ones_at is None else jnp.where((lane >= ones_at) & (lane < ones_at + 3), 1.0, 0.0)
    for i, t in enumerate(terms):
        blk = jnp.where(lane == terms_at + i, t, blk)
    return blk


def _fox_aug(qkv, cum, nh):
    s = qkv.shape[0]
    tm = _head_row_tile(s)

    def body(q_ref, k_ref, v_ref, cum_ref, qa_ref, ka_ref, va_ref):
        h = pl.program_id(1)
        c2 = _lane_col(cum_ref[...], h) * LOG2E
        hi, mid, lo = _split3(c2)
        qa_ref[:, :HEAD_DIM] = q_ref[...]
        qa_ref[:, HEAD_DIM:] = _aug_block(tm, (hi, mid, lo), 0, 3).astype(bf16)
        ka_ref[:, :HEAD_DIM] = k_ref[...]
        ka_ref[:, HEAD_DIM:] = _aug_block(tm, (-hi, -mid, -lo), 3, 0).astype(bf16)
        va_ref[:, :HEAD_DIM] = v_ref[...]
        va_ref[:, HEAD_DIM:] = _aug_block(tm, (), 0, 0).astype(bf16)

    ab = pl.BlockSpec((tm, AUG), lambda i, h: (i, h))
    return pl.pallas_call(
        body, name="fox_aug", grid=(s // tm, nh),
        in_specs=[pl.BlockSpec((tm, HEAD_DIM), lambda i, h: (i, h)), pl.BlockSpec((tm, HEAD_DIM), lambda i, h: (i, nh + h)),
                  pl.BlockSpec((tm, HEAD_DIM), lambda i, h: (i, 2 * nh + h)), pl.BlockSpec((tm, LANES), lambda i, h: (i, 0))],
        out_specs=[ab, ab, ab], out_shape=[jax.ShapeDtypeStruct((s, nh * AUG), bf16)] * 3,
        compiler_params=_cp("parallel", "parallel"))(qkv, qkv, qkv, cum)


def _fox_fwd(qa, ka, qkv, wn, nh, tq):
    s = qa.shape[0]
    fw = nh * HEAD_DIM
    group = FOX_KEY_GROUP
    while group > s // tq:
        group //= 2

    def body(qa_ref, ka_ref, v_ref, wn_ref, o_ref, on_ref, lse_ref):
        i = pl.program_id(1)
        q = qa_ref[...]

        def logits_t(j, rows):
            return _dotb(ka_ref[pl.ds(pl.multiple_of(j * tq, tq), rows), :], q, NT)

        def pv_t(j, p_t):
            return _dotb(v_ref[pl.ds(pl.multiple_of(j * tq, tq), p_t.shape[0]), :], p_t, TN)

        def update(j0, blocks, carry):
            m, l, acc = carry
            parts = [(j0, blocks)] if blocks == 1 else [(j0, blocks // 2), (j0 + blocks // 2, blocks // 2)]
            ts = [logits_t(j, nb * tq) for j, nb in parts]
            mn = functools.reduce(jnp.maximum, [m] + [jnp.max(t, axis=0, keepdims=True) for t in ts])
            ps = [jnp.exp2(t - mn) for t in ts]
            alpha = jnp.exp2(m - mn)
            l = functools.reduce(jnp.add, [alpha * l] + [jnp.sum(p, axis=0, keepdims=True) for p in ps])
            acc = functools.reduce(jnp.add, [alpha * acc] + [pv_t(j, p) for (j, _), p in zip(parts, ps)])
            return mn, l, acc

        key = lax.broadcasted_iota(jnp.int32, (tq, tq), 0)
        qry = lax.broadcasted_iota(jnp.int32, (tq, tq), 1)
        t = jnp.where(key <= qry, logits_t(i, tq), NEG)
        m = jnp.max(t, axis=0, keepdims=True)
        p_t = jnp.exp2(t - m)
        carry = (m, jnp.sum(p_t, axis=0, keepdims=True), pv_t(i, p_t))
        carry = lax.fori_loop(0, i // group, lambda jj, c: update(group * jj, group, c), carry)
        start, part = group * (i // group), group // 2
        while part:
            has = lax.rem(i // part, 2)
            carry = lax.fori_loop(0, has, functools.partial(lambda _, c, j0, blocks: update(j0, blocks, c), j0=start, blocks=part), carry)
            start, part = start + part * has, part // 2
        m, l, acc = carry
        o = (acc / l).T
        o_ref[...] = o
        sub = lax.broadcasted_iota(jnp.int32, (LANES, tq), 0)
        lse_ref[0] = jnp.where(sub == 0, m + jnp.log2(l), 0.0).T
        r = lax.rsqrt(jnp.mean(o * o, axis=-1, keepdims=True) + EPS)
        on_ref[...] = (o * r * wn_ref[...]).astype(on_ref.dtype)

    hb = pl.BlockSpec((tq, HEAD_DIM), lambda h, i: (i, h))
    return pl.pallas_call(
        body, name="fox_fwd", grid=(nh, s // tq),
        in_specs=[pl.BlockSpec((tq, AUG), lambda h, i: (i, h)), pl.BlockSpec((s, AUG), lambda h, i: (0, h)),
                  pl.BlockSpec((s, HEAD_DIM), lambda h, i: (0, 2 * nh + h)), pl.BlockSpec((1, HEAD_DIM), lambda h, i: (0, 0))],
        out_specs=[hb, hb, pl.BlockSpec((1, tq, LANES), lambda h, i: (h, i, 0))],
        out_shape=[jax.ShapeDtypeStruct((s, fw), f32), jax.ShapeDtypeStruct((s, fw), bf16), jax.ShapeDtypeStruct((nh, s, LANES), f32)],
        compiler_params=_cp("parallel", "parallel"))(qa, ka, qkv, wn)


def _fox_post_bwd(don, o, lse2, cum, qkv, wn, nh):
    s, fw = o.shape
    tm = _head_row_tile(s)

    def body(don_ref, o_ref, lse_ref, cum_ref, q_ref, wn_ref, qb_ref, doa_ref, qbt_ref, dot_ref, dwn_ref):
        i = pl.program_id(0)
        h = pl.program_id(1)

        @pl.when((i == 0) & (h == 0))
        def _():
            dwn_ref[...] = jnp.zeros_like(dwn_ref)

        o = o_ref[...]
        don = don_ref[...].astype(f32)
        r = lax.rsqrt(jnp.mean(o * o, axis=-1, keepdims=True) + EPS)
        n = o * r
        dwn_ref[...] += jnp.sum(don * n, axis=0, keepdims=True)
        dn = don * wn_ref[...]
        do = r * (dn - n * jnp.mean(dn * n, axis=-1, keepdims=True))
        delta = jnp.sum(do * o, axis=-1, keepdims=True)
        a2 = _lane_col(cum_ref[...], h) * LOG2E - _lane_col(lse_ref[0], 0)
        q_aug = _aug_block(tm, _split3(a2), 0, 3)
        q = q_ref[...]
        qb_ref[:, :HEAD_DIM] = q
        qb_ref[:, HEAD_DIM:] = q_aug.astype(bf16)
        doa_ref[:, :HEAD_DIM] = do.astype(bf16)
        doa_ref[:, HEAD_DIM:] = _aug_block(tm, _split3(-delta), 0).astype(bf16)
        qbt_ref[:HEAD_DIM, :] = q.astype(f32).T.astype(bf16)
        qbt_ref[HEAD_DIM:, :] = q_aug.T.astype(bf16)
        dot_ref[...] = do.T.astype(bf16)

    hb = pl.BlockSpec((tm, HEAD_DIM), lambda i, h: (i, h))
    ab = pl.BlockSpec((tm, AUG), lambda i, h: (i, h))
    return pl.pallas_call(
        body, name="fox_post_bwd", grid=(s // tm, nh),
        in_specs=[hb, hb, pl.BlockSpec((1, tm, LANES), lambda i, h: (h, i, 0)), pl.BlockSpec((tm, LANES), lambda i, h: (i, 0)),
                  hb, pl.BlockSpec((1, HEAD_DIM), lambda i, h: (0, 0))],
        out_specs=[ab, ab, pl.BlockSpec((AUG, tm), lambda i, h: (h, i)), pl.BlockSpec((HEAD_DIM, tm), lambda i, h: (h, i)),
                   pl.BlockSpec((1, HEAD_DIM), lambda i, h: (0, 0))],
        out_shape=[jax.ShapeDtypeStruct((s, nh * AUG), bf16), jax.ShapeDtypeStruct((s, nh * AUG), bf16),
                   jax.ShapeDtypeStruct((nh * AUG, s), bf16), jax.ShapeDtypeStruct((nh * HEAD_DIM, s), bf16),
                   jax.ShapeDtypeStruct((1, HEAD_DIM), f32)],
        compiler_params=_cp("arbitrary", "arbitrary"))(don, o, lse2, cum, qkv, wn)


def _fox_bwd(qb, doa, qb_t, do_t, ka, va, nh, tq):
    s = qb.shape[0]
    nq = s // tq
    fw = nh * HEAD_DIM

    def body(qb_ref, doa_ref, qbt_ref, dot_ref, ka_ref, va_ref, dqx_ref, dkx_ref, dv_ref, dk_acc, dv_acc):
        j = pl.program_id(1)

        @pl.when(j == 0)
        def _():
            dqx_ref[...] = jnp.zeros_like(dqx_ref)

        kj = ka_ref[...]
        vj = va_ref[...]

        def tile(i, rows=tq, first=False, keep=None):
            off = pl.multiple_of(i * tq, tq)
            p = jnp.exp2(_dotb(qb_ref[pl.ds(off, rows), :], kj, NT))
            if keep is not None:
                p = jnp.where(keep, p, 0.0)
            ds = (p * _dotb(doa_ref[pl.ds(off, rows), :], vj, NT)).astype(bf16)
            dv = _dotb(dot_ref[:, pl.ds(off, rows)], p)
            dk = _dotb(qbt_ref[:, pl.ds(off, rows)], ds)
            if first:
                dv_acc[...] = dv
                dk_acc[...] = dk
            else:
                dv_acc[...] += dv
                dk_acc[...] += dk
            dqx_ref[pl.ds(off, rows), :] += _dotb(ds, kj)

        n = nq - 1 - j
        b0 = jnp.minimum(j, nq - 2)
        qpos = b0 * tq + lax.broadcasted_iota(jnp.int32, (2 * tq, tq), 0)
        kpos = j * tq + lax.broadcasted_iota(jnp.int32, (2 * tq, tq), 1)
        tile(b0, 2 * tq, first=True, keep=(kpos <= qpos) & ((qpos < (j + 1) * tq) | (lax.rem(n, 2) == 1)))

        first_pair = j + 1 + lax.rem(n, 2)
        pairs = n // 2

        def pair(ii, carry):
            tile(first_pair, 2 * tq)
            return carry

        def quad(ii, carry):
            tile(first_pair + 2 * lax.rem(pairs, 2) + 4 * ii, 4 * tq)
            return carry

        lax.fori_loop(0, lax.rem(pairs, 2), pair, 0)
        lax.fori_loop(0, pairs // 2, quad, 0)
        dkx_ref[...] = dk_acc[...].T
        dv_ref[...] = dv_acc[...].T.astype(dv_ref.dtype)

    panel = pl.BlockSpec((s, AUG), lambda h, j: (0, h))
    blk = pl.BlockSpec((tq, AUG), lambda h, j: (j, h))
    return pl.pallas_call(
        body, name="fox_bwd", grid=(nh, nq),
        in_specs=[panel, panel, pl.BlockSpec((AUG, s), lambda h, j: (h, 0)), pl.BlockSpec((HEAD_DIM, s), lambda h, j: (h, 0)),
                  blk, blk],
        out_specs=[panel, blk, pl.BlockSpec((tq, HEAD_DIM), lambda h, j: (j, h))],
        out_shape=[jax.ShapeDtypeStruct((s, nh * AUG), f32), jax.ShapeDtypeStruct((s, nh * AUG), f32),
                   jax.ShapeDtypeStruct((s, fw), bf16)],
        scratch_shapes=[pltpu.VMEM((AUG, tq), f32), pltpu.VMEM((HEAD_DIM, tq), f32)],
        compiler_params=_cp("parallel", "arbitrary"))(qb, doa, qb_t, do_t, ka, va)


def _fox_unpack(dqx, dkx, nh):
    s = dqx.shape[0]
    fw = nh * HEAD_DIM
    tm = _head_row_tile(s)

    def body(dqx_ref, dkx_ref, dq_ref, dk_ref, dcum_ref):
        h = pl.program_id(1)

        @pl.when(h == 0)
        def _():
            dcum_ref[...] = jnp.zeros_like(dcum_ref)

        dq_ref[...] = (dqx_ref[:, :HEAD_DIM] * (HEAD_DIM ** -0.5)).astype(dq_ref.dtype)
        dk_ref[...] = (dkx_ref[:, :HEAD_DIM] * LN2).astype(dk_ref.dtype)
        d = _lane_col(dqx_ref[:, HEAD_DIM:], 0) - _lane_col(dkx_ref[:, HEAD_DIM:], 3)
        lane = lax.broadcasted_iota(jnp.int32, (tm, LANES), 1)
        dcum_ref[...] += jnp.where(lane == h, d, 0.0)

    ab = pl.BlockSpec((tm, AUG), lambda i, h: (i, h))
    hb = pl.BlockSpec((tm, HEAD_DIM), lambda i, h: (i, h))
    return pl.pallas_call(
        body, name="fox_unpack", grid=(s // tm, nh), in_specs=[ab, ab],
        out_specs=[hb, hb, pl.BlockSpec((tm, LANES), lambda i, h: (i, 0))],
        out_shape=[jax.ShapeDtypeStruct((s, fw), bf16), jax.ShapeDtypeStruct((s, fw), bf16), jax.ShapeDtypeStruct((s, LANES), f32)],
        compiler_params=_cp("parallel", "arbitrary"))(dqx, dkx)


def _conv_pre(xx, w, tm):
    pre = None
    for k in range(CONV_W):
        sh = CONV_W - 1 - k
        t = (pltpu.roll(xx, sh, 0) if sh else xx)[8:, :] * w[k:k + 1, :]
        pre = t if pre is None else pre + t
    return pre


def _gdn_pre(x, w, nh):
    s, cw = x.shape
    tm = _row_tile(s)
    fw = nh * HEAD_DIM

    def body(x_ref, prev_ref, w_ref, y_ref):
        i = pl.program_id(0)
        j = pl.program_id(1)
        for h in range(nh):
            sl = slice(h * HEAD_DIM, (h + 1) * HEAD_DIM)
            prev = jnp.where(i == 0, 0.0, prev_ref[:, sl])
            pre = _conv_pre(jnp.concatenate([prev, x_ref[:, sl]], axis=0), w_ref[:, sl], tm)
            y = pre * _sigmoid(pre)
            yn = y * lax.rsqrt(jnp.sum(y * y, axis=-1, keepdims=True) + EPS)
            y_ref[:, sl] = jnp.where(j < 2, yn, y)

    return pl.pallas_call(
        body, name="gdn_pre", grid=(s // tm, cw // fw),
        in_specs=[pl.BlockSpec((tm, fw), lambda i, j: (i, j)),
                  pl.BlockSpec((8, fw), lambda i, j: (jnp.maximum(i * (tm // 8) - 1, 0), j)),
                  pl.BlockSpec((CONV_W, fw), lambda i, j: (0, j))],
        out_specs=pl.BlockSpec((tm, fw), lambda i, j: (i, j)),
        out_shape=jax.ShapeDtypeStruct((s, cw), f32),
        compiler_params=_cp("parallel", "parallel"))(x, x, w)


def _gdn_pre_bwd(x, w, dyn, nh):
    s, cw = x.shape
    tm = _row_tile(s)
    fw = nh * HEAD_DIM

    def body(x_ref, prev_ref, w_ref, dyn_ref, dpre_ref):
        i = pl.program_id(0)
        j = pl.program_id(1)
        for h in range(nh):
            sl = slice(h * HEAD_DIM, (h + 1) * HEAD_DIM)
            prev = jnp.where(i == 0, 0.0, prev_ref[:, sl])
            pre = _conv_pre(jnp.concatenate([prev, x_ref[:, sl]], axis=0), w_ref[:, sl], tm)
            sg = _sigmoid(pre)
            y = pre * sg
            dyn = dyn_ref[:, sl]
            r = lax.rsqrt(jnp.sum(y * y, axis=-1, keepdims=True) + EPS)
            yn = y * r
            dy_n = r * (dyn - yn * jnp.sum(dyn * yn, axis=-1, keepdims=True))
            dy = jnp.where(j < 2, dy_n, dyn)
            dpre_ref[:, sl] = dy * sg * (1.0 + pre * (1.0 - sg))

    hb = pl.BlockSpec((tm, fw), lambda i, j: (i, j))
    return pl.pallas_call(
        body, name="gdn_pre_bwd", grid=(s // tm, cw // fw),
        in_specs=[hb, pl.BlockSpec((8, fw), lambda i, j: (jnp.maximum(i * (tm // 8) - 1, 0), j)),
                  pl.BlockSpec((CONV_W, fw), lambda i, j: (0, j)), hb],
        out_specs=hb, out_shape=jax.ShapeDtypeStruct((s, cw), f32),
        compiler_params=_cp("parallel", "parallel"))(x, x, w, dyn)


def _conv_bwd(x, w, dpre, nh):
    s, cw = x.shape
    tm = _row_tile(s)
    fw = nh * HEAD_DIM
    ni = s // tm

    def body(x_ref, prev_ref, w_ref, dp_ref, nxt_ref, dx_ref, dw_ref):
        i = pl.program_id(1)

        @pl.when(i == 0)
        def _():
            dw_ref[...] = jnp.zeros_like(dw_ref)

        for h in range(nh):
            sl = slice(h * HEAD_DIM, (h + 1) * HEAD_DIM)
            wv = w_ref[:, sl]
            dp = dp_ref[:, sl]
            nxt = jnp.where(i == ni - 1, 0.0, nxt_ref[:, sl])
            dd = jnp.concatenate([dp, nxt], axis=0)
            prev = jnp.where(i == 0, 0.0, prev_ref[:, sl])
            xx = jnp.concatenate([prev, x_ref[:, sl]], axis=0)
            dx = None
            rows = []
            for k in range(CONV_W):
                sh = CONV_W - 1 - k
                t = (pltpu.roll(dd, tm + 8 - sh, 0) if sh else dd)[:tm, :] * wv[k:k + 1, :]
                dx = t if dx is None else dx + t
                xs = (pltpu.roll(xx, sh, 0) if sh else xx)[8:, :]
                rows.append(jnp.sum(dp * xs, axis=0, keepdims=True))
            dx_ref[:, sl] = dx.astype(dx_ref.dtype)
            dw_ref[:, sl] += jnp.concatenate(rows, axis=0)

    hb = pl.BlockSpec((tm, fw), lambda j, i: (i, j))
    wb = pl.BlockSpec((CONV_W, fw), lambda j, i: (0, j))
    return pl.pallas_call(
        body, name="conv_bwd", grid=(cw // fw, ni),
        in_specs=[hb, pl.BlockSpec((8, fw), lambda j, i: (jnp.maximum(i * (tm // 8) - 1, 0), j)), wb, hb,
                  pl.BlockSpec((8, fw), lambda j, i: (jnp.minimum((i + 1) * (tm // 8), s // 8 - 1), j))],
        out_specs=[hb, wb],
        out_shape=[jax.ShapeDtypeStruct((s, cw), bf16), jax.ShapeDtypeStruct((CONV_W, cw), f32)],
        compiler_params=_cp("parallel", "arbitrary"))(x, x, w, dpre, dpre)


def _chunk_consts():
    c = GDN_CHUNK
    r = lax.broadcasted_iota(jnp.int32, (c, c), 0)
    q = lax.broadcasted_iota(jnp.int32, (c, c), 1)
    return r >= q, r > q, (r == q).astype(f32)


def _chunk_head(qkvn, sm, gcs, gcs_t, h, nh):
    fw = nh * HEAD_DIM
    q = qkvn[:, h * HEAD_DIM:(h + 1) * HEAD_DIM] * (HEAD_DIM ** -0.5)
    k = qkvn[:, fw + h * HEAD_DIM: fw + (h + 1) * HEAD_DIM]
    v = qkvn[:, 2 * fw + h * HEAD_DIM: 2 * fw + (h + 1) * HEAD_DIM]
    beta = _lane_col(sm, 2 * nh + h)
    gc = _lane_col(gcs, nh + h)
    gc_row = gcs_t[nh + h: nh + h + 1, :]
    incl, strict, _ = _chunk_consts()
    decay = jnp.where(incl, jnp.exp(jnp.minimum(gc - gc_row, 0.0)), 0.0)
    eg = jnp.exp(gc)
    g_last = gc[GDN_CHUNK - 1:GDN_CHUNK, :]
    egl = jnp.exp(g_last)
    ekd = jnp.exp(g_last - gc)
    kb = k * beta
    vb = v * beta
    kk = _dotb(kb, k, NT)
    qk = _dotb(q, k, NT)
    return dict(q=q, k=k, v=v, beta=beta, gc=gc, decay=decay, eg=eg, egl=egl, ekd=ekd, kb=kb, vb=vb, kk=kk, qk=qk,
                incl=incl, strict=strict)


def _unit_lower_inverses(lows, eye):
    c = GDN_CHUNK
    ts = [eye - low for low in lows]
    ps = [_dotm(low, low) for low in lows]
    for _ in range(4):
        both = [_dotm(jnp.concatenate([p, t], axis=0), p) for p, t in zip(ps, ts)]
        ts = [t + b[c:] for t, b in zip(ts, both)]
        ps = [b[:c] for b in both]
    return [t + _dotm(t, p) for t, p in zip(ts, ps)]


def _gdn_fwd(qkvn, sm, z, wn, nh):
    s = qkvn.shape[0]
    c = GDN_CHUNK
    nc = s // c
    fw = nh * HEAD_DIM

    def body(qkvn_ref, sm_ref, z_ref, wn_ref, on_ref, o_ref, st_ref, ti_ref, state):
        ci = pl.program_id(0)

        @pl.when(ci == 0)
        def _():
            state[...] = jnp.zeros_like(state)

        qkvn_v = qkvn_ref[...]
        sm_v = sm_ref[...]
        incl, strict, eye = _chunk_consts()
        gcs = _doth(incl.astype(f32), sm_v)
        gcs_t = gcs.T
        heads = range(nh)
        es = [_chunk_head(qkvn_v, sm_v, gcs, gcs_t, h, nh) for h in heads]
        tinvs = _unit_lower_inverses([jnp.where(strict, e["kk"] * e["decay"], 0.0) for e in es], eye)
        uws = [_dotm(t, jnp.concatenate([e["vb"], e["kb"] * e["eg"]], axis=1)) for t, e in zip(tinvs, es)]
        sts = [state[h] for h in heads]
        wq_s = [_dotb(jnp.concatenate([uw[:, HEAD_DIM:], e["q"] * e["eg"]], axis=0), st) for uw, e, st in zip(uws, es, sts)]
        v_news = [uw[:, :HEAD_DIM] - ws[:c] for uw, ws in zip(uws, wq_s)]
        os_ = [ws[c:] + _dotb(jnp.where(incl, e["qk"] * e["decay"], 0.0), vn) for ws, e, vn in zip(wq_s, es, v_news)]
        upd = [_dotb(e["k"] * e["ekd"], vn, TN) for e, vn in zip(es, v_news)]
        for h in heads:
            st_ref[0, h] = sts[h]
            ti_ref[0, h] = tinvs[h]
            state[h] = sts[h] * es[h]["egl"] + upd[h]
            sl = slice(h * HEAD_DIM, (h + 1) * HEAD_DIM)
            o = os_[h]
            o_ref[:, sl] = o
            zz = z_ref[:, sl]
            r = lax.rsqrt(jnp.mean(o * o, axis=-1, keepdims=True) + EPS)
            on_ref[:, sl] = (o * r * wn_ref[...] * (zz * _sigmoid(zz))).astype(on_ref.dtype)

    return pl.pallas_call(
        body, name="gdn_fwd", grid=(nc,),
        in_specs=[pl.BlockSpec((c, 3 * fw), lambda i: (i, 0)), pl.BlockSpec((c, LANES), lambda i: (i, 0)),
                  pl.BlockSpec((c, fw), lambda i: (i, 0)), pl.BlockSpec((1, HEAD_DIM), lambda i: (0, 0))],
        out_specs=[pl.BlockSpec((c, fw), lambda i: (i, 0)), pl.BlockSpec((c, fw), lambda i: (i, 0)),
                   pl.BlockSpec((1, nh, HEAD_DIM, HEAD_DIM), lambda i: (i, 0, 0, 0)),
                   pl.BlockSpec((1, nh, c, c), lambda i: (i, 0, 0, 0))],
        out_shape=[jax.ShapeDtypeStruct((s, fw), bf16), jax.ShapeDtypeStruct((s, fw), f32),
                   jax.ShapeDtypeStruct((nc, nh, HEAD_DIM, HEAD_DIM), f32), jax.ShapeDtypeStruct((nc, nh, c, c), f32)],
        scratch_shapes=[pltpu.VMEM((nh, HEAD_DIM, HEAD_DIM), f32)],
        compiler_params=_cp("arbitrary"))(qkvn, sm, z, wn)


def _gdn_post_bwd(don, o, z, wn, nh):
    s, fw = o.shape
    tm = _head_row_tile(s)

    def body(don_ref, o_ref, z_ref, wn_ref, do_ref, dz_ref, dwn_ref):
        i = pl.program_id(0)
        h = pl.program_id(1)

        @pl.when((i == 0) & (h == 0))
        def _():
            dwn_ref[...] = jnp.zeros_like(dwn_ref)

        o = o_ref[...]
        zz = z_ref[...]
        don = don_ref[...].astype(f32)
        wv = wn_ref[...]
        r = lax.rsqrt(jnp.mean(o * o, axis=-1, keepdims=True) + EPS)
        n = o * r
        sg = _sigmoid(zz)
        silu = zz * sg
        dz_ref[...] = (don * n * wv * sg * (1.0 + zz * (1.0 - sg))).astype(dz_ref.dtype)
        dnw = don * silu
        dwn_ref[...] += jnp.sum(dnw * n, axis=0, keepdims=True)
        dn = dnw * wv
        do_ref[...] = r * (dn - n * jnp.mean(dn * n, axis=-1, keepdims=True))

    hb = pl.BlockSpec((tm, HEAD_DIM), lambda i, h: (i, h))
    wb = pl.BlockSpec((1, HEAD_DIM), lambda i, h: (0, 0))
    return pl.pallas_call(
        body, name="gdn_post_bwd", grid=(s // tm, nh), in_specs=[hb, hb, hb, wb], out_specs=[hb, hb, wb],
        out_shape=[jax.ShapeDtypeStruct((s, fw), f32), jax.ShapeDtypeStruct((s, fw), bf16),
                   jax.ShapeDtypeStruct((1, HEAD_DIM), f32)],
        compiler_params=_cp("arbitrary", "arbitrary"))(don, o, z, wn)


def _gdn_bwd(qkvn, sm, do, states, tinvs, nh):
    s = qkvn.shape[0]
    c = GDN_CHUNK
    nc = s // c
    fw = nh * HEAD_DIM

    def body(qkvn_ref, sm_ref, do_ref, st_ref, ti_ref, dqkvn_ref, dsm_ref, dstate):
        ci = pl.program_id(0)

        @pl.when(ci == 0)
        def _():
            dstate[...] = jnp.zeros_like(dstate)

        qkvn_v = qkvn_ref[...]
        sm_v = sm_ref[...]
        incl, strict, eye = _chunk_consts()
        inclf = incl.astype(f32)
        gcs = _doth(inclf, sm_v)
        gcs_t = gcs.T
        lane = lax.broadcasted_iota(jnp.int32, (c, LANES), 1)
        last_row = lax.broadcasted_iota(jnp.int32, (c, 1), 0) == c - 1
        ones_cl = jnp.ones((c, LANES), f32)
        each = lambda f: [f(h) for h in range(nh)]
        es = each(lambda h: _chunk_head(qkvn_v, sm_v, gcs, gcs_t, h, nh))
        tinv = each(lambda h: ti_ref[0, h])
        st = each(lambda h: st_ref[0, h])
        dst = each(lambda h: dstate[h])
        do = each(lambda h: do_ref[:, h * HEAD_DIM:(h + 1) * HEAD_DIM])
        kg = each(lambda h: es[h]["kb"] * es[h]["eg"])
        qg = each(lambda h: es[h]["q"] * es[h]["eg"])
        kd = each(lambda h: es[h]["k"] * es[h]["ekd"])
        u = each(lambda h: _dotm(tinv[h], es[h]["vb"]))
        w = each(lambda h: _dotm(tinv[h], kg[h]))
        a = each(lambda h: jnp.where(incl, es[h]["qk"] * es[h]["decay"], 0.0))
        v_new = each(lambda h: u[h] - _dotb(w[h], st[h]))
        dv_new = each(lambda h: _dotb(a[h], do[h], TN) + _dotb(kd[h], dst[h]))
        da = each(lambda h: jnp.where(incl, _dotb(do[h], v_new[h], NT), 0.0))
        dqg = each(lambda h: _dotb(do[h], st[h], NT))
        dkd = each(lambda h: _dotb(v_new[h], dst[h], NT))
        dglast = each(lambda h: es[h]["egl"] * jnp.sum(jnp.sum(dst[h] * st[h], axis=1, keepdims=True), axis=0, keepdims=True))
        dw = each(lambda h: -_dotb(dv_new[h], st[h], NT))
        new_dst = each(lambda h: _dotb(qg[h], do[h], TN) + es[h]["egl"] * dst[h] - _dotb(w[h], dv_new[h], TN))
        dtinv = each(lambda h: _dotm(dv_new[h], es[h]["vb"], NT) + _dotm(dw[h], kg[h], NT))
        dvb = each(lambda h: _dotm(tinv[h], dv_new[h], TN))
        dkg = each(lambda h: _dotm(tinv[h], dw[h], TN))
        tdt = each(lambda h: _dotm(tinv[h], dtinv[h], TN))
        dlow = each(lambda h: -_dotm(tdt[h], tinv[h], NT))
        dkk = each(lambda h: jnp.where(strict, dlow[h] * es[h]["decay"], 0.0))
        dqk = each(lambda h: da[h] * es[h]["decay"])
        darg = each(lambda h: (jnp.where(strict, dlow[h] * es[h]["kk"], 0.0) + da[h] * es[h]["qk"]) * es[h]["decay"])
        dgc = each(lambda h: jnp.sum(darg[h], axis=1, keepdims=True) - _doth(darg[h], ones_cl, TN)[:, 0:1])
        dkb = each(lambda h: _dotb(dkk[h], es[h]["k"]) + dkg[h] * es[h]["eg"])
        dk = each(lambda h: _dotb(dkk[h], es[h]["kb"], TN) + _dotb(dqk[h], es[h]["q"], TN) + dkd[h] * es[h]["ekd"]
                  + dkb[h] * es[h]["beta"])
        dq = each(lambda h: (_dotb(dqk[h], es[h]["k"]) + dqg[h] * es[h]["eg"]) * (HEAD_DIM ** -0.5))
        s_kd = each(lambda h: jnp.sum(dkd[h] * kd[h], axis=1, keepdims=True))
        dgc = each(lambda h: dgc[h] + jnp.sum(dkg[h] * kg[h] + dqg[h] * qg[h], axis=1, keepdims=True) - s_kd[h]
                   + jnp.where(last_row, jnp.sum(s_kd[h], axis=0, keepdims=True) + dglast[h], 0.0))
        dg = each(lambda h: _doth(inclf, dgc[h] * ones_cl, TN)[:, 0:1])
        dsm = jnp.zeros((c, LANES), f32)
        for h in range(nh):
            dstate[h] = new_dst[h]
            dbeta = jnp.sum(dkb[h] * es[h]["k"] + dvb[h] * es[h]["v"], axis=1, keepdims=True)
            dqkvn_ref[:, h * HEAD_DIM:(h + 1) * HEAD_DIM] = dq[h]
            dqkvn_ref[:, fw + h * HEAD_DIM: fw + (h + 1) * HEAD_DIM] = dk[h]
            dqkvn_ref[:, 2 * fw + h * HEAD_DIM: 2 * fw + (h + 1) * HEAD_DIM] = dvb[h] * es[h]["beta"]
            dsm = dsm + jnp.where(lane == nh + h, dg[h], 0.0) + jnp.where(lane == 2 * nh + h, dbeta, 0.0)
        dsm_ref[...] = dsm

    rev = lambda i: (nc - 1 - i, 0)
    rev4 = lambda i: (nc - 1 - i, 0, 0, 0)
    return pl.pallas_call(
        body, name="gdn_bwd", grid=(nc,),
        in_specs=[pl.BlockSpec((c, 3 * fw), rev), pl.BlockSpec((c, LANES), rev), pl.BlockSpec((c, fw), rev),
                  pl.BlockSpec((1, nh, HEAD_DIM, HEAD_DIM), rev4), pl.BlockSpec((1, nh, c, c), rev4)],
        out_specs=[pl.BlockSpec((c, 3 * fw), rev), pl.BlockSpec((c, LANES), rev)],
        out_shape=[jax.ShapeDtypeStruct((s, 3 * fw), f32), jax.ShapeDtypeStruct((s, LANES), f32)],
        scratch_shapes=[pltpu.VMEM((nh, HEAD_DIM, HEAD_DIM), f32)],
        compiler_params=_cp("arbitrary"))(qkvn, sm, do, states, tinvs)


MM_TILES = (1024, 512, 2048)
MM_TILES_TN = (512, 1024, 4096)
MM_TILES_F_DEEP = (1024, 512, 2816)
MM_TILES_LONG_K = (1024, 512, 2560)


def _hosted(res, comm):
    return res if comm else (res, None)


def _ffn_fwd(tag, x, g, mod3, w, comm_up=None, comm_down=None, wd_of=None):
    wg_t, wu_t, wd = w
    sh, sc, gt = mod3
    h = _ada_in(tag + "_ada", x, g, sh, sc)
    (gate, up, act), got_up = _hosted(_mm(tag + "_up", [[(h, wg_t)], [(h, wu_t)]], "nt", MM_TILES, _ep_swiglu,
                                          (bf16, bf16, bf16), comm=comm_up), comm_up)
    if wd_of:
        wd = wd_of(got_up)
    (xn, y), got_down = _hosted(_mm(tag + "_down", [[(act, wd)]], "nn", MM_TILES_F_DEEP, _ep_residual, (f32, bf16),
                                    extras=((x, "mn"), (MACARON_W * gt, "n")), comm=comm_down), comm_down)
    return xn, dict(x=x, h=h, gate=gate, up=up, y=y), got_up, got_down


def _ffn_bwd(tag, dxn, dy, dgs, res, g, mod3, w, below=None, comm_dact=None, comm_dh_of=None):
    wg_t, wu_t, wd = w
    sh, sc, gt = mod3
    (act, dgate, dup), got = _hosted(_mm(tag + "_dact", [[(dy, wd)]], "nt", MM_TILES, _ep_swiglu_bwd, (bf16, bf16, bf16),
                                         extras=((res["gate"], "mn"), (res["up"], "mn")), comm=comm_dact), comm_dact)
    (dwd,) = _mm(tag + "_dwd", [[(act, dy)]], "tn", MM_TILES_TN, _ep_plain, (bf16,))
    (dwg_t,) = _mm(tag + "_dwg", [[(dgate, res["h"])]], "tn", MM_TILES_TN, _ep_plain, (bf16,))
    (dwu_t,) = _mm(tag + "_dwu", [[(dup, res["h"])]], "tn", MM_TILES_TN, _ep_plain, (bf16,))
    comm_dh = comm_dh_of and comm_dh_of((dwg_t, dwu_t, dwd))
    (dh,), got_dh = _hosted(_mm(tag + "_dh", [[(dgate, wg_t), (dup, wu_t)]], "nn", MM_TILES_F_DEEP, _ep_plain, (bf16,), comm=comm_dh),
                            comm_dh)
    if below:
        dx, dsh, a, dy_below, dgs_below = _ada_bwd(tag + "_ada_bwd", res["x"], g, sc, dh, dxn, *below)
    else:
        (dx, dsh, a), dy_below, dgs_below = _ada_bwd_first(tag + "_ada_bwd", res["x"], g, sc, dh, dxn), None, None
    return dx, (dy_below, dgs_below), (dwg_t, dwu_t, dwd), (dsh, a * g, MACARON_W * dgs), a * (1.0 + sc), got, got_dh


def _local_step(x, target, mods, norm_g, final_norm, ffn1_w, later_w, prm, fox_wn, gdn_wn, conv_w, nh, hooks=None):
    s, d = x.shape
    fw = nh * HEAD_DIM
    tq = _tile(s, min(256, s // 2))
    g_rows = [norm_g[i:i + 1] for i in range(3)]
    m1, m2, m3 = mods[0:3], mods[3:6], mods[6:9]

    x1, r1, got_up, got_down = _ffn_fwd("ffn1", x, g_rows[0], m1, ffn1_w, hooks and hooks.gather_mix_spec(),
                                        hooks and hooks.gather_ffn2_spec(), hooks and hooks.ffn1_wd)
    if hooks:
        ffn1_w = ffn1_w[:2] + (hooks.ffn1_wd(got_up),)
    w_cat_t, w_out, ffn2_w = hooks.gathered(got_up, got_down) if hooks else later_w
    h2 = _ada_in("mix_ada", x1, g_rows[1], m2[0], m2[1])
    w_fox, w_gdn, w_z, w_s = w_cat_t[:3 * fw], w_cat_t[3 * fw:6 * fw], w_cat_t[6 * fw:7 * fw], w_cat_t[7 * fw:]
    colscale = jnp.concatenate([jnp.full((1, fw), FOX_Q_SCALE, f32), jnp.ones((1, 2 * fw), f32)], axis=1)
    (qkv_f,) = _mm("proj_fox", [[(h2, w_fox)]], "nt", MM_TILES, _ep_colscale, (bf16,), extras=((colscale, "n"),))
    (qkv_g,) = _mm("proj_gdn", [[(h2, w_gdn)]], "nt", MM_TILES, _ep_plain, (f32,))
    (z,) = _mm("proj_z", [[(h2, w_z)]], "nt", MM_TILES, _ep_plain, (f32,))
    (ps,) = _mm("proj_s", [[(h2, w_s)]], "nt", MM_TILES, _ep_plain, (f32,))
    sm, cum = _small_fwd(ps, prm, nh)
    qa, ka, va = _fox_aug(qkv_f, cum, nh)
    o_f, on_f, lse2 = _fox_fwd(qa, ka, qkv_f, fox_wn, nh, tq)
    qkvn = _gdn_pre(qkv_g, conv_w, nh)
    on_g, o_g, states, tinvs = _gdn_fwd(qkvn, sm, z, gdn_wn, nh)
    w_top, w_bot = w_out[:fw], w_out[fw:]
    x2, mix = _mm("mix_out", [[(on_f, w_top), (on_g, w_bot)]], "nn", MM_TILES, _ep_residual, (f32, bf16),
                  extras=((x1, "mn"), (m2[2], "n")))
    x3, r3, got_up2, _ = _ffn_fwd("ffn2", x2, g_rows[2], m3, ffn2_w, hooks and hooks.gather_ffn2_wd_spec(), None,
                                  hooks and hooks.ffn2_wd)
    if hooks:
        ffn2_w = ffn2_w[:2] + (hooks.ffn2_wd(got_up2),)
    loss, dx3, dfinal, dy3, dgs3 = _final_loss(x3, final_norm, target, r3["y"], MACARON_W * m3[2])

    dx2, (dmix, dgt2), dffn2, dmod3, dg3, _, _ = _ffn_bwd("ffn2", dx3, dy3, dgs3, r3, g_rows[2], m3, ffn2_w, (mix, m2[2]))
    rs_ffn2 = hooks and hooks.rs_ffn2_spec(dffn2)
    (don_f,) = _mm("mix_dof", [[(dmix, w_top)]], "nt", MM_TILES, _ep_plain, (f32,))
    (don_g,) = _mm("mix_dog", [[(dmix, w_bot)]], "nt", MM_TILES, _ep_plain, (f32,))
    (dw_top,) = _mm("mix_dwtop", [[(on_f, dmix)]], "tn", MM_TILES_TN, _ep_plain, (bf16,))
    (dw_bot,) = _mm("mix_dwbot", [[(on_g, dmix)]], "tn", MM_TILES_TN, _ep_plain, (bf16,))
    qb, doa, qb_t, do_t, dfox_wn = _fox_post_bwd(don_f, o_f, lse2, cum, qkv_f, fox_wn, nh)
    dqx, dkx, dv_f = _fox_bwd(qb, doa, qb_t, do_t, ka, va, nh, tq)
    dq_f, dk_f, dcum = _fox_unpack(dqx, dkx, nh)
    do_g, dz, dgdn_wn = _gdn_post_bwd(don_g, o_g, z, gdn_wn, nh)
    dqkvn, dsm = _gdn_bwd(qkvn, sm, do_g, states, tinvs, nh)
    dpre = _gdn_pre_bwd(qkv_g, conv_w, dqkvn, nh)
    dqkv_g, dconv = _conv_bwd(qkv_g, conv_w, dpre, nh)
    dps, pg = _small_bwd(ps, prm, dsm, dcum, nh)
    dproj = jnp.concatenate([dq_f, dk_f, dv_f, dqkv_g, dz, dps], axis=1)
    ((dw_cat_t,), got_ffn2) = _hosted(_mm("proj_dw", [[(dproj, h2)]], "tn", MM_TILES_TN, _ep_plain, (bf16,), comm=rs_ffn2), rs_ffn2)
    dw_out = jnp.concatenate([dw_top, dw_bot], axis=0)
    rs_mix = hooks and hooks.rs_mix_spec(dw_cat_t, dw_out)
    (dh2,) = _mm("proj_dh", [[(dproj, w_cat_t)]], "nn", MM_TILES_LONG_K, _ep_plain, (bf16,))
    dx1, dsh2, a2, dy1, dgs1 = _ada_bwd("mix_ada_bwd", x1, g_rows[1], m2[1], dh2, dx2, r1["y"], MACARON_W * m1[2])
    dmod2 = (dsh2, a2 * g_rows[1], dgt2)
    dg2 = a2 * (1.0 + m2[1])
    dx0, _, dffn1, dmod1, dg1, got_mix, got_ffn1 = _ffn_bwd("ffn1", dx1, dy1, dgs1, r1, g_rows[0], m1, ffn1_w, None, rs_mix,
                                                             hooks and hooks.rs_ffn1_spec)

    big = dict(ffn=(dffn1, dffn2), w_cat_t=dw_cat_t, w_out=dw_out, got_ffn2=got_ffn2, got_mix=got_mix, got_ffn1=got_ffn1)
    small = dict(loss=loss, norm_g=jnp.concatenate([dg1, dg2, dg3], axis=0), final_norm=dfinal, fox_wn=dfox_wn,
                 gdn_wn=dgdn_wn, pg=pg, conv=dconv, mod=jnp.concatenate(list(dmod1) + list(dmod2) + list(dmod3), axis=1))
    return dx0, big, small


def _w_in_row_groups(nh):
    fw = nh * HEAD_DIM
    sizes = [3 * fw, nh, 3 * fw, nh, nh, fw]
    offs = [0]
    for sz in sizes:
        offs.append(offs[-1] + sz)
    return [(offs[i], offs[i + 1]) for i in range(len(sizes))]


def _build_w_cat_t(w_in_t, nh):
    gq, gf, gg, ga, gb, gz = _w_in_row_groups(nh)
    d = w_in_t.shape[1]
    rows = lambda r: w_in_t[r[0]:r[1]]
    pad = jnp.zeros((LANES - 3 * nh, d), w_in_t.dtype)
    return jnp.concatenate([rows(gq), rows(gg), rows(gz), rows(gf), rows(ga), rows(gb), pad], axis=0)


def _split_dw_cat_t(dw_cat_t, nh):
    fw = nh * HEAD_DIM
    o = 7 * fw
    return jnp.concatenate([dw_cat_t[:3 * fw], dw_cat_t[o:o + nh], dw_cat_t[3 * fw:6 * fw], dw_cat_t[o + nh:o + 2 * nh],
                            dw_cat_t[o + 2 * nh:o + 3 * nh], dw_cat_t[6 * fw:7 * fw]], axis=0)


def _head_params(fox_f_bias, gdn_dt_bias, gdn_a_log, nh):
    z = jnp.zeros((8, LANES), f32)
    z = z.at[0, 0:nh].set(fox_f_bias.reshape(nh))
    z = z.at[1, nh:2 * nh].set(gdn_dt_bias.reshape(nh))
    z = z.at[2, nh:2 * nh].set(gdn_a_log.reshape(nh))
    return z


ANY = pl.BlockSpec(memory_space=pl.ANY)
IN_VMEM = pl.BlockSpec(memory_space=pltpu.VMEM)
N_OTHER_CHIPS = 3


def _place():
    x, y, c = lax.axis_index("x"), lax.axis_index("y"), lax.axis_index("c")
    chips = [(1 - x, y), (x, 1 - y), (1 - x, 1 - y)]
    return x, y, c, chips


def _allgather8(name, v):
    r, n = v.shape

    def body(v_ref, out_ref, send_sems, recv_sems, local_sem):
        x, y, c, _ = _place()
        me = 4 * x + 2 * y + c
        mine = pltpu.make_async_copy(v_ref, out_ref.at[me], local_sem)
        mine.start()
        copies = []
        for k in range(1, 8):
            fx, fy, fc = (k >> 2) & 1, (k >> 1) & 1, k & 1
            peer = (x + fx - 2 * x * fx, y + fy - 2 * y * fy, c + fc - 2 * c * fc)
            cp = pltpu.make_async_remote_copy(src_ref=v_ref, dst_ref=out_ref.at[me], send_sem=send_sems.at[k - 1],
                                              recv_sem=recv_sems.at[k - 1], device_id=peer, device_id_type=MESH)
            cp.start()
            copies.append(cp)
        for cp in copies:
            cp.wait()
        mine.wait()

    return pl.pallas_call(
        body, name=name, in_specs=[IN_VMEM], out_specs=IN_VMEM, out_shape=jax.ShapeDtypeStruct((8, r, n), v.dtype),
        scratch_shapes=[pltpu.SemaphoreType.DMA((7,)), pltpu.SemaphoreType.DMA((7,)), pltpu.SemaphoreType.DMA],
        compiler_params=pltpu.CompilerParams(vmem_limit_bytes=VMEM_LIMIT_V7X))(v)


class _CommSpec:
    def __init__(self, ins, out_shapes, sem_counts, run):
        self.ins, self.out_shapes, self.sem_counts, self.run = list(ins), list(out_shapes), sem_counts, run

    def sems(self):
        return [pltpu.SemaphoreType.DMA((n,)) for n in self.sem_counts]


def _run_comm(name, spec):
    ni, no = len(spec.ins), len(spec.out_shapes)

    def body(*refs):
        ins, outs, sems = refs[:ni], refs[ni:ni + no], refs[ni + no:]
        spec.run("start", ins, outs, sems)
        spec.run("finish", ins, outs, sems)

    return pl.pallas_call(body, name=name, in_specs=[ANY] * ni, out_specs=[ANY] * no, out_shape=spec.out_shapes,
                          scratch_shapes=spec.sems())(*spec.ins)


def _gather_spec(halves):
    nw = len(halves)

    def run(phase, ins, outs, sems):
        ici_send, ici_recv, d2d_send, d2d_recv = sems
        x, y, c, chips = _place()
        s = 2 * x + y
        sib = (x, y, 1 - c)

        def over_ici(w, j, dst):
            cx, cy = chips[j]
            return pltpu.make_async_remote_copy(src_ref=ins[w].at[c], dst_ref=dst, send_sem=ici_send.at[w * 3 + j],
                                                recv_sem=ici_recv.at[w * 3 + j], device_id=(cx, cy, c), device_id_type=MESH)

        def to_sibling(w, j, blk):
            return pltpu.make_async_remote_copy(src_ref=blk, dst_ref=blk, send_sem=d2d_send.at[w * 3 + j],
                                                recv_sem=d2d_recv.at[w * 3 + j], device_id=sib, device_id_type=MESH)

        pairs = [(w, j) for w in range(nw) for j in range(N_OTHER_CHIPS)]
        chip_of = lambda j: 2 * chips[j][0] + chips[j][1]
        if phase == "start":
            for w, j in pairs:
                over_ici(w, j, outs[w].at[c, s]).start()
            return
        for w, j in pairs:
            landed = outs[w].at[c, chip_of(j)]
            over_ici(w, j, landed).wait_recv()
            to_sibling(w, j, landed).start()
        for w, j in pairs:
            to_sibling(w, j, outs[w].at[1 - c, chip_of(j)]).wait_recv()
        for w, j in pairs:
            over_ici(w, j, outs[w].at[c, s]).wait_send()
            to_sibling(w, j, outs[w].at[c, chip_of(j)]).wait_send()

    n3 = nw * N_OTHER_CHIPS
    return _CommSpec(halves, [jax.ShapeDtypeStruct((2, 4) + h.shape[1:], h.dtype) for h in halves], [n3] * 4, run)


def _to_chips_spec(partials):
    nw = len(partials)

    def run(phase, ins, outs, sems):
        send_sems, recv_sems = sems
        x, y, c, chips = _place()
        for w in range(nw):
            for j, (cx, cy) in enumerate(chips):
                cp = pltpu.make_async_remote_copy(src_ref=ins[w].at[2 * cx + cy], dst_ref=outs[w].at[j],
                                                  send_sem=send_sems.at[w * 3 + j], recv_sem=recv_sems.at[w * 3 + j],
                                                  device_id=(cx, cy, c), device_id_type=MESH)
                if phase == "start":
                    cp.start()
                else:
                    cp.wait()

    n3 = nw * N_OTHER_CHIPS
    return _CommSpec(partials, [jax.ShapeDtypeStruct((3,) + a.shape[1:], a.dtype) for a in partials], [n3, n3], run)


def _send_to_sibling(name, srcs, other_half):
    nw = len(srcs)

    def body(*refs):
        ins, outs = refs[:nw], refs[nw:2 * nw]
        send_sems, recv_sems = refs[2 * nw:]
        x, y, c, _ = _place()
        cps = []
        for w in range(nw):
            cp = pltpu.make_async_remote_copy(src_ref=ins[w].at[1 - c] if other_half else ins[w], dst_ref=outs[w],
                                              send_sem=send_sems.at[w], recv_sem=recv_sems.at[w],
                                              device_id=(x, y, 1 - c), device_id_type=MESH)
            cp.start()
            cps.append(cp)
        for cp in cps:
            cp.wait()

    return pl.pallas_call(
        body, name=name, in_specs=[ANY] * nw, out_specs=[ANY] * nw,
        out_shape=[jax.ShapeDtypeStruct(a.shape[1:] if other_half else a.shape, a.dtype) for a in srcs],
        scratch_shapes=[pltpu.SemaphoreType.DMA((nw,)), pltpu.SemaphoreType.DMA((nw,))],
    )(*srcs)


def _add_pair(name, g, recv, c):
    _, nchip, r, d = g.shape
    tr = _tile(r, 512, 16)

    def body(c_ref, g_ref, r_ref, o_ref):
        o_ref[...] = (g_ref[...].astype(f32) + r_ref[...].astype(f32)).astype(o_ref.dtype)

    gs = pltpu.PrefetchScalarGridSpec(
        num_scalar_prefetch=1, grid=(nchip, r // tr),
        in_specs=[pl.BlockSpec((None, None, tr, d), lambda t, i, cr: (cr[0], t, i, 0)),
                  pl.BlockSpec((None, tr, d), lambda t, i, cr: (t, i, 0))],
        out_specs=pl.BlockSpec((None, tr, d), lambda t, i, cr: (t, i, 0)))
    return pl.pallas_call(body, name=name, grid_spec=gs, out_shape=jax.ShapeDtypeStruct((nchip, r, d), bf16),
                          compiler_params=_cp("parallel", "parallel"))(c.reshape(1).astype(jnp.int32), g, recv)


def _add_chips(name, p, recv, s_chip):
    _, r, d = p.shape
    tr = _tile(r, 512, 16)

    def body(s_ref, p_ref, r_ref, o_ref):
        o_ref[...] = ((p_ref[...].astype(f32) + r_ref[0].astype(f32)) + r_ref[1].astype(f32)) + r_ref[2].astype(f32)

    gs = pltpu.PrefetchScalarGridSpec(
        num_scalar_prefetch=1, grid=(r // tr,),
        in_specs=[pl.BlockSpec((None, tr, d), lambda i, sr: (sr[0], i, 0)),
                  pl.BlockSpec((3, tr, d), lambda i, sr: (0, i, 0))],
        out_specs=pl.BlockSpec((tr, d), lambda i, sr: (i, 0)))
    return pl.pallas_call(body, name=name, grid_spec=gs, out_shape=jax.ShapeDtypeStruct((r, d), f32),
                          compiler_params=_cp("parallel"))(s_chip.reshape(1).astype(jnp.int32), p, recv)


def _rs_pair_sums(tag, grads, c):
    from_sib = _send_to_sibling("rs_to_sibling_" + tag, grads, True)
    return [_add_pair("rs_add_pair_%s%d" % (tag, n), g, r, c) for n, (g, r) in enumerate(zip(grads, from_sib))]


def _rs_chip_sums(tag, partial, from_chips, s_chip):
    return [_add_chips("rs_add_chips_%s%d" % (tag, n), p, r, s_chip) for n, (p, r) in enumerate(zip(partial, from_chips))]


def _rs_both_halves(mine, c):
    theirs = _send_to_sibling("rs_exchange_halves", mine, False)
    return [jnp.where(c == 0, jnp.stack([a, b]), jnp.stack([b, a])) for a, b in zip(mine, theirs)]


def _sum_devices(v):
    n = v.shape[2]

    def body(v_ref, o_ref):
        t = v_ref[0]
        for k in range(1, 8):
            t = t + v_ref[k]
        o_ref[...] = t

    return pl.pallas_call(body, name="sum_devices", out_shape=jax.ShapeDtypeStruct((1, n), f32))(v)


def _silu_rows(v):
    def body(v_ref, o_ref):
        t = v_ref[...]
        o_ref[...] = t * _sigmoid(t)

    return pl.pallas_call(body, name="silu_cond", out_shape=jax.ShapeDtypeStruct(v.shape, f32))(v)


ADAMW_BLOCK_ELEMS = 600 * 1024


def _adamw(name, w, g, m, v):
    nl, r, cdim = w.shape
    tr = _tile(r, max(8, min(256, (ADAMW_BLOCK_ELEMS // cdim) // 8 * 8)), 8)
    c1 = 1.0 - ADAM_B1 ** ADAM_STEP
    c2 = 1.0 - ADAM_B2 ** ADAM_STEP

    def body(w_ref, g_ref, m_ref, v_ref, d_ref, mo_ref, vo_ref):
        gv = g_ref[...]
        mn = ADAM_B1 * m_ref[...] + (1.0 - ADAM_B1) * gv
        vn = ADAM_B2 * v_ref[...] + (1.0 - ADAM_B2) * (gv * gv)
        d_ref[...] = -ADAM_LR * ((mn / c1) / (jnp.sqrt(vn / c2) + ADAM_EPS) + ADAM_WD * w_ref[...])
        mo_ref[...] = mn
        vo_ref[...] = vn

    blk = pl.BlockSpec((None, tr, cdim), lambda l, i: (l, i, 0))
    return pl.pallas_call(body, name=name, grid=(nl, r // tr), in_specs=[blk] * 4, out_specs=[blk] * 3,
                          out_shape=[jax.ShapeDtypeStruct((nl, r, cdim), f32)] * 3,
                          compiler_params=_cp("parallel", "parallel"))(w, g, m, v)


def _ep_bias(accs, ex):
    return (accs[0] + ex[0],)


def kernel(x, c, ada_w, ada_b, norm_g, ffn_w_gate, ffn_w_up, ffn_w_down, w_in, w_out, fox_f_bias, fox_out_norm, gdn_conv, gdn_A_log, gdn_dt_bias, gdn_out_norm, final_norm, loss_target, m_ada_w, m_ada_b, m_norm_g, m_ffn_w_gate, m_ffn_w_up, m_ffn_w_down, m_w_in, m_w_out, m_fox_f_bias, m_fox_out_norm, m_gdn_conv, m_gdn_A_log, m_gdn_dt_bias, m_gdn_out_norm, m_final_norm, v_ada_w, v_ada_b, v_norm_g, v_ffn_w_gate, v_ffn_w_up, v_ffn_w_down, v_w_in, v_w_out, v_fox_f_bias, v_fox_out_norm, v_gdn_conv, v_gdn_A_log, v_gdn_dt_bias, v_gdn_out_norm, v_final_norm):
    ix, iy, ic = lax.axis_index("x"), lax.axis_index("y"), lax.axis_index("c")
    s_chip = 2 * ix + iy
    me = 4 * ix + 2 * iy + ic
    _, s, d = x.shape
    nh = d // (2 * HEAD_DIM)
    fw = nh * HEAD_DIM
    ncol = ada_w.shape[2]
    dg_sh = norm_g.shape[2]
    cv_sh = gdn_conv.shape[2]
    ff_sh = ffn_w_gate.shape[3]
    in_sh = w_in.shape[2]
    in_pad = -(-in_sh // 32) * 32
    out_sh = w_out.shape[1]
    per_chip = lambda a, t: a[2 * t]

    pack0 = jnp.concatenate([_silu_rows(c), norm_g[0].reshape(1, 3 * dg_sh), gdn_conv[0].reshape(1, CONV_W * cv_sh)], axis=1)
    got0 = _allgather8("gather_cond", pack0)
    cond_all = got0[:, 0, :d]
    norm_g_full = jnp.concatenate([per_chip(got0, t)[0, d:d + 3 * dg_sh].reshape(3, dg_sh) for t in range(4)], axis=1)
    conv_full = jnp.concatenate([per_chip(got0, t)[0, d + 3 * dg_sh:].reshape(CONV_W, cv_sh) for t in range(4)], axis=1)

    ada_b_sh = lax.dynamic_slice_in_dim(ada_b, s_chip * ncol, ncol, axis=1)
    (mod_sh,) = _mm("ada_mod", [[(cond_all, ada_w[0])]], "nn", (8, 512, 2048), _ep_bias, (f32,), extras=((ada_b_sh, "n"),))
    mod_all = _allgather8("gather_mod", mod_sh)
    mod = jnp.concatenate([lax.dynamic_index_in_dim(per_chip(mod_all, t), me, axis=0, keepdims=True) for t in range(4)], axis=1)
    mods = [mod[:, i * d:(i + 1) * d] for i in range(9)]

    halved = lambda a: a.reshape(2, a.shape[0] // 2, d)
    ffn_halves = [[halved(ffn_w_gate[0, j].T.astype(bf16)), halved(ffn_w_up[0, j].T.astype(bf16)),
                   halved(ffn_w_down[0, j].astype(bf16))] for j in range(2)]
    mix_halves = [halved(jnp.pad(w_in[0].T.astype(bf16), ((0, in_pad - in_sh), (0, 0)))), halved(w_out[0].astype(bf16))]
    with_own = lambda got, hs: [lax.dynamic_update_slice(g, h[:, None], (0, s_chip, 0, 0)) for g, h in zip(got, hs)]
    ffn_full = lambda got, hs: tuple(g.reshape(4 * ff_sh, d) for g in with_own(got, hs))
    ffn_blocks = lambda grads: [g.reshape(2, 4, ff_sh // 2, d) for g in grads]
    ffn1_w = ffn_full(_run_comm("gather_ffn1", _gather_spec(ffn_halves[0][:2])), ffn_halves[0][:2]) + (None,)
    prm = _head_params(fox_f_bias, gdn_dt_bias, gdn_A_log, nh)

    class Hooks:
        def gather_mix_spec(self):
            return _gather_spec(mix_halves + ffn_halves[0][2:])

        def ffn1_wd(self, got):
            return ffn_full(got[2:], ffn_halves[0][2:])[0]

        def gather_ffn2_spec(self):
            return _gather_spec(ffn_halves[1][:2])

        def gather_ffn2_wd_spec(self):
            return _gather_spec(ffn_halves[1][2:])

        def ffn2_wd(self, got):
            return ffn_full(got, ffn_halves[1][2:])[0]

        def gathered(self, got_mix, got_ffn2):
            g_win, g_wo = with_own(got_mix[:2], mix_halves)
            w_in_t = jnp.swapaxes(g_win, 0, 1).reshape(4, in_pad, d)[:, :in_sh].reshape(4 * in_sh, d)
            return (_build_w_cat_t(w_in_t, nh), jnp.swapaxes(g_wo, 0, 1).reshape(4 * out_sh, d),
                    ffn_full(got_ffn2, ffn_halves[1][:2]) + (None,))

        def rs_ffn2_spec(self, dffn2):
            self.ffn2_pairs = _rs_pair_sums("ffn2", ffn_blocks(dffn2), ic)
            return _to_chips_spec(self.ffn2_pairs)

        def rs_ffn1_spec(self, dffn1):
            self.ffn1_pairs = _rs_pair_sums("ffn1", ffn_blocks(dffn1), ic)
            return _to_chips_spec(self.ffn1_pairs)

        def rs_mix_spec(self, dw_cat_t, dw_out):
            dw_in_t = jnp.pad(_split_dw_cat_t(dw_cat_t, nh).reshape(4, in_sh, d), ((0, 0), (0, in_pad - in_sh), (0, 0)))
            grads = [jnp.swapaxes(dw_in_t.reshape(4, 2, in_pad // 2, d), 0, 1),
                     jnp.swapaxes(dw_out.reshape(4, 2, out_sh // 2, d), 0, 1)]
            self.mix_pairs = _rs_pair_sums("mix", grads, ic)
            return _to_chips_spec(self.mix_pairs)

    hooks = Hooks()

    dx0, big, small = _local_step(x[0], loss_target[0], mods, norm_g_full, final_norm.reshape(1, d), ffn1_w, None, prm,
                                  fox_out_norm, gdn_out_norm, conv_full, nh, hooks)

    pack1 = jnp.concatenate([small["loss"], small["norm_g"].reshape(1, 3 * d), small["final_norm"], small["fox_wn"],
                             small["gdn_wn"], small["pg"][0:1], small["pg"][1:2], small["conv"].reshape(1, CONV_W * 3 * fw),
                             small["mod"]], axis=1)
    got1 = _allgather8("gather_small_grads", pack1)
    tot = _sum_devices(got1)
    o = [0]

    def take(n):
        o[0] += n
        return tot[:, o[0] - n:o[0]]

    loss = take(LANES)[0, 0]
    g_norm_g = lax.dynamic_slice_in_dim(take(3 * d).reshape(3, d), s_chip * dg_sh, dg_sh, axis=1)[None]
    g_final = take(d).reshape(d)
    g_fox_wn = take(HEAD_DIM)
    g_gdn_wn = take(HEAD_DIM)
    pg0, pg1 = take(LANES), take(LANES)
    g_fbias, g_dtb, g_alog = pg0[:, 0:nh], pg0[:, nh:2 * nh], pg1[:, nh:2 * nh]
    g_conv = lax.dynamic_slice_in_dim(take(CONV_W * 3 * fw).reshape(CONV_W, 3 * fw), s_chip * cv_sh, cv_sh, axis=1)[None]
    g_ada_b = take(9 * d)
    dmod_all = got1[:, 0, o[0] - 9 * d:o[0]]
    dmod_sh = lax.dynamic_slice_in_dim(dmod_all, s_chip * ncol, ncol, axis=1)
    (g_ada_w,) = _mm("ada_dw", [[(cond_all, dmod_sh)]], "tn", (2048, 512, 8), _ep_plain, (f32,))

    ffn1_mine = _rs_chip_sums("ffn1", hooks.ffn1_pairs, big["got_ffn1"], s_chip)
    ffn2_mine = _rs_chip_sums("ffn2", hooks.ffn2_pairs, big["got_ffn2"], s_chip)
    mix_mine = _rs_chip_sums("mix", hooks.mix_pairs, big["got_mix"], s_chip)
    r1g, r1u, r1d, r2g, r2u, r2d, r_win, r_wo = _rs_both_halves(ffn1_mine + ffn2_mine + mix_mine, ic)
    rows = lambda r: r.reshape(-1, d)
    g_ffn_gate = jnp.stack([rows(r1g).T, rows(r2g).T])[None]
    g_ffn_up = jnp.stack([rows(r1u).T, rows(r2u).T])[None]
    g_ffn_down = jnp.stack([rows(r1d), rows(r2d)])[None]
    g_w_in = rows(r_win)[:in_sh].T[None]
    g_w_out = rows(r_wo)[None]

    def upd(name, w, g, m, v):
        shp = w.shape
        three = lambda a: a.reshape((-1,) + shp[-2:])
        return tuple(t.reshape(shp) for t in _adamw(name, three(w), three(g), three(m), three(v)))

    big_upd = [upd("adamw_ada_w", ada_w, g_ada_w[None], m_ada_w, v_ada_w),
               upd("adamw_ffn_gate", ffn_w_gate, g_ffn_gate, m_ffn_w_gate, v_ffn_w_gate),
               upd("adamw_ffn_up", ffn_w_up, g_ffn_up, m_ffn_w_up, v_ffn_w_up),
               upd("adamw_ffn_down", ffn_w_down, g_ffn_down, m_ffn_w_down, v_ffn_w_down),
               upd("adamw_w_in", w_in, g_w_in, m_w_in, v_w_in),
               upd("adamw_w_out", w_out, g_w_out, m_w_out, v_w_out)]
    small_w = [ada_b, norm_g, fox_f_bias, fox_out_norm, gdn_conv, gdn_A_log, gdn_dt_bias, gdn_out_norm, final_norm]
    small_g = [g_ada_b, g_norm_g, g_fbias, g_fox_wn, g_conv, g_alog, g_dtb, g_gdn_wn, g_final]
    small_m = [m_ada_b, m_norm_g, m_fox_f_bias, m_fox_out_norm, m_gdn_conv, m_gdn_A_log, m_gdn_dt_bias, m_gdn_out_norm, m_final_norm]
    small_v = [v_ada_b, v_norm_g, v_fox_f_bias, v_fox_out_norm, v_gdn_conv, v_gdn_A_log, v_gdn_dt_bias, v_gdn_out_norm, v_final_norm]
    sizes = [a.size for a in small_w]
    npad = -sum(sizes) % LANES
    flat = lambda arrs, fill: jnp.concatenate([a.reshape(1, -1) for a in arrs] + [jnp.full((1, npad), fill, f32)], axis=1)[None]
    sd, sm_, sv = _adamw("adamw_small", flat(small_w, 0.0), flat(small_g, 0.0), flat(small_m, 0.0), flat(small_v, 1.0))

    def unflat(t):
        out, off = [], 0
        for a, n in zip(small_w, sizes):
            out.append(t[0, 0, off:off + n].reshape(a.shape))
            off += n
        return out

    small_g = [g.reshape(a.shape) for g, a in zip(small_g, small_w)]
    s_d, s_m, s_v = unflat(sd), unflat(sm_), unflat(sv)
    def order(bigs, smalls):
        return [bigs[0], smalls[0], smalls[1], bigs[1], bigs[2], bigs[3], bigs[4], bigs[5]] + list(smalls[2:])

    grads_out = order([g_ada_w[None], g_ffn_gate, g_ffn_up, g_ffn_down, g_w_in, g_w_out], small_g)
    deltas = order([u[0] for u in big_upd], s_d)
    new_m = order([u[1] for u in big_upd], s_m)
    new_v = order([u[2] for u in big_upd], s_v)
    return (loss, dx0[None], *grads_out, *deltas, *new_m, *new_v)
```

```python
import functools
import math

import jax
import jax.numpy as jnp
from jax import lax
from jax.experimental import pallas as pl
from jax.experimental.pallas import tpu as pltpu

f32 = jnp.float32
bf16 = jnp.bfloat16
HI = lax.Precision.HIGHEST
MESH = pl.DeviceIdType.MESH

EPS = 1e-6
HEAD_DIM = 128
LANES = 128
GDN_CHUNK = 64
CONV_W = 4
MACARON_W = 0.5
ADAM_LR, ADAM_B1, ADAM_B2, ADAM_EPS, ADAM_WD, ADAM_STEP = 0.001, 0.9, 0.999, 1e-08, 0.01, 10
VMEM_LIMIT_V7X = 56 * 1024 * 1024
NEG = -1e30

NN = (((1,), (0,)), ((), ()))
NT = (((1,), (1,)), ((), ()))
TN = (((0,), (0,)), ((), ()))


def _cp(*sem):
    return pltpu.CompilerParams(dimension_semantics=sem, vmem_limit_bytes=VMEM_LIMIT_V7X)


def _dotb(a, b, dn=NN):
    return lax.dot_general(a.astype(bf16), b.astype(bf16), dn, preferred_element_type=f32)


def _doth(a, b, dn=NN):
    return lax.dot_general(a.astype(f32), b.astype(f32), dn, precision=HI, preferred_element_type=f32)


def _dotm(a, b, dn=NN):
    return lax.dot_general(a.astype(f32), b.astype(f32), dn, precision=lax.Precision.HIGH, preferred_element_type=f32)


def _sigmoid(x):
    return 1.0 / (1.0 + jnp.exp(-x))


def _softplus(x):
    return jnp.maximum(x, 0.0) + jnp.log(1.0 + jnp.exp(-jnp.abs(x)))


def _lane_col(blk, lane_idx):
    lane = lax.broadcasted_iota(jnp.int32, blk.shape, 1)
    return jnp.sum(jnp.where(lane == lane_idx, blk, 0.0), axis=1, keepdims=True)


def _tile(n, pref, mult=LANES):
    if n <= pref:
        return n
    t = (pref // mult) * mult
    while t >= mult:
        if n % t == 0:
            return t
        t -= mult
    return n


MM_EPILOGUE_CHUNKS = 2


def _mm(name, groups, mode, tiles, epilogue, out_dtypes, extras=(), comm=None):
    a0, b0 = groups[0][0]
    if mode == "nn":
        (m, k), n = a0.shape, b0.shape[1]
    elif mode == "nt":
        (m, k), n = a0.shape, b0.shape[0]
    else:
        (k, m), n = a0.shape, b0.shape[1]
    tm, tn, tk = _tile(m, tiles[0]), _tile(n, tiles[1]), _tile(k, tiles[2])
    nk = k // tk
    assert m % tm == 0 and n % tn == 0 and k % tk == 0, (name, m, n, k, tm, tn, tk)
    if mode == "nn":
        a_spec = pl.BlockSpec((tm, tk), lambda i, j, kk: (i, kk))
        b_spec = pl.BlockSpec((tk, tn), lambda i, j, kk: (kk, j))
        dn = NN
    elif mode == "nt":
        a_spec = pl.BlockSpec((tm, tk), lambda i, j, kk: (i, kk))
        b_spec = pl.BlockSpec((tn, tk), lambda i, j, kk: (j, kk))
        dn = NT
    else:
        a_spec = pl.BlockSpec((tk, tm), lambda i, j, kk: (kk, i))
        b_spec = pl.BlockSpec((tk, tn), lambda i, j, kk: (kk, j))
        dn = TN
    npairs = sum(len(g) for g in groups)
    nacc, nex, nout = len(groups), len(extras), len(out_dtypes)
    in_specs, args = [], []
    for g in groups:
        for a, b in g:
            in_specs += [a_spec, b_spec]
            args += [a, b]
    for arr, kind in extras:
        if kind == "mn":
            in_specs.append(pl.BlockSpec((tm, tn), lambda i, j, kk: (i, j)))
        else:
            in_specs.append(pl.BlockSpec((1, tn), lambda i, j, kk: (0, j)))
        args.append(arr)
    nci = len(comm.ins) if comm else 0
    nco = len(comm.out_shapes) if comm else 0
    grid = (m // tm, n // tn, nk)

    def body(*refs):
        ins = refs[: 2 * npairs]
        ex = refs[2 * npairs: 2 * npairs + nex]
        c_ins = refs[2 * npairs + nex: 2 * npairs + nex + nci]
        o0 = 2 * npairs + nex + nci
        outs = refs[o0: o0 + nout]
        c_outs = refs[o0 + nout: o0 + nout + nco]
        accs = refs[o0 + nout + nco: o0 + nout + nco + nacc]
        c_sems = refs[o0 + nout + nco + nacc:]
        kk = pl.program_id(2)
        if comm:
            step = (pl.program_id(0) * grid[1] + pl.program_id(1)) * nk + kk

            @pl.when(step == 0)
            def _():
                comm.run("start", c_ins, c_outs, c_sems)

        if nk == 1 and mode != "tn":
            nsub = MM_EPILOGUE_CHUNKS if tn % (MM_EPILOGUE_CHUNKS * LANES) == 0 else 1
            w = tn // nsub
            for cidx in range(nsub):
                lo = cidx * w
                sums, p = [], 0
                for g in groups:
                    t = None
                    for _ in g:
                        b = ins[2 * p + 1][lo:lo + w, :] if mode == "nt" else ins[2 * p + 1][:, lo:lo + w]
                        d = _dotb(ins[2 * p][...], b, dn)
                        t = d if t is None else t + d
                        p += 1
                    sums.append(t)
                res = epilogue(sums, [e[:, lo:lo + w] for e in ex])
                for o, r in zip(outs, res):
                    o[:, lo:lo + w] = r.astype(o.dtype)
        else:
            @pl.when(kk == 0)
            def _():
                for acc in accs:
                    acc[...] = jnp.zeros_like(acc)

            p = 0
            for gi, g in enumerate(groups):
                t = None
                for _ in g:
                    d = _dotb(ins[2 * p][...], ins[2 * p + 1][...], dn)
                    t = d if t is None else t + d
                    p += 1
                accs[gi][...] += t

            @pl.when(kk == nk - 1)
            def _():
                res = epilogue([acc[...] for acc in accs], [e[...] for e in ex])
                for o, r in zip(outs, res):
                    o[...] = r.astype(o.dtype)

        if comm:
            @pl.when(step == grid[0] * grid[1] * nk - 1)
            def _():
                comm.run("finish", c_ins, c_outs, c_sems)

    any_spec = pl.BlockSpec(memory_space=pl.ANY)
    res = pl.pallas_call(
        body, name=name, grid=grid,
        in_specs=in_specs + [any_spec] * nci,
        out_specs=[pl.BlockSpec((tm, tn), lambda i, j, kk: (i, j)) for _ in out_dtypes] + [any_spec] * nco,
        out_shape=[jax.ShapeDtypeStruct((m, n), dt) for dt in out_dtypes] + (list(comm.out_shapes) if comm else []),
        scratch_shapes=[pltpu.VMEM((tm, tn), f32) for _ in range(nacc)] + (comm.sems() if comm else []),
        compiler_params=_cp(*(("arbitrary",) * 3 if comm else ("parallel", "parallel", "arbitrary"))),
    )(*args, *(comm.ins if comm else []))
    return (res[:nout], res[nout:]) if comm else res


def _ep_plain(accs, ex):
    return (accs[0],)


def _ep_colscale(accs, ex):
    return (accs[0] * ex[0],)


def _ep_swiglu(accs, ex):
    gate, up = accs
    act = gate * _sigmoid(gate) * up
    return gate, up, act


def _ep_residual(accs, ex):
    x, gs = ex
    y = accs[0]
    return x + gs * y, y


def _ep_swiglu_bwd(accs, ex):
    gate, up = ex[0].astype(f32), ex[1].astype(f32)
    dact = accs[0]
    sg = _sigmoid(gate)
    silu = gate * sg
    act = silu * up
    dup = dact * silu
    dgate = dact * up * sg * (1.0 + gate * (1.0 - sg))
    return act, dgate, dup


def _row_tile(s):
    return _tile(s, 256, 8)


def _head_row_tile(s):
    return _tile(s, 1024, 8)


def _ada_in(name, x, g, shift, scale):
    s, d = x.shape
    tm = _row_tile(s)

    def body(x_ref, g_ref, sh_ref, sc_ref, h_ref):
        xv = x_ref[...]
        r = lax.rsqrt(jnp.mean(xv * xv, axis=-1, keepdims=True) + EPS)
        h_ref[...] = (xv * r * g_ref[...] * (1.0 + sc_ref[...]) + sh_ref[...]).astype(h_ref.dtype)

    row = pl.BlockSpec((1, d), lambda i: (0, 0))
    blk = pl.BlockSpec((tm, d), lambda i: (i, 0))
    return pl.pallas_call(body, name=name, grid=(s // tm,), in_specs=[blk, row, row, row], out_specs=blk,
                          out_shape=jax.ShapeDtypeStruct((s, d), bf16), compiler_params=_cp("parallel"))(x, g, shift, scale)


def _gate_terms(dxv, y_ref, gs_ref, dy_ref, dgs_ref):
    dy_ref[...] = (dxv * gs_ref[...]).astype(dy_ref.dtype)
    dgs_ref[...] += jnp.sum(dxv * y_ref[...].astype(f32), axis=0, keepdims=True)


def _ada_bwd(name, x, g, scale, dh, dres, y, gs):
    s, d = x.shape
    tm = _row_tile(s)

    def body(x_ref, g_ref, sc_ref, dh_ref, dres_ref, y_ref, gs_ref, dx_ref, dsh_ref, a_ref, dy_ref, dgs_ref):
        i = pl.program_id(0)

        @pl.when(i == 0)
        def _():
            dsh_ref[...] = jnp.zeros_like(dsh_ref)
            a_ref[...] = jnp.zeros_like(a_ref)
            dgs_ref[...] = jnp.zeros_like(dgs_ref)

        xv = x_ref[...]
        dhv = dh_ref[...].astype(f32)
        r = lax.rsqrt(jnp.mean(xv * xv, axis=-1, keepdims=True) + EPS)
        n = xv * r
        dn = dhv * (g_ref[...] * (1.0 + sc_ref[...]))
        dxv = dres_ref[...] + r * (dn - n * jnp.mean(dn * n, axis=-1, keepdims=True))
        dx_ref[...] = dxv
        dsh_ref[...] += jnp.sum(dhv, axis=0, keepdims=True)
        a_ref[...] += jnp.sum(dhv * n, axis=0, keepdims=True)
        _gate_terms(dxv, y_ref, gs_ref, dy_ref, dgs_ref)

    row = pl.BlockSpec((1, d), lambda i: (0, 0))
    blk = pl.BlockSpec((tm, d), lambda i: (i, 0))
    return pl.pallas_call(
        body, name=name, grid=(s // tm,), in_specs=[blk, row, row, blk, blk, blk, row], out_specs=[blk, row, row, blk, row],
        out_shape=[jax.ShapeDtypeStruct((s, d), f32), jax.ShapeDtypeStruct((1, d), f32), jax.ShapeDtypeStruct((1, d), f32),
                   jax.ShapeDtypeStruct((s, d), bf16), jax.ShapeDtypeStruct((1, d), f32)],
        compiler_params=_cp("arbitrary"))(x, g, scale, dh, dres, y, gs)


def _ada_bwd_first(name, x, g, scale, dh, dres):
    s, d = x.shape
    tm = _row_tile(s)

    def body(x_ref, g_ref, sc_ref, dh_ref, dres_ref, dx_ref, dsh_ref, a_ref):
        i = pl.program_id(0)

        @pl.when(i == 0)
        def _():
            dsh_ref[...] = jnp.zeros_like(dsh_ref)
            a_ref[...] = jnp.zeros_like(a_ref)

        xv = x_ref[...]
        dhv = dh_ref[...].astype(f32)
        r = lax.rsqrt(jnp.mean(xv * xv, axis=-1, keepdims=True) + EPS)
        n = xv * r
        dn = dhv * (g_ref[...] * (1.0 + sc_ref[...]))
        dx_ref[...] = dres_ref[...] + r * (dn - n * jnp.mean(dn * n, axis=-1, keepdims=True))
        dsh_ref[...] += jnp.sum(dhv, axis=0, keepdims=True)
        a_ref[...] += jnp.sum(dhv * n, axis=0, keepdims=True)

    row = pl.BlockSpec((1, d), lambda i: (0, 0))
    blk = pl.BlockSpec((tm, d), lambda i: (i, 0))
    return pl.pallas_call(
        body, name=name, grid=(s // tm,), in_specs=[blk, row, row, blk, blk], out_specs=[blk, row, row],
        out_shape=[jax.ShapeDtypeStruct((s, d), f32), jax.ShapeDtypeStruct((1, d), f32), jax.ShapeDtypeStruct((1, d), f32)],
        compiler_params=_cp("arbitrary"))(x, g, scale, dh, dres)


def _final_loss(x, fg, target, y, gs):
    s, d = x.shape
    tm = _row_tile(s)

    def body(x_ref, g_ref, t_ref, y_ref, gs_ref, loss_ref, dx_ref, dg_ref, dy_ref, dgs_ref):
        i = pl.program_id(0)

        @pl.when(i == 0)
        def _():
            loss_ref[...] = jnp.zeros_like(loss_ref)
            dg_ref[...] = jnp.zeros_like(dg_ref)
            dgs_ref[...] = jnp.zeros_like(dgs_ref)

        xv = x_ref[...]
        gv = g_ref[...]
        r = lax.rsqrt(jnp.mean(xv * xv, axis=-1, keepdims=True) + EPS)
        n = xv * r
        e = n * gv - t_ref[...]
        per_tok = jnp.mean(e * e, axis=-1, keepdims=True)
        loss_ref[...] += 0.5 * jnp.sum(per_tok, axis=0, keepdims=True) * jnp.ones((1, LANES), f32)
        dy = e * (1.0 / d)
        dg_ref[...] += jnp.sum(dy * n, axis=0, keepdims=True)
        dn = dy * gv
        dxv = r * (dn - n * jnp.mean(dn * n, axis=-1, keepdims=True))
        dx_ref[...] = dxv
        _gate_terms(dxv, y_ref, gs_ref, dy_ref, dgs_ref)

    row = pl.BlockSpec((1, d), lambda i: (0, 0))
    blk = pl.BlockSpec((tm, d), lambda i: (i, 0))
    return pl.pallas_call(
        body, name="final_loss", grid=(s // tm,), in_specs=[blk, row, blk, blk, row],
        out_specs=[pl.BlockSpec((1, LANES), lambda i: (0, 0)), blk, row, blk, row],
        out_shape=[jax.ShapeDtypeStruct((1, LANES), f32), jax.ShapeDtypeStruct((s, d), f32), jax.ShapeDtypeStruct((1, d), f32),
                   jax.ShapeDtypeStruct((s, d), bf16), jax.ShapeDtypeStruct((1, d), f32)],
        compiler_params=_cp("arbitrary"))(x, fg, target, y, gs)


def _small_fwd(ps, prm, nh):
    s = ps.shape[0]
    tb = LANES

    def body(ps_ref, prm_ref, sm_ref, cum_ref, carry):
        i = pl.program_id(0)

        @pl.when(i == 0)
        def _():
            carry[...] = jnp.zeros_like(carry)

        x = ps_ref[...]
        lane = lax.broadcasted_iota(jnp.int32, x.shape, 1)
        fb, dtb, alog = prm_ref[0:1, :], prm_ref[1:2, :], prm_ref[2:3, :]
        logf = -_softplus(-(x + fb))
        glog = -jnp.exp(alog) * _softplus(x + dtb)
        beta = _sigmoid(x)
        sm = jnp.where(lane < nh, logf, jnp.where(lane < 2 * nh, glog, jnp.where(lane < 3 * nh, beta, 0.0)))
        sm_ref[...] = sm
        r = lax.broadcasted_iota(jnp.int32, (tb, tb), 0)
        c = lax.broadcasted_iota(jnp.int32, (tb, tb), 1)
        tril = (c <= r).astype(f32)
        cs = _doth(tril, sm) + carry[...]
        cum_ref[...] = cs
        carry[...] = cs[tb - 1:tb, :]

    blk = pl.BlockSpec((tb, LANES), lambda i: (i, 0))
    return pl.pallas_call(
        body, name="small_fwd", grid=(s // tb,),
        in_specs=[blk, pl.BlockSpec((8, LANES), lambda i: (0, 0))],
        out_specs=[blk, blk],
        out_shape=[jax.ShapeDtypeStruct((s, LANES), f32), jax.ShapeDtypeStruct((s, LANES), f32)],
        scratch_shapes=[pltpu.VMEM((1, LANES), f32)],
        compiler_params=_cp("arbitrary"))(ps, prm)


def _small_bwd(ps, prm, dsm, dcum, nh):
    s = ps.shape[0]
    tb = LANES
    nb = s // tb

    def body(ps_ref, prm_ref, dsm_ref, dct_ref, dps_ref, pg_ref, carry):
        i = pl.program_id(0)

        @pl.when(i == 0)
        def _():
            carry[...] = jnp.zeros_like(carry)
            pg_ref[...] = jnp.zeros_like(pg_ref)

        x = ps_ref[...]
        dsm = dsm_ref[...]
        lane = lax.broadcasted_iota(jnp.int32, x.shape, 1)
        fb, dtb, alog = prm_ref[0:1, :], prm_ref[1:2, :], prm_ref[2:3, :]
        r = lax.broadcasted_iota(jnp.int32, (tb, tb), 0)
        c = lax.broadcasted_iota(jnp.int32, (tb, tb), 1)
        triu = (c >= r).astype(f32)
        dlogf = _doth(triu, dct_ref[...]) + carry[...]
        carry[...] = dlogf[0:1, :]
        d_f = dlogf * _sigmoid(-(x + fb))
        nega = -jnp.exp(alog)
        xa = x + dtb
        glog = nega * _softplus(xa)
        d_a = dsm * nega * _sigmoid(xa)
        beta = _sigmoid(x)
        d_b = dsm * beta * (1.0 - beta)
        dps = jnp.where(lane < nh, d_f, jnp.where(lane < 2 * nh, d_a, jnp.where(lane < 3 * nh, d_b, 0.0)))
        dps_ref[...] = dps.astype(dps_ref.dtype)
        row0 = jnp.sum(dps, axis=0, keepdims=True)
        row1 = jnp.sum(jnp.where((lane >= nh) & (lane < 2 * nh), dsm * glog, 0.0), axis=0, keepdims=True)
        sub = lax.broadcasted_iota(jnp.int32, (8, LANES), 0)
        pg_ref[...] += jnp.where(sub == 0, row0, jnp.where(sub == 1, row1, 0.0))

    rev = pl.BlockSpec((tb, LANES), lambda i: (nb - 1 - i, 0))
    fix = pl.BlockSpec((8, LANES), lambda i: (0, 0))
    return pl.pallas_call(
        body, name="small_bwd", grid=(nb,),
        in_specs=[rev, fix, rev, rev],
        out_specs=[rev, fix],
        out_shape=[jax.ShapeDtypeStruct((s, LANES), bf16), jax.ShapeDtypeStruct((8, LANES), f32)],
        scratch_shapes=[pltpu.VMEM((1, LANES), f32)],
        compiler_params=_cp("arbitrary"))(ps, prm, dsm, dcum)


LOG2E = 1.4426950408889634
LN2 = 0.6931471805599453
AUG = 2 * HEAD_DIM
FOX_Q_SCALE = LOG2E / math.sqrt(HEAD_DIM)
FOX_KEY_GROUP = 8


def _split3(col):
    hi = col.astype(bf16).astype(f32)
    r1 = col - hi
    mid = r1.astype(bf16).astype(f32)
    lo = (r1 - mid).astype(bf16).astype(f32)
    return hi, mid, lo


def _aug_block(rows, terms, terms_at, ones_at=None):
    lane = lax.broadcasted_iota(jnp.int32, (rows, LANES), 1)
    blk = jnp.zeros((rows, LANES), f32) if ones_at is None else jnp.where((lane >= ones_at) & (lane < ones_at + 3), 1.0, 0.0)
    for i, t in enumerate(terms):
        blk = jnp.where(lane == terms_at + i, t, blk)
    return blk


def _fox_aug(qkv, cum, nh):
    s = qkv.shape[0]
    tm = _head_row_tile(s)

    def body(q_ref, k_ref, v_ref, cum_ref, qa_ref, ka_ref, va_ref):
        h = pl.program_id(1)
        c2 = _lane_col(cum_ref[...], h) * LOG2E
        hi, mid, lo = _split3(c2)
        qa_ref[:, :HEAD_DIM] = q_ref[...]
        qa_ref[:, HEAD_DIM:] = _aug_block(tm, (hi, mid, lo), 0, 3).astype(bf16)
        ka_ref[:, :HEAD_DIM] = k_ref[...]
        ka_ref[:, HEAD_DIM:] = _aug_block(tm, (-hi, -mid, -lo), 3, 0).astype(bf16)
        va_ref[:, :HEAD_DIM] = v_ref[...]
        va_ref[:, HEAD_DIM:] = _aug_block(tm, (), 0, 0).astype(bf16)

    ab = pl.BlockSpec((tm, AUG), lambda i, h: (i, h))
    return pl.pallas_call(
        body, name="fox_aug", grid=(s // tm, nh),
        in_specs=[pl.BlockSpec((tm, HEAD_DIM), lambda i, h: (i, h)), pl.BlockSpec((tm, HEAD_DIM), lambda i, h: (i, nh + h)),
                  pl.BlockSpec((tm, HEAD_DIM), lambda i, h: (i, 2 * nh + h)), pl.BlockSpec((tm, LANES), lambda i, h: (i, 0))],
        out_specs=[ab, ab, ab], out_shape=[jax.ShapeDtypeStruct((s, nh * AUG), bf16)] * 3,
        compiler_params=_cp("parallel", "parallel"))(qkv, qkv, qkv, cum)


def _fox_fwd(qa, ka, qkv, wn, nh, tq):
    s = qa.shape[0]
    fw = nh * HEAD_DIM
    group = FOX_KEY_GROUP
    while group > s // tq:
        group //= 2

    def body(qa_ref, ka_ref, v_ref, wn_ref, o_ref, on_ref, lse_ref):
        i = pl.program_id(1)
        q = qa_ref[...]

        def logits_t(j, rows):
            return _dotb(ka_ref[pl.ds(pl.multiple_of(j * tq, tq), rows), :], q, NT)

        def pv_t(j, p_t):
            return _dotb(v_ref[pl.ds(pl.multiple_of(j * tq, tq), p_t.shape[0]), :], p_t, TN)

        def update(j0, blocks, carry):
            m, l, acc = carry
            parts = [(j0, blocks)] if blocks == 1 else [(j0, blocks // 2), (j0 + blocks // 2, blocks // 2)]
            ts = [logits_t(j, nb * tq) for j, nb in parts]
            mn = functools.reduce(jnp.maximum, [m] + [jnp.max(t, axis=0, keepdims=True) for t in ts])
            ps = [jnp.exp2(t - mn) for t in ts]
            alpha = jnp.exp2(m - mn)
            l = functools.reduce(jnp.add, [alpha * l] + [jnp.sum(p, axis=0, keepdims=True) for p in ps])
            acc = functools.reduce(jnp.add, [alpha * acc] + [pv_t(j, p) for (j, _), p in zip(parts, ps)])
            return mn, l, acc

        key = lax.broadcasted_iota(jnp.int32, (tq, tq), 0)
        qry = lax.broadcasted_iota(jnp.int32, (tq, tq), 1)
        t = jnp.where(key <= qry, logits_t(i, tq), NEG)
        m = jnp.max(t, axis=0, keepdims=True)
        p_t = jnp.exp2(t - m)
        carry = (m, jnp.sum(p_t, axis=0, keepdims=True), pv_t(i, p_t))
        carry = lax.fori_loop(0, i // group, lambda jj, c: update(group * jj, group, c), carry)
        start, part = group * (i // group), group // 2
        while part:
            has = lax.rem(i // part, 2)
            carry = lax.fori_loop(0, has, functools.partial(lambda _, c, j0, blocks: update(j0, blocks, c), j0=start, blocks=part), carry)
            start, part = start + part * has, part // 2
        m, l, acc = carry
        o = (acc / l).T
        o_ref[...] = o
        sub = lax.broadcasted_iota(jnp.int32, (LANES, tq), 0)
        lse_ref[0] = jnp.where(sub == 0, m + jnp.log2(l), 0.0).T
        r = lax.rsqrt(jnp.mean(o * o, axis=-1, keepdims=True) + EPS)
        on_ref[...] = (o * r * wn_ref[...]).astype(on_ref.dtype)

    hb = pl.BlockSpec((tq, HEAD_DIM), lambda h, i: (i, h))
    return pl.pallas_call(
        body, name="fox_fwd", grid=(nh, s // tq),
        in_specs=[pl.BlockSpec((tq, AUG), lambda h, i: (i, h)), pl.BlockSpec((s, AUG), lambda h, i: (0, h)),
                  pl.BlockSpec((s, HEAD_DIM), lambda h, i: (0, 2 * nh + h)), pl.BlockSpec((1, HEAD_DIM), lambda h, i: (0, 0))],
        out_specs=[hb, hb, pl.BlockSpec((1, tq, LANES), lambda h, i: (h, i, 0))],
        out_shape=[jax.ShapeDtypeStruct((s, fw), f32), jax.ShapeDtypeStruct((s, fw), bf16), jax.ShapeDtypeStruct((nh, s, LANES), f32)],
        compiler_params=_cp("parallel", "parallel"))(qa, ka, qkv, wn)


def _fox_post_bwd(don, o, lse2, cum, qkv, wn, nh):
    s, fw = o.shape
    tm = _head_row_tile(s)

    def body(don_ref, o_ref, lse_ref, cum_ref, q_ref, wn_ref, qb_ref, doa_ref, qbt_ref, dot_ref, dwn_ref):
        i = pl.program_id(0)
        h = pl.program_id(1)

        @pl.when((i == 0) & (h == 0))
        def _():
            dwn_ref[...] = jnp.zeros_like(dwn_ref)

        o = o_ref[...]
        don = don_ref[...].astype(f32)
        r = lax.rsqrt(jnp.mean(o * o, axis=-1, keepdims=True) + EPS)
        n = o * r
        dwn_ref[...] += jnp.sum(don * n, axis=0, keepdims=True)
        dn = don * wn_ref[...]
        do = r * (dn - n * jnp.mean(dn * n, axis=-1, keepdims=True))
        delta = jnp.sum(do * o, axis=-1, keepdims=True)
        a2 = _lane_col(cum_ref[...], h) * LOG2E - _lane_col(lse_ref[0], 0)
        q_aug = _aug_block(tm, _split3(a2), 0, 3)
        q = q_ref[...]
        qb_ref[:, :HEAD_DIM] = q
        qb_ref[:, HEAD_DIM:] = q_aug.astype(bf16)
        doa_ref[:, :HEAD_DIM] = do.astype(bf16)
        doa_ref[:, HEAD_DIM:] = _aug_block(tm, _split3(-delta), 0).astype(bf16)
        qbt_ref[:HEAD_DIM, :] = q.astype(f32).T.astype(bf16)
        qbt_ref[HEAD_DIM:, :] = q_aug.T.astype(bf16)
        dot_ref[...] = do.T.astype(bf16)

    hb = pl.BlockSpec((tm, HEAD_DIM), lambda i, h: (i, h))
    ab = pl.BlockSpec((tm, AUG), lambda i, h: (i, h))
    return pl.pallas_call(
        body, name="fox_post_bwd", grid=(s // tm, nh),
        in_specs=[hb, hb, pl.BlockSpec((1, tm, LANES), lambda i, h: (h, i, 0)), pl.BlockSpec((tm, LANES), lambda i, h: (i, 0)),
                  hb, pl.BlockSpec((1, HEAD_DIM), lambda i, h: (0, 0))],
        out_specs=[ab, ab, pl.BlockSpec((AUG, tm), lambda i, h: (h, i)), pl.BlockSpec((HEAD_DIM, tm), lambda i, h: (h, i)),
                   pl.BlockSpec((1, HEAD_DIM), lambda i, h: (0, 0))],
        out_shape=[jax.ShapeDtypeStruct((s, nh * AUG), bf16), jax.ShapeDtypeStruct((s, nh * AUG), bf16),
                   jax.ShapeDtypeStruct((nh * AUG, s), bf16), jax.ShapeDtypeStruct((nh * HEAD_DIM, s), bf16),
                   jax.ShapeDtypeStruct((1, HEAD_DIM), f32)],
        compiler_params=_cp("arbitrary", "arbitrary"))(don, o, lse2, cum, qkv, wn)


def _fox_bwd(qb, doa, qb_t, do_t, ka, va, nh, tq):
    s = qb.shape[0]
    nq = s // tq
    fw = nh * HEAD_DIM

    def body(qb_ref, doa_ref, qbt_ref, dot_ref, ka_ref, va_ref, dqx_ref, dkx_ref, dv_ref, dk_acc, dv_acc):
        j = pl.program_id(1)

        @pl.when(j == 0)
        def _():
            dqx_ref[...] = jnp.zeros_like(dqx_ref)

        kj = ka_ref[...]
        vj = va_ref[...]

        def tile(i, rows=tq, first=False, keep=None):
            off = pl.multiple_of(i * tq, tq)
            p = jnp.exp2(_dotb(qb_ref[pl.ds(off, rows), :], kj, NT))
            if keep is not None:
                p = jnp.where(keep, p, 0.0)
            ds = (p * _dotb(doa_ref[pl.ds(off, rows), :], vj, NT)).astype(bf16)
            dv = _dotb(dot_ref[:, pl.ds(off, rows)], p)
            dk = _dotb(qbt_ref[:, pl.ds(off, rows)], ds)
            if first:
                dv_acc[...] = dv
                dk_acc[...] = dk
            else:
                dv_acc[...] += dv
                dk_acc[...] += dk
            dqx_ref[pl.ds(off, rows), :] += _dotb(ds, kj)

        n = nq - 1 - j
        b0 = jnp.minimum(j, nq - 2)
        qpos = b0 * tq + lax.broadcasted_iota(jnp.int32, (2 * tq, tq), 0)
        kpos = j * tq + lax.broadcasted_iota(jnp.int32, (2 * tq, tq), 1)
        tile(b0, 2 * tq, first=True, keep=(kpos <= qpos) & ((qpos < (j + 1) * tq) | (lax.rem(n, 2) == 1)))

        first_pair = j + 1 + lax.rem(n, 2)
        pairs = n // 2

        def pair(ii, carry):
            tile(first_pair, 2 * tq)
            return carry

        def quad(ii, carry):
            tile(first_pair + 2 * lax.rem(pairs, 2) + 4 * ii, 4 * tq)
            return carry

        lax.fori_loop(0, lax.rem(pairs, 2), pair, 0)
        lax.fori_loop(0, pairs // 2, quad, 0)
        dkx_ref[...] = dk_acc[...].T
        dv_ref[...] = dv_acc[...].T.astype(dv_ref.dtype)

    panel = pl.BlockSpec((s, AUG), lambda h, j: (0, h))
    blk = pl.BlockSpec((tq, AUG), lambda h, j: (j, h))
    return pl.pallas_call(
        body, name="fox_bwd", grid=(nh, nq),
        in_specs=[panel, panel, pl.BlockSpec((AUG, s), lambda h, j: (h, 0)), pl.BlockSpec((HEAD_DIM, s), lambda h, j: (h, 0)),
                  blk, blk],
        out_specs=[panel, blk, pl.BlockSpec((tq, HEAD_DIM), lambda h, j: (j, h))],
        out_shape=[jax.ShapeDtypeStruct((s, nh * AUG), f32), jax.ShapeDtypeStruct((s, nh * AUG), f32),
                   jax.ShapeDtypeStruct((s, fw), bf16)],
        scratch_shapes=[pltpu.VMEM((AUG, tq), f32), pltpu.VMEM((HEAD_DIM, tq), f32)],
        compiler_params=_cp("parallel", "arbitrary"))(qb, doa, qb_t, do_t, ka, va)


def _fox_unpack(dqx, dkx, nh):
    s = dqx.shape[0]
    fw = nh * HEAD_DIM
    tm = _head_row_tile(s)

    def body(dqx_ref, dkx_ref, dq_ref, dk_ref, dcum_ref):
        h = pl.program_id(1)

        @pl.when(h == 0)
        def _():
            dcum_ref[...] = jnp.zeros_like(dcum_ref)

        dq_ref[...] = (dqx_ref[:, :HEAD_DIM] * (HEAD_DIM ** -0.5)).astype(dq_ref.dtype)
        dk_ref[...] = (dkx_ref[:, :HEAD_DIM] * LN2).astype(dk_ref.dtype)
        d = _lane_col(dqx_ref[:, HEAD_DIM:], 0) - _lane_col(dkx_ref[:, HEAD_DIM:], 3)
        lane = lax.broadcasted_iota(jnp.int32, (tm, LANES), 1)
        dcum_ref[...] += jnp.where(lane == h, d, 0.0)

    ab = pl.BlockSpec((tm, AUG), lambda i, h: (i, h))
    hb = pl.BlockSpec((tm, HEAD_DIM), lambda i, h: (i, h))
    return pl.pallas_call(
        body, name="fox_unpack", grid=(s // tm, nh), in_specs=[ab, ab],
        out_specs=[hb, hb, pl.BlockSpec((tm, LANES), lambda i, h: (i, 0))],
        out_shape=[jax.ShapeDtypeStruct((s, fw), bf16), jax.ShapeDtypeStruct((s, fw), bf16), jax.ShapeDtypeStruct((s, LANES), f32)],
        compiler_params=_cp("parallel", "arbitrary"))(dqx, dkx)


def _conv_pre(xx, w, tm):
    pre = None
    for k in range(CONV_W):
        sh = CONV_W - 1 - k
        t = (pltpu.roll(xx, sh, 0) if sh else xx)[8:, :] * w[k:k + 1, :]
        pre = t if pre is None else pre + t
    return pre


def _gdn_pre(x, w, nh):
    s, cw = x.shape
    tm = _row_tile(s)
    fw = nh * HEAD_DIM

    def body(x_ref, prev_ref, w_ref, y_ref):
        i = pl.program_id(0)
        j = pl.program_id(1)
        for h in range(nh):
            sl = slice(h * HEAD_DIM, (h + 1) * HEAD_DIM)
            prev = jnp.where(i == 0, 0.0, prev_ref[:, sl])
            pre = _conv_pre(jnp.concatenate([prev, x_ref[:, sl]], axis=0), w_ref[:, sl], tm)
            y = pre * _sigmoid(pre)
            yn = y * lax.rsqrt(jnp.sum(y * y, axis=-1, keepdims=True) + EPS)
            y_ref[:, sl] = jnp.where(j < 2, yn, y)

    return pl.pallas_call(
        body, name="gdn_pre", grid=(s // tm, cw // fw),
        in_specs=[pl.BlockSpec((tm, fw), lambda i, j: (i, j)),
                  pl.BlockSpec((8, fw), lambda i, j: (jnp.maximum(i * (tm // 8) - 1, 0), j)),
                  pl.BlockSpec((CONV_W, fw), lambda i, j: (0, j))],
        out_specs=pl.BlockSpec((tm, fw), lambda i, j: (i, j)),
        out_shape=jax.ShapeDtypeStruct((s, cw), f32),
        compiler_params=_cp("parallel", "parallel"))(x, x, w)


def _gdn_pre_bwd(x, w, dyn, nh):
    s, cw = x.shape
    tm = _row_tile(s)
    fw = nh * HEAD_DIM

    def body(x_ref, prev_ref, w_ref, dyn_ref, dpre_ref):
        i = pl.program_id(0)
        j = pl.program_id(1)
        for h in range(nh):
            sl = slice(h * HEAD_DIM, (h + 1) * HEAD_DIM)
            prev = jnp.where(i == 0, 0.0, prev_ref[:, sl])
            pre = _conv_pre(jnp.concatenate([prev, x_ref[:, sl]], axis=0), w_ref[:, sl], tm)
            sg = _sigmoid(pre)
            y = pre * sg
            dyn = dyn_ref[:, sl]
            r = lax.rsqrt(jnp.sum(y * y, axis=-1, keepdims=True) + EPS)
            yn = y * r
            dy_n = r * (dyn - yn * jnp.sum(dyn * yn, axis=-1, keepdims=True))
            dy = jnp.where(j < 2, dy_n, dyn)
            dpre_ref[:, sl] = dy * sg * (1.0 + pre * (1.0 - sg))

    hb = pl.BlockSpec((tm, fw), lambda i, j: (i, j))
    return pl.pallas_call(
        body, name="gdn_pre_bwd", grid=(s // tm, cw // fw),
        in_specs=[hb, pl.BlockSpec((8, fw), lambda i, j: (jnp.maximum(i * (tm // 8) - 1, 0), j)),
                  pl.BlockSpec((CONV_W, fw), lambda i, j: (0, j)), hb],
        out_specs=hb, out_shape=jax.ShapeDtypeStruct((s, cw), f32),
        compiler_params=_cp("parallel", "parallel"))(x, x, w, dyn)


def _conv_bwd(x, w, dpre, nh):
    s, cw = x.shape
    tm = _row_tile(s)
    fw = nh * HEAD_DIM
    ni = s // tm

    def body(x_ref, prev_ref, w_ref, dp_ref, nxt_ref, dx_ref, dw_ref):
        i = pl.program_id(1)

        @pl.when(i == 0)
        def _():
            dw_ref[...] = jnp.zeros_like(dw_ref)

        for h in range(nh):
            sl = slice(h * HEAD_DIM, (h + 1) * HEAD_DIM)
            wv = w_ref[:, sl]
            dp = dp_ref[:, sl]
            nxt = jnp.where(i == ni - 1, 0.0, nxt_ref[:, sl])
            dd = jnp.concatenate([dp, nxt], axis=0)
            prev = jnp.where(i == 0, 0.0, prev_ref[:, sl])
            xx = jnp.concatenate([prev, x_ref[:, sl]], axis=0)
            dx = None
            rows = []
            for k in range(CONV_W):
                sh = CONV_W - 1 - k
                t = (pltpu.roll(dd, tm + 8 - sh, 0) if sh else dd)[:tm, :] * wv[k:k + 1, :]
                dx = t if dx is None else dx + t
                xs = (pltpu.roll(xx, sh, 0) if sh else xx)[8:, :]
                rows.append(jnp.sum(dp * xs, axis=0, keepdims=True))
            dx_ref[:, sl] = dx.astype(dx_ref.dtype)
            dw_ref[:, sl] += jnp.concatenate(rows, axis=0)

    hb = pl.BlockSpec((tm, fw), lambda j, i: (i, j))
    wb = pl.BlockSpec((CONV_W, fw), lambda j, i: (0, j))
    return pl.pallas_call(
        body, name="conv_bwd", grid=(cw // fw, ni),
        in_specs=[hb, pl.BlockSpec((8, fw), lambda j, i: (jnp.maximum(i * (tm // 8) - 1, 0), j)), wb, hb,
                  pl.BlockSpec((8, fw), lambda j, i: (jnp.minimum((i + 1) * (tm // 8), s // 8 - 1), j))],
        out_specs=[hb, wb],
        out_shape=[jax.ShapeDtypeStruct((s, cw), bf16), jax.ShapeDtypeStruct((CONV_W, cw), f32)],
        compiler_params=_cp("parallel", "arbitrary"))(x, x, w, dpre, dpre)


def _chunk_consts():
    c = GDN_CHUNK
    r = lax.broadcasted_iota(jnp.int32, (c, c), 0)
    q = lax.broadcasted_iota(jnp.int32, (c, c), 1)
    return r >= q, r > q, (r == q).astype(f32)


def _chunk_head(qkvn, sm, gcs, gcs_t, h, nh):
    fw = nh * HEAD_DIM
    q = qkvn[:, h * HEAD_DIM:(h + 1) * HEAD_DIM] * (HEAD_DIM ** -0.5)
    k = qkvn[:, fw + h * HEAD_DIM: fw + (h + 1) * HEAD_DIM]
    v = qkvn[:, 2 * fw + h * HEAD_DIM: 2 * fw + (h + 1) * HEAD_DIM]
    beta = _lane_col(sm, 2 * nh + h)
    gc = _lane_col(gcs, nh + h)
    gc_row = gcs_t[nh + h: nh + h + 1, :]
    incl, strict, _ = _chunk_consts()
    decay = jnp.where(incl, jnp.exp(jnp.minimum(gc - gc_row, 0.0)), 0.0)
    eg = jnp.exp(gc)
    g_last = gc[GDN_CHUNK - 1:GDN_CHUNK, :]
    egl = jnp.exp(g_last)
    ekd = jnp.exp(g_last - gc)
    kb = k * beta
    vb = v * beta
    kk = _dotb(kb, k, NT)
    qk = _dotb(q, k, NT)
    return dict(q=q, k=k, v=v, beta=beta, gc=gc, decay=decay, eg=eg, egl=egl, ekd=ekd, kb=kb, vb=vb, kk=kk, qk=qk,
                incl=incl, strict=strict)


def _unit_lower_inverses(lows, eye):
    c = GDN_CHUNK
    ts = [eye - low for low in lows]
    ps = [_dotm(low, low) for low in lows]
    for _ in range(4):
        both = [_dotm(jnp.concatenate([p, t], axis=0), p) for p, t in zip(ps, ts)]
        ts = [t + b[c:] for t, b in zip(ts, both)]
        ps = [b[:c] for b in both]
    return [t + _dotm(t, p) for t, p in zip(ts, ps)]


def _gdn_fwd(qkvn, sm, z, wn, nh):
    s = qkvn.shape[0]
    c = GDN_CHUNK
    nc = s // c
    fw = nh * HEAD_DIM

    def body(qkvn_ref, sm_ref, z_ref, wn_ref, on_ref, o_ref, st_ref, ti_ref, state):
        ci = pl.program_id(0)

        @pl.when(ci == 0)
        def _():
            state[...] = jnp.zeros_like(state)

        qkvn_v = qkvn_ref[...]
        sm_v = sm_ref[...]
        incl, strict, eye = _chunk_consts()
        gcs = _doth(incl.astype(f32), sm_v)
        gcs_t = gcs.T
        heads = range(nh)
        es = [_chunk_head(qkvn_v, sm_v, gcs, gcs_t, h, nh) for h in heads]
        tinvs = _unit_lower_inverses([jnp.where(strict, e["kk"] * e["decay"], 0.0) for e in es], eye)
        uws = [_dotm(t, jnp.concatenate([e["vb"], e["kb"] * e["eg"]], axis=1)) for t, e in zip(tinvs, es)]
        sts = [state[h] for h in heads]
        wq_s = [_dotb(jnp.concatenate([uw[:, HEAD_DIM:], e["q"] * e["eg"]], axis=0), st) for uw, e, st in zip(uws, es, sts)]
        v_news = [uw[:, :HEAD_DIM] - ws[:c] for uw, ws in zip(uws, wq_s)]
        os_ = [ws[c:] + _dotb(jnp.where(incl, e["qk"] * e["decay"], 0.0), vn) for ws, e, vn in zip(wq_s, es, v_news)]
        upd = [_dotb(e["k"] * e["ekd"], vn, TN) for e, vn in zip(es, v_news)]
        for h in heads:
            st_ref[0, h] = sts[h]
            ti_ref[0, h] = tinvs[h]
            state[h] = sts[h] * es[h]["egl"] + upd[h]
            sl = slice(h * HEAD_DIM, (h + 1) * HEAD_DIM)
            o = os_[h]
            o_ref[:, sl] = o
            zz = z_ref[:, sl]
            r = lax.rsqrt(jnp.mean(o * o, axis=-1, keepdims=True) + EPS)
            on_ref[:, sl] = (o * r * wn_ref[...] * (zz * _sigmoid(zz))).astype(on_ref.dtype)

    return pl.pallas_call(
        body, name="gdn_fwd", grid=(nc,),
        in_specs=[pl.BlockSpec((c, 3 * fw), lambda i: (i, 0)), pl.BlockSpec((c, LANES), lambda i: (i, 0)),
                  pl.BlockSpec((c, fw), lambda i: (i, 0)), pl.BlockSpec((1, HEAD_DIM), lambda i: (0, 0))],
        out_specs=[pl.BlockSpec((c, fw), lambda i: (i, 0)), pl.BlockSpec((c, fw), lambda i: (i, 0)),
                   pl.BlockSpec((1, nh, HEAD_DIM, HEAD_DIM), lambda i: (i, 0, 0, 0)),
                   pl.BlockSpec((1, nh, c, c), lambda i: (i, 0, 0, 0))],
        out_shape=[jax.ShapeDtypeStruct((s, fw), bf16), jax.ShapeDtypeStruct((s, fw), f32),
                   jax.ShapeDtypeStruct((nc, nh, HEAD_DIM, HEAD_DIM), f32), jax.ShapeDtypeStruct((nc, nh, c, c), f32)],
        scratch_shapes=[pltpu.VMEM((nh, HEAD_DIM, HEAD_DIM), f32)],
        compiler_params=_cp("arbitrary"))(qkvn, sm, z, wn)


def _gdn_post_bwd(don, o, z, wn, nh):
    s, fw = o.shape
    tm = _head_row_tile(s)

    def body(don_ref, o_ref, z_ref, wn_ref, do_ref, dz_ref, dwn_ref):
        i = pl.program_id(0)
        h = pl.program_id(1)

        @pl.when((i == 0) & (h == 0))
        def _():
            dwn_ref[...] = jnp.zeros_like(dwn_ref)

        o = o_ref[...]
        zz = z_ref[...]
        don = don_ref[...].astype(f32)
        wv = wn_ref[...]
        r = lax.rsqrt(jnp.mean(o * o, axis=-1, keepdims=True) + EPS)
        n = o * r
        sg = _sigmoid(zz)
        silu = zz * sg
        dz_ref[...] = (don * n * wv * sg * (1.0 + zz * (1.0 - sg))).astype(dz_ref.dtype)
        dnw = don * silu
        dwn_ref[...] += jnp.sum(dnw * n, axis=0, keepdims=True)
        dn = dnw * wv
        do_ref[...] = r * (dn - n * jnp.mean(dn * n, axis=-1, keepdims=True))

    hb = pl.BlockSpec((tm, HEAD_DIM), lambda i, h: (i, h))
    wb = pl.BlockSpec((1, HEAD_DIM), lambda i, h: (0, 0))
    return pl.pallas_call(
        body, name="gdn_post_bwd", grid=(s // tm, nh), in_specs=[hb, hb, hb, wb], out_specs=[hb, hb, wb],
        out_shape=[jax.ShapeDtypeStruct((s, fw), f32), jax.ShapeDtypeStruct((s, fw), bf16),
                   jax.ShapeDtypeStruct((1, HEAD_DIM), f32)],
        compiler_params=_cp("arbitrary", "arbitrary"))(don, o, z, wn)


def _gdn_bwd(qkvn, sm, do, states, tinvs, nh):
    s = qkvn.shape[0]
    c = GDN_CHUNK
    nc = s // c
    fw = nh * HEAD_DIM

    def body(qkvn_ref, sm_ref, do_ref, st_ref, ti_ref, dqkvn_ref, dsm_ref, dstate):
        ci = pl.program_id(0)

        @pl.when(ci == 0)
        def _():
            dstate[...] = jnp.zeros_like(dstate)

        qkvn_v = qkvn_ref[...]
        sm_v = sm_ref[...]
        incl, strict, eye = _chunk_consts()
        inclf = incl.astype(f32)
        gcs = _doth(inclf, sm_v)
        gcs_t = gcs.T
        lane = lax.broadcasted_iota(jnp.int32, (c, LANES), 1)
        last_row = lax.broadcasted_iota(jnp.int32, (c, 1), 0) == c - 1
        ones_cl = jnp.ones((c, LANES), f32)
        each = lambda f: [f(h) for h in range(nh)]
        es = each(lambda h: _chunk_head(qkvn_v, sm_v, gcs, gcs_t, h, nh))
        tinv = each(lambda h: ti_ref[0, h])
        st = each(lambda h: st_ref[0, h])
        dst = each(lambda h: dstate[h])
        do = each(lambda h: do_ref[:, h * HEAD_DIM:(h + 1) * HEAD_DIM])
        kg = each(lambda h: es[h]["kb"] * es[h]["eg"])
        qg = each(lambda h: es[h]["q"] * es[h]["eg"])
        kd = each(lambda h: es[h]["k"] * es[h]["ekd"])
        u = each(lambda h: _dotm(tinv[h], es[h]["vb"]))
        w = each(lambda h: _dotm(tinv[h], kg[h]))
        a = each(lambda h: jnp.where(incl, es[h]["qk"] * es[h]["decay"], 0.0))
        v_new = each(lambda h: u[h] - _dotb(w[h], st[h]))
        dv_new = each(lambda h: _dotb(a[h], do[h], TN) + _dotb(kd[h], dst[h]))
        da = each(lambda h: jnp.where(incl, _dotb(do[h], v_new[h], NT), 0.0))
        dqg = each(lambda h: _dotb(do[h], st[h], NT))
        dkd = each(lambda h: _dotb(v_new[h], dst[h], NT))
        dglast = each(lambda h: es[h]["egl"] * jnp.sum(jnp.sum(dst[h] * st[h], axis=1, keepdims=True), axis=0, keepdims=True))
        dw = each(lambda h: -_dotb(dv_new[h], st[h], NT))
        new_dst = each(lambda h: _dotb(qg[h], do[h], TN) + es[h]["egl"] * dst[h] - _dotb(w[h], dv_new[h], TN))
        dtinv = each(lambda h: _dotm(dv_new[h], es[h]["vb"], NT) + _dotm(dw[h], kg[h], NT))
        dvb = each(lambda h: _dotm(tinv[h], dv_new[h], TN))
        dkg = each(lambda h: _dotm(tinv[h], dw[h], TN))
        tdt = each(lambda h: _dotm(tinv[h], dtinv[h], TN))
        dlow = each(lambda h: -_dotm(tdt[h], tinv[h], NT))
        dkk = each(lambda h: jnp.where(strict, dlow[h] * es[h]["decay"], 0.0))
        dqk = each(lambda h: da[h] * es[h]["decay"])
        darg = each(lambda h: (jnp.where(strict, dlow[h] * es[h]["kk"], 0.0) + da[h] * es[h]["qk"]) * es[h]["decay"])
        dgc = each(lambda h: jnp.sum(darg[h], axis=1, keepdims=True) - _doth(darg[h], ones_cl, TN)[:, 0:1])
        dkb = each(lambda h: _dotb(dkk[h], es[h]["k"]) + dkg[h] * es[h]["eg"])
        dk = each(lambda h: _dotb(dkk[h], es[h]["kb"], TN) + _dotb(dqk[h], es[h]["q"], TN) + dkd[h] * es[h]["ekd"]
                  + dkb[h] * es[h]["beta"])
        dq = each(lambda h: (_dotb(dqk[h], es[h]["k"]) + dqg[h] * es[h]["eg"]) * (HEAD_DIM ** -0.5))
        s_kd = each(lambda h: jnp.sum(dkd[h] * kd[h], axis=1, keepdims=True))
        dgc = each(lambda h: dgc[h] + jnp.sum(dkg[h] * kg[h] + dqg[h] * qg[h], axis=1, keepdims=True) - s_kd[h]
                   + jnp.where(last_row, jnp.sum(s_kd[h], axis=0, keepdims=True) + dglast[h], 0.0))
        dg = each(lambda h: _doth(inclf, dgc[h] * ones_cl, TN)[:, 0:1])
        dsm = jnp.zeros((c, LANES), f32)
        for h in range(nh):
            dstate[h] = new_dst[h]
            dbeta = jnp.sum(dkb[h] * es[h]["k"] + dvb[h] * es[h]["v"], axis=1, keepdims=True)
            dqkvn_ref[:, h * HEAD_DIM:(h + 1) * HEAD_DIM] = dq[h]
            dqkvn_ref[:, fw + h * HEAD_DIM: fw + (h + 1) * HEAD_DIM] = dk[h]
            dqkvn_ref[:, 2 * fw + h * HEAD_DIM: 2 * fw + (h + 1) * HEAD_DIM] = dvb[h] * es[h]["beta"]
            dsm = dsm + jnp.where(lane == nh + h, dg[h], 0.0) + jnp.where(lane == 2 * nh + h, dbeta, 0.0)
        dsm_ref[...] = dsm

    rev = lambda i: (nc - 1 - i, 0)
    rev4 = lambda i: (nc - 1 - i, 0, 0, 0)
    return pl.pallas_call(
        body, name="gdn_bwd", grid=(nc,),
        in_specs=[pl.BlockSpec((c, 3 * fw), rev), pl.BlockSpec((c, LANES), rev), pl.BlockSpec((c, fw), rev),
                  pl.BlockSpec((1, nh, HEAD_DIM, HEAD_DIM), rev4), pl.BlockSpec((1, nh, c, c), rev4)],
        out_specs=[pl.BlockSpec((c, 3 * fw), rev), pl.BlockSpec((c, LANES), rev)],
        out_shape=[jax.ShapeDtypeStruct((s, 3 * fw), f32), jax.ShapeDtypeStruct((s, LANES), f32)],
        scratch_shapes=[pltpu.VMEM((nh, HEAD_DIM, HEAD_DIM), f32)],
        compiler_params=_cp("arbitrary"))(qkvn, sm, do, states, tinvs)


MM_TILES = (1024, 512, 2048)
MM_TILES_TN = (512, 1024, 4096)
MM_TILES_F_DEEP = (1024, 512, 2816)
MM_TILES_LONG_K = (1024, 512, 2560)


def _hosted(res, comm):
    return res if comm else (res, None)


def _ffn_fwd(tag, x, g, mod3, w, comm_up=None, comm_down=None, wd_of=None):
    wg_t, wu_t, wd = w
    sh, sc, gt = mod3
    h = _ada_in(tag + "_ada", x, g, sh, sc)
    (gate, up, act), got_up = _hosted(_mm(tag + "_up", [[(h, wg_t)], [(h, wu_t)]], "nt", MM_TILES, _ep_swiglu,
                                          (bf16, bf16, bf16), comm=comm_up), comm_up)
    if wd_of:
        wd = wd_of(got_up)
    (xn, y), got_down = _hosted(_mm(tag + "_down", [[(act, wd)]], "nn", MM_TILES_F_DEEP, _ep_residual, (f32, bf16),
                                    extras=((x, "mn"), (MACARON_W * gt, "n")), comm=comm_down), comm_down)
    return xn, dict(x=x, h=h, gate=gate, up=up, y=y), got_up, got_down


def _ffn_bwd(tag, dxn, dy, dgs, res, g, mod3, w, below=None, comm_dact=None, comm_dh_of=None):
    wg_t, wu_t, wd = w
    sh, sc, gt = mod3
    (act, dgate, dup), got = _hosted(_mm(tag + "_dact", [[(dy, wd)]], "nt", MM_TILES, _ep_swiglu_bwd, (bf16, bf16, bf16),
                                         extras=((res["gate"], "mn"), (res["up"], "mn")), comm=comm_dact), comm_dact)
    (dwd,) = _mm(tag + "_dwd", [[(act, dy)]], "tn", MM_TILES_TN, _ep_plain, (bf16,))
    (dwg_t,) = _mm(tag + "_dwg", [[(dgate, res["h"])]], "tn", MM_TILES_TN, _ep_plain, (bf16,))
    (dwu_t,) = _mm(tag + "_dwu", [[(dup, res["h"])]], "tn", MM_TILES_TN, _ep_plain, (bf16,))
    comm_dh = comm_dh_of and comm_dh_of((dwg_t, dwu_t, dwd))
    (dh,), got_dh = _hosted(_mm(tag + "_dh", [[(dgate, wg_t), (dup, wu_t)]], "nn", MM_TILES_F_DEEP, _ep_plain, (bf16,), comm=comm_dh),
                            comm_dh)
    if below:
        dx, dsh, a, dy_below, dgs_below = _ada_bwd(tag + "_ada_bwd", res["x"], g, sc, dh, dxn, *below)
    else:
        (dx, dsh, a), dy_below, dgs_below = _ada_bwd_first(tag + "_ada_bwd", res["x"], g, sc, dh, dxn), None, None
    return dx, (dy_below, dgs_below), (dwg_t, dwu_t, dwd), (dsh, a * g, MACARON_W * dgs), a * (1.0 + sc), got, got_dh


def _local_step(x, target, mods, norm_g, final_norm, ffn1_w, later_w, prm, fox_wn, gdn_wn, conv_w, nh, hooks=None):
    s, d = x.shape
    fw = nh * HEAD_DIM
    tq = _tile(s, min(256, s // 2))
    g_rows = [norm_g[i:i + 1] for i in range(3)]
    m1, m2, m3 = mods[0:3], mods[3:6], mods[6:9]

    x1, r1, got_up, got_down = _ffn_fwd("ffn1", x, g_rows[0], m1, ffn1_w, hooks and hooks.gather_mix_spec(),
                                        hooks and hooks.gather_ffn2_spec(), hooks and hooks.ffn1_wd)
    if hooks:
        ffn1_w = ffn1_w[:2] + (hooks.ffn1_wd(got_up),)
    w_cat_t, w_out, ffn2_w = hooks.gathered(got_up, got_down) if hooks else later_w
    h2 = _ada_in("mix_ada", x1, g_rows[1], m2[0], m2[1])
    w_fox, w_gdn, w_z, w_s = w_cat_t[:3 * fw], w_cat_t[3 * fw:6 * fw], w_cat_t[6 * fw:7 * fw], w_cat_t[7 * fw:]
    colscale = jnp.concatenate([jnp.full((1, fw), FOX_Q_SCALE, f32), jnp.ones((1, 2 * fw), f32)], axis=1)
    (qkv_f,) = _mm("proj_fox", [[(h2, w_fox)]], "nt", MM_TILES, _ep_colscale, (bf16,), extras=((colscale, "n"),))
    (qkv_g,) = _mm("proj_gdn", [[(h2, w_gdn)]], "nt", MM_TILES, _ep_plain, (f32,))
    (z,) = _mm("proj_z", [[(h2, w_z)]], "nt", MM_TILES, _ep_plain, (f32,))
    (ps,) = _mm("proj_s", [[(h2, w_s)]], "nt", MM_TILES, _ep_plain, (f32,))
    sm, cum = _small_fwd(ps, prm, nh)
    qa, ka, va = _fox_aug(qkv_f, cum, nh)
    o_f, on_f, lse2 = _fox_fwd(qa, ka, qkv_f, fox_wn, nh, tq)
    qkvn = _gdn_pre(qkv_g, conv_w, nh)
    on_g, o_g, states, tinvs = _gdn_fwd(qkvn, sm, z, gdn_wn, nh)
    w_top, w_bot = w_out[:fw], w_out[fw:]
    x2, mix = _mm("mix_out", [[(on_f, w_top), (on_g, w_bot)]], "nn", MM_TILES, _ep_residual, (f32, bf16),
                  extras=((x1, "mn"), (m2[2], "n")))
    x3, r3, got_up2, _ = _ffn_fwd("ffn2", x2, g_rows[2], m3, ffn2_w, hooks and hooks.gather_ffn2_wd_spec(), None,
                                  hooks and hooks.ffn2_wd)
    if hooks:
        ffn2_w = ffn2_w[:2] + (hooks.ffn2_wd(got_up2),)
    loss, dx3, dfinal, dy3, dgs3 = _final_loss(x3, final_norm, target, r3["y"], MACARON_W * m3[2])

    dx2, (dmix, dgt2), dffn2, dmod3, dg3, _, _ = _ffn_bwd("ffn2", dx3, dy3, dgs3, r3, g_rows[2], m3, ffn2_w, (mix, m2[2]))
    rs_ffn2 = hooks and hooks.rs_ffn2_spec(dffn2)
    (don_f,) = _mm("mix_dof", [[(dmix, w_top)]], "nt", MM_TILES, _ep_plain, (f32,))
    (don_g,) = _mm("mix_dog", [[(dmix, w_bot)]], "nt", MM_TILES, _ep_plain, (f32,))
    (dw_top,) = _mm("mix_dwtop", [[(on_f, dmix)]], "tn", MM_TILES_TN, _ep_plain, (bf16,))
    (dw_bot,) = _mm("mix_dwbot", [[(on_g, dmix)]], "tn", MM_TILES_TN, _ep_plain, (bf16,))
    qb, doa, qb_t, do_t, dfox_wn = _fox_post_bwd(don_f, o_f, lse2, cum, qkv_f, fox_wn, nh)
    dqx, dkx, dv_f = _fox_bwd(qb, doa, qb_t, do_t, ka, va, nh, tq)
    dq_f, dk_f, dcum = _fox_unpack(dqx, dkx, nh)
    do_g, dz, dgdn_wn = _gdn_post_bwd(don_g, o_g, z, gdn_wn, nh)
    dqkvn, dsm = _gdn_bwd(qkvn, sm, do_g, states, tinvs, nh)
    dpre = _gdn_pre_bwd(qkv_g, conv_w, dqkvn, nh)
    dqkv_g, dconv = _conv_bwd(qkv_g, conv_w, dpre, nh)
    dps, pg = _small_bwd(ps, prm, dsm, dcum, nh)
    dproj = jnp.concatenate([dq_f, dk_f, dv_f, dqkv_g, dz, dps], axis=1)
    ((dw_cat_t,), got_ffn2) = _hosted(_mm("proj_dw", [[(dproj, h2)]], "tn", MM_TILES_TN, _ep_plain, (bf16,), comm=rs_ffn2), rs_ffn2)
    dw_out = jnp.concatenate([dw_top, dw_bot], axis=0)
    rs_mix = hooks and hooks.rs_mix_spec(dw_cat_t, dw_out)
    (dh2,) = _mm("proj_dh", [[(dproj, w_cat_t)]], "nn", MM_TILES_LONG_K, _ep_plain, (bf16,))
    dx1, dsh2, a2, dy1, dgs1 = _ada_bwd("mix_ada_bwd", x1, g_rows[1], m2[1], dh2, dx2, r1["y"], MACARON_W * m1[2])
    dmod2 = (dsh2, a2 * g_rows[1], dgt2)
    dg2 = a2 * (1.0 + m2[1])
    dx0, _, dffn1, dmod1, dg1, got_mix, got_ffn1 = _ffn_bwd("ffn1", dx1, dy1, dgs1, r1, g_rows[0], m1, ffn1_w, None, rs_mix,
                                                             hooks and hooks.rs_ffn1_spec)

    big = dict(ffn=(dffn1, dffn2), w_cat_t=dw_cat_t, w_out=dw_out, got_ffn2=got_ffn2, got_mix=got_mix, got_ffn1=got_ffn1)
    small = dict(loss=loss, norm_g=jnp.concatenate([dg1, dg2, dg3], axis=0), final_norm=dfinal, fox_wn=dfox_wn,
                 gdn_wn=dgdn_wn, pg=pg, conv=dconv, mod=jnp.concatenate(list(dmod1) + list(dmod2) + list(dmod3), axis=1))
    return dx0, big, small


def _w_in_row_groups(nh):
    fw = nh * HEAD_DIM
    sizes = [3 * fw, nh, 3 * fw, nh, nh, fw]
    offs = [0]
    for sz in sizes:
        offs.append(offs[-1] + sz)
    return [(offs[i], offs[i + 1]) for i in range(len(sizes))]


def _build_w_cat_t(w_in_t, nh):
    gq, gf, gg, ga, gb, gz = _w_in_row_groups(nh)
    d = w_in_t.shape[1]
    rows = lambda r: w_in_t[r[0]:r[1]]
    pad = jnp.zeros((LANES - 3 * nh, d), w_in_t.dtype)
    return jnp.concatenate([rows(gq), rows(gg), rows(gz), rows(gf), rows(ga), rows(gb), pad], axis=0)


def _split_dw_cat_t(dw_cat_t, nh):
    fw = nh * HEAD_DIM
    o = 7 * fw
    return jnp.concatenate([dw_cat_t[:3 * fw], dw_cat_t[o:o + nh], dw_cat_t[3 * fw:6 * fw], dw_cat_t[o + nh:o + 2 * nh],
                            dw_cat_t[o + 2 * nh:o + 3 * nh], dw_cat_t[6 * fw:7 * fw]], axis=0)


def _head_params(fox_f_bias, gdn_dt_bias, gdn_a_log, nh):
    z = jnp.zeros((8, LANES), f32)
    z = z.at[0, 0:nh].set(fox_f_bias.reshape(nh))
    z = z.at[1, nh:2 * nh].set(gdn_dt_bias.reshape(nh))
    z = z.at[2, nh:2 * nh].set(gdn_a_log.reshape(nh))
    return z


ANY = pl.BlockSpec(memory_space=pl.ANY)
IN_VMEM = pl.BlockSpec(memory_space=pltpu.VMEM)
N_OTHER_CHIPS = 3


def _place():
    x, y, c = lax.axis_index("x"), lax.axis_index("y"), lax.axis_index("c")
    chips = [(1 - x, y), (x, 1 - y), (1 - x, 1 - y)]
    return x, y, c, chips


def _allgather8(name, v):
    r, n = v.shape

    def body(v_ref, out_ref, send_sems, recv_sems, local_sem):
        x, y, c, _ = _place()
        me = 4 * x + 2 * y + c
        mine = pltpu.make_async_copy(v_ref, out_ref.at[me], local_sem)
        mine.start()
        copies = []
        for k in range(1, 8):
            fx, fy, fc = (k >> 2) & 1, (k >> 1) & 1, k & 1
            peer = (x + fx - 2 * x * fx, y + fy - 2 * y * fy, c + fc - 2 * c * fc)
            cp = pltpu.make_async_remote_copy(src_ref=v_ref, dst_ref=out_ref.at[me], send_sem=send_sems.at[k - 1],
                                              recv_sem=recv_sems.at[k - 1], device_id=peer, device_id_type=MESH)
            cp.start()
            copies.append(cp)
        for cp in copies:
            cp.wait()
        mine.wait()

    return pl.pallas_call(
        body, name=name, in_specs=[IN_VMEM], out_specs=IN_VMEM, out_shape=jax.ShapeDtypeStruct((8, r, n), v.dtype),
        scratch_shapes=[pltpu.SemaphoreType.DMA((7,)), pltpu.SemaphoreType.DMA((7,)), pltpu.SemaphoreType.DMA],
        compiler_params=pltpu.CompilerParams(vmem_limit_bytes=VMEM_LIMIT_V7X))(v)


class _CommSpec:
    def __init__(self, ins, out_shapes, sem_counts, run):
        self.ins, self.out_shapes, self.sem_counts, self.run = list(ins), list(out_shapes), sem_counts, run

    def sems(self):
        return [pltpu.SemaphoreType.DMA((n,)) for n in self.sem_counts]


def _run_comm(name, spec):
    ni, no = len(spec.ins), len(spec.out_shapes)

    def body(*refs):
        ins, outs, sems = refs[:ni], refs[ni:ni + no], refs[ni + no:]
        spec.run("start", ins, outs, sems)
        spec.run("finish", ins, outs, sems)

    return pl.pallas_call(body, name=name, in_specs=[ANY] * ni, out_specs=[ANY] * no, out_shape=spec.out_shapes,
                          scratch_shapes=spec.sems())(*spec.ins)


def _gather_spec(halves):
    nw = len(halves)

    def run(phase, ins, outs, sems):
        ici_send, ici_recv, d2d_send, d2d_recv = sems
        x, y, c, chips = _place()
        s = 2 * x + y
        sib = (x, y, 1 - c)

        def over_ici(w, j, dst):
            cx, cy = chips[j]
            return pltpu.make_async_remote_copy(src_ref=ins[w].at[c], dst_ref=dst, send_sem=ici_send.at[w * 3 + j],
                                                recv_sem=ici_recv.at[w * 3 + j], device_id=(cx, cy, c), device_id_type=MESH)

        def to_sibling(w, j, blk):
            return pltpu.make_async_remote_copy(src_ref=blk, dst_ref=blk, send_sem=d2d_send.at[w * 3 + j],
                                                recv_sem=d2d_recv.at[w * 3 + j], device_id=sib, device_id_type=MESH)

        pairs = [(w, j) for w in range(nw) for j in range(N_OTHER_CHIPS)]
        chip_of = lambda j: 2 * chips[j][0] + chips[j][1]
        if phase == "start":
            for w, j in pairs:
                over_ici(w, j, outs[w].at[c, s]).start()
            return
        for w, j in pairs:
            landed = outs[w].at[c, chip_of(j)]
            over_ici(w, j, landed).wait_recv()
            to_sibling(w, j, landed).start()
        for w, j in pairs:
            to_sibling(w, j, outs[w].at[1 - c, chip_of(j)]).wait_recv()
        for w, j in pairs:
            over_ici(w, j, outs[w].at[c, s]).wait_send()
            to_sibling(w, j, outs[w].at[c, chip_of(j)]).wait_send()

    n3 = nw * N_OTHER_CHIPS
    return _CommSpec(halves, [jax.ShapeDtypeStruct((2, 4) + h.shape[1:], h.dtype) for h in halves], [n3] * 4, run)


def _to_chips_spec(partials):
    nw = len(partials)

    def run(phase, ins, outs, sems):
        send_sems, recv_sems = sems
        x, y, c, chips = _place()
        for w in range(nw):
            for j, (cx, cy) in enumerate(chips):
                cp = pltpu.make_async_remote_copy(src_ref=ins[w].at[2 * cx + cy], dst_ref=outs[w].at[j],
                                                  send_sem=send_sems.at[w * 3 + j], recv_sem=recv_sems.at[w * 3 + j],
                                                  device_id=(cx, cy, c), device_id_type=MESH)
                if phase == "start":
                    cp.start()
                else:
                    cp.wait()

    n3 = nw * N_OTHER_CHIPS
    return _CommSpec(partials, [jax.ShapeDtypeStruct((3,) + a.shape[1:], a.dtype) for a in partials], [n3, n3], run)


def _send_to_sibling(name, srcs, other_half):
    nw = len(srcs)

    def body(*refs):
        ins, outs = refs[:nw], refs[nw:2 * nw]
        send_sems, recv_sems = refs[2 * nw:]
        x, y, c, _ = _place()
        cps = []
        for w in range(nw):
            cp = pltpu.make_async_remote_copy(src_ref=ins[w].at[1 - c] if other_half else ins[w], dst_ref=outs[w],
                                              send_sem=send_sems.at[w], recv_sem=recv_sems.at[w],
                                              device_id=(x, y, 1 - c), device_id_type=MESH)
            cp.start()
            cps.append(cp)
        for cp in cps:
            cp.wait()

    return pl.pallas_call(
        body, name=name, in_specs=[ANY] * nw, out_specs=[ANY] * nw,
        out_shape=[jax.ShapeDtypeStruct(a.shape[1:] if other_half else a.shape, a.dtype) for a in srcs],
        scratch_shapes=[pltpu.SemaphoreType.DMA((nw,)), pltpu.SemaphoreType.DMA((nw,))],
    )(*srcs)


def _add_pair(name, g, recv, c):
    _, nchip, r, d = g.shape
    tr = _tile(r, 512, 16)

    def body(c_ref, g_ref, r_ref, o_ref):
        o_ref[...] = (g_ref[...].astype(f32) + r_ref[...].astype(f32)).astype(o_ref.dtype)

    gs = pltpu.PrefetchScalarGridSpec(
        num_scalar_prefetch=1, grid=(nchip, r // tr),
        in_specs=[pl.BlockSpec((None, None, tr, d), lambda t, i, cr: (cr[0], t, i, 0)),
                  pl.BlockSpec((None, tr, d), lambda t, i, cr: (t, i, 0))],
        out_specs=pl.BlockSpec((None, tr, d), lambda t, i, cr: (t, i, 0)))
    return pl.pallas_call(body, name=name, grid_spec=gs, out_shape=jax.ShapeDtypeStruct((nchip, r, d), bf16),
                          compiler_params=_cp("parallel", "parallel"))(c.reshape(1).astype(jnp.int32), g, recv)


def _add_chips(name, p, recv, s_chip):
    _, r, d = p.shape
    tr = _tile(r, 512, 16)

    def body(s_ref, p_ref, r_ref, o_ref):
        o_ref[...] = ((p_ref[...].astype(f32) + r_ref[0].astype(f32)) + r_ref[1].astype(f32)) + r_ref[2].astype(f32)

    gs = pltpu.PrefetchScalarGridSpec(
        num_scalar_prefetch=1, grid=(r // tr,),
        in_specs=[pl.BlockSpec((None, tr, d), lambda i, sr: (sr[0], i, 0)),
                  pl.BlockSpec((3, tr, d), lambda i, sr: (0, i, 0))],
        out_specs=pl.BlockSpec((tr, d), lambda i, sr: (i, 0)))
    return pl.pallas_call(body, name=name, grid_spec=gs, out_shape=jax.ShapeDtypeStruct((r, d), f32),
                          compiler_params=_cp("parallel"))(s_chip.reshape(1).astype(jnp.int32), p, recv)


def _rs_pair_sums(tag, grads, c):
    from_sib = _send_to_sibling("rs_to_sibling_" + tag, grads, True)
    return [_add_pair("rs_add_pair_%s%d" % (tag, n), g, r, c) for n, (g, r) in enumerate(zip(grads, from_sib))]


def _rs_chip_sums(tag, partial, from_chips, s_chip):
    return [_add_chips("rs_add_chips_%s%d" % (tag, n), p, r, s_chip) for n, (p, r) in enumerate(zip(partial, from_chips))]


def _rs_both_halves(mine, c):
    theirs = _send_to_sibling("rs_exchange_halves", mine, False)
    return [jnp.where(c == 0, jnp.stack([a, b]), jnp.stack([b, a])) for a, b in zip(mine, theirs)]


def _sum_devices(v):
    n = v.shape[2]

    def body(v_ref, o_ref):
        t = v_ref[0]
        for k in range(1, 8):
            t = t + v_ref[k]
        o_ref[...] = t

    return pl.pallas_call(body, name="sum_devices", out_shape=jax.ShapeDtypeStruct((1, n), f32))(v)


def _silu_rows(v):
    def body(v_ref, o_ref):
        t = v_ref[...]
        o_ref[...] = t * _sigmoid(t)

    return pl.pallas_call(body, name="silu_cond", out_shape=jax.ShapeDtypeStruct(v.shape, f32))(v)


ADAMW_BLOCK_ELEMS = 600 * 1024


def _adamw(name, w, g, m, v):
    r, cdim = w.shape
    tr = _tile(r, max(8, min(256, (ADAMW_BLOCK_ELEMS // cdim) // 8 * 8)), 8)
    c1 = 1.0 - ADAM_B1 ** ADAM_STEP
    c2 = 1.0 - ADAM_B2 ** ADAM_STEP

    def body(w_ref, g_ref, m_ref, v_ref, d_ref, mo_ref, vo_ref):
        gv = g_ref[...]
        mn = ADAM_B1 * m_ref[...] + (1.0 - ADAM_B1) * gv
        vn = ADAM_B2 * v_ref[...] + (1.0 - ADAM_B2) * (gv * gv)
        d_ref[...] = -ADAM_LR * ((mn / c1) / (jnp.sqrt(vn / c2) + ADAM_EPS) + ADAM_WD * w_ref[...])
        mo_ref[...] = mn
        vo_ref[...] = vn

    blk = pl.BlockSpec((tr, cdim), lambda i: (i, 0))
    return pl.pallas_call(body, name=name, grid=(r // tr,), in_specs=[blk] * 4, out_specs=[blk] * 3,
                          out_shape=[jax.ShapeDtypeStruct((r, cdim), f32)] * 3, compiler_params=_cp("parallel"))(w, g, m, v)


def _ep_bias(accs, ex):
    return (accs[0] + ex[0],)


def kernel(x, c, ada_w, ada_b, norm_g, ffn_w_gate, ffn_w_up, ffn_w_down, w_in, w_out, fox_f_bias, fox_out_norm, gdn_conv, gdn_A_log, gdn_dt_bias, gdn_out_norm, final_norm, loss_target, m_ada_w, m_ada_b, m_norm_g, m_ffn_w_gate, m_ffn_w_up, m_ffn_w_down, m_w_in, m_w_out, m_fox_f_bias, m_fox_out_norm, m_gdn_conv, m_gdn_A_log, m_gdn_dt_bias, m_gdn_out_norm, m_final_norm, v_ada_w, v_ada_b, v_norm_g, v_ffn_w_gate, v_ffn_w_up, v_ffn_w_down, v_w_in, v_w_out, v_fox_f_bias, v_fox_out_norm, v_gdn_conv, v_gdn_A_log, v_gdn_dt_bias, v_gdn_out_norm, v_final_norm):
    ix, iy, ic = lax.axis_index("x"), lax.axis_index("y"), lax.axis_index("c")
    s_chip = 2 * ix + iy
    me = 4 * ix + 2 * iy + ic
    _, s, d = x.shape
    nh = d // (2 * HEAD_DIM)
    fw = nh * HEAD_DIM
    ncol = ada_w.shape[2]
    dg_sh = norm_g.shape[2]
    cv_sh = gdn_conv.shape[2]
    ff_sh = ffn_w_gate.shape[3]
    in_sh = w_in.shape[2]
    in_pad = -(-in_sh // 32) * 32
    out_sh = w_out.shape[1]
    per_chip = lambda a, t: a[2 * t]

    pack0 = jnp.concatenate([_silu_rows(c), norm_g[0].reshape(1, 3 * dg_sh), gdn_conv[0].reshape(1, CONV_W * cv_sh)], axis=1)
    got0 = _allgather8("gather_cond", pack0)
    cond_all = got0[:, 0, :d]
    norm_g_full = jnp.concatenate([per_chip(got0, t)[0, d:d + 3 * dg_sh].reshape(3, dg_sh) for t in range(4)], axis=1)
    conv_full = jnp.concatenate([per_chip(got0, t)[0, d + 3 * dg_sh:].reshape(CONV_W, cv_sh) for t in range(4)], axis=1)

    ada_b_sh = lax.dynamic_slice_in_dim(ada_b, s_chip * ncol, ncol, axis=1)
    (mod_sh,) = _mm("ada_mod", [[(cond_all, ada_w[0])]], "nn", (8, 512, 2048), _ep_bias, (f32,), extras=((ada_b_sh, "n"),))
    mod_all = _allgather8("gather_mod", mod_sh)
    mod = jnp.concatenate([lax.dynamic_index_in_dim(per_chip(mod_all, t), me, axis=0, keepdims=True) for t in range(4)], axis=1)
    mods = [mod[:, i * d:(i + 1) * d] for i in range(9)]

    halved = lambda a: a.reshape(2, a.shape[0] // 2, d)
    ffn_halves = [[halved(ffn_w_gate[0, j].T.astype(bf16)), halved(ffn_w_up[0, j].T.astype(bf16)),
                   halved(ffn_w_down[0, j].astype(bf16))] for j in range(2)]
    mix_halves = [halved(jnp.pad(w_in[0].T.astype(bf16), ((0, in_pad - in_sh), (0, 0)))), halved(w_out[0].astype(bf16))]
    with_own = lambda got, hs: [lax.dynamic_update_slice(g, h[:, None], (0, s_chip, 0, 0)) for g, h in zip(got, hs)]
    ffn_full = lambda got, hs: tuple(g.reshape(4 * ff_sh, d) for g in with_own(got, hs))
    ffn_blocks = lambda grads: [g.reshape(2, 4, ff_sh // 2, d) for g in grads]
    ffn1_w = ffn_full(_run_comm("gather_ffn1", _gather_spec(ffn_halves[0][:2])), ffn_halves[0][:2]) + (None,)
    prm = _head_params(fox_f_bias, gdn_dt_bias, gdn_A_log, nh)

    class Hooks:
        def gather_mix_spec(self):
            return _gather_spec(mix_halves + ffn_halves[0][2:])

        def ffn1_wd(self, got):
            return ffn_full(got[2:], ffn_halves[0][2:])[0]

        def gather_ffn2_spec(self):
            return _gather_spec(ffn_halves[1][:2])

        def gather_ffn2_wd_spec(self):
            return _gather_spec(ffn_halves[1][2:])

        def ffn2_wd(self, got):
            return ffn_full(got, ffn_halves[1][2:])[0]

        def gathered(self, got_mix, got_ffn2):
            g_win, g_wo = with_own(got_mix[:2], mix_halves)
            w_in_t = jnp.swapaxes(g_win, 0, 1).reshape(4, in_pad, d)[:, :in_sh].reshape(4 * in_sh, d)
            return (_build_w_cat_t(w_in_t, nh), jnp.swapaxes(g_wo, 0, 1).reshape(4 * out_sh, d),
                    ffn_full(got_ffn2, ffn_halves[1][:2]) + (None,))

        def rs_ffn2_spec(self, dffn2):
            self.ffn2_pairs = _rs_pair_sums("ffn2", ffn_blocks(dffn2), ic)
            return _to_chips_spec(self.ffn2_pairs)

        def rs_ffn1_spec(self, dffn1):
            self.ffn1_pairs = _rs_pair_sums("ffn1", ffn_blocks(dffn1), ic)
            return _to_chips_spec(self.ffn1_pairs)

        def rs_mix_spec(self, dw_cat_t, dw_out):
            dw_in_t = jnp.pad(_split_dw_cat_t(dw_cat_t, nh).reshape(4, in_sh, d), ((0, 0), (0, in_pad - in_sh), (0, 0)))
            grads = [jnp.swapaxes(dw_in_t.reshape(4, 2, in_pad // 2, d), 0, 1),
                     jnp.swapaxes(dw_out.reshape(4, 2, out_sh // 2, d), 0, 1)]
            self.mix_pairs = _rs_pair_sums("mix", grads, ic)
            return _to_chips_spec(self.mix_pairs)

    hooks = Hooks()

    dx0, big, small = _local_step(x[0], loss_target[0], mods, norm_g_full, final_norm.reshape(1, d), ffn1_w, None, prm,
                                  fox_out_norm, gdn_out_norm, conv_full, nh, hooks)

    pack1 = jnp.concatenate([small["loss"], small["norm_g"].reshape(1, 3 * d), small["final_norm"], small["fox_wn"],
                             small["gdn_wn"], small["pg"][0:1], small["pg"][1:2], small["conv"].reshape(1, CONV_W * 3 * fw),
                             small["mod"]], axis=1)
    got1 = _allgather8("gather_small_grads", pack1)
    tot = _sum_devices(got1)
    o = [0]

    def take(n):
        o[0] += n
        return tot[:, o[0] - n:o[0]]

    loss = take(LANES)[0, 0]
    g_norm_g = lax.dynamic_slice_in_dim(take(3 * d).reshape(3, d), s_chip * dg_sh, dg_sh, axis=1)[None]
    g_final = take(d).reshape(d)
    g_fox_wn = take(HEAD_DIM)
    g_gdn_wn = take(HEAD_DIM)
    pg0, pg1 = take(LANES), take(LANES)
    g_fbias, g_dtb, g_alog = pg0[:, 0:nh], pg0[:, nh:2 * nh], pg1[:, nh:2 * nh]
    g_conv = lax.dynamic_slice_in_dim(take(CONV_W * 3 * fw).reshape(CONV_W, 3 * fw), s_chip * cv_sh, cv_sh, axis=1)[None]
    g_ada_b = take(9 * d)
    dmod_all = got1[:, 0, o[0] - 9 * d:o[0]]
    dmod_sh = lax.dynamic_slice_in_dim(dmod_all, s_chip * ncol, ncol, axis=1)
    (g_ada_w,) = _mm("ada_dw", [[(cond_all, dmod_sh)]], "tn", (2048, 512, 8), _ep_plain, (f32,))

    ffn1_mine = _rs_chip_sums("ffn1", hooks.ffn1_pairs, big["got_ffn1"], s_chip)
    ffn2_mine = _rs_chip_sums("ffn2", hooks.ffn2_pairs, big["got_ffn2"], s_chip)
    mix_mine = _rs_chip_sums("mix", hooks.mix_pairs, big["got_mix"], s_chip)
    r1g, r1u, r1d, r2g, r2u, r2d, r_win, r_wo = _rs_both_halves(ffn1_mine + ffn2_mine + mix_mine, ic)
    rows = lambda r: r.reshape(-1, d)
    g_ffn_gate = jnp.stack([rows(r1g).T, rows(r2g).T])[None]
    g_ffn_up = jnp.stack([rows(r1u).T, rows(r2u).T])[None]
    g_ffn_down = jnp.stack([rows(r1d), rows(r2d)])[None]
    g_w_in = rows(r_win)[:in_sh].T[None]
    g_w_out = rows(r_wo)[None]

    def upd(name, w, g, m, v):
        shp = w.shape
        two = lambda a: a.reshape(-1, shp[-1])
        return tuple(t.reshape(shp) for t in _adamw(name, two(w), two(g), two(m), two(v)))

    big_upd = [upd("adamw_ada_w", ada_w, g_ada_w[None], m_ada_w, v_ada_w),
               upd("adamw_ffn_gate", ffn_w_gate, g_ffn_gate, m_ffn_w_gate, v_ffn_w_gate),
               upd("adamw_ffn_up", ffn_w_up, g_ffn_up, m_ffn_w_up, v_ffn_w_up),
               upd("adamw_ffn_down", ffn_w_down, g_ffn_down, m_ffn_w_down, v_ffn_w_down),
               upd("adamw_w_in", w_in, g_w_in, m_w_in, v_w_in),
               upd("adamw_w_out", w_out, g_w_out, m_w_out, v_w_out)]
    small_w = [ada_b, norm_g, fox_f_bias, fox_out_norm, gdn_conv, gdn_A_log, gdn_dt_bias, gdn_out_norm, final_norm]
    small_g = [g_ada_b, g_norm_g, g_fbias, g_fox_wn, g_conv, g_alog, g_dtb, g_gdn_wn, g_final]
    small_m = [m_ada_b, m_norm_g, m_fox_f_bias, m_fox_out_norm, m_gdn_conv, m_gdn_A_log, m_gdn_dt_bias, m_gdn_out_norm, m_final_norm]
    small_v = [v_ada_b, v_norm_g, v_fox_f_bias, v_fox_out_norm, v_gdn_conv, v_gdn_A_log, v_gdn_dt_bias, v_gdn_out_norm, v_final_norm]
    sizes = [a.size for a in small_w]
    npad = -sum(sizes) % LANES
    flat = lambda arrs, fill: jnp.concatenate([a.reshape(1, -1) for a in arrs] + [jnp.full((1, npad), fill, f32)], axis=1)
    sd, sm_, sv = _adamw("adamw_small", flat(small_w, 0.0), flat(small_g, 0.0), flat(small_m, 0.0), flat(small_v, 1.0))

    def unflat(t):
        out, off = [], 0
        for a, n in zip(small_w, sizes):
            out.append(t[0, off:off + n].reshape(a.shape))
            off += n
        return out

    small_g = [g.reshape(a.shape) for g, a in zip(small_g, small_w)]
    s_d, s_m, s_v = unflat(sd), unflat(sm_), unflat(sv)
    def order(bigs, smalls):
        return [bigs[0], smalls[0], smalls[1], bigs[1], bigs[2], bigs[3], bigs[4], bigs[5]] + list(smalls[2:])

    grads_out = order([g_ada_w[None], g_ffn_gate, g_ffn_up, g_ffn_down, g_w_in, g_w_out], small_g)
    deltas = order([u[0] for u in big_upd], s_d)
    new_m = order([u[1] for u in big_upd], s_m)
    new_v = order([u[2] for u in big_upd], s_v)
    return (loss, dx0[None], *grads_out, *deltas, *new_m, *new_v)
```

```python
import functools
import math

import jax
import jax.numpy as jnp
from jax import lax
from jax.experimental import pallas as pl
from jax.experimental.pallas import tpu as pltpu

f32 = jnp.float32
bf16 = jnp.bfloat16
HI = lax.Precision.HIGHEST
MESH = pl.DeviceIdType.MESH

EPS = 1e-6
HEAD_DIM = 128
LANES = 128
GDN_CHUNK = 64
CONV_W = 4
MACARON_W = 0.5
ADAM_LR, ADAM_B1, ADAM_B2, ADAM_EPS, ADAM_WD, ADAM_STEP = 0.001, 0.9, 0.999, 1e-08, 0.01, 10
VMEM_LIMIT_V7X = 56 * 1024 * 1024
NEG = -1e30

NN = (((1,), (0,)), ((), ()))
NT = (((1,), (1,)), ((), ()))
TN = (((0,), (0,)), ((), ()))


def _cp(*sem):
    return pltpu.CompilerParams(dimension_semantics=sem, vmem_limit_bytes=VMEM_LIMIT_V7X)


def _dotb(a, b, dn=NN):
    return lax.dot_general(a.astype(bf16), b.astype(bf16), dn, preferred_element_type=f32)


def _doth(a, b, dn=NN):
    return lax.dot_general(a.astype(f32), b.astype(f32), dn, precision=HI, preferred_element_type=f32)


def _dotm(a, b, dn=NN):
    return lax.dot_general(a.astype(f32), b.astype(f32), dn, precision=lax.Precision.HIGH, preferred_element_type=f32)


def _sigmoid(x):
    return 1.0 / (1.0 + jnp.exp(-x))


def _softplus(x):
    return jnp.maximum(x, 0.0) + jnp.log(1.0 + jnp.exp(-jnp.abs(x)))


def _lane_col(blk, lane_idx):
    lane = lax.broadcasted_iota(jnp.int32, blk.shape, 1)
    return jnp.sum(jnp.where(lane == lane_idx, blk, 0.0), axis=1, keepdims=True)


def _tile(n, pref, mult=LANES):
    if n <= pref:
        return n
    t = (pref // mult) * mult
    while t >= mult:
        if n % t == 0:
            return t
        t -= mult
    return n


MM_EPILOGUE_CHUNKS = 2


def _mm(name, groups, mode, tiles, epilogue, out_dtypes, extras=(), comm=None):
    a0, b0 = groups[0][0]
    if mode == "nn":
        (m, k), n = a0.shape, b0.shape[1]
    elif mode == "nt":
        (m, k), n = a0.shape, b0.shape[0]
    else:
        (k, m), n = a0.shape, b0.shape[1]
    tm, tn, tk = _tile(m, tiles[0]), _tile(n, tiles[1]), _tile(k, tiles[2])
    nk = k // tk
    assert m % tm == 0 and n % tn == 0 and k % tk == 0, (name, m, n, k, tm, tn, tk)
    if mode == "nn":
        a_spec = pl.BlockSpec((tm, tk), lambda i, j, kk: (i, kk))
        b_spec = pl.BlockSpec((tk, tn), lambda i, j, kk: (kk, j))
        dn = NN
    elif mode == "nt":
        a_spec = pl.BlockSpec((tm, tk), lambda i, j, kk: (i, kk))
        b_spec = pl.BlockSpec((tn, tk), lambda i, j, kk: (j, kk))
        dn = NT
    else:
        a_spec = pl.BlockSpec((tk, tm), lambda i, j, kk: (kk, i))
        b_spec = pl.BlockSpec((tk, tn), lambda i, j, kk: (kk, j))
        dn = TN
    npairs = sum(len(g) for g in groups)
    nacc, nex, nout = len(groups), len(extras), len(out_dtypes)
    in_specs, args = [], []
    for g in groups:
        for a, b in g:
            in_specs += [a_spec, b_spec]
            args += [a, b]
    for arr, kind in extras:
        if kind == "mn":
            in_specs.append(pl.BlockSpec((tm, tn), lambda i, j, kk: (i, j)))
        else:
            in_specs.append(pl.BlockSpec((1, tn), lambda i, j, kk: (0, j)))
        args.append(arr)
    nci = len(comm.ins) if comm else 0
    nco = len(comm.out_shapes) if comm else 0
    grid = (m // tm, n // tn, nk)

    def body(*refs):
        ins = refs[: 2 * npairs]
        ex = refs[2 * npairs: 2 * npairs + nex]
        c_ins = refs[2 * npairs + nex: 2 * npairs + nex + nci]
        o0 = 2 * npairs + nex + nci
        outs = refs[o0: o0 + nout]
        c_outs = refs[o0 + nout: o0 + nout + nco]
        accs = refs[o0 + nout + nco: o0 + nout + nco + nacc]
        c_sems = refs[o0 + nout + nco + nacc:]
        kk = pl.program_id(2)
        if comm:
            step = (pl.program_id(0) * grid[1] + pl.program_id(1)) * nk + kk

            @pl.when(step == 0)
            def _():
                comm.run("start", c_ins, c_outs, c_sems)

        if nk == 1 and mode != "tn":
            nsub = MM_EPILOGUE_CHUNKS if tn % (MM_EPILOGUE_CHUNKS * LANES) == 0 else 1
            w = tn // nsub
            for cidx in range(nsub):
                lo = cidx * w
                sums, p = [], 0
                for g in groups:
                    t = None
                    for _ in g:
                        b = ins[2 * p + 1][lo:lo + w, :] if mode == "nt" else ins[2 * p + 1][:, lo:lo + w]
                        d = _dotb(ins[2 * p][...], b, dn)
                        t = d if t is None else t + d
                        p += 1
                    sums.append(t)
                res = epilogue(sums, [e[:, lo:lo + w] for e in ex])
                for o, r in zip(outs, res):
                    o[:, lo:lo + w] = r.astype(o.dtype)
        else:
            @pl.when(kk == 0)
            def _():
                for acc in accs:
                    acc[...] = jnp.zeros_like(acc)

            p = 0
            for gi, g in enumerate(groups):
                t = None
                for _ in g:
                    d = _dotb(ins[2 * p][...], ins[2 * p + 1][...], dn)
                    t = d if t is None else t + d
                    p += 1
                accs[gi][...] += t

            @pl.when(kk == nk - 1)
            def _():
                res = epilogue([acc[...] for acc in accs], [e[...] for e in ex])
                for o, r in zip(outs, res):
                    o[...] = r.astype(o.dtype)

        if comm:
            @pl.when(step == grid[0] * grid[1] * nk - 1)
            def _():
                comm.run("finish", c_ins, c_outs, c_sems)

    any_spec = pl.BlockSpec(memory_space=pl.ANY)
    res = pl.pallas_call(
        body, name=name, grid=grid,
        in_specs=in_specs + [any_spec] * nci,
        out_specs=[pl.BlockSpec((tm, tn), lambda i, j, kk: (i, j)) for _ in out_dtypes] + [any_spec] * nco,
        out_shape=[jax.ShapeDtypeStruct((m, n), dt) for dt in out_dtypes] + (list(comm.out_shapes) if comm else []),
        scratch_shapes=[pltpu.VMEM((tm, tn), f32) for _ in range(nacc)] + (comm.sems() if comm else []),
        compiler_params=_cp(*(("arbitrary",) * 3 if comm else ("parallel", "parallel", "arbitrary"))),
    )(*args, *(comm.ins if comm else []))
    return (res[:nout], res[nout:]) if comm else res


def _ep_plain(accs, ex):
    return (accs[0],)


def _ep_colscale(accs, ex):
    return (accs[0] * ex[0],)


def _ep_swiglu(accs, ex):
    gate, up = accs
    act = gate * _sigmoid(gate) * up
    return gate, up, act


def _ep_residual(accs, ex):
    x, gs = ex
    y = accs[0]
    return x + gs * y, y


def _ep_swiglu_bwd(accs, ex):
    gate, up = ex[0].astype(f32), ex[1].astype(f32)
    dact = accs[0]
    sg = _sigmoid(gate)
    silu = gate * sg
    act = silu * up
    dup = dact * silu
    dgate = dact * up * sg * (1.0 + gate * (1.0 - sg))
    return act, dgate, dup


def _row_tile(s):
    return _tile(s, 256, 8)


def _head_row_tile(s):
    return _tile(s, 1024, 8)


def _ada_in(name, x, g, shift, scale):
    s, d = x.shape
    tm = _row_tile(s)

    def body(x_ref, g_ref, sh_ref, sc_ref, h_ref):
        xv = x_ref[...]
        r = lax.rsqrt(jnp.mean(xv * xv, axis=-1, keepdims=True) + EPS)
        h_ref[...] = (xv * r * g_ref[...] * (1.0 + sc_ref[...]) + sh_ref[...]).astype(h_ref.dtype)

    row = pl.BlockSpec((1, d), lambda i: (0, 0))
    blk = pl.BlockSpec((tm, d), lambda i: (i, 0))
    return pl.pallas_call(body, name=name, grid=(s // tm,), in_specs=[blk, row, row, row], out_specs=blk,
                          out_shape=jax.ShapeDtypeStruct((s, d), bf16), compiler_params=_cp("parallel"))(x, g, shift, scale)


def _gate_terms(dxv, y_ref, gs_ref, dy_ref, dgs_ref):
    dy_ref[...] = (dxv * gs_ref[...]).astype(dy_ref.dtype)
    dgs_ref[...] += jnp.sum(dxv * y_ref[...].astype(f32), axis=0, keepdims=True)


def _ada_bwd(name, x, g, scale, dh, dres, y, gs):
    s, d = x.shape
    tm = _row_tile(s)

    def body(x_ref, g_ref, sc_ref, dh_ref, dres_ref, y_ref, gs_ref, dx_ref, dsh_ref, a_ref, dy_ref, dgs_ref):
        i = pl.program_id(0)

        @pl.when(i == 0)
        def _():
            dsh_ref[...] = jnp.zeros_like(dsh_ref)
            a_ref[...] = jnp.zeros_like(a_ref)
            dgs_ref[...] = jnp.zeros_like(dgs_ref)

        xv = x_ref[...]
        dhv = dh_ref[...].astype(f32)
        r = lax.rsqrt(jnp.mean(xv * xv, axis=-1, keepdims=True) + EPS)
        n = xv * r
        dn = dhv * (g_ref[...] * (1.0 + sc_ref[...]))
        dxv = dres_ref[...] + r * (dn - n * jnp.mean(dn * n, axis=-1, keepdims=True))
        dx_ref[...] = dxv
        dsh_ref[...] += jnp.sum(dhv, axis=0, keepdims=True)
        a_ref[...] += jnp.sum(dhv * n, axis=0, keepdims=True)
        _gate_terms(dxv, y_ref, gs_ref, dy_ref, dgs_ref)

    row = pl.BlockSpec((1, d), lambda i: (0, 0))
    blk = pl.BlockSpec((tm, d), lambda i: (i, 0))
    return pl.pallas_call(
        body, name=name, grid=(s // tm,), in_specs=[blk, row, row, blk, blk, blk, row], out_specs=[blk, row, row, blk, row],
        out_shape=[jax.ShapeDtypeStruct((s, d), f32), jax.ShapeDtypeStruct((1, d), f32), jax.ShapeDtypeStruct((1, d), f32),
                   jax.ShapeDtypeStruct((s, d), bf16), jax.ShapeDtypeStruct((1, d), f32)],
        compiler_params=_cp("arbitrary"))(x, g, scale, dh, dres, y, gs)


def _ada_bwd_first(name, x, g, scale, dh, dres):
    s, d = x.shape
    tm = _row_tile(s)

    def body(x_ref, g_ref, sc_ref, dh_ref, dres_ref, dx_ref, dsh_ref, a_ref):
        i = pl.program_id(0)

        @pl.when(i == 0)
        def _():
            dsh_ref[...] = jnp.zeros_like(dsh_ref)
            a_ref[...] = jnp.zeros_like(a_ref)

        xv = x_ref[...]
        dhv = dh_ref[...].astype(f32)
        r = lax.rsqrt(jnp.mean(xv * xv, axis=-1, keepdims=True) + EPS)
        n = xv * r
        dn = dhv * (g_ref[...] * (1.0 + sc_ref[...]))
        dx_ref[...] = dres_ref[...] + r * (dn - n * jnp.mean(dn * n, axis=-1, keepdims=True))
        dsh_ref[...] += jnp.sum(dhv, axis=0, keepdims=True)
        a_ref[...] += jnp.sum(dhv * n, axis=0, keepdims=True)

    row = pl.BlockSpec((1, d), lambda i: (0, 0))
    blk = pl.BlockSpec((tm, d), lambda i: (i, 0))
    return pl.pallas_call(
        body, name=name, grid=(s // tm,), in_specs=[blk, row, row, blk, blk], out_specs=[blk, row, row],
        out_shape=[jax.ShapeDtypeStruct((s, d), f32), jax.ShapeDtypeStruct((1, d), f32), jax.ShapeDtypeStruct((1, d), f32)],
        compiler_params=_cp("arbitrary"))(x, g, scale, dh, dres)


def _final_loss(x, fg, target, y, gs):
    s, d = x.shape
    tm = _row_tile(s)

    def body(x_ref, g_ref, t_ref, y_ref, gs_ref, loss_ref, dx_ref, dg_ref, dy_ref, dgs_ref):
        i = pl.program_id(0)

        @pl.when(i == 0)
        def _():
            loss_ref[...] = jnp.zeros_like(loss_ref)
            dg_ref[...] = jnp.zeros_like(dg_ref)
            dgs_ref[...] = jnp.zeros_like(dgs_ref)

        xv = x_ref[...]
        gv = g_ref[...]
        r = lax.rsqrt(jnp.mean(xv * xv, axis=-1, keepdims=True) + EPS)
        n = xv * r
        e = n * gv - t_ref[...]
        per_tok = jnp.mean(e * e, axis=-1, keepdims=True)
        loss_ref[...] += 0.5 * jnp.sum(per_tok, axis=0, keepdims=True) * jnp.ones((1, LANES), f32)
        dy = e * (1.0 / d)
        dg_ref[...] += jnp.sum(dy * n, axis=0, keepdims=True)
        dn = dy * gv
        dxv = r * (dn - n * jnp.mean(dn * n, axis=-1, keepdims=True))
        dx_ref[...] = dxv
        _gate_terms(dxv, y_ref, gs_ref, dy_ref, dgs_ref)

    row = pl.BlockSpec((1, d), lambda i: (0, 0))
    blk = pl.BlockSpec((tm, d), lambda i: (i, 0))
    return pl.pallas_call(
        body, name="final_loss", grid=(s // tm,), in_specs=[blk, row, blk, blk, row],
        out_specs=[pl.BlockSpec((1, LANES), lambda i: (0, 0)), blk, row, blk, row],
        out_shape=[jax.ShapeDtypeStruct((1, LANES), f32), jax.ShapeDtypeStruct((s, d), f32), jax.ShapeDtypeStruct((1, d), f32),
                   jax.ShapeDtypeStruct((s, d), bf16), jax.ShapeDtypeStruct((1, d), f32)],
        compiler_params=_cp("arbitrary"))(x, fg, target, y, gs)


def _small_fwd(ps, prm, nh):
    s = ps.shape[0]
    tb = LANES

    def body(ps_ref, prm_ref, sm_ref, cum_ref, carry):
        i = pl.program_id(0)

        @pl.when(i == 0)
        def _():
            carry[...] = jnp.zeros_like(carry)

        x = ps_ref[...]
        lane = lax.broadcasted_iota(jnp.int32, x.shape, 1)
        fb, dtb, alog = prm_ref[0:1, :], prm_ref[1:2, :], prm_ref[2:3, :]
        logf = -_softplus(-(x + fb))
        glog = -jnp.exp(alog) * _softplus(x + dtb)
        beta = _sigmoid(x)
        sm = jnp.where(lane < nh, logf, jnp.where(lane < 2 * nh, glog, jnp.where(lane < 3 * nh, beta, 0.0)))
        sm_ref[...] = sm
        r = lax.broadcasted_iota(jnp.int32, (tb, tb), 0)
        c = lax.broadcasted_iota(jnp.int32, (tb, tb), 1)
        tril = (c <= r).astype(f32)
        cs = _doth(tril, sm) + carry[...]
        cum_ref[...] = cs
        carry[...] = cs[tb - 1:tb, :]

    blk = pl.BlockSpec((tb, LANES), lambda i: (i, 0))
    return pl.pallas_call(
        body, name="small_fwd", grid=(s // tb,),
        in_specs=[blk, pl.BlockSpec((8, LANES), lambda i: (0, 0))],
        out_specs=[blk, blk],
        out_shape=[jax.ShapeDtypeStruct((s, LANES), f32), jax.ShapeDtypeStruct((s, LANES), f32)],
        scratch_shapes=[pltpu.VMEM((1, LANES), f32)],
        compiler_params=_cp("arbitrary"))(ps, prm)


def _small_bwd(ps, prm, dsm, dcum, nh):
    s = ps.shape[0]
    tb = LANES
    nb = s // tb

    def body(ps_ref, prm_ref, dsm_ref, dct_ref, dps_ref, pg_ref, carry):
        i = pl.program_id(0)

        @pl.when(i == 0)
        def _():
            carry[...] = jnp.zeros_like(carry)
            pg_ref[...] = jnp.zeros_like(pg_ref)

        x = ps_ref[...]
        dsm = dsm_ref[...]
        lane = lax.broadcasted_iota(jnp.int32, x.shape, 1)
        fb, dtb, alog = prm_ref[0:1, :], prm_ref[1:2, :], prm_ref[2:3, :]
        r = lax.broadcasted_iota(jnp.int32, (tb, tb), 0)
        c = lax.broadcasted_iota(jnp.int32, (tb, tb), 1)
        triu = (c >= r).astype(f32)
        dlogf = _doth(triu, dct_ref[...]) + carry[...]
        carry[...] = dlogf[0:1, :]
        d_f = dlogf * _sigmoid(-(x + fb))
        nega = -jnp.exp(alog)
        xa = x + dtb
        glog = nega * _softplus(xa)
        d_a = dsm * nega * _sigmoid(xa)
        beta = _sigmoid(x)
        d_b = dsm * beta * (1.0 - beta)
        dps = jnp.where(lane < nh, d_f, jnp.where(lane < 2 * nh, d_a, jnp.where(lane < 3 * nh, d_b, 0.0)))
        dps_ref[...] = dps.astype(dps_ref.dtype)
        row0 = jnp.sum(dps, axis=0, keepdims=True)
        row1 = jnp.sum(jnp.where((lane >= nh) & (lane < 2 * nh), dsm * glog, 0.0), axis=0, keepdims=True)
        sub = lax.broadcasted_iota(jnp.int32, (8, LANES), 0)
        pg_ref[...] += jnp.where(sub == 0, row0, jnp.where(sub == 1, row1, 0.0))

    rev = pl.BlockSpec((tb, LANES), lambda i: (nb - 1 - i, 0))
    fix = pl.BlockSpec((8, LANES), lambda i: (0, 0))
    return pl.pallas_call(
        body, name="small_bwd", grid=(nb,),
        in_specs=[rev, fix, rev, rev],
        out_specs=[rev, fix],
        out_shape=[jax.ShapeDtypeStruct((s, LANES), bf16), jax.ShapeDtypeStruct((8, LANES), f32)],
        scratch_shapes=[pltpu.VMEM((1, LANES), f32)],
        compiler_params=_cp("arbitrary"))(ps, prm, dsm, dcum)


LOG2E = 1.4426950408889634
LN2 = 0.6931471805599453
AUG = 2 * HEAD_DIM
FOX_Q_SCALE = LOG2E / math.sqrt(HEAD_DIM)
FOX_KEY_GROUP = 8


def _split3(col):
    hi = col.astype(bf16).astype(f32)
    r1 = col - hi
    mid = r1.astype(bf16).astype(f32)
    lo = (r1 - mid).astype(bf16).astype(f32)
    return hi, mid, lo


def _aug_block(rows, terms, terms_at, ones_at=None):
    lane = lax.broadcasted_iota(jnp.int32, (rows, LANES), 1)
    blk = jnp.zeros((rows, LANES), f32) if ones_at is None else jnp.where((lane >= ones_at) & (lane < ones_at + 3), 1.0, 0.0)
    for i, t in enumerate(terms):
        blk = jnp.where(lane == terms_at + i, t, blk)
    return blk


def _fox_aug(qkv, cum, nh):
    s = qkv.shape[0]
    tm = _head_row_tile(s)

    def body(q_ref, k_ref, v_ref, cum_ref, qa_ref, ka_ref, va_ref):
        h = pl.program_id(1)
        c2 = _lane_col(cum_ref[...], h) * LOG2E
        hi, mid, lo = _split3(c2)
        qa_ref[:, :HEAD_DIM] = q_ref[...]
        qa_ref[:, HEAD_DIM:] = _aug_block(tm, (hi, mid, lo), 0, 3).astype(bf16)
        ka_ref[:, :HEAD_DIM] = k_ref[...]
        ka_ref[:, HEAD_DIM:] = _aug_block(tm, (-hi, -mid, -lo), 3, 0).astype(bf16)
        va_ref[:, :HEAD_DIM] = v_ref[...]
        va_ref[:, HEAD_DIM:] = _aug_block(tm, (), 0, 0).astype(bf16)

    ab = pl.BlockSpec((tm, AUG), lambda i, h: (i, h))
    return pl.pallas_call(
        body, name="fox_aug", grid=(s // tm, nh),
        in_specs=[pl.BlockSpec((tm, HEAD_DIM), lambda i, h: (i, h)), pl.BlockSpec((tm, HEAD_DIM), lambda i, h: (i, nh + h)),
                  pl.BlockSpec((tm, HEAD_DIM), lambda i, h: (i, 2 * nh + h)), pl.BlockSpec((tm, LANES), lambda i, h: (i, 0))],
        out_specs=[ab, ab, ab], out_shape=[jax.ShapeDtypeStruct((s, nh * AUG), bf16)] * 3,
        compiler_params=_cp("parallel", "parallel"))(qkv, qkv, qkv, cum)


def _fox_fwd(qa, ka, qkv, wn, nh, tq):
    s = qa.shape[0]
    fw = nh * HEAD_DIM
    group = FOX_KEY_GROUP
    while group > s // tq:
        group //= 2

    def body(qa_ref, ka_ref, v_ref, wn_ref, o_ref, on_ref, lse_ref):
        i = pl.program_id(1)
        q = qa_ref[...]

        def logits_t(j, rows):
            return _dotb(ka_ref[pl.ds(pl.multiple_of(j * tq, tq), rows), :], q, NT)

        def pv_t(j, p_t):
            return _dotb(v_ref[pl.ds(pl.multiple_of(j * tq, tq), p_t.shape[0]), :], p_t, TN)

        def update(j0, blocks, carry):
            m, l, acc = carry
            parts = [(j0, blocks)] if blocks == 1 else [(j0, blocks // 2), (j0 + blocks // 2, blocks // 2)]
            ts = [logits_t(j, nb * tq) for j, nb in parts]
            mn = functools.reduce(jnp.maximum, [m] + [jnp.max(t, axis=0, keepdims=True) for t in ts])
            ps = [jnp.exp2(t - mn) for t in ts]
            alpha = jnp.exp2(m - mn)
            l = functools.reduce(jnp.add, [alpha * l] + [jnp.sum(p, axis=0, keepdims=True) for p in ps])
            acc = functools.reduce(jnp.add, [alpha * acc] + [pv_t(j, p) for (j, _), p in zip(parts, ps)])
            return mn, l, acc

        key = lax.broadcasted_iota(jnp.int32, (tq, tq), 0)
        qry = lax.broadcasted_iota(jnp.int32, (tq, tq), 1)
        t = jnp.where(key <= qry, logits_t(i, tq), NEG)
        m = jnp.max(t, axis=0, keepdims=True)
        p_t = jnp.exp2(t - m)
        carry = (m, jnp.sum(p_t, axis=0, keepdims=True), pv_t(i, p_t))
        carry = lax.fori_loop(0, i // group, lambda jj, c: update(group * jj, group, c), carry)
        start, part = group * (i // group), group // 2
        while part:
            has = lax.rem(i // part, 2)
            carry = lax.fori_loop(0, has, functools.partial(lambda _, c, j0, blocks: update(j0, blocks, c), j0=start, blocks=part), carry)
            start, part = start + part * has, part // 2
        m, l, acc = carry
        o = (acc / l).T
        o_ref[...] = o
        sub = lax.broadcasted_iota(jnp.int32, (LANES, tq), 0)
        lse_ref[0] = jnp.where(sub == 0, m + jnp.log2(l), 0.0).T
        r = lax.rsqrt(jnp.mean(o * o, axis=-1, keepdims=True) + EPS)
        on_ref[...] = (o * r * wn_ref[...]).astype(on_ref.dtype)

    hb = pl.BlockSpec((tq, HEAD_DIM), lambda h, i: (i, h))
    return pl.pallas_call(
        body, name="fox_fwd", grid=(nh, s // tq),
        in_specs=[pl.BlockSpec((tq, AUG), lambda h, i: (i, h)), pl.BlockSpec((s, AUG), lambda h, i: (0, h)),
                  pl.BlockSpec((s, HEAD_DIM), lambda h, i: (0, 2 * nh + h)), pl.BlockSpec((1, HEAD_DIM), lambda h, i: (0, 0))],
        out_specs=[hb, hb, pl.BlockSpec((1, tq, LANES), lambda h, i: (h, i, 0))],
        out_shape=[jax.ShapeDtypeStruct((s, fw), f32), jax.ShapeDtypeStruct((s, fw), bf16), jax.ShapeDtypeStruct((nh, s, LANES), f32)],
        compiler_params=_cp("parallel", "parallel"))(qa, ka, qkv, wn)


def _fox_post_bwd(don, o, lse2, cum, qkv, wn, nh):
    s, fw = o.shape
    tm = _head_row_tile(s)

    def body(don_ref, o_ref, lse_ref, cum_ref, q_ref, wn_ref, qb_ref, doa_ref, qbt_ref, dot_ref, dwn_ref):
        i = pl.program_id(0)
        h = pl.program_id(1)

        @pl.when((i == 0) & (h == 0))
        def _():
            dwn_ref[...] = jnp.zeros_like(dwn_ref)

        o = o_ref[...]
        don = don_ref[...].astype(f32)
        r = lax.rsqrt(jnp.mean(o * o, axis=-1, keepdims=True) + EPS)
        n = o * r
        dwn_ref[...] += jnp.sum(don * n, axis=0, keepdims=True)
        dn = don * wn_ref[...]
        do = r * (dn - n * jnp.mean(dn * n, axis=-1, keepdims=True))
        delta = jnp.sum(do * o, axis=-1, keepdims=True)
        a2 = _lane_col(cum_ref[...], h) * LOG2E - _lane_col(lse_ref[0], 0)
        q_aug = _aug_block(tm, _split3(a2), 0, 3)
        q = q_ref[...]
        qb_ref[:, :HEAD_DIM] = q
        qb_ref[:, HEAD_DIM:] = q_aug.astype(bf16)
        doa_ref[:, :HEAD_DIM] = do.astype(bf16)
        doa_ref[:, HEAD_DIM:] = _aug_block(tm, _split3(-delta), 0).astype(bf16)
        qbt_ref[:HEAD_DIM, :] = q.astype(f32).T.astype(bf16)
        qbt_ref[HEAD_DIM:, :] = q_aug.T.astype(bf16)
        dot_ref[...] = do.T.astype(bf16)

    hb = pl.BlockSpec((tm, HEAD_DIM), lambda i, h: (i, h))
    ab = pl.BlockSpec((tm, AUG), lambda i, h: (i, h))
    return pl.pallas_call(
        body, name="fox_post_bwd", grid=(s // tm, nh),
        in_specs=[hb, hb, pl.BlockSpec((1, tm, LANES), lambda i, h: (h, i, 0)), pl.BlockSpec((tm, LANES), lambda i, h: (i, 0)),
                  hb, pl.BlockSpec((1, HEAD_DIM), lambda i, h: (0, 0))],
        out_specs=[ab, ab, pl.BlockSpec((AUG, tm), lambda i, h: (h, i)), pl.BlockSpec((HEAD_DIM, tm), lambda i, h: (h, i)),
                   pl.BlockSpec((1, HEAD_DIM), lambda i, h: (0, 0))],
        out_shape=[jax.ShapeDtypeStruct((s, nh * AUG), bf16), jax.ShapeDtypeStruct((s, nh * AUG), bf16),
                   jax.ShapeDtypeStruct((nh * AUG, s), bf16), jax.ShapeDtypeStruct((nh * HEAD_DIM, s), bf16),
                   jax.ShapeDtypeStruct((1, HEAD_DIM), f32)],
        compiler_params=_cp("arbitrary", "arbitrary"))(don, o, lse2, cum, qkv, wn)


def _fox_bwd(qb, doa, qb_t, do_t, ka, va, nh, tq):
    s = qb.shape[0]
    nq = s // tq
    fw = nh * HEAD_DIM

    def body(qb_ref, doa_ref, qbt_ref, dot_ref, ka_ref, va_ref, dqx_ref, dkx_ref, dv_ref, dk_acc, dv_acc):
        j = pl.program_id(1)

        @pl.when(j == 0)
        def _():
            dqx_ref[...] = jnp.zeros_like(dqx_ref)

        kj = ka_ref[...]
        vj = va_ref[...]

        def tile(i, rows=tq, first=False, keep=None):
            off = pl.multiple_of(i * tq, tq)
            p = jnp.exp2(_dotb(qb_ref[pl.ds(off, rows), :], kj, NT))
            if keep is not None:
                p = jnp.where(keep, p, 0.0)
            ds = (p * _dotb(doa_ref[pl.ds(off, rows), :], vj, NT)).astype(bf16)
            dv = _dotb(dot_ref[:, pl.ds(off, rows)], p)
            dk = _dotb(qbt_ref[:, pl.ds(off, rows)], ds)
            if first:
                dv_acc[...] = dv
                dk_acc[...] = dk
            else:
                dv_acc[...] += dv
                dk_acc[...] += dk
            dqx_ref[pl.ds(off, rows), :] += _dotb(ds, kj)

        n = nq - 1 - j
        b0 = jnp.minimum(j, nq - 2)
        qpos = b0 * tq + lax.broadcasted_iota(jnp.int32, (2 * tq, tq), 0)
        kpos = j * tq + lax.broadcasted_iota(jnp.int32, (2 * tq, tq), 1)
        tile(b0, 2 * tq, first=True, keep=(kpos <= qpos) & ((qpos < (j + 1) * tq) | (lax.rem(n, 2) == 1)))

        first_pair = j + 1 + lax.rem(n, 2)
        pairs = n // 2

        def pair(ii, carry):
            tile(first_pair, 2 * tq)
            return carry

        def quad(ii, carry):
            tile(first_pair + 2 * lax.rem(pairs, 2) + 4 * ii, 4 * tq)
            return carry

        lax.fori_loop(0, lax.rem(pairs, 2), pair, 0)
        lax.fori_loop(0, pairs // 2, quad, 0)
        dkx_ref[...] = dk_acc[...].T
        dv_ref[...] = dv_acc[...].T.astype(dv_ref.dtype)

    panel = pl.BlockSpec((s, AUG), lambda h, j: (0, h))
    blk = pl.BlockSpec((tq, AUG), lambda h, j: (j, h))
    return pl.pallas_call(
        body, name="fox_bwd", grid=(nh, nq),
        in_specs=[panel, panel, pl.BlockSpec((AUG, s), lambda h, j: (h, 0)), pl.BlockSpec((HEAD_DIM, s), lambda h, j: (h, 0)),
                  blk, blk],
        out_specs=[panel, blk, pl.BlockSpec((tq, HEAD_DIM), lambda h, j: (j, h))],
        out_shape=[jax.ShapeDtypeStruct((s, nh * AUG), f32), jax.ShapeDtypeStruct((s, nh * AUG), f32),
                   jax.ShapeDtypeStruct((s, fw), bf16)],
        scratch_shapes=[pltpu.VMEM((AUG, tq), f32), pltpu.VMEM((HEAD_DIM, tq), f32)],
        compiler_params=_cp("parallel", "arbitrary"))(qb, doa, qb_t, do_t, ka, va)


def _fox_unpack(dqx, dkx, nh):
    s = dqx.shape[0]
    fw = nh * HEAD_DIM
    tm = _head_row_tile(s)

    def body(dqx_ref, dkx_ref, dq_ref, dk_ref, dcum_ref):
        h = pl.program_id(1)

        @pl.when(h == 0)
        def _():
            dcum_ref[...] = jnp.zeros_like(dcum_ref)

        dq_ref[...] = (dqx_ref[:, :HEAD_DIM] * (HEAD_DIM ** -0.5)).astype(dq_ref.dtype)
        dk_ref[...] = (dkx_ref[:, :HEAD_DIM] * LN2).astype(dk_ref.dtype)
        d = _lane_col(dqx_ref[:, HEAD_DIM:], 0) - _lane_col(dkx_ref[:, HEAD_DIM:], 3)
        lane = lax.broadcasted_iota(jnp.int32, (tm, LANES), 1)
        dcum_ref[...] += jnp.where(lane == h, d, 0.0)

    ab = pl.BlockSpec((tm, AUG), lambda i, h: (i, h))
    hb = pl.BlockSpec((tm, HEAD_DIM), lambda i, h: (i, h))
    return pl.pallas_call(
        body, name="fox_unpack", grid=(s // tm, nh), in_specs=[ab, ab],
        out_specs=[hb, hb, pl.BlockSpec((tm, LANES), lambda i, h: (i, 0))],
        out_shape=[jax.ShapeDtypeStruct((s, fw), bf16), jax.ShapeDtypeStruct((s, fw), bf16), jax.ShapeDtypeStruct((s, LANES), f32)],
        compiler_params=_cp("parallel", "arbitrary"))(dqx, dkx)


def _conv_pre(xx, w, tm):
    pre = None
    for k in range(CONV_W):
        sh = CONV_W - 1 - k
        t = (pltpu.roll(xx, sh, 0) if sh else xx)[8:, :] * w[k:k + 1, :]
        pre = t if pre is None else pre + t
    return pre


def _gdn_pre(x, w, nh):
    s, cw = x.shape
    tm = _row_tile(s)
    fw = nh * HEAD_DIM

    def body(x_ref, prev_ref, w_ref, y_ref):
        i = pl.program_id(0)
        j = pl.program_id(1)
        for h in range(nh):
            sl = slice(h * HEAD_DIM, (h + 1) * HEAD_DIM)
            prev = jnp.where(i == 0, 0.0, prev_ref[:, sl])
            pre = _conv_pre(jnp.concatenate([prev, x_ref[:, sl]], axis=0), w_ref[:, sl], tm)
            y = pre * _sigmoid(pre)
            yn = y * lax.rsqrt(jnp.sum(y * y, axis=-1, keepdims=True) + EPS)
            y_ref[:, sl] = jnp.where(j < 2, yn, y)

    return pl.pallas_call(
        body, name="gdn_pre", grid=(s // tm, cw // fw),
        in_specs=[pl.BlockSpec((tm, fw), lambda i, j: (i, j)),
                  pl.BlockSpec((8, fw), lambda i, j: (jnp.maximum(i * (tm // 8) - 1, 0), j)),
                  pl.BlockSpec((CONV_W, fw), lambda i, j: (0, j))],
        out_specs=pl.BlockSpec((tm, fw), lambda i, j: (i, j)),
        out_shape=jax.ShapeDtypeStruct((s, cw), f32),
        compiler_params=_cp("parallel", "parallel"))(x, x, w)


def _gdn_pre_bwd(x, w, dyn, nh):
    s, cw = x.shape
    tm = _row_tile(s)
    fw = nh * HEAD_DIM

    def body(x_ref, prev_ref, w_ref, dyn_ref, dpre_ref):
        i = pl.program_id(0)
        j = pl.program_id(1)
        for h in range(nh):
            sl = slice(h * HEAD_DIM, (h + 1) * HEAD_DIM)
            prev = jnp.where(i == 0, 0.0, prev_ref[:, sl])
            pre = _conv_pre(jnp.concatenate([prev, x_ref[:, sl]], axis=0), w_ref[:, sl], tm)
            sg = _sigmoid(pre)
            y = pre * sg
            dyn = dyn_ref[:, sl]
            r = lax.rsqrt(jnp.sum(y * y, axis=-1, keepdims=True) + EPS)
            yn = y * r
            dy_n = r * (dyn - yn * jnp.sum(dyn * yn, axis=-1, keepdims=True))
            dy = jnp.where(j < 2, dy_n, dyn)
            dpre_ref[:, sl] = dy * sg * (1.0 + pre * (1.0 - sg))

    hb = pl.BlockSpec((tm, fw), lambda i, j: (i, j))
    return pl.pallas_call(
        body, name="gdn_pre_bwd", grid=(s // tm, cw // fw),
        in_specs=[hb, pl.BlockSpec((8, fw), lambda i, j: (jnp.maximum(i * (tm // 8) - 1, 0), j)),
                  pl.BlockSpec((CONV_W, fw), lambda i, j: (0, j)), hb],
        out_specs=hb, out_shape=jax.ShapeDtypeStruct((s, cw), f32),
        compiler_params=_cp("parallel", "parallel"))(x, x, w, dyn)


def _conv_bwd(x, w, dpre, nh):
    s, cw = x.shape
    tm = _row_tile(s)
    fw = nh * HEAD_DIM
    ni = s // tm

    def body(x_ref, prev_ref, w_ref, dp_ref, nxt_ref, dx_ref, dw_ref):
        i = pl.program_id(1)

        @pl.when(i == 0)
        def _():
            dw_ref[...] = jnp.zeros_like(dw_ref)

        for h in range(nh):
            sl = slice(h * HEAD_DIM, (h + 1) * HEAD_DIM)
            wv = w_ref[:, sl]
            dp = dp_ref[:, sl]
            nxt = jnp.where(i == ni - 1, 0.0, nxt_ref[:, sl])
            dd = jnp.concatenate([dp, nxt], axis=0)
            prev = jnp.where(i == 0, 0.0, prev_ref[:, sl])
            xx = jnp.concatenate([prev, x_ref[:, sl]], axis=0)
            dx = None
            rows = []
            for k in range(CONV_W):
                sh = CONV_W - 1 - k
                t = (pltpu.roll(dd, tm + 8 - sh, 0) if sh else dd)[:tm, :] * wv[k:k + 1, :]
                dx = t if dx is None else dx + t
                xs = (pltpu.roll(xx, sh, 0) if sh else xx)[8:, :]
                rows.append(jnp.sum(dp * xs, axis=0, keepdims=True))
            dx_ref[:, sl] = dx.astype(dx_ref.dtype)
            dw_ref[:, sl] += jnp.concatenate(rows, axis=0)

    hb = pl.BlockSpec((tm, fw), lambda j, i: (i, j))
    wb = pl.BlockSpec((CONV_W, fw), lambda j, i: (0, j))
    return pl.pallas_call(
        body, name="conv_bwd", grid=(cw // fw, ni),
        in_specs=[hb, pl.BlockSpec((8, fw), lambda j, i: (jnp.maximum(i * (tm // 8) - 1, 0), j)), wb, hb,
                  pl.BlockSpec((8, fw), lambda j, i: (jnp.minimum((i + 1) * (tm // 8), s // 8 - 1), j))],
        out_specs=[hb, wb],
        out_shape=[jax.ShapeDtypeStruct((s, cw), bf16), jax.ShapeDtypeStruct((CONV_W, cw), f32)],
        compiler_params=_cp("parallel", "arbitrary"))(x, x, w, dpre, dpre)


def _chunk_consts():
    c = GDN_CHUNK
    r = lax.broadcasted_iota(jnp.int32, (c, c), 0)
    q = lax.broadcasted_iota(jnp.int32, (c, c), 1)
    return r >= q, r > q, (r == q).astype(f32)


def _chunk_head(qkvn, sm, gcs, gcs_t, h, nh):
    fw = nh * HEAD_DIM
    q = qkvn[:, h * HEAD_DIM:(h + 1) * HEAD_DIM] * (HEAD_DIM ** -0.5)
    k = qkvn[:, fw + h * HEAD_DIM: fw + (h + 1) * HEAD_DIM]
    v = qkvn[:, 2 * fw + h * HEAD_DIM: 2 * fw + (h + 1) * HEAD_DIM]
    beta = _lane_col(sm, 2 * nh + h)
    gc = _lane_col(gcs, nh + h)
    gc_row = gcs_t[nh + h: nh + h + 1, :]
    incl, strict, _ = _chunk_consts()
    decay = jnp.where(incl, jnp.exp(jnp.minimum(gc - gc_row, 0.0)), 0.0)
    eg = jnp.exp(gc)
    g_last = gc[GDN_CHUNK - 1:GDN_CHUNK, :]
    egl = jnp.exp(g_last)
    ekd = jnp.exp(g_last - gc)
    kb = k * beta
    vb = v * beta
    kk = _dotb(kb, k, NT)
    qk = _dotb(q, k, NT)
    return dict(q=q, k=k, v=v, beta=beta, gc=gc, decay=decay, eg=eg, egl=egl, ekd=ekd, kb=kb, vb=vb, kk=kk, qk=qk,
                incl=incl, strict=strict)


def _unit_lower_inverses(lows, eye):
    c = GDN_CHUNK
    ts = [eye - low for low in lows]
    ps = [_dotm(low, low) for low in lows]
    for _ in range(4):
        both = [_dotm(jnp.concatenate([p, t], axis=0), p) for p, t in zip(ps, ts)]
        ts = [t + b[c:] for t, b in zip(ts, both)]
        ps = [b[:c] for b in both]
    return [t + _dotm(t, p) for t, p in zip(ts, ps)]


def _gdn_fwd(qkvn, sm, z, wn, nh):
    s = qkvn.shape[0]
    c = GDN_CHUNK
    nc = s // c
    fw = nh * HEAD_DIM

    def body(qkvn_ref, sm_ref, z_ref, wn_ref, on_ref, o_ref, st_ref, ti_ref, state):
        ci = pl.program_id(0)

        @pl.when(ci == 0)
        def _():
            state[...] = jnp.zeros_like(state)

        qkvn_v = qkvn_ref[...]
        sm_v = sm_ref[...]
        incl, strict, eye = _chunk_consts()
        gcs = _doth(incl.astype(f32), sm_v)
        gcs_t = gcs.T
        heads = range(nh)
        es = [_chunk_head(qkvn_v, sm_v, gcs, gcs_t, h, nh) for h in heads]
        tinvs = _unit_lower_inverses([jnp.where(strict, e["kk"] * e["decay"], 0.0) for e in es], eye)
        uws = [_dotm(t, jnp.concatenate([e["vb"], e["kb"] * e["eg"]], axis=1)) for t, e in zip(tinvs, es)]
        sts = [state[h] for h in heads]
        wq_s = [_dotb(jnp.concatenate([uw[:, HEAD_DIM:], e["q"] * e["eg"]], axis=0), st) for uw, e, st in zip(uws, es, sts)]
        v_news = [uw[:, :HEAD_DIM] - ws[:c] for uw, ws in zip(uws, wq_s)]
        os_ = [ws[c:] + _dotb(jnp.where(incl, e["qk"] * e["decay"], 0.0), vn) for ws, e, vn in zip(wq_s, es, v_news)]
        upd = [_dotb(e["k"] * e["ekd"], vn, TN) for e, vn in zip(es, v_news)]
        for h in heads:
            st_ref[0, h] = sts[h]
            ti_ref[0, h] = tinvs[h]
            state[h] = sts[h] * es[h]["egl"] + upd[h]
            sl = slice(h * HEAD_DIM, (h + 1) * HEAD_DIM)
            o = os_[h]
            o_ref[:, sl] = o
            zz = z_ref[:, sl]
            r = lax.rsqrt(jnp.mean(o * o, axis=-1, keepdims=True) + EPS)
            on_ref[:, sl] = (o * r * wn_ref[...] * (zz * _sigmoid(zz))).astype(on_ref.dtype)

    return pl.pallas_call(
        body, name="gdn_fwd", grid=(nc,),
        in_specs=[pl.BlockSpec((c, 3 * fw), lambda i: (i, 0)), pl.BlockSpec((c, LANES), lambda i: (i, 0)),
                  pl.BlockSpec((c, fw), lambda i: (i, 0)), pl.BlockSpec((1, HEAD_DIM), lambda i: (0, 0))],
        out_specs=[pl.BlockSpec((c, fw), lambda i: (i, 0)), pl.BlockSpec((c, fw), lambda i: (i, 0)),
                   pl.BlockSpec((1, nh, HEAD_DIM, HEAD_DIM), lambda i: (i, 0, 0, 0)),
                   pl.BlockSpec((1, nh, c, c), lambda i: (i, 0, 0, 0))],
        out_shape=[jax.ShapeDtypeStruct((s, fw), bf16), jax.ShapeDtypeStruct((s, fw), f32),
                   jax.ShapeDtypeStruct((nc, nh, HEAD_DIM, HEAD_DIM), f32), jax.ShapeDtypeStruct((nc, nh, c, c), f32)],
        scratch_shapes=[pltpu.VMEM((nh, HEAD_DIM, HEAD_DIM), f32)],
        compiler_params=_cp("arbitrary"))(qkvn, sm, z, wn)


def _gdn_post_bwd(don, o, z, wn, nh):
    s, fw = o.shape
    tm = _head_row_tile(s)

    def body(don_ref, o_ref, z_ref, wn_ref, do_ref, dz_ref, dwn_ref):
        i = pl.program_id(0)
        h = pl.program_id(1)

        @pl.when((i == 0) & (h == 0))
        def _():
            dwn_ref[...] = jnp.zeros_like(dwn_ref)

        o = o_ref[...]
        zz = z_ref[...]
        don = don_ref[...].astype(f32)
        wv = wn_ref[...]
        r = lax.rsqrt(jnp.mean(o * o, axis=-1, keepdims=True) + EPS)
        n = o * r
        sg = _sigmoid(zz)
        silu = zz * sg
        dz_ref[...] = (don * n * wv * sg * (1.0 + zz * (1.0 - sg))).astype(dz_ref.dtype)
        dnw = don * silu
        dwn_ref[...] += jnp.sum(dnw * n, axis=0, keepdims=True)
        dn = dnw * wv
        do_ref[...] = r * (dn - n * jnp.mean(dn * n, axis=-1, keepdims=True))

    hb = pl.BlockSpec((tm, HEAD_DIM), lambda i, h: (i, h))
    wb = pl.BlockSpec((1, HEAD_DIM), lambda i, h: (0, 0))
    return pl.pallas_call(
        body, name="gdn_post_bwd", grid=(s // tm, nh), in_specs=[hb, hb, hb, wb], out_specs=[hb, hb, wb],
        out_shape=[jax.ShapeDtypeStruct((s, fw), f32), jax.ShapeDtypeStruct((s, fw), bf16),
                   jax.ShapeDtypeStruct((1, HEAD_DIM), f32)],
        compiler_params=_cp("arbitrary", "arbitrary"))(don, o, z, wn)


def _gdn_bwd(qkvn, sm, do, states, tinvs, nh):
    s = qkvn.shape[0]
    c = GDN_CHUNK
    nc = s // c
    fw = nh * HEAD_DIM

    def body(qkvn_ref, sm_ref, do_ref, st_ref, ti_ref, dqkvn_ref, dsm_ref, dstate):
        ci = pl.program_id(0)

        @pl.when(ci == 0)
        def _():
            dstate[...] = jnp.zeros_like(dstate)

        qkvn_v = qkvn_ref[...]
        sm_v = sm_ref[...]
        incl, strict, eye = _chunk_consts()
        inclf = incl.astype(f32)
        gcs = _doth(inclf, sm_v)
        gcs_t = gcs.T
        lane = lax.broadcasted_iota(jnp.int32, (c, LANES), 1)
        last_row = lax.broadcasted_iota(jnp.int32, (c, 1), 0) == c - 1
        ones_cl = jnp.ones((c, LANES), f32)
        each = lambda f: [f(h) for h in range(nh)]
        es = each(lambda h: _chunk_head(qkvn_v, sm_v, gcs, gcs_t, h, nh))
        tinv = each(lambda h: ti_ref[0, h])
        st = each(lambda h: st_ref[0, h])
        dst = each(lambda h: dstate[h])
        do = each(lambda h: do_ref[:, h * HEAD_DIM:(h + 1) * HEAD_DIM])
        kg = each(lambda h: es[h]["kb"] * es[h]["eg"])
        qg = each(lambda h: es[h]["q"] * es[h]["eg"])
        kd = each(lambda h: es[h]["k"] * es[h]["ekd"])
        u = each(lambda h: _dotm(tinv[h], es[h]["vb"]))
        w = each(lambda h: _dotm(tinv[h], kg[h]))
        a = each(lambda h: jnp.where(incl, es[h]["qk"] * es[h]["decay"], 0.0))
        v_new = each(lambda h: u[h] - _dotb(w[h], st[h]))
        dv_new = each(lambda h: _dotb(a[h], do[h], TN) + _dotb(kd[h], dst[h]))
        da = each(lambda h: jnp.where(incl, _dotb(do[h], v_new[h], NT), 0.0))
        dqg = each(lambda h: _dotb(do[h], st[h], NT))
        dkd = each(lambda h: _dotb(v_new[h], dst[h], NT))
        dglast = each(lambda h: es[h]["egl"] * jnp.sum(jnp.sum(dst[h] * st[h], axis=1, keepdims=True), axis=0, keepdims=True))
        dw = each(lambda h: -_dotb(dv_new[h], st[h], NT))
        new_dst = each(lambda h: _dotb(qg[h], do[h], TN) + es[h]["egl"] * dst[h] - _dotb(w[h], dv_new[h], TN))
        dtinv = each(lambda h: _dotm(dv_new[h], es[h]["vb"], NT) + _dotm(dw[h], kg[h], NT))
        dvb = each(lambda h: _dotm(tinv[h], dv_new[h], TN))
        dkg = each(lambda h: _dotm(tinv[h], dw[h], TN))
        tdt = each(lambda h: _dotm(tinv[h], dtinv[h], TN))
        dlow = each(lambda h: -_dotm(tdt[h], tinv[h], NT))
        dkk = each(lambda h: jnp.where(strict, dlow[h] * es[h]["decay"], 0.0))
        dqk = each(lambda h: da[h] * es[h]["decay"])
        darg = each(lambda h: (jnp.where(strict, dlow[h] * es[h]["kk"], 0.0) + da[h] * es[h]["qk"]) * es[h]["decay"])
        dgc = each(lambda h: jnp.sum(darg[h], axis=1, keepdims=True) - _doth(darg[h], ones_cl, TN)[:, 0:1])
        dkb = each(lambda h: _dotb(dkk[h], es[h]["k"]) + dkg[h] * es[h]["eg"])
        dk = each(lambda h: _dotb(dkk[h], es[h]["kb"], TN) + _dotb(dqk[h], es[h]["q"], TN) + dkd[h] * es[h]["ekd"]
                  + dkb[h] * es[h]["beta"])
        dq = each(lambda h: (_dotb(dqk[h], es[h]["k"]) + dqg[h] * es[h]["eg"]) * (HEAD_DIM ** -0.5))
        s_kd = each(lambda h: jnp.sum(dkd[h] * kd[h], axis=1, keepdims=True))
        dgc = each(lambda h: dgc[h] + jnp.sum(dkg[h] * kg[h] + dqg[h] * qg[h], axis=1, keepdims=True) - s_kd[h]
                   + jnp.where(last_row, jnp.sum(s_kd[h], axis=0, keepdims=True) + dglast[h], 0.0))
        dg = each(lambda h: _doth(inclf, dgc[h] * ones_cl, TN)[:, 0:1])
        dsm = jnp.zeros((c, LANES), f32)
        for h in range(nh):
            dstate[h] = new_dst[h]
            dbeta = jnp.sum(dkb[h] * es[h]["k"] + dvb[h] * es[h]["v"], axis=1, keepdims=True)
            dqkvn_ref[:, h * HEAD_DIM:(h + 1) * HEAD_DIM] = dq[h]
            dqkvn_ref[:, fw + h * HEAD_DIM: fw + (h + 1) * HEAD_DIM] = dk[h]
            dqkvn_ref[:, 2 * fw + h * HEAD_DIM: 2 * fw + (h + 1) * HEAD_DIM] = dvb[h] * es[h]["beta"]
            dsm = dsm + jnp.where(lane == nh + h, dg[h], 0.0) + jnp.where(lane == 2 * nh + h, dbeta, 0.0)
        dsm_ref[...] = dsm

    rev = lambda i: (nc - 1 - i, 0)
    rev4 = lambda i: (nc - 1 - i, 0, 0, 0)
    return pl.pallas_call(
        body, name="gdn_bwd", grid=(nc,),
        in_specs=[pl.BlockSpec((c, 3 * fw), rev), pl.BlockSpec((c, LANES), rev), pl.BlockSpec((c, fw), rev),
                  pl.BlockSpec((1, nh, HEAD_DIM, HEAD_DIM), rev4), pl.BlockSpec((1, nh, c, c), rev4)],
        out_specs=[pl.BlockSpec((c, 3 * fw), rev), pl.BlockSpec((c, LANES), rev)],
        out_shape=[jax.ShapeDtypeStruct((s, 3 * fw), f32), jax.ShapeDtypeStruct((s, LANES), f32)],
        scratch_shapes=[pltpu.VMEM((nh, HEAD_DIM, HEAD_DIM), f32)],
        compiler_params=_cp("arbitrary"))(qkvn, sm, do, states, tinvs)


MM_TILES = (1024, 512, 2048)
MM_TILES_TN = (512, 1024, 4096)
MM_TILES_F_DEEP = (1024, 512, 2816)
MM_TILES_LONG_K = (1024, 512, 2560)


def _hosted(res, comm):
    return res if comm else (res, None)


def _ffn_fwd(tag, x, g, mod3, w, comm_up=None, comm_down=None, wd_of=None):
    wg_t, wu_t, wd = w
    sh, sc, gt = mod3
    h = _ada_in(tag + "_ada", x, g, sh, sc)
    (gate, up, act), got_up = _hosted(_mm(tag + "_up", [[(h, wg_t)], [(h, wu_t)]], "nt", MM_TILES, _ep_swiglu,
                                          (bf16, bf16, bf16), comm=comm_up), comm_up)
    if wd_of:
        wd = wd_of(got_up)
    (xn, y), got_down = _hosted(_mm(tag + "_down", [[(act, wd)]], "nn", MM_TILES_F_DEEP, _ep_residual, (f32, bf16),
                                    extras=((x, "mn"), (MACARON_W * gt, "n")), comm=comm_down), comm_down)
    return xn, dict(x=x, h=h, gate=gate, up=up, y=y), got_up, got_down


def _ffn_bwd(tag, dxn, dy, dgs, res, g, mod3, w, below=None, comm_dact=None, comm_dh_of=None):
    wg_t, wu_t, wd = w
    sh, sc, gt = mod3
    (act, dgate, dup), got = _hosted(_mm(tag + "_dact", [[(dy, wd)]], "nt", MM_TILES, _ep_swiglu_bwd, (bf16, bf16, bf16),
                                         extras=((res["gate"], "mn"), (res["up"], "mn")), comm=comm_dact), comm_dact)
    (dwd,) = _mm(tag + "_dwd", [[(act, dy)]], "tn", MM_TILES_TN, _ep_plain, (bf16,))
    (dwg_t,) = _mm(tag + "_dwg", [[(dgate, res["h"])]], "tn", MM_TILES_TN, _ep_plain, (bf16,))
    (dwu_t,) = _mm(tag + "_dwu", [[(dup, res["h"])]], "tn", MM_TILES_TN, _ep_plain, (bf16,))
    comm_dh = comm_dh_of and comm_dh_of((dwg_t, dwu_t, dwd))
    (dh,), got_dh = _hosted(_mm(tag + "_dh", [[(dgate, wg_t), (dup, wu_t)]], "nn", MM_TILES_F_DEEP, _ep_plain, (bf16,), comm=comm_dh),
                            comm_dh)
    if below:
        dx, dsh, a, dy_below, dgs_below = _ada_bwd(tag + "_ada_bwd", res["x"], g, sc, dh, dxn, *below)
    else:
        (dx, dsh, a), dy_below, dgs_below = _ada_bwd_first(tag + "_ada_bwd", res["x"], g, sc, dh, dxn), None, None
    return dx, (dy_below, dgs_below), (dwg_t, dwu_t, dwd), (dsh, a * g, MACARON_W * dgs), a * (1.0 + sc), got, got_dh


def _local_step(x, target, mods, norm_g, final_norm, ffn1_w, later_w, prm, fox_wn, gdn_wn, conv_w, nh, hooks=None):
    s, d = x.shape
    fw = nh * HEAD_DIM
    tq = _tile(s, min(256, s // 2))
    g_rows = [norm_g[i:i + 1] for i in range(3)]
    m1, m2, m3 = mods[0:3], mods[3:6], mods[6:9]

    x1, r1, got_up, got_down = _ffn_fwd("ffn1", x, g_rows[0], m1, ffn1_w, hooks and hooks.gather_mix_spec(),
                                        hooks and hooks.gather_ffn2_spec(), hooks and hooks.ffn1_wd)
    if hooks:
        ffn1_w = ffn1_w[:2] + (hooks.ffn1_wd(got_up),)
    w_cat_t, w_out, ffn2_w = hooks.gathered(got_up, got_down) if hooks else later_w
    h2 = _ada_in("mix_ada", x1, g_rows[1], m2[0], m2[1])
    w_fox, w_gdn, w_z, w_s = w_cat_t[:3 * fw], w_cat_t[3 * fw:6 * fw], w_cat_t[6 * fw:7 * fw], w_cat_t[7 * fw:]
    colscale = jnp.concatenate([jnp.full((1, fw), FOX_Q_SCALE, f32), jnp.ones((1, 2 * fw), f32)], axis=1)
    (qkv_f,) = _mm("proj_fox", [[(h2, w_fox)]], "nt", MM_TILES, _ep_colscale, (bf16,), extras=((colscale, "n"),))
    (qkv_g,) = _mm("proj_gdn", [[(h2, w_gdn)]], "nt", MM_TILES, _ep_plain, (f32,))
    (z,) = _mm("proj_z", [[(h2, w_z)]], "nt", MM_TILES, _ep_plain, (f32,))
    (ps,) = _mm("proj_s", [[(h2, w_s)]], "nt", MM_TILES, _ep_plain, (f32,))
    sm, cum = _small_fwd(ps, prm, nh)
    qa, ka, va = _fox_aug(qkv_f, cum, nh)
    o_f, on_f, lse2 = _fox_fwd(qa, ka, qkv_f, fox_wn, nh, tq)
    qkvn = _gdn_pre(qkv_g, conv_w, nh)
    on_g, o_g, states, tinvs = _gdn_fwd(qkvn, sm, z, gdn_wn, nh)
    w_top, w_bot = w_out[:fw], w_out[fw:]
    x2, mix = _mm("mix_out", [[(on_f, w_top), (on_g, w_bot)]], "nn", MM_TILES, _ep_residual, (f32, bf16),
                  extras=((x1, "mn"), (m2[2], "n")))
    x3, r3, got_up2, _ = _ffn_fwd("ffn2", x2, g_rows[2], m3, ffn2_w, hooks and hooks.gather_ffn2_wd_spec(), None,
                                  hooks and hooks.ffn2_wd)
    if hooks:
        ffn2_w = ffn2_w[:2] + (hooks.ffn2_wd(got_up2),)
    loss, dx3, dfinal, dy3, dgs3 = _final_loss(x3, final_norm, target, r3["y"], MACARON_W * m3[2])

    dx2, (dmix, dgt2), dffn2, dmod3, dg3, _, _ = _ffn_bwd("ffn2", dx3, dy3, dgs3, r3, g_rows[2], m3, ffn2_w, (mix, m2[2]))
    sib_ffn2 = hooks and hooks.rs_ffn2_sibling_spec(dffn2)
    (don_f,), got_sib = _hosted(_mm("mix_dof", [[(dmix, w_top)]], "nt", MM_TILES, _ep_plain, (f32,), comm=sib_ffn2), sib_ffn2)
    rs_ffn2 = hooks and hooks.rs_ffn2_spec(got_sib)
    (don_g,) = _mm("mix_dog", [[(dmix, w_bot)]], "nt", MM_TILES, _ep_plain, (f32,))
    (dw_top,) = _mm("mix_dwtop", [[(on_f, dmix)]], "tn", MM_TILES_TN, _ep_plain, (bf16,))
    (dw_bot,) = _mm("mix_dwbot", [[(on_g, dmix)]], "tn", MM_TILES_TN, _ep_plain, (bf16,))
    qb, doa, qb_t, do_t, dfox_wn = _fox_post_bwd(don_f, o_f, lse2, cum, qkv_f, fox_wn, nh)
    dqx, dkx, dv_f = _fox_bwd(qb, doa, qb_t, do_t, ka, va, nh, tq)
    dq_f, dk_f, dcum = _fox_unpack(dqx, dkx, nh)
    do_g, dz, dgdn_wn = _gdn_post_bwd(don_g, o_g, z, gdn_wn, nh)
    dqkvn, dsm = _gdn_bwd(qkvn, sm, do_g, states, tinvs, nh)
    dpre = _gdn_pre_bwd(qkv_g, conv_w, dqkvn, nh)
    dqkv_g, dconv = _conv_bwd(qkv_g, conv_w, dpre, nh)
    dps, pg = _small_bwd(ps, prm, dsm, dcum, nh)
    dproj = jnp.concatenate([dq_f, dk_f, dv_f, dqkv_g, dz, dps], axis=1)
    ((dw_cat_t,), got_ffn2) = _hosted(_mm("proj_dw", [[(dproj, h2)]], "tn", MM_TILES_TN, _ep_plain, (bf16,), comm=rs_ffn2), rs_ffn2)
    dw_out = jnp.concatenate([dw_top, dw_bot], axis=0)
    sib_mix = hooks and hooks.rs_mix_sibling_spec(dw_cat_t, dw_out)
    (dh2,), got_sib = _hosted(_mm("proj_dh", [[(dproj, w_cat_t)]], "nn", MM_TILES_LONG_K, _ep_plain, (bf16,), comm=sib_mix), sib_mix)
    rs_mix = hooks and hooks.rs_mix_spec(got_sib)
    dx1, dsh2, a2, dy1, dgs1 = _ada_bwd("mix_ada_bwd", x1, g_rows[1], m2[1], dh2, dx2, r1["y"], MACARON_W * m1[2])
    dmod2 = (dsh2, a2 * g_rows[1], dgt2)
    dg2 = a2 * (1.0 + m2[1])
    dx0, _, dffn1, dmod1, dg1, got_mix, got_ffn1 = _ffn_bwd("ffn1", dx1, dy1, dgs1, r1, g_rows[0], m1, ffn1_w, None, rs_mix,
                                                             hooks and hooks.rs_ffn1_spec)

    big = dict(ffn=(dffn1, dffn2), w_cat_t=dw_cat_t, w_out=dw_out, got_ffn2=got_ffn2, got_mix=got_mix, got_ffn1=got_ffn1)
    small = dict(loss=loss, norm_g=jnp.concatenate([dg1, dg2, dg3], axis=0), final_norm=dfinal, fox_wn=dfox_wn,
                 gdn_wn=dgdn_wn, pg=pg, conv=dconv, mod=jnp.concatenate(list(dmod1) + list(dmod2) + list(dmod3), axis=1))
    return dx0, big, small


def _w_in_row_groups(nh):
    fw = nh * HEAD_DIM
    sizes = [3 * fw, nh, 3 * fw, nh, nh, fw]
    offs = [0]
    for sz in sizes:
        offs.append(offs[-1] + sz)
    return [(offs[i], offs[i + 1]) for i in range(len(sizes))]


def _build_w_cat_t(w_in_t, nh):
    gq, gf, gg, ga, gb, gz = _w_in_row_groups(nh)
    d = w_in_t.shape[1]
    rows = lambda r: w_in_t[r[0]:r[1]]
    pad = jnp.zeros((LANES - 3 * nh, d), w_in_t.dtype)
    return jnp.concatenate([rows(gq), rows(gg), rows(gz), rows(gf), rows(ga), rows(gb), pad], axis=0)


def _split_dw_cat_t(dw_cat_t, nh):
    fw = nh * HEAD_DIM
    o = 7 * fw
    return jnp.concatenate([dw_cat_t[:3 * fw], dw_cat_t[o:o + nh], dw_cat_t[3 * fw:6 * fw], dw_cat_t[o + nh:o + 2 * nh],
                            dw_cat_t[o + 2 * nh:o + 3 * nh], dw_cat_t[6 * fw:7 * fw]], axis=0)


def _head_params(fox_f_bias, gdn_dt_bias, gdn_a_log, nh):
    z = jnp.zeros((8, LANES), f32)
    z = z.at[0, 0:nh].set(fox_f_bias.reshape(nh))
    z = z.at[1, nh:2 * nh].set(gdn_dt_bias.reshape(nh))
    z = z.at[2, nh:2 * nh].set(gdn_a_log.reshape(nh))
    return z


ANY = pl.BlockSpec(memory_space=pl.ANY)
IN_VMEM = pl.BlockSpec(memory_space=pltpu.VMEM)
N_OTHER_CHIPS = 3


def _place():
    x, y, c = lax.axis_index("x"), lax.axis_index("y"), lax.axis_index("c")
    chips = [(1 - x, y), (x, 1 - y), (1 - x, 1 - y)]
    return x, y, c, chips


def _allgather8(name, v):
    r, n = v.shape

    def body(v_ref, out_ref, send_sems, recv_sems, local_sem):
        x, y, c, _ = _place()
        me = 4 * x + 2 * y + c
        mine = pltpu.make_async_copy(v_ref, out_ref.at[me], local_sem)
        mine.start()
        copies = []
        for k in range(1, 8):
            fx, fy, fc = (k >> 2) & 1, (k >> 1) & 1, k & 1
            peer = (x + fx - 2 * x * fx, y + fy - 2 * y * fy, c + fc - 2 * c * fc)
            cp = pltpu.make_async_remote_copy(src_ref=v_ref, dst_ref=out_ref.at[me], send_sem=send_sems.at[k - 1],
                                              recv_sem=recv_sems.at[k - 1], device_id=peer, device_id_type=MESH)
            cp.start()
            copies.append(cp)
        for cp in copies:
            cp.wait()
        mine.wait()

    return pl.pallas_call(
        body, name=name, in_specs=[IN_VMEM], out_specs=IN_VMEM, out_shape=jax.ShapeDtypeStruct((8, r, n), v.dtype),
        scratch_shapes=[pltpu.SemaphoreType.DMA((7,)), pltpu.SemaphoreType.DMA((7,)), pltpu.SemaphoreType.DMA],
        compiler_params=pltpu.CompilerParams(vmem_limit_bytes=VMEM_LIMIT_V7X))(v)


class _CommSpec:
    def __init__(self, ins, out_shapes, sem_counts, run):
        self.ins, self.out_shapes, self.sem_counts, self.run = list(ins), list(out_shapes), sem_counts, run

    def sems(self):
        return [pltpu.SemaphoreType.DMA((n,)) for n in self.sem_counts]


def _run_comm(name, spec):
    ni, no = len(spec.ins), len(spec.out_shapes)

    def body(*refs):
        ins, outs, sems = refs[:ni], refs[ni:ni + no], refs[ni + no:]
        spec.run("start", ins, outs, sems)
        spec.run("finish", ins, outs, sems)

    return pl.pallas_call(body, name=name, in_specs=[ANY] * ni, out_specs=[ANY] * no, out_shape=spec.out_shapes,
                          scratch_shapes=spec.sems())(*spec.ins)


def _gather_spec(halves):
    nw = len(halves)

    def run(phase, ins, outs, sems):
        ici_send, ici_recv, d2d_send, d2d_recv = sems
        x, y, c, chips = _place()
        s = 2 * x + y
        sib = (x, y, 1 - c)

        def over_ici(w, j, dst):
            cx, cy = chips[j]
            return pltpu.make_async_remote_copy(src_ref=ins[w].at[c], dst_ref=dst, send_sem=ici_send.at[w * 3 + j],
                                                recv_sem=ici_recv.at[w * 3 + j], device_id=(cx, cy, c), device_id_type=MESH)

        def to_sibling(w, j, blk):
            return pltpu.make_async_remote_copy(src_ref=blk, dst_ref=blk, send_sem=d2d_send.at[w * 3 + j],
                                                recv_sem=d2d_recv.at[w * 3 + j], device_id=sib, device_id_type=MESH)

        pairs = [(w, j) for w in range(nw) for j in range(N_OTHER_CHIPS)]
        chip_of = lambda j: 2 * chips[j][0] + chips[j][1]
        if phase == "start":
            for w, j in pairs:
                over_ici(w, j, outs[w].at[c, s]).start()
            return
        for w, j in pairs:
            landed = outs[w].at[c, chip_of(j)]
            over_ici(w, j, landed).wait_recv()
            to_sibling(w, j, landed).start()
        for w, j in pairs:
            to_sibling(w, j, outs[w].at[1 - c, chip_of(j)]).wait_recv()
        for w, j in pairs:
            over_ici(w, j, outs[w].at[c, s]).wait_send()
            to_sibling(w, j, outs[w].at[c, chip_of(j)]).wait_send()

    n3 = nw * N_OTHER_CHIPS
    return _CommSpec(halves, [jax.ShapeDtypeStruct((2, 4) + h.shape[1:], h.dtype) for h in halves], [n3] * 4, run)


def _to_chips_spec(partials):
    nw = len(partials)

    def run(phase, ins, outs, sems):
        send_sems, recv_sems = sems
        x, y, c, chips = _place()
        for w in range(nw):
            for j, (cx, cy) in enumerate(chips):
                cp = pltpu.make_async_remote_copy(src_ref=ins[w].at[2 * cx + cy], dst_ref=outs[w].at[j],
                                                  send_sem=send_sems.at[w * 3 + j], recv_sem=recv_sems.at[w * 3 + j],
                                                  device_id=(cx, cy, c), device_id_type=MESH)
                if phase == "start":
                    cp.start()
                else:
                    cp.wait()

    n3 = nw * N_OTHER_CHIPS
    return _CommSpec(partials, [jax.ShapeDtypeStruct((3,) + a.shape[1:], a.dtype) for a in partials], [n3, n3], run)


def _send_to_sibling(name, srcs, other_half):
    return _run_comm(name, _to_sibling_spec(srcs, other_half))


def _to_sibling_spec(srcs, other_half):
    nw = len(srcs)

    def run(phase, ins, outs, sems):
        send_sems, recv_sems = sems
        x, y, c, _ = _place()
        for w in range(nw):
            cp = pltpu.make_async_remote_copy(src_ref=ins[w].at[1 - c] if other_half else ins[w], dst_ref=outs[w],
                                              send_sem=send_sems.at[w], recv_sem=recv_sems.at[w],
                                              device_id=(x, y, 1 - c), device_id_type=MESH)
            if phase == "start":
                cp.start()
            else:
                cp.wait()

    return _CommSpec(srcs, [jax.ShapeDtypeStruct(a.shape[1:] if other_half else a.shape, a.dtype) for a in srcs], [nw, nw], run)


def _add_pair(name, g, recv, c):
    _, nchip, r, d = g.shape
    tr = _tile(r, 512, 16)

    def body(c_ref, g_ref, r_ref, o_ref):
        o_ref[...] = (g_ref[...].astype(f32) + r_ref[...].astype(f32)).astype(o_ref.dtype)

    gs = pltpu.PrefetchScalarGridSpec(
        num_scalar_prefetch=1, grid=(nchip, r // tr),
        in_specs=[pl.BlockSpec((None, None, tr, d), lambda t, i, cr: (cr[0], t, i, 0)),
                  pl.BlockSpec((None, tr, d), lambda t, i, cr: (t, i, 0))],
        out_specs=pl.BlockSpec((None, tr, d), lambda t, i, cr: (t, i, 0)))
    return pl.pallas_call(body, name=name, grid_spec=gs, out_shape=jax.ShapeDtypeStruct((nchip, r, d), bf16),
                          compiler_params=_cp("parallel", "parallel"))(c.reshape(1).astype(jnp.int32), g, recv)


def _add_chips(name, p, recv, s_chip):
    _, r, d = p.shape
    tr = _tile(r, 512, 16)

    def body(s_ref, p_ref, r_ref, o_ref):
        o_ref[...] = ((p_ref[...].astype(f32) + r_ref[0].astype(f32)) + r_ref[1].astype(f32)) + r_ref[2].astype(f32)

    gs = pltpu.PrefetchScalarGridSpec(
        num_scalar_prefetch=1, grid=(r // tr,),
        in_specs=[pl.BlockSpec((None, tr, d), lambda i, sr: (sr[0], i, 0)),
                  pl.BlockSpec((3, tr, d), lambda i, sr: (0, i, 0))],
        out_specs=pl.BlockSpec((tr, d), lambda i, sr: (i, 0)))
    return pl.pallas_call(body, name=name, grid_spec=gs, out_shape=jax.ShapeDtypeStruct((r, d), f32),
                          compiler_params=_cp("parallel"))(s_chip.reshape(1).astype(jnp.int32), p, recv)


def _rs_pair_sums(tag, grads, c, from_sib=None):
    if from_sib is None:
        from_sib = _send_to_sibling("rs_to_sibling_" + tag, grads, True)
    return [_add_pair("rs_add_pair_%s%d" % (tag, n), g, r, c) for n, (g, r) in enumerate(zip(grads, from_sib))]


def _rs_chip_sums(tag, partial, from_chips, s_chip):
    return [_add_chips("rs_add_chips_%s%d" % (tag, n), p, r, s_chip) for n, (p, r) in enumerate(zip(partial, from_chips))]


def _rs_both_halves(mine, c):
    theirs = _send_to_sibling("rs_exchange_halves", mine, False)
    return [jnp.where(c == 0, jnp.stack([a, b]), jnp.stack([b, a])) for a, b in zip(mine, theirs)]


def _sum_devices(v):
    n = v.shape[2]

    def body(v_ref, o_ref):
        t = v_ref[0]
        for k in range(1, 8):
            t = t + v_ref[k]
        o_ref[...] = t

    return pl.pallas_call(body, name="sum_devices", out_shape=jax.ShapeDtypeStruct((1, n), f32))(v)


def _silu_rows(v):
    def body(v_ref, o_ref):
        t = v_ref[...]
        o_ref[...] = t * _sigmoid(t)

    return pl.pallas_call(body, name="silu_cond", out_shape=jax.ShapeDtypeStruct(v.shape, f32))(v)


ADAMW_BLOCK_ELEMS = 600 * 1024


def _adamw(name, w, g, m, v):
    r, cdim = w.shape
    tr = _tile(r, max(8, min(256, (ADAMW_BLOCK_ELEMS // cdim) // 8 * 8)), 8)
    c1 = 1.0 - ADAM_B1 ** ADAM_STEP
    c2 = 1.0 - ADAM_B2 ** ADAM_STEP

    def body(w_ref, g_ref, m_ref, v_ref, d_ref, mo_ref, vo_ref):
        gv = g_ref[...]
        mn = ADAM_B1 * m_ref[...] + (1.0 - ADAM_B1) * gv
        vn = ADAM_B2 * v_ref[...] + (1.0 - ADAM_B2) * (gv * gv)
        d_ref[...] = -ADAM_LR * ((mn / c1) / (jnp.sqrt(vn / c2) + ADAM_EPS) + ADAM_WD * w_ref[...])
        mo_ref[...] = mn
        vo_ref[...] = vn

    blk = pl.BlockSpec((tr, cdim), lambda i: (i, 0))
    return pl.pallas_call(body, name=name, grid=(r // tr,), in_specs=[blk] * 4, out_specs=[blk] * 3,
                          out_shape=[jax.ShapeDtypeStruct((r, cdim), f32)] * 3, compiler_params=_cp("parallel"))(w, g, m, v)


def _ep_bias(accs, ex):
    return (accs[0] + ex[0],)


def kernel(x, c, ada_w, ada_b, norm_g, ffn_w_gate, ffn_w_up, ffn_w_down, w_in, w_out, fox_f_bias, fox_out_norm, gdn_conv, gdn_A_log, gdn_dt_bias, gdn_out_norm, final_norm, loss_target, m_ada_w, m_ada_b, m_norm_g, m_ffn_w_gate, m_ffn_w_up, m_ffn_w_down, m_w_in, m_w_out, m_fox_f_bias, m_fox_out_norm, m_gdn_conv, m_gdn_A_log, m_gdn_dt_bias, m_gdn_out_norm, m_final_norm, v_ada_w, v_ada_b, v_norm_g, v_ffn_w_gate, v_ffn_w_up, v_ffn_w_down, v_w_in, v_w_out, v_fox_f_bias, v_fox_out_norm, v_gdn_conv, v_gdn_A_log, v_gdn_dt_bias, v_gdn_out_norm, v_final_norm):
    ix, iy, ic = lax.axis_index("x"), lax.axis_index("y"), lax.axis_index("c")
    s_chip = 2 * ix + iy
    me = 4 * ix + 2 * iy + ic
    _, s, d = x.shape
    nh = d // (2 * HEAD_DIM)
    fw = nh * HEAD_DIM
    ncol = ada_w.shape[2]
    dg_sh = norm_g.shape[2]
    cv_sh = gdn_conv.shape[2]
    ff_sh = ffn_w_gate.shape[3]
    in_sh = w_in.shape[2]
    in_pad = -(-in_sh // 32) * 32
    out_sh = w_out.shape[1]
    per_chip = lambda a, t: a[2 * t]

    pack0 = jnp.concatenate([_silu_rows(c), norm_g[0].reshape(1, 3 * dg_sh), gdn_conv[0].reshape(1, CONV_W * cv_sh)], axis=1)
    got0 = _allgather8("gather_cond", pack0)
    cond_all = got0[:, 0, :d]
    norm_g_full = jnp.concatenate([per_chip(got0, t)[0, d:d + 3 * dg_sh].reshape(3, dg_sh) for t in range(4)], axis=1)
    conv_full = jnp.concatenate([per_chip(got0, t)[0, d + 3 * dg_sh:].reshape(CONV_W, cv_sh) for t in range(4)], axis=1)

    ada_b_sh = lax.dynamic_slice_in_dim(ada_b, s_chip * ncol, ncol, axis=1)
    (mod_sh,) = _mm("ada_mod", [[(cond_all, ada_w[0])]], "nn", (8, 512, 2048), _ep_bias, (f32,), extras=((ada_b_sh, "n"),))
    mod_all = _allgather8("gather_mod", mod_sh)
    mod = jnp.concatenate([lax.dynamic_index_in_dim(per_chip(mod_all, t), me, axis=0, keepdims=True) for t in range(4)], axis=1)
    mods = [mod[:, i * d:(i + 1) * d] for i in range(9)]

    halved = lambda a: a.reshape(2, a.shape[0] // 2, d)
    ffn_halves = [[halved(ffn_w_gate[0, j].T.astype(bf16)), halved(ffn_w_up[0, j].T.astype(bf16)),
                   halved(ffn_w_down[0, j].astype(bf16))] for j in range(2)]
    mix_halves = [halved(jnp.pad(w_in[0].T.astype(bf16), ((0, in_pad - in_sh), (0, 0)))), halved(w_out[0].astype(bf16))]
    with_own = lambda got, hs: [lax.dynamic_update_slice(g, h[:, None], (0, s_chip, 0, 0)) for g, h in zip(got, hs)]
    ffn_full = lambda got, hs: tuple(g.reshape(4 * ff_sh, d) for g in with_own(got, hs))
    ffn_blocks = lambda grads: [g.reshape(2, 4, ff_sh // 2, d) for g in grads]
    ffn1_w = ffn_full(_run_comm("gather_ffn1", _gather_spec(ffn_halves[0][:2])), ffn_halves[0][:2]) + (None,)
    prm = _head_params(fox_f_bias, gdn_dt_bias, gdn_A_log, nh)

    class Hooks:
        def gather_mix_spec(self):
            return _gather_spec(mix_halves + ffn_halves[0][2:])

        def ffn1_wd(self, got):
            return ffn_full(got[2:], ffn_halves[0][2:])[0]

        def gather_ffn2_spec(self):
            return _gather_spec(ffn_halves[1][:2])

        def gather_ffn2_wd_spec(self):
            return _gather_spec(ffn_halves[1][2:])

        def ffn2_wd(self, got):
            return ffn_full(got, ffn_halves[1][2:])[0]

        def gathered(self, got_mix, got_ffn2):
            g_win, g_wo = with_own(got_mix[:2], mix_halves)
            w_in_t = jnp.swapaxes(g_win, 0, 1).reshape(4, in_pad, d)[:, :in_sh].reshape(4 * in_sh, d)
            return (_build_w_cat_t(w_in_t, nh), jnp.swapaxes(g_wo, 0, 1).reshape(4 * out_sh, d),
                    ffn_full(got_ffn2, ffn_halves[1][:2]) + (None,))

        def rs_ffn2_sibling_spec(self, dffn2):
            self.ffn2_blocks = ffn_blocks(dffn2)
            return _to_sibling_spec(self.ffn2_blocks, True)

        def rs_ffn2_spec(self, from_sib):
            self.ffn2_pairs = _rs_pair_sums("ffn2", self.ffn2_blocks, ic, from_sib)
            return _to_chips_spec(self.ffn2_pairs)

        def rs_ffn1_spec(self, dffn1):
            self.ffn1_pairs = _rs_pair_sums("ffn1", ffn_blocks(dffn1), ic)
            return _to_chips_spec(self.ffn1_pairs)

        def rs_mix_sibling_spec(self, dw_cat_t, dw_out):
            dw_in_t = jnp.pad(_split_dw_cat_t(dw_cat_t, nh).reshape(4, in_sh, d), ((0, 0), (0, in_pad - in_sh), (0, 0)))
            self.mix_blocks = [jnp.swapaxes(dw_in_t.reshape(4, 2, in_pad // 2, d), 0, 1),
                               jnp.swapaxes(dw_out.reshape(4, 2, out_sh // 2, d), 0, 1)]
            return _to_sibling_spec(self.mix_blocks, True)

        def rs_mix_spec(self, from_sib):
            self.mix_pairs = _rs_pair_sums("mix", self.mix_blocks, ic, from_sib)
            return _to_chips_spec(self.mix_pairs)

    hooks = Hooks()

    dx0, big, small = _local_step(x[0], loss_target[0], mods, norm_g_full, final_norm.reshape(1, d), ffn1_w, None, prm,
                                  fox_out_norm, gdn_out_norm, conv_full, nh, hooks)

    pack1 = jnp.concatenate([small["loss"], small["norm_g"].reshape(1, 3 * d), small["final_norm"], small["fox_wn"],
                             small["gdn_wn"], small["pg"][0:1], small["pg"][1:2], small["conv"].reshape(1, CONV_W * 3 * fw),
                             small["mod"]], axis=1)
    got1 = _allgather8("gather_small_grads", pack1)
    tot = _sum_devices(got1)
    o = [0]

    def take(n):
        o[0] += n
        return tot[:, o[0] - n:o[0]]

    loss = take(LANES)[0, 0]
    g_norm_g = lax.dynamic_slice_in_dim(take(3 * d).reshape(3, d), s_chip * dg_sh, dg_sh, axis=1)[None]
    g_final = take(d).reshape(d)
    g_fox_wn = take(HEAD_DIM)
    g_gdn_wn = take(HEAD_DIM)
    pg0, pg1 = take(LANES), take(LANES)
    g_fbias, g_dtb, g_alog = pg0[:, 0:nh], pg0[:, nh:2 * nh], pg1[:, nh:2 * nh]
    g_conv = lax.dynamic_slice_in_dim(take(CONV_W * 3 * fw).reshape(CONV_W, 3 * fw), s_chip * cv_sh, cv_sh, axis=1)[None]
    g_ada_b = take(9 * d)
    dmod_all = got1[:, 0, o[0] - 9 * d:o[0]]
    dmod_sh = lax.dynamic_slice_in_dim(dmod_all, s_chip * ncol, ncol, axis=1)
    (g_ada_w,) = _mm("ada_dw", [[(cond_all, dmod_sh)]], "tn", (2048, 512, 8), _ep_plain, (f32,))

    ffn1_mine = _rs_chip_sums("ffn1", hooks.ffn1_pairs, big["got_ffn1"], s_chip)
    ffn2_mine = _rs_chip_sums("ffn2", hooks.ffn2_pairs, big["got_ffn2"], s_chip)
    mix_mine = _rs_chip_sums("mix", hooks.mix_pairs, big["got_mix"], s_chip)
    r1g, r1u, r1d, r2g, r2u, r2d, r_win, r_wo = _rs_both_halves(ffn1_mine + ffn2_mine + mix_mine, ic)
    rows = lambda r: r.reshape(-1, d)
    g_ffn_gate = jnp.stack([rows(r1g).T, rows(r2g).T])[None]
    g_ffn_up = jnp.stack([rows(r1u).T, rows(r2u).T])[None]
    g_ffn_down = jnp.stack([rows(r1d), rows(r2d)])[None]
    g_w_in = rows(r_win)[:in_sh].T[None]
    g_w_out = rows(r_wo)[None]

    def upd(name, w, g, m, v):
        shp = w.shape
        two = lambda a: a.reshape(-1, shp[-1])
        return tuple(t.reshape(shp) for t in _adamw(name, two(w), two(g), two(m), two(v)))

    big_upd = [upd("adamw_ada_w", ada_w, g_ada_w[None], m_ada_w, v_ada_w),
               upd("adamw_ffn_gate", ffn_w_gate, g_ffn_gate, m_ffn_w_gate, v_ffn_w_gate),
               upd("adamw_ffn_up", ffn_w_up, g_ffn_up, m_ffn_w_up, v_ffn_w_up),
               upd("adamw_ffn_down", ffn_w_down, g_ffn_down, m_ffn_w_down, v_ffn_w_down),
               upd("adamw_w_in", w_in, g_w_in, m_w_in, v_w_in),
               upd("adamw_w_out", w_out, g_w_out, m_w_out, v_w_out)]
    small_w = [ada_b, norm_g, fox_f_bias, fox_out_norm, gdn_conv, gdn_A_log, gdn_dt_bias, gdn_out_norm, final_norm]
    small_g = [g_ada_b, g_norm_g, g_fbias, g_fox_wn, g_conv, g_alog, g_dtb, g_gdn_wn, g_final]
    small_m = [m_ada_b, m_norm_g, m_fox_f_bias, m_fox_out_norm, m_gdn_conv, m_gdn_A_log, m_gdn_dt_bias, m_gdn_out_norm, m_final_norm]
    small_v = [v_ada_b, v_norm_g, v_fox_f_bias, v_fox_out_norm, v_gdn_conv, v_gdn_A_log, v_gdn_dt_bias, v_gdn_out_norm, v_final_norm]
    sizes = [a.size for a in small_w]
    npad = -sum(sizes) % LANES
    flat = lambda arrs, fill: jnp.concatenate([a.reshape(1, -1) for a in arrs] + [jnp.full((1, npad), fill, f32)], axis=1)
    sd, sm_, sv = _adamw("adamw_small", flat(small_w, 0.0), flat(small_g, 0.0), flat(small_m, 0.0), flat(small_v, 1.0))

    def unflat(t):
        out, off = [], 0
        for a, n in zip(small_w, sizes):
            out.append(t[0, off:off + n].reshape(a.shape))
            off += n
        return out

    small_g = [g.reshape(a.shape) for g, a in zip(small_g, small_w)]
    s_d, s_m, s_v = unflat(sd), unflat(sm_), unflat(sv)
    def order(bigs, smalls):
        return [bigs[0], smalls[0], smalls[1], bigs[1], bigs[2], bigs[3], bigs[4], bigs[5]] + list(smalls[2:])

    grads_out = order([g_ada_w[None], g_ffn_gate, g_ffn_up, g_ffn_down, g_w_in, g_w_out], small_g)
    deltas = order([u[0] for u in big_upd], s_d)
    new_m = order([u[1] for u in big_upd], s_m)
    new_v = order([u[2] for u in big_upd], s_v)
    return (loss, dx0[None], *grads_out, *deltas, *new_m, *new_v)
```
